```python
import math
import jax, jax.numpy as jnp
from jax import lax
import numpy as np

D_MODEL = 1024
BATCH = 8
SEQ = 16384
DEPTH = 2

N_A = DEPTH // 2
N_B = DEPTH - N_A
CONV_WIDTH = 31
D_FF = ((8 * D_MODEL + 3 * 256 - 1) // (3 * 256)) * 256
HEAD_DIM = 64
N_Q_HEADS = D_MODEL // HEAD_DIM
N_KV_HEADS = 2
GROUP = N_Q_HEADS // N_KV_HEADS
WINDOW = 128
BLOCK = 128
ALIBI_MAX = 8.0
ALPHA = (2.0 * DEPTH) ** 0.25
BETA = (8.0 * DEPTH) ** -0.25
LN_EPS = 1e-5
NEG_INF = -1e30

kernel_name = "yoco_conformer_swa_sink_alibi_deepnorm"


def layer_norm(x, g, b):
    xf = x.astype(jnp.float32)
    mu = jnp.mean(xf, axis=-1, keepdims=True)
    var = jnp.mean(jnp.square(xf - mu), axis=-1, keepdims=True)
    y = (xf - mu) * lax.rsqrt(var + LN_EPS)
    return (y * g.astype(jnp.float32) + b.astype(jnp.float32)).astype(x.dtype)


def conformer_conv(x, w_pw1, b_pw1, w_dw, b_dw, ln_g, ln_b, w_pw2, b_pw2):
    h = x @ w_pw1 + b_pw1
    h = h[..., :D_MODEL] * jax.nn.sigmoid(h[..., D_MODEL:])
    h = lax.conv_general_dilated(
        h, w_dw[:, None, :].astype(h.dtype), window_strides=(1,),
        padding=[(CONV_WIDTH - 1, 0)],
        dimension_numbers=("NWC", "WIO", "NWC"),
        feature_group_count=D_MODEL) + b_dw
    h = jax.nn.silu(layer_norm(h, ln_g, ln_b))
    return h @ w_pw2 + b_pw2


def swiglu(x, w_gate, w_up, w_down):
    return (jax.nn.silu(x @ w_gate) * (x @ w_up)) @ w_down


def banded_blocks(t):
    b, s = t.shape[0], t.shape[1]
    nb = s // BLOCK
    pad = jnp.zeros((b, BLOCK) + t.shape[2:], t.dtype)
    prev = jnp.concatenate([pad, t[:, :s - BLOCK]], axis=1).reshape(b, nb, BLOCK, *t.shape[2:])
    cur = t.reshape(b, nb, BLOCK, *t.shape[2:])
    return jnp.concatenate([prev, cur], axis=2)


def shared_kv(h, w_k, b_k, w_v, b_v):
    b, s, _ = h.shape
    k = (h @ w_k + b_k).reshape(b, s, N_KV_HEADS, HEAD_DIM)
    v = (h @ w_v + b_v).reshape(b, s, N_KV_HEADS, HEAD_DIM)
    return banded_blocks(k), banded_blocks(v)


def window_attention(x, k_blk, v_blk, w_q, b_q, sinks, w_o, b_o):
    b, s, _ = x.shape
    nb = s // BLOCK
    q = (x @ w_q + b_q).reshape(b, nb, BLOCK, N_KV_HEADS, GROUP, HEAD_DIM)
    scores = jnp.einsum("bnikgd,bnjkd->bnkgij", q, k_blk).astype(jnp.float32)
    scores = scores * (1.0 / math.sqrt(HEAD_DIM))
    qi = jnp.arange(BLOCK)[:, None]
    kj = jnp.arange(2 * BLOCK)[None, :]
    delta = qi + BLOCK - kj
    key_pos = jnp.arange(nb)[:, None, None] * BLOCK - BLOCK + kj[None]
    valid = (delta >= 0) & (delta < WINDOW) & (key_pos >= 0)
    slopes = jnp.exp2(-ALIBI_MAX * jnp.arange(1, N_Q_HEADS + 1, dtype=jnp.float32) / N_Q_HEADS)
    slopes = slopes.reshape(N_KV_HEADS, GROUP)
    scores = scores - slopes[None, None, :, :, None, None] * delta.astype(jnp.float32)[None, None, None, None]
    scores = jnp.where(valid[None, :, None, None], scores, NEG_INF)
    sink = jnp.broadcast_to(
        sinks.astype(jnp.float32).reshape(N_KV_HEADS, GROUP)[None, None, :, :, None, None],
        scores.shape[:-1] + (1,))
    probs = jax.nn.softmax(jnp.concatenate([scores, sink], axis=-1), axis=-1)[..., :-1]
    o = jnp.einsum("bnkgij,bnjkd->bnikgd", probs.astype(v_blk.dtype), v_blk)
    o = o.reshape(b, s, N_Q_HEADS * HEAD_DIM)
    return o @ w_o + b_o


def _fwd_setup_inputs(seed: int = 0) -> dict:
    key = jax.random.key(seed)
    ks = jax.random.split(key, 40)
    f32 = jnp.float32
    D, F, HD = D_MODEL, D_FF, N_Q_HEADS * HEAD_DIM
    KVD = N_KV_HEADS * HEAD_DIM

    def nrm(k, shape, scale):
        return jax.random.normal(k, shape, f32) * scale

    def gain(k, shape):
        return 1.0 + 0.05 * jax.random.normal(k, shape, f32)

    return {
        "x": nrm(ks[0], (BATCH, SEQ, D), 1.0),
        "conv_w_pw1": nrm(ks[1], (N_A, D, 2 * D), D ** -0.5),
        "conv_b_pw1": nrm(ks[2], (N_A, 2 * D), 0.02),
        "conv_w_dw": nrm(ks[3], (N_A, CONV_WIDTH, D), CONV_WIDTH ** -0.5),
        "conv_b_dw": nrm(ks[4], (N_A, D), 0.02),
        "conv_ln_g": gain(ks[5], (N_A, D)),
        "conv_ln_b": nrm(ks[6], (N_A, D), 0.02),
        "conv_w_pw2": nrm(ks[7], (N_A, D, D), BETA * D ** -0.5),
        "conv_b_pw2": nrm(ks[8], (N_A, D), 0.02),
        "kv_w_k": nrm(ks[9], (D, KVD), D ** -0.5),
        "kv_b_k": nrm(ks[10], (KVD,), 0.02),
        "kv_w_v": nrm(ks[11], (D, KVD), BETA * D ** -0.5),
        "kv_b_v": nrm(ks[12], (KVD,), 0.02),
        "attn_w_q": nrm(ks[13], (N_B, D, HD), D ** -0.5),
        "attn_b_q": nrm(ks[14], (N_B, HD), 0.02),
        "attn_sinks": nrm(ks[15], (N_B, N_Q_HEADS), 1.0),
        "attn_w_o": nrm(ks[16], (N_B, HD, D), BETA * HD ** -0.5),
        "attn_b_o": nrm(ks[17], (N_B, D), 0.02),
        "ffn_w_gate": nrm(ks[18], (DEPTH, D, F), D ** -0.5),
        "ffn_w_up": nrm(ks[19], (DEPTH, D, F), D ** -0.5),
        "ffn_w_down": nrm(ks[20], (DEPTH, F, D), BETA * F ** -0.5),
        "ln_mix_g": gain(ks[21], (DEPTH, D)),
        "ln_mix_b": nrm(ks[22], (DEPTH, D), 0.02),
        "ln_ffn_g": gain(ks[23], (DEPTH, D)),
        "ln_ffn_b": nrm(ks[24], (DEPTH, D), 0.02),
    }


def _fwd_reference(x, conv_w_pw1, conv_b_pw1, conv_w_dw, conv_b_dw, conv_ln_g, conv_ln_b,
              conv_w_pw2, conv_b_pw2, kv_w_k, kv_b_k, kv_w_v, kv_b_v,
              attn_w_q, attn_b_q, attn_sinks, attn_w_o, attn_b_o,
              ffn_w_gate, ffn_w_up, ffn_w_down,
              ln_mix_g, ln_mix_b, ln_ffn_g, ln_ffn_b):
    k_blk, v_blk = None, None
    for layer in range(DEPTH):
        if layer < N_A:
            a = layer
            m = conformer_conv(x, conv_w_pw1[a], conv_b_pw1[a], conv_w_dw[a], conv_b_dw[a],
                               conv_ln_g[a], conv_ln_b[a], conv_w_pw2[a], conv_b_pw2[a])
        else:
            l = layer - N_A
            m = window_attention(x, k_blk, v_blk, attn_w_q[l], attn_b_q[l], attn_sinks[l],
                                 attn_w_o[l], attn_b_o[l])
        x = layer_norm(ALPHA * x + m, ln_mix_g[layer], ln_mix_b[layer])
        f = swiglu(x, ffn_w_gate[layer], ffn_w_up[layer], ffn_w_down[layer])
        x = layer_norm(ALPHA * x + f, ln_ffn_g[layer], ln_ffn_b[layer])
        if layer == N_A - 1:
            k_blk, v_blk = shared_kv(x, kv_w_k, kv_b_k, kv_w_v, kv_b_v)
    return x


import jax as _jax
import jax.numpy as _jnp

TWIN_FORMAT = 'train_step'
FWD_PARAMS = ['x', 'conv_w_pw1', 'conv_b_pw1', 'conv_w_dw', 'conv_b_dw', 'conv_ln_g', 'conv_ln_b', 'conv_w_pw2', 'conv_b_pw2', 'kv_w_k', 'kv_b_k', 'kv_w_v', 'kv_b_v', 'attn_w_q', 'attn_b_q', 'attn_sinks', 'attn_w_o', 'attn_b_o', 'ffn_w_gate', 'ffn_w_up', 'ffn_w_down', 'ln_mix_g', 'ln_mix_b', 'ln_ffn_g', 'ln_ffn_b']
TWIN_WEIGHTS = ['conv_w_pw1', 'conv_b_pw1', 'conv_w_dw', 'conv_b_dw', 'conv_ln_g', 'conv_ln_b', 'conv_w_pw2', 'conv_b_pw2', 'kv_w_k', 'kv_b_k', 'kv_w_v', 'kv_b_v', 'attn_w_q', 'attn_b_q', 'attn_sinks', 'attn_w_o', 'attn_b_o', 'ffn_w_gate', 'ffn_w_up', 'ffn_w_down', 'ln_mix_g', 'ln_mix_b', 'ln_ffn_g', 'ln_ffn_b']
TWIN_DIFF_INPUT = 'x'
TWIN_INPUTS = ['x', 'conv_w_pw1', 'conv_b_pw1', 'conv_w_dw', 'conv_b_dw', 'conv_ln_g', 'conv_ln_b', 'conv_w_pw2', 'conv_b_pw2', 'kv_w_k', 'kv_b_k', 'kv_w_v', 'kv_b_v', 'attn_w_q', 'attn_b_q', 'attn_sinks', 'attn_w_o', 'attn_b_o', 'ffn_w_gate', 'ffn_w_up', 'ffn_w_down', 'ln_mix_g', 'ln_mix_b', 'ln_ffn_g', 'ln_ffn_b', 'loss_target', 'm_conv_w_pw1', 'm_conv_b_pw1', 'm_conv_w_dw', 'm_conv_b_dw', 'm_conv_ln_g', 'm_conv_ln_b', 'm_conv_w_pw2', 'm_conv_b_pw2', 'm_kv_w_k', 'm_kv_b_k', 'm_kv_w_v', 'm_kv_b_v', 'm_attn_w_q', 'm_attn_b_q', 'm_attn_sinks', 'm_attn_w_o', 'm_attn_b_o', 'm_ffn_w_gate', 'm_ffn_w_up', 'm_ffn_w_down', 'm_ln_mix_g', 'm_ln_mix_b', 'm_ln_ffn_g', 'm_ln_ffn_b', 'v_conv_w_pw1', 'v_conv_b_pw1', 'v_conv_w_dw', 'v_conv_b_dw', 'v_conv_ln_g', 'v_conv_ln_b', 'v_conv_w_pw2', 'v_conv_b_pw2', 'v_kv_w_k', 'v_kv_b_k', 'v_kv_w_v', 'v_kv_b_v', 'v_attn_w_q', 'v_attn_b_q', 'v_attn_sinks', 'v_attn_w_o', 'v_attn_b_o', 'v_ffn_w_gate', 'v_ffn_w_up', 'v_ffn_w_down', 'v_ln_mix_g', 'v_ln_mix_b', 'v_ln_ffn_g', 'v_ln_ffn_b']
TWIN_OUTPUTS = ['loss', 'grad_x', 'grad_conv_w_pw1', 'grad_conv_b_pw1', 'grad_conv_w_dw', 'grad_conv_b_dw', 'grad_conv_ln_g', 'grad_conv_ln_b', 'grad_conv_w_pw2', 'grad_conv_b_pw2', 'grad_kv_w_k', 'grad_kv_b_k', 'grad_kv_w_v', 'grad_kv_b_v', 'grad_attn_w_q', 'grad_attn_b_q', 'grad_attn_sinks', 'grad_attn_w_o', 'grad_attn_b_o', 'grad_ffn_w_gate', 'grad_ffn_w_up', 'grad_ffn_w_down', 'grad_ln_mix_g', 'grad_ln_mix_b', 'grad_ln_ffn_g', 'grad_ln_ffn_b', 'delta_conv_w_pw1', 'delta_conv_b_pw1', 'delta_conv_w_dw', 'delta_conv_b_dw', 'delta_conv_ln_g', 'delta_conv_ln_b', 'delta_conv_w_pw2', 'delta_conv_b_pw2', 'delta_kv_w_k', 'delta_kv_b_k', 'delta_kv_w_v', 'delta_kv_b_v', 'delta_attn_w_q', 'delta_attn_b_q', 'delta_attn_sinks', 'delta_attn_w_o', 'delta_attn_b_o', 'delta_ffn_w_gate', 'delta_ffn_w_up', 'delta_ffn_w_down', 'delta_ln_mix_g', 'delta_ln_mix_b', 'delta_ln_ffn_g', 'delta_ln_ffn_b', 'new_m_conv_w_pw1', 'new_m_conv_b_pw1', 'new_m_conv_w_dw', 'new_m_conv_b_dw', 'new_m_conv_ln_g', 'new_m_conv_ln_b', 'new_m_conv_w_pw2', 'new_m_conv_b_pw2', 'new_m_kv_w_k', 'new_m_kv_b_k', 'new_m_kv_w_v', 'new_m_kv_b_v', 'new_m_attn_w_q', 'new_m_attn_b_q', 'new_m_attn_sinks', 'new_m_attn_w_o', 'new_m_attn_b_o', 'new_m_ffn_w_gate', 'new_m_ffn_w_up', 'new_m_ffn_w_down', 'new_m_ln_mix_g', 'new_m_ln_mix_b', 'new_m_ln_ffn_g', 'new_m_ln_ffn_b', 'new_v_conv_w_pw1', 'new_v_conv_b_pw1', 'new_v_conv_w_dw', 'new_v_conv_b_dw', 'new_v_conv_ln_g', 'new_v_conv_ln_b', 'new_v_conv_w_pw2', 'new_v_conv_b_pw2', 'new_v_kv_w_k', 'new_v_kv_b_k', 'new_v_kv_w_v', 'new_v_kv_b_v', 'new_v_attn_w_q', 'new_v_attn_b_q', 'new_v_attn_sinks', 'new_v_attn_w_o', 'new_v_attn_b_o', 'new_v_ffn_w_gate', 'new_v_ffn_w_up', 'new_v_ffn_w_down', 'new_v_ln_mix_g', 'new_v_ln_mix_b', 'new_v_ln_ffn_g', 'new_v_ln_ffn_b']
TWIN_LEAF_KINDS = {'loss': 'loss', 'grad_x': 'grad_x', 'grad_conv_w_pw1': 'grad_w', 'grad_conv_b_pw1': 'grad_w', 'grad_conv_w_dw': 'grad_w', 'grad_conv_b_dw': 'grad_w', 'grad_conv_ln_g': 'grad_w', 'grad_conv_ln_b': 'grad_w', 'grad_conv_w_pw2': 'grad_w', 'grad_conv_b_pw2': 'grad_w', 'grad_kv_w_k': 'grad_w', 'grad_kv_b_k': 'grad_w', 'grad_kv_w_v': 'grad_w', 'grad_kv_b_v': 'grad_w', 'grad_attn_w_q': 'grad_w', 'grad_attn_b_q': 'grad_w', 'grad_attn_sinks': 'grad_w', 'grad_attn_w_o': 'grad_w', 'grad_attn_b_o': 'grad_w', 'grad_ffn_w_gate': 'grad_w', 'grad_ffn_w_up': 'grad_w', 'grad_ffn_w_down': 'grad_w', 'grad_ln_mix_g': 'grad_w', 'grad_ln_mix_b': 'grad_w', 'grad_ln_ffn_g': 'grad_w', 'grad_ln_ffn_b': 'grad_w', 'delta_conv_w_pw1': 'delta_w', 'delta_conv_b_pw1': 'delta_w', 'delta_conv_w_dw': 'delta_w', 'delta_conv_b_dw': 'delta_w', 'delta_conv_ln_g': 'delta_w', 'delta_conv_ln_b': 'delta_w', 'delta_conv_w_pw2': 'delta_w', 'delta_conv_b_pw2': 'delta_w', 'delta_kv_w_k': 'delta_w', 'delta_kv_b_k': 'delta_w', 'delta_kv_w_v': 'delta_w', 'delta_kv_b_v': 'delta_w', 'delta_attn_w_q': 'delta_w', 'delta_attn_b_q': 'delta_w', 'delta_attn_sinks': 'delta_w', 'delta_attn_w_o': 'delta_w', 'delta_attn_b_o': 'delta_w', 'delta_ffn_w_gate': 'delta_w', 'delta_ffn_w_up': 'delta_w', 'delta_ffn_w_down': 'delta_w', 'delta_ln_mix_g': 'delta_w', 'delta_ln_mix_b': 'delta_w', 'delta_ln_ffn_g': 'delta_w', 'delta_ln_ffn_b': 'delta_w', 'new_m_conv_w_pw1': 'new_m', 'new_m_conv_b_pw1': 'new_m', 'new_m_conv_w_dw': 'new_m', 'new_m_conv_b_dw': 'new_m', 'new_m_conv_ln_g': 'new_m', 'new_m_conv_ln_b': 'new_m', 'new_m_conv_w_pw2': 'new_m', 'new_m_conv_b_pw2': 'new_m', 'new_m_kv_w_k': 'new_m', 'new_m_kv_b_k': 'new_m', 'new_m_kv_w_v': 'new_m', 'new_m_kv_b_v': 'new_m', 'new_m_attn_w_q': 'new_m', 'new_m_attn_b_q': 'new_m', 'new_m_attn_sinks': 'new_m', 'new_m_attn_w_o': 'new_m', 'new_m_attn_b_o': 'new_m', 'new_m_ffn_w_gate': 'new_m', 'new_m_ffn_w_up': 'new_m', 'new_m_ffn_w_down': 'new_m', 'new_m_ln_mix_g': 'new_m', 'new_m_ln_mix_b': 'new_m', 'new_m_ln_ffn_g': 'new_m', 'new_m_ln_ffn_b': 'new_m', 'new_v_conv_w_pw1': 'new_v', 'new_v_conv_b_pw1': 'new_v', 'new_v_conv_w_dw': 'new_v', 'new_v_conv_b_dw': 'new_v', 'new_v_conv_ln_g': 'new_v', 'new_v_conv_ln_b': 'new_v', 'new_v_conv_w_pw2': 'new_v', 'new_v_conv_b_pw2': 'new_v', 'new_v_kv_w_k': 'new_v', 'new_v_kv_b_k': 'new_v', 'new_v_kv_w_v': 'new_v', 'new_v_kv_b_v': 'new_v', 'new_v_attn_w_q': 'new_v', 'new_v_attn_b_q': 'new_v', 'new_v_attn_sinks': 'new_v', 'new_v_attn_w_o': 'new_v', 'new_v_attn_b_o': 'new_v', 'new_v_ffn_w_gate': 'new_v', 'new_v_ffn_w_up': 'new_v', 'new_v_ffn_w_down': 'new_v', 'new_v_ln_mix_g': 'new_v', 'new_v_ln_mix_b': 'new_v', 'new_v_ln_ffn_g': 'new_v', 'new_v_ln_ffn_b': 'new_v'}


def _forward(args):
    return _fwd_reference(*[args[k] for k in FWD_PARAMS])


def _output_shape():
    def fwd():
        inp = _fwd_setup_inputs(0)
        return _fwd_reference(*[inp[k] for k in FWD_PARAMS])
    out = _jax.eval_shape(fwd)
    return out.shape, out.dtype

N_MICROBATCH = 1
ADAM_LR = 0.001
ADAM_B1 = 0.9
ADAM_B2 = 0.999
ADAM_EPS = 1e-08
ADAM_WD = 0.01
ADAM_STEP = 10
PER_EXAMPLE_BATCH_AXIS = {'x': 0, 'loss_target': 0}
SHARED_INPUTS = []
_WEIGHT_DTYPES = {'conv_w_pw1': _jnp.float32, 'conv_b_pw1': _jnp.float32, 'conv_w_dw': _jnp.float32, 'conv_b_dw': _jnp.float32, 'conv_ln_g': _jnp.float32, 'conv_ln_b': _jnp.float32, 'conv_w_pw2': _jnp.float32, 'conv_b_pw2': _jnp.float32, 'kv_w_k': _jnp.float32, 'kv_b_k': _jnp.float32, 'kv_w_v': _jnp.float32, 'kv_b_v': _jnp.float32, 'attn_w_q': _jnp.float32, 'attn_b_q': _jnp.float32, 'attn_sinks': _jnp.float32, 'attn_w_o': _jnp.float32, 'attn_b_o': _jnp.float32, 'ffn_w_gate': _jnp.float32, 'ffn_w_up': _jnp.float32, 'ffn_w_down': _jnp.float32, 'ln_mix_g': _jnp.float32, 'ln_mix_b': _jnp.float32, 'ln_ffn_g': _jnp.float32, 'ln_ffn_b': _jnp.float32}
MOMENT_SCALE = {'conv_w_pw1': 6.077350e-02, 'conv_b_pw1': 2.081988e-01, 'conv_w_dw': 8.468473e-02, 'conv_b_dw': 4.576157e-01, 'conv_ln_g': 1.901602e-01, 'conv_ln_b': 3.030074e-01, 'conv_w_pw2': 2.364325e-01, 'conv_b_pw2': 1.368504e+00, 'kv_w_k': 4.277328e-02, 'kv_b_k': 1.031418e-02, 'kv_w_v': 1.250802e-01, 'kv_b_v': 1.847675e+00, 'attn_w_q': 1.509182e-02, 'attn_b_q': 1.529872e-02, 'attn_sinks': 6.679729e-02, 'attn_w_o': 4.532810e-02, 'attn_b_o': 1.362218e+00, 'ffn_w_gate': 4.643693e-02, 'ffn_w_up': 4.565004e-02, 'ffn_w_down': 1.512989e-01, 'ln_mix_g': 1.141258e+01, 'ln_mix_b': 1.937497e+00, 'ln_ffn_g': 9.203324e+01, 'ln_ffn_b': 7.388582e+00}


def _to_microbatches(a, axis):
    t = _jnp.moveaxis(a, axis, 0)
    t = t.reshape((N_MICROBATCH, t.shape[0] // N_MICROBATCH) + t.shape[1:])
    return _jnp.moveaxis(t, 1, axis + 1)


def setup_inputs(seed: int = 0) -> dict:
    inp = _fwd_setup_inputs(seed)
    key = _jax.random.fold_in(_jax.random.key(seed), 7919)
    shape, _ = _output_shape()
    out = dict(inp)
    out["loss_target"] = _jax.random.normal(_jax.random.fold_in(key, 0), shape, _jnp.float32)
    for i, name in enumerate(TWIN_WEIGHTS):
        w = inp[name].astype(_jnp.float32)
        if MOMENT_SCALE is None:
            s = _jnp.sqrt(_jnp.mean(_jnp.square(w)) + 1e-30)
        else:
            s = MOMENT_SCALE[name]
        km, kv = _jax.random.split(_jax.random.fold_in(key, i + 1))
        out[name] = w
        out["m_" + name] = s * _jax.random.normal(km, w.shape, _jnp.float32)
        out["v_" + name] = (s * s) * _jax.random.uniform(kv, w.shape, _jnp.float32, 0.5, 1.5)
    if N_MICROBATCH > 1:
        for name, axis in PER_EXAMPLE_BATCH_AXIS.items():
            out[name] = _to_microbatches(out[name], axis)
    return {'x': out['x'], 'conv_w_pw1': out['conv_w_pw1'], 'conv_b_pw1': out['conv_b_pw1'], 'conv_w_dw': out['conv_w_dw'], 'conv_b_dw': out['conv_b_dw'], 'conv_ln_g': out['conv_ln_g'], 'conv_ln_b': out['conv_ln_b'], 'conv_w_pw2': out['conv_w_pw2'], 'conv_b_pw2': out['conv_b_pw2'], 'kv_w_k': out['kv_w_k'], 'kv_b_k': out['kv_b_k'], 'kv_w_v': out['kv_w_v'], 'kv_b_v': out['kv_b_v'], 'attn_w_q': out['attn_w_q'], 'attn_b_q': out['attn_b_q'], 'attn_sinks': out['attn_sinks'], 'attn_w_o': out['attn_w_o'], 'attn_b_o': out['attn_b_o'], 'ffn_w_gate': out['ffn_w_gate'], 'ffn_w_up': out['ffn_w_up'], 'ffn_w_down': out['ffn_w_down'], 'ln_mix_g': out['ln_mix_g'], 'ln_mix_b': out['ln_mix_b'], 'ln_ffn_g': out['ln_ffn_g'], 'ln_ffn_b': out['ln_ffn_b'], 'loss_target': out['loss_target'], 'm_conv_w_pw1': out['m_conv_w_pw1'], 'm_conv_b_pw1': out['m_conv_b_pw1'], 'm_conv_w_dw': out['m_conv_w_dw'], 'm_conv_b_dw': out['m_conv_b_dw'], 'm_conv_ln_g': out['m_conv_ln_g'], 'm_conv_ln_b': out['m_conv_ln_b'], 'm_conv_w_pw2': out['m_conv_w_pw2'], 'm_conv_b_pw2': out['m_conv_b_pw2'], 'm_kv_w_k': out['m_kv_w_k'], 'm_kv_b_k': out['m_kv_b_k'], 'm_kv_w_v': out['m_kv_w_v'], 'm_kv_b_v': out['m_kv_b_v'], 'm_attn_w_q': out['m_attn_w_q'], 'm_attn_b_q': out['m_attn_b_q'], 'm_attn_sinks': out['m_attn_sinks'], 'm_attn_w_o': out['m_attn_w_o'], 'm_attn_b_o': out['m_attn_b_o'], 'm_ffn_w_gate': out['m_ffn_w_gate'], 'm_ffn_w_up': out['m_ffn_w_up'], 'm_ffn_w_down': out['m_ffn_w_down'], 'm_ln_mix_g': out['m_ln_mix_g'], 'm_ln_mix_b': out['m_ln_mix_b'], 'm_ln_ffn_g': out['m_ln_ffn_g'], 'm_ln_ffn_b': out['m_ln_ffn_b'], 'v_conv_w_pw1': out['v_conv_w_pw1'], 'v_conv_b_pw1': out['v_conv_b_pw1'], 'v_conv_w_dw': out['v_conv_w_dw'], 'v_conv_b_dw': out['v_conv_b_dw'], 'v_conv_ln_g': out['v_conv_ln_g'], 'v_conv_ln_b': out['v_conv_ln_b'], 'v_conv_w_pw2': out['v_conv_w_pw2'], 'v_conv_b_pw2': out['v_conv_b_pw2'], 'v_kv_w_k': out['v_kv_w_k'], 'v_kv_b_k': out['v_kv_b_k'], 'v_kv_w_v': out['v_kv_w_v'], 'v_kv_b_v': out['v_kv_b_v'], 'v_attn_w_q': out['v_attn_w_q'], 'v_attn_b_q': out['v_attn_b_q'], 'v_attn_sinks': out['v_attn_sinks'], 'v_attn_w_o': out['v_attn_w_o'], 'v_attn_b_o': out['v_attn_b_o'], 'v_ffn_w_gate': out['v_ffn_w_gate'], 'v_ffn_w_up': out['v_ffn_w_up'], 'v_ffn_w_down': out['v_ffn_w_down'], 'v_ln_mix_g': out['v_ln_mix_g'], 'v_ln_mix_b': out['v_ln_mix_b'], 'v_ln_ffn_g': out['v_ln_ffn_g'], 'v_ln_ffn_b': out['v_ln_ffn_b']}


def _loss(weights, diff, rest, loss_target):
    with _jax.named_scope("forward"):
        args = {**rest, TWIN_DIFF_INPUT: diff, **{k: w.astype(_WEIGHT_DTYPES[k]) for k, w in weights.items()}}
        y = _forward(args)
    with _jax.named_scope("loss_head"):
        err = _jnp.square(y.astype(_jnp.float32) - loss_target)
        return 0.5 * _jnp.sum(_jnp.mean(err, axis=-1)) if err.ndim else 0.5 * err


def _adamw(w, g, m, v):
    m = ADAM_B1 * m + (1.0 - ADAM_B1) * g
    v = ADAM_B2 * v + (1.0 - ADAM_B2) * _jnp.square(g)
    m_hat = m / (1.0 - ADAM_B1 ** ADAM_STEP)
    v_hat = v / (1.0 - ADAM_B2 ** ADAM_STEP)
    delta = -ADAM_LR * (m_hat / (_jnp.sqrt(v_hat) + ADAM_EPS) + ADAM_WD * w)
    return delta, m, v


def reference(x, conv_w_pw1, conv_b_pw1, conv_w_dw, conv_b_dw, conv_ln_g, conv_ln_b, conv_w_pw2, conv_b_pw2, kv_w_k, kv_b_k, kv_w_v, kv_b_v, attn_w_q, attn_b_q, attn_sinks, attn_w_o, attn_b_o, ffn_w_gate, ffn_w_up, ffn_w_down, ln_mix_g, ln_mix_b, ln_ffn_g, ln_ffn_b, loss_target, m_conv_w_pw1, m_conv_b_pw1, m_conv_w_dw, m_conv_b_dw, m_conv_ln_g, m_conv_ln_b, m_conv_w_pw2, m_conv_b_pw2, m_kv_w_k, m_kv_b_k, m_kv_w_v, m_kv_b_v, m_attn_w_q, m_attn_b_q, m_attn_sinks, m_attn_w_o, m_attn_b_o, m_ffn_w_gate, m_ffn_w_up, m_ffn_w_down, m_ln_mix_g, m_ln_mix_b, m_ln_ffn_g, m_ln_ffn_b, v_conv_w_pw1, v_conv_b_pw1, v_conv_w_dw, v_conv_b_dw, v_conv_ln_g, v_conv_ln_b, v_conv_w_pw2, v_conv_b_pw2, v_kv_w_k, v_kv_b_k, v_kv_w_v, v_kv_b_v, v_attn_w_q, v_attn_b_q, v_attn_sinks, v_attn_w_o, v_attn_b_o, v_ffn_w_gate, v_ffn_w_up, v_ffn_w_down, v_ln_mix_g, v_ln_mix_b, v_ln_ffn_g, v_ln_ffn_b):
    given = dict(x=x, conv_w_pw1=conv_w_pw1, conv_b_pw1=conv_b_pw1, conv_w_dw=conv_w_dw, conv_b_dw=conv_b_dw, conv_ln_g=conv_ln_g, conv_ln_b=conv_ln_b, conv_w_pw2=conv_w_pw2, conv_b_pw2=conv_b_pw2, kv_w_k=kv_w_k, kv_b_k=kv_b_k, kv_w_v=kv_w_v, kv_b_v=kv_b_v, attn_w_q=attn_w_q, attn_b_q=attn_b_q, attn_sinks=attn_sinks, attn_w_o=attn_w_o, attn_b_o=attn_b_o, ffn_w_gate=ffn_w_gate, ffn_w_up=ffn_w_up, ffn_w_down=ffn_w_down, ln_mix_g=ln_mix_g, ln_mix_b=ln_mix_b, ln_ffn_g=ln_ffn_g, ln_ffn_b=ln_ffn_b, loss_target=loss_target, m_conv_w_pw1=m_conv_w_pw1, m_conv_b_pw1=m_conv_b_pw1, m_conv_w_dw=m_conv_w_dw, m_conv_b_dw=m_conv_b_dw, m_conv_ln_g=m_conv_ln_g, m_conv_ln_b=m_conv_ln_b, m_conv_w_pw2=m_conv_w_pw2, m_conv_b_pw2=m_conv_b_pw2, m_kv_w_k=m_kv_w_k, m_kv_b_k=m_kv_b_k, m_kv_w_v=m_kv_w_v, m_kv_b_v=m_kv_b_v, m_attn_w_q=m_attn_w_q, m_attn_b_q=m_attn_b_q, m_attn_sinks=m_attn_sinks, m_attn_w_o=m_attn_w_o, m_attn_b_o=m_attn_b_o, m_ffn_w_gate=m_ffn_w_gate, m_ffn_w_up=m_ffn_w_up, m_ffn_w_down=m_ffn_w_down, m_ln_mix_g=m_ln_mix_g, m_ln_mix_b=m_ln_mix_b, m_ln_ffn_g=m_ln_ffn_g, m_ln_ffn_b=m_ln_ffn_b, v_conv_w_pw1=v_conv_w_pw1, v_conv_b_pw1=v_conv_b_pw1, v_conv_w_dw=v_conv_w_dw, v_conv_b_dw=v_conv_b_dw, v_conv_ln_g=v_conv_ln_g, v_conv_ln_b=v_conv_ln_b, v_conv_w_pw2=v_conv_w_pw2, v_conv_b_pw2=v_conv_b_pw2, v_kv_w_k=v_kv_w_k, v_kv_b_k=v_kv_b_k, v_kv_w_v=v_kv_w_v, v_kv_b_v=v_kv_b_v, v_attn_w_q=v_attn_w_q, v_attn_b_q=v_attn_b_q, v_attn_sinks=v_attn_sinks, v_attn_w_o=v_attn_w_o, v_attn_b_o=v_attn_b_o, v_ffn_w_gate=v_ffn_w_gate, v_ffn_w_up=v_ffn_w_up, v_ffn_w_down=v_ffn_w_down, v_ln_mix_g=v_ln_mix_g, v_ln_mix_b=v_ln_mix_b, v_ln_ffn_g=v_ln_ffn_g, v_ln_ffn_b=v_ln_ffn_b)
    weights = {n: given[n] for n in TWIN_WEIGHTS}
    shared = {n: given[n] for n in SHARED_INPUTS}
    per_example = {n: given[n] for n in ['x']}
    grad_fn = _jax.value_and_grad(_loss, argnums=(0, 1))

    def one_microbatch(ex, loss_target):
        ex = dict(ex)
        diff = ex.pop(TWIN_DIFF_INPUT)
        return grad_fn(weights, diff, {**shared, **ex}, loss_target)

    if N_MICROBATCH == 1:
        loss, (grad_w, grad_x) = one_microbatch(per_example, given["loss_target"])
    else:
        def body(carry, xs):
            loss_sum, grad_sum = carry
            l_k, (gw_k, gx_k) = one_microbatch(xs[0], xs[1])
            with _jax.named_scope("update"):
                return (loss_sum + l_k, _jax.tree.map(_jnp.add, grad_sum, gw_k)), gx_k

        init = (_jnp.zeros((), _jnp.float32), _jax.tree.map(_jnp.zeros_like, weights))
        (loss, grad_w), grad_x = _jax.lax.scan(body, init, (per_example, given["loss_target"]))
    with _jax.named_scope("update"):
        delta_w, new_m, new_v = {}, {}, {}
        for n in TWIN_WEIGHTS:
            delta_w[n], new_m[n], new_v[n] = _adamw(weights[n], grad_w[n], given["m_" + n], given["v_" + n])
    return (loss, grad_x, *[grad_w[n] for n in TWIN_WEIGHTS], *[delta_w[n] for n in TWIN_WEIGHTS],
            *[new_m[n] for n in TWIN_WEIGHTS], *[new_v[n] for n in TWIN_WEIGHTS])
```

```python
import functools

import jax
import jax.numpy as jnp
from jax import lax
from jax.experimental import pallas as pl
from jax.experimental.pallas import tpu as pltpu

F32 = jnp.float32
BF16 = jnp.bfloat16

N_DEV = 8
HEAD_DIM = 64
N_KV_HEADS = 2
BLOCK = 128
CONV_WIDTH = 31
HALO = 32
ALIBI_MAX = 8.0
DEPTH = 2
ALPHA = (2.0 * DEPTH) ** 0.25
LN_EPS = 1e-5
MASKED_DIST = 1e32
ADAM_LR = 0.001
ADAM_B1 = 0.9
ADAM_B2 = 0.999
ADAM_EPS = 1e-08
ADAM_WD = 0.01
ADAM_STEP = 10
VMEM_LIMIT = 56 * 1024 * 1024
MESH = pl.DeviceIdType.MESH


def _dot(a, b):
    return jnp.dot(a, b, preferred_element_type=F32)


def _dot_nt(a, b):
    return lax.dot_general(a, b, (((1,), (1,)), ((), ())), preferred_element_type=F32)


def _dot_tn(a, b):
    return lax.dot_general(a, b, (((0,), (0,)), ((), ())), preferred_element_type=F32)


def _sigmoid(v):
    return 1.0 / (1.0 + jnp.exp(-v))


def _ln_fwd(z):
    mu = jnp.mean(z, axis=-1, keepdims=True)
    zc = z - mu
    var = jnp.mean(zc * zc, axis=-1, keepdims=True)
    rstd = lax.rsqrt(var + LN_EPS)
    return zc * rstd, rstd


def _ln_bwd(dout, xh, rstd, g):
    dxh = dout * g
    m1 = jnp.mean(dxh, axis=-1, keepdims=True)
    m2 = jnp.mean(dxh * xh, axis=-1, keepdims=True)
    dz = rstd * (dxh - m1 - xh * m2)
    return dz, jnp.sum(dout * xh, axis=0, keepdims=True), jnp.sum(dout, axis=0, keepdims=True)


def _params(vmem=VMEM_LIMIT):
    return pltpu.CompilerParams(dimension_semantics=("arbitrary",), vmem_limit_bytes=vmem)


def _row(d):
    return pl.BlockSpec((1, d), lambda i: (0, 0))


def _tile(tm, d):
    return pl.BlockSpec((tm, d), lambda i: (i, 0))


def _fixed(r, d):
    return pl.BlockSpec((r, d), lambda i: (0, 0))


ANY = pl.BlockSpec(memory_space=pl.ANY)


class _Layout:
    def __init__(self, d, f):
        self.d, self.f = d, f
        names = [("pw1t", 2 * d // N_DEV), ("pw2", d // N_DEV), ("wq", d // N_DEV), ("wo", d // N_DEV)]
        for l in range(DEPTH):
            names += [(f"gt{l}", f // N_DEV), (f"ut{l}", f // N_DEV), (f"dn{l}", f // N_DEV)]
        names += [("wkv", (d // N_DEV) * 2 * BLOCK // d)]
        self.off, self.n = {}, {}
        r = 0
        for name, n in names:
            self.off[name], self.n[name] = r, n
            r += n
        self.rows = r


def _load_weight(wb_ref, lay, name, dst):
    pltpu.sync_copy(wb_ref.at[:, pl.ds(lay.off[name], lay.n[name]), :], dst)


def _wscratch(lay, name):
    return pltpu.VMEM((N_DEV, lay.n[name], lay.d), BF16)


def _wfull(ref):
    s = ref.shape
    return ref[...].reshape(s[0] * s[1], s[2])


def _wchunk(ref, c, nchunks):
    s = ref.shape
    per = s[0] // nchunks
    return ref[c * per:(c + 1) * per].reshape(per * s[1], s[2])


CONV_RB = 32
CONV_LC = 256
CONV_WIN = CONV_RB + HALO + 8


def _shifted(win, r):
    return win if r == 0 else pltpu.roll(win, win.shape[0] - r, 0)


def _conv_fwd(x, wb, lay, w_dw, b_pw1, b_dw, cg, cb, b_pw2, lg, lb, tm):
    t, d = x.shape
    nsteps = t // tm

    def body(x_ref, xh_ref, wb_ref, wdw_ref, b1_ref, bdw_ref, cg_ref, cb_ref, b2_ref, lg_ref, lb_ref,
             xb_ref, ag_ref, xhc_ref, rsc_ref, xh1_ref, rs1_ref, w1_s, w2_s, ubuf, cv_s):
        i = pl.program_id(0)

        @pl.when(i == 0)
        def _():
            _load_weight(wb_ref, lay, "pw1t", w1_s)
            _load_weight(wb_ref, lay, "pw2", w2_s)
            ubuf[pl.ds(HALO + tm, 8), :] = jnp.zeros((8, d), F32)

        w1 = _wfull(w1_s)
        xv = x_ref[...]
        xb = xv.astype(BF16)
        xb_ref[...] = xb
        h = _dot_nt(xb, w1) + b1_ref[...]
        ag_ref[...] = h.astype(BF16)
        ubuf[pl.ds(HALO, tm), :] = h[:, :d] * _sigmoid(h[:, d:])
        hh = _dot_nt(xh_ref[...].astype(BF16), w1) + b1_ref[...]
        uh = hh[:, :d] * _sigmoid(hh[:, d:])
        ubuf[pl.ds(0, HALO), :] = jnp.where(i > 0, uh, 0.0)

        def conv_block(rb, carry):
            t0 = pl.multiple_of(rb * CONV_RB, CONV_RB)
            for lc in range(d // CONV_LC):
                lanes = slice(lc * CONV_LC, (lc + 1) * CONV_LC)
                win = ubuf[pl.ds(t0, CONV_WIN), lanes]
                acc = jnp.zeros((CONV_RB, CONV_LC), F32)
                for r in range(8):
                    wr = _shifted(win, r)
                    for k in range(CONV_WIDTH):
                        s = HALO - (CONV_WIDTH - 1) + k
                        if s % 8 == r:
                            q = 8 * (s // 8)
                            acc = acc + wr[q:q + CONV_RB] * wdw_ref[k:k + 1, lanes]
                cv_s[pl.ds(t0, CONV_RB), lanes] = acc
            return carry

        lax.fori_loop(0, tm // CONV_RB, conv_block, 0)
        cv = cv_s[...] + bdw_ref[...]
        xhc, rsc = _ln_fwd(cv)
        xhc_ref[...] = xhc
        rsc_ref[...] = rsc
        n = xhc * cg_ref[...] + cb_ref[...]
        s_act = n * _sigmoid(n)
        m = _dot(s_act.astype(BF16), _wfull(w2_s)) + b2_ref[...]
        xh1, rs1 = _ln_fwd(ALPHA * xv + m)
        xh1_ref[...] = xh1
        rs1_ref[...] = rs1

    hb = tm // HALO
    return pl.pallas_call(
        body, name="conv_fwd", grid=(nsteps,),
        in_specs=[_tile(tm, d), pl.BlockSpec((HALO, d), lambda i: (jnp.maximum(i * hb - 1, 0), 0)), ANY,
                  _fixed(HALO, d), _row(2 * d), _row(d), _row(d), _row(d), _row(d), _row(d), _row(d)],
        out_specs=[_tile(tm, d), _tile(tm, 2 * d), _tile(tm, d), _tile(tm, 1), _tile(tm, d), _tile(tm, 1)],
        out_shape=[jax.ShapeDtypeStruct((t, d), BF16), jax.ShapeDtypeStruct((t, 2 * d), BF16),
                   jax.ShapeDtypeStruct((t, d), F32), jax.ShapeDtypeStruct((t, 1), F32),
                   jax.ShapeDtypeStruct((t, d), F32), jax.ShapeDtypeStruct((t, 1), F32)],
        scratch_shapes=[_wscratch(lay, "pw1t"), _wscratch(lay, "pw2"),
                        pltpu.VMEM((HALO + tm + 8, d), F32), pltpu.VMEM((tm, d), F32)],
        compiler_params=_params(),
    )(x, x, wb, w_dw, b_pw1, b_dw, cg, cb, b_pw2, lg, lb)


def _conv_bwd1(dz1, xhc, rsc, wb, lay, cg, cb, tm):
    t, d = dz1.shape

    def body(dz_ref, xhc_ref, rsc_ref, wb_ref, cg_ref, cb_ref, dzb_ref, s_ref, dcv_ref, st_ref, w2_s):
        i = pl.program_id(0)

        @pl.when(i == 0)
        def _():
            _load_weight(wb_ref, lay, "pw2", w2_s)
            st_ref[...] = jnp.zeros(st_ref.shape, F32)

        dz = dz_ref[...]
        dzb = dz.astype(BF16)
        dzb_ref[...] = dzb
        xhc_v = xhc_ref[...]
        n = xhc_v * cg_ref[...] + cb_ref[...]
        sg = _sigmoid(n)
        s_ref[...] = (n * sg).astype(BF16)
        ds = _dot_nt(dzb, _wfull(w2_s))
        dn = ds * (sg * (1.0 + n * (1.0 - sg)))
        dcv, dg, db = _ln_bwd(dn, xhc_v, rsc_ref[...], cg_ref[...])
        dcv_ref[...] = dcv
        st_ref[0:1, :] += dg
        st_ref[1:2, :] += db
        st_ref[2:3, :] += jnp.sum(dcv, axis=0, keepdims=True)
        st_ref[3:4, :] += jnp.sum(dz, axis=0, keepdims=True)

    return pl.pallas_call(
        body, name="conv_bwd1", grid=(t // tm,),
        in_specs=[_tile(tm, d), _tile(tm, d), _tile(tm, 1), ANY, _row(d), _row(d)],
        out_specs=[_tile(tm, d), _tile(tm, d), _tile(tm, d), _fixed(8, d)],
        out_shape=[jax.ShapeDtypeStruct((t, d), BF16), jax.ShapeDtypeStruct((t, d), BF16),
                   jax.ShapeDtypeStruct((t, d), F32), jax.ShapeDtypeStruct((8, d), F32)],
        scratch_shapes=[_wscratch(lay, "pw2")],
        compiler_params=_params(),
    )(dz1, xhc, rsc, wb, cg, cb)


def _conv_bwd2(dz1, dcv, ag, wb, lay, w_dw, tm):
    t, d = dz1.shape
    nsteps = t // tm

    def body(dz_ref, dcv_ref, dcvn_ref, ag_ref, agp_ref, wb_ref, wdw_ref,
             gx_ref, dh_ref, dw_ref, db1_ref, w1_s, ubuf, dbuf, du_s, dwacc):
        i = pl.program_id(0)

        @pl.when(i == 0)
        def _():
            _load_weight(wb_ref, lay, "pw1t", w1_s)
            ubuf[pl.ds(HALO + tm, 8), :] = jnp.zeros((8, d), F32)
            dbuf[pl.ds(HALO + tm, 8), :] = jnp.zeros((8, d), F32)
            dwacc[...] = jnp.zeros(dwacc.shape, F32)
            db1_ref[...] = jnp.zeros(db1_ref.shape, F32)

        ag_v = ag_ref[...].astype(F32)
        a, g = ag_v[:, :d], ag_v[:, d:]
        sg = _sigmoid(g)
        ubuf[pl.ds(HALO, tm), :] = a * sg
        agp = agp_ref[...].astype(F32)
        ubuf[pl.ds(0, HALO), :] = jnp.where(i > 0, agp[:, :d] * _sigmoid(agp[:, d:]), 0.0)
        dbuf[pl.ds(0, tm), :] = dcv_ref[...]
        dbuf[pl.ds(tm, HALO), :] = jnp.where(i < nsteps - 1, dcvn_ref[...], 0.0)

        def conv_block(rb, carry):
            t0 = pl.multiple_of(rb * CONV_RB, CONV_RB)
            for lc in range(d // CONV_LC):
                lanes = slice(lc * CONV_LC, (lc + 1) * CONV_LC)
                dwin = dbuf[pl.ds(t0, CONV_WIN), lanes]
                uwin = ubuf[pl.ds(t0, CONV_WIN), lanes]
                dcur = dwin[0:CONV_RB]
                acc = jnp.zeros((CONV_RB, CONV_LC), F32)
                for r in range(8):
                    dr = _shifted(dwin, r)
                    ur = _shifted(uwin, r)
                    for k in range(CONV_WIDTH):
                        sd = CONV_WIDTH - 1 - k
                        if sd % 8 == r:
                            q = 8 * (sd // 8)
                            acc = acc + dr[q:q + CONV_RB] * wdw_ref[k:k + 1, lanes]
                        su = HALO - (CONV_WIDTH - 1) + k
                        if su % 8 == r:
                            q = 8 * (su // 8)
                            prod = dcur * ur[q:q + CONV_RB]
                            part = prod[0:8]
                            for j in range(1, CONV_RB // 8):
                                part = part + prod[8 * j:8 * j + 8]
                            dwacc[k, :, lanes] += part
                du_s[pl.ds(t0, CONV_RB), lanes] = acc
            return carry

        lax.fori_loop(0, tm // CONV_RB, conv_block, 0)
        du = du_s[...]
        da = du * sg
        dg = du * a * sg * (1.0 - sg)
        dh = jnp.concatenate([da, dg], axis=1)
        dhb = dh.astype(BF16)
        dh_ref[...] = dhb
        db1_ref[...] += jnp.sum(dh, axis=0, keepdims=True)
        gx_ref[...] = ALPHA * dz_ref[...] + _dot(dhb, _wfull(w1_s))

        @pl.when(i == nsteps - 1)
        def _():
            dw_ref[...] = jnp.sum(dwacc[...], axis=1)

    hb = tm // HALO
    last = t // HALO - 1
    return pl.pallas_call(
        body, name="conv_bwd2", grid=(nsteps,),
        in_specs=[_tile(tm, d), _tile(tm, d),
                  pl.BlockSpec((HALO, d), lambda i: (jnp.minimum((i + 1) * hb, last), 0)),
                  _tile(tm, 2 * d),
                  pl.BlockSpec((HALO, 2 * d), lambda i: (jnp.maximum(i * hb - 1, 0), 0)),
                  ANY, _fixed(HALO, d)],
        out_specs=[_tile(tm, d), _tile(tm, 2 * d), _fixed(HALO, d), _row(2 * d)],
        out_shape=[jax.ShapeDtypeStruct((t, d), F32), jax.ShapeDtypeStruct((t, 2 * d), BF16),
                   jax.ShapeDtypeStruct((HALO, d), F32), jax.ShapeDtypeStruct((1, 2 * d), F32)],
        scratch_shapes=[_wscratch(lay, "pw1t"), pltpu.VMEM((HALO + tm + 8, d), F32),
                        pltpu.VMEM((HALO + tm + 8, d), F32), pltpu.VMEM((tm, d), F32),
                        pltpu.VMEM((HALO, 8, d), F32)],
        compiler_params=_params(),
    )(dz1, dcv, dcv, ag, ag, wb, w_dw)


FFN_CHUNKS = 2


def _ffn_fwd(xh_in, g_in, b_in, wb, lay, layer, tm, *, kv=None, loss=None):
    t, d = xh_in.shape
    f = lay.f
    fc = f // FFN_CHUNKS
    names = (f"gt{layer}", f"ut{layer}", f"dn{layer}")

    def body(*refs):
        xh_ref, gi_ref, bi_ref, wb_ref = refs[:4]
        pos = 4
        if kv is not None:
            go_ref, bo_ref, wkv_ref, bkv_ref = refs[pos:pos + 4]
            pos += 4
        if loss is not None:
            go_ref, bo_ref, tgt_ref = refs[pos:pos + 3]
            pos += 3
        xb_ref, hg_ref, hu_ref = refs[pos:pos + 3]
        pos += 3
        if kv is not None:
            xho_ref, rso_ref, xob_ref, kv_ref = refs[pos:pos + 4]
            pos += 4
        if loss is not None:
            dz_ref, st_ref, loss_ref = refs[pos:pos + 3]
            pos += 3
        gt_s, ut_s, dn_s = refs[pos:pos + 3]
        i = pl.program_id(0)

        @pl.when(i == 0)
        def _():
            for name, dst in zip(names, (gt_s, ut_s, dn_s)):
                _load_weight(wb_ref, lay, name, dst)
            if loss is not None:
                st_ref[...] = jnp.zeros(st_ref.shape, F32)
                loss_ref[...] = jnp.zeros(loss_ref.shape, F32)

        xin = xh_ref[...] * gi_ref[...] + bi_ref[...]
        xb = xin.astype(BF16)
        xb_ref[...] = xb
        fo = jnp.zeros((tm, d), F32)
        for c in range(FFN_CHUNKS):
            rows = slice(c * fc, (c + 1) * fc)
            hg = _dot_nt(xb, _wchunk(gt_s, c, FFN_CHUNKS))
            hu = _dot_nt(xb, _wchunk(ut_s, c, FFN_CHUNKS))
            hg_ref[:, rows] = hg.astype(BF16)
            hu_ref[:, rows] = hu.astype(BF16)
            act = hg * _sigmoid(hg) * hu
            fo = fo + _dot(act.astype(BF16), _wchunk(dn_s, c, FFN_CHUNKS))
        xho, rso = _ln_fwd(ALPHA * xin + fo)
        if kv is not None:
            xho_ref[...] = xho
            rso_ref[...] = rso
            xob = (xho * go_ref[...] + bo_ref[...]).astype(BF16)
            xob_ref[...] = xob
            kv_ref[...] = (_dot(xob, wkv_ref[...]) + bkv_ref[...]).astype(BF16)
        if loss is not None:
            diff = xho * go_ref[...] + bo_ref[...] - tgt_ref[...]
            loss_ref[...] += (0.5 / d) * jnp.sum(diff * diff)
            dz, dg, db = _ln_bwd(diff * (1.0 / d), xho, rso, go_ref[...])
            dz_ref[...] = dz
            st_ref[0:1, :] += dg
            st_ref[1:2, :] += db

    in_specs = [_tile(tm, d), _row(d), _row(d), ANY]
    args = [xh_in, g_in, b_in, wb]
    out_specs = [_tile(tm, d), _tile(tm, f), _tile(tm, f)]
    out_shape = [jax.ShapeDtypeStruct((t, d), BF16), jax.ShapeDtypeStruct((t, f), BF16),
                 jax.ShapeDtypeStruct((t, f), BF16)]
    if kv is not None:
        in_specs += [_row(d), _row(d), _fixed(d, 2 * BLOCK), _row(2 * BLOCK)]
        args += list(kv)
        out_specs += [_tile(tm, d), _tile(tm, 1), _tile(tm, d), _tile(tm, 2 * BLOCK)]
        out_shape += [jax.ShapeDtypeStruct((t, d), F32), jax.ShapeDtypeStruct((t, 1), F32),
                      jax.ShapeDtypeStruct((t, d), BF16), jax.ShapeDtypeStruct((t, 2 * BLOCK), BF16)]
    if loss is not None:
        in_specs += [_row(d), _row(d), _tile(tm, d)]
        args += list(loss)
        out_specs += [_tile(tm, d), _fixed(8, d), _fixed(8, 128)]
        out_shape += [jax.ShapeDtypeStruct((t, d), F32), jax.ShapeDtypeStruct((8, d), F32),
                      jax.ShapeDtypeStruct((8, 128), F32)]
    return pl.pallas_call(
        body, name=f"ffn_fwd{layer}", grid=(t // tm,), in_specs=in_specs, out_specs=out_specs,
        out_shape=out_shape, scratch_shapes=[_wscratch(lay, n) for n in names],
        compiler_params=_params(),
    )(*args)


def _ffn_bwd(dz, hg, hu, xh_in, rs_in, g_in, wb, lay, layer, tm):
    t, d = dz.shape
    f = lay.f
    fc = f // FFN_CHUNKS
    names = (f"gt{layer}", f"ut{layer}", f"dn{layer}")

    def body(dz_ref, hg_ref, hu_ref, xh_ref, rs_ref, gi_ref, wb_ref,
             dzb_ref, act_ref, dhg_ref, dhu_ref, dzp_ref, st_ref, gt_s, ut_s, dn_s):
        i = pl.program_id(0)

        @pl.when(i == 0)
        def _():
            for name, dst in zip(names, (gt_s, ut_s, dn_s)):
                _load_weight(wb_ref, lay, name, dst)
            st_ref[...] = jnp.zeros(st_ref.shape, F32)

        dzv = dz_ref[...]
        dzb = dzv.astype(BF16)
        dzb_ref[...] = dzb
        dx = ALPHA * dzv
        for c in range(FFN_CHUNKS):
            rows = slice(c * fc, (c + 1) * fc)
            hg_v = hg_ref[:, rows].astype(F32)
            hu_v = hu_ref[:, rows].astype(F32)
            sg = _sigmoid(hg_v)
            silu = hg_v * sg
            act_ref[:, rows] = (silu * hu_v).astype(BF16)
            dact = _dot_nt(dzb, _wchunk(dn_s, c, FFN_CHUNKS))
            dhu = (dact * silu).astype(BF16)
            dhg = (dact * hu_v * (sg * (1.0 + hg_v * (1.0 - sg)))).astype(BF16)
            dhu_ref[:, rows] = dhu
            dhg_ref[:, rows] = dhg
            dx = dx + _dot(dhg, _wchunk(gt_s, c, FFN_CHUNKS)) + _dot(dhu, _wchunk(ut_s, c, FFN_CHUNKS))
        dzp, dg, db = _ln_bwd(dx, xh_ref[...], rs_ref[...], gi_ref[...])
        dzp_ref[...] = dzp
        st_ref[0:1, :] += dg
        st_ref[1:2, :] += db

    return pl.pallas_call(
        body, name=f"ffn_bwd{layer}", grid=(t // tm,),
        in_specs=[_tile(tm, d), _tile(tm, f), _tile(tm, f), _tile(tm, d), _tile(tm, 1), _row(d), ANY],
        out_specs=[_tile(tm, d), _tile(tm, f), _tile(tm, f), _tile(tm, f), _tile(tm, d), _fixed(8, d)],
        out_shape=[jax.ShapeDtypeStruct((t, d), BF16), jax.ShapeDtypeStruct((t, f), BF16),
                   jax.ShapeDtypeStruct((t, f), BF16), jax.ShapeDtypeStruct((t, f), BF16),
                   jax.ShapeDtypeStruct((t, d), F32), jax.ShapeDtypeStruct((8, d), F32)],
        scratch_shapes=[_wscratch(lay, n) for n in names],
        compiler_params=_params(),
    )(dz, hg, hu, xh_in, rs_in, g_in, wb)


def _alibi_slope(h, nq):
    return 2.0 ** (-ALIBI_MAX * (h + 1) / nq)


def _band_distance():
    qi = lax.broadcasted_iota(jnp.int32, (BLOCK, 2 * BLOCK), 0)
    kj = lax.broadcasted_iota(jnp.int32, (BLOCK, 2 * BLOCK), 1)
    delta = qi + BLOCK - kj
    valid = (delta >= 0) & (delta < BLOCK)
    return jnp.where(valid, delta.astype(F32), MASKED_DIST), kj


def _padded_kv(kvb, kvh):
    lane = lax.broadcasted_iota(jnp.int32, (2 * BLOCK, BLOCK), 1)
    mine = (lane < HEAD_DIM) if kvh == 0 else (lane >= HEAD_DIM)
    out = []
    for sec in (kvb[:, :BLOCK], kvb[:, BLOCK:]):
        m = jnp.where(mine, sec.astype(F32), 0.0)
        sw = pltpu.roll(m, HEAD_DIM, 1)
        pair = (m, sw) if kvh == 0 else (sw, m)
        out.append(tuple(p.astype(BF16) for p in pair))
    return out


def _attn_fwd(xh_in, g_in, b_in, x_in_b, kvs, wb, lay, bq, sinks, bo, tm):
    t, d = xh_in.shape
    nq = d // HEAD_DIM
    pairs_per_kv = (d // BLOCK) // N_KV_HEADS
    nbt = tm // BLOCK
    scale = HEAD_DIM ** -0.5

    def body(xh_ref, gi_ref, bi_ref, xb_ref, kv_ref, kvp_ref, wb_ref, bq_ref, sk_ref, bo_ref,
             q_ref, o_ref, lse_ref, xho_ref, rso_ref, wq_s, wo_s, kvall, q_s, o_s):
        i = pl.program_id(0)

        @pl.when(i == 0)
        def _():
            _load_weight(wb_ref, lay, "wq", wq_s)
            _load_weight(wb_ref, lay, "wo", wo_s)

        qv = ((_dot(xb_ref[...], _wfull(wq_s)) + bq_ref[...]) * scale).astype(BF16)
        q_s[...] = qv
        q_ref[...] = qv
        kvall[pl.ds(0, BLOCK), :] = kvp_ref[...]
        kvall[pl.ds(BLOCK, tm), :] = kv_ref[...]
        dist, kj = _band_distance()

        def block(j, carry):
            r0 = pl.multiple_of(j * BLOCK, BLOCK)
            rows = pl.ds(r0, BLOCK)
            kvb = kvall[pl.ds(r0, 2 * BLOCK), :]
            dj = jnp.where((i * nbt + j > 0) | (kj >= BLOCK), dist, MASKED_DIST)
            for kvh in range(N_KV_HEADS):
                kpad, vpad = _padded_kv(kvb, kvh)
                for a in range(kvh * pairs_per_kv, (kvh + 1) * pairs_per_kv):
                    qp = q_s[rows, a * BLOCK:(a + 1) * BLOCK]
                    opair = jnp.zeros((BLOCK, BLOCK), F32)
                    for e in range(2):
                        h = 2 * a + e
                        sink = sk_ref[:, h:h + 1]
                        s = _dot_nt(qp, kpad[e]) - _alibi_slope(h, nq) * dj
                        m = jnp.maximum(jnp.max(s, axis=-1, keepdims=True), sink)
                        p = jnp.exp(s - m)
                        l = jnp.sum(p, axis=-1, keepdims=True) + jnp.exp(sink - m)
                        opair = opair + _dot((p * (1.0 / l)).astype(BF16), vpad[e])
                        lse_ref[rows, h:h + 1] = m + jnp.log(l)
                    o_s[rows, a * BLOCK:(a + 1) * BLOCK] = opair.astype(BF16)
            return carry

        lax.fori_loop(0, nbt, block, 0)
        ov = o_s[...]
        o_ref[...] = ov
        xin = xh_ref[...] * gi_ref[...] + bi_ref[...]
        xho, rso = _ln_fwd(ALPHA * xin + _dot(ov, _wfull(wo_s)) + bo_ref[...])
        xho_ref[...] = xho
        rso_ref[...] = rso

    return pl.pallas_call(
        body, name="attn_fwd", grid=(t // tm,),
        in_specs=[_tile(tm, d), _row(d), _row(d), _tile(tm, d), _tile(tm, 2 * BLOCK),
                  pl.BlockSpec((BLOCK, 2 * BLOCK), lambda i: (jnp.maximum(i * nbt - 1, 0), 0)),
                  ANY, _row(d), _row(nq), _row(d)],
        out_specs=[_tile(tm, d), _tile(tm, d), _tile(tm, nq), _tile(tm, d), _tile(tm, 1)],
        out_shape=[jax.ShapeDtypeStruct((t, d), BF16), jax.ShapeDtypeStruct((t, d), BF16),
                   jax.ShapeDtypeStruct((t, nq), F32), jax.ShapeDtypeStruct((t, d), F32),
                   jax.ShapeDtypeStruct((t, 1), F32)],
        scratch_shapes=[_wscratch(lay, "wq"), _wscratch(lay, "wo"),
                        pltpu.VMEM((BLOCK + tm, 2 * BLOCK), BF16), pltpu.VMEM((tm, d), BF16),
                        pltpu.VMEM((tm, d), BF16)],
        compiler_params=_params(),
    )(xh_in, g_in, b_in, x_in_b, kvs, kvs, wb, bq, sinks, bo)


def _attn_bwd(dz, q, o, lse, kvs, wb, lay, sinks, tm):
    t, d = dz.shape
    nq = d // HEAD_DIM
    pairs_per_kv = (d // BLOCK) // N_KV_HEADS
    nbt = tm // BLOCK
    scale = HEAD_DIM ** -0.5

    def body(dz_ref, q_ref, o_ref, lse_ref, kv_ref, kvp_ref, wb_ref, sk_ref,
             dzb_ref, dq_ref, dkc_ref, dkp_ref, st_ref, dsk_ref, wo_s, kvall, do_s, dq_s):
        i = pl.program_id(0)

        @pl.when(i == 0)
        def _():
            _load_weight(wb_ref, lay, "wo", wo_s)
            st_ref[...] = jnp.zeros(st_ref.shape, F32)
            dsk_ref[...] = jnp.zeros(dsk_ref.shape, F32)

        dzv = dz_ref[...]
        dzb = dzv.astype(BF16)
        dzb_ref[...] = dzb
        do_s[...] = _dot_nt(dzb, _wfull(wo_s))
        kvall[pl.ds(0, BLOCK), :] = kvp_ref[...]
        kvall[pl.ds(BLOCK, tm), :] = kv_ref[...]
        dist, kj = _band_distance()
        lane = lax.broadcasted_iota(jnp.int32, (BLOCK, BLOCK), 1)
        lane2 = lax.broadcasted_iota(jnp.int32, (2 * BLOCK, BLOCK), 1)

        def block(j, carry):
            r0 = pl.multiple_of(j * BLOCK, BLOCK)
            rows = pl.ds(r0, BLOCK)
            kvb = kvall[pl.ds(r0, 2 * BLOCK), :]
            dj = jnp.where((i * nbt + j > 0) | (kj >= BLOCK), dist, MASKED_DIST)
            dsecs = []
            for kvh in range(N_KV_HEADS):
                kpad, vpad = _padded_kv(kvb, kvh)
                dk_acc = jnp.zeros((2 * BLOCK, BLOCK), F32)
                dv_acc = jnp.zeros((2 * BLOCK, BLOCK), F32)
                for a in range(kvh * pairs_per_kv, (kvh + 1) * pairs_per_kv):
                    cols = slice(a * BLOCK, (a + 1) * BLOCK)
                    qp = q_ref[rows, cols]
                    dop = do_s[rows, cols]
                    prod = dop * o_ref[rows, cols].astype(F32)
                    dqp = jnp.zeros((BLOCK, BLOCK), F32)
                    for e in range(2):
                        h = 2 * a + e
                        mine = (lane < HEAD_DIM) if e == 0 else (lane >= HEAD_DIM)
                        sink = sk_ref[:, h:h + 1]
                        lse_h = lse_ref[rows, h:h + 1]
                        dh = jnp.sum(jnp.where(mine, prod, 0.0), axis=-1, keepdims=True)
                        s = _dot_nt(qp, kpad[e]) - _alibi_slope(h, nq) * dj
                        p = jnp.exp(s - lse_h)
                        dp = _dot_nt(dop.astype(BF16), vpad[e])
                        dsb = (p * (dp - dh)).astype(BF16)
                        dqp = dqp + _dot(dsb, kpad[e])
                        dk_acc = dk_acc + _dot_tn(dsb, jnp.where(mine, qp, jnp.zeros_like(qp)))
                        dv_acc = dv_acc + _dot_tn(p.astype(BF16), jnp.where(mine, dop, 0.0).astype(BF16))
                        dsk_ref[:, h:h + 1] += -jnp.sum(jnp.exp(sink - lse_h) * dh, axis=0, keepdims=True)
                    dq_s[rows, cols] = dqp * scale
                dsecs.append((dk_acc + pltpu.roll(dk_acc, HEAD_DIM, 1), dv_acc + pltpu.roll(dv_acc, HEAD_DIM, 1)))
            lo = lane2 < HEAD_DIM
            dkv = jnp.concatenate([jnp.where(lo, dsecs[0][0], dsecs[1][0]),
                                   jnp.where(lo, dsecs[0][1], dsecs[1][1])], axis=1)
            dkp_ref[rows, :] = dkv[:BLOCK]
            dkc_ref[rows, :] = dkv[BLOCK:]
            return carry

        lax.fori_loop(0, nbt, block, 0)
        dqv = dq_s[...]
        dq_ref[...] = dqv.astype(BF16)
        st_ref[0:1, :] += jnp.sum(dqv, axis=0, keepdims=True)
        st_ref[1:2, :] += jnp.sum(dzv, axis=0, keepdims=True)

    return pl.pallas_call(
        body, name="attn_bwd", grid=(t // tm,),
        in_specs=[_tile(tm, d), _tile(tm, d), _tile(tm, d), _tile(tm, nq), _tile(tm, 2 * BLOCK),
                  pl.BlockSpec((BLOCK, 2 * BLOCK), lambda i: (jnp.maximum(i * nbt - 1, 0), 0)),
                  ANY, _row(nq)],
        out_specs=[_tile(tm, d), _tile(tm, d), _tile(tm, 2 * BLOCK), _tile(tm, 2 * BLOCK),
                   _fixed(8, d), _row(nq)],
        out_shape=[jax.ShapeDtypeStruct((t, d), BF16), jax.ShapeDtypeStruct((t, d), BF16),
                   jax.ShapeDtypeStruct((t, 2 * BLOCK), F32), jax.ShapeDtypeStruct((t, 2 * BLOCK), F32),
                   jax.ShapeDtypeStruct((8, d), F32), jax.ShapeDtypeStruct((1, nq), F32)],
        scratch_shapes=[_wscratch(lay, "wo"), pltpu.VMEM((BLOCK + tm, 2 * BLOCK), BF16),
                        pltpu.VMEM((tm, d), F32), pltpu.VMEM((tm, d), F32)],
        compiler_params=_params(),
    )(dz, q, o, lse, kvs, kvs, wb, sinks)


def _x2_bwd(dz, dq, dkc, dkp, xh_in, rs_in, g_in, wb, lay, wkv, tm):
    t, d = dz.shape
    nbt = tm // BLOCK
    nsteps = t // tm
    scale = HEAD_DIM ** -0.5

    def body(dz_ref, dq_ref, dkc_ref, dkp_ref, dkn_ref, xh_ref, rs_ref, gi_ref, wb_ref, wkv_ref,
             dkv_ref, dzp_ref, st_ref, dbkv_ref, wq_s):
        i = pl.program_id(0)

        @pl.when(i == 0)
        def _():
            _load_weight(wb_ref, lay, "wq", wq_s)
            st_ref[...] = jnp.zeros(st_ref.shape, F32)
            dbkv_ref[...] = jnp.zeros(dbkv_ref.shape, F32)

        nxt = jnp.where(i < nsteps - 1, dkn_ref[...], 0.0)
        if nbt > 1:
            shifted = jnp.concatenate([dkp_ref[pl.ds(BLOCK, tm - BLOCK), :], nxt], axis=0)
        else:
            shifted = nxt
        dkv = dkc_ref[...] + shifted
        dkvb = dkv.astype(BF16)
        dkv_ref[...] = dkvb
        dbkv_ref[...] += jnp.sum(dkv, axis=0, keepdims=True)
        dx = ALPHA * dz_ref[...] + _dot_nt(dq_ref[...], _wfull(wq_s)) + _dot_nt(dkvb, wkv_ref[...])
        dzp, dg, db = _ln_bwd(dx, xh_ref[...], rs_ref[...], gi_ref[...])
        dzp_ref[...] = dzp
        st_ref[0:1, :] += dg
        st_ref[1:2, :] += db

    del scale
    last = t // BLOCK - 1
    return pl.pallas_call(
        body, name="x2_bwd", grid=(nsteps,),
        in_specs=[_tile(tm, d), _tile(tm, d), _tile(tm, 2 * BLOCK), _tile(tm, 2 * BLOCK),
                  pl.BlockSpec((BLOCK, 2 * BLOCK), lambda i: (jnp.minimum((i + 1) * nbt, last), 0)),
                  _tile(tm, d), _tile(tm, 1), _row(d), ANY, _fixed(d, 2 * BLOCK)],
        out_specs=[_tile(tm, 2 * BLOCK), _tile(tm, d), _fixed(8, d), _row(2 * BLOCK)],
        out_shape=[jax.ShapeDtypeStruct((t, 2 * BLOCK), BF16), jax.ShapeDtypeStruct((t, d), F32),
                   jax.ShapeDtypeStruct((8, d), F32), jax.ShapeDtypeStruct((1, 2 * BLOCK), F32)],
        scratch_shapes=[_wscratch(lay, "wq")],
        compiler_params=_params(),
    )(dz, dq, dkc, dkp, dkp, xh_in, rs_in, g_in, wb, wkv)


def _tn_matmul(a, b, name, bm, tk):
    t, m = a.shape
    n = b.shape[1]
    ksteps = t // tk

    def body(a_ref, b_ref, o_ref):
        k = pl.program_id(1)
        part = _dot_tn(a_ref[...], b_ref[...])

        @pl.when(k == 0)
        def _():
            o_ref[...] = part

        @pl.when(k > 0)
        def _():
            o_ref[...] += part

    return pl.pallas_call(
        body, name=name, grid=(m // bm, ksteps),
        in_specs=[pl.BlockSpec((tk, bm), lambda j, k: (k, j)), pl.BlockSpec((tk, n), lambda j, k: (k, 0))],
        out_specs=pl.BlockSpec((bm, n), lambda j, k: (j, 0)),
        out_shape=jax.ShapeDtypeStruct((m, n), F32),
        compiler_params=pltpu.CompilerParams(dimension_semantics=("arbitrary", "arbitrary"),
                                             vmem_limit_bytes=VMEM_LIMIT),
    )(a, b)


def _me():
    return lax.axis_index("x"), lax.axis_index("y"), lax.axis_index("c")


def _all_gather(arrays, name):
    n = len(arrays)

    def body(*refs):
        ins, outs = refs[:n], refs[n:2 * n]
        send_sems, recv_sems, local_sems = refs[2 * n:]
        x, y, c = _me()
        me, sibling = (x, y, c), (x, y, 1 - c)
        chips = [(1 - x, y), (x, 1 - y), (1 - x, 1 - y)]

        def slot(ref, dev):
            return ref.at[4 * dev[0] + 2 * dev[1] + dev[2]]

        def copy(a, k, block, to, src=None):
            return pltpu.make_async_remote_copy(
                src_ref=slot(outs[a], block) if src is None else src, dst_ref=slot(outs[a], block),
                send_sem=send_sems.at[a, k], recv_sem=recv_sems.at[a, k], device_id=to, device_id_type=MESH)

        mine = [pltpu.make_async_copy(ins[a], slot(outs[a], me), local_sems.at[a]) for a in range(n)]
        for cp in mine:
            cp.start()
        first = []
        for a in range(n):
            first.append(copy(a, 0, me, sibling, src=ins[a]))
            first += [copy(a, 1 + j, me, (*chip, c), src=ins[a]) for j, chip in enumerate(chips)]
        for cp in first:
            cp.start()
        passed = []
        for a in range(n):
            for j, chip in enumerate(chips):
                copy(a, 1 + j, (*chip, c), me).wait_recv()
                cp = copy(a, 4 + j, (*chip, c), sibling)
                cp.start()
                passed.append(cp)
        for a in range(n):
            copy(a, 0, sibling, me).wait_recv()
            for j, chip in enumerate(chips):
                copy(a, 4 + j, (*chip, 1 - c), me).wait_recv()
        for cp in first + passed:
            cp.wait_send()
        for cp in mine:
            cp.wait()

    return pl.pallas_call(
        body, name=name, in_specs=[ANY] * n, out_specs=[ANY] * n,
        out_shape=[jax.ShapeDtypeStruct((N_DEV,) + a.shape, a.dtype) for a in arrays],
        scratch_shapes=[pltpu.SemaphoreType.DMA((n, 7)), pltpu.SemaphoreType.DMA((n, 7)),
                        pltpu.SemaphoreType.DMA((n,))],
    )(*arrays)


def _exchange_blocks(arrays, row_offsets, rows, name):
    n = len(arrays)
    width = arrays[0].shape[2]

    def body(*refs):
        ins, out = refs[:n], refs[n]
        send_sems, recv_sems, local_sems = refs[n + 1:]
        x, y, c = _me()
        me = 4 * x + 2 * y + c

        def dst(k, src_dev):
            return out.at[src_dev, pl.ds(row_offsets[k], arrays[k].shape[1]), :]

        local = [pltpu.make_async_copy(ins[k].at[me], dst(k, me), local_sems.at[k]) for k in range(n)]
        for cp in local:
            cp.start()
        sends = []
        for mask in range(1, N_DEV):
            px = 1 - x if mask & 4 else x
            py = 1 - y if mask & 2 else y
            pc = 1 - c if mask & 1 else c
            peer = 4 * px + 2 * py + pc
            for k in range(n):
                cp = pltpu.make_async_remote_copy(
                    src_ref=ins[k].at[peer], dst_ref=dst(k, me),
                    send_sem=send_sems.at[k, mask - 1], recv_sem=recv_sems.at[k, mask - 1],
                    device_id=(px, py, pc), device_id_type=MESH)
                cp.start()
                sends.append((cp, k, peer, mask))
        for cp, k, peer, mask in sends:
            pltpu.make_async_remote_copy(
                src_ref=ins[k].at[me], dst_ref=dst(k, peer),
                send_sem=send_sems.at[k, mask - 1], recv_sem=recv_sems.at[k, mask - 1],
                device_id=(x, y, c), device_id_type=MESH).wait_recv()
        for cp, _, _, _ in sends:
            cp.wait_send()
        for cp in local:
            cp.wait()

    return pl.pallas_call(
        body, name=name, in_specs=[ANY] * n, out_specs=ANY,
        out_shape=jax.ShapeDtypeStruct((N_DEV, rows, width), arrays[0].dtype),
        scratch_shapes=[pltpu.SemaphoreType.DMA((n, 7)), pltpu.SemaphoreType.DMA((n, 7)),
                        pltpu.SemaphoreType.DMA((n,))],
    )(*arrays)


def _adamw_sum(g8, w, m, v, name, tr):
    r, width = w.shape
    bc1 = 1.0 - ADAM_B1 ** ADAM_STEP
    bc2 = 1.0 - ADAM_B2 ** ADAM_STEP

    def body(g_ref, w_ref, m_ref, v_ref, go_ref, d_ref, mo_ref, vo_ref):
        g = g_ref[0]
        for s in range(1, N_DEV):
            g = g + g_ref[s]
        mn = ADAM_B1 * m_ref[...] + (1.0 - ADAM_B1) * g
        vn = ADAM_B2 * v_ref[...] + (1.0 - ADAM_B2) * (g * g)
        m_hat = mn / bc1
        v_hat = vn / bc2
        go_ref[...] = g
        d_ref[...] = -ADAM_LR * (m_hat / (jnp.sqrt(v_hat) + ADAM_EPS) + ADAM_WD * w_ref[...])
        mo_ref[...] = mn
        vo_ref[...] = vn

    spec = pl.BlockSpec((tr, width), lambda i: (i, 0))
    return pl.pallas_call(
        body, name=name, grid=(r // tr,),
        in_specs=[pl.BlockSpec((N_DEV, tr, width), lambda i: (0, i, 0)), spec, spec, spec],
        out_specs=[spec] * 4, out_shape=[jax.ShapeDtypeStruct((r, width), F32)] * 4,
        compiler_params=_params(),
    )(g8, w, m, v)


def _local_step(x, target, wb, lay, wkv, sm, tm, tk):
    t, d = x.shape
    f = lay.f
    w_dw32 = jnp.concatenate([sm["w_dw"], jnp.zeros((HALO - CONV_WIDTH, d), F32)], axis=0)
    lmg, lmb, lfg, lfb = sm["ln_mix_g"], sm["ln_mix_b"], sm["ln_ffn_g"], sm["ln_ffn_b"]
    bkv = jnp.concatenate([sm["b_k"], sm["b_v"]], axis=1)

    xb0, ag, xhc, rsc, xh1, rs1 = _conv_fwd(x, wb, lay, w_dw32, sm["b_pw1"], sm["b_dw"], sm["cg"], sm["cb"],
                                            sm["b_pw2"], lmg[0:1], lmb[0:1], tm)
    x1b, hg0, hu0, xh2, rs2, x2b, kvs = _ffn_fwd(xh1, lmg[0:1], lmb[0:1], wb, lay, 0, tm,
                                                kv=(lfg[0:1], lfb[0:1], wkv, bkv))
    q, o, lse, xh3, rs3 = _attn_fwd(xh2, lfg[0:1], lfb[0:1], x2b, kvs, wb, lay, sm["b_q"], sm["sinks"],
                                    sm["b_o"], tm)
    x3b, hg1, hu1, dz4, st4, loss = _ffn_fwd(xh3, lmg[1:2], lmb[1:2], wb, lay, 1, tm,
                                             loss=(lfg[1:2], lfb[1:2], target))

    dz4b, act1, dhg1, dhu1, dz3, st3 = _ffn_bwd(dz4, hg1, hu1, xh3, rs3, lmg[1:2], wb, lay, 1, tm)
    dz3b, dq, dkc, dkp, stq, dsinks = _attn_bwd(dz3, q, o, lse, kvs, wb, lay, sm["sinks"], tm)
    dkv, dz2, st2, dbkv = _x2_bwd(dz3, dq, dkc, dkp, xh2, rs2, lfg[0:1], wb, lay, wkv, tm)
    dz2b, act0, dhg0, dhu0, dz1, st1 = _ffn_bwd(dz2, hg0, hu0, xh1, rs1, lmg[0:1], wb, lay, 0, tm)
    dz1b, s_act, dcv, stc = _conv_bwd1(dz1, xhc, rsc, wb, lay, sm["cg"], sm["cb"], tm)
    grad_x, dh1, dwdw, db1 = _conv_bwd2(dz1, dcv, ag, wb, lay, w_dw32, tm)

    bm_f = f // 2 if (f // 2) % 128 == 0 else f
    big = {
        "pw1t": _tn_matmul(dh1, xb0, "dw_pw1", d, tk),
        "pw2": _tn_matmul(s_act, dz1b, "dw_pw2", d, tk),
        "wq": _tn_matmul(x2b, dq, "dw_q", d, tk),
        "wo": _tn_matmul(o, dz3b, "dw_o", d, tk),
        "gt0": _tn_matmul(dhg0, x1b, "dw_gate0", bm_f, tk),
        "ut0": _tn_matmul(dhu0, x1b, "dw_up0", bm_f, tk),
        "dn0": _tn_matmul(act0, dz2b, "dw_down0", bm_f, tk),
        "gt1": _tn_matmul(dhg1, x3b, "dw_gate1", bm_f, tk),
        "ut1": _tn_matmul(dhu1, x3b, "dw_up1", bm_f, tk),
        "dn1": _tn_matmul(act1, dz4b, "dw_down1", bm_f, tk),
        "wkv": _tn_matmul(x2b, dkv, "dw_kv", d, tk),
    }
    small = {
        "w_dw": dwdw[:CONV_WIDTH], "b_pw1": db1, "b_dw": stc[2:3], "cg": stc[0:1], "cb": stc[1:2],
        "b_pw2": stc[3:4], "b_k": dbkv[:, :BLOCK], "b_v": dbkv[:, BLOCK:], "b_q": stq[0:1], "sinks": dsinks,
        "b_o": stq[1:2],
        "ln_mix_g": jnp.concatenate([st1[0:1], st3[0:1]], axis=0),
        "ln_mix_b": jnp.concatenate([st1[1:2], st3[1:2]], axis=0),
        "ln_ffn_g": jnp.concatenate([st2[0:1], st4[0:1]], axis=0),
        "ln_ffn_b": jnp.concatenate([st2[1:2], st4[1:2]], axis=0),
    }
    return loss[0, 0], grad_x, big, small


SP_ROWS = 40
SP_BDW, SP_CG, SP_CB, SP_BPW2, SP_BPW1 = 32, 33, 34, 35, 36
RP_ROWS = 88


def _pack_big(lay, pw1, pw2, wq, wo, gate, up, down, wk, wv):
    d = lay.d
    parts = [pw1[0].T, pw2[0], wq[0], wo[0]]
    for l in range(DEPTH):
        parts += [gate[l].T, up[l].T, down[l]]
    parts.append(jnp.concatenate([wk, wv], axis=1).reshape(-1, d))
    return jnp.concatenate(parts, axis=0)


def _unpack_big(lay, p):
    d = lay.d

    def seg(name):
        return p[lay.off[name]:lay.off[name] + lay.n[name]]

    kvw = seg("wkv").reshape(d // N_DEV, 2 * BLOCK)
    return dict(
        pw1=seg("pw1t").T[None], pw2=seg("pw2")[None], wq=seg("wq")[None], wo=seg("wo")[None],
        gate=jnp.stack([seg(f"gt{l}").T for l in range(DEPTH)]),
        up=jnp.stack([seg(f"ut{l}").T for l in range(DEPTH)]),
        down=jnp.stack([seg(f"dn{l}") for l in range(DEPTH)]),
        wk=kvw[:, :BLOCK], wv=kvw[:, BLOCK:])


def _pack_small(w_dw, b_dw, cg, cb, b_pw2, b_pw1):
    z = jnp.zeros((1, 128), F32)
    return jnp.concatenate([w_dw[0], z, b_dw, cg, cb, b_pw2, b_pw1.reshape(2, 128), z, z], axis=0)


def _unpack_small(p):
    return dict(w_dw=p[None, :CONV_WIDTH], b_dw=p[SP_BDW:SP_BDW + 1], cg=p[SP_CG:SP_CG + 1],
                cb=p[SP_CB:SP_CB + 1], b_pw2=p[SP_BPW2:SP_BPW2 + 1], b_pw1=p[SP_BPW1:SP_BPW1 + 2].reshape(1, 256))


def _small_full(g):
    d = N_DEV * 128

    def wide(r0, n=1):
        return jnp.transpose(g[:, r0:r0 + n], (1, 0, 2)).reshape(n, d)

    return dict(w_dw=wide(0, CONV_WIDTH), b_dw=wide(SP_BDW), cg=wide(SP_CG), cb=wide(SP_CB),
                b_pw2=wide(SP_BPW2), b_pw1=g[:, SP_BPW1:SP_BPW1 + 2].reshape(1, 2 * d))


def _small_grad_blocks(sg):
    def narrow(a):
        return jnp.transpose(a.reshape(a.shape[0], N_DEV, 128), (1, 0, 2))

    z = jnp.zeros((N_DEV, 1, 128), F32)
    return jnp.concatenate([narrow(sg["w_dw"]), z, narrow(sg["b_dw"]), narrow(sg["cg"]), narrow(sg["cb"]),
                            narrow(sg["b_pw2"]), sg["b_pw1"].reshape(N_DEV, 2, 128), z, z], axis=1)


RP_FIELDS = (("ln_mix_g", 16), ("ln_mix_b", 16), ("ln_ffn_g", 16), ("ln_ffn_b", 16), ("b_q", 8), ("b_o", 8),
             ("b_k", 1), ("b_v", 1), ("sinks", 1))


def _pack_rep(vals):
    parts = []
    for name, rows in RP_FIELDS:
        a = vals[name].reshape(-1)
        if name == "sinks":
            a = jnp.concatenate([a, jnp.zeros((128 - a.shape[0],), F32)])
        parts.append(a.reshape(rows, 128))
    used = sum(r for _, r in RP_FIELDS)
    parts.append(jnp.zeros((RP_ROWS - used, 128), F32))
    return jnp.concatenate(parts, axis=0)


def _unpack_rep(p, shapes):
    out, r = {}, 0
    for name, rows in RP_FIELDS:
        a = p[r:r + rows].reshape(-1)
        n = 1
        for s in shapes[name]:
            n *= s
        out[name] = a[:n].reshape(shapes[name])
        r += rows
    return out


def kernel(x, conv_w_pw1, conv_b_pw1, conv_w_dw, conv_b_dw, conv_ln_g, conv_ln_b, conv_w_pw2, conv_b_pw2, kv_w_k, kv_b_k, kv_w_v, kv_b_v, attn_w_q, attn_b_q, attn_sinks, attn_w_o, attn_b_o, ffn_w_gate, ffn_w_up, ffn_w_down, ln_mix_g, ln_mix_b, ln_ffn_g, ln_ffn_b, loss_target, m_conv_w_pw1, m_conv_b_pw1, m_conv_w_dw, m_conv_b_dw, m_conv_ln_g, m_conv_ln_b, m_conv_w_pw2, m_conv_b_pw2, m_kv_w_k, m_kv_b_k, m_kv_w_v, m_kv_b_v, m_attn_w_q, m_attn_b_q, m_attn_sinks, m_attn_w_o, m_attn_b_o, m_ffn_w_gate, m_ffn_w_up, m_ffn_w_down, m_ln_mix_g, m_ln_mix_b, m_ln_ffn_g, m_ln_ffn_b, v_conv_w_pw1, v_conv_b_pw1, v_conv_w_dw, v_conv_b_dw, v_conv_ln_g, v_conv_ln_b, v_conv_w_pw2, v_conv_b_pw2, v_kv_w_k, v_kv_b_k, v_kv_w_v, v_kv_b_v, v_attn_w_q, v_attn_b_q, v_attn_sinks, v_attn_w_o, v_attn_b_o, v_ffn_w_gate, v_ffn_w_up, v_ffn_w_down, v_ln_mix_g, v_ln_mix_b, v_ln_ffn_g, v_ln_ffn_b):
    t, d = x.shape[1], x.shape[2]
    f = ffn_w_gate.shape[2] * N_DEV
    lay = _Layout(d, f)
    tm, tk = 256, 1024

    def big_pack(pw1, pw2, wq, wo, gate, up, down, wk, wv):
        return _pack_big(lay, pw1, pw2, wq, wo, gate, up, down, wk, wv)

    def small_pack(w_dw, b_dw, cg, cb, b_pw2, b_pw1):
        return _pack_small(w_dw, b_dw, cg, cb, b_pw2, b_pw1)

    rep_shapes = dict(ln_mix_g=ln_mix_g.shape, ln_mix_b=ln_mix_b.shape, ln_ffn_g=ln_ffn_g.shape,
                      ln_ffn_b=ln_ffn_b.shape, b_q=attn_b_q.shape, b_o=attn_b_o.shape, b_k=kv_b_k.shape,
                      b_v=kv_b_v.shape, sinks=attn_sinks.shape)

    def rep_pack(lmg, lmb, lfg, lfb, bq, bo, bk, bv, sk):
        return _pack_rep(dict(ln_mix_g=lmg, ln_mix_b=lmb, ln_ffn_g=lfg, ln_ffn_b=lfb, b_q=bq, b_o=bo,
                              b_k=bk, b_v=bv, sinks=sk))

    w_big = big_pack(conv_w_pw1, conv_w_pw2, attn_w_q, attn_w_o, ffn_w_gate, ffn_w_up, ffn_w_down, kv_w_k, kv_w_v)
    w_small = small_pack(conv_w_dw, conv_b_dw, conv_ln_g, conv_ln_b, conv_b_pw2, conv_b_pw1)
    w_rep = rep_pack(ln_mix_g, ln_mix_b, ln_ffn_g, ln_ffn_b, attn_b_q, attn_b_o, kv_b_k, kv_b_v, attn_sinks)

    wb, smg = _all_gather([w_big.astype(BF16), w_small], "gather_weights")
    wkv = wb[:, lay.off["wkv"]:lay.off["wkv"] + lay.n["wkv"], :].reshape(d, 2 * BLOCK)
    sm = _small_full(smg)
    sm.update(ln_mix_g=ln_mix_g, ln_mix_b=ln_mix_b, ln_ffn_g=ln_ffn_g, ln_ffn_b=ln_ffn_b, b_q=attn_b_q,
              b_o=attn_b_o, sinks=attn_sinks, b_k=kv_b_k.reshape(1, -1), b_v=kv_b_v.reshape(1, -1))

    loss_part, grad_x, gbig, gsmall = _local_step(x[0], loss_target[0], wb, lay, wkv, sm, tm, tk)
    loss = lax.psum(loss_part, ("x", "y", "c"))

    names = [n for n in lay.off]
    blocks = []
    for n in names:
        g = gbig[n]
        if n == "wkv":
            g = g.reshape(N_DEV, lay.n[n], d)
        else:
            g = g.reshape(N_DEV, lay.n[n], d)
        blocks.append(g)
    g8_big = _exchange_blocks(blocks, [lay.off[n] for n in names], lay.rows, "exchange_grads")
    g8_small = _exchange_blocks([_small_grad_blocks(gsmall)], [0], SP_ROWS, "exchange_small_grads")
    (g8_rep,) = _all_gather([_pack_rep(gsmall)], "gather_replicated_grads")

    m_big = big_pack(m_conv_w_pw1, m_conv_w_pw2, m_attn_w_q, m_attn_w_o, m_ffn_w_gate, m_ffn_w_up, m_ffn_w_down, m_kv_w_k, m_kv_w_v)
    v_big = big_pack(v_conv_w_pw1, v_conv_w_pw2, v_attn_w_q, v_attn_w_o, v_ffn_w_gate, v_ffn_w_up, v_ffn_w_down, v_kv_w_k, v_kv_w_v)
    m_small = small_pack(m_conv_w_dw, m_conv_b_dw, m_conv_ln_g, m_conv_ln_b, m_conv_b_pw2, m_conv_b_pw1)
    v_small = small_pack(v_conv_w_dw, v_conv_b_dw, v_conv_ln_g, v_conv_ln_b, v_conv_b_pw2, v_conv_b_pw1)
    m_rep = rep_pack(m_ln_mix_g, m_ln_mix_b, m_ln_ffn_g, m_ln_ffn_b, m_attn_b_q, m_attn_b_o, m_kv_b_k, m_kv_b_v, m_attn_sinks)
    v_rep = rep_pack(v_ln_mix_g, v_ln_mix_b, v_ln_ffn_g, v_ln_ffn_b, v_attn_b_q, v_attn_b_o, v_kv_b_k, v_kv_b_v, v_attn_sinks)

    tr = max(r for r in range(8, 129, 8) if lay.rows % r == 0)
    big_out = [_unpack_big(lay, a) for a in _adamw_sum(g8_big, w_big, m_big, v_big, "adamw_big", tr)]
    small_out = [_unpack_small(a) for a in _adamw_sum(g8_small, w_small, m_small, v_small, "adamw_small", SP_ROWS)]
    rep_out = [_unpack_rep(a, rep_shapes) for a in _adamw_sum(g8_rep, w_rep, m_rep, v_rep, "adamw_rep", RP_ROWS)]

    outs = [loss, grad_x[None]]
    for b, s, r in zip(big_out, small_out, rep_out):
        outs += [b["pw1"], s["b_pw1"], s["w_dw"], s["b_dw"], s["cg"], s["cb"], b["pw2"], s["b_pw2"],
                 b["wk"], r["b_k"], b["wv"], r["b_v"], b["wq"], r["b_q"], r["sinks"], b["wo"], r["b_o"],
                 b["gate"], b["up"], b["down"], r["ln_mix_g"], r["ln_mix_b"], r["ln_ffn_g"], r["ln_ffn_b"]]
    return tuple(outs)
```

```python
import functools

import jax
import jax.numpy as jnp
from jax import lax
from jax.experimental import pallas as pl
from jax.experimental.pallas import tpu as pltpu

F32 = jnp.float32
BF16 = jnp.bfloat16

N_DEV = 8
HEAD_DIM = 64
N_KV_HEADS = 2
BLOCK = 128
CONV_WIDTH = 31
HALO = 32
ALIBI_MAX = 8.0
DEPTH = 2
ALPHA = (2.0 * DEPTH) ** 0.25
LN_EPS = 1e-5
MASKED_DIST = 1e32
ADAM_LR = 0.001
ADAM_B1 = 0.9
ADAM_B2 = 0.999
ADAM_EPS = 1e-08
ADAM_WD = 0.01
ADAM_STEP = 10
VMEM_LIMIT = 56 * 1024 * 1024
MESH = pl.DeviceIdType.MESH


def _dot(a, b):
    return jnp.dot(a, b, preferred_element_type=F32)


def _dot_nt(a, b):
    return lax.dot_general(a, b, (((1,), (1,)), ((), ())), preferred_element_type=F32)


def _dot_tn(a, b):
    return lax.dot_general(a, b, (((0,), (0,)), ((), ())), preferred_element_type=F32)


def _sigmoid(v):
    return 1.0 / (1.0 + jnp.exp(-v))


def _ln_fwd(z):
    mu = jnp.mean(z, axis=-1, keepdims=True)
    zc = z - mu
    var = jnp.mean(zc * zc, axis=-1, keepdims=True)
    rstd = lax.rsqrt(var + LN_EPS)
    return zc * rstd, rstd


def _ln_bwd(dout, xh, rstd, g):
    dxh = dout * g
    m1 = jnp.mean(dxh, axis=-1, keepdims=True)
    m2 = jnp.mean(dxh * xh, axis=-1, keepdims=True)
    dz = rstd * (dxh - m1 - xh * m2)
    return dz, jnp.sum(dout * xh, axis=0, keepdims=True), jnp.sum(dout, axis=0, keepdims=True)


def _params(vmem=VMEM_LIMIT):
    return pltpu.CompilerParams(dimension_semantics=("arbitrary",), vmem_limit_bytes=vmem)


def _row(d):
    return pl.BlockSpec((1, d), lambda i: (0, 0))


def _tile(tm, d):
    return pl.BlockSpec((tm, d), lambda i: (i, 0))


def _fixed(r, d):
    return pl.BlockSpec((r, d), lambda i: (0, 0))


ANY = pl.BlockSpec(memory_space=pl.ANY)


class _Layout:
    def __init__(self, d, f):
        self.d, self.f = d, f
        names = [("pw1t", 2 * d // N_DEV), ("pw2", d // N_DEV), ("wq", d // N_DEV), ("wo", d // N_DEV)]
        for l in range(DEPTH):
            names += [(f"gt{l}", f // N_DEV), (f"ut{l}", f // N_DEV), (f"dn{l}", f // N_DEV)]
        names += [("wkv", (d // N_DEV) * 2 * BLOCK // d)]
        self.off, self.n = {}, {}
        r = 0
        for name, n in names:
            self.off[name], self.n[name] = r, n
            r += n
        self.rows = r


def _load_weight(wb_ref, lay, name, dst):
    pltpu.sync_copy(wb_ref.at[:, pl.ds(lay.off[name], lay.n[name]), :], dst)


def _wscratch(lay, name):
    return pltpu.VMEM((N_DEV, lay.n[name], lay.d), BF16)


def _wfull(ref):
    s = ref.shape
    return ref[...].reshape(s[0] * s[1], s[2])


def _wchunk(ref, c, nchunks):
    s = ref.shape
    per = s[0] // nchunks
    return ref[c * per:(c + 1) * per].reshape(per * s[1], s[2])


CONV_RB = 32
CONV_LC = 256
CONV_WIN = CONV_RB + HALO + 8


def _shifted(win, r):
    return win if r == 0 else pltpu.roll(win, win.shape[0] - r, 0)


def _conv_fwd(x, wb, lay, w_dw, b_pw1, b_dw, cg, cb, b_pw2, lg, lb, tm):
    t, d = x.shape
    nsteps = t // tm

    def body(x_ref, xh_ref, wb_ref, wdw_ref, b1_ref, bdw_ref, cg_ref, cb_ref, b2_ref, lg_ref, lb_ref,
             xb_ref, ag_ref, xhc_ref, rsc_ref, xh1_ref, rs1_ref, w1_s, w2_s, ubuf, cv_s):
        i = pl.program_id(0)

        @pl.when(i == 0)
        def _():
            _load_weight(wb_ref, lay, "pw1t", w1_s)
            _load_weight(wb_ref, lay, "pw2", w2_s)
            ubuf[pl.ds(HALO + tm, 8), :] = jnp.zeros((8, d), F32)

        w1 = _wfull(w1_s)
        xv = x_ref[...]
        xb = xv.astype(BF16)
        xb_ref[...] = xb
        h = _dot_nt(xb, w1) + b1_ref[...]
        ag_ref[...] = h.astype(BF16)
        ubuf[pl.ds(HALO, tm), :] = h[:, :d] * _sigmoid(h[:, d:])
        hh = _dot_nt(xh_ref[...].astype(BF16), w1) + b1_ref[...]
        uh = hh[:, :d] * _sigmoid(hh[:, d:])
        ubuf[pl.ds(0, HALO), :] = jnp.where(i > 0, uh, 0.0)

        def conv_block(rb, carry):
            t0 = pl.multiple_of(rb * CONV_RB, CONV_RB)
            for lc in range(d // CONV_LC):
                lanes = slice(lc * CONV_LC, (lc + 1) * CONV_LC)
                win = ubuf[pl.ds(t0, CONV_WIN), lanes]
                acc = jnp.zeros((CONV_RB, CONV_LC), F32)
                for r in range(8):
                    wr = _shifted(win, r)
                    for k in range(CONV_WIDTH):
                        s = HALO - (CONV_WIDTH - 1) + k
                        if s % 8 == r:
                            q = 8 * (s // 8)
                            acc = acc + wr[q:q + CONV_RB] * wdw_ref[k:k + 1, lanes]
                cv_s[pl.ds(t0, CONV_RB), lanes] = acc
            return carry

        lax.fori_loop(0, tm // CONV_RB, conv_block, 0)
        cv = cv_s[...] + bdw_ref[...]
        xhc, rsc = _ln_fwd(cv)
        xhc_ref[...] = xhc
        rsc_ref[...] = rsc
        n = xhc * cg_ref[...] + cb_ref[...]
        s_act = n * _sigmoid(n)
        m = _dot(s_act.astype(BF16), _wfull(w2_s)) + b2_ref[...]
        xh1, rs1 = _ln_fwd(ALPHA * xv + m)
        xh1_ref[...] = xh1
        rs1_ref[...] = rs1

    hb = tm // HALO
    return pl.pallas_call(
        body, name="conv_fwd", grid=(nsteps,),
        in_specs=[_tile(tm, d), pl.BlockSpec((HALO, d), lambda i: (jnp.maximum(i * hb - 1, 0), 0)), ANY,
                  _fixed(HALO, d), _row(2 * d), _row(d), _row(d), _row(d), _row(d), _row(d), _row(d)],
        out_specs=[_tile(tm, d), _tile(tm, 2 * d), _tile(tm, d), _tile(tm, 1), _tile(tm, d), _tile(tm, 1)],
        out_shape=[jax.ShapeDtypeStruct((t, d), BF16), jax.ShapeDtypeStruct((t, 2 * d), BF16),
                   jax.ShapeDtypeStruct((t, d), F32), jax.ShapeDtypeStruct((t, 1), F32),
                   jax.ShapeDtypeStruct((t, d), F32), jax.ShapeDtypeStruct((t, 1), F32)],
        scratch_shapes=[_wscratch(lay, "pw1t"), _wscratch(lay, "pw2"),
                        pltpu.VMEM((HALO + tm + 8, d), F32), pltpu.VMEM((tm, d), F32)],
        compiler_params=_params(),
    )(x, x, wb, w_dw, b_pw1, b_dw, cg, cb, b_pw2, lg, lb)


def _conv_bwd1(dz1, xhc, rsc, wb, lay, cg, cb, tm):
    t, d = dz1.shape

    def body(dz_ref, xhc_ref, rsc_ref, wb_ref, cg_ref, cb_ref, dzb_ref, s_ref, dcv_ref, st_ref, w2_s):
        i = pl.program_id(0)

        @pl.when(i == 0)
        def _():
            _load_weight(wb_ref, lay, "pw2", w2_s)
            st_ref[...] = jnp.zeros(st_ref.shape, F32)

        dz = dz_ref[...]
        dzb = dz.astype(BF16)
        dzb_ref[...] = dzb
        xhc_v = xhc_ref[...]
        n = xhc_v * cg_ref[...] + cb_ref[...]
        sg = _sigmoid(n)
        s_ref[...] = (n * sg).astype(BF16)
        ds = _dot_nt(dzb, _wfull(w2_s))
        dn = ds * (sg * (1.0 + n * (1.0 - sg)))
        dcv, dg, db = _ln_bwd(dn, xhc_v, rsc_ref[...], cg_ref[...])
        dcv_ref[...] = dcv
        st_ref[0:1, :] += dg
        st_ref[1:2, :] += db
        st_ref[2:3, :] += jnp.sum(dcv, axis=0, keepdims=True)
        st_ref[3:4, :] += jnp.sum(dz, axis=0, keepdims=True)

    return pl.pallas_call(
        body, name="conv_bwd1", grid=(t // tm,),
        in_specs=[_tile(tm, d), _tile(tm, d), _tile(tm, 1), ANY, _row(d), _row(d)],
        out_specs=[_tile(tm, d), _tile(tm, d), _tile(tm, d), _fixed(8, d)],
        out_shape=[jax.ShapeDtypeStruct((t, d), BF16), jax.ShapeDtypeStruct((t, d), BF16),
                   jax.ShapeDtypeStruct((t, d), F32), jax.ShapeDtypeStruct((8, d), F32)],
        scratch_shapes=[_wscratch(lay, "pw2")],
        compiler_params=_params(),
    )(dz1, xhc, rsc, wb, cg, cb)


def _conv_bwd2(dz1, dcv, ag, wb, lay, w_dw, tm):
    t, d = dz1.shape
    nsteps = t // tm

    def body(dz_ref, dcv_ref, dcvn_ref, ag_ref, agp_ref, wb_ref, wdw_ref,
             gx_ref, dh_ref, dw_ref, db1_ref, w1_s, ubuf, dbuf, du_s, dwacc):
        i = pl.program_id(0)

        @pl.when(i == 0)
        def _():
            _load_weight(wb_ref, lay, "pw1t", w1_s)
            ubuf[pl.ds(HALO + tm, 8), :] = jnp.zeros((8, d), F32)
            dbuf[pl.ds(HALO + tm, 8), :] = jnp.zeros((8, d), F32)
            dwacc[...] = jnp.zeros(dwacc.shape, F32)
            db1_ref[...] = jnp.zeros(db1_ref.shape, F32)

        ag_v = ag_ref[...].astype(F32)
        a, g = ag_v[:, :d], ag_v[:, d:]
        sg = _sigmoid(g)
        ubuf[pl.ds(HALO, tm), :] = a * sg
        agp = agp_ref[...].astype(F32)
        ubuf[pl.ds(0, HALO), :] = jnp.where(i > 0, agp[:, :d] * _sigmoid(agp[:, d:]), 0.0)
        dbuf[pl.ds(0, tm), :] = dcv_ref[...]
        dbuf[pl.ds(tm, HALO), :] = jnp.where(i < nsteps - 1, dcvn_ref[...], 0.0)

        def conv_block(rb, carry):
            t0 = pl.multiple_of(rb * CONV_RB, CONV_RB)
            for lc in range(d // CONV_LC):
                lanes = slice(lc * CONV_LC, (lc + 1) * CONV_LC)
                dwin = dbuf[pl.ds(t0, CONV_WIN), lanes]
                uwin = ubuf[pl.ds(t0, CONV_WIN), lanes]
                dcur = dwin[0:CONV_RB]
                acc = jnp.zeros((CONV_RB, CONV_LC), F32)
                for r in range(8):
                    dr = _shifted(dwin, r)
                    ur = _shifted(uwin, r)
                    for k in range(CONV_WIDTH):
                        sd = CONV_WIDTH - 1 - k
                        if sd % 8 == r:
                            q = 8 * (sd // 8)
                            acc = acc + dr[q:q + CONV_RB] * wdw_ref[k:k + 1, lanes]
                        su = HALO - (CONV_WIDTH - 1) + k
                        if su % 8 == r:
                            q = 8 * (su // 8)
                            prod = dcur * ur[q:q + CONV_RB]
                            part = prod[0:8]
                            for j in range(1, CONV_RB // 8):
                                part = part + prod[8 * j:8 * j + 8]
                            dwacc[k, :, lanes] += part
                du_s[pl.ds(t0, CONV_RB), lanes] = acc
            return carry

        lax.fori_loop(0, tm // CONV_RB, conv_block, 0)
        du = du_s[...]
        da = du * sg
        dg = du * a * sg * (1.0 - sg)
        dh = jnp.concatenate([da, dg], axis=1)
        dhb = dh.astype(BF16)
        dh_ref[...] = dhb
        db1_ref[...] += jnp.sum(dh, axis=0, keepdims=True)
        gx_ref[...] = ALPHA * dz_ref[...] + _dot(dhb, _wfull(w1_s))

        @pl.when(i == nsteps - 1)
        def _():
            dw_ref[...] = jnp.sum(dwacc[...], axis=1)

    hb = tm // HALO
    last = t // HALO - 1
    return pl.pallas_call(
        body, name="conv_bwd2", grid=(nsteps,),
        in_specs=[_tile(tm, d), _tile(tm, d),
                  pl.BlockSpec((HALO, d), lambda i: (jnp.minimum((i + 1) * hb, last), 0)),
                  _tile(tm, 2 * d),
                  pl.BlockSpec((HALO, 2 * d), lambda i: (jnp.maximum(i * hb - 1, 0), 0)),
                  ANY, _fixed(HALO, d)],
        out_specs=[_tile(tm, d), _tile(tm, 2 * d), _fixed(HALO, d), _row(2 * d)],
        out_shape=[jax.ShapeDtypeStruct((t, d), F32), jax.ShapeDtypeStruct((t, 2 * d), BF16),
                   jax.ShapeDtypeStruct((HALO, d), F32), jax.ShapeDtypeStruct((1, 2 * d), F32)],
        scratch_shapes=[_wscratch(lay, "pw1t"), pltpu.VMEM((HALO + tm + 8, d), F32),
                        pltpu.VMEM((HALO + tm + 8, d), F32), pltpu.VMEM((tm, d), F32),
                        pltpu.VMEM((HALO, 8, d), F32)],
        compiler_params=_params(),
    )(dz1, dcv, dcv, ag, ag, wb, w_dw)


FFN_CHUNKS = 2


def _ffn_fwd(xh_in, g_in, b_in, wb, lay, layer, tm, *, kv=None, loss=None):
    t, d = xh_in.shape
    f = lay.f
    fc = f // FFN_CHUNKS
    names = (f"gt{layer}", f"ut{layer}", f"dn{layer}")

    def body(*refs):
        xh_ref, gi_ref, bi_ref, wb_ref = refs[:4]
        pos = 4
        if kv is not None:
            go_ref, bo_ref, wkv_ref, bkv_ref = refs[pos:pos + 4]
            pos += 4
        if loss is not None:
            go_ref, bo_ref, tgt_ref = refs[pos:pos + 3]
            pos += 3
        xb_ref, hg_ref, hu_ref = refs[pos:pos + 3]
        pos += 3
        if kv is not None:
            xho_ref, rso_ref, xob_ref, kv_ref = refs[pos:pos + 4]
            pos += 4
        if loss is not None:
            dz_ref, st_ref, loss_ref = refs[pos:pos + 3]
            pos += 3
        gt_s, ut_s, dn_s = refs[pos:pos + 3]
        i = pl.program_id(0)

        @pl.when(i == 0)
        def _():
            for name, dst in zip(names, (gt_s, ut_s, dn_s)):
                _load_weight(wb_ref, lay, name, dst)
            if loss is not None:
                st_ref[...] = jnp.zeros(st_ref.shape, F32)
                loss_ref[...] = jnp.zeros(loss_ref.shape, F32)

        xin = xh_ref[...] * gi_ref[...] + bi_ref[...]
        xb = xin.astype(BF16)
        xb_ref[...] = xb
        fo = jnp.zeros((tm, d), F32)
        for c in range(FFN_CHUNKS):
            rows = slice(c * fc, (c + 1) * fc)
            hg = _dot_nt(xb, _wchunk(gt_s, c, FFN_CHUNKS))
            hu = _dot_nt(xb, _wchunk(ut_s, c, FFN_CHUNKS))
            hg_ref[:, rows] = hg.astype(BF16)
            hu_ref[:, rows] = hu.astype(BF16)
            act = hg * _sigmoid(hg) * hu
            fo = fo + _dot(act.astype(BF16), _wchunk(dn_s, c, FFN_CHUNKS))
        xho, rso = _ln_fwd(ALPHA * xin + fo)
        if kv is not None:
            xho_ref[...] = xho
            rso_ref[...] = rso
            xob = (xho * go_ref[...] + bo_ref[...]).astype(BF16)
            xob_ref[...] = xob
            kv_ref[...] = (_dot(xob, wkv_ref[...]) + bkv_ref[...]).astype(BF16)
        if loss is not None:
            diff = xho * go_ref[...] + bo_ref[...] - tgt_ref[...]
            loss_ref[...] += (0.5 / d) * jnp.sum(diff * diff)
            dz, dg, db = _ln_bwd(diff * (1.0 / d), xho, rso, go_ref[...])
            dz_ref[...] = dz
            st_ref[0:1, :] += dg
            st_ref[1:2, :] += db

    in_specs = [_tile(tm, d), _row(d), _row(d), ANY]
    args = [xh_in, g_in, b_in, wb]
    out_specs = [_tile(tm, d), _tile(tm, f), _tile(tm, f)]
    out_shape = [jax.ShapeDtypeStruct((t, d), BF16), jax.ShapeDtypeStruct((t, f), BF16),
                 jax.ShapeDtypeStruct((t, f), BF16)]
    if kv is not None:
        in_specs += [_row(d), _row(d), _fixed(d, 2 * BLOCK), _row(2 * BLOCK)]
        args += list(kv)
        out_specs += [_tile(tm, d), _tile(tm, 1), _tile(tm, d), _tile(tm, 2 * BLOCK)]
        out_shape += [jax.ShapeDtypeStruct((t, d), F32), jax.ShapeDtypeStruct((t, 1), F32),
                      jax.ShapeDtypeStruct((t, d), BF16), jax.ShapeDtypeStruct((t, 2 * BLOCK), BF16)]
    if loss is not None:
        in_specs += [_row(d), _row(d), _tile(tm, d)]
        args += list(loss)
        out_specs += [_tile(tm, d), _fixed(8, d), _fixed(8, 128)]
        out_shape += [jax.ShapeDtypeStruct((t, d), F32), jax.ShapeDtypeStruct((8, d), F32),
                      jax.ShapeDtypeStruct((8, 128), F32)]
    return pl.pallas_call(
        body, name=f"ffn_fwd{layer}", grid=(t // tm,), in_specs=in_specs, out_specs=out_specs,
        out_shape=out_shape, scratch_shapes=[_wscratch(lay, n) for n in names],
        compiler_params=_params(),
    )(*args)


def _ffn_bwd(dz, hg, hu, xh_in, rs_in, g_in, wb, lay, layer, tm):
    t, d = dz.shape
    f = lay.f
    fc = f // FFN_CHUNKS
    names = (f"gt{layer}", f"ut{layer}", f"dn{layer}")

    def body(dz_ref, hg_ref, hu_ref, xh_ref, rs_ref, gi_ref, wb_ref,
             dzb_ref, act_ref, dhg_ref, dhu_ref, dzp_ref, st_ref, gt_s, ut_s, dn_s):
        i = pl.program_id(0)

        @pl.when(i == 0)
        def _():
            for name, dst in zip(names, (gt_s, ut_s, dn_s)):
                _load_weight(wb_ref, lay, name, dst)
            st_ref[...] = jnp.zeros(st_ref.shape, F32)

        dzv = dz_ref[...]
        dzb = dzv.astype(BF16)
        dzb_ref[...] = dzb
        dx = ALPHA * dzv
        for c in range(FFN_CHUNKS):
            rows = slice(c * fc, (c + 1) * fc)
            hg_v = hg_ref[:, rows].astype(F32)
            hu_v = hu_ref[:, rows].astype(F32)
            sg = _sigmoid(hg_v)
            silu = hg_v * sg
            act_ref[:, rows] = (silu * hu_v).astype(BF16)
            dact = _dot_nt(dzb, _wchunk(dn_s, c, FFN_CHUNKS))
            dhu = (dact * silu).astype(BF16)
            dhg = (dact * hu_v * (sg * (1.0 + hg_v * (1.0 - sg)))).astype(BF16)
            dhu_ref[:, rows] = dhu
            dhg_ref[:, rows] = dhg
            dx = dx + _dot(dhg, _wchunk(gt_s, c, FFN_CHUNKS)) + _dot(dhu, _wchunk(ut_s, c, FFN_CHUNKS))
        dzp, dg, db = _ln_bwd(dx, xh_ref[...], rs_ref[...], gi_ref[...])
        dzp_ref[...] = dzp
        st_ref[0:1, :] += dg
        st_ref[1:2, :] += db

    return pl.pallas_call(
        body, name=f"ffn_bwd{layer}", grid=(t // tm,),
        in_specs=[_tile(tm, d), _tile(tm, f), _tile(tm, f), _tile(tm, d), _tile(tm, 1), _row(d), ANY],
        out_specs=[_tile(tm, d), _tile(tm, f), _tile(tm, f), _tile(tm, f), _tile(tm, d), _fixed(8, d)],
        out_shape=[jax.ShapeDtypeStruct((t, d), BF16), jax.ShapeDtypeStruct((t, f), BF16),
                   jax.ShapeDtypeStruct((t, f), BF16), jax.ShapeDtypeStruct((t, f), BF16),
                   jax.ShapeDtypeStruct((t, d), F32), jax.ShapeDtypeStruct((8, d), F32)],
        scratch_shapes=[_wscratch(lay, n) for n in names],
        compiler_params=_params(),
    )(dz, hg, hu, xh_in, rs_in, g_in, wb)


def _alibi_slope(h, nq):
    return 2.0 ** (-ALIBI_MAX * (h + 1) / nq)


def _fill_alibi_bias(bias_s, nq):
    qi = lax.broadcasted_iota(jnp.int32, (BLOCK, 2 * BLOCK), 0)
    kj = lax.broadcasted_iota(jnp.int32, (BLOCK, 2 * BLOCK), 1)
    delta = qi + BLOCK - kj
    valid = (delta >= 0) & (delta < BLOCK)
    dist = jnp.where(valid, delta.astype(F32), MASKED_DIST)
    dist_first = jnp.where(kj >= BLOCK, dist, MASKED_DIST)
    for h in range(nq):
        bias_s[0, h] = _alibi_slope(h, nq) * dist
        bias_s[1, h] = _alibi_slope(h, nq) * dist_first


def _padded_kv(kvb, kvh):
    lane = lax.broadcasted_iota(jnp.int32, (2 * BLOCK, BLOCK), 1)
    mine = (lane < HEAD_DIM) if kvh == 0 else (lane >= HEAD_DIM)
    out = []
    for sec in (kvb[:, :BLOCK], kvb[:, BLOCK:]):
        m = jnp.where(mine, sec.astype(F32), 0.0)
        sw = pltpu.roll(m, HEAD_DIM, 1)
        pair = (m, sw) if kvh == 0 else (sw, m)
        out.append(tuple(p.astype(BF16) for p in pair))
    return out


def _attn_fwd(xh_in, g_in, b_in, x_in_b, kvs, wb, lay, bq, sinks, bo, tm):
    t, d = xh_in.shape
    nq = d // HEAD_DIM
    pairs_per_kv = (d // BLOCK) // N_KV_HEADS
    nbt = tm // BLOCK
    scale = HEAD_DIM ** -0.5

    def body(xh_ref, gi_ref, bi_ref, xb_ref, kv_ref, kvp_ref, wb_ref, bq_ref, sk_ref, bo_ref,
             q_ref, o_ref, lse_ref, xho_ref, rso_ref, wq_s, wo_s, kvall, q_s, o_s, bias_s):
        i = pl.program_id(0)

        @pl.when(i == 0)
        def _():
            _load_weight(wb_ref, lay, "wq", wq_s)
            _load_weight(wb_ref, lay, "wo", wo_s)
            _fill_alibi_bias(bias_s, nq)

        qv = ((_dot(xb_ref[...], _wfull(wq_s)) + bq_ref[...]) * scale).astype(BF16)
        q_s[...] = qv
        q_ref[...] = qv
        kvall[pl.ds(0, BLOCK), :] = kvp_ref[...]
        kvall[pl.ds(BLOCK, tm), :] = kv_ref[...]
        lane = lax.broadcasted_iota(jnp.int32, (BLOCK, BLOCK), 1)
        ones = jnp.ones((2 * BLOCK, BLOCK), BF16)

        def block(j, carry):
            r0 = pl.multiple_of(j * BLOCK, BLOCK)
            rows = pl.ds(r0, BLOCK)
            kvb = kvall[pl.ds(r0, 2 * BLOCK), :]
            first = (i * nbt + j == 0).astype(jnp.int32)
            pads = [_padded_kv(kvb, kvh) for kvh in range(N_KV_HEADS)]
            scores = []
            for a in range(d // BLOCK):
                kpad = pads[a // pairs_per_kv][0]
                qp = q_s[rows, a * BLOCK:(a + 1) * BLOCK]
                for e in range(2):
                    scores.append(_dot_nt(qp, kpad[e]) - bias_s[first, 2 * a + e])
            probs, inv = [], []
            lse_t = jnp.zeros((BLOCK, BLOCK), F32)
            for h in range(nq):
                sink = sk_ref[:, h:h + 1]
                m = jnp.maximum(jnp.max(scores[h], axis=-1, keepdims=True), sink)
                p = jnp.exp(scores[h] - m).astype(BF16)
                l = _dot(p, ones) + jnp.exp(sink - m)
                lse_t = jnp.where(lane == h, m + jnp.log(l), lse_t)
                probs.append(p)
                inv.append(1.0 / l)
            lse_ref[rows, :] = lse_t
            for a in range(d // BLOCK):
                vpad = pads[a // pairs_per_kv][1]
                opair = (_dot(probs[2 * a], vpad[0]) * inv[2 * a]
                         + _dot(probs[2 * a + 1], vpad[1]) * inv[2 * a + 1])
                o_s[rows, a * BLOCK:(a + 1) * BLOCK] = opair.astype(BF16)
            return carry

        lax.fori_loop(0, nbt, block, 0)
        ov = o_s[...]
        o_ref[...] = ov
        xin = xh_ref[...] * gi_ref[...] + bi_ref[...]
        xho, rso = _ln_fwd(ALPHA * xin + _dot(ov, _wfull(wo_s)) + bo_ref[...])
        xho_ref[...] = xho
        rso_ref[...] = rso

    return pl.pallas_call(
        body, name="attn_fwd", grid=(t // tm,),
        in_specs=[_tile(tm, d), _row(d), _row(d), _tile(tm, d), _tile(tm, 2 * BLOCK),
                  pl.BlockSpec((BLOCK, 2 * BLOCK), lambda i: (jnp.maximum(i * nbt - 1, 0), 0)),
                  ANY, _row(d), _row(nq), _row(d)],
        out_specs=[_tile(tm, d), _tile(tm, d), _tile(tm, BLOCK), _tile(tm, d), _tile(tm, 1)],
        out_shape=[jax.ShapeDtypeStruct((t, d), BF16), jax.ShapeDtypeStruct((t, d), BF16),
                   jax.ShapeDtypeStruct((t, BLOCK), F32), jax.ShapeDtypeStruct((t, d), F32),
                   jax.ShapeDtypeStruct((t, 1), F32)],
        scratch_shapes=[_wscratch(lay, "wq"), _wscratch(lay, "wo"),
                        pltpu.VMEM((BLOCK + tm, 2 * BLOCK), BF16), pltpu.VMEM((tm, d), BF16),
                        pltpu.VMEM((tm, d), BF16), pltpu.VMEM((2, nq, BLOCK, 2 * BLOCK), F32)],
        compiler_params=_params(),
    )(xh_in, g_in, b_in, x_in_b, kvs, kvs, wb, bq, sinks, bo)


def _attn_bwd(dz, q, o, lse, kvs, wb, lay, sinks, tm):
    t, d = dz.shape
    nq = d // HEAD_DIM
    pairs_per_kv = (d // BLOCK) // N_KV_HEADS
    nbt = tm // BLOCK
    scale = HEAD_DIM ** -0.5

    def body(dz_ref, q_ref, o_ref, lse_ref, kv_ref, kvp_ref, wb_ref, sk_ref,
             dzb_ref, dq_ref, dkc_ref, dkp_ref, st_ref, dsk_ref, wo_s, kvall, do_s, dq_s, bias_s):
        i = pl.program_id(0)

        @pl.when(i == 0)
        def _():
            _load_weight(wb_ref, lay, "wo", wo_s)
            _fill_alibi_bias(bias_s, nq)
            st_ref[...] = jnp.zeros(st_ref.shape, F32)
            dsk_ref[...] = jnp.zeros(dsk_ref.shape, F32)

        dzv = dz_ref[...]
        dzb = dzv.astype(BF16)
        dzb_ref[...] = dzb
        do_s[...] = _dot_nt(dzb, _wfull(wo_s))
        kvall[pl.ds(0, BLOCK), :] = kvp_ref[...]
        kvall[pl.ds(BLOCK, tm), :] = kv_ref[...]
        lane = lax.broadcasted_iota(jnp.int32, (BLOCK, BLOCK), 1)
        lane1 = lax.broadcasted_iota(jnp.int32, (1, BLOCK), 1)
        lane2 = lax.broadcasted_iota(jnp.int32, (2 * BLOCK, BLOCK), 1)
        halves = (lane < HEAD_DIM, lane >= HEAD_DIM)

        def block(j, carry):
            r0 = pl.multiple_of(j * BLOCK, BLOCK)
            rows = pl.ds(r0, BLOCK)
            kvb = kvall[pl.ds(r0, 2 * BLOCK), :]
            first = (i * nbt + j == 0).astype(jnp.int32)
            pads = [_padded_kv(kvb, kvh) for kvh in range(N_KV_HEADS)]
            scores, dps, dhs, qms, doms = [], [], [], [], []
            for a in range(d // BLOCK):
                kpad, vpad = pads[a // pairs_per_kv]
                cols = slice(a * BLOCK, (a + 1) * BLOCK)
                qp = q_ref[rows, cols]
                dop = do_s[rows, cols]
                dopb = dop.astype(BF16)
                prod = dop * o_ref[rows, cols].astype(F32)
                for e in range(2):
                    scores.append(_dot_nt(qp, kpad[e]) - bias_s[first, 2 * a + e])
                    dps.append(_dot_nt(dopb, vpad[e]))
                    dhs.append(jnp.sum(jnp.where(halves[e], prod, 0.0), axis=-1, keepdims=True))
                    qms.append(jnp.where(halves[e], qp, jnp.zeros_like(qp)))
                    doms.append(jnp.where(halves[e], dopb, jnp.zeros_like(dopb)))
            dss, pbs = [], []
            dsk_t = jnp.zeros((1, BLOCK), F32)
            for h in range(nq):
                lse_h = lse_ref[rows, h:h + 1]
                p = jnp.exp(scores[h] - lse_h)
                dss.append((p * (dps[h] - dhs[h])).astype(BF16))
                pbs.append(p.astype(BF16))
                dsink = -jnp.sum(jnp.exp(sk_ref[:, h:h + 1] - lse_h) * dhs[h], axis=0, keepdims=True)
                dsk_t = jnp.where(lane1 == h, dsink, dsk_t)
            dsk_ref[...] += dsk_t
            dsecs = []
            for kvh in range(N_KV_HEADS):
                kpad = pads[kvh][0]
                dk_acc = jnp.zeros((2 * BLOCK, BLOCK), F32)
                dv_acc = jnp.zeros((2 * BLOCK, BLOCK), F32)
                for a in range(kvh * pairs_per_kv, (kvh + 1) * pairs_per_kv):
                    dqp = _dot(dss[2 * a], kpad[0]) + _dot(dss[2 * a + 1], kpad[1])
                    dq_s[rows, a * BLOCK:(a + 1) * BLOCK] = dqp * scale
                    for e in range(2):
                        h = 2 * a + e
                        dk_acc = dk_acc + _dot_tn(dss[h], qms[h])
                        dv_acc = dv_acc + _dot_tn(pbs[h], doms[h])
                dsecs.append((dk_acc + pltpu.roll(dk_acc, HEAD_DIM, 1), dv_acc + pltpu.roll(dv_acc, HEAD_DIM, 1)))
            lo = lane2 < HEAD_DIM
            dkv = jnp.concatenate([jnp.where(lo, dsecs[0][0], dsecs[1][0]),
                                   jnp.where(lo, dsecs[0][1], dsecs[1][1])], axis=1)
            dkp_ref[rows, :] = dkv[:BLOCK]
            dkc_ref[rows, :] = dkv[BLOCK:]
            return carry

        lax.fori_loop(0, nbt, block, 0)
        dqv = dq_s[...]
        dq_ref[...] = dqv.astype(BF16)
        st_ref[0:1, :] += jnp.sum(dqv, axis=0, keepdims=True)
        st_ref[1:2, :] += jnp.sum(dzv, axis=0, keepdims=True)

    return pl.pallas_call(
        body, name="attn_bwd", grid=(t // tm,),
        in_specs=[_tile(tm, d), _tile(tm, d), _tile(tm, d), _tile(tm, BLOCK), _tile(tm, 2 * BLOCK),
                  pl.BlockSpec((BLOCK, 2 * BLOCK), lambda i: (jnp.maximum(i * nbt - 1, 0), 0)),
                  ANY, _row(nq)],
        out_specs=[_tile(tm, d), _tile(tm, d), _tile(tm, 2 * BLOCK), _tile(tm, 2 * BLOCK),
                   _fixed(8, d), _row(BLOCK)],
        out_shape=[jax.ShapeDtypeStruct((t, d), BF16), jax.ShapeDtypeStruct((t, d), BF16),
                   jax.ShapeDtypeStruct((t, 2 * BLOCK), F32), jax.ShapeDtypeStruct((t, 2 * BLOCK), F32),
                   jax.ShapeDtypeStruct((8, d), F32), jax.ShapeDtypeStruct((1, BLOCK), F32)],
        scratch_shapes=[_wscratch(lay, "wo"), pltpu.VMEM((BLOCK + tm, 2 * BLOCK), BF16),
                        pltpu.VMEM((tm, d), F32), pltpu.VMEM((tm, d), F32),
                        pltpu.VMEM((2, nq, BLOCK, 2 * BLOCK), F32)],
        compiler_params=_params(),
    )(dz, q, o, lse, kvs, kvs, wb, sinks)


def _x2_bwd(dz, dq, dkc, dkp, xh_in, rs_in, g_in, wb, lay, wkv, tm):
    t, d = dz.shape
    nbt = tm // BLOCK
    nsteps = t // tm
    scale = HEAD_DIM ** -0.5

    def body(dz_ref, dq_ref, dkc_ref, dkp_ref, dkn_ref, xh_ref, rs_ref, gi_ref, wb_ref, wkv_ref,
             dkv_ref, dzp_ref, st_ref, dbkv_ref, wq_s):
        i = pl.program_id(0)

        @pl.when(i == 0)
        def _():
            _load_weight(wb_ref, lay, "wq", wq_s)
            st_ref[...] = jnp.zeros(st_ref.shape, F32)
            dbkv_ref[...] = jnp.zeros(dbkv_ref.shape, F32)

        nxt = jnp.where(i < nsteps - 1, dkn_ref[...], 0.0)
        if nbt > 1:
            shifted = jnp.concatenate([dkp_ref[pl.ds(BLOCK, tm - BLOCK), :], nxt], axis=0)
        else:
            shifted = nxt
        dkv = dkc_ref[...] + shifted
        dkvb = dkv.astype(BF16)
        dkv_ref[...] = dkvb
        dbkv_ref[...] += jnp.sum(dkv, axis=0, keepdims=True)
        dx = ALPHA * dz_ref[...] + _dot_nt(dq_ref[...], _wfull(wq_s)) + _dot_nt(dkvb, wkv_ref[...])
        dzp, dg, db = _ln_bwd(dx, xh_ref[...], rs_ref[...], gi_ref[...])
        dzp_ref[...] = dzp
        st_ref[0:1, :] += dg
        st_ref[1:2, :] += db

    del scale
    last = t // BLOCK - 1
    return pl.pallas_call(
        body, name="x2_bwd", grid=(nsteps,),
        in_specs=[_tile(tm, d), _tile(tm, d), _tile(tm, 2 * BLOCK), _tile(tm, 2 * BLOCK),
                  pl.BlockSpec((BLOCK, 2 * BLOCK), lambda i: (jnp.minimum((i + 1) * nbt, last), 0)),
                  _tile(tm, d), _tile(tm, 1), _row(d), ANY, _fixed(d, 2 * BLOCK)],
        out_specs=[_tile(tm, 2 * BLOCK), _tile(tm, d), _fixed(8, d), _row(2 * BLOCK)],
        out_shape=[jax.ShapeDtypeStruct((t, 2 * BLOCK), BF16), jax.ShapeDtypeStruct((t, d), F32),
                   jax.ShapeDtypeStruct((8, d), F32), jax.ShapeDtypeStruct((1, 2 * BLOCK), F32)],
        scratch_shapes=[_wscratch(lay, "wq")],
        compiler_params=_params(),
    )(dz, dq, dkc, dkp, dkp, xh_in, rs_in, g_in, wb, wkv)


def _tn_matmul(a, b, name, bm, tk):
    t, m = a.shape
    n = b.shape[1]
    ksteps = t // tk

    def body(a_ref, b_ref, o_ref):
        k = pl.program_id(1)
        part = _dot_tn(a_ref[...], b_ref[...])

        @pl.when(k == 0)
        def _():
            o_ref[...] = part

        @pl.when(k > 0)
        def _():
            o_ref[...] += part

    return pl.pallas_call(
        body, name=name, grid=(m // bm, ksteps),
        in_specs=[pl.BlockSpec((tk, bm), lambda j, k: (k, j)), pl.BlockSpec((tk, n), lambda j, k: (k, 0))],
        out_specs=pl.BlockSpec((bm, n), lambda j, k: (j, 0)),
        out_shape=jax.ShapeDtypeStruct((m, n), F32),
        compiler_params=pltpu.CompilerParams(dimension_semantics=("arbitrary", "arbitrary"),
                                             vmem_limit_bytes=VMEM_LIMIT),
    )(a, b)


def _me():
    return lax.axis_index("x"), lax.axis_index("y"), lax.axis_index("c")


def _all_gather(arrays, name):
    n = len(arrays)

    def body(*refs):
        ins, outs = refs[:n], refs[n:2 * n]
        send_sems, recv_sems, local_sems = refs[2 * n:]
        x, y, c = _me()
        me, sibling = (x, y, c), (x, y, 1 - c)
        chips = [(1 - x, y), (x, 1 - y), (1 - x, 1 - y)]

        def slot(ref, dev):
            return ref.at[4 * dev[0] + 2 * dev[1] + dev[2]]

        def copy(a, k, block, to, src=None):
            return pltpu.make_async_remote_copy(
                src_ref=slot(outs[a], block) if src is None else src, dst_ref=slot(outs[a], block),
                send_sem=send_sems.at[a, k], recv_sem=recv_sems.at[a, k], device_id=to, device_id_type=MESH)

        mine = [pltpu.make_async_copy(ins[a], slot(outs[a], me), local_sems.at[a]) for a in range(n)]
        for cp in mine:
            cp.start()
        first = []
        for a in range(n):
            first.append(copy(a, 0, me, sibling, src=ins[a]))
            first += [copy(a, 1 + j, me, (*chip, c), src=ins[a]) for j, chip in enumerate(chips)]
        for cp in first:
            cp.start()
        passed = []
        for a in range(n):
            for j, chip in enumerate(chips):
                copy(a, 1 + j, (*chip, c), me).wait_recv()
                cp = copy(a, 4 + j, (*chip, c), sibling)
                cp.start()
                passed.append(cp)
        for a in range(n):
            copy(a, 0, sibling, me).wait_recv()
            for j, chip in enumerate(chips):
                copy(a, 4 + j, (*chip, 1 - c), me).wait_recv()
        for cp in first + passed:
            cp.wait_send()
        for cp in mine:
            cp.wait()

    return pl.pallas_call(
        body, name=name, in_specs=[ANY] * n, out_specs=[ANY] * n,
        out_shape=[jax.ShapeDtypeStruct((N_DEV,) + a.shape, a.dtype) for a in arrays],
        scratch_shapes=[pltpu.SemaphoreType.DMA((n, 7)), pltpu.SemaphoreType.DMA((n, 7)),
                        pltpu.SemaphoreType.DMA((n,))],
    )(*arrays)


def _exchange_blocks(arrays, row_offsets, rows, name):
    n = len(arrays)
    width = arrays[0].shape[2]

    def body(*refs):
        ins, out = refs[:n], refs[n]
        send_sems, recv_sems, local_sems = refs[n + 1:]
        x, y, c = _me()
        me = 4 * x + 2 * y + c

        def dst(k, src_dev):
            return out.at[src_dev, pl.ds(row_offsets[k], arrays[k].shape[1]), :]

        local = [pltpu.make_async_copy(ins[k].at[me], dst(k, me), local_sems.at[k]) for k in range(n)]
        for cp in local:
            cp.start()
        sends = []
        for mask in range(1, N_DEV):
            px = 1 - x if mask & 4 else x
            py = 1 - y if mask & 2 else y
            pc = 1 - c if mask & 1 else c
            peer = 4 * px + 2 * py + pc
            for k in range(n):
                cp = pltpu.make_async_remote_copy(
                    src_ref=ins[k].at[peer], dst_ref=dst(k, me),
                    send_sem=send_sems.at[k, mask - 1], recv_sem=recv_sems.at[k, mask - 1],
                    device_id=(px, py, pc), device_id_type=MESH)
                cp.start()
                sends.append((cp, k, peer, mask))
        for cp, k, peer, mask in sends:
            pltpu.make_async_remote_copy(
                src_ref=ins[k].at[me], dst_ref=dst(k, peer),
                send_sem=send_sems.at[k, mask - 1], recv_sem=recv_sems.at[k, mask - 1],
                device_id=(x, y, c), device_id_type=MESH).wait_recv()
        for cp, _, _, _ in sends:
            cp.wait_send()
        for cp in local:
            cp.wait()

    return pl.pallas_call(
        body, name=name, in_specs=[ANY] * n, out_specs=ANY,
        out_shape=jax.ShapeDtypeStruct((N_DEV, rows, width), arrays[0].dtype),
        scratch_shapes=[pltpu.SemaphoreType.DMA((n, 7)), pltpu.SemaphoreType.DMA((n, 7)),
                        pltpu.SemaphoreType.DMA((n,))],
    )(*arrays)


def _adamw_sum(g8, w, m, v, name, tr):
    r, width = w.shape
    bc1 = 1.0 - ADAM_B1 ** ADAM_STEP
    bc2 = 1.0 - ADAM_B2 ** ADAM_STEP

    def body(g_ref, w_ref, m_ref, v_ref, go_ref, d_ref, mo_ref, vo_ref):
        g = g_ref[0]
        for s in range(1, N_DEV):
            g = g + g_ref[s]
        mn = ADAM_B1 * m_ref[...] + (1.0 - ADAM_B1) * g
        vn = ADAM_B2 * v_ref[...] + (1.0 - ADAM_B2) * (g * g)
        m_hat = mn / bc1
        v_hat = vn / bc2
        go_ref[...] = g
        d_ref[...] = -ADAM_LR * (m_hat / (jnp.sqrt(v_hat) + ADAM_EPS) + ADAM_WD * w_ref[...])
        mo_ref[...] = mn
        vo_ref[...] = vn

    spec = pl.BlockSpec((tr, width), lambda i: (i, 0))
    return pl.pallas_call(
        body, name=name, grid=(r // tr,),
        in_specs=[pl.BlockSpec((N_DEV, tr, width), lambda i: (0, i, 0)), spec, spec, spec],
        out_specs=[spec] * 4, out_shape=[jax.ShapeDtypeStruct((r, width), F32)] * 4,
        compiler_params=_params(),
    )(g8, w, m, v)


def _local_step(x, target, wb, lay, wkv, sm, tm, tk):
    t, d = x.shape
    f = lay.f
    w_dw32 = jnp.concatenate([sm["w_dw"], jnp.zeros((HALO - CONV_WIDTH, d), F32)], axis=0)
    lmg, lmb, lfg, lfb = sm["ln_mix_g"], sm["ln_mix_b"], sm["ln_ffn_g"], sm["ln_ffn_b"]
    bkv = jnp.concatenate([sm["b_k"], sm["b_v"]], axis=1)

    xb0, ag, xhc, rsc, xh1, rs1 = _conv_fwd(x, wb, lay, w_dw32, sm["b_pw1"], sm["b_dw"], sm["cg"], sm["cb"],
                                            sm["b_pw2"], lmg[0:1], lmb[0:1], tm)
    x1b, hg0, hu0, xh2, rs2, x2b, kvs = _ffn_fwd(xh1, lmg[0:1], lmb[0:1], wb, lay, 0, tm,
                                                kv=(lfg[0:1], lfb[0:1], wkv, bkv))
    q, o, lse, xh3, rs3 = _attn_fwd(xh2, lfg[0:1], lfb[0:1], x2b, kvs, wb, lay, sm["b_q"], sm["sinks"],
                                    sm["b_o"], tm)
    x3b, hg1, hu1, dz4, st4, loss = _ffn_fwd(xh3, lmg[1:2], lmb[1:2], wb, lay, 1, tm,
                                             loss=(lfg[1:2], lfb[1:2], target))

    dz4b, act1, dhg1, dhu1, dz3, st3 = _ffn_bwd(dz4, hg1, hu1, xh3, rs3, lmg[1:2], wb, lay, 1, tm)
    dz3b, dq, dkc, dkp, stq, dsinks = _attn_bwd(dz3, q, o, lse, kvs, wb, lay, sm["sinks"], tm)
    dkv, dz2, st2, dbkv = _x2_bwd(dz3, dq, dkc, dkp, xh2, rs2, lfg[0:1], wb, lay, wkv, tm)
    dz2b, act0, dhg0, dhu0, dz1, st1 = _ffn_bwd(dz2, hg0, hu0, xh1, rs1, lmg[0:1], wb, lay, 0, tm)
    dz1b, s_act, dcv, stc = _conv_bwd1(dz1, xhc, rsc, wb, lay, sm["cg"], sm["cb"], tm)
    grad_x, dh1, dwdw, db1 = _conv_bwd2(dz1, dcv, ag, wb, lay, w_dw32, tm)

    bm_f = f // 2 if (f // 2) % 128 == 0 else f
    big = {
        "pw1t": _tn_matmul(dh1, xb0, "dw_pw1", d, tk),
        "pw2": _tn_matmul(s_act, dz1b, "dw_pw2", d, tk),
        "wq": _tn_matmul(x2b, dq, "dw_q", d, tk),
        "wo": _tn_matmul(o, dz3b, "dw_o", d, tk),
        "gt0": _tn_matmul(dhg0, x1b, "dw_gate0", bm_f, tk),
        "ut0": _tn_matmul(dhu0, x1b, "dw_up0", bm_f, tk),
        "dn0": _tn_matmul(act0, dz2b, "dw_down0", bm_f, tk),
        "gt1": _tn_matmul(dhg1, x3b, "dw_gate1", bm_f, tk),
        "ut1": _tn_matmul(dhu1, x3b, "dw_up1", bm_f, tk),
        "dn1": _tn_matmul(act1, dz4b, "dw_down1", bm_f, tk),
        "wkv": _tn_matmul(x2b, dkv, "dw_kv", d, tk),
    }
    small = {
        "w_dw": dwdw[:CONV_WIDTH], "b_pw1": db1, "b_dw": stc[2:3], "cg": stc[0:1], "cb": stc[1:2],
        "b_pw2": stc[3:4], "b_k": dbkv[:, :BLOCK], "b_v": dbkv[:, BLOCK:], "b_q": stq[0:1],
        "sinks": dsinks[:, :d // HEAD_DIM],
        "b_o": stq[1:2],
        "ln_mix_g": jnp.concatenate([st1[0:1], st3[0:1]], axis=0),
        "ln_mix_b": jnp.concatenate([st1[1:2], st3[1:2]], axis=0),
        "ln_ffn_g": jnp.concatenate([st2[0:1], st4[0:1]], axis=0),
        "ln_ffn_b": jnp.concatenate([st2[1:2], st4[1:2]], axis=0),
    }
    return loss[0, 0], grad_x, big, small


SP_ROWS = 40
SP_BDW, SP_CG, SP_CB, SP_BPW2, SP_BPW1 = 32, 33, 34, 35, 36
RP_ROWS = 88


def _pack_big(lay, pw1, pw2, wq, wo, gate, up, down, wk, wv):
    d = lay.d
    parts = [pw1[0].T, pw2[0], wq[0], wo[0]]
    for l in range(DEPTH):
        parts += [gate[l].T, up[l].T, down[l]]
    parts.append(jnp.concatenate([wk, wv], axis=1).reshape(-1, d))
    return jnp.concatenate(parts, axis=0)


def _unpack_big(lay, p):
    d = lay.d

    def seg(name):
        return p[lay.off[name]:lay.off[name] + lay.n[name]]

    kvw = seg("wkv").reshape(d // N_DEV, 2 * BLOCK)
    return dict(
        pw1=seg("pw1t").T[None], pw2=seg("pw2")[None], wq=seg("wq")[None], wo=seg("wo")[None],
        gate=jnp.stack([seg(f"gt{l}").T for l in range(DEPTH)]),
        up=jnp.stack([seg(f"ut{l}").T for l in range(DEPTH)]),
        down=jnp.stack([seg(f"dn{l}") for l in range(DEPTH)]),
        wk=kvw[:, :BLOCK], wv=kvw[:, BLOCK:])


def _pack_small(w_dw, b_dw, cg, cb, b_pw2, b_pw1):
    z = jnp.zeros((1, 128), F32)
    return jnp.concatenate([w_dw[0], z, b_dw, cg, cb, b_pw2, b_pw1.reshape(2, 128), z, z], axis=0)


def _unpack_small(p):
    return dict(w_dw=p[None, :CONV_WIDTH], b_dw=p[SP_BDW:SP_BDW + 1], cg=p[SP_CG:SP_CG + 1],
                cb=p[SP_CB:SP_CB + 1], b_pw2=p[SP_BPW2:SP_BPW2 + 1], b_pw1=p[SP_BPW1:SP_BPW1 + 2].reshape(1, 256))


def _small_full(g):
    d = N_DEV * 128

    def wide(r0, n=1):
        return jnp.transpose(g[:, r0:r0 + n], (1, 0, 2)).reshape(n, d)

    return dict(w_dw=wide(0, CONV_WIDTH), b_dw=wide(SP_BDW), cg=wide(SP_CG), cb=wide(SP_CB),
                b_pw2=wide(SP_BPW2), b_pw1=g[:, SP_BPW1:SP_BPW1 + 2].reshape(1, 2 * d))


def _small_grad_blocks(sg):
    def narrow(a):
        return jnp.transpose(a.reshape(a.shape[0], N_DEV, 128), (1, 0, 2))

    z = jnp.zeros((N_DEV, 1, 128), F32)
    return jnp.concatenate([narrow(sg["w_dw"]), z, narrow(sg["b_dw"]), narrow(sg["cg"]), narrow(sg["cb"]),
                            narrow(sg["b_pw2"]), sg["b_pw1"].reshape(N_DEV, 2, 128), z, z], axis=1)


RP_FIELDS = (("ln_mix_g", 16), ("ln_mix_b", 16), ("ln_ffn_g", 16), ("ln_ffn_b", 16), ("b_q", 8), ("b_o", 8),
             ("b_k", 1), ("b_v", 1), ("sinks", 1))


def _pack_rep(vals):
    parts = []
    for name, rows in RP_FIELDS:
        a = vals[name].reshape(-1)
        if name == "sinks":
            a = jnp.concatenate([a, jnp.zeros((128 - a.shape[0],), F32)])
        parts.append(a.reshape(rows, 128))
    used = sum(r for _, r in RP_FIELDS)
    parts.append(jnp.zeros((RP_ROWS - used, 128), F32))
    return jnp.concatenate(parts, axis=0)


def _unpack_rep(p, shapes):
    out, r = {}, 0
    for name, rows in RP_FIELDS:
        a = p[r:r + rows].reshape(-1)
        n = 1
        for s in shapes[name]:
            n *= s
        out[name] = a[:n].reshape(shapes[name])
        r += rows
    return out


def kernel(x, conv_w_pw1, conv_b_pw1, conv_w_dw, conv_b_dw, conv_ln_g, conv_ln_b, conv_w_pw2, conv_b_pw2, kv_w_k, kv_b_k, kv_w_v, kv_b_v, attn_w_q, attn_b_q, attn_sinks, attn_w_o, attn_b_o, ffn_w_gate, ffn_w_up, ffn_w_down, ln_mix_g, ln_mix_b, ln_ffn_g, ln_ffn_b, loss_target, m_conv_w_pw1, m_conv_b_pw1, m_conv_w_dw, m_conv_b_dw, m_conv_ln_g, m_conv_ln_b, m_conv_w_pw2, m_conv_b_pw2, m_kv_w_k, m_kv_b_k, m_kv_w_v, m_kv_b_v, m_attn_w_q, m_attn_b_q, m_attn_sinks, m_attn_w_o, m_attn_b_o, m_ffn_w_gate, m_ffn_w_up, m_ffn_w_down, m_ln_mix_g, m_ln_mix_b, m_ln_ffn_g, m_ln_ffn_b, v_conv_w_pw1, v_conv_b_pw1, v_conv_w_dw, v_conv_b_dw, v_conv_ln_g, v_conv_ln_b, v_conv_w_pw2, v_conv_b_pw2, v_kv_w_k, v_kv_b_k, v_kv_w_v, v_kv_b_v, v_attn_w_q, v_attn_b_q, v_attn_sinks, v_attn_w_o, v_attn_b_o, v_ffn_w_gate, v_ffn_w_up, v_ffn_w_down, v_ln_mix_g, v_ln_mix_b, v_ln_ffn_g, v_ln_ffn_b):
    t, d = x.shape[1], x.shape[2]
    f = ffn_w_gate.shape[2] * N_DEV
    lay = _Layout(d, f)
    tm, tk = 256, 1024

    def big_pack(pw1, pw2, wq, wo, gate, up, down, wk, wv):
        return _pack_big(lay, pw1, pw2, wq, wo, gate, up, down, wk, wv)

    def small_pack(w_dw, b_dw, cg, cb, b_pw2, b_pw1):
        return _pack_small(w_dw, b_dw, cg, cb, b_pw2, b_pw1)

    rep_shapes = dict(ln_mix_g=ln_mix_g.shape, ln_mix_b=ln_mix_b.shape, ln_ffn_g=ln_ffn_g.shape,
                      ln_ffn_b=ln_ffn_b.shape, b_q=attn_b_q.shape, b_o=attn_b_o.shape, b_k=kv_b_k.shape,
                      b_v=kv_b_v.shape, sinks=attn_sinks.shape)

    def rep_pack(lmg, lmb, lfg, lfb, bq, bo, bk, bv, sk):
        return _pack_rep(dict(ln_mix_g=lmg, ln_mix_b=lmb, ln_ffn_g=lfg, ln_ffn_b=lfb, b_q=bq, b_o=bo,
                              b_k=bk, b_v=bv, sinks=sk))

    w_big = big_pack(conv_w_pw1, conv_w_pw2, attn_w_q, attn_w_o, ffn_w_gate, ffn_w_up, ffn_w_down, kv_w_k, kv_w_v)
    w_small = small_pack(conv_w_dw, conv_b_dw, conv_ln_g, conv_ln_b, conv_b_pw2, conv_b_pw1)
    w_rep = rep_pack(ln_mix_g, ln_mix_b, ln_ffn_g, ln_ffn_b, attn_b_q, attn_b_o, kv_b_k, kv_b_v, attn_sinks)

    wb, smg = _all_gather([w_big.astype(BF16), w_small], "gather_weights")
    wkv = wb[:, lay.off["wkv"]:lay.off["wkv"] + lay.n["wkv"], :].reshape(d, 2 * BLOCK)
    sm = _small_full(smg)
    sm.update(ln_mix_g=ln_mix_g, ln_mix_b=ln_mix_b, ln_ffn_g=ln_ffn_g, ln_ffn_b=ln_ffn_b, b_q=attn_b_q,
              b_o=attn_b_o, sinks=attn_sinks, b_k=kv_b_k.reshape(1, -1), b_v=kv_b_v.reshape(1, -1))

    loss_part, grad_x, gbig, gsmall = _local_step(x[0], loss_target[0], wb, lay, wkv, sm, tm, tk)
    loss = lax.psum(loss_part, ("x", "y", "c"))

    names = [n for n in lay.off]
    blocks = []
    for n in names:
        g = gbig[n]
        if n == "wkv":
            g = g.reshape(N_DEV, lay.n[n], d)
        else:
            g = g.reshape(N_DEV, lay.n[n], d)
        blocks.append(g)
    g8_big = _exchange_blocks(blocks, [lay.off[n] for n in names], lay.rows, "exchange_grads")
    g8_small = _exchange_blocks([_small_grad_blocks(gsmall)], [0], SP_ROWS, "exchange_small_grads")
    (g8_rep,) = _all_gather([_pack_rep(gsmall)], "gather_replicated_grads")

    m_big = big_pack(m_conv_w_pw1, m_conv_w_pw2, m_attn_w_q, m_attn_w_o, m_ffn_w_gate, m_ffn_w_up, m_ffn_w_down, m_kv_w_k, m_kv_w_v)
    v_big = big_pack(v_conv_w_pw1, v_conv_w_pw2, v_attn_w_q, v_attn_w_o, v_ffn_w_gate, v_ffn_w_up, v_ffn_w_down, v_kv_w_k, v_kv_w_v)
    m_small = small_pack(m_conv_w_dw, m_conv_b_dw, m_conv_ln_g, m_conv_ln_b, m_conv_b_pw2, m_conv_b_pw1)
    v_small = small_pack(v_conv_w_dw, v_conv_b_dw, v_conv_ln_g, v_conv_ln_b, v_conv_b_pw2, v_conv_b_pw1)
    m_rep = rep_pack(m_ln_mix_g, m_ln_mix_b, m_ln_ffn_g, m_ln_ffn_b, m_attn_b_q, m_attn_b_o, m_kv_b_k, m_kv_b_v, m_attn_sinks)
    v_rep = rep_pack(v_ln_mix_g, v_ln_mix_b, v_ln_ffn_g, v_ln_ffn_b, v_attn_b_q, v_attn_b_o, v_kv_b_k, v_kv_b_v, v_attn_sinks)

    tr = max(r for r in range(8, 129, 8) if lay.rows % r == 0)
    big_out = [_unpack_big(lay, a) for a in _adamw_sum(g8_big, w_big, m_big, v_big, "adamw_big", tr)]
    small_out = [_unpack_small(a) for a in _adamw_sum(g8_small, w_small, m_small, v_small, "adamw_small", SP_ROWS)]
    rep_out = [_unpack_rep(a, rep_shapes) for a in _adamw_sum(g8_rep, w_rep, m_rep, v_rep, "adamw_rep", RP_ROWS)]

    outs = [loss, grad_x[None]]
    for b, s, r in zip(big_out, small_out, rep_out):
        outs += [b["pw1"], s["b_pw1"], s["w_dw"], s["b_dw"], s["cg"], s["cb"], b["pw2"], s["b_pw2"],
                 b["wk"], r["b_k"], b["wv"], r["b_v"], b["wq"], r["b_q"], r["sinks"], b["wo"], r["b_o"],
                 b["gate"], b["up"], b["down"], r["ln_mix_g"], r["ln_mix_b"], r["ln_ffn_g"], r["ln_ffn_b"]]
    return tuple(outs)
```

```python
import functools

import jax
import jax.numpy as jnp
from jax import lax
from jax.experimental import pallas as pl
from jax.experimental.pallas import tpu as pltpu

F32 = jnp.float32
BF16 = jnp.bfloat16

N_DEV = 8
HEAD_DIM = 64
N_KV_HEADS = 2
BLOCK = 128
CONV_WIDTH = 31
HALO = 32
ALIBI_MAX = 8.0
DEPTH = 2
ALPHA = (2.0 * DEPTH) ** 0.25
LN_EPS = 1e-5
MASKED_DIST = 1e32
ADAM_LR = 0.001
ADAM_B1 = 0.9
ADAM_B2 = 0.999
ADAM_EPS = 1e-08
ADAM_WD = 0.01
ADAM_STEP = 10
VMEM_LIMIT = 56 * 1024 * 1024
MESH = pl.DeviceIdType.MESH


def _dot(a, b):
    return jnp.dot(a, b, preferred_element_type=F32)


def _dot_nt(a, b):
    return lax.dot_general(a, b, (((1,), (1,)), ((), ())), preferred_element_type=F32)


def _dot_tn(a, b):
    return lax.dot_general(a, b, (((0,), (0,)), ((), ())), preferred_element_type=F32)


def _sigmoid(v):
    return 1.0 / (1.0 + jnp.exp(-v))


def _ln_fwd(z):
    mu = jnp.mean(z, axis=-1, keepdims=True)
    zc = z - mu
    var = jnp.mean(zc * zc, axis=-1, keepdims=True)
    rstd = lax.rsqrt(var + LN_EPS)
    return zc * rstd, rstd


def _ln_bwd(dout, xh, rstd, g):
    dxh = dout * g
    m1 = jnp.mean(dxh, axis=-1, keepdims=True)
    m2 = jnp.mean(dxh * xh, axis=-1, keepdims=True)
    dz = rstd * (dxh - m1 - xh * m2)
    return dz, jnp.sum(dout * xh, axis=0, keepdims=True), jnp.sum(dout, axis=0, keepdims=True)


def _params(vmem=VMEM_LIMIT):
    return pltpu.CompilerParams(dimension_semantics=("arbitrary",), vmem_limit_bytes=vmem)


def _row(d):
    return pl.BlockSpec((1, d), lambda i: (0, 0))


def _tile(tm, d):
    return pl.BlockSpec((tm, d), lambda i: (i, 0))


def _fixed(r, d):
    return pl.BlockSpec((r, d), lambda i: (0, 0))


ANY = pl.BlockSpec(memory_space=pl.ANY)


class _Layout:
    GATHER = {"a": ("pw1t", "pw2"),
              "b": ("wq", "wo", "gt0", "ut0", "dn0", "gt1", "ut1", "dn1", "wkv")}
    GRADS = {"r1": ("gt1", "ut1", "dn1"), "r2": ("wq", "wo", "wkv"), "r3": ("gt0", "ut0", "dn0", "pw2"),
             "r4": ("pw1t",)}

    def __init__(self, d, f):
        self.d, self.f = d, f
        self.n = {"pw1t": 2 * d // N_DEV, "pw2": d // N_DEV, "wq": d // N_DEV, "wo": d // N_DEV,
                  "wkv": (d // N_DEV) * 2 * BLOCK // d}
        for l in range(DEPTH):
            self.n.update({f"gt{l}": f // N_DEV, f"ut{l}": f // N_DEV, f"dn{l}": f // N_DEV})
        self.goff, self.grows = self._offsets(self.GATHER)
        self.roff, self.rrows = self._offsets(self.GRADS)

    def _offsets(self, groups):
        off, rows = {}, {}
        for g, names in groups.items():
            r = 0
            for name in names:
                off[name] = r
                r += self.n[name]
            rows[g] = r
        return off, rows


def _load_weight(wb_ref, lay, name, dst):
    pltpu.sync_copy(wb_ref.at[:, pl.ds(lay.goff[name], lay.n[name]), :], dst)


def _wscratch(lay, name):
    return pltpu.VMEM((N_DEV, lay.n[name], lay.d), BF16)


def _wfull(ref):
    s = ref.shape
    return ref[...].reshape(s[0] * s[1], s[2])


def _wchunk(ref, c, nchunks):
    s = ref.shape
    per = s[0] // nchunks
    return ref[c * per:(c + 1) * per].reshape(per * s[1], s[2])


def _me():
    return lax.axis_index("x"), lax.axis_index("y"), lax.axis_index("c")


def _peer(mask):
    x, y, c = _me()
    return (1 - x if mask & 4 else x, 1 - y if mask & 2 else y, 1 - c if mask & 1 else c)


def _index(dev):
    return 4 * dev[0] + 2 * dev[1] + dev[2]


class _HostedGather:
    def __init__(self, array):
        self.arrays = [array]
        self.out_shape = jax.ShapeDtypeStruct((N_DEV,) + array.shape, array.dtype)

    def scratch(self):
        return [pltpu.SemaphoreType.DMA((7,)), pltpu.SemaphoreType.DMA((7,)), pltpu.SemaphoreType.DMA(())]

    def _copies(self, ins, out, send_sems, recv_sems, local_sem):
        x, y, c = _me()
        me, sibling = (x, y, c), (x, y, 1 - c)
        chips = [(1 - x, y), (x, 1 - y), (1 - x, 1 - y)]

        def copy(k, block, to, src=None):
            rows = out.at[_index(block)]
            return pltpu.make_async_remote_copy(
                src_ref=rows if src is None else src, dst_ref=rows, send_sem=send_sems.at[k],
                recv_sem=recv_sems.at[k], device_id=to, device_id_type=MESH)

        return dict(
            mine=lambda: pltpu.make_async_copy(ins[0], out.at[_index(me)], local_sem),
            first=lambda: [copy(0, me, sibling, src=ins[0])] + [copy(1 + j, me, (*chip, c), src=ins[0])
                                                                for j, chip in enumerate(chips)],
            over_ici=lambda: [copy(1 + j, (*chip, c), me) for j, chip in enumerate(chips)],
            passed=lambda: [copy(4 + j, (*chip, c), sibling) for j, chip in enumerate(chips)],
            from_sibling=lambda: [copy(0, sibling, me)] + [copy(4 + j, (*chip, 1 - c), me)
                                                           for j, chip in enumerate(chips)])

    def start(self, *refs):
        cp = self._copies(*refs)
        cp["mine"]().start()
        for c in cp["first"]():
            c.start()

    def middle(self, *refs):
        cp = self._copies(*refs)
        for arrived, onward in zip(cp["over_ici"](), cp["passed"]()):
            arrived.wait_recv()
            onward.start()

    def finish(self, *refs):
        cp = self._copies(*refs)
        for c in cp["from_sibling"]():
            c.wait_recv()
        for c in cp["first"]() + cp["passed"]():
            c.wait_send()
        cp["mine"]().wait()


class _HostedExchange:
    def __init__(self, arrays, offsets, rows):
        self.arrays, self.offsets = list(arrays), list(offsets)
        self.out_shape = jax.ShapeDtypeStruct((N_DEV, rows, arrays[0].shape[2]), arrays[0].dtype)

    def scratch(self):
        n = len(self.arrays)
        return [pltpu.SemaphoreType.DMA((n, 7)), pltpu.SemaphoreType.DMA((n, 7)), pltpu.SemaphoreType.DMA((n,))]

    def _copies(self, ins, out, send_sems, recv_sems, local_sems):
        me = _index(_me())

        def dst(k, src_dev):
            return out.at[src_dev, pl.ds(self.offsets[k], self.arrays[k].shape[1]), :]

        pairs = [(k, mask) for k in range(len(self.arrays)) for mask in range(1, N_DEV)]

        def local():
            return [pltpu.make_async_copy(ins[k].at[me], dst(k, me), local_sems.at[k])
                    for k in range(len(self.arrays))]

        def sends():
            return [pltpu.make_async_remote_copy(
                src_ref=ins[k].at[_index(_peer(mask))], dst_ref=dst(k, me), send_sem=send_sems.at[k, mask - 1],
                recv_sem=recv_sems.at[k, mask - 1], device_id=_peer(mask), device_id_type=MESH)
                for k, mask in pairs]

        def arrivals():
            return [pltpu.make_async_remote_copy(
                src_ref=ins[k].at[me], dst_ref=dst(k, _index(_peer(mask))), send_sem=send_sems.at[k, mask - 1],
                recv_sem=recv_sems.at[k, mask - 1], device_id=_me(), device_id_type=MESH)
                for k, mask in pairs]

        return local, sends, arrivals

    def start(self, *refs):
        local, sends, _ = self._copies(*refs)
        for c in local() + sends():
            c.start()

    def middle(self, *refs):
        pass

    def finish(self, *refs):
        local, sends, arrivals = self._copies(*refs)
        for c in arrivals():
            c.wait_recv()
        for c in sends():
            c.wait_send()
        for c in local():
            c.wait()


def _gridded_call(body, name, nsteps, in_specs, out_specs, out_shape, scratch, args, hosted=None):
    if hosted is None:
        return pl.pallas_call(body, name=name, grid=(nsteps,), in_specs=in_specs, out_specs=out_specs,
                              out_shape=out_shape, scratch_shapes=scratch, compiler_params=_params())(*args)
    n_in, n_out, n_scr, h_in = len(in_specs), len(out_specs), len(scratch), len(hosted.arrays)

    def with_hosted(*refs):
        a = n_in + h_in
        b = a + n_out
        e = b + 1 + n_scr
        comm = (refs[n_in:a], refs[b], refs[e], refs[e + 1], refs[e + 2])
        i = pl.program_id(0)

        @pl.when(i == 0)
        def _():
            hosted.start(*comm)

        body(*refs[:n_in], *refs[a:b], *refs[b + 1:e])

        @pl.when(i == nsteps // 2)
        def _():
            hosted.middle(*comm)

        @pl.when(i == nsteps - 1)
        def _():
            hosted.finish(*comm)

    return pl.pallas_call(
        with_hosted, name=name, grid=(nsteps,), in_specs=list(in_specs) + [ANY] * h_in,
        out_specs=list(out_specs) + [ANY], out_shape=list(out_shape) + [hosted.out_shape],
        scratch_shapes=list(scratch) + hosted.scratch(), compiler_params=_params(),
    )(*args, *hosted.arrays)


CONV_RB = 32
CONV_LC = 256
CONV_WIN = CONV_RB + HALO + 8


def _shifted(win, r):
    return win if r == 0 else pltpu.roll(win, win.shape[0] - r, 0)


def _conv_fwd(x, wb, lay, w_dw, b_pw1, b_dw, cg, cb, b_pw2, lg, lb, tm, hosted=None):
    t, d = x.shape
    nsteps = t // tm

    def body(x_ref, xh_ref, wb_ref, wdw_ref, b1_ref, bdw_ref, cg_ref, cb_ref, b2_ref, lg_ref, lb_ref,
             xb_ref, ag_ref, xhc_ref, rsc_ref, xh1_ref, rs1_ref, w1_s, w2_s, ubuf, cv_s):
        i = pl.program_id(0)

        @pl.when(i == 0)
        def _():
            _load_weight(wb_ref, lay, "pw1t", w1_s)
            _load_weight(wb_ref, lay, "pw2", w2_s)
            ubuf[pl.ds(HALO + tm, 8), :] = jnp.zeros((8, d), F32)

        w1 = _wfull(w1_s)
        xv = x_ref[...]
        xb = xv.astype(BF16)
        xb_ref[...] = xb
        h = _dot_nt(xb, w1) + b1_ref[...]
        ag_ref[...] = h.astype(BF16)
        ubuf[pl.ds(HALO, tm), :] = h[:, :d] * _sigmoid(h[:, d:])
        hh = _dot_nt(xh_ref[...].astype(BF16), w1) + b1_ref[...]
        uh = hh[:, :d] * _sigmoid(hh[:, d:])
        ubuf[pl.ds(0, HALO), :] = jnp.where(i > 0, uh, 0.0)

        def conv_block(rb, carry):
            t0 = pl.multiple_of(rb * CONV_RB, CONV_RB)
            for lc in range(d // CONV_LC):
                lanes = slice(lc * CONV_LC, (lc + 1) * CONV_LC)
                win = ubuf[pl.ds(t0, CONV_WIN), lanes]
                acc = jnp.zeros((CONV_RB, CONV_LC), F32)
                for r in range(8):
                    wr = _shifted(win, r)
                    for k in range(CONV_WIDTH):
                        s = HALO - (CONV_WIDTH - 1) + k
                        if s % 8 == r:
                            q = 8 * (s // 8)
                            acc = acc + wr[q:q + CONV_RB] * wdw_ref[k:k + 1, lanes]
                cv_s[pl.ds(t0, CONV_RB), lanes] = acc
            return carry

        lax.fori_loop(0, tm // CONV_RB, conv_block, 0)
        cv = cv_s[...] + bdw_ref[...]
        xhc, rsc = _ln_fwd(cv)
        xhc_ref[...] = xhc
        rsc_ref[...] = rsc
        n = xhc * cg_ref[...] + cb_ref[...]
        s_act = n * _sigmoid(n)
        m = _dot(s_act.astype(BF16), _wfull(w2_s)) + b2_ref[...]
        xh1, rs1 = _ln_fwd(ALPHA * xv + m)
        xh1_ref[...] = xh1
        rs1_ref[...] = rs1

    hb = tm // HALO
    return _gridded_call(
        body, "conv_fwd", nsteps,
        [_tile(tm, d), pl.BlockSpec((HALO, d), lambda i: (jnp.maximum(i * hb - 1, 0), 0)), ANY,
         _fixed(HALO, d), _row(2 * d), _row(d), _row(d), _row(d), _row(d), _row(d), _row(d)],
        [_tile(tm, d), _tile(tm, 2 * d), _tile(tm, d), _tile(tm, 1), _tile(tm, d), _tile(tm, 1)],
        [jax.ShapeDtypeStruct((t, d), BF16), jax.ShapeDtypeStruct((t, 2 * d), BF16),
         jax.ShapeDtypeStruct((t, d), F32), jax.ShapeDtypeStruct((t, 1), F32),
         jax.ShapeDtypeStruct((t, d), F32), jax.ShapeDtypeStruct((t, 1), F32)],
        [_wscratch(lay, "pw1t"), _wscratch(lay, "pw2"),
         pltpu.VMEM((HALO + tm + 8, d), F32), pltpu.VMEM((tm, d), F32)],
        (x, x, wb, w_dw, b_pw1, b_dw, cg, cb, b_pw2, lg, lb), hosted)


def _conv_bwd1(dz1, xhc, rsc, wb, lay, cg, cb, tm):
    t, d = dz1.shape

    def body(dz_ref, xhc_ref, rsc_ref, wb_ref, cg_ref, cb_ref, dzb_ref, s_ref, dcv_ref, st_ref, w2_s):
        i = pl.program_id(0)

        @pl.when(i == 0)
        def _():
            _load_weight(wb_ref, lay, "pw2", w2_s)
            st_ref[...] = jnp.zeros(st_ref.shape, F32)

        dz = dz_ref[...]
        dzb = dz.astype(BF16)
        dzb_ref[...] = dzb
        xhc_v = xhc_ref[...]
        n = xhc_v * cg_ref[...] + cb_ref[...]
        sg = _sigmoid(n)
        s_ref[...] = (n * sg).astype(BF16)
        ds = _dot_nt(dzb, _wfull(w2_s))
        dn = ds * (sg * (1.0 + n * (1.0 - sg)))
        dcv, dg, db = _ln_bwd(dn, xhc_v, rsc_ref[...], cg_ref[...])
        dcv_ref[...] = dcv
        st_ref[0:1, :] += dg
        st_ref[1:2, :] += db
        st_ref[2:3, :] += jnp.sum(dcv, axis=0, keepdims=True)
        st_ref[3:4, :] += jnp.sum(dz, axis=0, keepdims=True)

    return pl.pallas_call(
        body, name="conv_bwd1", grid=(t // tm,),
        in_specs=[_tile(tm, d), _tile(tm, d), _tile(tm, 1), ANY, _row(d), _row(d)],
        out_specs=[_tile(tm, d), _tile(tm, d), _tile(tm, d), _fixed(8, d)],
        out_shape=[jax.ShapeDtypeStruct((t, d), BF16), jax.ShapeDtypeStruct((t, d), BF16),
                   jax.ShapeDtypeStruct((t, d), F32), jax.ShapeDtypeStruct((8, d), F32)],
        scratch_shapes=[_wscratch(lay, "pw2")],
        compiler_params=_params(),
    )(dz1, xhc, rsc, wb, cg, cb)


def _conv_bwd2(dz1, dcv, ag, wb, lay, w_dw, tm, hosted=None):
    t, d = dz1.shape
    nsteps = t // tm

    def body(dz_ref, dcv_ref, dcvn_ref, ag_ref, agp_ref, wb_ref, wdw_ref,
             gx_ref, dh_ref, dw_ref, db1_ref, w1_s, ubuf, dbuf, du_s, dwacc):
        i = pl.program_id(0)

        @pl.when(i == 0)
        def _():
            _load_weight(wb_ref, lay, "pw1t", w1_s)
            ubuf[pl.ds(HALO + tm, 8), :] = jnp.zeros((8, d), F32)
            dbuf[pl.ds(HALO + tm, 8), :] = jnp.zeros((8, d), F32)
            dwacc[...] = jnp.zeros(dwacc.shape, F32)
            db1_ref[...] = jnp.zeros(db1_ref.shape, F32)

        ag_v = ag_ref[...].astype(F32)
        a, g = ag_v[:, :d], ag_v[:, d:]
        sg = _sigmoid(g)
        ubuf[pl.ds(HALO, tm), :] = a * sg
        agp = agp_ref[...].astype(F32)
        ubuf[pl.ds(0, HALO), :] = jnp.where(i > 0, agp[:, :d] * _sigmoid(agp[:, d:]), 0.0)
        dbuf[pl.ds(0, tm), :] = dcv_ref[...]
        dbuf[pl.ds(tm, HALO), :] = jnp.where(i < nsteps - 1, dcvn_ref[...], 0.0)

        def conv_block(rb, carry):
            t0 = pl.multiple_of(rb * CONV_RB, CONV_RB)
            for lc in range(d // CONV_LC):
                lanes = slice(lc * CONV_LC, (lc + 1) * CONV_LC)
                dwin = dbuf[pl.ds(t0, CONV_WIN), lanes]
                uwin = ubuf[pl.ds(t0, CONV_WIN), lanes]
                dcur = dwin[0:CONV_RB]
                acc = jnp.zeros((CONV_RB, CONV_LC), F32)
                for r in range(8):
                    dr = _shifted(dwin, r)
                    ur = _shifted(uwin, r)
                    for k in range(CONV_WIDTH):
                        sd = CONV_WIDTH - 1 - k
                        if sd % 8 == r:
                            q = 8 * (sd // 8)
                            acc = acc + dr[q:q + CONV_RB] * wdw_ref[k:k + 1, lanes]
                        su = HALO - (CONV_WIDTH - 1) + k
                        if su % 8 == r:
                            q = 8 * (su // 8)
                            prod = dcur * ur[q:q + CONV_RB]
                            part = prod[0:8]
                            for j in range(1, CONV_RB // 8):
                                part = part + prod[8 * j:8 * j + 8]
                            dwacc[k, :, lanes] += part
                du_s[pl.ds(t0, CONV_RB), lanes] = acc
            return carry

        lax.fori_loop(0, tm // CONV_RB, conv_block, 0)
        du = du_s[...]
        da = du * sg
        dg = du * a * sg * (1.0 - sg)
        dh = jnp.concatenate([da, dg], axis=1)
        dhb = dh.astype(BF16)
        dh_ref[...] = dhb
        db1_ref[...] += jnp.sum(dh, axis=0, keepdims=True)
        gx_ref[...] = ALPHA * dz_ref[...] + _dot(dhb, _wfull(w1_s))

        @pl.when(i == nsteps - 1)
        def _():
            dw_ref[...] = jnp.sum(dwacc[...], axis=1)

    hb = tm // HALO
    last = t // HALO - 1
    return _gridded_call(
        body, "conv_bwd2", nsteps,
        [_tile(tm, d), _tile(tm, d),
         pl.BlockSpec((HALO, d), lambda i: (jnp.minimum((i + 1) * hb, last), 0)),
         _tile(tm, 2 * d),
         pl.BlockSpec((HALO, 2 * d), lambda i: (jnp.maximum(i * hb - 1, 0), 0)),
         ANY, _fixed(HALO, d)],
        [_tile(tm, d), _tile(tm, 2 * d), _fixed(HALO, d), _row(2 * d)],
        [jax.ShapeDtypeStruct((t, d), F32), jax.ShapeDtypeStruct((t, 2 * d), BF16),
         jax.ShapeDtypeStruct((HALO, d), F32), jax.ShapeDtypeStruct((1, 2 * d), F32)],
        [_wscratch(lay, "pw1t"), pltpu.VMEM((HALO + tm + 8, d), F32),
         pltpu.VMEM((HALO + tm + 8, d), F32), pltpu.VMEM((tm, d), F32),
         pltpu.VMEM((HALO, 8, d), F32)],
        (dz1, dcv, dcv, ag, ag, wb, w_dw), hosted)


FFN_CHUNKS = 2


def _ffn_fwd(xh_in, g_in, b_in, wb, lay, layer, tm, *, kv=None, loss=None):
    t, d = xh_in.shape
    f = lay.f
    fc = f // FFN_CHUNKS
    names = (f"gt{layer}", f"ut{layer}", f"dn{layer}")

    def body(*refs):
        xh_ref, gi_ref, bi_ref, wb_ref = refs[:4]
        pos = 4
        if kv is not None:
            go_ref, bo_ref, wkv_ref, bkv_ref = refs[pos:pos + 4]
            pos += 4
        if loss is not None:
            go_ref, bo_ref, tgt_ref = refs[pos:pos + 3]
            pos += 3
        xb_ref, hg_ref, hu_ref = refs[pos:pos + 3]
        pos += 3
        if kv is not None:
            xho_ref, rso_ref, xob_ref, kv_ref = refs[pos:pos + 4]
            pos += 4
        if loss is not None:
            dz_ref, st_ref, loss_ref = refs[pos:pos + 3]
            pos += 3
        gt_s, ut_s, dn_s = refs[pos:pos + 3]
        i = pl.program_id(0)

        @pl.when(i == 0)
        def _():
            for name, dst in zip(names, (gt_s, ut_s, dn_s)):
                _load_weight(wb_ref, lay, name, dst)
            if loss is not None:
                st_ref[...] = jnp.zeros(st_ref.shape, F32)
                loss_ref[...] = jnp.zeros(loss_ref.shape, F32)

        xin = xh_ref[...] * gi_ref[...] + bi_ref[...]
        xb = xin.astype(BF16)
        xb_ref[...] = xb
        fo = jnp.zeros((tm, d), F32)
        for c in range(FFN_CHUNKS):
            rows = slice(c * fc, (c + 1) * fc)
            hg = _dot_nt(xb, _wchunk(gt_s, c, FFN_CHUNKS))
            hu = _dot_nt(xb, _wchunk(ut_s, c, FFN_CHUNKS))
            hg_ref[:, rows] = hg.astype(BF16)
            hu_ref[:, rows] = hu.astype(BF16)
            act = hg * _sigmoid(hg) * hu
            fo = fo + _dot(act.astype(BF16), _wchunk(dn_s, c, FFN_CHUNKS))
        xho, rso = _ln_fwd(ALPHA * xin + fo)
        if kv is not None:
            xho_ref[...] = xho
            rso_ref[...] = rso
            xob = (xho * go_ref[...] + bo_ref[...]).astype(BF16)
            xob_ref[...] = xob
            kv_ref[...] = (_dot(xob, wkv_ref[...]) + bkv_ref[...]).astype(BF16)
        if loss is not None:
            diff = xho * go_ref[...] + bo_ref[...] - tgt_ref[...]
            loss_ref[...] += (0.5 / d) * jnp.sum(diff * diff)
            dz, dg, db = _ln_bwd(diff * (1.0 / d), xho, rso, go_ref[...])
            dz_ref[...] = dz
            st_ref[0:1, :] += dg
            st_ref[1:2, :] += db

    in_specs = [_tile(tm, d), _row(d), _row(d), ANY]
    args = [xh_in, g_in, b_in, wb]
    out_specs = [_tile(tm, d), _tile(tm, f), _tile(tm, f)]
    out_shape = [jax.ShapeDtypeStruct((t, d), BF16), jax.ShapeDtypeStruct((t, f), BF16),
                 jax.ShapeDtypeStruct((t, f), BF16)]
    if kv is not None:
        in_specs += [_row(d), _row(d), _fixed(d, 2 * BLOCK), _row(2 * BLOCK)]
        args += list(kv)
        out_specs += [_tile(tm, d), _tile(tm, 1), _tile(tm, d), _tile(tm, 2 * BLOCK)]
        out_shape += [jax.ShapeDtypeStruct((t, d), F32), jax.ShapeDtypeStruct((t, 1), F32),
                      jax.ShapeDtypeStruct((t, d), BF16), jax.ShapeDtypeStruct((t, 2 * BLOCK), BF16)]
    if loss is not None:
        in_specs += [_row(d), _row(d), _tile(tm, d)]
        args += list(loss)
        out_specs += [_tile(tm, d), _fixed(8, d), _fixed(8, 128)]
        out_shape += [jax.ShapeDtypeStruct((t, d), F32), jax.ShapeDtypeStruct((8, d), F32),
                      jax.ShapeDtypeStruct((8, 128), F32)]
    return pl.pallas_call(
        body, name=f"ffn_fwd{layer}", grid=(t // tm,), in_specs=in_specs, out_specs=out_specs,
        out_shape=out_shape, scratch_shapes=[_wscratch(lay, n) for n in names],
        compiler_params=_params(),
    )(*args)


def _ffn_bwd(dz, hg, hu, xh_in, rs_in, g_in, wb, lay, layer, tm, hosted=None):
    t, d = dz.shape
    f = lay.f
    fc = f // FFN_CHUNKS
    names = (f"gt{layer}", f"ut{layer}", f"dn{layer}")

    def body(dz_ref, hg_ref, hu_ref, xh_ref, rs_ref, gi_ref, wb_ref,
             dzb_ref, act_ref, dhg_ref, dhu_ref, dzp_ref, st_ref, gt_s, ut_s, dn_s):
        i = pl.program_id(0)

        @pl.when(i == 0)
        def _():
            for name, dst in zip(names, (gt_s, ut_s, dn_s)):
                _load_weight(wb_ref, lay, name, dst)
            st_ref[...] = jnp.zeros(st_ref.shape, F32)

        dzv = dz_ref[...]
        dzb = dzv.astype(BF16)
        dzb_ref[...] = dzb
        dx = ALPHA * dzv
        for c in range(FFN_CHUNKS):
            rows = slice(c * fc, (c + 1) * fc)
            hg_v = hg_ref[:, rows].astype(F32)
            hu_v = hu_ref[:, rows].astype(F32)
            sg = _sigmoid(hg_v)
            silu = hg_v * sg
            act_ref[:, rows] = (silu * hu_v).astype(BF16)
            dact = _dot_nt(dzb, _wchunk(dn_s, c, FFN_CHUNKS))
            dhu = (dact * silu).astype(BF16)
            dhg = (dact * hu_v * (sg * (1.0 + hg_v * (1.0 - sg)))).astype(BF16)
            dhu_ref[:, rows] = dhu
            dhg_ref[:, rows] = dhg
            dx = dx + _dot(dhg, _wchunk(gt_s, c, FFN_CHUNKS)) + _dot(dhu, _wchunk(ut_s, c, FFN_CHUNKS))
        dzp, dg, db = _ln_bwd(dx, xh_ref[...], rs_ref[...], gi_ref[...])
        dzp_ref[...] = dzp
        st_ref[0:1, :] += dg
        st_ref[1:2, :] += db

    return _gridded_call(
        body, f"ffn_bwd{layer}", t // tm,
        [_tile(tm, d), _tile(tm, f), _tile(tm, f), _tile(tm, d), _tile(tm, 1), _row(d), ANY],
        [_tile(tm, d), _tile(tm, f), _tile(tm, f), _tile(tm, f), _tile(tm, d), _fixed(8, d)],
        [jax.ShapeDtypeStruct((t, d), BF16), jax.ShapeDtypeStruct((t, f), BF16),
         jax.ShapeDtypeStruct((t, f), BF16), jax.ShapeDtypeStruct((t, f), BF16),
         jax.ShapeDtypeStruct((t, d), F32), jax.ShapeDtypeStruct((8, d), F32)],
        [_wscratch(lay, n) for n in names],
        (dz, hg, hu, xh_in, rs_in, g_in, wb), hosted)


def _alibi_slope(h, nq):
    return 2.0 ** (-ALIBI_MAX * (h + 1) / nq)


def _fill_alibi_bias(bias_s, nq):
    qi = lax.broadcasted_iota(jnp.int32, (BLOCK, 2 * BLOCK), 0)
    kj = lax.broadcasted_iota(jnp.int32, (BLOCK, 2 * BLOCK), 1)
    delta = qi + BLOCK - kj
    valid = (delta >= 0) & (delta < BLOCK)
    dist = jnp.where(valid, delta.astype(F32), MASKED_DIST)
    dist_first = jnp.where(kj >= BLOCK, dist, MASKED_DIST)
    for h in range(nq):
        bias_s[0, h] = _alibi_slope(h, nq) * dist
        bias_s[1, h] = _alibi_slope(h, nq) * dist_first


def _padded_kv(kvb, kvh):
    lane = lax.broadcasted_iota(jnp.int32, (2 * BLOCK, BLOCK), 1)
    mine = (lane < HEAD_DIM) if kvh == 0 else (lane >= HEAD_DIM)
    out = []
    for sec in (kvb[:, :BLOCK], kvb[:, BLOCK:]):
        m = jnp.where(mine, sec.astype(F32), 0.0)
        sw = pltpu.roll(m, HEAD_DIM, 1)
        pair = (m, sw) if kvh == 0 else (sw, m)
        out.append(tuple(p.astype(BF16) for p in pair))
    return out


def _attn_fwd(xh_in, g_in, b_in, x_in_b, kvs, wb, lay, bq, sinks, bo, tm):
    t, d = xh_in.shape
    nq = d // HEAD_DIM
    pairs_per_kv = (d // BLOCK) // N_KV_HEADS
    nbt = tm // BLOCK
    scale = HEAD_DIM ** -0.5

    def body(xh_ref, gi_ref, bi_ref, xb_ref, kv_ref, kvp_ref, wb_ref, bq_ref, sk_ref, bo_ref,
             q_ref, o_ref, lse_ref, xho_ref, rso_ref, wq_s, wo_s, kvall, q_s, o_s, bias_s):
        i = pl.program_id(0)

        @pl.when(i == 0)
        def _():
            _load_weight(wb_ref, lay, "wq", wq_s)
            _load_weight(wb_ref, lay, "wo", wo_s)
            _fill_alibi_bias(bias_s, nq)

        qv = ((_dot(xb_ref[...], _wfull(wq_s)) + bq_ref[...]) * scale).astype(BF16)
        q_s[...] = qv
        q_ref[...] = qv
        kvall[pl.ds(0, BLOCK), :] = kvp_ref[...]
        kvall[pl.ds(BLOCK, tm), :] = kv_ref[...]
        lane = lax.broadcasted_iota(jnp.int32, (BLOCK, BLOCK), 1)
        ones = jnp.ones((2 * BLOCK, BLOCK), BF16)

        def block(j, carry):
            r0 = pl.multiple_of(j * BLOCK, BLOCK)
            rows = pl.ds(r0, BLOCK)
            kvb = kvall[pl.ds(r0, 2 * BLOCK), :]
            first = (i * nbt + j == 0).astype(jnp.int32)
            pads = [_padded_kv(kvb, kvh) for kvh in range(N_KV_HEADS)]
            scores = []
            for a in range(d // BLOCK):
                kpad = pads[a // pairs_per_kv][0]
                qp = q_s[rows, a * BLOCK:(a + 1) * BLOCK]
                for e in range(2):
                    scores.append(_dot_nt(qp, kpad[e]) - bias_s[first, 2 * a + e])
            probs, inv = [], []
            lse_t = jnp.zeros((BLOCK, BLOCK), F32)
            for h in range(nq):
                sink = sk_ref[:, h:h + 1]
                m = jnp.maximum(jnp.max(scores[h], axis=-1, keepdims=True), sink)
                p = jnp.exp(scores[h] - m).astype(BF16)
                l = _dot(p, ones) + jnp.exp(sink - m)
                lse_t = jnp.where(lane == h, m + jnp.log(l), lse_t)
                probs.append(p)
                inv.append(1.0 / l)
            lse_ref[rows, :] = lse_t
            for a in range(d // BLOCK):
                vpad = pads[a // pairs_per_kv][1]
                opair = (_dot(probs[2 * a], vpad[0]) * inv[2 * a]
                         + _dot(probs[2 * a + 1], vpad[1]) * inv[2 * a + 1])
                o_s[rows, a * BLOCK:(a + 1) * BLOCK] = opair.astype(BF16)
            return carry

        lax.fori_loop(0, nbt, block, 0)
        ov = o_s[...]
        o_ref[...] = ov
        xin = xh_ref[...] * gi_ref[...] + bi_ref[...]
        xho, rso = _ln_fwd(ALPHA * xin + _dot(ov, _wfull(wo_s)) + bo_ref[...])
        xho_ref[...] = xho
        rso_ref[...] = rso

    return pl.pallas_call(
        body, name="attn_fwd", grid=(t // tm,),
        in_specs=[_tile(tm, d), _row(d), _row(d), _tile(tm, d), _tile(tm, 2 * BLOCK),
                  pl.BlockSpec((BLOCK, 2 * BLOCK), lambda i: (jnp.maximum(i * nbt - 1, 0), 0)),
                  ANY, _row(d), _row(nq), _row(d)],
        out_specs=[_tile(tm, d), _tile(tm, d), _tile(tm, BLOCK), _tile(tm, d), _tile(tm, 1)],
        out_shape=[jax.ShapeDtypeStruct((t, d), BF16), jax.ShapeDtypeStruct((t, d), BF16),
                   jax.ShapeDtypeStruct((t, BLOCK), F32), jax.ShapeDtypeStruct((t, d), F32),
                   jax.ShapeDtypeStruct((t, 1), F32)],
        scratch_shapes=[_wscratch(lay, "wq"), _wscratch(lay, "wo"),
                        pltpu.VMEM((BLOCK + tm, 2 * BLOCK), BF16), pltpu.VMEM((tm, d), BF16),
                        pltpu.VMEM((tm, d), BF16), pltpu.VMEM((2, nq, BLOCK, 2 * BLOCK), F32)],
        compiler_params=_params(),
    )(xh_in, g_in, b_in, x_in_b, kvs, kvs, wb, bq, sinks, bo)


def _attn_bwd(dz, q, o, lse, kvs, wb, lay, sinks, tm, hosted=None):
    t, d = dz.shape
    nq = d // HEAD_DIM
    pairs_per_kv = (d // BLOCK) // N_KV_HEADS
    nbt = tm // BLOCK
    scale = HEAD_DIM ** -0.5

    def body(dz_ref, q_ref, o_ref, lse_ref, kv_ref, kvp_ref, wb_ref, sk_ref,
             dzb_ref, dq_ref, dkc_ref, dkp_ref, st_ref, dsk_ref, wo_s, kvall, do_s, dq_s, bias_s):
        i = pl.program_id(0)

        @pl.when(i == 0)
        def _():
            _load_weight(wb_ref, lay, "wo", wo_s)
            _fill_alibi_bias(bias_s, nq)
            st_ref[...] = jnp.zeros(st_ref.shape, F32)
            dsk_ref[...] = jnp.zeros(dsk_ref.shape, F32)

        dzv = dz_ref[...]
        dzb = dzv.astype(BF16)
        dzb_ref[...] = dzb
        do_s[...] = _dot_nt(dzb, _wfull(wo_s))
        kvall[pl.ds(0, BLOCK), :] = kvp_ref[...]
        kvall[pl.ds(BLOCK, tm), :] = kv_ref[...]
        lane = lax.broadcasted_iota(jnp.int32, (BLOCK, BLOCK), 1)
        lane1 = lax.broadcasted_iota(jnp.int32, (1, BLOCK), 1)
        lane2 = lax.broadcasted_iota(jnp.int32, (2 * BLOCK, BLOCK), 1)
        halves = (lane < HEAD_DIM, lane >= HEAD_DIM)

        def block(j, carry):
            r0 = pl.multiple_of(j * BLOCK, BLOCK)
            rows = pl.ds(r0, BLOCK)
            kvb = kvall[pl.ds(r0, 2 * BLOCK), :]
            first = (i * nbt + j == 0).astype(jnp.int32)
            pads = [_padded_kv(kvb, kvh) for kvh in range(N_KV_HEADS)]
            scores, dps, dhs, qms, doms = [], [], [], [], []
            for a in range(d // BLOCK):
                kpad, vpad = pads[a // pairs_per_kv]
                cols = slice(a * BLOCK, (a + 1) * BLOCK)
                qp = q_ref[rows, cols]
                dop = do_s[rows, cols]
                dopb = dop.astype(BF16)
                prod = dop * o_ref[rows, cols].astype(F32)
                for e in range(2):
                    scores.append(_dot_nt(qp, kpad[e]) - bias_s[first, 2 * a + e])
                    dps.append(_dot_nt(dopb, vpad[e]))
                    dhs.append(jnp.sum(jnp.where(halves[e], prod, 0.0), axis=-1, keepdims=True))
                    qms.append(jnp.where(halves[e], qp, jnp.zeros_like(qp)))
                    doms.append(jnp.where(halves[e], dopb, jnp.zeros_like(dopb)))
            dss, pbs = [], []
            dsk_t = jnp.zeros((1, BLOCK), F32)
            for h in range(nq):
                lse_h = lse_ref[rows, h:h + 1]
                p = jnp.exp(scores[h] - lse_h)
                dss.append((p * (dps[h] - dhs[h])).astype(BF16))
                pbs.append(p.astype(BF16))
                dsink = -jnp.sum(jnp.exp(sk_ref[:, h:h + 1] - lse_h) * dhs[h], axis=0, keepdims=True)
                dsk_t = jnp.where(lane1 == h, dsink, dsk_t)
            dsk_ref[...] += dsk_t
            dsecs = []
            for kvh in range(N_KV_HEADS):
                kpad = pads[kvh][0]
                dk_acc = jnp.zeros((2 * BLOCK, BLOCK), F32)
                dv_acc = jnp.zeros((2 * BLOCK, BLOCK), F32)
                for a in range(kvh * pairs_per_kv, (kvh + 1) * pairs_per_kv):
                    dqp = _dot(dss[2 * a], kpad[0]) + _dot(dss[2 * a + 1], kpad[1])
                    dq_s[rows, a * BLOCK:(a + 1) * BLOCK] = dqp * scale
                    for e in range(2):
                        h = 2 * a + e
                        dk_acc = dk_acc + _dot_tn(dss[h], qms[h])
                        dv_acc = dv_acc + _dot_tn(pbs[h], doms[h])
                dsecs.append((dk_acc + pltpu.roll(dk_acc, HEAD_DIM, 1), dv_acc + pltpu.roll(dv_acc, HEAD_DIM, 1)))
            lo = lane2 < HEAD_DIM
            dkv = jnp.concatenate([jnp.where(lo, dsecs[0][0], dsecs[1][0]),
                                   jnp.where(lo, dsecs[0][1], dsecs[1][1])], axis=1)
            dkp_ref[rows, :] = dkv[:BLOCK]
            dkc_ref[rows, :] = dkv[BLOCK:]
            return carry

        lax.fori_loop(0, nbt, block, 0)
        dqv = dq_s[...]
        dq_ref[...] = dqv.astype(BF16)
        st_ref[0:1, :] += jnp.sum(dqv, axis=0, keepdims=True)
        st_ref[1:2, :] += jnp.sum(dzv, axis=0, keepdims=True)

    return _gridded_call(
        body, "attn_bwd", t // tm,
        [_tile(tm, d), _tile(tm, d), _tile(tm, d), _tile(tm, BLOCK), _tile(tm, 2 * BLOCK),
         pl.BlockSpec((BLOCK, 2 * BLOCK), lambda i: (jnp.maximum(i * nbt - 1, 0), 0)),
         ANY, _row(nq)],
        [_tile(tm, d), _tile(tm, d), _tile(tm, 2 * BLOCK), _tile(tm, 2 * BLOCK),
         _fixed(8, d), _row(BLOCK)],
        [jax.ShapeDtypeStruct((t, d), BF16), jax.ShapeDtypeStruct((t, d), BF16),
         jax.ShapeDtypeStruct((t, 2 * BLOCK), F32), jax.ShapeDtypeStruct((t, 2 * BLOCK), F32),
         jax.ShapeDtypeStruct((8, d), F32), jax.ShapeDtypeStruct((1, BLOCK), F32)],
        [_wscratch(lay, "wo"), pltpu.VMEM((BLOCK + tm, 2 * BLOCK), BF16),
         pltpu.VMEM((tm, d), F32), pltpu.VMEM((tm, d), F32),
         pltpu.VMEM((2, nq, BLOCK, 2 * BLOCK), F32)],
        (dz, q, o, lse, kvs, kvs, wb, sinks), hosted)


def _x2_bwd(dz, dq, dkc, dkp, xh_in, rs_in, g_in, wb, lay, wkv, tm):
    t, d = dz.shape
    nbt = tm // BLOCK
    nsteps = t // tm
    scale = HEAD_DIM ** -0.5

    def body(dz_ref, dq_ref, dkc_ref, dkp_ref, dkn_ref, xh_ref, rs_ref, gi_ref, wb_ref, wkv_ref,
             dkv_ref, dzp_ref, st_ref, dbkv_ref, wq_s):
        i = pl.program_id(0)

        @pl.when(i == 0)
        def _():
            _load_weight(wb_ref, lay, "wq", wq_s)
            st_ref[...] = jnp.zeros(st_ref.shape, F32)
            dbkv_ref[...] = jnp.zeros(dbkv_ref.shape, F32)

        nxt = jnp.where(i < nsteps - 1, dkn_ref[...], 0.0)
        if nbt > 1:
            shifted = jnp.concatenate([dkp_ref[pl.ds(BLOCK, tm - BLOCK), :], nxt], axis=0)
        else:
            shifted = nxt
        dkv = dkc_ref[...] + shifted
        dkvb = dkv.astype(BF16)
        dkv_ref[...] = dkvb
        dbkv_ref[...] += jnp.sum(dkv, axis=0, keepdims=True)
        dx = ALPHA * dz_ref[...] + _dot_nt(dq_ref[...], _wfull(wq_s)) + _dot_nt(dkvb, wkv_ref[...])
        dzp, dg, db = _ln_bwd(dx, xh_ref[...], rs_ref[...], gi_ref[...])
        dzp_ref[...] = dzp
        st_ref[0:1, :] += dg
        st_ref[1:2, :] += db

    del scale
    last = t // BLOCK - 1
    return pl.pallas_call(
        body, name="x2_bwd", grid=(nsteps,),
        in_specs=[_tile(tm, d), _tile(tm, d), _tile(tm, 2 * BLOCK), _tile(tm, 2 * BLOCK),
                  pl.BlockSpec((BLOCK, 2 * BLOCK), lambda i: (jnp.minimum((i + 1) * nbt, last), 0)),
                  _tile(tm, d), _tile(tm, 1), _row(d), ANY, _fixed(d, 2 * BLOCK)],
        out_specs=[_tile(tm, 2 * BLOCK), _tile(tm, d), _fixed(8, d), _row(2 * BLOCK)],
        out_shape=[jax.ShapeDtypeStruct((t, 2 * BLOCK), BF16), jax.ShapeDtypeStruct((t, d), F32),
                   jax.ShapeDtypeStruct((8, d), F32), jax.ShapeDtypeStruct((1, 2 * BLOCK), F32)],
        scratch_shapes=[_wscratch(lay, "wq")],
        compiler_params=_params(),
    )(dz, dq, dkc, dkp, dkp, xh_in, rs_in, g_in, wb, wkv)


def _tn_matmul(a, b, name, bm, tk):
    t, m = a.shape
    n = b.shape[1]
    ksteps = t // tk

    def body(a_ref, b_ref, o_ref):
        k = pl.program_id(1)
        part = _dot_tn(a_ref[...], b_ref[...])

        @pl.when(k == 0)
        def _():
            o_ref[...] = part

        @pl.when(k > 0)
        def _():
            o_ref[...] += part

    return pl.pallas_call(
        body, name=name, grid=(m // bm, ksteps),
        in_specs=[pl.BlockSpec((tk, bm), lambda j, k: (k, j)), pl.BlockSpec((tk, n), lambda j, k: (k, 0))],
        out_specs=pl.BlockSpec((bm, n), lambda j, k: (j, 0)),
        out_shape=jax.ShapeDtypeStruct((m, n), F32),
        compiler_params=pltpu.CompilerParams(dimension_semantics=("arbitrary", "arbitrary"),
                                             vmem_limit_bytes=VMEM_LIMIT),
    )(a, b)


def _all_gather(arrays, name):
    n = len(arrays)

    def body(*refs):
        ins, outs = refs[:n], refs[n:2 * n]
        send_sems, recv_sems, local_sems = refs[2 * n:]
        x, y, c = _me()
        me, sibling = (x, y, c), (x, y, 1 - c)
        chips = [(1 - x, y), (x, 1 - y), (1 - x, 1 - y)]

        def slot(ref, dev):
            return ref.at[4 * dev[0] + 2 * dev[1] + dev[2]]

        def copy(a, k, block, to, src=None):
            return pltpu.make_async_remote_copy(
                src_ref=slot(outs[a], block) if src is None else src, dst_ref=slot(outs[a], block),
                send_sem=send_sems.at[a, k], recv_sem=recv_sems.at[a, k], device_id=to, device_id_type=MESH)

        mine = [pltpu.make_async_copy(ins[a], slot(outs[a], me), local_sems.at[a]) for a in range(n)]
        for cp in mine:
            cp.start()
        first = []
        for a in range(n):
            first.append(copy(a, 0, me, sibling, src=ins[a]))
            first += [copy(a, 1 + j, me, (*chip, c), src=ins[a]) for j, chip in enumerate(chips)]
        for cp in first:
            cp.start()
        passed = []
        for a in range(n):
            for j, chip in enumerate(chips):
                copy(a, 1 + j, (*chip, c), me).wait_recv()
                cp = copy(a, 4 + j, (*chip, c), sibling)
                cp.start()
                passed.append(cp)
        for a in range(n):
            copy(a, 0, sibling, me).wait_recv()
            for j, chip in enumerate(chips):
                copy(a, 4 + j, (*chip, 1 - c), me).wait_recv()
        for cp in first + passed:
            cp.wait_send()
        for cp in mine:
            cp.wait()

    return pl.pallas_call(
        body, name=name, in_specs=[ANY] * n, out_specs=[ANY] * n,
        out_shape=[jax.ShapeDtypeStruct((N_DEV,) + a.shape, a.dtype) for a in arrays],
        scratch_shapes=[pltpu.SemaphoreType.DMA((n, 7)), pltpu.SemaphoreType.DMA((n, 7)),
                        pltpu.SemaphoreType.DMA((n,))],
    )(*arrays)


def _exchange(arrays, name):
    n = len(arrays)
    blocked = [a.ndim == 3 for a in arrays]

    def body(*refs):
        ins, outs = refs[:n], refs[n:2 * n]
        send_sems, recv_sems, local_sems = refs[2 * n:]
        me = _index(_me())

        def src(k, dev):
            return ins[k].at[dev] if blocked[k] else ins[k]

        local = [pltpu.make_async_copy(src(k, me), outs[k].at[me], local_sems.at[k]) for k in range(n)]
        sends, arrivals = [], []
        for k in range(n):
            for mask in range(1, N_DEV):
                peer = _peer(mask)
                sends.append(pltpu.make_async_remote_copy(
                    src_ref=src(k, _index(peer)), dst_ref=outs[k].at[me], send_sem=send_sems.at[k, mask - 1],
                    recv_sem=recv_sems.at[k, mask - 1], device_id=peer, device_id_type=MESH))
                arrivals.append(pltpu.make_async_remote_copy(
                    src_ref=src(k, me), dst_ref=outs[k].at[_index(peer)], send_sem=send_sems.at[k, mask - 1],
                    recv_sem=recv_sems.at[k, mask - 1], device_id=_me(), device_id_type=MESH))
        for cp in local + sends:
            cp.start()
        for cp in arrivals:
            cp.wait_recv()
        for cp in sends:
            cp.wait_send()
        for cp in local:
            cp.wait()

    return pl.pallas_call(
        body, name=name, in_specs=[ANY] * n, out_specs=[ANY] * n,
        out_shape=[jax.ShapeDtypeStruct((N_DEV,) + a.shape[-2:], a.dtype) for a in arrays],
        scratch_shapes=[pltpu.SemaphoreType.DMA((n, 7)), pltpu.SemaphoreType.DMA((n, 7)),
                        pltpu.SemaphoreType.DMA((n,))],
    )(*arrays)


def _adamw_sum(g8, w, m, v, name, tr):
    r, width = w.shape
    bc1 = 1.0 - ADAM_B1 ** ADAM_STEP
    bc2 = 1.0 - ADAM_B2 ** ADAM_STEP

    def body(g_ref, w_ref, m_ref, v_ref, go_ref, d_ref, mo_ref, vo_ref):
        g = g_ref[0]
        for s in range(1, N_DEV):
            g = g + g_ref[s]
        mn = ADAM_B1 * m_ref[...] + (1.0 - ADAM_B1) * g
        vn = ADAM_B2 * v_ref[...] + (1.0 - ADAM_B2) * (g * g)
        m_hat = mn / bc1
        v_hat = vn / bc2
        go_ref[...] = g
        d_ref[...] = -ADAM_LR * (m_hat / (jnp.sqrt(v_hat) + ADAM_EPS) + ADAM_WD * w_ref[...])
        mo_ref[...] = mn
        vo_ref[...] = vn

    spec = pl.BlockSpec((tr, width), lambda i: (i, 0))
    return pl.pallas_call(
        body, name=name, grid=(r // tr,),
        in_specs=[pl.BlockSpec((N_DEV, tr, width), lambda i: (0, i, 0)), spec, spec, spec],
        out_specs=[spec] * 4, out_shape=[jax.ShapeDtypeStruct((r, width), F32)] * 4,
        compiler_params=_params(),
    )(g8, w, m, v)


def _local_step(x, target, wba, shard_b, lay, sm, tm, tk):
    t, d = x.shape
    f = lay.f
    w_dw32 = jnp.concatenate([sm["w_dw"], jnp.zeros((HALO - CONV_WIDTH, d), F32)], axis=0)
    lmg, lmb, lfg, lfb = sm["ln_mix_g"], sm["ln_mix_b"], sm["ln_ffn_g"], sm["ln_ffn_b"]
    bkv = jnp.concatenate([sm["b_k"], sm["b_v"]], axis=1)
    bm_f = f // 2 if (f // 2) % 128 == 0 else f

    def exchange(group, grads):
        names = lay.GRADS[group]
        return _HostedExchange([grads[n].reshape(N_DEV, lay.n[n], d) for n in names],
                               [lay.roff[n] for n in names], lay.rrows[group])

    xb0, ag, xhc, rsc, xh1, rs1, wbb = _conv_fwd(x, wba, lay, w_dw32, sm["b_pw1"], sm["b_dw"], sm["cg"],
                                                 sm["cb"], sm["b_pw2"], lmg[0:1], lmb[0:1], tm,
                                                 hosted=_HostedGather(shard_b))
    wkv = wbb[:, lay.goff["wkv"]:lay.goff["wkv"] + lay.n["wkv"], :].reshape(d, 2 * BLOCK)
    x1b, hg0, hu0, xh2, rs2, x2b, kvs = _ffn_fwd(xh1, lmg[0:1], lmb[0:1], wbb, lay, 0, tm,
                                                kv=(lfg[0:1], lfb[0:1], wkv, bkv))
    q, o, lse, xh3, rs3 = _attn_fwd(xh2, lfg[0:1], lfb[0:1], x2b, kvs, wbb, lay, sm["b_q"], sm["sinks"],
                                    sm["b_o"], tm)
    x3b, hg1, hu1, dz4, st4, loss = _ffn_fwd(xh3, lmg[1:2], lmb[1:2], wbb, lay, 1, tm,
                                             loss=(lfg[1:2], lfb[1:2], target))

    dz4b, act1, dhg1, dhu1, dz3, st3 = _ffn_bwd(dz4, hg1, hu1, xh3, rs3, lmg[1:2], wbb, lay, 1, tm)
    g1 = {"gt1": _tn_matmul(dhg1, x3b, "dw_gate1", bm_f, tk), "ut1": _tn_matmul(dhu1, x3b, "dw_up1", bm_f, tk),
          "dn1": _tn_matmul(act1, dz4b, "dw_down1", bm_f, tk)}
    dz3b, dq, dkc, dkp, stq, dsinks, recv1 = _attn_bwd(dz3, q, o, lse, kvs, wbb, lay, sm["sinks"], tm,
                                                       hosted=exchange("r1", g1))
    dkv, dz2, st2, dbkv = _x2_bwd(dz3, dq, dkc, dkp, xh2, rs2, lfg[0:1], wbb, lay, wkv, tm)
    g2 = {"wq": _tn_matmul(x2b, dq, "dw_q", d, tk), "wo": _tn_matmul(o, dz3b, "dw_o", d, tk),
          "wkv": _tn_matmul(x2b, dkv, "dw_kv", d, tk)}
    dz2b, act0, dhg0, dhu0, dz1, st1, recv2 = _ffn_bwd(dz2, hg0, hu0, xh1, rs1, lmg[0:1], wbb, lay, 0, tm,
                                                       hosted=exchange("r2", g2))
    dz1b, s_act, dcv, stc = _conv_bwd1(dz1, xhc, rsc, wba, lay, sm["cg"], sm["cb"], tm)
    g3 = {"gt0": _tn_matmul(dhg0, x1b, "dw_gate0", bm_f, tk), "ut0": _tn_matmul(dhu0, x1b, "dw_up0", bm_f, tk),
          "dn0": _tn_matmul(act0, dz2b, "dw_down0", bm_f, tk), "pw2": _tn_matmul(s_act, dz1b, "dw_pw2", d, tk)}
    grad_x, dh1, dwdw, db1, recv3 = _conv_bwd2(dz1, dcv, ag, wba, lay, w_dw32, tm, hosted=exchange("r3", g3))
    g_pw1t = _tn_matmul(dh1, xb0, "dw_pw1", d, tk)
    received = {"r1": recv1, "r2": recv2, "r3": recv3}
    small = {
        "w_dw": dwdw[:CONV_WIDTH], "b_pw1": db1, "b_dw": stc[2:3], "cg": stc[0:1], "cb": stc[1:2],
        "b_pw2": stc[3:4], "b_k": dbkv[:, :BLOCK], "b_v": dbkv[:, BLOCK:], "b_q": stq[0:1],
        "sinks": dsinks[:, :d // HEAD_DIM],
        "b_o": stq[1:2],
        "ln_mix_g": jnp.concatenate([st1[0:1], st3[0:1]], axis=0),
        "ln_mix_b": jnp.concatenate([st1[1:2], st3[1:2]], axis=0),
        "ln_ffn_g": jnp.concatenate([st2[0:1], st4[0:1]], axis=0),
        "ln_ffn_b": jnp.concatenate([st2[1:2], st4[1:2]], axis=0),
    }
    return loss[0, 0], grad_x, received, g_pw1t, small


SP_ROWS = 40
SP_BDW, SP_CG, SP_CB, SP_BPW2, SP_BPW1 = 32, 33, 34, 35, 36
RP_NAMES = ("ln_mix_g", "ln_mix_b", "ln_ffn_g", "ln_ffn_b", "b_q", "b_o", "b_k", "b_v", "sinks")


def _row_forms(d, pw1, pw2, wq, wo, gate, up, down, wk, wv):
    rf = {"pw1t": pw1[0].T, "pw2": pw2[0], "wq": wq[0], "wo": wo[0],
          "wkv": jnp.concatenate([wk, wv], axis=1).reshape(-1, d)}
    for l in range(DEPTH):
        rf.update({f"gt{l}": gate[l].T, f"ut{l}": up[l].T, f"dn{l}": down[l]})
    return rf


def _from_row_forms(d, rf):
    kvw = rf["wkv"].reshape(d // N_DEV, 2 * BLOCK)
    return dict(
        pw1=rf["pw1t"].T[None], pw2=rf["pw2"][None], wq=rf["wq"][None], wo=rf["wo"][None],
        gate=jnp.stack([rf[f"gt{l}"].T for l in range(DEPTH)]),
        up=jnp.stack([rf[f"ut{l}"].T for l in range(DEPTH)]),
        down=jnp.stack([rf[f"dn{l}"] for l in range(DEPTH)]),
        wk=kvw[:, :BLOCK], wv=kvw[:, BLOCK:])


def _pack_rows(rf, names):
    return jnp.concatenate([rf[n] for n in names], axis=0)


def _pack_small(w_dw, b_dw, cg, cb, b_pw2, b_pw1):
    cw = b_dw.shape[1]
    z = jnp.zeros((1, cw), F32)
    return jnp.concatenate([w_dw[0], z, b_dw, cg, cb, b_pw2, b_pw1.reshape(2, cw), z, z], axis=0)


def _unpack_small(p):
    cw = p.shape[1]
    return dict(w_dw=p[None, :CONV_WIDTH], b_dw=p[SP_BDW:SP_BDW + 1], cg=p[SP_CG:SP_CG + 1],
                cb=p[SP_CB:SP_CB + 1], b_pw2=p[SP_BPW2:SP_BPW2 + 1],
                b_pw1=p[SP_BPW1:SP_BPW1 + 2].reshape(1, 2 * cw))


def _small_full(g):
    d = N_DEV * g.shape[2]

    def wide(r0, n=1):
        return jnp.transpose(g[:, r0:r0 + n], (1, 0, 2)).reshape(n, d)

    return dict(w_dw=wide(0, CONV_WIDTH), b_dw=wide(SP_BDW), cg=wide(SP_CG), cb=wide(SP_CB),
                b_pw2=wide(SP_BPW2), b_pw1=g[:, SP_BPW1:SP_BPW1 + 2].reshape(1, 2 * d))


def _small_grad_blocks(sg):
    cw = sg["b_dw"].shape[1] // N_DEV

    def narrow(a):
        return jnp.transpose(a.reshape(a.shape[0], N_DEV, cw), (1, 0, 2))

    z = jnp.zeros((N_DEV, 1, cw), F32)
    return jnp.concatenate([narrow(sg["w_dw"]), z, narrow(sg["b_dw"]), narrow(sg["cg"]), narrow(sg["cb"]),
                            narrow(sg["b_pw2"]), sg["b_pw1"].reshape(N_DEV, 2, cw), z, z], axis=1)


def _pack_rep(vals):
    parts = []
    for name in RP_NAMES:
        a = vals[name].reshape(-1)
        pad = -a.shape[0] % 128
        parts.append(jnp.concatenate([a, jnp.zeros((pad,), F32)]).reshape(-1, 128))
    rows = sum(p.shape[0] for p in parts)
    parts.append(jnp.zeros((-rows % 8, 128), F32))
    return jnp.concatenate(parts, axis=0)


def _unpack_rep(p, shapes):
    out, r = {}, 0
    for name in RP_NAMES:
        n = 1
        for s in shapes[name]:
            n *= s
        rows = -(-n // 128)
        out[name] = p[r:r + rows].reshape(-1)[:n].reshape(shapes[name])
        r += rows
    return out


def kernel(x, conv_w_pw1, conv_b_pw1, conv_w_dw, conv_b_dw, conv_ln_g, conv_ln_b, conv_w_pw2, conv_b_pw2, kv_w_k, kv_b_k, kv_w_v, kv_b_v, attn_w_q, attn_b_q, attn_sinks, attn_w_o, attn_b_o, ffn_w_gate, ffn_w_up, ffn_w_down, ln_mix_g, ln_mix_b, ln_ffn_g, ln_ffn_b, loss_target, m_conv_w_pw1, m_conv_b_pw1, m_conv_w_dw, m_conv_b_dw, m_conv_ln_g, m_conv_ln_b, m_conv_w_pw2, m_conv_b_pw2, m_kv_w_k, m_kv_b_k, m_kv_w_v, m_kv_b_v, m_attn_w_q, m_attn_b_q, m_attn_sinks, m_attn_w_o, m_attn_b_o, m_ffn_w_gate, m_ffn_w_up, m_ffn_w_down, m_ln_mix_g, m_ln_mix_b, m_ln_ffn_g, m_ln_ffn_b, v_conv_w_pw1, v_conv_b_pw1, v_conv_w_dw, v_conv_b_dw, v_conv_ln_g, v_conv_ln_b, v_conv_w_pw2, v_conv_b_pw2, v_kv_w_k, v_kv_b_k, v_kv_w_v, v_kv_b_v, v_attn_w_q, v_attn_b_q, v_attn_sinks, v_attn_w_o, v_attn_b_o, v_ffn_w_gate, v_ffn_w_up, v_ffn_w_down, v_ln_mix_g, v_ln_mix_b, v_ln_ffn_g, v_ln_ffn_b):
    t, d = x.shape[1], x.shape[2]
    f = ffn_w_gate.shape[2] * N_DEV
    lay = _Layout(d, f)
    tm, tk = 256, 1024

    rep_shapes = dict(ln_mix_g=ln_mix_g.shape, ln_mix_b=ln_mix_b.shape, ln_ffn_g=ln_ffn_g.shape,
                      ln_ffn_b=ln_ffn_b.shape, b_q=attn_b_q.shape, b_o=attn_b_o.shape, b_k=kv_b_k.shape,
                      b_v=kv_b_v.shape, sinks=attn_sinks.shape)

    def rep_pack(lmg, lmb, lfg, lfb, bq, bo, bk, bv, sk):
        return _pack_rep(dict(ln_mix_g=lmg, ln_mix_b=lmb, ln_ffn_g=lfg, ln_ffn_b=lfb, b_q=bq, b_o=bo,
                              b_k=bk, b_v=bv, sinks=sk))

    w_rf = _row_forms(d, conv_w_pw1, conv_w_pw2, attn_w_q, attn_w_o, ffn_w_gate, ffn_w_up, ffn_w_down, kv_w_k, kv_w_v)
    m_rf = _row_forms(d, m_conv_w_pw1, m_conv_w_pw2, m_attn_w_q, m_attn_w_o, m_ffn_w_gate, m_ffn_w_up, m_ffn_w_down, m_kv_w_k, m_kv_w_v)
    v_rf = _row_forms(d, v_conv_w_pw1, v_conv_w_pw2, v_attn_w_q, v_attn_w_o, v_ffn_w_gate, v_ffn_w_up, v_ffn_w_down, v_kv_w_k, v_kv_w_v)
    w_small = _pack_small(conv_w_dw, conv_b_dw, conv_ln_g, conv_ln_b, conv_b_pw2, conv_b_pw1)
    m_small = _pack_small(m_conv_w_dw, m_conv_b_dw, m_conv_ln_g, m_conv_ln_b, m_conv_b_pw2, m_conv_b_pw1)
    v_small = _pack_small(v_conv_w_dw, v_conv_b_dw, v_conv_ln_g, v_conv_ln_b, v_conv_b_pw2, v_conv_b_pw1)
    w_rep = rep_pack(ln_mix_g, ln_mix_b, ln_ffn_g, ln_ffn_b, attn_b_q, attn_b_o, kv_b_k, kv_b_v, attn_sinks)
    m_rep = rep_pack(m_ln_mix_g, m_ln_mix_b, m_ln_ffn_g, m_ln_ffn_b, m_attn_b_q, m_attn_b_o, m_kv_b_k, m_kv_b_v, m_attn_sinks)
    v_rep = rep_pack(v_ln_mix_g, v_ln_mix_b, v_ln_ffn_g, v_ln_ffn_b, v_attn_b_q, v_attn_b_o, v_kv_b_k, v_kv_b_v, v_attn_sinks)

    wba, smg = _all_gather([_pack_rows(w_rf, lay.GATHER["a"]).astype(BF16), w_small], "gather_conv_weights")
    shard_b = _pack_rows(w_rf, lay.GATHER["b"]).astype(BF16)
    sm = _small_full(smg)
    sm.update(ln_mix_g=ln_mix_g, ln_mix_b=ln_mix_b, ln_ffn_g=ln_ffn_g, ln_ffn_b=ln_ffn_b, b_q=attn_b_q,
              b_o=attn_b_o, sinks=attn_sinks, b_k=kv_b_k.reshape(1, -1), b_v=kv_b_v.reshape(1, -1))

    loss_part, grad_x, received, g_pw1t, gsmall = _local_step(x[0], loss_target[0], wba, shard_b, lay, sm, tm, tk)
    loss = lax.psum(loss_part, ("x", "y", "c"))

    received["r4"], g8_small, g8_rep = _exchange(
        [g_pw1t.reshape(N_DEV, lay.n["pw1t"], d), _small_grad_blocks(gsmall), _pack_rep(gsmall)], "exchange_last_grads")

    big_rf = [{}, {}, {}, {}]
    for group, names in lay.GRADS.items():
        rows = lay.rrows[group]
        tr = max(r for r in range(8, 129, 8) if rows % r == 0)
        res = _adamw_sum(received[group], _pack_rows(w_rf, names), _pack_rows(m_rf, names), _pack_rows(v_rf, names),
                         f"adamw_{group}", tr)
        for out_rf, a in zip(big_rf, res):
            for n in names:
                out_rf[n] = a[lay.roff[n]:lay.roff[n] + lay.n[n]]
    big_out = [_from_row_forms(d, rf) for rf in big_rf]
    small_out = [_unpack_small(a) for a in _adamw_sum(g8_small, w_small, m_small, v_small, "adamw_small", SP_ROWS)]
    rep_out = [_unpack_rep(a, rep_shapes)
               for a in _adamw_sum(g8_rep, w_rep, m_rep, v_rep, "adamw_rep", w_rep.shape[0])]

    outs = [loss, grad_x[None]]
    for b, s, r in zip(big_out, small_out, rep_out):
        outs += [b["pw1"], s["b_pw1"], s["w_dw"], s["b_dw"], s["cg"], s["cb"], b["pw2"], s["b_pw2"],
                 b["wk"], r["b_k"], b["wv"], r["b_v"], b["wq"], r["b_q"], r["sinks"], b["wo"], r["b_o"],
                 b["gate"], b["up"], b["down"], r["ln_mix_g"], r["ln_mix_b"], r["ln_ffn_g"], r["ln_ffn_b"]]
    return tuple(outs)
```

```python
import functools

import jax
import jax.numpy as jnp
from jax import lax
from jax.experimental import pallas as pl
from jax.experimental.pallas import tpu as pltpu

F32 = jnp.float32
BF16 = jnp.bfloat16

N_DEV = 8
HEAD_DIM = 64
N_KV_HEADS = 2
BLOCK = 128
CONV_WIDTH = 31
HALO = 32
ALIBI_MAX = 8.0
DEPTH = 2
ALPHA = (2.0 * DEPTH) ** 0.25
LN_EPS = 1e-5
MASKED_DIST = 1e32
ADAM_LR = 0.001
ADAM_B1 = 0.9
ADAM_B2 = 0.999
ADAM_EPS = 1e-08
ADAM_WD = 0.01
ADAM_STEP = 10
VMEM_LIMIT = 56 * 1024 * 1024
MESH = pl.DeviceIdType.MESH


def _dot(a, b):
    return jnp.dot(a, b, preferred_element_type=F32)


def _dot_nt(a, b):
    return lax.dot_general(a, b, (((1,), (1,)), ((), ())), preferred_element_type=F32)


def _dot_tn(a, b):
    return lax.dot_general(a, b, (((0,), (0,)), ((), ())), preferred_element_type=F32)


def _sigmoid(v):
    return 1.0 / (1.0 + jnp.exp(-v))


def _ln_fwd(z):
    mu = jnp.mean(z, axis=-1, keepdims=True)
    zc = z - mu
    var = jnp.mean(zc * zc, axis=-1, keepdims=True)
    rstd = lax.rsqrt(var + LN_EPS)
    return zc * rstd, rstd


def _ln_bwd(dout, xh, rstd, g):
    dxh = dout * g
    m1 = jnp.mean(dxh, axis=-1, keepdims=True)
    m2 = jnp.mean(dxh * xh, axis=-1, keepdims=True)
    dz = rstd * (dxh - m1 - xh * m2)
    return dz, jnp.sum(dout * xh, axis=0, keepdims=True), jnp.sum(dout, axis=0, keepdims=True)


def _params(vmem=VMEM_LIMIT):
    return pltpu.CompilerParams(dimension_semantics=("arbitrary",), vmem_limit_bytes=vmem)


def _row(d):
    return pl.BlockSpec((1, d), lambda i: (0, 0))


def _tile(tm, d):
    return pl.BlockSpec((tm, d), lambda i: (i, 0))


def _fixed(r, d):
    return pl.BlockSpec((r, d), lambda i: (0, 0))


ANY = pl.BlockSpec(memory_space=pl.ANY)


class _Layout:
    GATHER = {"a": ("pw1t", "pw2"),
              "b": ("wq", "wo", "gt0", "ut0", "dn0", "gt1", "ut1", "dn1", "wkv")}
    GRADS = {"r1": ("gt1", "ut1", "dn1"), "r2": ("wq", "wo", "wkv"), "r3": ("gt0", "ut0", "dn0", "pw2"),
             "r4": ("pw1t",)}

    def __init__(self, d, f):
        self.d, self.f = d, f
        self.n = {"pw1t": 2 * d // N_DEV, "pw2": d // N_DEV, "wq": d // N_DEV, "wo": d // N_DEV,
                  "wkv": (d // N_DEV) * 2 * BLOCK // d}
        for l in range(DEPTH):
            self.n.update({f"gt{l}": f // N_DEV, f"ut{l}": f // N_DEV, f"dn{l}": f // N_DEV})
        self.goff, self.grows = self._offsets(self.GATHER)
        self.roff, self.rrows = self._offsets(self.GRADS)

    def _offsets(self, groups):
        off, rows = {}, {}
        for g, names in groups.items():
            r = 0
            for name in names:
                off[name] = r
                r += self.n[name]
            rows[g] = r
        return off, rows


def _load_weight(wb_ref, lay, name, dst):
    n = lay.n[name]
    for p in range(N_DEV):
        pltpu.sync_copy(wb_ref.at[p, pl.ds(lay.goff[name], n), :], dst.at[pl.ds(p * n, n), :])


def _wscratch(lay, name):
    return pltpu.VMEM((N_DEV * lay.n[name], lay.d), BF16)


def _wfull(ref):
    return ref[...]


def _wrows(ref, r0, nrows):
    return ref[r0:r0 + nrows, :]


def _me():
    return lax.axis_index("x"), lax.axis_index("y"), lax.axis_index("c")


def _peer(mask):
    x, y, c = _me()
    return (1 - x if mask & 4 else x, 1 - y if mask & 2 else y, 1 - c if mask & 1 else c)


def _index(dev):
    return 4 * dev[0] + 2 * dev[1] + dev[2]


class _HostedGather:
    def __init__(self, array):
        self.arrays = [array]
        self.out_shape = jax.ShapeDtypeStruct((N_DEV,) + array.shape, array.dtype)

    def scratch(self):
        return [pltpu.SemaphoreType.DMA((7,)), pltpu.SemaphoreType.DMA((7,)), pltpu.SemaphoreType.DMA(())]

    def _copies(self, ins, out, send_sems, recv_sems, local_sem):
        x, y, c = _me()
        me, sibling = (x, y, c), (x, y, 1 - c)
        chips = [(1 - x, y), (x, 1 - y), (1 - x, 1 - y)]

        def copy(k, block, to, src=None):
            rows = out.at[_index(block)]
            return pltpu.make_async_remote_copy(
                src_ref=rows if src is None else src, dst_ref=rows, send_sem=send_sems.at[k],
                recv_sem=recv_sems.at[k], device_id=to, device_id_type=MESH)

        return dict(
            mine=lambda: pltpu.make_async_copy(ins[0], out.at[_index(me)], local_sem),
            first=lambda: [copy(0, me, sibling, src=ins[0])] + [copy(1 + j, me, (*chip, c), src=ins[0])
                                                                for j, chip in enumerate(chips)],
            over_ici=lambda: [copy(1 + j, (*chip, c), me) for j, chip in enumerate(chips)],
            passed=lambda: [copy(4 + j, (*chip, c), sibling) for j, chip in enumerate(chips)],
            from_sibling=lambda: [copy(0, sibling, me)] + [copy(4 + j, (*chip, 1 - c), me)
                                                           for j, chip in enumerate(chips)])

    def start(self, *refs):
        cp = self._copies(*refs)
        cp["mine"]().start()
        for c in cp["first"]():
            c.start()

    def middle(self, *refs):
        cp = self._copies(*refs)
        for arrived, onward in zip(cp["over_ici"](), cp["passed"]()):
            arrived.wait_recv()
            onward.start()

    def finish(self, *refs):
        cp = self._copies(*refs)
        for c in cp["from_sibling"]():
            c.wait_recv()
        for c in cp["first"]() + cp["passed"]():
            c.wait_send()
        cp["mine"]().wait()


class _HostedExchange:
    def __init__(self, arrays, offsets, rows):
        self.arrays, self.offsets = list(arrays), list(offsets)
        self.out_shape = jax.ShapeDtypeStruct((N_DEV, rows, arrays[0].shape[2]), arrays[0].dtype)

    def scratch(self):
        n = len(self.arrays)
        return [pltpu.SemaphoreType.DMA((n, 7)), pltpu.SemaphoreType.DMA((n, 7)), pltpu.SemaphoreType.DMA((n,))]

    def _copies(self, ins, out, send_sems, recv_sems, local_sems):
        me = _index(_me())

        def dst(k, src_dev):
            return out.at[src_dev, pl.ds(self.offsets[k], self.arrays[k].shape[1]), :]

        pairs = [(k, mask) for k in range(len(self.arrays)) for mask in range(1, N_DEV)]

        def local():
            return [pltpu.make_async_copy(ins[k].at[me], dst(k, me), local_sems.at[k])
                    for k in range(len(self.arrays))]

        def sends():
            return [pltpu.make_async_remote_copy(
                src_ref=ins[k].at[_index(_peer(mask))], dst_ref=dst(k, me), send_sem=send_sems.at[k, mask - 1],
                recv_sem=recv_sems.at[k, mask - 1], device_id=_peer(mask), device_id_type=MESH)
                for k, mask in pairs]

        def arrivals():
            return [pltpu.make_async_remote_copy(
                src_ref=ins[k].at[me], dst_ref=dst(k, _index(_peer(mask))), send_sem=send_sems.at[k, mask - 1],
                recv_sem=recv_sems.at[k, mask - 1], device_id=_me(), device_id_type=MESH)
                for k, mask in pairs]

        return local, sends, arrivals

    def start(self, *refs):
        local, sends, _ = self._copies(*refs)
        for c in local() + sends():
            c.start()

    def middle(self, *refs):
        pass

    def finish(self, *refs):
        local, sends, arrivals = self._copies(*refs)
        for c in arrivals():
            c.wait_recv()
        for c in sends():
            c.wait_send()
        for c in local():
            c.wait()


def _gridded_call(body, name, nsteps, in_specs, out_specs, out_shape, scratch, args, hosted=None):
    if hosted is None:
        return pl.pallas_call(body, name=name, grid=(nsteps,), in_specs=in_specs, out_specs=out_specs,
                              out_shape=out_shape, scratch_shapes=scratch, compiler_params=_params())(*args)
    n_in, n_out, n_scr, h_in = len(in_specs), len(out_specs), len(scratch), len(hosted.arrays)

    def with_hosted(*refs):
        a = n_in + h_in
        b = a + n_out
        e = b + 1 + n_scr
        comm = (refs[n_in:a], refs[b], refs[e], refs[e + 1], refs[e + 2])
        i = pl.program_id(0)

        @pl.when(i == 0)
        def _():
            hosted.start(*comm)

        body(*refs[:n_in], *refs[a:b], *refs[b + 1:e])

        @pl.when(i == nsteps // 2)
        def _():
            hosted.middle(*comm)

        @pl.when(i == nsteps - 1)
        def _():
            hosted.finish(*comm)

    return pl.pallas_call(
        with_hosted, name=name, grid=(nsteps,), in_specs=list(in_specs) + [ANY] * h_in,
        out_specs=list(out_specs) + [ANY], out_shape=list(out_shape) + [hosted.out_shape],
        scratch_shapes=list(scratch) + hosted.scratch(), compiler_params=_params(),
    )(*args, *hosted.arrays)


CONV_RB = 64
CONV_LC = 128
CONV_WIN = CONV_RB + HALO + 8
CONV_MC = 256


def _shifted(win, r):
    return win if r == 0 else pltpu.roll(win, win.shape[0] - r, 0)


def _conv_fwd(x, wb, lay, w_dw, b_pw1, b_dw, cg, cb, b_pw2, lg, lb, tm, hosted=None):
    t, d = x.shape
    nsteps = t // tm

    def body(x_ref, xh_ref, wb_ref, wdw_ref, b1_ref, bdw_ref, cg_ref, cb_ref, b2_ref, lg_ref, lb_ref,
             xb_ref, ag_ref, xhc_ref, rsc_ref, xh1_ref, rs1_ref, w1_s, w2_s, ubuf, cv_s):
        i = pl.program_id(0)

        @pl.when(i == 0)
        def _():
            _load_weight(wb_ref, lay, "pw1t", w1_s)
            _load_weight(wb_ref, lay, "pw2", w2_s)
            ubuf[pl.ds(HALO + tm, 8), :] = jnp.zeros((8, d), F32)

        xv = x_ref[...]
        xb = xv.astype(BF16)
        xb_ref[...] = xb
        xcat = jnp.concatenate([xh_ref[...].astype(BF16), xb], axis=0)
        for mc in range(d // CONV_MC):
            c0 = mc * CONV_MC
            acols, gcols = slice(c0, c0 + CONV_MC), slice(d + c0, d + c0 + CONV_MC)
            ha = _dot_nt(xcat, _wrows(w1_s, c0, CONV_MC)) + b1_ref[:, acols]
            hg = _dot_nt(xcat, _wrows(w1_s, d + c0, CONV_MC)) + b1_ref[:, gcols]
            ag_ref[:, acols] = ha[HALO:].astype(BF16)
            ag_ref[:, gcols] = hg[HALO:].astype(BF16)
            u = ha * _sigmoid(hg)
            ubuf[0:HALO, acols] = jnp.where(i > 0, u[:HALO], 0.0)
            ubuf[HALO:HALO + tm, acols] = u[HALO:]
            for rb in range(tm // CONV_RB):
                t0 = rb * CONV_RB
                for lc in range(CONV_MC // CONV_LC):
                    lanes = slice(c0 + lc * CONV_LC, c0 + (lc + 1) * CONV_LC)
                    win = ubuf[t0:t0 + CONV_WIN, lanes]
                    acc = jnp.zeros((CONV_RB, CONV_LC), F32)
                    for r in range(8):
                        wr = _shifted(win, r)
                        for k in range(CONV_WIDTH):
                            s = HALO - (CONV_WIDTH - 1) + k
                            if s % 8 == r:
                                q = 8 * (s // 8)
                                acc = acc + wr[q:q + CONV_RB] * wdw_ref[k:k + 1, lanes]
                    cv_s[t0:t0 + CONV_RB, lanes] = acc
        cv = cv_s[...] + bdw_ref[...]
        xhc, rsc = _ln_fwd(cv)
        xhc_ref[...] = xhc
        rsc_ref[...] = rsc
        n = xhc * cg_ref[...] + cb_ref[...]
        s_act = n * _sigmoid(n)
        m = _dot(s_act.astype(BF16), _wfull(w2_s)) + b2_ref[...]
        xh1, rs1 = _ln_fwd(ALPHA * xv + m)
        xh1_ref[...] = xh1
        rs1_ref[...] = rs1

    hb = tm // HALO
    return _gridded_call(
        body, "conv_fwd", nsteps,
        [_tile(tm, d), pl.BlockSpec((HALO, d), lambda i: (jnp.maximum(i * hb - 1, 0), 0)), ANY,
         _fixed(HALO, d), _row(2 * d), _row(d), _row(d), _row(d), _row(d), _row(d), _row(d)],
        [_tile(tm, d), _tile(tm, 2 * d), _tile(tm, d), _tile(tm, 1), _tile(tm, d), _tile(tm, 1)],
        [jax.ShapeDtypeStruct((t, d), BF16), jax.ShapeDtypeStruct((t, 2 * d), BF16),
         jax.ShapeDtypeStruct((t, d), F32), jax.ShapeDtypeStruct((t, 1), F32),
         jax.ShapeDtypeStruct((t, d), F32), jax.ShapeDtypeStruct((t, 1), F32)],
        [_wscratch(lay, "pw1t"), _wscratch(lay, "pw2"),
         pltpu.VMEM((HALO + tm + 8, d), F32), pltpu.VMEM((tm, d), F32)],
        (x, x, wb, w_dw, b_pw1, b_dw, cg, cb, b_pw2, lg, lb), hosted)


def _conv_bwd1(dz1, xhc, rsc, wb, lay, cg, cb, tm):
    t, d = dz1.shape

    def body(dz_ref, xhc_ref, rsc_ref, wb_ref, cg_ref, cb_ref, dzb_ref, s_ref, dcv_ref, st_ref, w2_s):
        i = pl.program_id(0)

        @pl.when(i == 0)
        def _():
            _load_weight(wb_ref, lay, "pw2", w2_s)
            st_ref[...] = jnp.zeros(st_ref.shape, F32)

        dz = dz_ref[...]
        dzb = dz.astype(BF16)
        dzb_ref[...] = dzb
        xhc_v = xhc_ref[...]
        n = xhc_v * cg_ref[...] + cb_ref[...]
        sg = _sigmoid(n)
        s_ref[...] = (n * sg).astype(BF16)
        ds = _dot_nt(dzb, _wfull(w2_s))
        dn = ds * (sg * (1.0 + n * (1.0 - sg)))
        dcv, dg, db = _ln_bwd(dn, xhc_v, rsc_ref[...], cg_ref[...])
        dcv_ref[...] = dcv
        st_ref[0:1, :] += dg
        st_ref[1:2, :] += db
        st_ref[2:3, :] += jnp.sum(dcv, axis=0, keepdims=True)
        st_ref[3:4, :] += jnp.sum(dz, axis=0, keepdims=True)

    return pl.pallas_call(
        body, name="conv_bwd1", grid=(t // tm,),
        in_specs=[_tile(tm, d), _tile(tm, d), _tile(tm, 1), ANY, _row(d), _row(d)],
        out_specs=[_tile(tm, d), _tile(tm, d), _tile(tm, d), _fixed(8, d)],
        out_shape=[jax.ShapeDtypeStruct((t, d), BF16), jax.ShapeDtypeStruct((t, d), BF16),
                   jax.ShapeDtypeStruct((t, d), F32), jax.ShapeDtypeStruct((8, d), F32)],
        scratch_shapes=[_wscratch(lay, "pw2")],
        compiler_params=_params(),
    )(dz1, xhc, rsc, wb, cg, cb)


def _conv_bwd2(dz1, dcv, ag, wb, lay, w_dw, tm, hosted=None):
    t, d = dz1.shape
    nsteps = t // tm

    def body(dz_ref, dcv_ref, dcvn_ref, ag_ref, agp_ref, wb_ref, wdw_ref,
             gx_ref, dh_ref, dw_ref, db1_ref, w1_s, ubuf, dbuf, du_s, dwacc):
        i = pl.program_id(0)

        @pl.when(i == 0)
        def _():
            _load_weight(wb_ref, lay, "pw1t", w1_s)
            ubuf[pl.ds(HALO + tm, 8), :] = jnp.zeros((8, d), F32)
            dbuf[pl.ds(HALO + tm, 8), :] = jnp.zeros((8, d), F32)
            dwacc[...] = jnp.zeros(dwacc.shape, F32)
            db1_ref[...] = jnp.zeros(db1_ref.shape, F32)

        dbuf[0:tm, :] = dcv_ref[...]
        dbuf[tm:tm + HALO, :] = jnp.where(i < nsteps - 1, dcvn_ref[...], 0.0)
        gx = ALPHA * dz_ref[...]
        for mc in range(d // CONV_MC):
            c0 = mc * CONV_MC
            acols, gcols = slice(c0, c0 + CONV_MC), slice(d + c0, d + c0 + CONV_MC)
            a = ag_ref[:, acols].astype(F32)
            sg = _sigmoid(ag_ref[:, gcols].astype(F32))
            ubuf[HALO:HALO + tm, acols] = a * sg
            up = agp_ref[:, acols].astype(F32) * _sigmoid(agp_ref[:, gcols].astype(F32))
            ubuf[0:HALO, acols] = jnp.where(i > 0, up, 0.0)
            for rb in range(tm // CONV_RB):
                t0 = rb * CONV_RB
                for lc in range(CONV_MC // CONV_LC):
                    lanes = slice(c0 + lc * CONV_LC, c0 + (lc + 1) * CONV_LC)
                    dwin = dbuf[t0:t0 + CONV_WIN, lanes]
                    uwin = ubuf[t0:t0 + CONV_WIN, lanes]
                    dcur = dwin[0:CONV_RB]
                    acc = jnp.zeros((CONV_RB, CONV_LC), F32)
                    for r in range(8):
                        dr = _shifted(dwin, r)
                        ur = _shifted(uwin, r)
                        for k in range(CONV_WIDTH):
                            sd = CONV_WIDTH - 1 - k
                            if sd % 8 == r:
                                q = 8 * (sd // 8)
                                acc = acc + dr[q:q + CONV_RB] * wdw_ref[k:k + 1, lanes]
                            su = HALO - (CONV_WIDTH - 1) + k
                            if su % 8 == r:
                                q = 8 * (su // 8)
                                prod = dcur * ur[q:q + CONV_RB]
                                part = prod[0:8]
                                for j in range(1, CONV_RB // 8):
                                    part = part + prod[8 * j:8 * j + 8]
                                dwacc[k, :, lanes] += part
                    du_s[t0:t0 + CONV_RB, lanes] = acc
            du = du_s[:, acols]
            da = du * sg
            dg = du * a * sg * (1.0 - sg)
            dab, dgb = da.astype(BF16), dg.astype(BF16)
            dh_ref[:, acols] = dab
            dh_ref[:, gcols] = dgb
            db1_ref[:, acols] += jnp.sum(da, axis=0, keepdims=True)
            db1_ref[:, gcols] += jnp.sum(dg, axis=0, keepdims=True)
            gx = gx + _dot(dab, _wrows(w1_s, c0, CONV_MC)) + _dot(dgb, _wrows(w1_s, d + c0, CONV_MC))
        gx_ref[...] = gx

        @pl.when(i == nsteps - 1)
        def _():
            dw_ref[...] = jnp.sum(dwacc[...], axis=1)

    hb = tm // HALO
    last = t // HALO - 1
    return _gridded_call(
        body, "conv_bwd2", nsteps,
        [_tile(tm, d), _tile(tm, d),
         pl.BlockSpec((HALO, d), lambda i: (jnp.minimum((i + 1) * hb, last), 0)),
         _tile(tm, 2 * d),
         pl.BlockSpec((HALO, 2 * d), lambda i: (jnp.maximum(i * hb - 1, 0), 0)),
         ANY, _fixed(HALO, d)],
        [_tile(tm, d), _tile(tm, 2 * d), _fixed(HALO, d), _row(2 * d)],
        [jax.ShapeDtypeStruct((t, d), F32), jax.ShapeDtypeStruct((t, 2 * d), BF16),
         jax.ShapeDtypeStruct((HALO, d), F32), jax.ShapeDtypeStruct((1, 2 * d), F32)],
        [_wscratch(lay, "pw1t"), pltpu.VMEM((HALO + tm + 8, d), F32),
         pltpu.VMEM((HALO + tm + 8, d), F32), pltpu.VMEM((tm, d), F32),
         pltpu.VMEM((HALO, 8, d), F32)],
        (dz1, dcv, dcv, ag, ag, wb, w_dw), hosted)


FFN_FC = 256


def _ffn_fwd(xh_in, g_in, b_in, wb, lay, layer, tm, *, kv=None, loss=None):
    t, d = xh_in.shape
    f = lay.f
    names = (f"gt{layer}", f"ut{layer}", f"dn{layer}")

    def body(*refs):
        xh_ref, gi_ref, bi_ref, wb_ref = refs[:4]
        pos = 4
        if kv is not None:
            go_ref, bo_ref, wkv_ref, bkv_ref = refs[pos:pos + 4]
            pos += 4
        if loss is not None:
            go_ref, bo_ref, tgt_ref = refs[pos:pos + 3]
            pos += 3
        xb_ref, hg_ref, hu_ref = refs[pos:pos + 3]
        pos += 3
        if kv is not None:
            xho_ref, rso_ref, xob_ref, kv_ref = refs[pos:pos + 4]
            pos += 4
        if loss is not None:
            dz_ref, st_ref, loss_ref = refs[pos:pos + 3]
            pos += 3
        gt_s, ut_s, dn_s = refs[pos:pos + 3]
        i = pl.program_id(0)

        @pl.when(i == 0)
        def _():
            for name, dst in zip(names, (gt_s, ut_s, dn_s)):
                _load_weight(wb_ref, lay, name, dst)
            if loss is not None:
                st_ref[...] = jnp.zeros(st_ref.shape, F32)
                loss_ref[...] = jnp.zeros(loss_ref.shape, F32)

        xin = xh_ref[...] * gi_ref[...] + bi_ref[...]
        xb = xin.astype(BF16)
        xb_ref[...] = xb
        def up(c):
            return (_dot_nt(xb, _wrows(gt_s, c * FFN_FC, FFN_FC)), _dot_nt(xb, _wrows(ut_s, c * FFN_FC, FFN_FC)))

        fo = jnp.zeros((tm, d), F32)
        nc = f // FFN_FC
        ahead = up(0)
        for c in range(nc):
            rows = slice(c * FFN_FC, (c + 1) * FFN_FC)
            hg, hu = ahead
            if c + 1 < nc:
                ahead = up(c + 1)
            hg_ref[:, rows] = hg.astype(BF16)
            hu_ref[:, rows] = hu.astype(BF16)
            act = hg * _sigmoid(hg) * hu
            fo = fo + _dot(act.astype(BF16), _wrows(dn_s, c * FFN_FC, FFN_FC))
        xho, rso = _ln_fwd(ALPHA * xin + fo)
        if kv is not None:
            xho_ref[...] = xho
            rso_ref[...] = rso
            xob = (xho * go_ref[...] + bo_ref[...]).astype(BF16)
            xob_ref[...] = xob
            kv_ref[...] = (_dot(xob, wkv_ref[...]) + bkv_ref[...]).astype(BF16)
        if loss is not None:
            diff = xho * go_ref[...] + bo_ref[...] - tgt_ref[...]
            loss_ref[...] += (0.5 / d) * jnp.sum(diff * diff)
            dz, dg, db = _ln_bwd(diff * (1.0 / d), xho, rso, go_ref[...])
            dz_ref[...] = dz
            st_ref[0:1, :] += dg
            st_ref[1:2, :] += db

    in_specs = [_tile(tm, d), _row(d), _row(d), ANY]
    args = [xh_in, g_in, b_in, wb]
    out_specs = [_tile(tm, d), _tile(tm, f), _tile(tm, f)]
    out_shape = [jax.ShapeDtypeStruct((t, d), BF16), jax.ShapeDtypeStruct((t, f), BF16),
                 jax.ShapeDtypeStruct((t, f), BF16)]
    if kv is not None:
        in_specs += [_row(d), _row(d), _fixed(d, 2 * BLOCK), _row(2 * BLOCK)]
        args += list(kv)
        out_specs += [_tile(tm, d), _tile(tm, 1), _tile(tm, d), _tile(tm, 2 * BLOCK)]
        out_shape += [jax.ShapeDtypeStruct((t, d), F32), jax.ShapeDtypeStruct((t, 1), F32),
                      jax.ShapeDtypeStruct((t, d), BF16), jax.ShapeDtypeStruct((t, 2 * BLOCK), BF16)]
    if loss is not None:
        in_specs += [_row(d), _row(d), _tile(tm, d)]
        args += list(loss)
        out_specs += [_tile(tm, d), _fixed(8, d), _fixed(8, 128)]
        out_shape += [jax.ShapeDtypeStruct((t, d), F32), jax.ShapeDtypeStruct((8, d), F32),
                      jax.ShapeDtypeStruct((8, 128), F32)]
    return pl.pallas_call(
        body, name=f"ffn_fwd{layer}", grid=(t // tm,), in_specs=in_specs, out_specs=out_specs,
        out_shape=out_shape, scratch_shapes=[_wscratch(lay, n) for n in names],
        compiler_params=_params(),
    )(*args)


def _ffn_bwd(dz, hg, hu, xh_in, rs_in, g_in, wb, lay, layer, tm, hosted=None):
    t, d = dz.shape
    f = lay.f
    names = (f"gt{layer}", f"ut{layer}", f"dn{layer}")

    def body(dz_ref, hg_ref, hu_ref, xh_ref, rs_ref, gi_ref, wb_ref,
             dzb_ref, act_ref, dhg_ref, dhu_ref, dzp_ref, st_ref, gt_s, ut_s, dn_s):
        i = pl.program_id(0)

        @pl.when(i == 0)
        def _():
            for name, dst in zip(names, (gt_s, ut_s, dn_s)):
                _load_weight(wb_ref, lay, name, dst)
            st_ref[...] = jnp.zeros(st_ref.shape, F32)

        dzv = dz_ref[...]
        dzb = dzv.astype(BF16)
        dzb_ref[...] = dzb
        dx = ALPHA * dzv
        def back(c):
            return _dot_nt(dzb, _wrows(dn_s, c * FFN_FC, FFN_FC))

        nc = f // FFN_FC
        ahead = back(0)
        for c in range(nc):
            rows = slice(c * FFN_FC, (c + 1) * FFN_FC)
            dact = ahead
            if c + 1 < nc:
                ahead = back(c + 1)
            hg_v = hg_ref[:, rows].astype(F32)
            hu_v = hu_ref[:, rows].astype(F32)
            sg = _sigmoid(hg_v)
            silu = hg_v * sg
            act_ref[:, rows] = (silu * hu_v).astype(BF16)
            dhu = (dact * silu).astype(BF16)
            dhg = (dact * hu_v * (sg * (1.0 + hg_v * (1.0 - sg)))).astype(BF16)
            dhu_ref[:, rows] = dhu
            dhg_ref[:, rows] = dhg
            dx = (dx + _dot(dhg, _wrows(gt_s, c * FFN_FC, FFN_FC))
                  + _dot(dhu, _wrows(ut_s, c * FFN_FC, FFN_FC)))
        dzp, dg, db = _ln_bwd(dx, xh_ref[...], rs_ref[...], gi_ref[...])
        dzp_ref[...] = dzp
        st_ref[0:1, :] += dg
        st_ref[1:2, :] += db

    return _gridded_call(
        body, f"ffn_bwd{layer}", t // tm,
        [_tile(tm, d), _tile(tm, f), _tile(tm, f), _tile(tm, d), _tile(tm, 1), _row(d), ANY],
        [_tile(tm, d), _tile(tm, f), _tile(tm, f), _tile(tm, f), _tile(tm, d), _fixed(8, d)],
        [jax.ShapeDtypeStruct((t, d), BF16), jax.ShapeDtypeStruct((t, f), BF16),
         jax.ShapeDtypeStruct((t, f), BF16), jax.ShapeDtypeStruct((t, f), BF16),
         jax.ShapeDtypeStruct((t, d), F32), jax.ShapeDtypeStruct((8, d), F32)],
        [_wscratch(lay, n) for n in names],
        (dz, hg, hu, xh_in, rs_in, g_in, wb), hosted)


def _alibi_slope(h, nq):
    return 2.0 ** (-ALIBI_MAX * (h + 1) / nq)


def _fill_alibi_bias(bias_s, nq):
    qi = lax.broadcasted_iota(jnp.int32, (BLOCK, 2 * BLOCK), 0)
    kj = lax.broadcasted_iota(jnp.int32, (BLOCK, 2 * BLOCK), 1)
    delta = qi + BLOCK - kj
    valid = (delta >= 0) & (delta < BLOCK)
    dist = jnp.where(valid, delta.astype(F32), MASKED_DIST)
    dist_first = jnp.where(kj >= BLOCK, dist, MASKED_DIST)
    for h in range(nq):
        bias_s[0, h] = _alibi_slope(h, nq) * dist
        bias_s[1, h] = _alibi_slope(h, nq) * dist_first


def _padded_kv(kvb, kvh):
    lane = lax.broadcasted_iota(jnp.int32, (2 * BLOCK, BLOCK), 1)
    mine = (lane < HEAD_DIM) if kvh == 0 else (lane >= HEAD_DIM)
    out = []
    for sec in (kvb[:, :BLOCK], kvb[:, BLOCK:]):
        m = jnp.where(mine, sec.astype(F32), 0.0)
        sw = pltpu.roll(m, HEAD_DIM, 1)
        pair = (m, sw) if kvh == 0 else (sw, m)
        out.append(tuple(p.astype(BF16) for p in pair))
    return out


def _attn_fwd(xh_in, g_in, b_in, x_in_b, kvs, wb, lay, bq, sinks, bo, tm):
    t, d = xh_in.shape
    nq = d // HEAD_DIM
    pairs_per_kv = (d // BLOCK) // N_KV_HEADS
    nbt = tm // BLOCK
    scale = HEAD_DIM ** -0.5

    def body(xh_ref, gi_ref, bi_ref, xb_ref, kv_ref, kvp_ref, wb_ref, bq_ref, sk_ref, bo_ref,
             q_ref, o_ref, lse_ref, xho_ref, rso_ref, wq_s, wo_s, kvall, q_s, o_s, bias_s):
        i = pl.program_id(0)

        @pl.when(i == 0)
        def _():
            _load_weight(wb_ref, lay, "wq", wq_s)
            _load_weight(wb_ref, lay, "wo", wo_s)
            _fill_alibi_bias(bias_s, nq)

        qv = ((_dot(xb_ref[...], _wfull(wq_s)) + bq_ref[...]) * scale).astype(BF16)
        q_s[...] = qv
        q_ref[...] = qv
        kvall[pl.ds(0, BLOCK), :] = kvp_ref[...]
        kvall[pl.ds(BLOCK, tm), :] = kv_ref[...]
        lane = lax.broadcasted_iota(jnp.int32, (BLOCK, BLOCK), 1)
        ones = jnp.ones((2 * BLOCK, BLOCK), BF16)

        def score_phase(j):
            rows = slice(j * BLOCK, (j + 1) * BLOCK)
            kvb = kvall[j * BLOCK:(j + 2) * BLOCK, :]
            first = (i * nbt + j == 0).astype(jnp.int32)
            pads = [_padded_kv(kvb, kvh) for kvh in range(N_KV_HEADS)]
            scores = []
            for a in range(d // BLOCK):
                kpad = pads[a // pairs_per_kv][0]
                qp = q_s[rows, a * BLOCK:(a + 1) * BLOCK]
                for e in range(2):
                    scores.append(_dot_nt(qp, kpad[e]) - bias_s[first, 2 * a + e])
            return rows, pads, scores

        def softmax_phase(state):
            rows, pads, scores = state
            probs, inv = [], []
            lse_t = jnp.zeros((BLOCK, BLOCK), F32)
            for h in range(nq):
                sink = sk_ref[:, h:h + 1]
                m = jnp.maximum(jnp.max(scores[h], axis=-1, keepdims=True), sink)
                p = jnp.exp(scores[h] - m).astype(BF16)
                l = _dot(p, ones) + jnp.exp(sink - m)
                lse_t = jnp.where(lane == h, m + jnp.log(l), lse_t)
                probs.append(p)
                inv.append(1.0 / l)
            lse_ref[rows, :] = lse_t
            return rows, pads, probs, inv

        def value_phase(state):
            rows, pads, probs, inv = state
            for a in range(d // BLOCK):
                vpad = pads[a // pairs_per_kv][1]
                opair = (_dot(probs[2 * a], vpad[0]) * inv[2 * a]
                         + _dot(probs[2 * a + 1], vpad[1]) * inv[2 * a + 1])
                o_s[rows, a * BLOCK:(a + 1) * BLOCK] = opair.astype(BF16)

        for state in [softmax_phase(s) for s in [score_phase(j) for j in range(nbt)]]:
            value_phase(state)
        ov = o_s[...]
        o_ref[...] = ov
        xin = xh_ref[...] * gi_ref[...] + bi_ref[...]
        xho, rso = _ln_fwd(ALPHA * xin + _dot(ov, _wfull(wo_s)) + bo_ref[...])
        xho_ref[...] = xho
        rso_ref[...] = rso

    return pl.pallas_call(
        body, name="attn_fwd", grid=(t // tm,),
        in_specs=[_tile(tm, d), _row(d), _row(d), _tile(tm, d), _tile(tm, 2 * BLOCK),
                  pl.BlockSpec((BLOCK, 2 * BLOCK), lambda i: (jnp.maximum(i * nbt - 1, 0), 0)),
                  ANY, _row(d), _row(nq), _row(d)],
        out_specs=[_tile(tm, d), _tile(tm, d), _tile(tm, BLOCK), _tile(tm, d), _tile(tm, 1)],
        out_shape=[jax.ShapeDtypeStruct((t, d), BF16), jax.ShapeDtypeStruct((t, d), BF16),
                   jax.ShapeDtypeStruct((t, BLOCK), F32), jax.ShapeDtypeStruct((t, d), F32),
                   jax.ShapeDtypeStruct((t, 1), F32)],
        scratch_shapes=[_wscratch(lay, "wq"), _wscratch(lay, "wo"),
                        pltpu.VMEM((BLOCK + tm, 2 * BLOCK), BF16), pltpu.VMEM((tm, d), BF16),
                        pltpu.VMEM((tm, d), BF16), pltpu.VMEM((2, nq, BLOCK, 2 * BLOCK), F32)],
        compiler_params=_params(),
    )(xh_in, g_in, b_in, x_in_b, kvs, kvs, wb, bq, sinks, bo)


def _attn_bwd(dz, q, o, lse, kvs, wb, lay, sinks, tm, hosted=None):
    t, d = dz.shape
    nq = d // HEAD_DIM
    pairs_per_kv = (d // BLOCK) // N_KV_HEADS
    nbt = tm // BLOCK
    scale = HEAD_DIM ** -0.5

    def body(dz_ref, q_ref, o_ref, lse_ref, kv_ref, kvp_ref, wb_ref, sk_ref,
             dzb_ref, dq_ref, dkc_ref, dkp_ref, st_ref, dsk_ref, wo_s, kvall, do_s, dq_s, bias_s):
        i = pl.program_id(0)

        @pl.when(i == 0)
        def _():
            _load_weight(wb_ref, lay, "wo", wo_s)
            _fill_alibi_bias(bias_s, nq)
            st_ref[...] = jnp.zeros(st_ref.shape, F32)
            dsk_ref[...] = jnp.zeros(dsk_ref.shape, F32)

        dzv = dz_ref[...]
        dzb = dzv.astype(BF16)
        dzb_ref[...] = dzb
        do_s[...] = _dot_nt(dzb, _wfull(wo_s))
        kvall[pl.ds(0, BLOCK), :] = kvp_ref[...]
        kvall[pl.ds(BLOCK, tm), :] = kv_ref[...]
        lane = lax.broadcasted_iota(jnp.int32, (BLOCK, BLOCK), 1)
        lane1 = lax.broadcasted_iota(jnp.int32, (1, BLOCK), 1)
        lane2 = lax.broadcasted_iota(jnp.int32, (2 * BLOCK, BLOCK), 1)
        halves = (lane < HEAD_DIM, lane >= HEAD_DIM)

        def score_phase(j):
            rows = slice(j * BLOCK, (j + 1) * BLOCK)
            kvb = kvall[j * BLOCK:(j + 2) * BLOCK, :]
            first = (i * nbt + j == 0).astype(jnp.int32)
            pads = [_padded_kv(kvb, kvh) for kvh in range(N_KV_HEADS)]
            scores, dps, dhs, qms, doms = [], [], [], [], []
            for a in range(d // BLOCK):
                kpad, vpad = pads[a // pairs_per_kv]
                cols = slice(a * BLOCK, (a + 1) * BLOCK)
                qp = q_ref[rows, cols]
                dop = do_s[rows, cols]
                dopb = dop.astype(BF16)
                prod = dop * o_ref[rows, cols].astype(F32)
                for e in range(2):
                    scores.append(_dot_nt(qp, kpad[e]) - bias_s[first, 2 * a + e])
                    dps.append(_dot_nt(dopb, vpad[e]))
                    dhs.append(jnp.sum(jnp.where(halves[e], prod, 0.0), axis=-1, keepdims=True))
                    qms.append(jnp.where(halves[e], qp, jnp.zeros_like(qp)))
                    doms.append(jnp.where(halves[e], dopb, jnp.zeros_like(dopb)))
            return rows, pads, scores, dps, dhs, qms, doms

        def softmax_phase(state):
            rows, pads, scores, dps, dhs, qms, doms = state
            dss, pbs = [], []
            dsk_t = jnp.zeros((1, BLOCK), F32)
            for h in range(nq):
                lse_h = lse_ref[rows, h:h + 1]
                p = jnp.exp(scores[h] - lse_h)
                dss.append((p * (dps[h] - dhs[h])).astype(BF16))
                pbs.append(p.astype(BF16))
                dsink = -jnp.sum(jnp.exp(sk_ref[:, h:h + 1] - lse_h) * dhs[h], axis=0, keepdims=True)
                dsk_t = jnp.where(lane1 == h, dsink, dsk_t)
            dsk_ref[...] += dsk_t
            return rows, pads, dss, pbs, qms, doms

        def grad_phase(state):
            rows, pads, dss, pbs, qms, doms = state
            dsecs = []
            for kvh in range(N_KV_HEADS):
                kpad = pads[kvh][0]
                dk_acc = jnp.zeros((2 * BLOCK, BLOCK), F32)
                dv_acc = jnp.zeros((2 * BLOCK, BLOCK), F32)
                for a in range(kvh * pairs_per_kv, (kvh + 1) * pairs_per_kv):
                    dqp = _dot(dss[2 * a], kpad[0]) + _dot(dss[2 * a + 1], kpad[1])
                    dq_s[rows, a * BLOCK:(a + 1) * BLOCK] = dqp * scale
                    for e in range(2):
                        h = 2 * a + e
                        dk_acc = dk_acc + _dot_tn(dss[h], qms[h])
                        dv_acc = dv_acc + _dot_tn(pbs[h], doms[h])
                dsecs.append((dk_acc + pltpu.roll(dk_acc, HEAD_DIM, 1), dv_acc + pltpu.roll(dv_acc, HEAD_DIM, 1)))
            lo = lane2 < HEAD_DIM
            dkv = jnp.concatenate([jnp.where(lo, dsecs[0][0], dsecs[1][0]),
                                   jnp.where(lo, dsecs[0][1], dsecs[1][1])], axis=1)
            dkp_ref[rows, :] = dkv[:BLOCK]
            dkc_ref[rows, :] = dkv[BLOCK:]

        for state in [softmax_phase(s) for s in [score_phase(j) for j in range(nbt)]]:
            grad_phase(state)
        dqv = dq_s[...]
        dq_ref[...] = dqv.astype(BF16)
        st_ref[0:1, :] += jnp.sum(dqv, axis=0, keepdims=True)
        st_ref[1:2, :] += jnp.sum(dzv, axis=0, keepdims=True)

    return _gridded_call(
        body, "attn_bwd", t // tm,
        [_tile(tm, d), _tile(tm, d), _tile(tm, d), _tile(tm, BLOCK), _tile(tm, 2 * BLOCK),
         pl.BlockSpec((BLOCK, 2 * BLOCK), lambda i: (jnp.maximum(i * nbt - 1, 0), 0)),
         ANY, _row(nq)],
        [_tile(tm, d), _tile(tm, d), _tile(tm, 2 * BLOCK), _tile(tm, 2 * BLOCK),
         _fixed(8, d), _row(BLOCK)],
        [jax.ShapeDtypeStruct((t, d), BF16), jax.ShapeDtypeStruct((t, d), BF16),
         jax.ShapeDtypeStruct((t, 2 * BLOCK), F32), jax.ShapeDtypeStruct((t, 2 * BLOCK), F32),
         jax.ShapeDtypeStruct((8, d), F32), jax.ShapeDtypeStruct((1, BLOCK), F32)],
        [_wscratch(lay, "wo"), pltpu.VMEM((BLOCK + tm, 2 * BLOCK), BF16),
         pltpu.VMEM((tm, d), F32), pltpu.VMEM((tm, d), F32),
         pltpu.VMEM((2, nq, BLOCK, 2 * BLOCK), F32)],
        (dz, q, o, lse, kvs, kvs, wb, sinks), hosted)


def _x2_bwd(dz, dq, dkc, dkp, xh_in, rs_in, g_in, wb, lay, wkv, tm):
    t, d = dz.shape
    nbt = tm // BLOCK
    nsteps = t // tm
    scale = HEAD_DIM ** -0.5

    def body(dz_ref, dq_ref, dkc_ref, dkp_ref, dkn_ref, xh_ref, rs_ref, gi_ref, wb_ref, wkv_ref,
             dkv_ref, dzp_ref, st_ref, dbkv_ref, wq_s):
        i = pl.program_id(0)

        @pl.when(i == 0)
        def _():
            _load_weight(wb_ref, lay, "wq", wq_s)
            st_ref[...] = jnp.zeros(st_ref.shape, F32)
            dbkv_ref[...] = jnp.zeros(dbkv_ref.shape, F32)

        nxt = jnp.where(i < nsteps - 1, dkn_ref[...], 0.0)
        if nbt > 1:
            shifted = jnp.concatenate([dkp_ref[pl.ds(BLOCK, tm - BLOCK), :], nxt], axis=0)
        else:
            shifted = nxt
        dkv = dkc_ref[...] + shifted
        dkvb = dkv.astype(BF16)
        dkv_ref[...] = dkvb
        dbkv_ref[...] += jnp.sum(dkv, axis=0, keepdims=True)
        dx = ALPHA * dz_ref[...] + _dot_nt(dq_ref[...], _wfull(wq_s)) + _dot_nt(dkvb, wkv_ref[...])
        dzp, dg, db = _ln_bwd(dx, xh_ref[...], rs_ref[...], gi_ref[...])
        dzp_ref[...] = dzp
        st_ref[0:1, :] += dg
        st_ref[1:2, :] += db

    del scale
    last = t // BLOCK - 1
    return pl.pallas_call(
        body, name="x2_bwd", grid=(nsteps,),
        in_specs=[_tile(tm, d), _tile(tm, d), _tile(tm, 2 * BLOCK), _tile(tm, 2 * BLOCK),
                  pl.BlockSpec((BLOCK, 2 * BLOCK), lambda i: (jnp.minimum((i + 1) * nbt, last), 0)),
                  _tile(tm, d), _tile(tm, 1), _row(d), ANY, _fixed(d, 2 * BLOCK)],
        out_specs=[_tile(tm, 2 * BLOCK), _tile(tm, d), _fixed(8, d), _row(2 * BLOCK)],
        out_shape=[jax.ShapeDtypeStruct((t, 2 * BLOCK), BF16), jax.ShapeDtypeStruct((t, d), F32),
                   jax.ShapeDtypeStruct((8, d), F32), jax.ShapeDtypeStruct((1, 2 * BLOCK), F32)],
        scratch_shapes=[_wscratch(lay, "wq")],
        compiler_params=_params(),
    )(dz, dq, dkc, dkp, dkp, xh_in, rs_in, g_in, wb, wkv)


def _tn_matmul(a, b, name, bm, tk):
    t, m = a.shape
    n = b.shape[1]
    ksteps = t // tk

    def body(a_ref, b_ref, o_ref):
        k = pl.program_id(1)
        part = _dot_tn(a_ref[...], b_ref[...])

        @pl.when(k == 0)
        def _():
            o_ref[...] = part

        @pl.when(k > 0)
        def _():
            o_ref[...] += part

    return pl.pallas_call(
        body, name=name, grid=(m // bm, ksteps),
        in_specs=[pl.BlockSpec((tk, bm), lambda j, k: (k, j)), pl.BlockSpec((tk, n), lambda j, k: (k, 0))],
        out_specs=pl.BlockSpec((bm, n), lambda j, k: (j, 0)),
        out_shape=jax.ShapeDtypeStruct((m, n), F32),
        compiler_params=pltpu.CompilerParams(dimension_semantics=("arbitrary", "arbitrary"),
                                             vmem_limit_bytes=VMEM_LIMIT),
    )(a, b)


def _all_gather(arrays, name):
    n = len(arrays)

    def body(*refs):
        ins, outs = refs[:n], refs[n:2 * n]
        send_sems, recv_sems, local_sems = refs[2 * n:]
        x, y, c = _me()
        me, sibling = (x, y, c), (x, y, 1 - c)
        chips = [(1 - x, y), (x, 1 - y), (1 - x, 1 - y)]

        def slot(ref, dev):
            return ref.at[4 * dev[0] + 2 * dev[1] + dev[2]]

        def copy(a, k, block, to, src=None):
            return pltpu.make_async_remote_copy(
                src_ref=slot(outs[a], block) if src is None else src, dst_ref=slot(outs[a], block),
                send_sem=send_sems.at[a, k], recv_sem=recv_sems.at[a, k], device_id=to, device_id_type=MESH)

        mine = [pltpu.make_async_copy(ins[a], slot(outs[a], me), local_sems.at[a]) for a in range(n)]
        for cp in mine:
            cp.start()
        first = []
        for a in range(n):
            first.append(copy(a, 0, me, sibling, src=ins[a]))
            first += [copy(a, 1 + j, me, (*chip, c), src=ins[a]) for j, chip in enumerate(chips)]
        for cp in first:
            cp.start()
        passed = []
        for a in range(n):
            for j, chip in enumerate(chips):
                copy(a, 1 + j, (*chip, c), me).wait_recv()
                cp = copy(a, 4 + j, (*chip, c), sibling)
                cp.start()
                passed.append(cp)
        for a in range(n):
            copy(a, 0, sibling, me).wait_recv()
            for j, chip in enumerate(chips):
                copy(a, 4 + j, (*chip, 1 - c), me).wait_recv()
        for cp in first + passed:
            cp.wait_send()
        for cp in mine:
            cp.wait()

    return pl.pallas_call(
        body, name=name, in_specs=[ANY] * n, out_specs=[ANY] * n,
        out_shape=[jax.ShapeDtypeStruct((N_DEV,) + a.shape, a.dtype) for a in arrays],
        scratch_shapes=[pltpu.SemaphoreType.DMA((n, 7)), pltpu.SemaphoreType.DMA((n, 7)),
                        pltpu.SemaphoreType.DMA((n,))],
    )(*arrays)


def _exchange(arrays, name):
    n = len(arrays)
    blocked = [a.ndim == 3 for a in arrays]

    def body(*refs):
        ins, outs = refs[:n], refs[n:2 * n]
        send_sems, recv_sems, local_sems = refs[2 * n:]
        me = _index(_me())

        def src(k, dev):
            return ins[k].at[dev] if blocked[k] else ins[k]

        local = [pltpu.make_async_copy(src(k, me), outs[k].at[me], local_sems.at[k]) for k in range(n)]
        sends, arrivals = [], []
        for k in range(n):
            for mask in range(1, N_DEV):
                peer = _peer(mask)
                sends.append(pltpu.make_async_remote_copy(
                    src_ref=src(k, _index(peer)), dst_ref=outs[k].at[me], send_sem=send_sems.at[k, mask - 1],
                    recv_sem=recv_sems.at[k, mask - 1], device_id=peer, device_id_type=MESH))
                arrivals.append(pltpu.make_async_remote_copy(
                    src_ref=src(k, me), dst_ref=outs[k].at[_index(peer)], send_sem=send_sems.at[k, mask - 1],
                    recv_sem=recv_sems.at[k, mask - 1], device_id=_me(), device_id_type=MESH))
        for cp in local + sends:
            cp.start()
        for cp in arrivals:
            cp.wait_recv()
        for cp in sends:
            cp.wait_send()
        for cp in local:
            cp.wait()

    return pl.pallas_call(
        body, name=name, in_specs=[ANY] * n, out_specs=[ANY] * n,
        out_shape=[jax.ShapeDtypeStruct((N_DEV,) + a.shape[-2:], a.dtype) for a in arrays],
        scratch_shapes=[pltpu.SemaphoreType.DMA((n, 7)), pltpu.SemaphoreType.DMA((n, 7)),
                        pltpu.SemaphoreType.DMA((n,))],
    )(*arrays)


def _adamw_sum(g8, w, m, v, name, tr):
    r, width = w.shape
    bc1 = 1.0 - ADAM_B1 ** ADAM_STEP
    bc2 = 1.0 - ADAM_B2 ** ADAM_STEP

    def body(g_ref, w_ref, m_ref, v_ref, go_ref, d_ref, mo_ref, vo_ref):
        g = g_ref[0].astype(F32)
        for s in range(1, N_DEV):
            g = g + g_ref[s].astype(F32)
        mn = ADAM_B1 * m_ref[...] + (1.0 - ADAM_B1) * g
        vn = ADAM_B2 * v_ref[...] + (1.0 - ADAM_B2) * (g * g)
        m_hat = mn / bc1
        v_hat = vn / bc2
        go_ref[...] = g
        d_ref[...] = -ADAM_LR * (m_hat / (jnp.sqrt(v_hat) + ADAM_EPS) + ADAM_WD * w_ref[...])
        mo_ref[...] = mn
        vo_ref[...] = vn

    spec = pl.BlockSpec((tr, width), lambda i: (i, 0))
    return pl.pallas_call(
        body, name=name, grid=(r // tr,),
        in_specs=[pl.BlockSpec((N_DEV, tr, width), lambda i: (0, i, 0)), spec, spec, spec],
        out_specs=[spec] * 4, out_shape=[jax.ShapeDtypeStruct((r, width), F32)] * 4,
        compiler_params=_params(),
    )(g8, w, m, v)


def _local_step(x, target, wba, shard_b, lay, sm, tm, tk):
    t, d = x.shape
    f = lay.f
    w_dw32 = jnp.concatenate([sm["w_dw"], jnp.zeros((HALO - CONV_WIDTH, d), F32)], axis=0)
    lmg, lmb, lfg, lfb = sm["ln_mix_g"], sm["ln_mix_b"], sm["ln_ffn_g"], sm["ln_ffn_b"]
    bkv = jnp.concatenate([sm["b_k"], sm["b_v"]], axis=1)
    bm_f = f // 2 if (f // 2) % 128 == 0 else f

    def exchange(group, grads):
        names = lay.GRADS[group]
        return _HostedExchange([grads[n].reshape(N_DEV, lay.n[n], d) for n in names],
                               [lay.roff[n] for n in names], lay.rrows[group])

    xb0, ag, xhc, rsc, xh1, rs1, wbb = _conv_fwd(x, wba, lay, w_dw32, sm["b_pw1"], sm["b_dw"], sm["cg"],
                                                 sm["cb"], sm["b_pw2"], lmg[0:1], lmb[0:1], tm,
                                                 hosted=_HostedGather(shard_b))
    wkv = wbb[:, lay.goff["wkv"]:lay.goff["wkv"] + lay.n["wkv"], :].reshape(d, 2 * BLOCK)
    x1b, hg0, hu0, xh2, rs2, x2b, kvs = _ffn_fwd(xh1, lmg[0:1], lmb[0:1], wbb, lay, 0, tm,
                                                kv=(lfg[0:1], lfb[0:1], wkv, bkv))
    q, o, lse, xh3, rs3 = _attn_fwd(xh2, lfg[0:1], lfb[0:1], x2b, kvs, wbb, lay, sm["b_q"], sm["sinks"],
                                    sm["b_o"], tm)
    x3b, hg1, hu1, dz4, st4, loss = _ffn_fwd(xh3, lmg[1:2], lmb[1:2], wbb, lay, 1, tm,
                                             loss=(lfg[1:2], lfb[1:2], target))

    dz4b, act1, dhg1, dhu1, dz3, st3 = _ffn_bwd(dz4, hg1, hu1, xh3, rs3, lmg[1:2], wbb, lay, 1, tm)
    g1 = {"gt1": _tn_matmul(dhg1, x3b, "dw_gate1", bm_f, tk), "ut1": _tn_matmul(dhu1, x3b, "dw_up1", bm_f, tk),
          "dn1": _tn_matmul(act1, dz4b, "dw_down1", bm_f, tk)}
    dz3b, dq, dkc, dkp, stq, dsinks, recv1 = _attn_bwd(dz3, q, o, lse, kvs, wbb, lay, sm["sinks"], tm,
                                                       hosted=exchange("r1", g1))
    dkv, dz2, st2, dbkv = _x2_bwd(dz3, dq, dkc, dkp, xh2, rs2, lfg[0:1], wbb, lay, wkv, tm)
    g2 = {"wq": _tn_matmul(x2b, dq, "dw_q", d, tk), "wo": _tn_matmul(o, dz3b, "dw_o", d, tk),
          "wkv": _tn_matmul(x2b, dkv, "dw_kv", d, tk)}
    dz2b, act0, dhg0, dhu0, dz1, st1, recv2 = _ffn_bwd(dz2, hg0, hu0, xh1, rs1, lmg[0:1], wbb, lay, 0, tm,
                                                       hosted=exchange("r2", g2))
    dz1b, s_act, dcv, stc = _conv_bwd1(dz1, xhc, rsc, wba, lay, sm["cg"], sm["cb"], tm)
    g3 = {"gt0": _tn_matmul(dhg0, x1b, "dw_gate0", bm_f, tk), "ut0": _tn_matmul(dhu0, x1b, "dw_up0", bm_f, tk),
          "dn0": _tn_matmul(act0, dz2b, "dw_down0", bm_f, tk), "pw2": _tn_matmul(s_act, dz1b, "dw_pw2", d, tk)}
    grad_x, dh1, dwdw, db1, recv3 = _conv_bwd2(dz1, dcv, ag, wba, lay, w_dw32, tm, hosted=exchange("r3", g3))
    g_pw1t = _tn_matmul(dh1, xb0, "dw_pw1", d, tk)
    received = {"r1": recv1, "r2": recv2, "r3": recv3}
    small = {
        "w_dw": dwdw[:CONV_WIDTH], "b_pw1": db1, "b_dw": stc[2:3], "cg": stc[0:1], "cb": stc[1:2],
        "b_pw2": stc[3:4], "b_k": dbkv[:, :BLOCK], "b_v": dbkv[:, BLOCK:], "b_q": stq[0:1],
        "sinks": dsinks[:, :d // HEAD_DIM],
        "b_o": stq[1:2],
        "ln_mix_g": jnp.concatenate([st1[0:1], st3[0:1]], axis=0),
        "ln_mix_b": jnp.concatenate([st1[1:2], st3[1:2]], axis=0),
        "ln_ffn_g": jnp.concatenate([st2[0:1], st4[0:1]], axis=0),
        "ln_ffn_b": jnp.concatenate([st2[1:2], st4[1:2]], axis=0),
    }
    return loss[0, 0], grad_x, received, g_pw1t, small


SP_ROWS = 40
SP_BDW, SP_CG, SP_CB, SP_BPW2, SP_BPW1 = 32, 33, 34, 35, 36
RP_NAMES = ("ln_mix_g", "ln_mix_b", "ln_ffn_g", "ln_ffn_b", "b_q", "b_o", "b_k", "b_v", "sinks")


def _row_forms(d, pw1, pw2, wq, wo, gate, up, down, wk, wv):
    rf = {"pw1t": pw1[0].T, "pw2": pw2[0], "wq": wq[0], "wo": wo[0],
          "wkv": jnp.concatenate([wk, wv], axis=1).reshape(-1, d)}
    for l in range(DEPTH):
        rf.update({f"gt{l}": gate[l].T, f"ut{l}": up[l].T, f"dn{l}": down[l]})
    return rf


def _from_row_forms(d, rf):
    kvw = rf["wkv"].reshape(d // N_DEV, 2 * BLOCK)
    return dict(
        pw1=rf["pw1t"].T[None], pw2=rf["pw2"][None], wq=rf["wq"][None], wo=rf["wo"][None],
        gate=jnp.stack([rf[f"gt{l}"].T for l in range(DEPTH)]),
        up=jnp.stack([rf[f"ut{l}"].T for l in range(DEPTH)]),
        down=jnp.stack([rf[f"dn{l}"] for l in range(DEPTH)]),
        wk=kvw[:, :BLOCK], wv=kvw[:, BLOCK:])


def _pack_rows(rf, names):
    return jnp.concatenate([rf[n] for n in names], axis=0)


def _pack_small(w_dw, b_dw, cg, cb, b_pw2, b_pw1):
    cw = b_dw.shape[1]
    z = jnp.zeros((1, cw), F32)
    return jnp.concatenate([w_dw[0], z, b_dw, cg, cb, b_pw2, b_pw1.reshape(2, cw), z, z], axis=0)


def _unpack_small(p):
    cw = p.shape[1]
    return dict(w_dw=p[None, :CONV_WIDTH], b_dw=p[SP_BDW:SP_BDW + 1], cg=p[SP_CG:SP_CG + 1],
                cb=p[SP_CB:SP_CB + 1], b_pw2=p[SP_BPW2:SP_BPW2 + 1],
                b_pw1=p[SP_BPW1:SP_BPW1 + 2].reshape(1, 2 * cw))


def _small_full(g):
    d = N_DEV * g.shape[2]

    def wide(r0, n=1):
        return jnp.transpose(g[:, r0:r0 + n], (1, 0, 2)).reshape(n, d)

    return dict(w_dw=wide(0, CONV_WIDTH), b_dw=wide(SP_BDW), cg=wide(SP_CG), cb=wide(SP_CB),
                b_pw2=wide(SP_BPW2), b_pw1=g[:, SP_BPW1:SP_BPW1 + 2].reshape(1, 2 * d))


def _small_grad_blocks(sg):
    cw = sg["b_dw"].shape[1] // N_DEV

    def narrow(a):
        return jnp.transpose(a.reshape(a.shape[0], N_DEV, cw), (1, 0, 2))

    z = jnp.zeros((N_DEV, 1, cw), F32)
    return jnp.concatenate([narrow(sg["w_dw"]), z, narrow(sg["b_dw"]), narrow(sg["cg"]), narrow(sg["cb"]),
                            narrow(sg["b_pw2"]), sg["b_pw1"].reshape(N_DEV, 2, cw), z, z], axis=1)


def _pack_rep(vals):
    parts = []
    for name in RP_NAMES:
        a = vals[name].reshape(-1)
        pad = -a.shape[0] % 128
        parts.append(jnp.concatenate([a, jnp.zeros((pad,), F32)]).reshape(-1, 128))
    rows = sum(p.shape[0] for p in parts)
    parts.append(jnp.zeros((-rows % 8, 128), F32))
    return jnp.concatenate(parts, axis=0)


def _unpack_rep(p, shapes):
    out, r = {}, 0
    for name in RP_NAMES:
        n = 1
        for s in shapes[name]:
            n *= s
        rows = -(-n // 128)
        out[name] = p[r:r + rows].reshape(-1)[:n].reshape(shapes[name])
        r += rows
    return out


def kernel(x, conv_w_pw1, conv_b_pw1, conv_w_dw, conv_b_dw, conv_ln_g, conv_ln_b, conv_w_pw2, conv_b_pw2, kv_w_k, kv_b_k, kv_w_v, kv_b_v, attn_w_q, attn_b_q, attn_sinks, attn_w_o, attn_b_o, ffn_w_gate, ffn_w_up, ffn_w_down, ln_mix_g, ln_mix_b, ln_ffn_g, ln_ffn_b, loss_target, m_conv_w_pw1, m_conv_b_pw1, m_conv_w_dw, m_conv_b_dw, m_conv_ln_g, m_conv_ln_b, m_conv_w_pw2, m_conv_b_pw2, m_kv_w_k, m_kv_b_k, m_kv_w_v, m_kv_b_v, m_attn_w_q, m_attn_b_q, m_attn_sinks, m_attn_w_o, m_attn_b_o, m_ffn_w_gate, m_ffn_w_up, m_ffn_w_down, m_ln_mix_g, m_ln_mix_b, m_ln_ffn_g, m_ln_ffn_b, v_conv_w_pw1, v_conv_b_pw1, v_conv_w_dw, v_conv_b_dw, v_conv_ln_g, v_conv_ln_b, v_conv_w_pw2, v_conv_b_pw2, v_kv_w_k, v_kv_b_k, v_kv_w_v, v_kv_b_v, v_attn_w_q, v_attn_b_q, v_attn_sinks, v_attn_w_o, v_attn_b_o, v_ffn_w_gate, v_ffn_w_up, v_ffn_w_down, v_ln_mix_g, v_ln_mix_b, v_ln_ffn_g, v_ln_ffn_b):
    t, d = x.shape[1], x.shape[2]
    f = ffn_w_gate.shape[2] * N_DEV
    lay = _Layout(d, f)
    tm, tk = 256, 1024

    rep_shapes = dict(ln_mix_g=ln_mix_g.shape, ln_mix_b=ln_mix_b.shape, ln_ffn_g=ln_ffn_g.shape,
                      ln_ffn_b=ln_ffn_b.shape, b_q=attn_b_q.shape, b_o=attn_b_o.shape, b_k=kv_b_k.shape,
                      b_v=kv_b_v.shape, sinks=attn_sinks.shape)

    def rep_pack(lmg, lmb, lfg, lfb, bq, bo, bk, bv, sk):
        return _pack_rep(dict(ln_mix_g=lmg, ln_mix_b=lmb, ln_ffn_g=lfg, ln_ffn_b=lfb, b_q=bq, b_o=bo,
                              b_k=bk, b_v=bv, sinks=sk))

    w_rf = _row_forms(d, conv_w_pw1, conv_w_pw2, attn_w_q, attn_w_o, ffn_w_gate, ffn_w_up, ffn_w_down, kv_w_k, kv_w_v)
    m_rf = _row_forms(d, m_conv_w_pw1, m_conv_w_pw2, m_attn_w_q, m_attn_w_o, m_ffn_w_gate, m_ffn_w_up, m_ffn_w_down, m_kv_w_k, m_kv_w_v)
    v_rf = _row_forms(d, v_conv_w_pw1, v_conv_w_pw2, v_attn_w_q, v_attn_w_o, v_ffn_w_gate, v_ffn_w_up, v_ffn_w_down, v_kv_w_k, v_kv_w_v)
    w_small = _pack_small(conv_w_dw, conv_b_dw, conv_ln_g, conv_ln_b, conv_b_pw2, conv_b_pw1)
    m_small = _pack_small(m_conv_w_dw, m_conv_b_dw, m_conv_ln_g, m_conv_ln_b, m_conv_b_pw2, m_conv_b_pw1)
    v_small = _pack_small(v_conv_w_dw, v_conv_b_dw, v_conv_ln_g, v_conv_ln_b, v_conv_b_pw2, v_conv_b_pw1)
    w_rep = rep_pack(ln_mix_g, ln_mix_b, ln_ffn_g, ln_ffn_b, attn_b_q, attn_b_o, kv_b_k, kv_b_v, attn_sinks)
    m_rep = rep_pack(m_ln_mix_g, m_ln_mix_b, m_ln_ffn_g, m_ln_ffn_b, m_attn_b_q, m_attn_b_o, m_kv_b_k, m_kv_b_v, m_attn_sinks)
    v_rep = rep_pack(v_ln_mix_g, v_ln_mix_b, v_ln_ffn_g, v_ln_ffn_b, v_attn_b_q, v_attn_b_o, v_kv_b_k, v_kv_b_v, v_attn_sinks)

    wba, smg = _all_gather([_pack_rows(w_rf, lay.GATHER["a"]).astype(BF16), w_small], "gather_conv_weights")
    shard_b = _pack_rows(w_rf, lay.GATHER["b"]).astype(BF16)
    sm = _small_full(smg)
    sm.update(ln_mix_g=ln_mix_g, ln_mix_b=ln_mix_b, ln_ffn_g=ln_ffn_g, ln_ffn_b=ln_ffn_b, b_q=attn_b_q,
              b_o=attn_b_o, sinks=attn_sinks, b_k=kv_b_k.reshape(1, -1), b_v=kv_b_v.reshape(1, -1))

    loss_part, grad_x, received, g_pw1t, gsmall = _local_step(x[0], loss_target[0], wba, shard_b, lay, sm, tm, tk)
    loss = lax.psum(loss_part, ("x", "y", "c"))

    received["r4"], g8_small, g8_rep = _exchange(
        [g_pw1t.astype(BF16).reshape(N_DEV, lay.n["pw1t"], d), _small_grad_blocks(gsmall), _pack_rep(gsmall)],
        "exchange_last_grads")

    big_rf = [{}, {}, {}, {}]
    for group, names in lay.GRADS.items():
        rows = lay.rrows[group]
        tr = max(r for r in range(8, 129, 8) if rows % r == 0)
        res = _adamw_sum(received[group], _pack_rows(w_rf, names), _pack_rows(m_rf, names), _pack_rows(v_rf, names),
                         f"adamw_{group}", tr)
        for out_rf, a in zip(big_rf, res):
            for n in names:
                out_rf[n] = a[lay.roff[n]:lay.roff[n] + lay.n[n]]
    big_out = [_from_row_forms(d, rf) for rf in big_rf]
    small_out = [_unpack_small(a) for a in _adamw_sum(g8_small, w_small, m_small, v_small, "adamw_small", SP_ROWS)]
    rep_out = [_unpack_rep(a, rep_shapes)
               for a in _adamw_sum(g8_rep, w_rep, m_rep, v_rep, "adamw_rep", w_rep.shape[0])]

    outs = [loss, grad_x[None]]
    for b, s, r in zip(big_out, small_out, rep_out):
        outs += [b["pw1"], s["b_pw1"], s["w_dw"], s["b_dw"], s["cg"], s["cb"], b["pw2"], s["b_pw2"],
                 b["wk"], r["b_k"], b["wv"], r["b_v"], b["wq"], r["b_q"], r["sinks"], b["wo"], r["b_o"],
                 b["gate"], b["up"], b["down"], r["ln_mix_g"], r["ln_mix_b"], r["ln_ffn_g"], r["ln_ffn_b"]]
    return tuple(outs)
```

```python
import functools

import jax
import jax.numpy as jnp
from jax import lax
from jax.experimental import pallas as pl
from jax.experimental.pallas import tpu as pltpu

F32 = jnp.float32
BF16 = jnp.bfloat16

N_DEV = 8
HEAD_DIM = 64
N_KV_HEADS = 2
BLOCK = 128
CONV_WIDTH = 31
HALO = 32
ALIBI_MAX = 8.0
DEPTH = 2
ALPHA = (2.0 * DEPTH) ** 0.25
LN_EPS = 1e-5
MASKED_DIST = 1e32
ADAM_LR = 0.001
ADAM_B1 = 0.9
ADAM_B2 = 0.999
ADAM_EPS = 1e-08
ADAM_WD = 0.01
ADAM_STEP = 10
VMEM_LIMIT = 56 * 1024 * 1024
MESH = pl.DeviceIdType.MESH


def _dot(a, b):
    return jnp.dot(a, b, preferred_element_type=F32)


def _dot_nt(a, b):
    return lax.dot_general(a, b, (((1,), (1,)), ((), ())), preferred_element_type=F32)


def _dot_tn(a, b):
    return lax.dot_general(a, b, (((0,), (0,)), ((), ())), preferred_element_type=F32)


def _sigmoid(v):
    return 1.0 / (1.0 + jnp.exp(-v))


def _ln_fwd(z):
    mu = jnp.mean(z, axis=-1, keepdims=True)
    zc = z - mu
    var = jnp.mean(zc * zc, axis=-1, keepdims=True)
    rstd = lax.rsqrt(var + LN_EPS)
    return zc * rstd, rstd


def _ln_bwd(dout, xh, rstd, g):
    dxh = dout * g
    m1 = jnp.mean(dxh, axis=-1, keepdims=True)
    m2 = jnp.mean(dxh * xh, axis=-1, keepdims=True)
    dz = rstd * (dxh - m1 - xh * m2)
    return dz, jnp.sum(dout * xh, axis=0, keepdims=True), jnp.sum(dout, axis=0, keepdims=True)


def _params(vmem=VMEM_LIMIT):
    return pltpu.CompilerParams(dimension_semantics=("arbitrary",), vmem_limit_bytes=vmem)


def _row(d):
    return pl.BlockSpec((1, d), lambda i: (0, 0))


def _tile(tm, d):
    return pl.BlockSpec((tm, d), lambda i: (i, 0))


def _fixed(r, d):
    return pl.BlockSpec((r, d), lambda i: (0, 0))


ANY = pl.BlockSpec(memory_space=pl.ANY)


class _Layout:
    GATHER = {"a": ("pw1t", "pw2"),
              "b": ("wq", "wo", "gt0", "ut0", "dn0", "gt1", "ut1", "dn1", "wkv")}
    GRADS = {"r1": ("gt1", "ut1", "dn1"), "r2": ("wq", "wo", "wkv"), "r3": ("gt0", "ut0", "dn0", "pw2"),
             "r4": ("pw1t",)}

    def __init__(self, d, f):
        self.d, self.f = d, f
        self.n = {"pw1t": 2 * d // N_DEV, "pw2": d // N_DEV, "wq": d // N_DEV, "wo": d // N_DEV,
                  "wkv": (d // N_DEV) * 2 * BLOCK // d}
        for l in range(DEPTH):
            self.n.update({f"gt{l}": f // N_DEV, f"ut{l}": f // N_DEV, f"dn{l}": f // N_DEV})
        self.goff, self.grows = self._offsets(self.GATHER)
        self.roff, self.rrows = self._offsets(self.GRADS)

    def _offsets(self, groups):
        off, rows = {}, {}
        for g, names in groups.items():
            r = 0
            for name in names:
                off[name] = r
                r += self.n[name]
            rows[g] = r
        return off, rows


def _load_weight(wb_ref, lay, name, dst):
    n = lay.n[name]
    for p in range(N_DEV):
        pltpu.sync_copy(wb_ref.at[p, pl.ds(lay.goff[name], n), :], dst.at[pl.ds(p * n, n), :])


def _wscratch(lay, name):
    return pltpu.VMEM((N_DEV * lay.n[name], lay.d), BF16)


def _wfull(ref):
    return ref[...]


def _wrows(ref, r0, nrows):
    return ref[r0:r0 + nrows, :]


def _me():
    return lax.axis_index("x"), lax.axis_index("y"), lax.axis_index("c")


def _peer(mask):
    x, y, c = _me()
    return (1 - x if mask & 4 else x, 1 - y if mask & 2 else y, 1 - c if mask & 1 else c)


def _index(dev):
    return 4 * dev[0] + 2 * dev[1] + dev[2]


class _HostedGather:
    def __init__(self, array):
        self.arrays = [array]
        self.out_shape = jax.ShapeDtypeStruct((N_DEV,) + array.shape, array.dtype)

    def scratch(self):
        return [pltpu.SemaphoreType.DMA((7,)), pltpu.SemaphoreType.DMA((7,)), pltpu.SemaphoreType.DMA(())]

    def _copies(self, ins, out, send_sems, recv_sems, local_sem):
        x, y, c = _me()
        me, sibling = (x, y, c), (x, y, 1 - c)
        chips = [(1 - x, y), (x, 1 - y), (1 - x, 1 - y)]

        def copy(k, block, to, src=None):
            rows = out.at[_index(block)]
            return pltpu.make_async_remote_copy(
                src_ref=rows if src is None else src, dst_ref=rows, send_sem=send_sems.at[k],
                recv_sem=recv_sems.at[k], device_id=to, device_id_type=MESH)

        return dict(
            mine=lambda: pltpu.make_async_copy(ins[0], out.at[_index(me)], local_sem),
            first=lambda: [copy(0, me, sibling, src=ins[0])] + [copy(1 + j, me, (*chip, c), src=ins[0])
                                                                for j, chip in enumerate(chips)],
            over_ici=lambda: [copy(1 + j, (*chip, c), me) for j, chip in enumerate(chips)],
            passed=lambda: [copy(4 + j, (*chip, c), sibling) for j, chip in enumerate(chips)],
            from_sibling=lambda: [copy(0, sibling, me)] + [copy(4 + j, (*chip, 1 - c), me)
                                                           for j, chip in enumerate(chips)])

    def start(self, *refs):
        cp = self._copies(*refs)
        cp["mine"]().start()
        for c in cp["first"]():
            c.start()

    def middle(self, *refs):
        cp = self._copies(*refs)
        for arrived, onward in zip(cp["over_ici"](), cp["passed"]()):
            arrived.wait_recv()
            onward.start()

    def finish(self, *refs):
        cp = self._copies(*refs)
        for c in cp["from_sibling"]():
            c.wait_recv()
        for c in cp["first"]() + cp["passed"]():
            c.wait_send()
        cp["mine"]().wait()


class _HostedExchange:
    def __init__(self, arrays, offsets, rows):
        self.arrays, self.offsets = list(arrays), list(offsets)
        self.out_shape = jax.ShapeDtypeStruct((N_DEV, rows, arrays[0].shape[2]), arrays[0].dtype)

    def scratch(self):
        n = len(self.arrays)
        return [pltpu.SemaphoreType.DMA((n, 7)), pltpu.SemaphoreType.DMA((n, 7)), pltpu.SemaphoreType.DMA((n,))]

    def _copies(self, ins, out, send_sems, recv_sems, local_sems):
        me = _index(_me())

        def dst(k, src_dev):
            return out.at[src_dev, pl.ds(self.offsets[k], self.arrays[k].shape[1]), :]

        pairs = [(k, mask) for k in range(len(self.arrays)) for mask in range(1, N_DEV)]

        def local():
            return [pltpu.make_async_copy(ins[k].at[me], dst(k, me), local_sems.at[k])
                    for k in range(len(self.arrays))]

        def sends():
            return [pltpu.make_async_remote_copy(
                src_ref=ins[k].at[_index(_peer(mask))], dst_ref=dst(k, me), send_sem=send_sems.at[k, mask - 1],
                recv_sem=recv_sems.at[k, mask - 1], device_id=_peer(mask), device_id_type=MESH)
                for k, mask in pairs]

        def arrivals():
            return [pltpu.make_async_remote_copy(
                src_ref=ins[k].at[me], dst_ref=dst(k, _index(_peer(mask))), send_sem=send_sems.at[k, mask - 1],
                recv_sem=recv_sems.at[k, mask - 1], device_id=_me(), device_id_type=MESH)
                for k, mask in pairs]

        return local, sends, arrivals

    def start(self, *refs):
        local, sends, _ = self._copies(*refs)
        for c in local() + sends():
            c.start()

    def middle(self, *refs):
        pass

    def finish(self, *refs):
        local, sends, arrivals = self._copies(*refs)
        for c in arrivals():
            c.wait_recv()
        for c in sends():
            c.wait_send()
        for c in local():
            c.wait()


def _gridded_call(body, name, nsteps, in_specs, out_specs, out_shape, scratch, args, hosted=None):
    if hosted is None:
        return pl.pallas_call(body, name=name, grid=(nsteps,), in_specs=in_specs, out_specs=out_specs,
                              out_shape=out_shape, scratch_shapes=scratch, compiler_params=_params())(*args)
    n_in, n_out, n_scr, h_in = len(in_specs), len(out_specs), len(scratch), len(hosted.arrays)

    def with_hosted(*refs):
        a = n_in + h_in
        b = a + n_out
        e = b + 1 + n_scr
        comm = (refs[n_in:a], refs[b], refs[e], refs[e + 1], refs[e + 2])
        i = pl.program_id(0)

        @pl.when(i == 0)
        def _():
            hosted.start(*comm)

        body(*refs[:n_in], *refs[a:b], *refs[b + 1:e])

        @pl.when(i == nsteps // 2)
        def _():
            hosted.middle(*comm)

        @pl.when(i == nsteps - 1)
        def _():
            hosted.finish(*comm)

    return pl.pallas_call(
        with_hosted, name=name, grid=(nsteps,), in_specs=list(in_specs) + [ANY] * h_in,
        out_specs=list(out_specs) + [ANY], out_shape=list(out_shape) + [hosted.out_shape],
        scratch_shapes=list(scratch) + hosted.scratch(), compiler_params=_params(),
    )(*args, *hosted.arrays)


CONV_RB = 64
CONV_LC = 128
CONV_WIN = CONV_RB + HALO + 8
CONV_MC = 256


def _shifted(win, r):
    return win if r == 0 else pltpu.roll(win, win.shape[0] - r, 0)


def _conv_fwd(x, wb, lay, w_dw, b_pw1, b_dw, cg, cb, b_pw2, lg, lb, tm, hosted=None):
    t, d = x.shape
    nsteps = t // tm

    def body(x_ref, xh_ref, wb_ref, wdw_ref, b1_ref, bdw_ref, cg_ref, cb_ref, b2_ref, lg_ref, lb_ref,
             xb_ref, ag_ref, xhc_ref, rsc_ref, xh1_ref, rs1_ref, w1_s, w2_s, ubuf, cv_s):
        i = pl.program_id(0)

        @pl.when(i == 0)
        def _():
            _load_weight(wb_ref, lay, "pw1t", w1_s)
            _load_weight(wb_ref, lay, "pw2", w2_s)
            ubuf[pl.ds(HALO + tm, 8), :] = jnp.zeros((8, d), F32)

        xv = x_ref[...]
        xb = xv.astype(BF16)
        xb_ref[...] = xb
        xcat = jnp.concatenate([xh_ref[...].astype(BF16), xb], axis=0)
        for mc in range(d // CONV_MC):
            c0 = mc * CONV_MC
            acols, gcols = slice(c0, c0 + CONV_MC), slice(d + c0, d + c0 + CONV_MC)
            ha = _dot_nt(xcat, _wrows(w1_s, c0, CONV_MC)) + b1_ref[:, acols]
            hg = _dot_nt(xcat, _wrows(w1_s, d + c0, CONV_MC)) + b1_ref[:, gcols]
            ag_ref[:, acols] = ha[HALO:].astype(BF16)
            ag_ref[:, gcols] = hg[HALO:].astype(BF16)
            u = ha * _sigmoid(hg)
            ubuf[0:HALO, acols] = jnp.where(i > 0, u[:HALO], 0.0)
            ubuf[HALO:HALO + tm, acols] = u[HALO:]
            for rb in range(tm // CONV_RB):
                t0 = rb * CONV_RB
                for lc in range(CONV_MC // CONV_LC):
                    lanes = slice(c0 + lc * CONV_LC, c0 + (lc + 1) * CONV_LC)
                    win = ubuf[t0:t0 + CONV_WIN, lanes]
                    acc = jnp.zeros((CONV_RB, CONV_LC), F32)
                    for r in range(8):
                        wr = _shifted(win, r)
                        for k in range(CONV_WIDTH):
                            s = HALO - (CONV_WIDTH - 1) + k
                            if s % 8 == r:
                                q = 8 * (s // 8)
                                acc = acc + wr[q:q + CONV_RB] * wdw_ref[k:k + 1, lanes]
                    cv_s[t0:t0 + CONV_RB, lanes] = acc
        cv = cv_s[...] + bdw_ref[...]
        xhc, rsc = _ln_fwd(cv)
        xhc_ref[...] = xhc
        rsc_ref[...] = rsc
        n = xhc * cg_ref[...] + cb_ref[...]
        s_act = n * _sigmoid(n)
        m = _dot(s_act.astype(BF16), _wfull(w2_s)) + b2_ref[...]
        xh1, rs1 = _ln_fwd(ALPHA * xv + m)
        xh1_ref[...] = xh1
        rs1_ref[...] = rs1

    hb = tm // HALO
    return _gridded_call(
        body, "conv_fwd", nsteps,
        [_tile(tm, d), pl.BlockSpec((HALO, d), lambda i: (jnp.maximum(i * hb - 1, 0), 0)), ANY,
         _fixed(HALO, d), _row(2 * d), _row(d), _row(d), _row(d), _row(d), _row(d), _row(d)],
        [_tile(tm, d), _tile(tm, 2 * d), _tile(tm, d), _tile(tm, 1), _tile(tm, d), _tile(tm, 1)],
        [jax.ShapeDtypeStruct((t, d), BF16), jax.ShapeDtypeStruct((t, 2 * d), BF16),
         jax.ShapeDtypeStruct((t, d), F32), jax.ShapeDtypeStruct((t, 1), F32),
         jax.ShapeDtypeStruct((t, d), F32), jax.ShapeDtypeStruct((t, 1), F32)],
        [_wscratch(lay, "pw1t"), _wscratch(lay, "pw2"),
         pltpu.VMEM((HALO + tm + 8, d), F32), pltpu.VMEM((tm, d), F32)],
        (x, x, wb, w_dw, b_pw1, b_dw, cg, cb, b_pw2, lg, lb), hosted)


def _conv_bwd1(dz1, xhc, rsc, wb, lay, cg, cb, tm):
    t, d = dz1.shape

    def body(dz_ref, xhc_ref, rsc_ref, wb_ref, cg_ref, cb_ref, dzb_ref, s_ref, dcv_ref, st_ref, w2_s):
        i = pl.program_id(0)

        @pl.when(i == 0)
        def _():
            _load_weight(wb_ref, lay, "pw2", w2_s)
            st_ref[...] = jnp.zeros(st_ref.shape, F32)

        dz = dz_ref[...]
        dzb = dz.astype(BF16)
        dzb_ref[...] = dzb
        xhc_v = xhc_ref[...]
        n = xhc_v * cg_ref[...] + cb_ref[...]
        sg = _sigmoid(n)
        s_ref[...] = (n * sg).astype(BF16)
        ds = _dot_nt(dzb, _wfull(w2_s))
        dn = ds * (sg * (1.0 + n * (1.0 - sg)))
        dcv, dg, db = _ln_bwd(dn, xhc_v, rsc_ref[...], cg_ref[...])
        dcv_ref[...] = dcv
        st_ref[0:1, :] += dg
        st_ref[1:2, :] += db
        st_ref[2:3, :] += jnp.sum(dcv, axis=0, keepdims=True)
        st_ref[3:4, :] += jnp.sum(dz, axis=0, keepdims=True)

    return pl.pallas_call(
        body, name="conv_bwd1", grid=(t // tm,),
        in_specs=[_tile(tm, d), _tile(tm, d), _tile(tm, 1), ANY, _row(d), _row(d)],
        out_specs=[_tile(tm, d), _tile(tm, d), _tile(tm, d), _fixed(8, d)],
        out_shape=[jax.ShapeDtypeStruct((t, d), BF16), jax.ShapeDtypeStruct((t, d), BF16),
                   jax.ShapeDtypeStruct((t, d), F32), jax.ShapeDtypeStruct((8, d), F32)],
        scratch_shapes=[_wscratch(lay, "pw2")],
        compiler_params=_params(),
    )(dz1, xhc, rsc, wb, cg, cb)


def _conv_bwd2(dz1, dcv, ag, wb, lay, w_dw, tm, hosted=None):
    t, d = dz1.shape
    nsteps = t // tm

    def body(dz_ref, dcv_ref, dcvn_ref, ag_ref, wb_ref, wdw_ref,
             gx_ref, dh_ref, dw_ref, db1_ref, w1_s, ubuf, dbuf, du_s, dwacc):
        i = pl.program_id(0)

        @pl.when(i == 0)
        def _():
            _load_weight(wb_ref, lay, "pw1t", w1_s)
            dbuf[pl.ds(HALO + tm, 8), :] = jnp.zeros((8, d), F32)
            dwacc[...] = jnp.zeros(dwacc.shape, F32)
            db1_ref[...] = jnp.zeros(db1_ref.shape, F32)

        dbuf[0:tm, :] = dcv_ref[...]
        dbuf[tm:tm + HALO, :] = jnp.where(i < nsteps - 1, dcvn_ref[...], 0.0)
        gx = ALPHA * dz_ref[...]
        for mc in range(d // CONV_MC):
            c0 = mc * CONV_MC
            acols, gcols = slice(c0, c0 + CONV_MC), slice(d + c0, d + c0 + CONV_MC)
            a = ag_ref[:, acols].astype(F32)
            sg = _sigmoid(ag_ref[:, gcols].astype(F32))
            ubuf[:, acols] = a * sg
            for rb in range(tm // CONV_RB):
                t0 = rb * CONV_RB
                for lc in range(CONV_MC // CONV_LC):
                    lanes = slice(c0 + lc * CONV_LC, c0 + (lc + 1) * CONV_LC)
                    dwin = dbuf[t0:t0 + CONV_WIN, lanes]
                    ucur = ubuf[t0:t0 + CONV_RB, lanes]
                    acc = jnp.zeros((CONV_RB, CONV_LC), F32)
                    for r in range(8):
                        dr = _shifted(dwin, r)
                        for k in range(CONV_WIDTH):
                            sd = CONV_WIDTH - 1 - k
                            if sd % 8 == r:
                                q = 8 * (sd // 8)
                                dk = dr[q:q + CONV_RB]
                                acc = acc + dk * wdw_ref[k:k + 1, lanes]
                                prod = ucur * dk
                                part = prod[0:8]
                                for j in range(1, CONV_RB // 8):
                                    part = part + prod[8 * j:8 * j + 8]
                                dwacc[k, :, lanes] += part
                    du_s[t0:t0 + CONV_RB, lanes] = acc
            du = du_s[:, acols]
            da = du * sg
            dg = du * a * sg * (1.0 - sg)
            dab, dgb = da.astype(BF16), dg.astype(BF16)
            dh_ref[:, acols] = dab
            dh_ref[:, gcols] = dgb
            db1_ref[:, acols] += jnp.sum(da, axis=0, keepdims=True)
            db1_ref[:, gcols] += jnp.sum(dg, axis=0, keepdims=True)
            gx = gx + _dot(dab, _wrows(w1_s, c0, CONV_MC)) + _dot(dgb, _wrows(w1_s, d + c0, CONV_MC))
        gx_ref[...] = gx

        @pl.when(i == nsteps - 1)
        def _():
            dw_ref[...] = jnp.sum(dwacc[...], axis=1)

    hb = tm // HALO
    last = t // HALO - 1
    return _gridded_call(
        body, "conv_bwd2", nsteps,
        [_tile(tm, d), _tile(tm, d),
         pl.BlockSpec((HALO, d), lambda i: (jnp.minimum((i + 1) * hb, last), 0)),
         _tile(tm, 2 * d), ANY, _fixed(HALO, d)],
        [_tile(tm, d), _tile(tm, 2 * d), _fixed(HALO, d), _row(2 * d)],
        [jax.ShapeDtypeStruct((t, d), F32), jax.ShapeDtypeStruct((t, 2 * d), BF16),
         jax.ShapeDtypeStruct((HALO, d), F32), jax.ShapeDtypeStruct((1, 2 * d), F32)],
        [_wscratch(lay, "pw1t"), pltpu.VMEM((tm, d), F32),
         pltpu.VMEM((HALO + tm + 8, d), F32), pltpu.VMEM((tm, d), F32),
         pltpu.VMEM((HALO, 8, d), F32)],
        (dz1, dcv, dcv, ag, wb, w_dw), hosted)


FFN_FC = 256


def _ffn_fwd(xh_in, g_in, b_in, wb, lay, layer, tm, *, kv=None, loss=None):
    t, d = xh_in.shape
    f = lay.f
    names = (f"gt{layer}", f"ut{layer}", f"dn{layer}")

    def body(*refs):
        xh_ref, gi_ref, bi_ref, wb_ref = refs[:4]
        pos = 4
        if kv is not None:
            go_ref, bo_ref, wkv_ref, bkv_ref = refs[pos:pos + 4]
            pos += 4
        if loss is not None:
            go_ref, bo_ref, tgt_ref = refs[pos:pos + 3]
            pos += 3
        xb_ref, hg_ref, hu_ref = refs[pos:pos + 3]
        pos += 3
        if kv is not None:
            xho_ref, rso_ref, xob_ref, kv_ref = refs[pos:pos + 4]
            pos += 4
        if loss is not None:
            dz_ref, st_ref, loss_ref = refs[pos:pos + 3]
            pos += 3
        gt_s, ut_s, dn_s = refs[pos:pos + 3]
        i = pl.program_id(0)

        @pl.when(i == 0)
        def _():
            for name, dst in zip(names, (gt_s, ut_s, dn_s)):
                _load_weight(wb_ref, lay, name, dst)
            if loss is not None:
                st_ref[...] = jnp.zeros(st_ref.shape, F32)
                loss_ref[...] = jnp.zeros(loss_ref.shape, F32)

        xin = xh_ref[...] * gi_ref[...] + bi_ref[...]
        xb = xin.astype(BF16)
        xb_ref[...] = xb
        def up(c):
            return (_dot_nt(xb, _wrows(gt_s, c * FFN_FC, FFN_FC)), _dot_nt(xb, _wrows(ut_s, c * FFN_FC, FFN_FC)))

        fo = jnp.zeros((tm, d), F32)
        nc = f // FFN_FC
        ahead = up(0)
        for c in range(nc):
            rows = slice(c * FFN_FC, (c + 1) * FFN_FC)
            hg, hu = ahead
            if c + 1 < nc:
                ahead = up(c + 1)
            hg_ref[:, rows] = hg.astype(BF16)
            hu_ref[:, rows] = hu.astype(BF16)
            act = hg * _sigmoid(hg) * hu
            fo = fo + _dot(act.astype(BF16), _wrows(dn_s, c * FFN_FC, FFN_FC))
        xho, rso = _ln_fwd(ALPHA * xin + fo)
        if kv is not None:
            xho_ref[...] = xho
            rso_ref[...] = rso
            xob = (xho * go_ref[...] + bo_ref[...]).astype(BF16)
            xob_ref[...] = xob
            kv_ref[...] = (_dot(xob, wkv_ref[...]) + bkv_ref[...]).astype(BF16)
        if loss is not None:
            diff = xho * go_ref[...] + bo_ref[...] - tgt_ref[...]
            loss_ref[...] += (0.5 / d) * jnp.sum(diff * diff)
            dz, dg, db = _ln_bwd(diff * (1.0 / d), xho, rso, go_ref[...])
            dz_ref[...] = dz
            st_ref[0:1, :] += dg
            st_ref[1:2, :] += db

    in_specs = [_tile(tm, d), _row(d), _row(d), ANY]
    args = [xh_in, g_in, b_in, wb]
    out_specs = [_tile(tm, d), _tile(tm, f), _tile(tm, f)]
    out_shape = [jax.ShapeDtypeStruct((t, d), BF16), jax.ShapeDtypeStruct((t, f), BF16),
                 jax.ShapeDtypeStruct((t, f), BF16)]
    if kv is not None:
        in_specs += [_row(d), _row(d), _fixed(d, 2 * BLOCK), _row(2 * BLOCK)]
        args += list(kv)
        out_specs += [_tile(tm, d), _tile(tm, 1), _tile(tm, d), _tile(tm, 2 * BLOCK)]
        out_shape += [jax.ShapeDtypeStruct((t, d), F32), jax.ShapeDtypeStruct((t, 1), F32),
                      jax.ShapeDtypeStruct((t, d), BF16), jax.ShapeDtypeStruct((t, 2 * BLOCK), BF16)]
    if loss is not None:
        in_specs += [_row(d), _row(d), _tile(tm, d)]
        args += list(loss)
        out_specs += [_tile(tm, d), _fixed(8, d), _fixed(8, 128)]
        out_shape += [jax.ShapeDtypeStruct((t, d), F32), jax.ShapeDtypeStruct((8, d), F32),
                      jax.ShapeDtypeStruct((8, 128), F32)]
    return pl.pallas_call(
        body, name=f"ffn_fwd{layer}", grid=(t // tm,), in_specs=in_specs, out_specs=out_specs,
        out_shape=out_shape, scratch_shapes=[_wscratch(lay, n) for n in names],
        compiler_params=_params(),
    )(*args)


def _ffn_bwd(dz, hg, hu, xh_in, rs_in, g_in, wb, lay, layer, tm, hosted=None):
    t, d = dz.shape
    f = lay.f
    names = (f"gt{layer}", f"ut{layer}", f"dn{layer}")

    def body(dz_ref, hg_ref, hu_ref, xh_ref, rs_ref, gi_ref, wb_ref,
             dzb_ref, act_ref, dhg_ref, dhu_ref, dzp_ref, st_ref, gt_s, ut_s, dn_s):
        i = pl.program_id(0)

        @pl.when(i == 0)
        def _():
            for name, dst in zip(names, (gt_s, ut_s, dn_s)):
                _load_weight(wb_ref, lay, name, dst)
            st_ref[...] = jnp.zeros(st_ref.shape, F32)

        dzv = dz_ref[...]
        dzb = dzv.astype(BF16)
        dzb_ref[...] = dzb
        dx = ALPHA * dzv
        def back(c):
            return _dot_nt(dzb, _wrows(dn_s, c * FFN_FC, FFN_FC))

        nc = f // FFN_FC
        ahead = back(0)
        for c in range(nc):
            rows = slice(c * FFN_FC, (c + 1) * FFN_FC)
            dact = ahead
            if c + 1 < nc:
                ahead = back(c + 1)
            hg_v = hg_ref[:, rows].astype(F32)
            hu_v = hu_ref[:, rows].astype(F32)
            sg = _sigmoid(hg_v)
            silu = hg_v * sg
            act_ref[:, rows] = (silu * hu_v).astype(BF16)
            dhu = (dact * silu).astype(BF16)
            dhg = (dact * hu_v * (sg * (1.0 + hg_v * (1.0 - sg)))).astype(BF16)
            dhu_ref[:, rows] = dhu
            dhg_ref[:, rows] = dhg
            dx = (dx + _dot(dhg, _wrows(gt_s, c * FFN_FC, FFN_FC))
                  + _dot(dhu, _wrows(ut_s, c * FFN_FC, FFN_FC)))
        dzp, dg, db = _ln_bwd(dx, xh_ref[...], rs_ref[...], gi_ref[...])
        dzp_ref[...] = dzp
        st_ref[0:1, :] += dg
        st_ref[1:2, :] += db

    return _gridded_call(
        body, f"ffn_bwd{layer}", t // tm,
        [_tile(tm, d), _tile(tm, f), _tile(tm, f), _tile(tm, d), _tile(tm, 1), _row(d), ANY],
        [_tile(tm, d), _tile(tm, f), _tile(tm, f), _tile(tm, f), _tile(tm, d), _fixed(8, d)],
        [jax.ShapeDtypeStruct((t, d), BF16), jax.ShapeDtypeStruct((t, f), BF16),
         jax.ShapeDtypeStruct((t, f), BF16), jax.ShapeDtypeStruct((t, f), BF16),
         jax.ShapeDtypeStruct((t, d), F32), jax.ShapeDtypeStruct((8, d), F32)],
        [_wscratch(lay, n) for n in names],
        (dz, hg, hu, xh_in, rs_in, g_in, wb), hosted)


def _alibi_slope(h, nq):
    return 2.0 ** (-ALIBI_MAX * (h + 1) / nq)


def _fill_alibi_bias(bias_s, nq):
    qi = lax.broadcasted_iota(jnp.int32, (BLOCK, 2 * BLOCK), 0)
    kj = lax.broadcasted_iota(jnp.int32, (BLOCK, 2 * BLOCK), 1)
    delta = qi + BLOCK - kj
    valid = (delta >= 0) & (delta < BLOCK)
    dist = jnp.where(valid, delta.astype(F32), MASKED_DIST)
    dist_first = jnp.where(kj >= BLOCK, dist, MASKED_DIST)
    for h in range(nq):
        bias_s[0, h] = _alibi_slope(h, nq) * dist
        bias_s[1, h] = _alibi_slope(h, nq) * dist_first


def _padded_kv(kvb, kvh):
    lane = lax.broadcasted_iota(jnp.int32, (2 * BLOCK, BLOCK), 1)
    mine = (lane < HEAD_DIM) if kvh == 0 else (lane >= HEAD_DIM)
    out = []
    for sec in (kvb[:, :BLOCK], kvb[:, BLOCK:]):
        m = jnp.where(mine, sec.astype(F32), 0.0)
        sw = pltpu.roll(m, HEAD_DIM, 1)
        pair = (m, sw) if kvh == 0 else (sw, m)
        out.append(tuple(p.astype(BF16) for p in pair))
    return out


def _attn_fwd(xh_in, g_in, b_in, x_in_b, kvs, wb, lay, bq, sinks, bo, tm):
    t, d = xh_in.shape
    nq = d // HEAD_DIM
    pairs_per_kv = (d // BLOCK) // N_KV_HEADS
    nbt = tm // BLOCK
    scale = HEAD_DIM ** -0.5

    def body(xh_ref, gi_ref, bi_ref, xb_ref, kv_ref, kvp_ref, wb_ref, bq_ref, sk_ref, bo_ref,
             q_ref, o_ref, lse_ref, xho_ref, rso_ref, wq_s, wo_s, kvall, q_s, o_s, bias_s):
        i = pl.program_id(0)

        @pl.when(i == 0)
        def _():
            _load_weight(wb_ref, lay, "wq", wq_s)
            _load_weight(wb_ref, lay, "wo", wo_s)
            _fill_alibi_bias(bias_s, nq)

        qv = ((_dot(xb_ref[...], _wfull(wq_s)) + bq_ref[...]) * scale).astype(BF16)
        q_s[...] = qv
        q_ref[...] = qv
        kvall[pl.ds(0, BLOCK), :] = kvp_ref[...]
        kvall[pl.ds(BLOCK, tm), :] = kv_ref[...]
        lane = lax.broadcasted_iota(jnp.int32, (BLOCK, BLOCK), 1)
        ones = jnp.ones((2 * BLOCK, BLOCK), BF16)

        def score_phase(j):
            rows = slice(j * BLOCK, (j + 1) * BLOCK)
            kvb = kvall[j * BLOCK:(j + 2) * BLOCK, :]
            first = (i * nbt + j == 0).astype(jnp.int32)
            pads = [_padded_kv(kvb, kvh) for kvh in range(N_KV_HEADS)]
            scores = []
            for a in range(d // BLOCK):
                kpad = pads[a // pairs_per_kv][0]
                qp = q_s[rows, a * BLOCK:(a + 1) * BLOCK]
                for e in range(2):
                    scores.append(_dot_nt(qp, kpad[e]) - bias_s[first, 2 * a + e])
            return rows, pads, scores

        def softmax_phase(state):
            rows, pads, scores = state
            probs, inv = [], []
            lse_t = jnp.zeros((BLOCK, BLOCK), F32)
            for h in range(nq):
                sink = sk_ref[:, h:h + 1]
                m = jnp.maximum(jnp.max(scores[h], axis=-1, keepdims=True), sink)
                p = jnp.exp(scores[h] - m).astype(BF16)
                l = _dot(p, ones) + jnp.exp(sink - m)
                lse_t = jnp.where(lane == h, m + jnp.log(l), lse_t)
                probs.append(p)
                inv.append(1.0 / l)
            lse_ref[rows, :] = lse_t
            return rows, pads, probs, inv

        def value_phase(state):
            rows, pads, probs, inv = state
            for a in range(d // BLOCK):
                vpad = pads[a // pairs_per_kv][1]
                opair = (_dot(probs[2 * a], vpad[0]) * inv[2 * a]
                         + _dot(probs[2 * a + 1], vpad[1]) * inv[2 * a + 1])
                o_s[rows, a * BLOCK:(a + 1) * BLOCK] = opair.astype(BF16)

        for state in [softmax_phase(s) for s in [score_phase(j) for j in range(nbt)]]:
            value_phase(state)
        ov = o_s[...]
        o_ref[...] = ov
        xin = xh_ref[...] * gi_ref[...] + bi_ref[...]
        xho, rso = _ln_fwd(ALPHA * xin + _dot(ov, _wfull(wo_s)) + bo_ref[...])
        xho_ref[...] = xho
        rso_ref[...] = rso

    return pl.pallas_call(
        body, name="attn_fwd", grid=(t // tm,),
        in_specs=[_tile(tm, d), _row(d), _row(d), _tile(tm, d), _tile(tm, 2 * BLOCK),
                  pl.BlockSpec((BLOCK, 2 * BLOCK), lambda i: (jnp.maximum(i * nbt - 1, 0), 0)),
                  ANY, _row(d), _row(nq), _row(d)],
        out_specs=[_tile(tm, d), _tile(tm, d), _tile(tm, BLOCK), _tile(tm, d), _tile(tm, 1)],
        out_shape=[jax.ShapeDtypeStruct((t, d), BF16), jax.ShapeDtypeStruct((t, d), BF16),
                   jax.ShapeDtypeStruct((t, BLOCK), F32), jax.ShapeDtypeStruct((t, d), F32),
                   jax.ShapeDtypeStruct((t, 1), F32)],
        scratch_shapes=[_wscratch(lay, "wq"), _wscratch(lay, "wo"),
                        pltpu.VMEM((BLOCK + tm, 2 * BLOCK), BF16), pltpu.VMEM((tm, d), BF16),
                        pltpu.VMEM((tm, d), BF16), pltpu.VMEM((2, nq, BLOCK, 2 * BLOCK), F32)],
        compiler_params=_params(),
    )(xh_in, g_in, b_in, x_in_b, kvs, kvs, wb, bq, sinks, bo)


def _attn_bwd(dz, q, o, lse, kvs, wb, lay, sinks, tm, hosted=None):
    t, d = dz.shape
    nq = d // HEAD_DIM
    pairs_per_kv = (d // BLOCK) // N_KV_HEADS
    nbt = tm // BLOCK
    scale = HEAD_DIM ** -0.5

    def body(dz_ref, q_ref, o_ref, lse_ref, kv_ref, kvp_ref, wb_ref, sk_ref,
             dzb_ref, dq_ref, dkc_ref, dkp_ref, st_ref, dsk_ref, wo_s, kvall, do_s, dq_s, bias_s):
        i = pl.program_id(0)

        @pl.when(i == 0)
        def _():
            _load_weight(wb_ref, lay, "wo", wo_s)
            _fill_alibi_bias(bias_s, nq)
            st_ref[...] = jnp.zeros(st_ref.shape, F32)
            dsk_ref[...] = jnp.zeros(dsk_ref.shape, F32)

        dzv = dz_ref[...]
        dzb = dzv.astype(BF16)
        dzb_ref[...] = dzb
        do_s[...] = _dot_nt(dzb, _wfull(wo_s))
        kvall[pl.ds(0, BLOCK), :] = kvp_ref[...]
        kvall[pl.ds(BLOCK, tm), :] = kv_ref[...]
        lane = lax.broadcasted_iota(jnp.int32, (BLOCK, BLOCK), 1)
        lane1 = lax.broadcasted_iota(jnp.int32, (1, BLOCK), 1)
        lane2 = lax.broadcasted_iota(jnp.int32, (2 * BLOCK, BLOCK), 1)
        halves = (lane < HEAD_DIM, lane >= HEAD_DIM)

        def score_phase(j):
            rows = slice(j * BLOCK, (j + 1) * BLOCK)
            kvb = kvall[j * BLOCK:(j + 2) * BLOCK, :]
            first = (i * nbt + j == 0).astype(jnp.int32)
            pads = [_padded_kv(kvb, kvh) for kvh in range(N_KV_HEADS)]
            scores, dps, dhs, qms, doms = [], [], [], [], []
            for a in range(d // BLOCK):
                kpad, vpad = pads[a // pairs_per_kv]
                cols = slice(a * BLOCK, (a + 1) * BLOCK)
                qp = q_ref[rows, cols]
                dop = do_s[rows, cols]
                dopb = dop.astype(BF16)
                prod = dop * o_ref[rows, cols].astype(F32)
                for e in range(2):
                    scores.append(_dot_nt(qp, kpad[e]) - bias_s[first, 2 * a + e])
                    dps.append(_dot_nt(dopb, vpad[e]))
                    dhs.append(jnp.sum(jnp.where(halves[e], prod, 0.0), axis=-1, keepdims=True))
                    qms.append(jnp.where(halves[e], qp, jnp.zeros_like(qp)))
                    doms.append(jnp.where(halves[e], dopb, jnp.zeros_like(dopb)))
            return rows, pads, scores, dps, dhs, qms, doms

        def softmax_phase(state):
            rows, pads, scores, dps, dhs, qms, doms = state
            dss, pbs = [], []
            dsk_t = jnp.zeros((1, BLOCK), F32)
            for h in range(nq):
                lse_h = lse_ref[rows, h:h + 1]
                p = jnp.exp(scores[h] - lse_h)
                dss.append((p * (dps[h] - dhs[h])).astype(BF16))
                pbs.append(p.astype(BF16))
                dsink = -jnp.sum(jnp.exp(sk_ref[:, h:h + 1] - lse_h) * dhs[h], axis=0, keepdims=True)
                dsk_t = jnp.where(lane1 == h, dsink, dsk_t)
            dsk_ref[...] += dsk_t
            return rows, pads, dss, pbs, qms, doms

        def grad_phase(state):
            rows, pads, dss, pbs, qms, doms = state
            dsecs = []
            for kvh in range(N_KV_HEADS):
                kpad = pads[kvh][0]
                dk_acc = jnp.zeros((2 * BLOCK, BLOCK), F32)
                dv_acc = jnp.zeros((2 * BLOCK, BLOCK), F32)
                for a in range(kvh * pairs_per_kv, (kvh + 1) * pairs_per_kv):
                    dqp = _dot(dss[2 * a], kpad[0]) + _dot(dss[2 * a + 1], kpad[1])
                    dq_s[rows, a * BLOCK:(a + 1) * BLOCK] = dqp * scale
                    for e in range(2):
                        h = 2 * a + e
                        dk_acc = dk_acc + _dot_tn(dss[h], qms[h])
                        dv_acc = dv_acc + _dot_tn(pbs[h], doms[h])
                dsecs.append((dk_acc + pltpu.roll(dk_acc, HEAD_DIM, 1), dv_acc + pltpu.roll(dv_acc, HEAD_DIM, 1)))
            lo = lane2 < HEAD_DIM
            dkv = jnp.concatenate([jnp.where(lo, dsecs[0][0], dsecs[1][0]),
                                   jnp.where(lo, dsecs[0][1], dsecs[1][1])], axis=1)
            dkp_ref[rows, :] = dkv[:BLOCK]
            dkc_ref[rows, :] = dkv[BLOCK:]

        for state in [softmax_phase(s) for s in [score_phase(j) for j in range(nbt)]]:
            grad_phase(state)
        dqv = dq_s[...]
        dq_ref[...] = dqv.astype(BF16)
        st_ref[0:1, :] += jnp.sum(dqv, axis=0, keepdims=True)
        st_ref[1:2, :] += jnp.sum(dzv, axis=0, keepdims=True)

    return _gridded_call(
        body, "attn_bwd", t // tm,
        [_tile(tm, d), _tile(tm, d), _tile(tm, d), _tile(tm, BLOCK), _tile(tm, 2 * BLOCK),
         pl.BlockSpec((BLOCK, 2 * BLOCK), lambda i: (jnp.maximum(i * nbt - 1, 0), 0)),
         ANY, _row(nq)],
        [_tile(tm, d), _tile(tm, d), _tile(tm, 2 * BLOCK), _tile(tm, 2 * BLOCK),
         _fixed(8, d), _row(BLOCK)],
        [jax.ShapeDtypeStruct((t, d), BF16), jax.ShapeDtypeStruct((t, d), BF16),
         jax.ShapeDtypeStruct((t, 2 * BLOCK), F32), jax.ShapeDtypeStruct((t, 2 * BLOCK), F32),
         jax.ShapeDtypeStruct((8, d), F32), jax.ShapeDtypeStruct((1, BLOCK), F32)],
        [_wscratch(lay, "wo"), pltpu.VMEM((BLOCK + tm, 2 * BLOCK), BF16),
         pltpu.VMEM((tm, d), F32), pltpu.VMEM((tm, d), F32),
         pltpu.VMEM((2, nq, BLOCK, 2 * BLOCK), F32)],
        (dz, q, o, lse, kvs, kvs, wb, sinks), hosted)


def _x2_bwd(dz, dq, dkc, dkp, xh_in, rs_in, g_in, wb, lay, wkv, tm):
    t, d = dz.shape
    nbt = tm // BLOCK
    nsteps = t // tm
    scale = HEAD_DIM ** -0.5

    def body(dz_ref, dq_ref, dkc_ref, dkp_ref, dkn_ref, xh_ref, rs_ref, gi_ref, wb_ref, wkv_ref,
             dkv_ref, dzp_ref, st_ref, dbkv_ref, wq_s):
        i = pl.program_id(0)

        @pl.when(i == 0)
        def _():
            _load_weight(wb_ref, lay, "wq", wq_s)
            st_ref[...] = jnp.zeros(st_ref.shape, F32)
            dbkv_ref[...] = jnp.zeros(dbkv_ref.shape, F32)

        nxt = jnp.where(i < nsteps - 1, dkn_ref[...], 0.0)
        if nbt > 1:
            shifted = jnp.concatenate([dkp_ref[pl.ds(BLOCK, tm - BLOCK), :], nxt], axis=0)
        else:
            shifted = nxt
        dkv = dkc_ref[...] + shifted
        dkvb = dkv.astype(BF16)
        dkv_ref[...] = dkvb
        dbkv_ref[...] += jnp.sum(dkv, axis=0, keepdims=True)
        dx = ALPHA * dz_ref[...] + _dot_nt(dq_ref[...], _wfull(wq_s)) + _dot_nt(dkvb, wkv_ref[...])
        dzp, dg, db = _ln_bwd(dx, xh_ref[...], rs_ref[...], gi_ref[...])
        dzp_ref[...] = dzp
        st_ref[0:1, :] += dg
        st_ref[1:2, :] += db

    del scale
    last = t // BLOCK - 1
    return pl.pallas_call(
        body, name="x2_bwd", grid=(nsteps,),
        in_specs=[_tile(tm, d), _tile(tm, d), _tile(tm, 2 * BLOCK), _tile(tm, 2 * BLOCK),
                  pl.BlockSpec((BLOCK, 2 * BLOCK), lambda i: (jnp.minimum((i + 1) * nbt, last), 0)),
                  _tile(tm, d), _tile(tm, 1), _row(d), ANY, _fixed(d, 2 * BLOCK)],
        out_specs=[_tile(tm, 2 * BLOCK), _tile(tm, d), _fixed(8, d), _row(2 * BLOCK)],
        out_shape=[jax.ShapeDtypeStruct((t, 2 * BLOCK), BF16), jax.ShapeDtypeStruct((t, d), F32),
                   jax.ShapeDtypeStruct((8, d), F32), jax.ShapeDtypeStruct((1, 2 * BLOCK), F32)],
        scratch_shapes=[_wscratch(lay, "wq")],
        compiler_params=_params(),
    )(dz, dq, dkc, dkp, dkp, xh_in, rs_in, g_in, wb, wkv)


def _tn_matmul(a, b, name, bm, tk):
    t, m = a.shape
    n = b.shape[1]
    ksteps = t // tk

    def body(a_ref, b_ref, o_ref):
        k = pl.program_id(1)
        part = _dot_tn(a_ref[...], b_ref[...])

        @pl.when(k == 0)
        def _():
            o_ref[...] = part

        @pl.when(k > 0)
        def _():
            o_ref[...] += part

    return pl.pallas_call(
        body, name=name, grid=(m // bm, ksteps),
        in_specs=[pl.BlockSpec((tk, bm), lambda j, k: (k, j)), pl.BlockSpec((tk, n), lambda j, k: (k, 0))],
        out_specs=pl.BlockSpec((bm, n), lambda j, k: (j, 0)),
        out_shape=jax.ShapeDtypeStruct((m, n), F32),
        compiler_params=pltpu.CompilerParams(dimension_semantics=("arbitrary", "arbitrary"),
                                             vmem_limit_bytes=VMEM_LIMIT),
    )(a, b)


def _all_gather(arrays, name):
    n = len(arrays)

    def body(*refs):
        ins, outs = refs[:n], refs[n:2 * n]
        send_sems, recv_sems, local_sems = refs[2 * n:]
        x, y, c = _me()
        me, sibling = (x, y, c), (x, y, 1 - c)
        chips = [(1 - x, y), (x, 1 - y), (1 - x, 1 - y)]

        def slot(ref, dev):
            return ref.at[4 * dev[0] + 2 * dev[1] + dev[2]]

        def copy(a, k, block, to, src=None):
            return pltpu.make_async_remote_copy(
                src_ref=slot(outs[a], block) if src is None else src, dst_ref=slot(outs[a], block),
                send_sem=send_sems.at[a, k], recv_sem=recv_sems.at[a, k], device_id=to, device_id_type=MESH)

        mine = [pltpu.make_async_copy(ins[a], slot(outs[a], me), local_sems.at[a]) for a in range(n)]
        for cp in mine:
            cp.start()
        first = []
        for a in range(n):
            first.append(copy(a, 0, me, sibling, src=ins[a]))
            first += [copy(a, 1 + j, me, (*chip, c), src=ins[a]) for j, chip in enumerate(chips)]
        for cp in first:
            cp.start()
        passed = []
        for a in range(n):
            for j, chip in enumerate(chips):
                copy(a, 1 + j, (*chip, c), me).wait_recv()
                cp = copy(a, 4 + j, (*chip, c), sibling)
                cp.start()
                passed.append(cp)
        for a in range(n):
            copy(a, 0, sibling, me).wait_recv()
            for j, chip in enumerate(chips):
                copy(a, 4 + j, (*chip, 1 - c), me).wait_recv()
        for cp in first + passed:
            cp.wait_send()
        for cp in mine:
            cp.wait()

    return pl.pallas_call(
        body, name=name, in_specs=[ANY] * n, out_specs=[ANY] * n,
        out_shape=[jax.ShapeDtypeStruct((N_DEV,) + a.shape, a.dtype) for a in arrays],
        scratch_shapes=[pltpu.SemaphoreType.DMA((n, 7)), pltpu.SemaphoreType.DMA((n, 7)),
                        pltpu.SemaphoreType.DMA((n,))],
    )(*arrays)


def _exchange(arrays, name):
    n = len(arrays)
    blocked = [a.ndim == 3 for a in arrays]

    def body(*refs):
        ins, outs = refs[:n], refs[n:2 * n]
        send_sems, recv_sems, local_sems = refs[2 * n:]
        me = _index(_me())

        def src(k, dev):
            return ins[k].at[dev] if blocked[k] else ins[k]

        local = [pltpu.make_async_copy(src(k, me), outs[k].at[me], local_sems.at[k]) for k in range(n)]
        sends, arrivals = [], []
        for k in range(n):
            for mask in range(1, N_DEV):
                peer = _peer(mask)
                sends.append(pltpu.make_async_remote_copy(
                    src_ref=src(k, _index(peer)), dst_ref=outs[k].at[me], send_sem=send_sems.at[k, mask - 1],
                    recv_sem=recv_sems.at[k, mask - 1], device_id=peer, device_id_type=MESH))
                arrivals.append(pltpu.make_async_remote_copy(
                    src_ref=src(k, me), dst_ref=outs[k].at[_index(peer)], send_sem=send_sems.at[k, mask - 1],
                    recv_sem=recv_sems.at[k, mask - 1], device_id=_me(), device_id_type=MESH))
        for cp in local + sends:
            cp.start()
        for cp in arrivals:
            cp.wait_recv()
        for cp in sends:
            cp.wait_send()
        for cp in local:
            cp.wait()

    return pl.pallas_call(
        body, name=name, in_specs=[ANY] * n, out_specs=[ANY] * n,
        out_shape=[jax.ShapeDtypeStruct((N_DEV,) + a.shape[-2:], a.dtype) for a in arrays],
        scratch_shapes=[pltpu.SemaphoreType.DMA((n, 7)), pltpu.SemaphoreType.DMA((n, 7)),
                        pltpu.SemaphoreType.DMA((n,))],
    )(*arrays)


def _adamw_sum(g8, w, m, v, name, tr):
    r, width = w.shape
    bc1 = 1.0 - ADAM_B1 ** ADAM_STEP
    bc2 = 1.0 - ADAM_B2 ** ADAM_STEP

    def body(g_ref, w_ref, m_ref, v_ref, go_ref, d_ref, mo_ref, vo_ref):
        g = g_ref[0].astype(F32)
        for s in range(1, N_DEV):
            g = g + g_ref[s].astype(F32)
        mn = ADAM_B1 * m_ref[...] + (1.0 - ADAM_B1) * g
        vn = ADAM_B2 * v_ref[...] + (1.0 - ADAM_B2) * (g * g)
        m_hat = mn / bc1
        v_hat = vn / bc2
        go_ref[...] = g
        d_ref[...] = -ADAM_LR * (m_hat / (jnp.sqrt(v_hat) + ADAM_EPS) + ADAM_WD * w_ref[...])
        mo_ref[...] = mn
        vo_ref[...] = vn

    spec = pl.BlockSpec((tr, width), lambda i: (i, 0))
    return pl.pallas_call(
        body, name=name, grid=(r // tr,),
        in_specs=[pl.BlockSpec((N_DEV, tr, width), lambda i: (0, i, 0)), spec, spec, spec],
        out_specs=[spec] * 4, out_shape=[jax.ShapeDtypeStruct((r, width), F32)] * 4,
        compiler_params=_params(),
    )(g8, w, m, v)


def _local_step(x, target, wba, shard_b, lay, sm, tm, tk):
    t, d = x.shape
    f = lay.f
    w_dw32 = jnp.concatenate([sm["w_dw"], jnp.zeros((HALO - CONV_WIDTH, d), F32)], axis=0)
    lmg, lmb, lfg, lfb = sm["ln_mix_g"], sm["ln_mix_b"], sm["ln_ffn_g"], sm["ln_ffn_b"]
    bkv = jnp.concatenate([sm["b_k"], sm["b_v"]], axis=1)
    bm_f = f // 2 if (f // 2) % 128 == 0 else f

    def exchange(group, grads):
        names = lay.GRADS[group]
        return _HostedExchange([grads[n].reshape(N_DEV, lay.n[n], d) for n in names],
                               [lay.roff[n] for n in names], lay.rrows[group])

    xb0, ag, xhc, rsc, xh1, rs1, wbb = _conv_fwd(x, wba, lay, w_dw32, sm["b_pw1"], sm["b_dw"], sm["cg"],
                                                 sm["cb"], sm["b_pw2"], lmg[0:1], lmb[0:1], tm,
                                                 hosted=_HostedGather(shard_b))
    wkv = wbb[:, lay.goff["wkv"]:lay.goff["wkv"] + lay.n["wkv"], :].reshape(d, 2 * BLOCK)
    x1b, hg0, hu0, xh2, rs2, x2b, kvs = _ffn_fwd(xh1, lmg[0:1], lmb[0:1], wbb, lay, 0, tm,
                                                kv=(lfg[0:1], lfb[0:1], wkv, bkv))
    q, o, lse, xh3, rs3 = _attn_fwd(xh2, lfg[0:1], lfb[0:1], x2b, kvs, wbb, lay, sm["b_q"], sm["sinks"],
                                    sm["b_o"], tm)
    x3b, hg1, hu1, dz4, st4, loss = _ffn_fwd(xh3, lmg[1:2], lmb[1:2], wbb, lay, 1, tm,
                                             loss=(lfg[1:2], lfb[1:2], target))

    dz4b, act1, dhg1, dhu1, dz3, st3 = _ffn_bwd(dz4, hg1, hu1, xh3, rs3, lmg[1:2], wbb, lay, 1, tm)
    g1 = {"gt1": _tn_matmul(dhg1, x3b, "dw_gate1", bm_f, tk), "ut1": _tn_matmul(dhu1, x3b, "dw_up1", bm_f, tk),
          "dn1": _tn_matmul(act1, dz4b, "dw_down1", bm_f, tk)}
    dz3b, dq, dkc, dkp, stq, dsinks, recv1 = _attn_bwd(dz3, q, o, lse, kvs, wbb, lay, sm["sinks"], tm,
                                                       hosted=exchange("r1", g1))
    dkv, dz2, st2, dbkv = _x2_bwd(dz3, dq, dkc, dkp, xh2, rs2, lfg[0:1], wbb, lay, wkv, tm)
    g2 = {"wq": _tn_matmul(x2b, dq, "dw_q", d, tk), "wo": _tn_matmul(o, dz3b, "dw_o", d, tk),
          "wkv": _tn_matmul(x2b, dkv, "dw_kv", d, tk)}
    dz2b, act0, dhg0, dhu0, dz1, st1, recv2 = _ffn_bwd(dz2, hg0, hu0, xh1, rs1, lmg[0:1], wbb, lay, 0, tm,
                                                       hosted=exchange("r2", g2))
    dz1b, s_act, dcv, stc = _conv_bwd1(dz1, xhc, rsc, wba, lay, sm["cg"], sm["cb"], tm)
    g3 = {"gt0": _tn_matmul(dhg0, x1b, "dw_gate0", bm_f, tk), "ut0": _tn_matmul(dhu0, x1b, "dw_up0", bm_f, tk),
          "dn0": _tn_matmul(act0, dz2b, "dw_down0", bm_f, tk), "pw2": _tn_matmul(s_act, dz1b, "dw_pw2", d, tk)}
    grad_x, dh1, dwdw, db1, recv3 = _conv_bwd2(dz1, dcv, ag, wba, lay, w_dw32, tm, hosted=exchange("r3", g3))
    g_pw1t = _tn_matmul(dh1, xb0, "dw_pw1", d, tk)
    received = {"r1": recv1, "r2": recv2, "r3": recv3}
    small = {
        "w_dw": dwdw[:CONV_WIDTH], "b_pw1": db1, "b_dw": stc[2:3], "cg": stc[0:1], "cb": stc[1:2],
        "b_pw2": stc[3:4], "b_k": dbkv[:, :BLOCK], "b_v": dbkv[:, BLOCK:], "b_q": stq[0:1],
        "sinks": dsinks[:, :d // HEAD_DIM],
        "b_o": stq[1:2],
        "ln_mix_g": jnp.concatenate([st1[0:1], st3[0:1]], axis=0),
        "ln_mix_b": jnp.concatenate([st1[1:2], st3[1:2]], axis=0),
        "ln_ffn_g": jnp.concatenate([st2[0:1], st4[0:1]], axis=0),
        "ln_ffn_b": jnp.concatenate([st2[1:2], st4[1:2]], axis=0),
    }
    return loss[0, 0], grad_x, received, g_pw1t, small


SP_ROWS = 40
SP_BDW, SP_CG, SP_CB, SP_BPW2, SP_BPW1 = 32, 33, 34, 35, 36
RP_NAMES = ("ln_mix_g", "ln_mix_b", "ln_ffn_g", "ln_ffn_b", "b_q", "b_o", "b_k", "b_v", "sinks")


def _row_forms(d, pw1, pw2, wq, wo, gate, up, down, wk, wv):
    rf = {"pw1t": pw1[0].T, "pw2": pw2[0], "wq": wq[0], "wo": wo[0],
          "wkv": jnp.concatenate([wk, wv], axis=1).reshape(-1, d)}
    for l in range(DEPTH):
        rf.update({f"gt{l}": gate[l].T, f"ut{l}": up[l].T, f"dn{l}": down[l]})
    return rf


def _from_row_forms(d, rf):
    kvw = rf["wkv"].reshape(d // N_DEV, 2 * BLOCK)
    return dict(
        pw1=rf["pw1t"].T[None], pw2=rf["pw2"][None], wq=rf["wq"][None], wo=rf["wo"][None],
        gate=jnp.stack([rf[f"gt{l}"].T for l in range(DEPTH)]),
        up=jnp.stack([rf[f"ut{l}"].T for l in range(DEPTH)]),
        down=jnp.stack([rf[f"dn{l}"] for l in range(DEPTH)]),
        wk=kvw[:, :BLOCK], wv=kvw[:, BLOCK:])


def _pack_rows(rf, names):
    return jnp.concatenate([rf[n] for n in names], axis=0)


def _pack_small(w_dw, b_dw, cg, cb, b_pw2, b_pw1):
    cw = b_dw.shape[1]
    z = jnp.zeros((1, cw), F32)
    return jnp.concatenate([w_dw[0], z, b_dw, cg, cb, b_pw2, b_pw1.reshape(2, cw), z, z], axis=0)


def _unpack_small(p):
    cw = p.shape[1]
    return dict(w_dw=p[None, :CONV_WIDTH], b_dw=p[SP_BDW:SP_BDW + 1], cg=p[SP_CG:SP_CG + 1],
                cb=p[SP_CB:SP_CB + 1], b_pw2=p[SP_BPW2:SP_BPW2 + 1],
                b_pw1=p[SP_BPW1:SP_BPW1 + 2].reshape(1, 2 * cw))


def _small_full(g):
    d = N_DEV * g.shape[2]

    def wide(r0, n=1):
        return jnp.transpose(g[:, r0:r0 + n], (1, 0, 2)).reshape(n, d)

    return dict(w_dw=wide(0, CONV_WIDTH), b_dw=wide(SP_BDW), cg=wide(SP_CG), cb=wide(SP_CB),
                b_pw2=wide(SP_BPW2), b_pw1=g[:, SP_BPW1:SP_BPW1 + 2].reshape(1, 2 * d))


def _small_grad_blocks(sg):
    cw = sg["b_dw"].shape[1] // N_DEV

    def narrow(a):
        return jnp.transpose(a.reshape(a.shape[0], N_DEV, cw), (1, 0, 2))

    z = jnp.zeros((N_DEV, 1, cw), F32)
    return jnp.concatenate([narrow(sg["w_dw"]), z, narrow(sg["b_dw"]), narrow(sg["cg"]), narrow(sg["cb"]),
                            narrow(sg["b_pw2"]), sg["b_pw1"].reshape(N_DEV, 2, cw), z, z], axis=1)


def _pack_rep(vals):
    parts = []
    for name in RP_NAMES:
        a = vals[name].reshape(-1)
        pad = -a.shape[0] % 128
        parts.append(jnp.concatenate([a, jnp.zeros((pad,), F32)]).reshape(-1, 128))
    rows = sum(p.shape[0] for p in parts)
    parts.append(jnp.zeros((-rows % 8, 128), F32))
    return jnp.concatenate(parts, axis=0)


def _unpack_rep(p, shapes):
    out, r = {}, 0
    for name in RP_NAMES:
        n = 1
        for s in shapes[name]:
            n *= s
        rows = -(-n // 128)
        out[name] = p[r:r + rows].reshape(-1)[:n].reshape(shapes[name])
        r += rows
    return out


def kernel(x, conv_w_pw1, conv_b_pw1, conv_w_dw, conv_b_dw, conv_ln_g, conv_ln_b, conv_w_pw2, conv_b_pw2, kv_w_k, kv_b_k, kv_w_v, kv_b_v, attn_w_q, attn_b_q, attn_sinks, attn_w_o, attn_b_o, ffn_w_gate, ffn_w_up, ffn_w_down, ln_mix_g, ln_mix_b, ln_ffn_g, ln_ffn_b, loss_target, m_conv_w_pw1, m_conv_b_pw1, m_conv_w_dw, m_conv_b_dw, m_conv_ln_g, m_conv_ln_b, m_conv_w_pw2, m_conv_b_pw2, m_kv_w_k, m_kv_b_k, m_kv_w_v, m_kv_b_v, m_attn_w_q, m_attn_b_q, m_attn_sinks, m_attn_w_o, m_attn_b_o, m_ffn_w_gate, m_ffn_w_up, m_ffn_w_down, m_ln_mix_g, m_ln_mix_b, m_ln_ffn_g, m_ln_ffn_b, v_conv_w_pw1, v_conv_b_pw1, v_conv_w_dw, v_conv_b_dw, v_conv_ln_g, v_conv_ln_b, v_conv_w_pw2, v_conv_b_pw2, v_kv_w_k, v_kv_b_k, v_kv_w_v, v_kv_b_v, v_attn_w_q, v_attn_b_q, v_attn_sinks, v_attn_w_o, v_attn_b_o, v_ffn_w_gate, v_ffn_w_up, v_ffn_w_down, v_ln_mix_g, v_ln_mix_b, v_ln_ffn_g, v_ln_ffn_b):
    t, d = x.shape[1], x.shape[2]
    f = ffn_w_gate.shape[2] * N_DEV
    lay = _Layout(d, f)
    tm, tk = 256, min(2048, t)

    rep_shapes = dict(ln_mix_g=ln_mix_g.shape, ln_mix_b=ln_mix_b.shape, ln_ffn_g=ln_ffn_g.shape,
                      ln_ffn_b=ln_ffn_b.shape, b_q=attn_b_q.shape, b_o=attn_b_o.shape, b_k=kv_b_k.shape,
                      b_v=kv_b_v.shape, sinks=attn_sinks.shape)

    def rep_pack(lmg, lmb, lfg, lfb, bq, bo, bk, bv, sk):
        return _pack_rep(dict(ln_mix_g=lmg, ln_mix_b=lmb, ln_ffn_g=lfg, ln_ffn_b=lfb, b_q=bq, b_o=bo,
                              b_k=bk, b_v=bv, sinks=sk))

    w_rf = _row_forms(d, conv_w_pw1, conv_w_pw2, attn_w_q, attn_w_o, ffn_w_gate, ffn_w_up, ffn_w_down, kv_w_k, kv_w_v)
    m_rf = _row_forms(d, m_conv_w_pw1, m_conv_w_pw2, m_attn_w_q, m_attn_w_o, m_ffn_w_gate, m_ffn_w_up, m_ffn_w_down, m_kv_w_k, m_kv_w_v)
    v_rf = _row_forms(d, v_conv_w_pw1, v_conv_w_pw2, v_attn_w_q, v_attn_w_o, v_ffn_w_gate, v_ffn_w_up, v_ffn_w_down, v_kv_w_k, v_kv_w_v)
    w_small = _pack_small(conv_w_dw, conv_b_dw, conv_ln_g, conv_ln_b, conv_b_pw2, conv_b_pw1)
    m_small = _pack_small(m_conv_w_dw, m_conv_b_dw, m_conv_ln_g, m_conv_ln_b, m_conv_b_pw2, m_conv_b_pw1)
    v_small = _pack_small(v_conv_w_dw, v_conv_b_dw, v_conv_ln_g, v_conv_ln_b, v_conv_b_pw2, v_conv_b_pw1)
    w_rep = rep_pack(ln_mix_g, ln_mix_b, ln_ffn_g, ln_ffn_b, attn_b_q, attn_b_o, kv_b_k, kv_b_v, attn_sinks)
    m_rep = rep_pack(m_ln_mix_g, m_ln_mix_b, m_ln_ffn_g, m_ln_ffn_b, m_attn_b_q, m_attn_b_o, m_kv_b_k, m_kv_b_v, m_attn_sinks)
    v_rep = rep_pack(v_ln_mix_g, v_ln_mix_b, v_ln_ffn_g, v_ln_ffn_b, v_attn_b_q, v_attn_b_o, v_kv_b_k, v_kv_b_v, v_attn_sinks)

    wba, smg = _all_gather([_pack_rows(w_rf, lay.GATHER["a"]).astype(BF16), w_small], "gather_conv_weights")
    shard_b = _pack_rows(w_rf, lay.GATHER["b"]).astype(BF16)
    sm = _small_full(smg)
    sm.update(ln_mix_g=ln_mix_g, ln_mix_b=ln_mix_b, ln_ffn_g=ln_ffn_g, ln_ffn_b=ln_ffn_b, b_q=attn_b_q,
              b_o=attn_b_o, sinks=attn_sinks, b_k=kv_b_k.reshape(1, -1), b_v=kv_b_v.reshape(1, -1))

    loss_part, grad_x, received, g_pw1t, gsmall = _local_step(x[0], loss_target[0], wba, shard_b, lay, sm, tm, tk)
    loss = lax.psum(loss_part, ("x", "y", "c"))

    received["r4"], g8_small, g8_rep = _exchange(
        [g_pw1t.astype(BF16).reshape(N_DEV, lay.n["pw1t"], d), _small_grad_blocks(gsmall), _pack_rep(gsmall)],
        "exchange_last_grads")

    big_rf = [{}, {}, {}, {}]
    for group, names in lay.GRADS.items():
        rows = lay.rrows[group]
        tr = max(r for r in range(8, 129, 8) if rows % r == 0)
        res = _adamw_sum(received[group], _pack_rows(w_rf, names), _pack_rows(m_rf, names), _pack_rows(v_rf, names),
                         f"adamw_{group}", tr)
        for out_rf, a in zip(big_rf, res):
            for n in names:
                out_rf[n] = a[lay.roff[n]:lay.roff[n] + lay.n[n]]
    big_out = [_from_row_forms(d, rf) for rf in big_rf]
    small_out = [_unpack_small(a) for a in _adamw_sum(g8_small, w_small, m_small, v_small, "adamw_small", SP_ROWS)]
    rep_out = [_unpack_rep(a, rep_shapes)
               for a in _adamw_sum(g8_rep, w_rep, m_rep, v_rep, "adamw_rep", w_rep.shape[0])]

    outs = [loss, grad_x[None]]
    for b, s, r in zip(big_out, small_out, rep_out):
        outs += [b["pw1"], s["b_pw1"], s["w_dw"], s["b_dw"], s["cg"], s["cb"], b["pw2"], s["b_pw2"],
                 b["wk"], r["b_k"], b["wv"], r["b_v"], b["wq"], r["b_q"], r["sinks"], b["wo"], r["b_o"],
                 b["gate"], b["up"], b["down"], r["ln_mix_g"], r["ln_mix_b"], r["ln_ffn_g"], r["ln_ffn_b"]]
    return tuple(outs)
```

```python
import functools

import jax
import jax.numpy as jnp
from jax import lax
from jax.experimental import pallas as pl
from jax.experimental.pallas import tpu as pltpu

F32 = jnp.float32
BF16 = jnp.bfloat16

N_DEV = 8
HEAD_DIM = 64
N_KV_HEADS = 2
BLOCK = 128
CONV_WIDTH = 31
HALO = 32
ALIBI_MAX = 8.0
DEPTH = 2
ALPHA = (2.0 * DEPTH) ** 0.25
LN_EPS = 1e-5
MASKED_DIST = 1e32
ADAM_LR = 0.001
ADAM_B1 = 0.9
ADAM_B2 = 0.999
ADAM_EPS = 1e-08
ADAM_WD = 0.01
ADAM_STEP = 10
VMEM_LIMIT = 56 * 1024 * 1024
MESH = pl.DeviceIdType.MESH


def _dot(a, b):
    return jnp.dot(a, b, preferred_element_type=F32)


def _dot_nt(a, b):
    return lax.dot_general(a, b, (((1,), (1,)), ((), ())), preferred_element_type=F32)


def _dot_tn(a, b):
    return lax.dot_general(a, b, (((0,), (0,)), ((), ())), preferred_element_type=F32)


def _sigmoid(v):
    return 1.0 / (1.0 + jnp.exp(-v))


def _ln_fwd(z):
    mu = jnp.mean(z, axis=-1, keepdims=True)
    zc = z - mu
    var = jnp.mean(zc * zc, axis=-1, keepdims=True)
    rstd = lax.rsqrt(var + LN_EPS)
    return zc * rstd, rstd


def _ln_bwd(dout, xh, rstd, g):
    dxh = dout * g
    m1 = jnp.mean(dxh, axis=-1, keepdims=True)
    m2 = jnp.mean(dxh * xh, axis=-1, keepdims=True)
    dz = rstd * (dxh - m1 - xh * m2)
    return dz, jnp.sum(dout * xh, axis=0, keepdims=True), jnp.sum(dout, axis=0, keepdims=True)


def _params(vmem=VMEM_LIMIT):
    return pltpu.CompilerParams(dimension_semantics=("arbitrary",), vmem_limit_bytes=vmem)


def _row(d):
    return pl.BlockSpec((1, d), lambda i: (0, 0))


def _tile(tm, d):
    return pl.BlockSpec((tm, d), lambda i: (i, 0))


def _fixed(r, d):
    return pl.BlockSpec((r, d), lambda i: (0, 0))


ANY = pl.BlockSpec(memory_space=pl.ANY)


class _Layout:
    GATHER = {"a": ("pw1t", "pw2"),
              "b": ("wq", "wo", "gt0", "ut0", "dn0", "gt1", "ut1", "dn1", "wkv")}
    GRADS = {"r1": ("gt1", "ut1", "dn1"), "r2": ("wq", "wo"), "r3": ("gt0", "ut0", "dn0", "pw2", "wkv"),
             "r4": ("pw1t",)}

    def __init__(self, d, f):
        self.d, self.f = d, f
        self.n = {"pw1t": 2 * d // N_DEV, "pw2": d // N_DEV, "wq": d // N_DEV, "wo": d // N_DEV,
                  "wkv": (d // N_DEV) * 2 * BLOCK // d}
        for l in range(DEPTH):
            self.n.update({f"gt{l}": f // N_DEV, f"ut{l}": f // N_DEV, f"dn{l}": f // N_DEV})
        self.goff, self.grows = self._offsets(self.GATHER)
        self.roff, self.rrows = self._offsets(self.GRADS)

    def _offsets(self, groups):
        off, rows = {}, {}
        for g, names in groups.items():
            r = 0
            for name in names:
                off[name] = r
                r += self.n[name]
            rows[g] = r
        return off, rows


def _load_weight(wb_ref, lay, name, dst):
    n = lay.n[name]
    for p in range(N_DEV):
        pltpu.sync_copy(wb_ref.at[p, pl.ds(lay.goff[name], n), :], dst.at[pl.ds(p * n, n), :])


def _wscratch(lay, name):
    return pltpu.VMEM((N_DEV * lay.n[name], lay.d), BF16)


def _wfull(ref):
    return ref[...]


def _wrows(ref, r0, nrows):
    return ref[r0:r0 + nrows, :]


def _me():
    return lax.axis_index("x"), lax.axis_index("y"), lax.axis_index("c")


def _peer(mask):
    x, y, c = _me()
    return (1 - x if mask & 4 else x, 1 - y if mask & 2 else y, 1 - c if mask & 1 else c)


def _index(dev):
    return 4 * dev[0] + 2 * dev[1] + dev[2]


class _HostedGather:
    def __init__(self, array):
        self.arrays = [array]
        self.out_shape = jax.ShapeDtypeStruct((N_DEV,) + array.shape, array.dtype)

    def scratch(self):
        return [pltpu.SemaphoreType.DMA((7,)), pltpu.SemaphoreType.DMA((7,)), pltpu.SemaphoreType.DMA(())]

    def _copies(self, ins, out, send_sems, recv_sems, local_sem):
        x, y, c = _me()
        me, sibling = (x, y, c), (x, y, 1 - c)
        chips = [(1 - x, y), (x, 1 - y), (1 - x, 1 - y)]

        def copy(k, block, to, src=None):
            rows = out.at[_index(block)]
            return pltpu.make_async_remote_copy(
                src_ref=rows if src is None else src, dst_ref=rows, send_sem=send_sems.at[k],
                recv_sem=recv_sems.at[k], device_id=to, device_id_type=MESH)

        return dict(
            mine=lambda: pltpu.make_async_copy(ins[0], out.at[_index(me)], local_sem),
            first=lambda: [copy(0, me, sibling, src=ins[0])] + [copy(1 + j, me, (*chip, c), src=ins[0])
                                                                for j, chip in enumerate(chips)],
            over_ici=lambda: [copy(1 + j, (*chip, c), me) for j, chip in enumerate(chips)],
            passed=lambda: [copy(4 + j, (*chip, c), sibling) for j, chip in enumerate(chips)],
            from_sibling=lambda: [copy(0, sibling, me)] + [copy(4 + j, (*chip, 1 - c), me)
                                                           for j, chip in enumerate(chips)])

    def start(self, *refs):
        cp = self._copies(*refs)
        cp["mine"]().start()
        for c in cp["first"]():
            c.start()

    def middle(self, *refs):
        cp = self._copies(*refs)
        for arrived, onward in zip(cp["over_ici"](), cp["passed"]()):
            arrived.wait_recv()
            onward.start()

    def finish(self, *refs):
        cp = self._copies(*refs)
        for c in cp["from_sibling"]():
            c.wait_recv()
        for c in cp["first"]() + cp["passed"]():
            c.wait_send()
        cp["mine"]().wait()


class _HostedExchange:
    def __init__(self, arrays, offsets, rows):
        self.arrays, self.offsets = list(arrays), list(offsets)
        self.out_shape = jax.ShapeDtypeStruct((N_DEV, rows, arrays[0].shape[2]), arrays[0].dtype)

    def scratch(self):
        n = len(self.arrays)
        return [pltpu.SemaphoreType.DMA((n, 7)), pltpu.SemaphoreType.DMA((n, 7)), pltpu.SemaphoreType.DMA((n,))]

    def _copies(self, ins, out, send_sems, recv_sems, local_sems):
        me = _index(_me())

        def dst(k, src_dev):
            return out.at[src_dev, pl.ds(self.offsets[k], self.arrays[k].shape[1]), :]

        pairs = [(k, mask) for k in range(len(self.arrays)) for mask in range(1, N_DEV)]

        def local():
            return [pltpu.make_async_copy(ins[k].at[me], dst(k, me), local_sems.at[k])
                    for k in range(len(self.arrays))]

        def sends():
            return [pltpu.make_async_remote_copy(
                src_ref=ins[k].at[_index(_peer(mask))], dst_ref=dst(k, me), send_sem=send_sems.at[k, mask - 1],
                recv_sem=recv_sems.at[k, mask - 1], device_id=_peer(mask), device_id_type=MESH)
                for k, mask in pairs]

        def arrivals():
            return [pltpu.make_async_remote_copy(
                src_ref=ins[k].at[me], dst_ref=dst(k, _index(_peer(mask))), send_sem=send_sems.at[k, mask - 1],
                recv_sem=recv_sems.at[k, mask - 1], device_id=_me(), device_id_type=MESH)
                for k, mask in pairs]

        return local, sends, arrivals

    def start(self, *refs):
        local, sends, _ = self._copies(*refs)
        for c in local() + sends():
            c.start()

    def middle(self, *refs):
        pass

    def finish(self, *refs):
        local, sends, arrivals = self._copies(*refs)
        for c in arrivals():
            c.wait_recv()
        for c in sends():
            c.wait_send()
        for c in local():
            c.wait()


def _gridded_call(body, name, nsteps, in_specs, out_specs, out_shape, scratch, args, hosted=None):
    if hosted is None:
        return pl.pallas_call(body, name=name, grid=(nsteps,), in_specs=in_specs, out_specs=out_specs,
                              out_shape=out_shape, scratch_shapes=scratch, compiler_params=_params())(*args)
    n_in, n_out, n_scr, h_in = len(in_specs), len(out_specs), len(scratch), len(hosted.arrays)

    def with_hosted(*refs):
        a = n_in + h_in
        b = a + n_out
        e = b + 1 + n_scr
        comm = (refs[n_in:a], refs[b], refs[e], refs[e + 1], refs[e + 2])
        i = pl.program_id(0)

        @pl.when(i == 0)
        def _():
            hosted.start(*comm)

        body(*refs[:n_in], *refs[a:b], *refs[b + 1:e])

        @pl.when(i == nsteps // 2)
        def _():
            hosted.middle(*comm)

        @pl.when(i == nsteps - 1)
        def _():
            hosted.finish(*comm)

    return pl.pallas_call(
        with_hosted, name=name, grid=(nsteps,), in_specs=list(in_specs) + [ANY] * h_in,
        out_specs=list(out_specs) + [ANY], out_shape=list(out_shape) + [hosted.out_shape],
        scratch_shapes=list(scratch) + hosted.scratch(), compiler_params=_params(),
    )(*args, *hosted.arrays)


CONV_RB = 64
CONV_LC = 128
CONV_WIN = CONV_RB + HALO + 8
CONV_MC = 256


def _shifted(win, r):
    return win if r == 0 else pltpu.roll(win, win.shape[0] - r, 0)


def _conv_fwd(x, wb, lay, w_dw, b_pw1, b_dw, cg, cb, b_pw2, lg, lb, tm, hosted=None):
    t, d = x.shape
    nsteps = t // tm

    def body(x_ref, xh_ref, wb_ref, wdw_ref, b1_ref, bdw_ref, cg_ref, cb_ref, b2_ref, lg_ref, lb_ref,
             xb_ref, ag_ref, xhc_ref, rsc_ref, xh1_ref, rs1_ref, w1_s, w2_s, ubuf, cv_s):
        i = pl.program_id(0)

        @pl.when(i == 0)
        def _():
            _load_weight(wb_ref, lay, "pw1t", w1_s)
            _load_weight(wb_ref, lay, "pw2", w2_s)
            ubuf[pl.ds(HALO + tm, 8), :] = jnp.zeros((8, d), F32)

        xv = x_ref[...]
        xb = xv.astype(BF16)
        xb_ref[...] = xb
        xcat = jnp.concatenate([xh_ref[...].astype(BF16), xb], axis=0)
        for mc in range(d // CONV_MC):
            c0 = mc * CONV_MC
            acols, gcols = slice(c0, c0 + CONV_MC), slice(d + c0, d + c0 + CONV_MC)
            ha = _dot_nt(xcat, _wrows(w1_s, c0, CONV_MC)) + b1_ref[:, acols]
            hg = _dot_nt(xcat, _wrows(w1_s, d + c0, CONV_MC)) + b1_ref[:, gcols]
            ag_ref[:, acols] = ha[HALO:].astype(BF16)
            ag_ref[:, gcols] = hg[HALO:].astype(BF16)
            u = ha * _sigmoid(hg)
            ubuf[0:HALO, acols] = jnp.where(i > 0, u[:HALO], 0.0)
            ubuf[HALO:HALO + tm, acols] = u[HALO:]
            for rb in range(tm // CONV_RB):
                t0 = rb * CONV_RB
                for lc in range(CONV_MC // CONV_LC):
                    lanes = slice(c0 + lc * CONV_LC, c0 + (lc + 1) * CONV_LC)
                    win = ubuf[t0:t0 + CONV_WIN, lanes]
                    acc = jnp.zeros((CONV_RB, CONV_LC), F32)
                    for r in range(8):
                        wr = _shifted(win, r)
                        for k in range(CONV_WIDTH):
                            s = HALO - (CONV_WIDTH - 1) + k
                            if s % 8 == r:
                                q = 8 * (s // 8)
                                acc = acc + wr[q:q + CONV_RB] * wdw_ref[k:k + 1, lanes]
                    cv_s[t0:t0 + CONV_RB, lanes] = acc
        cv = cv_s[...] + bdw_ref[...]
        xhc, rsc = _ln_fwd(cv)
        xhc_ref[...] = xhc
        rsc_ref[...] = rsc
        n = xhc * cg_ref[...] + cb_ref[...]
        s_act = n * _sigmoid(n)
        m = _dot(s_act.astype(BF16), _wfull(w2_s)) + b2_ref[...]
        xh1, rs1 = _ln_fwd(ALPHA * xv + m)
        xh1_ref[...] = xh1
        rs1_ref[...] = rs1

    hb = tm // HALO
    return _gridded_call(
        body, "conv_fwd", nsteps,
        [_tile(tm, d), pl.BlockSpec((HALO, d), lambda i: (jnp.maximum(i * hb - 1, 0), 0)), ANY,
         _fixed(HALO, d), _row(2 * d), _row(d), _row(d), _row(d), _row(d), _row(d), _row(d)],
        [_tile(tm, d), _tile(tm, 2 * d), _tile(tm, d), _tile(tm, 1), _tile(tm, d), _tile(tm, 1)],
        [jax.ShapeDtypeStruct((t, d), BF16), jax.ShapeDtypeStruct((t, 2 * d), BF16),
         jax.ShapeDtypeStruct((t, d), F32), jax.ShapeDtypeStruct((t, 1), F32),
         jax.ShapeDtypeStruct((t, d), F32), jax.ShapeDtypeStruct((t, 1), F32)],
        [_wscratch(lay, "pw1t"), _wscratch(lay, "pw2"),
         pltpu.VMEM((HALO + tm + 8, d), F32), pltpu.VMEM((tm, d), F32)],
        (x, x, wb, w_dw, b_pw1, b_dw, cg, cb, b_pw2, lg, lb), hosted)


def _conv_bwd1(dz1, xhc, rsc, wb, lay, cg, cb, tm):
    t, d = dz1.shape

    def body(dz_ref, xhc_ref, rsc_ref, wb_ref, cg_ref, cb_ref, dzb_ref, s_ref, dcv_ref, st_ref, w2_s):
        i = pl.program_id(0)

        @pl.when(i == 0)
        def _():
            _load_weight(wb_ref, lay, "pw2", w2_s)
            st_ref[...] = jnp.zeros(st_ref.shape, F32)

        dz = dz_ref[...]
        dzb = dz.astype(BF16)
        dzb_ref[...] = dzb
        xhc_v = xhc_ref[...]
        n = xhc_v * cg_ref[...] + cb_ref[...]
        sg = _sigmoid(n)
        s_ref[...] = (n * sg).astype(BF16)
        ds = _dot_nt(dzb, _wfull(w2_s))
        dn = ds * (sg * (1.0 + n * (1.0 - sg)))
        dcv, dg, db = _ln_bwd(dn, xhc_v, rsc_ref[...], cg_ref[...])
        dcv_ref[...] = dcv
        st_ref[0:1, :] += dg
        st_ref[1:2, :] += db
        st_ref[2:3, :] += jnp.sum(dcv, axis=0, keepdims=True)
        st_ref[3:4, :] += jnp.sum(dz, axis=0, keepdims=True)

    return pl.pallas_call(
        body, name="conv_bwd1", grid=(t // tm,),
        in_specs=[_tile(tm, d), _tile(tm, d), _tile(tm, 1), ANY, _row(d), _row(d)],
        out_specs=[_tile(tm, d), _tile(tm, d), _tile(tm, d), _fixed(8, d)],
        out_shape=[jax.ShapeDtypeStruct((t, d), BF16), jax.ShapeDtypeStruct((t, d), BF16),
                   jax.ShapeDtypeStruct((t, d), F32), jax.ShapeDtypeStruct((8, d), F32)],
        scratch_shapes=[_wscratch(lay, "pw2")],
        compiler_params=_params(),
    )(dz1, xhc, rsc, wb, cg, cb)


def _conv_bwd2(dz1, dcv, ag, wb, lay, w_dw, tm, hosted=None):
    t, d = dz1.shape
    nsteps = t // tm

    def body(dz_ref, dcv_ref, dcvn_ref, ag_ref, wb_ref, wdw_ref,
             gx_ref, dh_ref, dw_ref, db1_ref, w1_s, ubuf, dbuf, du_s, dwacc):
        i = pl.program_id(0)

        @pl.when(i == 0)
        def _():
            _load_weight(wb_ref, lay, "pw1t", w1_s)
            dbuf[pl.ds(HALO + tm, 8), :] = jnp.zeros((8, d), F32)
            dwacc[...] = jnp.zeros(dwacc.shape, F32)
            db1_ref[...] = jnp.zeros(db1_ref.shape, F32)

        dbuf[0:tm, :] = dcv_ref[...]
        dbuf[tm:tm + HALO, :] = jnp.where(i < nsteps - 1, dcvn_ref[...], 0.0)
        gx = ALPHA * dz_ref[...]
        for mc in range(d // CONV_MC):
            c0 = mc * CONV_MC
            acols, gcols = slice(c0, c0 + CONV_MC), slice(d + c0, d + c0 + CONV_MC)
            a = ag_ref[:, acols].astype(F32)
            sg = _sigmoid(ag_ref[:, gcols].astype(F32))
            ubuf[:, acols] = a * sg
            for rb in range(tm // CONV_RB):
                t0 = rb * CONV_RB
                for lc in range(CONV_MC // CONV_LC):
                    lanes = slice(c0 + lc * CONV_LC, c0 + (lc + 1) * CONV_LC)
                    dwin = dbuf[t0:t0 + CONV_WIN, lanes]
                    ucur = ubuf[t0:t0 + CONV_RB, lanes]
                    acc = jnp.zeros((CONV_RB, CONV_LC), F32)
                    for r in range(8):
                        dr = _shifted(dwin, r)
                        for k in range(CONV_WIDTH):
                            sd = CONV_WIDTH - 1 - k
                            if sd % 8 == r:
                                q = 8 * (sd // 8)
                                dk = dr[q:q + CONV_RB]
                                acc = acc + dk * wdw_ref[k:k + 1, lanes]
                                prod = ucur * dk
                                part = prod[0:8]
                                for j in range(1, CONV_RB // 8):
                                    part = part + prod[8 * j:8 * j + 8]
                                dwacc[k, :, lanes] += part
                    du_s[t0:t0 + CONV_RB, lanes] = acc
            du = du_s[:, acols]
            da = du * sg
            dg = du * a * sg * (1.0 - sg)
            dab, dgb = da.astype(BF16), dg.astype(BF16)
            dh_ref[:, acols] = dab
            dh_ref[:, gcols] = dgb
            db1_ref[:, acols] += jnp.sum(da, axis=0, keepdims=True)
            db1_ref[:, gcols] += jnp.sum(dg, axis=0, keepdims=True)
            gx = gx + _dot(dab, _wrows(w1_s, c0, CONV_MC)) + _dot(dgb, _wrows(w1_s, d + c0, CONV_MC))
        gx_ref[...] = gx

        @pl.when(i == nsteps - 1)
        def _():
            dw_ref[...] = jnp.sum(dwacc[...], axis=1)

    hb = tm // HALO
    last = t // HALO - 1
    return _gridded_call(
        body, "conv_bwd2", nsteps,
        [_tile(tm, d), _tile(tm, d),
         pl.BlockSpec((HALO, d), lambda i: (jnp.minimum((i + 1) * hb, last), 0)),
         _tile(tm, 2 * d), ANY, _fixed(HALO, d)],
        [_tile(tm, d), _tile(tm, 2 * d), _fixed(HALO, d), _row(2 * d)],
        [jax.ShapeDtypeStruct((t, d), F32), jax.ShapeDtypeStruct((t, 2 * d), BF16),
         jax.ShapeDtypeStruct((HALO, d), F32), jax.ShapeDtypeStruct((1, 2 * d), F32)],
        [_wscratch(lay, "pw1t"), pltpu.VMEM((tm, d), F32),
         pltpu.VMEM((HALO + tm + 8, d), F32), pltpu.VMEM((tm, d), F32),
         pltpu.VMEM((HALO, 8, d), F32)],
        (dz1, dcv, dcv, ag, wb, w_dw), hosted)


FFN_FC = 256
FFN_AHEAD = 1


def _ffn_fwd(xh_in, g_in, b_in, wb, lay, layer, tm, *, kv=None, loss=None):
    t, d = xh_in.shape
    f = lay.f
    names = (f"gt{layer}", f"ut{layer}", f"dn{layer}")

    def body(*refs):
        xh_ref, gi_ref, bi_ref, wb_ref = refs[:4]
        pos = 4
        if kv is not None:
            go_ref, bo_ref, wkv_ref, bkv_ref = refs[pos:pos + 4]
            pos += 4
        if loss is not None:
            go_ref, bo_ref, tgt_ref = refs[pos:pos + 3]
            pos += 3
        xb_ref, hg_ref, hu_ref = refs[pos:pos + 3]
        pos += 3
        if kv is not None:
            xho_ref, rso_ref, xob_ref, kv_ref = refs[pos:pos + 4]
            pos += 4
        if loss is not None:
            dz_ref, st_ref, loss_ref = refs[pos:pos + 3]
            pos += 3
        gt_s, ut_s, dn_s = refs[pos:pos + 3]
        i = pl.program_id(0)

        @pl.when(i == 0)
        def _():
            for name, dst in zip(names, (gt_s, ut_s, dn_s)):
                _load_weight(wb_ref, lay, name, dst)
            if loss is not None:
                st_ref[...] = jnp.zeros(st_ref.shape, F32)
                loss_ref[...] = jnp.zeros(loss_ref.shape, F32)

        xin = xh_ref[...] * gi_ref[...] + bi_ref[...]
        xb = xin.astype(BF16)
        xb_ref[...] = xb
        def up(c):
            return (_dot_nt(xb, _wrows(gt_s, c * FFN_FC, FFN_FC)), _dot_nt(xb, _wrows(ut_s, c * FFN_FC, FFN_FC)))

        fo = jnp.zeros((tm, d), F32)
        nc = f // FFN_FC
        ahead = [up(c) for c in range(min(FFN_AHEAD, nc))]
        for c in range(nc):
            rows = slice(c * FFN_FC, (c + 1) * FFN_FC)
            hg, hu = ahead.pop(0)
            if c + FFN_AHEAD < nc:
                ahead.append(up(c + FFN_AHEAD))
            hg_ref[:, rows] = hg.astype(BF16)
            hu_ref[:, rows] = hu.astype(BF16)
            act = hg * _sigmoid(hg) * hu
            fo = fo + _dot(act.astype(BF16), _wrows(dn_s, c * FFN_FC, FFN_FC))
        xho, rso = _ln_fwd(ALPHA * xin + fo)
        if kv is not None:
            xho_ref[...] = xho
            rso_ref[...] = rso
            xob = (xho * go_ref[...] + bo_ref[...]).astype(BF16)
            xob_ref[...] = xob
            kv_ref[...] = (_dot(xob, wkv_ref[...]) + bkv_ref[...]).astype(BF16)
        if loss is not None:
            diff = xho * go_ref[...] + bo_ref[...] - tgt_ref[...]
            loss_ref[...] += (0.5 / d) * jnp.sum(diff * diff)
            dz, dg, db = _ln_bwd(diff * (1.0 / d), xho, rso, go_ref[...])
            dz_ref[...] = dz
            st_ref[0:1, :] += dg
            st_ref[1:2, :] += db

    in_specs = [_tile(tm, d), _row(d), _row(d), ANY]
    args = [xh_in, g_in, b_in, wb]
    out_specs = [_tile(tm, d), _tile(tm, f), _tile(tm, f)]
    out_shape = [jax.ShapeDtypeStruct((t, d), BF16), jax.ShapeDtypeStruct((t, f), BF16),
                 jax.ShapeDtypeStruct((t, f), BF16)]
    if kv is not None:
        in_specs += [_row(d), _row(d), _fixed(d, 2 * BLOCK), _row(2 * BLOCK)]
        args += list(kv)
        out_specs += [_tile(tm, d), _tile(tm, 1), _tile(tm, d), _tile(tm, 2 * BLOCK)]
        out_shape += [jax.ShapeDtypeStruct((t, d), F32), jax.ShapeDtypeStruct((t, 1), F32),
                      jax.ShapeDtypeStruct((t, d), BF16), jax.ShapeDtypeStruct((t, 2 * BLOCK), BF16)]
    if loss is not None:
        in_specs += [_row(d), _row(d), _tile(tm, d)]
        args += list(loss)
        out_specs += [_tile(tm, d), _fixed(8, d), _fixed(8, 128)]
        out_shape += [jax.ShapeDtypeStruct((t, d), F32), jax.ShapeDtypeStruct((8, d), F32),
                      jax.ShapeDtypeStruct((8, 128), F32)]
    return pl.pallas_call(
        body, name=f"ffn_fwd{layer}", grid=(t // tm,), in_specs=in_specs, out_specs=out_specs,
        out_shape=out_shape, scratch_shapes=[_wscratch(lay, n) for n in names],
        compiler_params=_params(),
    )(*args)


def _ffn_bwd(dz, hg, hu, xh_in, rs_in, g_in, wb, lay, layer, tm, hosted=None, qkv=None):
    t, d = dz.shape
    f = lay.f
    nsteps = t // tm
    nbt = tm // BLOCK
    names = (f"gt{layer}", f"ut{layer}", f"dn{layer}")

    def body(*refs):
        dz_ref, hg_ref, hu_ref, xh_ref, rs_ref, gi_ref, wb_ref = refs[:7]
        pos = 7
        if qkv is not None:
            dq_ref, dkc_ref, dkp_ref, dkn_ref, xho_ref, rso_ref, go_ref, wkv_ref = refs[pos:pos + 8]
            pos += 8
        dzb_ref, act_ref, dhg_ref, dhu_ref, dzp_ref, st_ref = refs[pos:pos + 6]
        pos += 6
        if qkv is not None:
            dkv_ref, sto_ref, dbkv_ref = refs[pos:pos + 3]
            pos += 3
        gt_s, ut_s, dn_s = refs[pos:pos + 3]
        i = pl.program_id(0)

        @pl.when(i == 0)
        def _():
            for name, dst in zip(names, (gt_s, ut_s, dn_s)):
                _load_weight(wb_ref, lay, name, dst)
            st_ref[...] = jnp.zeros(st_ref.shape, F32)
            if qkv is not None:
                _load_weight(wb_ref, lay, "wq", refs[pos + 3])
                sto_ref[...] = jnp.zeros(sto_ref.shape, F32)
                dbkv_ref[...] = jnp.zeros(dbkv_ref.shape, F32)

        if qkv is None:
            dzv = dz_ref[...]
        else:
            nxt = jnp.where(i < nsteps - 1, dkn_ref[...], 0.0)
            shifted = jnp.concatenate([dkp_ref[pl.ds(BLOCK, tm - BLOCK), :], nxt], axis=0) if nbt > 1 else nxt
            dkv = dkc_ref[...] + shifted
            dkvb = dkv.astype(BF16)
            dkv_ref[...] = dkvb
            dbkv_ref[...] += jnp.sum(dkv, axis=0, keepdims=True)
            dxo = (ALPHA * dz_ref[...] + _dot_nt(dq_ref[...], refs[pos + 3][...])
                   + _dot_nt(dkvb, wkv_ref[...]))
            dzv, dgo, dbo = _ln_bwd(dxo, xho_ref[...], rso_ref[...], go_ref[...])
            sto_ref[0:1, :] += dgo
            sto_ref[1:2, :] += dbo
        dzb = dzv.astype(BF16)
        dzb_ref[...] = dzb
        dx = ALPHA * dzv
        def back(c):
            return _dot_nt(dzb, _wrows(dn_s, c * FFN_FC, FFN_FC))

        nc = f // FFN_FC
        ahead = [back(c) for c in range(min(FFN_AHEAD, nc))]
        for c in range(nc):
            rows = slice(c * FFN_FC, (c + 1) * FFN_FC)
            dact = ahead.pop(0)
            if c + FFN_AHEAD < nc:
                ahead.append(back(c + FFN_AHEAD))
            hg_v = hg_ref[:, rows].astype(F32)
            hu_v = hu_ref[:, rows].astype(F32)
            sg = _sigmoid(hg_v)
            silu = hg_v * sg
            act_ref[:, rows] = (silu * hu_v).astype(BF16)
            dhu = (dact * silu).astype(BF16)
            dhg = (dact * hu_v * (sg * (1.0 + hg_v * (1.0 - sg)))).astype(BF16)
            dhu_ref[:, rows] = dhu
            dhg_ref[:, rows] = dhg
            dx = (dx + _dot(dhg, _wrows(gt_s, c * FFN_FC, FFN_FC))
                  + _dot(dhu, _wrows(ut_s, c * FFN_FC, FFN_FC)))
        dzp, dg, db = _ln_bwd(dx, xh_ref[...], rs_ref[...], gi_ref[...])
        dzp_ref[...] = dzp
        st_ref[0:1, :] += dg
        st_ref[1:2, :] += db

    in_specs = [_tile(tm, d), _tile(tm, f), _tile(tm, f), _tile(tm, d), _tile(tm, 1), _row(d), ANY]
    args = [dz, hg, hu, xh_in, rs_in, g_in, wb]
    out_specs = [_tile(tm, d), _tile(tm, f), _tile(tm, f), _tile(tm, f), _tile(tm, d), _fixed(8, d)]
    out_shape = [jax.ShapeDtypeStruct((t, d), BF16), jax.ShapeDtypeStruct((t, f), BF16),
                 jax.ShapeDtypeStruct((t, f), BF16), jax.ShapeDtypeStruct((t, f), BF16),
                 jax.ShapeDtypeStruct((t, d), F32), jax.ShapeDtypeStruct((8, d), F32)]
    scratch = [_wscratch(lay, n) for n in names]
    if qkv is not None:
        dq, dkc, dkp, xh_out, rs_out, g_out, wkv = qkv
        last = t // BLOCK - 1
        in_specs += [_tile(tm, d), _tile(tm, 2 * BLOCK), _tile(tm, 2 * BLOCK),
                     pl.BlockSpec((BLOCK, 2 * BLOCK), lambda i: (jnp.minimum((i + 1) * nbt, last), 0)),
                     _tile(tm, d), _tile(tm, 1), _row(d), _fixed(d, 2 * BLOCK)]
        args += [dq, dkc, dkp, dkp, xh_out, rs_out, g_out, wkv]
        out_specs += [_tile(tm, 2 * BLOCK), _fixed(8, d), _row(2 * BLOCK)]
        out_shape += [jax.ShapeDtypeStruct((t, 2 * BLOCK), BF16), jax.ShapeDtypeStruct((8, d), F32),
                      jax.ShapeDtypeStruct((1, 2 * BLOCK), F32)]
        scratch.append(_wscratch(lay, "wq"))
    return _gridded_call(body, f"ffn_bwd{layer}", nsteps, in_specs, out_specs, out_shape, scratch, args, hosted)


def _alibi_slope(h, nq):
    return 2.0 ** (-ALIBI_MAX * (h + 1) / nq)


def _fill_alibi_bias(bias_s, nq):
    qi = lax.broadcasted_iota(jnp.int32, (BLOCK, 2 * BLOCK), 0)
    kj = lax.broadcasted_iota(jnp.int32, (BLOCK, 2 * BLOCK), 1)
    delta = qi + BLOCK - kj
    valid = (delta >= 0) & (delta < BLOCK)
    dist = jnp.where(valid, delta.astype(F32), MASKED_DIST)
    dist_first = jnp.where(kj >= BLOCK, dist, MASKED_DIST)
    for h in range(nq):
        bias_s[0, h] = _alibi_slope(h, nq) * dist
        bias_s[1, h] = _alibi_slope(h, nq) * dist_first


def _padded_kv(kvb, kvh):
    lane = lax.broadcasted_iota(jnp.int32, (2 * BLOCK, BLOCK), 1)
    mine = (lane < HEAD_DIM) if kvh == 0 else (lane >= HEAD_DIM)
    out = []
    for sec in (kvb[:, :BLOCK], kvb[:, BLOCK:]):
        m = jnp.where(mine, sec.astype(F32), 0.0)
        sw = pltpu.roll(m, HEAD_DIM, 1)
        pair = (m, sw) if kvh == 0 else (sw, m)
        out.append(tuple(p.astype(BF16) for p in pair))
    return out


def _attn_fwd(xh_in, g_in, b_in, x_in_b, kvs, wb, lay, bq, sinks, bo, tm):
    t, d = xh_in.shape
    nq = d // HEAD_DIM
    pairs_per_kv = (d // BLOCK) // N_KV_HEADS
    nbt = tm // BLOCK
    scale = HEAD_DIM ** -0.5

    def body(xh_ref, gi_ref, bi_ref, xb_ref, kv_ref, kvp_ref, wb_ref, bq_ref, sk_ref, bo_ref,
             q_ref, o_ref, lse_ref, xho_ref, rso_ref, wq_s, wo_s, kvall, q_s, o_s, bias_s):
        i = pl.program_id(0)

        @pl.when(i == 0)
        def _():
            _load_weight(wb_ref, lay, "wq", wq_s)
            _load_weight(wb_ref, lay, "wo", wo_s)
            _fill_alibi_bias(bias_s, nq)

        qv = ((_dot(xb_ref[...], _wfull(wq_s)) + bq_ref[...]) * scale).astype(BF16)
        q_s[...] = qv
        q_ref[...] = qv
        kvall[pl.ds(0, BLOCK), :] = kvp_ref[...]
        kvall[pl.ds(BLOCK, tm), :] = kv_ref[...]
        lane = lax.broadcasted_iota(jnp.int32, (BLOCK, BLOCK), 1)
        ones = jnp.ones((2 * BLOCK, BLOCK), BF16)

        def score_phase(j):
            rows = slice(j * BLOCK, (j + 1) * BLOCK)
            kvb = kvall[j * BLOCK:(j + 2) * BLOCK, :]
            first = (i * nbt + j == 0).astype(jnp.int32)
            pads = [_padded_kv(kvb, kvh) for kvh in range(N_KV_HEADS)]
            scores = []
            for a in range(d // BLOCK):
                kpad = pads[a // pairs_per_kv][0]
                qp = q_s[rows, a * BLOCK:(a + 1) * BLOCK]
                for e in range(2):
                    scores.append(_dot_nt(qp, kpad[e]) - bias_s[first, 2 * a + e])
            return rows, pads, scores

        def softmax_phase(state):
            rows, pads, scores = state
            probs, inv = [], []
            lse_t = jnp.zeros((BLOCK, BLOCK), F32)
            for h in range(nq):
                sink = sk_ref[:, h:h + 1]
                m = jnp.maximum(jnp.max(scores[h], axis=-1, keepdims=True), sink)
                p = jnp.exp(scores[h] - m).astype(BF16)
                l = _dot(p, ones) + jnp.exp(sink - m)
                lse_t = jnp.where(lane == h, m + jnp.log(l), lse_t)
                probs.append(p)
                inv.append(1.0 / l)
            lse_ref[rows, :] = lse_t
            return rows, pads, probs, inv

        def value_phase(state):
            rows, pads, probs, inv = state
            for a in range(d // BLOCK):
                vpad = pads[a // pairs_per_kv][1]
                opair = (_dot(probs[2 * a], vpad[0]) * inv[2 * a]
                         + _dot(probs[2 * a + 1], vpad[1]) * inv[2 * a + 1])
                o_s[rows, a * BLOCK:(a + 1) * BLOCK] = opair.astype(BF16)

        for state in [softmax_phase(s) for s in [score_phase(j) for j in range(nbt)]]:
            value_phase(state)
        ov = o_s[...]
        o_ref[...] = ov
        xin = xh_ref[...] * gi_ref[...] + bi_ref[...]
        xho, rso = _ln_fwd(ALPHA * xin + _dot(ov, _wfull(wo_s)) + bo_ref[...])
        xho_ref[...] = xho
        rso_ref[...] = rso

    return pl.pallas_call(
        body, name="attn_fwd", grid=(t // tm,),
        in_specs=[_tile(tm, d), _row(d), _row(d), _tile(tm, d), _tile(tm, 2 * BLOCK),
                  pl.BlockSpec((BLOCK, 2 * BLOCK), lambda i: (jnp.maximum(i * nbt - 1, 0), 0)),
                  ANY, _row(d), _row(nq), _row(d)],
        out_specs=[_tile(tm, d), _tile(tm, d), _tile(tm, BLOCK), _tile(tm, d), _tile(tm, 1)],
        out_shape=[jax.ShapeDtypeStruct((t, d), BF16), jax.ShapeDtypeStruct((t, d), BF16),
                   jax.ShapeDtypeStruct((t, BLOCK), F32), jax.ShapeDtypeStruct((t, d), F32),
                   jax.ShapeDtypeStruct((t, 1), F32)],
        scratch_shapes=[_wscratch(lay, "wq"), _wscratch(lay, "wo"),
                        pltpu.VMEM((BLOCK + tm, 2 * BLOCK), BF16), pltpu.VMEM((tm, d), BF16),
                        pltpu.VMEM((tm, d), BF16), pltpu.VMEM((2, nq, BLOCK, 2 * BLOCK), F32)],
        compiler_params=_params(),
    )(xh_in, g_in, b_in, x_in_b, kvs, kvs, wb, bq, sinks, bo)


def _attn_bwd(dz, q, o, lse, kvs, wb, lay, sinks, tm, hosted=None):
    t, d = dz.shape
    nq = d // HEAD_DIM
    pairs_per_kv = (d // BLOCK) // N_KV_HEADS
    nbt = tm // BLOCK
    scale = HEAD_DIM ** -0.5

    def body(dz_ref, q_ref, o_ref, lse_ref, kv_ref, kvp_ref, wb_ref, sk_ref,
             dzb_ref, dq_ref, dkc_ref, dkp_ref, st_ref, dsk_ref, wo_s, kvall, do_s, dq_s, bias_s):
        i = pl.program_id(0)

        @pl.when(i == 0)
        def _():
            _load_weight(wb_ref, lay, "wo", wo_s)
            _fill_alibi_bias(bias_s, nq)
            st_ref[...] = jnp.zeros(st_ref.shape, F32)
            dsk_ref[...] = jnp.zeros(dsk_ref.shape, F32)

        dzv = dz_ref[...]
        dzb = dzv.astype(BF16)
        dzb_ref[...] = dzb
        do_s[...] = _dot_nt(dzb, _wfull(wo_s))
        kvall[pl.ds(0, BLOCK), :] = kvp_ref[...]
        kvall[pl.ds(BLOCK, tm), :] = kv_ref[...]
        lane = lax.broadcasted_iota(jnp.int32, (BLOCK, BLOCK), 1)
        lane1 = lax.broadcasted_iota(jnp.int32, (1, BLOCK), 1)
        lane2 = lax.broadcasted_iota(jnp.int32, (2 * BLOCK, BLOCK), 1)
        halves = (lane < HEAD_DIM, lane >= HEAD_DIM)

        def score_phase(j):
            rows = slice(j * BLOCK, (j + 1) * BLOCK)
            kvb = kvall[j * BLOCK:(j + 2) * BLOCK, :]
            first = (i * nbt + j == 0).astype(jnp.int32)
            pads = [_padded_kv(kvb, kvh) for kvh in range(N_KV_HEADS)]
            scores, dps, dhs, qms, doms = [], [], [], [], []
            for a in range(d // BLOCK):
                kpad, vpad = pads[a // pairs_per_kv]
                cols = slice(a * BLOCK, (a + 1) * BLOCK)
                qp = q_ref[rows, cols]
                dop = do_s[rows, cols]
                dopb = dop.astype(BF16)
                prod = dop * o_ref[rows, cols].astype(F32)
                for e in range(2):
                    scores.append(_dot_nt(qp, kpad[e]) - bias_s[first, 2 * a + e])
                    dps.append(_dot_nt(dopb, vpad[e]))
                    dhs.append(jnp.sum(jnp.where(halves[e], prod, 0.0), axis=-1, keepdims=True))
                    qms.append(jnp.where(halves[e], qp, jnp.zeros_like(qp)))
                    doms.append(jnp.where(halves[e], dopb, jnp.zeros_like(dopb)))
            return rows, pads, scores, dps, dhs, qms, doms

        def softmax_phase(state):
            rows, pads, scores, dps, dhs, qms, doms = state
            dss, pbs = [], []
            dsk_t = jnp.zeros((1, BLOCK), F32)
            for h in range(nq):
                lse_h = lse_ref[rows, h:h + 1]
                p = jnp.exp(scores[h] - lse_h)
                dss.append((p * (dps[h] - dhs[h])).astype(BF16))
                pbs.append(p.astype(BF16))
                dsink = -jnp.sum(jnp.exp(sk_ref[:, h:h + 1] - lse_h) * dhs[h], axis=0, keepdims=True)
                dsk_t = jnp.where(lane1 == h, dsink, dsk_t)
            dsk_ref[...] += dsk_t
            return rows, pads, dss, pbs, qms, doms

        def grad_phase(state):
            rows, pads, dss, pbs, qms, doms = state
            dsecs = []
            for kvh in range(N_KV_HEADS):
                kpad = pads[kvh][0]
                dk_acc = jnp.zeros((2 * BLOCK, BLOCK), F32)
                dv_acc = jnp.zeros((2 * BLOCK, BLOCK), F32)
                for a in range(kvh * pairs_per_kv, (kvh + 1) * pairs_per_kv):
                    dqp = _dot(dss[2 * a], kpad[0]) + _dot(dss[2 * a + 1], kpad[1])
                    dq_s[rows, a * BLOCK:(a + 1) * BLOCK] = dqp * scale
                    for e in range(2):
                        h = 2 * a + e
                        dk_acc = dk_acc + _dot_tn(dss[h], qms[h])
                        dv_acc = dv_acc + _dot_tn(pbs[h], doms[h])
                dsecs.append((dk_acc + pltpu.roll(dk_acc, HEAD_DIM, 1), dv_acc + pltpu.roll(dv_acc, HEAD_DIM, 1)))
            lo = lane2 < HEAD_DIM
            dkv = jnp.concatenate([jnp.where(lo, dsecs[0][0], dsecs[1][0]),
                                   jnp.where(lo, dsecs[0][1], dsecs[1][1])], axis=1)
            dkp_ref[rows, :] = dkv[:BLOCK]
            dkc_ref[rows, :] = dkv[BLOCK:]

        for state in [softmax_phase(s) for s in [score_phase(j) for j in range(nbt)]]:
            grad_phase(state)
        dqv = dq_s[...]
        dq_ref[...] = dqv.astype(BF16)
        st_ref[0:1, :] += jnp.sum(dqv, axis=0, keepdims=True)
        st_ref[1:2, :] += jnp.sum(dzv, axis=0, keepdims=True)

    return _gridded_call(
        body, "attn_bwd", t // tm,
        [_tile(tm, d), _tile(tm, d), _tile(tm, d), _tile(tm, BLOCK), _tile(tm, 2 * BLOCK),
         pl.BlockSpec((BLOCK, 2 * BLOCK), lambda i: (jnp.maximum(i * nbt - 1, 0), 0)),
         ANY, _row(nq)],
        [_tile(tm, d), _tile(tm, d), _tile(tm, 2 * BLOCK), _tile(tm, 2 * BLOCK),
         _fixed(8, d), _row(BLOCK)],
        [jax.ShapeDtypeStruct((t, d), BF16), jax.ShapeDtypeStruct((t, d), BF16),
         jax.ShapeDtypeStruct((t, 2 * BLOCK), F32), jax.ShapeDtypeStruct((t, 2 * BLOCK), F32),
         jax.ShapeDtypeStruct((8, d), F32), jax.ShapeDtypeStruct((1, BLOCK), F32)],
        [_wscratch(lay, "wo"), pltpu.VMEM((BLOCK + tm, 2 * BLOCK), BF16),
         pltpu.VMEM((tm, d), F32), pltpu.VMEM((tm, d), F32),
         pltpu.VMEM((2, nq, BLOCK, 2 * BLOCK), F32)],
        (dz, q, o, lse, kvs, kvs, wb, sinks), hosted)


def _tn_matmul(a, b, name, bm, tk):
    t, m = a.shape
    n = b.shape[1]
    ksteps = t // tk

    def body(a_ref, b_ref, o_ref, acc):
        k = pl.program_id(1)
        part = _dot_tn(a_ref[...], b_ref[...])

        @pl.when(k == 0)
        def _():
            acc[...] = part

        @pl.when(k > 0)
        def _():
            acc[...] += part

        @pl.when(k == ksteps - 1)
        def _():
            o_ref[...] = acc[...].astype(BF16)

    return pl.pallas_call(
        body, name=name, grid=(m // bm, ksteps),
        in_specs=[pl.BlockSpec((tk, bm), lambda j, k: (k, j)), pl.BlockSpec((tk, n), lambda j, k: (k, 0))],
        out_specs=pl.BlockSpec((bm, n), lambda j, k: (j, 0)),
        out_shape=jax.ShapeDtypeStruct((m, n), BF16),
        scratch_shapes=[pltpu.VMEM((bm, n), F32)],
        compiler_params=pltpu.CompilerParams(dimension_semantics=("arbitrary", "arbitrary"),
                                             vmem_limit_bytes=VMEM_LIMIT),
    )(a, b)


def _all_gather(arrays, name):
    n = len(arrays)

    def body(*refs):
        ins, outs = refs[:n], refs[n:2 * n]
        send_sems, recv_sems, local_sems = refs[2 * n:]
        x, y, c = _me()
        me, sibling = (x, y, c), (x, y, 1 - c)
        chips = [(1 - x, y), (x, 1 - y), (1 - x, 1 - y)]

        def slot(ref, dev):
            return ref.at[4 * dev[0] + 2 * dev[1] + dev[2]]

        def copy(a, k, block, to, src=None):
            return pltpu.make_async_remote_copy(
                src_ref=slot(outs[a], block) if src is None else src, dst_ref=slot(outs[a], block),
                send_sem=send_sems.at[a, k], recv_sem=recv_sems.at[a, k], device_id=to, device_id_type=MESH)

        mine = [pltpu.make_async_copy(ins[a], slot(outs[a], me), local_sems.at[a]) for a in range(n)]
        for cp in mine:
            cp.start()
        first = []
        for a in range(n):
            first.append(copy(a, 0, me, sibling, src=ins[a]))
            first += [copy(a, 1 + j, me, (*chip, c), src=ins[a]) for j, chip in enumerate(chips)]
        for cp in first:
            cp.start()
        passed = []
        for a in range(n):
            for j, chip in enumerate(chips):
                copy(a, 1 + j, (*chip, c), me).wait_recv()
                cp = copy(a, 4 + j, (*chip, c), sibling)
                cp.start()
                passed.append(cp)
        for a in range(n):
            copy(a, 0, sibling, me).wait_recv()
            for j, chip in enumerate(chips):
                copy(a, 4 + j, (*chip, 1 - c), me).wait_recv()
        for cp in first + passed:
            cp.wait_send()
        for cp in mine:
            cp.wait()

    return pl.pallas_call(
        body, name=name, in_specs=[ANY] * n, out_specs=[ANY] * n,
        out_shape=[jax.ShapeDtypeStruct((N_DEV,) + a.shape, a.dtype) for a in arrays],
        scratch_shapes=[pltpu.SemaphoreType.DMA((n, 7)), pltpu.SemaphoreType.DMA((n, 7)),
                        pltpu.SemaphoreType.DMA((n,))],
    )(*arrays)


def _exchange(arrays, name):
    n = len(arrays)
    blocked = [a.ndim == 3 for a in arrays]

    def body(*refs):
        ins, outs = refs[:n], refs[n:2 * n]
        send_sems, recv_sems, local_sems = refs[2 * n:]
        me = _index(_me())

        def src(k, dev):
            return ins[k].at[dev] if blocked[k] else ins[k]

        local = [pltpu.make_async_copy(src(k, me), outs[k].at[me], local_sems.at[k]) for k in range(n)]
        sends, arrivals = [], []
        for k in range(n):
            for mask in range(1, N_DEV):
                peer = _peer(mask)
                sends.append(pltpu.make_async_remote_copy(
                    src_ref=src(k, _index(peer)), dst_ref=outs[k].at[me], send_sem=send_sems.at[k, mask - 1],
                    recv_sem=recv_sems.at[k, mask - 1], device_id=peer, device_id_type=MESH))
                arrivals.append(pltpu.make_async_remote_copy(
                    src_ref=src(k, me), dst_ref=outs[k].at[_index(peer)], send_sem=send_sems.at[k, mask - 1],
                    recv_sem=recv_sems.at[k, mask - 1], device_id=_me(), device_id_type=MESH))
        for cp in local + sends:
            cp.start()
        for cp in arrivals:
            cp.wait_recv()
        for cp in sends:
            cp.wait_send()
        for cp in local:
            cp.wait()

    return pl.pallas_call(
        body, name=name, in_specs=[ANY] * n, out_specs=[ANY] * n,
        out_shape=[jax.ShapeDtypeStruct((N_DEV,) + a.shape[-2:], a.dtype) for a in arrays],
        scratch_shapes=[pltpu.SemaphoreType.DMA((n, 7)), pltpu.SemaphoreType.DMA((n, 7)),
                        pltpu.SemaphoreType.DMA((n,))],
    )(*arrays)


def _adamw_sum(g8, w, m, v, name, tr):
    r, width = w.shape
    bc1 = 1.0 - ADAM_B1 ** ADAM_STEP
    bc2 = 1.0 - ADAM_B2 ** ADAM_STEP

    def body(g_ref, w_ref, m_ref, v_ref, go_ref, d_ref, mo_ref, vo_ref):
        g = g_ref[0].astype(F32)
        for s in range(1, N_DEV):
            g = g + g_ref[s].astype(F32)
        mn = ADAM_B1 * m_ref[...] + (1.0 - ADAM_B1) * g
        vn = ADAM_B2 * v_ref[...] + (1.0 - ADAM_B2) * (g * g)
        m_hat = mn / bc1
        v_hat = vn / bc2
        go_ref[...] = g
        d_ref[...] = -ADAM_LR * (m_hat / (jnp.sqrt(v_hat) + ADAM_EPS) + ADAM_WD * w_ref[...])
        mo_ref[...] = mn
        vo_ref[...] = vn

    spec = pl.BlockSpec((tr, width), lambda i: (i, 0))
    return pl.pallas_call(
        body, name=name, grid=(r // tr,),
        in_specs=[pl.BlockSpec((N_DEV, tr, width), lambda i: (0, i, 0)), spec, spec, spec],
        out_specs=[spec] * 4, out_shape=[jax.ShapeDtypeStruct((r, width), F32)] * 4,
        compiler_params=_params(),
    )(g8, w, m, v)


def _local_step(x, target, wba, shard_b, lay, sm, tm, tk):
    t, d = x.shape
    f = lay.f
    w_dw32 = jnp.concatenate([sm["w_dw"], jnp.zeros((HALO - CONV_WIDTH, d), F32)], axis=0)
    lmg, lmb, lfg, lfb = sm["ln_mix_g"], sm["ln_mix_b"], sm["ln_ffn_g"], sm["ln_ffn_b"]
    bkv = jnp.concatenate([sm["b_k"], sm["b_v"]], axis=1)
    bm_f = f // 2 if (f // 2) % 128 == 0 else f
    tm_light = 2 * tm

    def exchange(group, grads):
        names = lay.GRADS[group]
        return _HostedExchange([grads[n].reshape(N_DEV, lay.n[n], d) for n in names],
                               [lay.roff[n] for n in names], lay.rrows[group])

    xb0, ag, xhc, rsc, xh1, rs1, wbb = _conv_fwd(x, wba, lay, w_dw32, sm["b_pw1"], sm["b_dw"], sm["cg"],
                                                 sm["cb"], sm["b_pw2"], lmg[0:1], lmb[0:1], tm,
                                                 hosted=_HostedGather(shard_b))
    wkv = wbb[:, lay.goff["wkv"]:lay.goff["wkv"] + lay.n["wkv"], :].reshape(d, 2 * BLOCK)
    x1b, hg0, hu0, xh2, rs2, x2b, kvs = _ffn_fwd(xh1, lmg[0:1], lmb[0:1], wbb, lay, 0, tm,
                                                kv=(lfg[0:1], lfb[0:1], wkv, bkv))
    q, o, lse, xh3, rs3 = _attn_fwd(xh2, lfg[0:1], lfb[0:1], x2b, kvs, wbb, lay, sm["b_q"], sm["sinks"],
                                    sm["b_o"], tm)
    x3b, hg1, hu1, dz4, st4, loss = _ffn_fwd(xh3, lmg[1:2], lmb[1:2], wbb, lay, 1, tm,
                                             loss=(lfg[1:2], lfb[1:2], target))

    dz4b, act1, dhg1, dhu1, dz3, st3 = _ffn_bwd(dz4, hg1, hu1, xh3, rs3, lmg[1:2], wbb, lay, 1, tm)
    g1 = {"gt1": _tn_matmul(dhg1, x3b, "dw_gate1", bm_f, tk), "ut1": _tn_matmul(dhu1, x3b, "dw_up1", bm_f, tk),
          "dn1": _tn_matmul(act1, dz4b, "dw_down1", bm_f, tk)}
    dz3b, dq, dkc, dkp, stq, dsinks, recv1 = _attn_bwd(dz3, q, o, lse, kvs, wbb, lay, sm["sinks"], tm,
                                                       hosted=exchange("r1", g1))
    g2 = {"wq": _tn_matmul(x2b, dq, "dw_q", d, tk), "wo": _tn_matmul(o, dz3b, "dw_o", d, tk)}
    dz2b, act0, dhg0, dhu0, dz1, st1, dkv, st2, dbkv, recv2 = _ffn_bwd(
        dz3, hg0, hu0, xh1, rs1, lmg[0:1], wbb, lay, 0, tm, hosted=exchange("r2", g2),
        qkv=(dq, dkc, dkp, xh2, rs2, lfg[0:1], wkv))
    dz1b, s_act, dcv, stc = _conv_bwd1(dz1, xhc, rsc, wba, lay, sm["cg"], sm["cb"], tm_light)
    g3 = {"gt0": _tn_matmul(dhg0, x1b, "dw_gate0", bm_f, tk), "ut0": _tn_matmul(dhu0, x1b, "dw_up0", bm_f, tk),
          "dn0": _tn_matmul(act0, dz2b, "dw_down0", bm_f, tk), "pw2": _tn_matmul(s_act, dz1b, "dw_pw2", d, tk),
          "wkv": _tn_matmul(x2b, dkv, "dw_kv", d, tk)}
    grad_x, dh1, dwdw, db1, recv3 = _conv_bwd2(dz1, dcv, ag, wba, lay, w_dw32, tm, hosted=exchange("r3", g3))
    g_pw1t = _tn_matmul(dh1, xb0, "dw_pw1", d, tk)
    received = {"r1": recv1, "r2": recv2, "r3": recv3}
    small = {
        "w_dw": dwdw[:CONV_WIDTH], "b_pw1": db1, "b_dw": stc[2:3], "cg": stc[0:1], "cb": stc[1:2],
        "b_pw2": stc[3:4], "b_k": dbkv[:, :BLOCK], "b_v": dbkv[:, BLOCK:], "b_q": stq[0:1],
        "sinks": dsinks[:, :d // HEAD_DIM],
        "b_o": stq[1:2],
        "ln_mix_g": jnp.concatenate([st1[0:1], st3[0:1]], axis=0),
        "ln_mix_b": jnp.concatenate([st1[1:2], st3[1:2]], axis=0),
        "ln_ffn_g": jnp.concatenate([st2[0:1], st4[0:1]], axis=0),
        "ln_ffn_b": jnp.concatenate([st2[1:2], st4[1:2]], axis=0),
    }
    return loss[0, 0], grad_x, received, g_pw1t, small


SP_ROWS = 40
SP_BDW, SP_CG, SP_CB, SP_BPW2, SP_BPW1 = 32, 33, 34, 35, 36
RP_NAMES = ("ln_mix_g", "ln_mix_b", "ln_ffn_g", "ln_ffn_b", "b_q", "b_o", "b_k", "b_v", "sinks")


def _row_forms(d, pw1, pw2, wq, wo, gate, up, down, wk, wv):
    rf = {"pw1t": pw1[0].T, "pw2": pw2[0], "wq": wq[0], "wo": wo[0],
          "wkv": jnp.concatenate([wk, wv], axis=1).reshape(-1, d)}
    for l in range(DEPTH):
        rf.update({f"gt{l}": gate[l].T, f"ut{l}": up[l].T, f"dn{l}": down[l]})
    return rf


def _from_row_forms(d, rf):
    kvw = rf["wkv"].reshape(d // N_DEV, 2 * BLOCK)
    return dict(
        pw1=rf["pw1t"].T[None], pw2=rf["pw2"][None], wq=rf["wq"][None], wo=rf["wo"][None],
        gate=jnp.stack([rf[f"gt{l}"].T for l in range(DEPTH)]),
        up=jnp.stack([rf[f"ut{l}"].T for l in range(DEPTH)]),
        down=jnp.stack([rf[f"dn{l}"] for l in range(DEPTH)]),
        wk=kvw[:, :BLOCK], wv=kvw[:, BLOCK:])


def _pack_rows(rf, names):
    return jnp.concatenate([rf[n] for n in names], axis=0)


def _pack_small(w_dw, b_dw, cg, cb, b_pw2, b_pw1):
    cw = b_dw.shape[1]
    z = jnp.zeros((1, cw), F32)
    return jnp.concatenate([w_dw[0], z, b_dw, cg, cb, b_pw2, b_pw1.reshape(2, cw), z, z], axis=0)


def _unpack_small(p):
    cw = p.shape[1]
    return dict(w_dw=p[None, :CONV_WIDTH], b_dw=p[SP_BDW:SP_BDW + 1], cg=p[SP_CG:SP_CG + 1],
                cb=p[SP_CB:SP_CB + 1], b_pw2=p[SP_BPW2:SP_BPW2 + 1],
                b_pw1=p[SP_BPW1:SP_BPW1 + 2].reshape(1, 2 * cw))


def _small_full(g):
    d = N_DEV * g.shape[2]

    def wide(r0, n=1):
        return jnp.transpose(g[:, r0:r0 + n], (1, 0, 2)).reshape(n, d)

    return dict(w_dw=wide(0, CONV_WIDTH), b_dw=wide(SP_BDW), cg=wide(SP_CG), cb=wide(SP_CB),
                b_pw2=wide(SP_BPW2), b_pw1=g[:, SP_BPW1:SP_BPW1 + 2].reshape(1, 2 * d))


def _small_grad_blocks(sg):
    cw = sg["b_dw"].shape[1] // N_DEV

    def narrow(a):
        return jnp.transpose(a.reshape(a.shape[0], N_DEV, cw), (1, 0, 2))

    z = jnp.zeros((N_DEV, 1, cw), F32)
    return jnp.concatenate([narrow(sg["w_dw"]), z, narrow(sg["b_dw"]), narrow(sg["cg"]), narrow(sg["cb"]),
                            narrow(sg["b_pw2"]), sg["b_pw1"].reshape(N_DEV, 2, cw), z, z], axis=1)


def _pack_rep(vals, rider=0.0):
    parts = []
    for name in RP_NAMES:
        a = vals[name].reshape(-1)
        pad = -a.shape[0] % 128
        parts.append(jnp.concatenate([a, jnp.zeros((pad,), F32)]).reshape(-1, 128))
    parts.append(jnp.full((1, 128), rider, F32))
    rows = sum(p.shape[0] for p in parts)
    parts.append(jnp.zeros((-rows % 8, 128), F32))
    return jnp.concatenate(parts, axis=0)


def _rider_row(shapes):
    return sum(-(-_size(shapes[name]) // 128) for name in RP_NAMES)


def _size(shape):
    n = 1
    for s in shape:
        n *= s
    return n


def _unpack_rep(p, shapes):
    out, r = {}, 0
    for name in RP_NAMES:
        n = _size(shapes[name])
        rows = -(-n // 128)
        out[name] = p[r:r + rows].reshape(-1)[:n].reshape(shapes[name])
        r += rows
    return out


def kernel(x, conv_w_pw1, conv_b_pw1, conv_w_dw, conv_b_dw, conv_ln_g, conv_ln_b, conv_w_pw2, conv_b_pw2, kv_w_k, kv_b_k, kv_w_v, kv_b_v, attn_w_q, attn_b_q, attn_sinks, attn_w_o, attn_b_o, ffn_w_gate, ffn_w_up, ffn_w_down, ln_mix_g, ln_mix_b, ln_ffn_g, ln_ffn_b, loss_target, m_conv_w_pw1, m_conv_b_pw1, m_conv_w_dw, m_conv_b_dw, m_conv_ln_g, m_conv_ln_b, m_conv_w_pw2, m_conv_b_pw2, m_kv_w_k, m_kv_b_k, m_kv_w_v, m_kv_b_v, m_attn_w_q, m_attn_b_q, m_attn_sinks, m_attn_w_o, m_attn_b_o, m_ffn_w_gate, m_ffn_w_up, m_ffn_w_down, m_ln_mix_g, m_ln_mix_b, m_ln_ffn_g, m_ln_ffn_b, v_conv_w_pw1, v_conv_b_pw1, v_conv_w_dw, v_conv_b_dw, v_conv_ln_g, v_conv_ln_b, v_conv_w_pw2, v_conv_b_pw2, v_kv_w_k, v_kv_b_k, v_kv_w_v, v_kv_b_v, v_attn_w_q, v_attn_b_q, v_attn_sinks, v_attn_w_o, v_attn_b_o, v_ffn_w_gate, v_ffn_w_up, v_ffn_w_down, v_ln_mix_g, v_ln_mix_b, v_ln_ffn_g, v_ln_ffn_b):
    t, d = x.shape[1], x.shape[2]
    f = ffn_w_gate.shape[2] * N_DEV
    lay = _Layout(d, f)
    tm, tk = 256, min(2048, t)

    rep_shapes = dict(ln_mix_g=ln_mix_g.shape, ln_mix_b=ln_mix_b.shape, ln_ffn_g=ln_ffn_g.shape,
                      ln_ffn_b=ln_ffn_b.shape, b_q=attn_b_q.shape, b_o=attn_b_o.shape, b_k=kv_b_k.shape,
                      b_v=kv_b_v.shape, sinks=attn_sinks.shape)

    def rep_pack(lmg, lmb, lfg, lfb, bq, bo, bk, bv, sk):
        return _pack_rep(dict(ln_mix_g=lmg, ln_mix_b=lmb, ln_ffn_g=lfg, ln_ffn_b=lfb, b_q=bq, b_o=bo,
                              b_k=bk, b_v=bv, sinks=sk))

    w_rf = _row_forms(d, conv_w_pw1, conv_w_pw2, attn_w_q, attn_w_o, ffn_w_gate, ffn_w_up, ffn_w_down, kv_w_k, kv_w_v)
    m_rf = _row_forms(d, m_conv_w_pw1, m_conv_w_pw2, m_attn_w_q, m_attn_w_o, m_ffn_w_gate, m_ffn_w_up, m_ffn_w_down, m_kv_w_k, m_kv_w_v)
    v_rf = _row_forms(d, v_conv_w_pw1, v_conv_w_pw2, v_attn_w_q, v_attn_w_o, v_ffn_w_gate, v_ffn_w_up, v_ffn_w_down, v_kv_w_k, v_kv_w_v)
    w_small = _pack_small(conv_w_dw, conv_b_dw, conv_ln_g, conv_ln_b, conv_b_pw2, conv_b_pw1)
    m_small = _pack_small(m_conv_w_dw, m_conv_b_dw, m_conv_ln_g, m_conv_ln_b, m_conv_b_pw2, m_conv_b_pw1)
    v_small = _pack_small(v_conv_w_dw, v_conv_b_dw, v_conv_ln_g, v_conv_ln_b, v_conv_b_pw2, v_conv_b_pw1)
    w_rep = rep_pack(ln_mix_g, ln_mix_b, ln_ffn_g, ln_ffn_b, attn_b_q, attn_b_o, kv_b_k, kv_b_v, attn_sinks)
    m_rep = rep_pack(m_ln_mix_g, m_ln_mix_b, m_ln_ffn_g, m_ln_ffn_b, m_attn_b_q, m_attn_b_o, m_kv_b_k, m_kv_b_v, m_attn_sinks)
    v_rep = rep_pack(v_ln_mix_g, v_ln_mix_b, v_ln_ffn_g, v_ln_ffn_b, v_attn_b_q, v_attn_b_o, v_kv_b_k, v_kv_b_v, v_attn_sinks)

    wba, smg = _all_gather([_pack_rows(w_rf, lay.GATHER["a"]).astype(BF16), w_small], "gather_conv_weights")
    shard_b = _pack_rows(w_rf, lay.GATHER["b"]).astype(BF16)
    sm = _small_full(smg)
    sm.update(ln_mix_g=ln_mix_g, ln_mix_b=ln_mix_b, ln_ffn_g=ln_ffn_g, ln_ffn_b=ln_ffn_b, b_q=attn_b_q,
              b_o=attn_b_o, sinks=attn_sinks, b_k=kv_b_k.reshape(1, -1), b_v=kv_b_v.reshape(1, -1))

    loss_part, grad_x, received, g_pw1t, gsmall = _local_step(x[0], loss_target[0], wba, shard_b, lay, sm, tm, tk)

    received["r4"], g8_small, g8_rep = _exchange(
        [g_pw1t.reshape(N_DEV, lay.n["pw1t"], d), _small_grad_blocks(gsmall), _pack_rep(gsmall, loss_part)],
        "exchange_last_grads")

    big_rf = [{}, {}, {}, {}]
    for group, names in lay.GRADS.items():
        rows = lay.rrows[group]
        tr = max(r for r in range(16, 129, 16) if rows % r == 0)
        res = _adamw_sum(received[group], _pack_rows(w_rf, names), _pack_rows(m_rf, names), _pack_rows(v_rf, names),
                         f"adamw_{group}", tr)
        for out_rf, a in zip(big_rf, res):
            for n in names:
                out_rf[n] = a[lay.roff[n]:lay.roff[n] + lay.n[n]]
    big_out = [_from_row_forms(d, rf) for rf in big_rf]
    small_out = [_unpack_small(a) for a in _adamw_sum(g8_small, w_small, m_small, v_small, "adamw_small", SP_ROWS)]
    rep_res = _adamw_sum(g8_rep, w_rep, m_rep, v_rep, "adamw_rep", w_rep.shape[0])
    rep_out = [_unpack_rep(a, rep_shapes) for a in rep_res]
    loss = rep_res[0][_rider_row(rep_shapes), 0]

    outs = [loss, grad_x[None]]
    for b, s, r in zip(big_out, small_out, rep_out):
        outs += [b["pw1"], s["b_pw1"], s["w_dw"], s["b_dw"], s["cg"], s["cb"], b["pw2"], s["b_pw2"],
                 b["wk"], r["b_k"], b["wv"], r["b_v"], b["wq"], r["b_q"], r["sinks"], b["wo"], r["b_o"],
                 b["gate"], b["up"], b["down"], r["ln_mix_g"], r["ln_mix_b"], r["ln_ffn_g"], r["ln_ffn_b"]]
    return tuple(outs)
```

```python
import functools

import jax
import jax.numpy as jnp
from jax import lax
from jax.experimental import pallas as pl
from jax.experimental.pallas import tpu as pltpu

F32 = jnp.float32
BF16 = jnp.bfloat16

N_DEV = 8
HEAD_DIM = 64
N_KV_HEADS = 2
BLOCK = 128
CONV_WIDTH = 31
HALO = 32
ALIBI_MAX = 8.0
DEPTH = 2
ALPHA = (2.0 * DEPTH) ** 0.25
LN_EPS = 1e-5
MASKED_DIST = 1e32
ADAM_LR = 0.001
ADAM_B1 = 0.9
ADAM_B2 = 0.999
ADAM_EPS = 1e-08
ADAM_WD = 0.01
ADAM_STEP = 10
VMEM_LIMIT = 56 * 1024 * 1024
MESH = pl.DeviceIdType.MESH


def _dot(a, b):
    return jnp.dot(a, b, preferred_element_type=F32)


def _dot_nt(a, b):
    return lax.dot_general(a, b, (((1,), (1,)), ((), ())), preferred_element_type=F32)


def _dot_tn(a, b):
    return lax.dot_general(a, b, (((0,), (0,)), ((), ())), preferred_element_type=F32)


def _sigmoid(v):
    return 1.0 / (1.0 + jnp.exp(-v))


def _ln_fwd(z):
    mu = jnp.mean(z, axis=-1, keepdims=True)
    zc = z - mu
    var = jnp.mean(zc * zc, axis=-1, keepdims=True)
    rstd = lax.rsqrt(var + LN_EPS)
    return zc * rstd, rstd


def _ln_bwd(dout, xh, rstd, g):
    dxh = dout * g
    m1 = jnp.mean(dxh, axis=-1, keepdims=True)
    m2 = jnp.mean(dxh * xh, axis=-1, keepdims=True)
    dz = rstd * (dxh - m1 - xh * m2)
    return dz, jnp.sum(dout * xh, axis=0, keepdims=True), jnp.sum(dout, axis=0, keepdims=True)


def _params(vmem=VMEM_LIMIT):
    return pltpu.CompilerParams(dimension_semantics=("arbitrary",), vmem_limit_bytes=vmem)


def _row(d):
    return pl.BlockSpec((1, d), lambda i: (0, 0))


def _tile(tm, d):
    return pl.BlockSpec((tm, d), lambda i: (i, 0))


def _fixed(r, d):
    return pl.BlockSpec((r, d), lambda i: (0, 0))


def _tile_cur(tm, d, nsteps):
    return pl.BlockSpec((tm, d), lambda i: (jnp.minimum(i, nsteps - 1), 0))


def _tile_prev(tm, d):
    return pl.BlockSpec((tm, d), lambda i: (jnp.maximum(i - 1, 0), 0))


ANY = pl.BlockSpec(memory_space=pl.ANY)


class _Layout:
    GATHER = {"a": ("pw1t", "pw2"),
              "b": ("wq", "wo", "gt0", "ut0", "dn0", "gt1", "ut1", "dn1", "wkv")}

    def __init__(self, d, f):
        self.d, self.f = d, f
        self.n = {"pw1t": 2 * d // N_DEV, "pw2": d // N_DEV, "wq": d // N_DEV, "wo": d // N_DEV,
                  "wkv": (d // N_DEV) * 2 * BLOCK // d}
        for l in range(DEPTH):
            self.n.update({f"gt{l}": f // N_DEV, f"ut{l}": f // N_DEV, f"dn{l}": f // N_DEV})
        self.goff = {}
        for names in self.GATHER.values():
            r = 0
            for name in names:
                self.goff[name] = r
                r += self.n[name]


def _load_weight(wb_ref, lay, name, dst):
    n = lay.n[name]
    for p in range(N_DEV):
        pltpu.sync_copy(wb_ref.at[p, pl.ds(lay.goff[name], n), :], dst.at[pl.ds(p * n, n), :])


def _wscratch(lay, name):
    return pltpu.VMEM((N_DEV * lay.n[name], lay.d), BF16)


def _wfull(ref):
    return ref[...]


def _wrows(ref, r0, nrows):
    return ref[r0:r0 + nrows, :]


def _me():
    return lax.axis_index("x"), lax.axis_index("y"), lax.axis_index("c")


def _peer(mask):
    x, y, c = _me()
    return (1 - x if mask & 4 else x, 1 - y if mask & 2 else y, 1 - c if mask & 1 else c)


def _index(dev):
    return 4 * dev[0] + 2 * dev[1] + dev[2]


class _HostedGather:
    def __init__(self, array):
        self.arrays = [array]
        self.out_shapes = [jax.ShapeDtypeStruct((N_DEV,) + array.shape, array.dtype)]

    def scratch(self):
        return [pltpu.SemaphoreType.DMA((7,)), pltpu.SemaphoreType.DMA((7,)), pltpu.SemaphoreType.DMA(())]

    def _copies(self, ins, outs, send_sems, recv_sems, local_sem):
        out = outs[0]
        x, y, c = _me()
        me, sibling = (x, y, c), (x, y, 1 - c)
        chips = [(1 - x, y), (x, 1 - y), (1 - x, 1 - y)]

        def copy(k, block, to, src=None):
            rows = out.at[_index(block)]
            return pltpu.make_async_remote_copy(
                src_ref=rows if src is None else src, dst_ref=rows, send_sem=send_sems.at[k],
                recv_sem=recv_sems.at[k], device_id=to, device_id_type=MESH)

        return dict(
            mine=lambda: pltpu.make_async_copy(ins[0], out.at[_index(me)], local_sem),
            first=lambda: [copy(0, me, sibling, src=ins[0])] + [copy(1 + j, me, (*chip, c), src=ins[0])
                                                                for j, chip in enumerate(chips)],
            over_ici=lambda: [copy(1 + j, (*chip, c), me) for j, chip in enumerate(chips)],
            passed=lambda: [copy(4 + j, (*chip, c), sibling) for j, chip in enumerate(chips)],
            from_sibling=lambda: [copy(0, sibling, me)] + [copy(4 + j, (*chip, 1 - c), me)
                                                           for j, chip in enumerate(chips)])

    def start(self, *refs):
        cp = self._copies(*refs)
        cp["mine"]().start()
        for c in cp["first"]():
            c.start()

    def middle(self, *refs):
        cp = self._copies(*refs)
        for arrived, onward in zip(cp["over_ici"](), cp["passed"]()):
            arrived.wait_recv()
            onward.start()

    def finish(self, *refs):
        cp = self._copies(*refs)
        for c in cp["from_sibling"]():
            c.wait_recv()
        for c in cp["first"]() + cp["passed"]():
            c.wait_send()
        cp["mine"]().wait()


class _HostedExchange:
    def __init__(self, arrays):
        self.arrays = list(arrays)
        self.out_shapes = [jax.ShapeDtypeStruct(a.shape, a.dtype) for a in self.arrays]

    def scratch(self):
        n = len(self.arrays)
        return [pltpu.SemaphoreType.DMA((n, 7)), pltpu.SemaphoreType.DMA((n, 7)), pltpu.SemaphoreType.DMA((n,))]

    def _copies(self, ins, outs, send_sems, recv_sems, local_sems):
        me = _index(_me())

        def dst(k, src_dev):
            return outs[k].at[src_dev]

        pairs = [(k, mask) for k in range(len(self.arrays)) for mask in range(1, N_DEV)]

        def local():
            return [pltpu.make_async_copy(ins[k].at[me], dst(k, me), local_sems.at[k])
                    for k in range(len(self.arrays))]

        def sends():
            return [pltpu.make_async_remote_copy(
                src_ref=ins[k].at[_index(_peer(mask))], dst_ref=dst(k, me), send_sem=send_sems.at[k, mask - 1],
                recv_sem=recv_sems.at[k, mask - 1], device_id=_peer(mask), device_id_type=MESH)
                for k, mask in pairs]

        def arrivals():
            return [pltpu.make_async_remote_copy(
                src_ref=ins[k].at[me], dst_ref=dst(k, _index(_peer(mask))), send_sem=send_sems.at[k, mask - 1],
                recv_sem=recv_sems.at[k, mask - 1], device_id=_me(), device_id_type=MESH)
                for k, mask in pairs]

        return local, sends, arrivals

    def start(self, *refs):
        local, sends, _ = self._copies(*refs)
        for c in local() + sends():
            c.start()

    def middle(self, *refs):
        pass

    def finish(self, *refs):
        local, sends, arrivals = self._copies(*refs)
        for c in arrivals():
            c.wait_recv()
        for c in sends():
            c.wait_send()
        for c in local():
            c.wait()


def _gridded_call(body, name, nsteps, in_specs, out_specs, out_shape, scratch, args, hosted=None):
    if hosted is None:
        return pl.pallas_call(body, name=name, grid=(nsteps,), in_specs=in_specs, out_specs=out_specs,
                              out_shape=out_shape, scratch_shapes=scratch, compiler_params=_params())(*args)
    n_in, n_out, n_scr, h_in = len(in_specs), len(out_specs), len(scratch), len(hosted.arrays)
    h_out = len(hosted.out_shapes)

    def with_hosted(*refs):
        a = n_in + h_in
        b = a + n_out
        e = b + h_out + n_scr
        comm = (refs[n_in:a], refs[b:b + h_out], refs[e], refs[e + 1], refs[e + 2])
        i = pl.program_id(0)

        @pl.when(i == 0)
        def _():
            hosted.start(*comm)

        body(*refs[:n_in], *refs[a:b], *refs[b + h_out:e])

        @pl.when(i == nsteps // 2)
        def _():
            hosted.middle(*comm)

        @pl.when(i == nsteps - 1)
        def _():
            hosted.finish(*comm)

    return pl.pallas_call(
        with_hosted, name=name, grid=(nsteps,), in_specs=list(in_specs) + [ANY] * h_in,
        out_specs=list(out_specs) + [ANY] * h_out, out_shape=list(out_shape) + hosted.out_shapes,
        scratch_shapes=list(scratch) + hosted.scratch(), compiler_params=_params(),
    )(*args, *hosted.arrays)


CONV_RB = 64
CONV_LC = 128
CONV_WIN = CONV_RB + HALO + 8
CONV_MC = 256


def _shifted(win, r):
    return win if r == 0 else pltpu.roll(win, win.shape[0] - r, 0)


def _conv_fwd(x, wb, lay, w_dw, b_pw1, b_dw, cg, cb, b_pw2, lg, lb, tm, hosted=None):
    t, d = x.shape
    nsteps = t // tm

    def body(x_ref, xh_ref, wb_ref, wdw_ref, b1_ref, bdw_ref, cg_ref, cb_ref, b2_ref, lg_ref, lb_ref,
             xb_ref, ag_ref, xhc_ref, rsc_ref, xh1_ref, rs1_ref, w1_s, w2_s, ubuf, cv_s):
        i = pl.program_id(0)

        @pl.when(i == 0)
        def _():
            _load_weight(wb_ref, lay, "pw1t", w1_s)
            _load_weight(wb_ref, lay, "pw2", w2_s)
            ubuf[pl.ds(HALO + tm, 8), :] = jnp.zeros((8, d), F32)

        xv = x_ref[...]
        xb = xv.astype(BF16)
        xb_ref[...] = xb
        xcat = jnp.concatenate([xh_ref[...].astype(BF16), xb], axis=0)
        for mc in range(d // CONV_MC):
            c0 = mc * CONV_MC
            acols, gcols = slice(c0, c0 + CONV_MC), slice(d + c0, d + c0 + CONV_MC)
            ha = _dot_nt(xcat, _wrows(w1_s, c0, CONV_MC)) + b1_ref[:, acols]
            hg = _dot_nt(xcat, _wrows(w1_s, d + c0, CONV_MC)) + b1_ref[:, gcols]
            ag_ref[:, acols] = ha[HALO:].astype(BF16)
            ag_ref[:, gcols] = hg[HALO:].astype(BF16)
            u = ha * _sigmoid(hg)
            ubuf[0:HALO, acols] = jnp.where(i > 0, u[:HALO], 0.0)
            ubuf[HALO:HALO + tm, acols] = u[HALO:]
            for rb in range(tm // CONV_RB):
                t0 = rb * CONV_RB
                for lc in range(CONV_MC // CONV_LC):
                    lanes = slice(c0 + lc * CONV_LC, c0 + (lc + 1) * CONV_LC)
                    win = ubuf[t0:t0 + CONV_WIN, lanes]
                    acc = jnp.zeros((CONV_RB, CONV_LC), F32)
                    for r in range(8):
                        wr = _shifted(win, r)
                        for k in range(CONV_WIDTH):
                            s = HALO - (CONV_WIDTH - 1) + k
                            if s % 8 == r:
                                q = 8 * (s // 8)
                                acc = acc + wr[q:q + CONV_RB] * wdw_ref[k:k + 1, lanes]
                    cv_s[t0:t0 + CONV_RB, lanes] = acc
        cv = cv_s[...] + bdw_ref[...]
        xhc, rsc = _ln_fwd(cv)
        xhc_ref[...] = xhc
        rsc_ref[...] = rsc
        n = xhc * cg_ref[...] + cb_ref[...]
        s_act = n * _sigmoid(n)
        m = _dot(s_act.astype(BF16), _wfull(w2_s)) + b2_ref[...]
        xh1, rs1 = _ln_fwd(ALPHA * xv + m)
        xh1_ref[...] = xh1
        rs1_ref[...] = rs1

    hb = tm // HALO
    return _gridded_call(
        body, "conv_fwd", nsteps,
        [_tile(tm, d), pl.BlockSpec((HALO, d), lambda i: (jnp.maximum(i * hb - 1, 0), 0)), ANY,
         _fixed(HALO, d), _row(2 * d), _row(d), _row(d), _row(d), _row(d), _row(d), _row(d)],
        [_tile(tm, d), _tile(tm, 2 * d), _tile(tm, d), _tile(tm, 1), _tile(tm, d), _tile(tm, 1)],
        [jax.ShapeDtypeStruct((t, d), BF16), jax.ShapeDtypeStruct((t, 2 * d), BF16),
         jax.ShapeDtypeStruct((t, d), F32), jax.ShapeDtypeStruct((t, 1), F32),
         jax.ShapeDtypeStruct((t, d), F32), jax.ShapeDtypeStruct((t, 1), F32)],
        [_wscratch(lay, "pw1t"), _wscratch(lay, "pw2"),
         pltpu.VMEM((HALO + tm + 8, d), F32), pltpu.VMEM((tm, d), F32)],
        (x, x, wb, w_dw, b_pw1, b_dw, cg, cb, b_pw2, lg, lb), hosted)


def _conv_bwd1(dz1, xhc, rsc, wb, lay, cg, cb, tm):
    t, d = dz1.shape

    def body(dz_ref, xhc_ref, rsc_ref, wb_ref, cg_ref, cb_ref, dzb_ref, s_ref, dcv_ref, st_ref, w2_s):
        i = pl.program_id(0)

        @pl.when(i == 0)
        def _():
            _load_weight(wb_ref, lay, "pw2", w2_s)
            st_ref[...] = jnp.zeros(st_ref.shape, F32)

        dz = dz_ref[...]
        dzb = dz.astype(BF16)
        dzb_ref[...] = dzb
        xhc_v = xhc_ref[...]
        n = xhc_v * cg_ref[...] + cb_ref[...]
        sg = _sigmoid(n)
        s_ref[...] = (n * sg).astype(BF16)
        ds = _dot_nt(dzb, _wfull(w2_s))
        dn = ds * (sg * (1.0 + n * (1.0 - sg)))
        dcv, dg, db = _ln_bwd(dn, xhc_v, rsc_ref[...], cg_ref[...])
        dcv_ref[...] = dcv
        st_ref[0:1, :] += dg
        st_ref[1:2, :] += db
        st_ref[2:3, :] += jnp.sum(dcv, axis=0, keepdims=True)
        st_ref[3:4, :] += jnp.sum(dz, axis=0, keepdims=True)

    return pl.pallas_call(
        body, name="conv_bwd1", grid=(t // tm,),
        in_specs=[_tile(tm, d), _tile(tm, d), _tile(tm, 1), ANY, _row(d), _row(d)],
        out_specs=[_tile(tm, d), _tile(tm, d), _tile(tm, d), _fixed(8, d)],
        out_shape=[jax.ShapeDtypeStruct((t, d), BF16), jax.ShapeDtypeStruct((t, d), BF16),
                   jax.ShapeDtypeStruct((t, d), F32), jax.ShapeDtypeStruct((8, d), F32)],
        scratch_shapes=[_wscratch(lay, "pw2")],
        compiler_params=_params(),
    )(dz1, xhc, rsc, wb, cg, cb)


def _conv_bwd2(dz1, dcv, ag, wb, lay, w_dw, tm, hosted=None):
    t, d = dz1.shape
    nsteps = t // tm

    def body(dz_ref, dcv_ref, dcvn_ref, ag_ref, wb_ref, wdw_ref,
             gx_ref, dh_ref, dw_ref, db1_ref, w1_s, ubuf, dbuf, du_s, dwacc):
        i = pl.program_id(0)

        @pl.when(i == 0)
        def _():
            _load_weight(wb_ref, lay, "pw1t", w1_s)
            dbuf[pl.ds(HALO + tm, 8), :] = jnp.zeros((8, d), F32)
            dwacc[...] = jnp.zeros(dwacc.shape, F32)
            db1_ref[...] = jnp.zeros(db1_ref.shape, F32)

        dbuf[0:tm, :] = dcv_ref[...]
        dbuf[tm:tm + HALO, :] = jnp.where(i < nsteps - 1, dcvn_ref[...], 0.0)
        gx = ALPHA * dz_ref[...]
        for mc in range(d // CONV_MC):
            c0 = mc * CONV_MC
            acols, gcols = slice(c0, c0 + CONV_MC), slice(d + c0, d + c0 + CONV_MC)
            a = ag_ref[:, acols].astype(F32)
            sg = _sigmoid(ag_ref[:, gcols].astype(F32))
            ubuf[:, acols] = a * sg
            for rb in range(tm // CONV_RB):
                t0 = rb * CONV_RB
                for lc in range(CONV_MC // CONV_LC):
                    lanes = slice(c0 + lc * CONV_LC, c0 + (lc + 1) * CONV_LC)
                    dwin = dbuf[t0:t0 + CONV_WIN, lanes]
                    ucur = ubuf[t0:t0 + CONV_RB, lanes]
                    acc = jnp.zeros((CONV_RB, CONV_LC), F32)
                    for r in range(8):
                        dr = _shifted(dwin, r)
                        for k in range(CONV_WIDTH):
                            sd = CONV_WIDTH - 1 - k
                            if sd % 8 == r:
                                q = 8 * (sd // 8)
                                dk = dr[q:q + CONV_RB]
                                acc = acc + dk * wdw_ref[k:k + 1, lanes]
                                prod = ucur * dk
                                part = prod[0:8]
                                for j in range(1, CONV_RB // 8):
                                    part = part + prod[8 * j:8 * j + 8]
                                dwacc[k, :, lanes] += part
                    du_s[t0:t0 + CONV_RB, lanes] = acc
            du = du_s[:, acols]
            da = du * sg
            dg = du * a * sg * (1.0 - sg)
            dab, dgb = da.astype(BF16), dg.astype(BF16)
            dh_ref[:, acols] = dab
            dh_ref[:, gcols] = dgb
            db1_ref[:, acols] += jnp.sum(da, axis=0, keepdims=True)
            db1_ref[:, gcols] += jnp.sum(dg, axis=0, keepdims=True)
            gx = gx + _dot(dab, _wrows(w1_s, c0, CONV_MC)) + _dot(dgb, _wrows(w1_s, d + c0, CONV_MC))
        gx_ref[...] = gx

        @pl.when(i == nsteps - 1)
        def _():
            dw_ref[...] = jnp.sum(dwacc[...], axis=1)

    hb = tm // HALO
    last = t // HALO - 1
    return _gridded_call(
        body, "conv_bwd2", nsteps,
        [_tile(tm, d), _tile(tm, d),
         pl.BlockSpec((HALO, d), lambda i: (jnp.minimum((i + 1) * hb, last), 0)),
         _tile(tm, 2 * d), ANY, _fixed(HALO, d)],
        [_tile(tm, d), _tile(tm, 2 * d), _fixed(HALO, d), _row(2 * d)],
        [jax.ShapeDtypeStruct((t, d), F32), jax.ShapeDtypeStruct((t, 2 * d), BF16),
         jax.ShapeDtypeStruct((HALO, d), F32), jax.ShapeDtypeStruct((1, 2 * d), F32)],
        [_wscratch(lay, "pw1t"), pltpu.VMEM((tm, d), F32),
         pltpu.VMEM((HALO + tm + 8, d), F32), pltpu.VMEM((tm, d), F32),
         pltpu.VMEM((HALO, 8, d), F32)],
        (dz1, dcv, dcv, ag, wb, w_dw), hosted)


FFN_FC = 256
FFN_AHEAD = 1


def _ffn_fwd(xh_in, g_in, b_in, wb, lay, layer, tm, *, kv=None, loss=None):
    t, d = xh_in.shape
    f = lay.f
    names = (f"gt{layer}", f"ut{layer}", f"dn{layer}")

    def body(*refs):
        xh_ref, gi_ref, bi_ref, wb_ref = refs[:4]
        pos = 4
        if kv is not None:
            go_ref, bo_ref, wkv_ref, bkv_ref = refs[pos:pos + 4]
            pos += 4
        if loss is not None:
            go_ref, bo_ref, tgt_ref = refs[pos:pos + 3]
            pos += 3
        xb_ref, hg_ref, hu_ref = refs[pos:pos + 3]
        pos += 3
        if kv is not None:
            xho_ref, rso_ref, xob_ref, kv_ref = refs[pos:pos + 4]
            pos += 4
        if loss is not None:
            dz_ref, st_ref, loss_ref = refs[pos:pos + 3]
            pos += 3
        gt_s, ut_s, dn_s, xin_s, fo_s = refs[pos:pos + 5]
        i = pl.program_id(0)

        @pl.when(i == 0)
        def _():
            for name, dst in zip(names, (gt_s, ut_s, dn_s)):
                _load_weight(wb_ref, lay, name, dst)
            xin_s[...] = jnp.zeros(xin_s.shape, F32)
            fo_s[...] = jnp.zeros(fo_s.shape, F32)
            if loss is not None:
                st_ref[...] = jnp.zeros(st_ref.shape, F32)
                loss_ref[...] = jnp.zeros(loss_ref.shape, F32)

        xin_prev = xin_s[...]
        xho, rso = _ln_fwd(ALPHA * xin_prev + fo_s[...])
        if kv is not None:
            xho_ref[...] = xho
            rso_ref[...] = rso
            xob_ref[...] = (xho * go_ref[...] + bo_ref[...]).astype(BF16)
        if loss is not None:
            real = i > 0
            diff = xho * go_ref[...] + bo_ref[...] - tgt_ref[...]
            loss_ref[...] += jnp.where(real, (0.5 / d) * jnp.sum(diff * diff), 0.0)
            dz, dg, db = _ln_bwd(diff * (1.0 / d), xho, rso, go_ref[...])
            dz_ref[...] = dz
            st_ref[0:1, :] += jnp.where(real, dg, 0.0)
            st_ref[1:2, :] += jnp.where(real, db, 0.0)

        xin = xh_ref[...] * gi_ref[...] + bi_ref[...]
        xb = xin.astype(BF16)
        xb_ref[...] = xb

        def up(c):
            return (_dot_nt(xb, _wrows(gt_s, c * FFN_FC, FFN_FC)), _dot_nt(xb, _wrows(ut_s, c * FFN_FC, FFN_FC)))

        fo = jnp.zeros((tm, d), F32)
        nc = f // FFN_FC
        ahead = [up(c) for c in range(min(FFN_AHEAD, nc))]
        for c in range(nc):
            rows = slice(c * FFN_FC, (c + 1) * FFN_FC)
            hg, hu = ahead.pop(0)
            if c + FFN_AHEAD < nc:
                ahead.append(up(c + FFN_AHEAD))
            hg_ref[:, rows] = hg.astype(BF16)
            hu_ref[:, rows] = hu.astype(BF16)
            act = hg * _sigmoid(hg) * hu
            fo = fo + _dot(act.astype(BF16), _wrows(dn_s, c * FFN_FC, FFN_FC))
        xin_s[...] = xin
        fo_s[...] = fo
        if kv is not None:
            kv_ref[...] = (_dot(xob_ref[...], wkv_ref[...]) + bkv_ref[...]).astype(BF16)

    nsteps = t // tm
    in_specs = [_tile_cur(tm, d, nsteps), _row(d), _row(d), ANY]
    args = [xh_in, g_in, b_in, wb]
    out_specs = [_tile_cur(tm, d, nsteps), _tile_cur(tm, f, nsteps), _tile_cur(tm, f, nsteps)]
    out_shape = [jax.ShapeDtypeStruct((t, d), BF16), jax.ShapeDtypeStruct((t, f), BF16),
                 jax.ShapeDtypeStruct((t, f), BF16)]
    if kv is not None:
        in_specs += [_row(d), _row(d), _fixed(d, 2 * BLOCK), _row(2 * BLOCK)]
        args += list(kv)
        out_specs += [_tile_prev(tm, d), _tile_prev(tm, 1), _tile_prev(tm, d), _tile_prev(tm, 2 * BLOCK)]
        out_shape += [jax.ShapeDtypeStruct((t, d), F32), jax.ShapeDtypeStruct((t, 1), F32),
                      jax.ShapeDtypeStruct((t, d), BF16), jax.ShapeDtypeStruct((t, 2 * BLOCK), BF16)]
    if loss is not None:
        in_specs += [_row(d), _row(d), _tile_prev(tm, d)]
        args += list(loss)
        out_specs += [_tile_prev(tm, d), _fixed(8, d), _fixed(8, 128)]
        out_shape += [jax.ShapeDtypeStruct((t, d), F32), jax.ShapeDtypeStruct((8, d), F32),
                      jax.ShapeDtypeStruct((8, 128), F32)]
    return pl.pallas_call(
        body, name=f"ffn_fwd{layer}", grid=(nsteps + 1,), in_specs=in_specs, out_specs=out_specs,
        out_shape=out_shape,
        scratch_shapes=[_wscratch(lay, n) for n in names] + [pltpu.VMEM((tm, d), F32), pltpu.VMEM((tm, d), F32)],
        compiler_params=_params(),
    )(*args)


def _ffn_bwd(dz, hg, hu, xh_in, rs_in, g_in, wb, lay, layer, tm, hosted=None, qkv=None):
    t, d = dz.shape
    f = lay.f
    nsteps = t // tm
    nbt = tm // BLOCK
    names = (f"gt{layer}", f"ut{layer}", f"dn{layer}")

    def body(*refs):
        dz_ref, hg_ref, hu_ref, xh_ref, rs_ref, gi_ref, wb_ref = refs[:7]
        pos = 7
        if qkv is not None:
            dq_ref, dkc_ref, dkp_ref, dkn_ref, xho_ref, rso_ref, go_ref, wkv_ref = refs[pos:pos + 8]
            pos += 8
        dzb_ref, act_ref, dhg_ref, dhu_ref, dzp_ref, st_ref = refs[pos:pos + 6]
        pos += 6
        if qkv is not None:
            dkv_ref, sto_ref, dbkv_ref = refs[pos:pos + 3]
            pos += 3
        gt_s, ut_s, dn_s = refs[pos:pos + 3]
        i = pl.program_id(0)

        @pl.when(i == 0)
        def _():
            for name, dst in zip(names, (gt_s, ut_s, dn_s)):
                _load_weight(wb_ref, lay, name, dst)
            st_ref[...] = jnp.zeros(st_ref.shape, F32)
            if qkv is not None:
                _load_weight(wb_ref, lay, "wq", refs[pos + 3])
                sto_ref[...] = jnp.zeros(sto_ref.shape, F32)
                dbkv_ref[...] = jnp.zeros(dbkv_ref.shape, F32)

        if qkv is None:
            dzv = dz_ref[...]
        else:
            nxt = jnp.where(i < nsteps - 1, dkn_ref[...], 0.0)
            shifted = jnp.concatenate([dkp_ref[pl.ds(BLOCK, tm - BLOCK), :], nxt], axis=0) if nbt > 1 else nxt
            dkv = dkc_ref[...] + shifted
            dkvb = dkv.astype(BF16)
            dkv_ref[...] = dkvb
            dbkv_ref[...] += jnp.sum(dkv, axis=0, keepdims=True)
            dxo = (ALPHA * dz_ref[...] + _dot_nt(dq_ref[...], refs[pos + 3][...])
                   + _dot_nt(dkvb, wkv_ref[...]))
            dzv, dgo, dbo = _ln_bwd(dxo, xho_ref[...], rso_ref[...], go_ref[...])
            sto_ref[0:1, :] += dgo
            sto_ref[1:2, :] += dbo
        dzb = dzv.astype(BF16)
        dzb_ref[...] = dzb
        dx = ALPHA * dzv
        def back(c):
            return _dot_nt(dzb, _wrows(dn_s, c * FFN_FC, FFN_FC))

        nc = f // FFN_FC
        ahead = [back(c) for c in range(min(FFN_AHEAD, nc))]
        for c in range(nc):
            rows = slice(c * FFN_FC, (c + 1) * FFN_FC)
            dact = ahead.pop(0)
            if c + FFN_AHEAD < nc:
                ahead.append(back(c + FFN_AHEAD))
            hg_v = hg_ref[:, rows].astype(F32)
            hu_v = hu_ref[:, rows].astype(F32)
            sg = _sigmoid(hg_v)
            silu = hg_v * sg
            act_ref[:, rows] = (silu * hu_v).astype(BF16)
            dhu = (dact * silu).astype(BF16)
            dhg = (dact * hu_v * (sg * (1.0 + hg_v * (1.0 - sg)))).astype(BF16)
            dhu_ref[:, rows] = dhu
            dhg_ref[:, rows] = dhg
            dx = (dx + _dot(dhg, _wrows(gt_s, c * FFN_FC, FFN_FC))
                  + _dot(dhu, _wrows(ut_s, c * FFN_FC, FFN_FC)))
        dzp, dg, db = _ln_bwd(dx, xh_ref[...], rs_ref[...], gi_ref[...])
        dzp_ref[...] = dzp
        st_ref[0:1, :] += dg
        st_ref[1:2, :] += db

    in_specs = [_tile(tm, d), _tile(tm, f), _tile(tm, f), _tile(tm, d), _tile(tm, 1), _row(d), ANY]
    args = [dz, hg, hu, xh_in, rs_in, g_in, wb]
    out_specs = [_tile(tm, d), _tile(tm, f), _tile(tm, f), _tile(tm, f), _tile(tm, d), _fixed(8, d)]
    out_shape = [jax.ShapeDtypeStruct((t, d), BF16), jax.ShapeDtypeStruct((t, f), BF16),
                 jax.ShapeDtypeStruct((t, f), BF16), jax.ShapeDtypeStruct((t, f), BF16),
                 jax.ShapeDtypeStruct((t, d), F32), jax.ShapeDtypeStruct((8, d), F32)]
    scratch = [_wscratch(lay, n) for n in names]
    if qkv is not None:
        dq, dkc, dkp, xh_out, rs_out, g_out, wkv = qkv
        last = t // BLOCK - 1
        in_specs += [_tile(tm, d), _tile(tm, 2 * BLOCK), _tile(tm, 2 * BLOCK),
                     pl.BlockSpec((BLOCK, 2 * BLOCK), lambda i: (jnp.minimum((i + 1) * nbt, last), 0)),
                     _tile(tm, d), _tile(tm, 1), _row(d), _fixed(d, 2 * BLOCK)]
        args += [dq, dkc, dkp, dkp, xh_out, rs_out, g_out, wkv]
        out_specs += [_tile(tm, 2 * BLOCK), _fixed(8, d), _row(2 * BLOCK)]
        out_shape += [jax.ShapeDtypeStruct((t, 2 * BLOCK), BF16), jax.ShapeDtypeStruct((8, d), F32),
                      jax.ShapeDtypeStruct((1, 2 * BLOCK), F32)]
        scratch.append(_wscratch(lay, "wq"))
    return _gridded_call(body, f"ffn_bwd{layer}", nsteps, in_specs, out_specs, out_shape, scratch, args, hosted)


def _alibi_slope(h, nq):
    return 2.0 ** (-ALIBI_MAX * (h + 1) / nq)


def _fill_alibi_bias(bias_s, nq):
    qi = lax.broadcasted_iota(jnp.int32, (BLOCK, 2 * BLOCK), 0)
    kj = lax.broadcasted_iota(jnp.int32, (BLOCK, 2 * BLOCK), 1)
    delta = qi + BLOCK - kj
    valid = (delta >= 0) & (delta < BLOCK)
    dist = jnp.where(valid, delta.astype(F32), MASKED_DIST)
    dist_first = jnp.where(kj >= BLOCK, dist, MASKED_DIST)
    for h in range(nq):
        bias_s[0, h] = _alibi_slope(h, nq) * dist
        bias_s[1, h] = _alibi_slope(h, nq) * dist_first


def _padded_kv(kvb, kvh):
    lane = lax.broadcasted_iota(jnp.int32, (2 * BLOCK, BLOCK), 1)
    mine = (lane < HEAD_DIM) if kvh == 0 else (lane >= HEAD_DIM)
    out = []
    for sec in (kvb[:, :BLOCK], kvb[:, BLOCK:]):
        m = jnp.where(mine, sec.astype(F32), 0.0)
        sw = pltpu.roll(m, HEAD_DIM, 1)
        pair = (m, sw) if kvh == 0 else (sw, m)
        out.append(tuple(p.astype(BF16) for p in pair))
    return out


def _attn_fwd(xh_in, g_in, b_in, x_in_b, kvs, wb, lay, bq, sinks, bo, tm):
    t, d = xh_in.shape
    nq = d // HEAD_DIM
    pairs_per_kv = (d // BLOCK) // N_KV_HEADS
    nbt = tm // BLOCK
    scale = HEAD_DIM ** -0.5

    def body(xh_ref, gi_ref, bi_ref, xb_ref, kv_ref, kvp_ref, wb_ref, bq_ref, sk_ref, bo_ref,
             q_ref, o_ref, lse_ref, xho_ref, rso_ref, wq_s, wo_s, kvall, q_s, o_s, bias_s):
        i = pl.program_id(0)

        @pl.when(i == 0)
        def _():
            _load_weight(wb_ref, lay, "wq", wq_s)
            _load_weight(wb_ref, lay, "wo", wo_s)
            _fill_alibi_bias(bias_s, nq)

        qv = ((_dot(xb_ref[...], _wfull(wq_s)) + bq_ref[...]) * scale).astype(BF16)
        q_s[...] = qv
        q_ref[...] = qv
        kvall[pl.ds(0, BLOCK), :] = kvp_ref[...]
        kvall[pl.ds(BLOCK, tm), :] = kv_ref[...]
        lane = lax.broadcasted_iota(jnp.int32, (BLOCK, BLOCK), 1)
        ones = jnp.ones((2 * BLOCK, BLOCK), BF16)

        def score_phase(j):
            rows = slice(j * BLOCK, (j + 1) * BLOCK)
            kvb = kvall[j * BLOCK:(j + 2) * BLOCK, :]
            first = (i * nbt + j == 0).astype(jnp.int32)
            pads = [_padded_kv(kvb, kvh) for kvh in range(N_KV_HEADS)]
            scores = []
            for a in range(d // BLOCK):
                kpad = pads[a // pairs_per_kv][0]
                qp = q_s[rows, a * BLOCK:(a + 1) * BLOCK]
                for e in range(2):
                    scores.append(_dot_nt(qp, kpad[e]) - bias_s[first, 2 * a + e])
            return rows, pads, scores

        def softmax_phase(state):
            rows, pads, scores = state
            probs, inv = [], []
            lse_t = jnp.zeros((BLOCK, BLOCK), F32)
            for h in range(nq):
                sink = sk_ref[:, h:h + 1]
                m = jnp.maximum(jnp.max(scores[h], axis=-1, keepdims=True), sink)
                p = jnp.exp(scores[h] - m).astype(BF16)
                l = _dot(p, ones) + jnp.exp(sink - m)
                lse_t = jnp.where(lane == h, m + jnp.log(l), lse_t)
                probs.append(p)
                inv.append(1.0 / l)
            lse_ref[rows, :] = lse_t
            return rows, pads, probs, inv

        def value_phase(state):
            rows, pads, probs, inv = state
            for a in range(d // BLOCK):
                vpad = pads[a // pairs_per_kv][1]
                opair = (_dot(probs[2 * a], vpad[0]) * inv[2 * a]
                         + _dot(probs[2 * a + 1], vpad[1]) * inv[2 * a + 1])
                o_s[rows, a * BLOCK:(a + 1) * BLOCK] = opair.astype(BF16)

        for state in [softmax_phase(s) for s in [score_phase(j) for j in range(nbt)]]:
            value_phase(state)
        ov = o_s[...]
        o_ref[...] = ov
        xin = xh_ref[...] * gi_ref[...] + bi_ref[...]
        xho, rso = _ln_fwd(ALPHA * xin + _dot(ov, _wfull(wo_s)) + bo_ref[...])
        xho_ref[...] = xho
        rso_ref[...] = rso

    return pl.pallas_call(
        body, name="attn_fwd", grid=(t // tm,),
        in_specs=[_tile(tm, d), _row(d), _row(d), _tile(tm, d), _tile(tm, 2 * BLOCK),
                  pl.BlockSpec((BLOCK, 2 * BLOCK), lambda i: (jnp.maximum(i * nbt - 1, 0), 0)),
                  ANY, _row(d), _row(nq), _row(d)],
        out_specs=[_tile(tm, d), _tile(tm, d), _tile(tm, BLOCK), _tile(tm, d), _tile(tm, 1)],
        out_shape=[jax.ShapeDtypeStruct((t, d), BF16), jax.ShapeDtypeStruct((t, d), BF16),
                   jax.ShapeDtypeStruct((t, BLOCK), F32), jax.ShapeDtypeStruct((t, d), F32),
                   jax.ShapeDtypeStruct((t, 1), F32)],
        scratch_shapes=[_wscratch(lay, "wq"), _wscratch(lay, "wo"),
                        pltpu.VMEM((BLOCK + tm, 2 * BLOCK), BF16), pltpu.VMEM((tm, d), BF16),
                        pltpu.VMEM((tm, d), BF16), pltpu.VMEM((2, nq, BLOCK, 2 * BLOCK), F32)],
        compiler_params=_params(),
    )(xh_in, g_in, b_in, x_in_b, kvs, kvs, wb, bq, sinks, bo)


def _attn_bwd(dz, q, o, lse, kvs, wb, lay, sinks, tm, hosted=None):
    t, d = dz.shape
    nq = d // HEAD_DIM
    pairs_per_kv = (d // BLOCK) // N_KV_HEADS
    nbt = tm // BLOCK
    scale = HEAD_DIM ** -0.5

    def body(dz_ref, q_ref, o_ref, lse_ref, kv_ref, kvp_ref, wb_ref, sk_ref,
             dzb_ref, dq_ref, dkc_ref, dkp_ref, st_ref, dsk_ref, wo_s, kvall, do_s, dq_s, bias_s):
        i = pl.program_id(0)

        @pl.when(i == 0)
        def _():
            _load_weight(wb_ref, lay, "wo", wo_s)
            _fill_alibi_bias(bias_s, nq)
            st_ref[...] = jnp.zeros(st_ref.shape, F32)
            dsk_ref[...] = jnp.zeros(dsk_ref.shape, F32)

        dzv = dz_ref[...]
        dzb = dzv.astype(BF16)
        dzb_ref[...] = dzb
        do_s[...] = _dot_nt(dzb, _wfull(wo_s))
        kvall[pl.ds(0, BLOCK), :] = kvp_ref[...]
        kvall[pl.ds(BLOCK, tm), :] = kv_ref[...]
        lane = lax.broadcasted_iota(jnp.int32, (BLOCK, BLOCK), 1)
        lane1 = lax.broadcasted_iota(jnp.int32, (1, BLOCK), 1)
        lane2 = lax.broadcasted_iota(jnp.int32, (2 * BLOCK, BLOCK), 1)
        halves = (lane < HEAD_DIM, lane >= HEAD_DIM)

        def score_phase(j):
            rows = slice(j * BLOCK, (j + 1) * BLOCK)
            kvb = kvall[j * BLOCK:(j + 2) * BLOCK, :]
            first = (i * nbt + j == 0).astype(jnp.int32)
            pads = [_padded_kv(kvb, kvh) for kvh in range(N_KV_HEADS)]
            scores, dps, dhs, qms, doms = [], [], [], [], []
            for a in range(d // BLOCK):
                kpad, vpad = pads[a // pairs_per_kv]
                cols = slice(a * BLOCK, (a + 1) * BLOCK)
                qp = q_ref[rows, cols]
                dop = do_s[rows, cols]
                dopb = dop.astype(BF16)
                prod = dop * o_ref[rows, cols].astype(F32)
                for e in range(2):
                    scores.append(_dot_nt(qp, kpad[e]) - bias_s[first, 2 * a + e])
                    dps.append(_dot_nt(dopb, vpad[e]))
                    dhs.append(jnp.sum(jnp.where(halves[e], prod, 0.0), axis=-1, keepdims=True))
                    qms.append(jnp.where(halves[e], qp, jnp.zeros_like(qp)))
                    doms.append(jnp.where(halves[e], dopb, jnp.zeros_like(dopb)))
            return rows, pads, scores, dps, dhs, qms, doms

        def softmax_phase(state):
            rows, pads, scores, dps, dhs, qms, doms = state
            dss, pbs = [], []
            dsk_t = jnp.zeros((1, BLOCK), F32)
            for h in range(nq):
                lse_h = lse_ref[rows, h:h + 1]
                p = jnp.exp(scores[h] - lse_h)
                dss.append((p * (dps[h] - dhs[h])).astype(BF16))
                pbs.append(p.astype(BF16))
                dsink = -jnp.sum(jnp.exp(sk_ref[:, h:h + 1] - lse_h) * dhs[h], axis=0, keepdims=True)
                dsk_t = jnp.where(lane1 == h, dsink, dsk_t)
            dsk_ref[...] += dsk_t
            return rows, pads, dss, pbs, qms, doms

        def grad_phase(state):
            rows, pads, dss, pbs, qms, doms = state
            dsecs = []
            for kvh in range(N_KV_HEADS):
                kpad = pads[kvh][0]
                dk_acc = jnp.zeros((2 * BLOCK, BLOCK), F32)
                dv_acc = jnp.zeros((2 * BLOCK, BLOCK), F32)
                for a in range(kvh * pairs_per_kv, (kvh + 1) * pairs_per_kv):
                    dqp = _dot(dss[2 * a], kpad[0]) + _dot(dss[2 * a + 1], kpad[1])
                    dq_s[rows, a * BLOCK:(a + 1) * BLOCK] = dqp * scale
                    for e in range(2):
                        h = 2 * a + e
                        dk_acc = dk_acc + _dot_tn(dss[h], qms[h])
                        dv_acc = dv_acc + _dot_tn(pbs[h], doms[h])
                dsecs.append((dk_acc + pltpu.roll(dk_acc, HEAD_DIM, 1), dv_acc + pltpu.roll(dv_acc, HEAD_DIM, 1)))
            lo = lane2 < HEAD_DIM
            dkv = jnp.concatenate([jnp.where(lo, dsecs[0][0], dsecs[1][0]),
                                   jnp.where(lo, dsecs[0][1], dsecs[1][1])], axis=1)
            dkp_ref[rows, :] = dkv[:BLOCK]
            dkc_ref[rows, :] = dkv[BLOCK:]

        for state in [softmax_phase(s) for s in [score_phase(j) for j in range(nbt)]]:
            grad_phase(state)
        dqv = dq_s[...]
        dq_ref[...] = dqv.astype(BF16)
        st_ref[0:1, :] += jnp.sum(dqv, axis=0, keepdims=True)
        st_ref[1:2, :] += jnp.sum(dzv, axis=0, keepdims=True)

    return _gridded_call(
        body, "attn_bwd", t // tm,
        [_tile(tm, d), _tile(tm, d), _tile(tm, d), _tile(tm, BLOCK), _tile(tm, 2 * BLOCK),
         pl.BlockSpec((BLOCK, 2 * BLOCK), lambda i: (jnp.maximum(i * nbt - 1, 0), 0)),
         ANY, _row(nq)],
        [_tile(tm, d), _tile(tm, d), _tile(tm, 2 * BLOCK), _tile(tm, 2 * BLOCK),
         _fixed(8, d), _row(BLOCK)],
        [jax.ShapeDtypeStruct((t, d), BF16), jax.ShapeDtypeStruct((t, d), BF16),
         jax.ShapeDtypeStruct((t, 2 * BLOCK), F32), jax.ShapeDtypeStruct((t, 2 * BLOCK), F32),
         jax.ShapeDtypeStruct((8, d), F32), jax.ShapeDtypeStruct((1, BLOCK), F32)],
        [_wscratch(lay, "wo"), pltpu.VMEM((BLOCK + tm, 2 * BLOCK), BF16),
         pltpu.VMEM((tm, d), F32), pltpu.VMEM((tm, d), F32),
         pltpu.VMEM((2, nq, BLOCK, 2 * BLOCK), F32)],
        (dz, q, o, lse, kvs, kvs, wb, sinks), hosted)


def _tn_matmul(a, b, name, bm, tk):
    t, m = a.shape
    n = b.shape[1]
    ksteps = t // tk

    def body(a_ref, b_ref, o_ref, acc):
        k = pl.program_id(1)
        part = _dot_tn(a_ref[...], b_ref[...])

        @pl.when(k == 0)
        def _():
            acc[...] = part

        @pl.when(k > 0)
        def _():
            acc[...] += part

        @pl.when(k == ksteps - 1)
        def _():
            o_ref[...] = acc[...].astype(BF16)

    return pl.pallas_call(
        body, name=name, grid=(m // bm, ksteps),
        in_specs=[pl.BlockSpec((tk, bm), lambda j, k: (k, j)), pl.BlockSpec((tk, n), lambda j, k: (k, 0))],
        out_specs=pl.BlockSpec((bm, n), lambda j, k: (j, 0)),
        out_shape=jax.ShapeDtypeStruct((m, n), BF16),
        scratch_shapes=[pltpu.VMEM((bm, n), F32)],
        compiler_params=pltpu.CompilerParams(dimension_semantics=("arbitrary", "arbitrary"),
                                             vmem_limit_bytes=VMEM_LIMIT),
    )(a, b)


def _all_gather(arrays, name):
    n = len(arrays)

    def body(*refs):
        ins, outs = refs[:n], refs[n:2 * n]
        send_sems, recv_sems, local_sems = refs[2 * n:]
        x, y, c = _me()
        me, sibling = (x, y, c), (x, y, 1 - c)
        chips = [(1 - x, y), (x, 1 - y), (1 - x, 1 - y)]

        def slot(ref, dev):
            return ref.at[4 * dev[0] + 2 * dev[1] + dev[2]]

        def copy(a, k, block, to, src=None):
            return pltpu.make_async_remote_copy(
                src_ref=slot(outs[a], block) if src is None else src, dst_ref=slot(outs[a], block),
                send_sem=send_sems.at[a, k], recv_sem=recv_sems.at[a, k], device_id=to, device_id_type=MESH)

        mine = [pltpu.make_async_copy(ins[a], slot(outs[a], me), local_sems.at[a]) for a in range(n)]
        for cp in mine:
            cp.start()
        first = []
        for a in range(n):
            first.append(copy(a, 0, me, sibling, src=ins[a]))
            first += [copy(a, 1 + j, me, (*chip, c), src=ins[a]) for j, chip in enumerate(chips)]
        for cp in first:
            cp.start()
        passed = []
        for a in range(n):
            for j, chip in enumerate(chips):
                copy(a, 1 + j, (*chip, c), me).wait_recv()
                cp = copy(a, 4 + j, (*chip, c), sibling)
                cp.start()
                passed.append(cp)
        for a in range(n):
            copy(a, 0, sibling, me).wait_recv()
            for j, chip in enumerate(chips):
                copy(a, 4 + j, (*chip, 1 - c), me).wait_recv()
        for cp in first + passed:
            cp.wait_send()
        for cp in mine:
            cp.wait()

    return pl.pallas_call(
        body, name=name, in_specs=[ANY] * n, out_specs=[ANY] * n,
        out_shape=[jax.ShapeDtypeStruct((N_DEV,) + a.shape, a.dtype) for a in arrays],
        scratch_shapes=[pltpu.SemaphoreType.DMA((n, 7)), pltpu.SemaphoreType.DMA((n, 7)),
                        pltpu.SemaphoreType.DMA((n,))],
    )(*arrays)


def _exchange(arrays, name):
    n = len(arrays)
    blocked = [a.ndim == 3 for a in arrays]

    def body(*refs):
        ins, outs = refs[:n], refs[n:2 * n]
        send_sems, recv_sems, local_sems = refs[2 * n:]
        me = _index(_me())

        def src(k, dev):
            return ins[k].at[dev] if blocked[k] else ins[k]

        local = [pltpu.make_async_copy(src(k, me), outs[k].at[me], local_sems.at[k]) for k in range(n)]
        sends, arrivals = [], []
        for k in range(n):
            for mask in range(1, N_DEV):
                peer = _peer(mask)
                sends.append(pltpu.make_async_remote_copy(
                    src_ref=src(k, _index(peer)), dst_ref=outs[k].at[me], send_sem=send_sems.at[k, mask - 1],
                    recv_sem=recv_sems.at[k, mask - 1], device_id=peer, device_id_type=MESH))
                arrivals.append(pltpu.make_async_remote_copy(
                    src_ref=src(k, me), dst_ref=outs[k].at[_index(peer)], send_sem=send_sems.at[k, mask - 1],
                    recv_sem=recv_sems.at[k, mask - 1], device_id=_me(), device_id_type=MESH))
        for cp in local + sends:
            cp.start()
        for cp in arrivals:
            cp.wait_recv()
        for cp in sends:
            cp.wait_send()
        for cp in local:
            cp.wait()

    return pl.pallas_call(
        body, name=name, in_specs=[ANY] * n, out_specs=[ANY] * n,
        out_shape=[jax.ShapeDtypeStruct((N_DEV,) + a.shape[-2:], a.dtype) for a in arrays],
        scratch_shapes=[pltpu.SemaphoreType.DMA((n, 7)), pltpu.SemaphoreType.DMA((n, 7)),
                        pltpu.SemaphoreType.DMA((n,))],
    )(*arrays)


def _adamw_update(g, w_ref, m_ref, v_ref, go_ref, d_ref, mo_ref, vo_ref):
    mn = ADAM_B1 * m_ref[...] + (1.0 - ADAM_B1) * g
    vn = ADAM_B2 * v_ref[...] + (1.0 - ADAM_B2) * (g * g)
    m_hat = mn / (1.0 - ADAM_B1 ** ADAM_STEP)
    v_hat = vn / (1.0 - ADAM_B2 ** ADAM_STEP)
    go_ref[...] = g
    d_ref[...] = -ADAM_LR * (m_hat / (jnp.sqrt(v_hat) + ADAM_EPS) + ADAM_WD * w_ref[...])
    mo_ref[...] = mn
    vo_ref[...] = vn


def _sum_sources(g_refs, layer):
    total = None
    for l, g_ref in enumerate(g_refs):
        g = g_ref[0].astype(F32)
        for s in range(1, N_DEV):
            g = g + g_ref[s].astype(F32)
        total = g if total is None else jnp.where(layer == l, g, total)
    return total


def _layer_block(l_mine, nblocks):
    def index(l, j):
        return (0, jnp.where(l == l_mine, j, jnp.where(l < l_mine, 0, nblocks - 1)), 0)
    return index


def _adamw_sum(g8, w, m, v, name, tr):
    r, width = w.shape

    def body(g_ref, *refs):
        _adamw_update(_sum_sources([g_ref], 0), *refs)

    spec = pl.BlockSpec((tr, width), lambda i: (i, 0))
    return pl.pallas_call(
        body, name=name, grid=(r // tr,),
        in_specs=[pl.BlockSpec((N_DEV, tr, width), lambda i: (0, i, 0)), spec, spec, spec],
        out_specs=[spec] * 4, out_shape=[jax.ShapeDtypeStruct((r, width), F32)] * 4,
        compiler_params=_params(),
    )(g8, w, m, v)


def _adamw_rows(g8s, w, m, v, name):
    layers, n, width = w.shape
    tr = max(r for r in range(16, 177, 16) if n % r == 0)
    nb = n // tr

    def body(*refs):
        _adamw_update(_sum_sources(refs[:layers], pl.program_id(0)), *refs[layers:])

    spec = pl.BlockSpec((None, tr, width), lambda l, j: (l, j, 0))
    return pl.pallas_call(
        body, name=name, grid=(layers, nb),
        in_specs=[pl.BlockSpec((N_DEV, tr, width), _layer_block(l, nb)) for l in range(layers)] + [spec] * 3,
        out_specs=[spec] * 4, out_shape=[jax.ShapeDtypeStruct(w.shape, F32)] * 4,
        compiler_params=pltpu.CompilerParams(dimension_semantics=("arbitrary", "arbitrary"),
                                             vmem_limit_bytes=VMEM_LIMIT),
    )(*g8s, w, m, v)


def _adamw_cols(g8s, w, m, v, name):
    layers, k, n = w.shape
    cb = min(BLOCK, n)
    nb = pl.cdiv(n, cb)

    def body(*refs):
        _adamw_update(_sum_sources(refs[:layers], pl.program_id(0)).T, *refs[layers:])

    spec = pl.BlockSpec((None, k, cb), lambda l, j: (l, 0, j))
    return pl.pallas_call(
        body, name=name, grid=(layers, nb),
        in_specs=[pl.BlockSpec((N_DEV, cb, k), _layer_block(l, nb)) for l in range(layers)] + [spec] * 3,
        out_specs=[spec] * 4, out_shape=[jax.ShapeDtypeStruct(w.shape, F32)] * 4,
        compiler_params=pltpu.CompilerParams(dimension_semantics=("arbitrary", "arbitrary"),
                                             vmem_limit_bytes=VMEM_LIMIT),
    )(*g8s, w, m, v)


def _local_step(x, target, wba, shard_b, lay, sm, tm, tk):
    t, d = x.shape
    f = lay.f
    w_dw32 = jnp.concatenate([sm["w_dw"], jnp.zeros((HALO - CONV_WIDTH, d), F32)], axis=0)
    lmg, lmb, lfg, lfb = sm["ln_mix_g"], sm["ln_mix_b"], sm["ln_ffn_g"], sm["ln_ffn_b"]
    bkv = jnp.concatenate([sm["b_k"], sm["b_v"]], axis=1)
    bm_f = f // 2 if (f // 2) % 128 == 0 else f
    tm_light = 2 * tm

    received = {}

    def exchange(grads):
        return _HostedExchange([g.reshape(N_DEV, lay.n[n], d) for n, g in grads.items()])

    def keep(grads, arrived):
        received.update(zip(grads, arrived))

    xb0, ag, xhc, rsc, xh1, rs1, wbb = _conv_fwd(x, wba, lay, w_dw32, sm["b_pw1"], sm["b_dw"], sm["cg"],
                                                 sm["cb"], sm["b_pw2"], lmg[0:1], lmb[0:1], tm,
                                                 hosted=_HostedGather(shard_b))
    wkv = wbb[:, lay.goff["wkv"]:lay.goff["wkv"] + lay.n["wkv"], :].reshape(d, 2 * BLOCK)
    x1b, hg0, hu0, xh2, rs2, x2b, kvs = _ffn_fwd(xh1, lmg[0:1], lmb[0:1], wbb, lay, 0, tm,
                                                kv=(lfg[0:1], lfb[0:1], wkv, bkv))
    q, o, lse, xh3, rs3 = _attn_fwd(xh2, lfg[0:1], lfb[0:1], x2b, kvs, wbb, lay, sm["b_q"], sm["sinks"],
                                    sm["b_o"], tm)
    x3b, hg1, hu1, dz4, st4, loss = _ffn_fwd(xh3, lmg[1:2], lmb[1:2], wbb, lay, 1, tm,
                                             loss=(lfg[1:2], lfb[1:2], target))

    dz4b, act1, dhg1, dhu1, dz3, st3 = _ffn_bwd(dz4, hg1, hu1, xh3, rs3, lmg[1:2], wbb, lay, 1, tm)
    g1 = {"gt1": _tn_matmul(dhg1, x3b, "dw_gate1", bm_f, tk), "ut1": _tn_matmul(dhu1, x3b, "dw_up1", bm_f, tk),
          "dn1": _tn_matmul(act1, dz4b, "dw_down1", bm_f, tk)}
    dz3b, dq, dkc, dkp, stq, dsinks, *arrived = _attn_bwd(dz3, q, o, lse, kvs, wbb, lay, sm["sinks"], tm,
                                                          hosted=exchange(g1))
    keep(g1, arrived)
    g2 = {"wq": _tn_matmul(x2b, dq, "dw_q", d, tk), "wo": _tn_matmul(o, dz3b, "dw_o", d, tk)}
    dz2b, act0, dhg0, dhu0, dz1, st1, dkv, st2, dbkv, *arrived = _ffn_bwd(
        dz3, hg0, hu0, xh1, rs1, lmg[0:1], wbb, lay, 0, tm, hosted=exchange(g2),
        qkv=(dq, dkc, dkp, xh2, rs2, lfg[0:1], wkv))
    keep(g2, arrived)
    dz1b, s_act, dcv, stc = _conv_bwd1(dz1, xhc, rsc, wba, lay, sm["cg"], sm["cb"], tm_light)
    g3 = {"gt0": _tn_matmul(dhg0, x1b, "dw_gate0", bm_f, tk), "ut0": _tn_matmul(dhu0, x1b, "dw_up0", bm_f, tk),
          "dn0": _tn_matmul(act0, dz2b, "dw_down0", bm_f, tk), "pw2": _tn_matmul(s_act, dz1b, "dw_pw2", d, tk),
          "wkv": _tn_matmul(x2b, dkv, "dw_kv", d, tk)}
    grad_x, dh1, dwdw, db1, *arrived = _conv_bwd2(dz1, dcv, ag, wba, lay, w_dw32, tm, hosted=exchange(g3))
    keep(g3, arrived)
    g_pw1t = _tn_matmul(dh1, xb0, "dw_pw1", d, tk)
    small = {
        "w_dw": dwdw[:CONV_WIDTH], "b_pw1": db1, "b_dw": stc[2:3], "cg": stc[0:1], "cb": stc[1:2],
        "b_pw2": stc[3:4], "b_k": dbkv[:, :BLOCK], "b_v": dbkv[:, BLOCK:], "b_q": stq[0:1],
        "sinks": dsinks[:, :d // HEAD_DIM],
        "b_o": stq[1:2],
        "ln_mix_g": jnp.concatenate([st1[0:1], st3[0:1]], axis=0),
        "ln_mix_b": jnp.concatenate([st1[1:2], st3[1:2]], axis=0),
        "ln_ffn_g": jnp.concatenate([st2[0:1], st4[0:1]], axis=0),
        "ln_ffn_b": jnp.concatenate([st2[1:2], st4[1:2]], axis=0),
    }
    return loss[0, 0], grad_x, received, g_pw1t, small


SP_ROWS = 40
SP_BDW, SP_CG, SP_CB, SP_BPW2, SP_BPW1 = 32, 33, 34, 35, 36
RP_NAMES = ("ln_mix_g", "ln_mix_b", "ln_ffn_g", "ln_ffn_b", "b_q", "b_o", "b_k", "b_v", "sinks")


def _row_forms(d, pw1, pw2, wq, wo, gate, up, down, wk, wv):
    rf = {"pw1t": pw1[0].T, "pw2": pw2[0], "wq": wq[0], "wo": wo[0],
          "wkv": jnp.concatenate([wk, wv], axis=1).reshape(-1, d)}
    for l in range(DEPTH):
        rf.update({f"gt{l}": gate[l].T, f"ut{l}": up[l].T, f"dn{l}": down[l]})
    return rf


def _pack_rows(rf, names):
    return jnp.concatenate([rf[n] for n in names], axis=0)


def _pack_small(w_dw, b_dw, cg, cb, b_pw2, b_pw1):
    cw = b_dw.shape[1]
    z = jnp.zeros((1, cw), F32)
    return jnp.concatenate([w_dw[0], z, b_dw, cg, cb, b_pw2, b_pw1.reshape(2, cw), z, z], axis=0)


def _unpack_small(p):
    cw = p.shape[1]
    return dict(w_dw=p[None, :CONV_WIDTH], b_dw=p[SP_BDW:SP_BDW + 1], cg=p[SP_CG:SP_CG + 1],
                cb=p[SP_CB:SP_CB + 1], b_pw2=p[SP_BPW2:SP_BPW2 + 1],
                b_pw1=p[SP_BPW1:SP_BPW1 + 2].reshape(1, 2 * cw))


def _small_full(g):
    d = N_DEV * g.shape[2]

    def wide(r0, n=1):
        return jnp.transpose(g[:, r0:r0 + n], (1, 0, 2)).reshape(n, d)

    return dict(w_dw=wide(0, CONV_WIDTH), b_dw=wide(SP_BDW), cg=wide(SP_CG), cb=wide(SP_CB),
                b_pw2=wide(SP_BPW2), b_pw1=g[:, SP_BPW1:SP_BPW1 + 2].reshape(1, 2 * d))


def _small_grad_blocks(sg):
    cw = sg["b_dw"].shape[1] // N_DEV

    def narrow(a):
        return jnp.transpose(a.reshape(a.shape[0], N_DEV, cw), (1, 0, 2))

    z = jnp.zeros((N_DEV, 1, cw), F32)
    return jnp.concatenate([narrow(sg["w_dw"]), z, narrow(sg["b_dw"]), narrow(sg["cg"]), narrow(sg["cb"]),
                            narrow(sg["b_pw2"]), sg["b_pw1"].reshape(N_DEV, 2, cw), z, z], axis=1)


def _pack_rep(vals, rider=0.0):
    parts = []
    for name in RP_NAMES:
        a = vals[name].reshape(-1)
        pad = -a.shape[0] % 128
        parts.append(jnp.concatenate([a, jnp.zeros((pad,), F32)]).reshape(-1, 128))
    parts.append(jnp.full((1, 128), rider, F32))
    rows = sum(p.shape[0] for p in parts)
    parts.append(jnp.zeros((-rows % 8, 128), F32))
    return jnp.concatenate(parts, axis=0)


def _rider_row(shapes):
    return sum(-(-_size(shapes[name]) // 128) for name in RP_NAMES)


def _size(shape):
    n = 1
    for s in shape:
        n *= s
    return n


def _unpack_rep(p, shapes):
    out, r = {}, 0
    for name in RP_NAMES:
        n = _size(shapes[name])
        rows = -(-n // 128)
        out[name] = p[r:r + rows].reshape(-1)[:n].reshape(shapes[name])
        r += rows
    return out


def kernel(x, conv_w_pw1, conv_b_pw1, conv_w_dw, conv_b_dw, conv_ln_g, conv_ln_b, conv_w_pw2, conv_b_pw2, kv_w_k, kv_b_k, kv_w_v, kv_b_v, attn_w_q, attn_b_q, attn_sinks, attn_w_o, attn_b_o, ffn_w_gate, ffn_w_up, ffn_w_down, ln_mix_g, ln_mix_b, ln_ffn_g, ln_ffn_b, loss_target, m_conv_w_pw1, m_conv_b_pw1, m_conv_w_dw, m_conv_b_dw, m_conv_ln_g, m_conv_ln_b, m_conv_w_pw2, m_conv_b_pw2, m_kv_w_k, m_kv_b_k, m_kv_w_v, m_kv_b_v, m_attn_w_q, m_attn_b_q, m_attn_sinks, m_attn_w_o, m_attn_b_o, m_ffn_w_gate, m_ffn_w_up, m_ffn_w_down, m_ln_mix_g, m_ln_mix_b, m_ln_ffn_g, m_ln_ffn_b, v_conv_w_pw1, v_conv_b_pw1, v_conv_w_dw, v_conv_b_dw, v_conv_ln_g, v_conv_ln_b, v_conv_w_pw2, v_conv_b_pw2, v_kv_w_k, v_kv_b_k, v_kv_w_v, v_kv_b_v, v_attn_w_q, v_attn_b_q, v_attn_sinks, v_attn_w_o, v_attn_b_o, v_ffn_w_gate, v_ffn_w_up, v_ffn_w_down, v_ln_mix_g, v_ln_mix_b, v_ln_ffn_g, v_ln_ffn_b):
    t, d = x.shape[1], x.shape[2]
    f = ffn_w_gate.shape[2] * N_DEV
    lay = _Layout(d, f)
    tm, tk = 256, min(2048, t)

    rep_shapes = dict(ln_mix_g=ln_mix_g.shape, ln_mix_b=ln_mix_b.shape, ln_ffn_g=ln_ffn_g.shape,
                      ln_ffn_b=ln_ffn_b.shape, b_q=attn_b_q.shape, b_o=attn_b_o.shape, b_k=kv_b_k.shape,
                      b_v=kv_b_v.shape, sinks=attn_sinks.shape)

    def rep_pack(lmg, lmb, lfg, lfb, bq, bo, bk, bv, sk):
        return _pack_rep(dict(ln_mix_g=lmg, ln_mix_b=lmb, ln_ffn_g=lfg, ln_ffn_b=lfb, b_q=bq, b_o=bo,
                              b_k=bk, b_v=bv, sinks=sk))

    w_rf = _row_forms(d, conv_w_pw1, conv_w_pw2, attn_w_q, attn_w_o, ffn_w_gate, ffn_w_up, ffn_w_down, kv_w_k, kv_w_v)
    w_small = _pack_small(conv_w_dw, conv_b_dw, conv_ln_g, conv_ln_b, conv_b_pw2, conv_b_pw1)
    m_small = _pack_small(m_conv_w_dw, m_conv_b_dw, m_conv_ln_g, m_conv_ln_b, m_conv_b_pw2, m_conv_b_pw1)
    v_small = _pack_small(v_conv_w_dw, v_conv_b_dw, v_conv_ln_g, v_conv_ln_b, v_conv_b_pw2, v_conv_b_pw1)
    w_rep = rep_pack(ln_mix_g, ln_mix_b, ln_ffn_g, ln_ffn_b, attn_b_q, attn_b_o, kv_b_k, kv_b_v, attn_sinks)
    m_rep = rep_pack(m_ln_mix_g, m_ln_mix_b, m_ln_ffn_g, m_ln_ffn_b, m_attn_b_q, m_attn_b_o, m_kv_b_k, m_kv_b_v, m_attn_sinks)
    v_rep = rep_pack(v_ln_mix_g, v_ln_mix_b, v_ln_ffn_g, v_ln_ffn_b, v_attn_b_q, v_attn_b_o, v_kv_b_k, v_kv_b_v, v_attn_sinks)

    wba, smg = _all_gather([_pack_rows(w_rf, lay.GATHER["a"]).astype(BF16), w_small], "gather_conv_weights")
    shard_b = _pack_rows(w_rf, lay.GATHER["b"]).astype(BF16)
    sm = _small_full(smg)
    sm.update(ln_mix_g=ln_mix_g, ln_mix_b=ln_mix_b, ln_ffn_g=ln_ffn_g, ln_ffn_b=ln_ffn_b, b_q=attn_b_q,
              b_o=attn_b_o, sinks=attn_sinks, b_k=kv_b_k.reshape(1, -1), b_v=kv_b_v.reshape(1, -1))

    loss_part, grad_x, received, g_pw1t, gsmall = _local_step(x[0], loss_target[0], wba, shard_b, lay, sm, tm, tk)

    received["pw1t"], g8_small, g8_rep = _exchange(
        [g_pw1t.reshape(N_DEV, lay.n["pw1t"], d), _small_grad_blocks(gsmall), _pack_rep(gsmall, loss_part)],
        "exchange_last_grads")

    def kv_rows(wk, wv):
        return jnp.concatenate([wk, wv], axis=1).reshape(1, -1, d)

    def kv_split(a):
        a = a.reshape(d // N_DEV, 2 * BLOCK)
        return a[:, :BLOCK], a[:, BLOCK:]

    big = dict(
        pw1=_adamw_cols([received["pw1t"]], conv_w_pw1, m_conv_w_pw1, v_conv_w_pw1, "adamw_pw1"),
        gate=_adamw_cols([received["gt0"], received["gt1"]], ffn_w_gate, m_ffn_w_gate, v_ffn_w_gate, "adamw_gate"),
        up=_adamw_cols([received["ut0"], received["ut1"]], ffn_w_up, m_ffn_w_up, v_ffn_w_up, "adamw_up"),
        down=_adamw_rows([received["dn0"], received["dn1"]], ffn_w_down, m_ffn_w_down, v_ffn_w_down, "adamw_down"),
        pw2=_adamw_rows([received["pw2"]], conv_w_pw2, m_conv_w_pw2, v_conv_w_pw2, "adamw_pw2"),
        wq=_adamw_rows([received["wq"]], attn_w_q, m_attn_w_q, v_attn_w_q, "adamw_q"),
        wo=_adamw_rows([received["wo"]], attn_w_o, m_attn_w_o, v_attn_w_o, "adamw_o"),
        wkv=[kv_split(a) for a in _adamw_rows([received["wkv"]], kv_rows(kv_w_k, kv_w_v), kv_rows(m_kv_w_k, m_kv_w_v),
                                              kv_rows(v_kv_w_k, v_kv_w_v), "adamw_kv")])
    big_out = [dict(pw1=big["pw1"][i], pw2=big["pw2"][i], wq=big["wq"][i], wo=big["wo"][i], gate=big["gate"][i],
                    up=big["up"][i], down=big["down"][i], wk=big["wkv"][i][0], wv=big["wkv"][i][1])
               for i in range(4)]
    small_out = [_unpack_small(a) for a in _adamw_sum(g8_small, w_small, m_small, v_small, "adamw_small", SP_ROWS)]
    rep_res = _adamw_sum(g8_rep, w_rep, m_rep, v_rep, "adamw_rep", w_rep.shape[0])
    rep_out = [_unpack_rep(a, rep_shapes) for a in rep_res]
    loss = rep_res[0][_rider_row(rep_shapes), 0]

    outs = [loss, grad_x[None]]
    for b, s, r in zip(big_out, small_out, rep_out):
        outs += [b["pw1"], s["b_pw1"], s["w_dw"], s["b_dw"], s["cg"], s["cb"], b["pw2"], s["b_pw2"],
                 b["wk"], r["b_k"], b["wv"], r["b_v"], b["wq"], r["b_q"], r["sinks"], b["wo"], r["b_o"],
                 b["gate"], b["up"], b["down"], r["ln_mix_g"], r["ln_mix_b"], r["ln_ffn_g"], r["ln_ffn_b"]]
    return tuple(outs)
```

```python
import functools

import jax
import jax.numpy as jnp
from jax import lax
from jax.experimental import pallas as pl
from jax.experimental.pallas import tpu as pltpu

F32 = jnp.float32
BF16 = jnp.bfloat16

N_DEV = 8
HEAD_DIM = 64
N_KV_HEADS = 2
BLOCK = 128
CONV_WIDTH = 31
HALO = 32
ALIBI_MAX = 8.0
DEPTH = 2
ALPHA = (2.0 * DEPTH) ** 0.25
LN_EPS = 1e-5
MASKED_DIST = 1e32
ADAM_LR = 0.001
ADAM_B1 = 0.9
ADAM_B2 = 0.999
ADAM_EPS = 1e-08
ADAM_WD = 0.01
ADAM_STEP = 10
VMEM_LIMIT = 56 * 1024 * 1024
MESH = pl.DeviceIdType.MESH


def _dot(a, b):
    return jnp.dot(a, b, preferred_element_type=F32)


def _dot_nt(a, b):
    return lax.dot_general(a, b, (((1,), (1,)), ((), ())), preferred_element_type=F32)


def _dot_tn(a, b):
    return lax.dot_general(a, b, (((0,), (0,)), ((), ())), preferred_element_type=F32)


def _sigmoid(v):
    return 1.0 / (1.0 + jnp.exp(-v))


def _ln_fwd(z):
    mu = jnp.mean(z, axis=-1, keepdims=True)
    zc = z - mu
    var = jnp.mean(zc * zc, axis=-1, keepdims=True)
    rstd = lax.rsqrt(var + LN_EPS)
    return zc * rstd, rstd


def _ln_bwd(dout, xh, rstd, g):
    dxh = dout * g
    m1 = jnp.mean(dxh, axis=-1, keepdims=True)
    m2 = jnp.mean(dxh * xh, axis=-1, keepdims=True)
    dz = rstd * (dxh - m1 - xh * m2)
    return dz, jnp.sum(dout * xh, axis=0, keepdims=True), jnp.sum(dout, axis=0, keepdims=True)


def _params(vmem=VMEM_LIMIT):
    return pltpu.CompilerParams(dimension_semantics=("arbitrary",), vmem_limit_bytes=vmem)


def _row(d):
    return pl.BlockSpec((1, d), lambda i: (0, 0))


def _tile(tm, d):
    return pl.BlockSpec((tm, d), lambda i: (i, 0))


def _fixed(r, d):
    return pl.BlockSpec((r, d), lambda i: (0, 0))


def _tile_cur(tm, d, nsteps):
    return pl.BlockSpec((tm, d), lambda i: (jnp.minimum(i, nsteps - 1), 0))


def _tile_prev(tm, d):
    return pl.BlockSpec((tm, d), lambda i: (jnp.maximum(i - 1, 0), 0))


ANY = pl.BlockSpec(memory_space=pl.ANY)


class _Layout:
    GATHER = {"a": ("pw1t", "pw2"),
              "b": ("wq", "wo", "gt0", "ut0", "dn0", "gt1", "ut1", "dn1", "wkv")}

    def __init__(self, d, f):
        self.d, self.f = d, f
        self.n = {"pw1t": 2 * d // N_DEV, "pw2": d // N_DEV, "wq": d // N_DEV, "wo": d // N_DEV,
                  "wkv": (d // N_DEV) * 2 * BLOCK // d}
        for l in range(DEPTH):
            self.n.update({f"gt{l}": f // N_DEV, f"ut{l}": f // N_DEV, f"dn{l}": f // N_DEV})
        self.goff = {}
        for names in self.GATHER.values():
            r = 0
            for name in names:
                self.goff[name] = r
                r += self.n[name]


def _load_weight(wb_ref, lay, name, dst):
    n = lay.n[name]
    for p in range(N_DEV):
        pltpu.sync_copy(wb_ref.at[p, pl.ds(lay.goff[name], n), :], dst.at[pl.ds(p * n, n), :])


def _wscratch(lay, name):
    return pltpu.VMEM((N_DEV * lay.n[name], lay.d), BF16)


def _wfull(ref):
    return ref[...]


def _wrows(ref, r0, nrows):
    return ref[r0:r0 + nrows, :]


def _me():
    return lax.axis_index("x"), lax.axis_index("y"), lax.axis_index("c")


def _peer(mask):
    x, y, c = _me()
    return (1 - x if mask & 4 else x, 1 - y if mask & 2 else y, 1 - c if mask & 1 else c)


def _index(dev):
    return 4 * dev[0] + 2 * dev[1] + dev[2]


class _HostedGather:
    def __init__(self, array):
        self.arrays = [array]
        self.out_shapes = [jax.ShapeDtypeStruct((N_DEV,) + array.shape, array.dtype)]

    def scratch(self):
        return [pltpu.SemaphoreType.DMA((7,)), pltpu.SemaphoreType.DMA((7,)), pltpu.SemaphoreType.DMA(())]

    def _copies(self, ins, outs, send_sems, recv_sems, local_sem):
        out = outs[0]
        x, y, c = _me()
        me, sibling = (x, y, c), (x, y, 1 - c)
        chips = [(1 - x, y), (x, 1 - y), (1 - x, 1 - y)]

        def copy(k, block, to, src=None):
            rows = out.at[_index(block)]
            return pltpu.make_async_remote_copy(
                src_ref=rows if src is None else src, dst_ref=rows, send_sem=send_sems.at[k],
                recv_sem=recv_sems.at[k], device_id=to, device_id_type=MESH)

        return dict(
            mine=lambda: pltpu.make_async_copy(ins[0], out.at[_index(me)], local_sem),
            first=lambda: [copy(0, me, sibling, src=ins[0])] + [copy(1 + j, me, (*chip, c), src=ins[0])
                                                                for j, chip in enumerate(chips)],
            over_ici=lambda: [copy(1 + j, (*chip, c), me) for j, chip in enumerate(chips)],
            passed=lambda: [copy(4 + j, (*chip, c), sibling) for j, chip in enumerate(chips)],
            from_sibling=lambda: [copy(0, sibling, me)] + [copy(4 + j, (*chip, 1 - c), me)
                                                           for j, chip in enumerate(chips)])

    def start(self, *refs):
        cp = self._copies(*refs)
        cp["mine"]().start()
        for c in cp["first"]():
            c.start()

    def middle(self, *refs):
        cp = self._copies(*refs)
        for arrived, onward in zip(cp["over_ici"](), cp["passed"]()):
            arrived.wait_recv()
            onward.start()

    def finish(self, *refs):
        cp = self._copies(*refs)
        for c in cp["from_sibling"]():
            c.wait_recv()
        for c in cp["first"]() + cp["passed"]():
            c.wait_send()
        cp["mine"]().wait()


class _HostedExchange:
    def __init__(self, arrays):
        self.arrays = list(arrays)
        self.out_shapes = [jax.ShapeDtypeStruct(a.shape, a.dtype) for a in self.arrays]

    def scratch(self):
        n = len(self.arrays)
        return [pltpu.SemaphoreType.DMA((n, 7)), pltpu.SemaphoreType.DMA((n, 7)), pltpu.SemaphoreType.DMA((n,))]

    def _copies(self, ins, outs, send_sems, recv_sems, local_sems):
        me = _index(_me())

        def dst(k, src_dev):
            return outs[k].at[src_dev]

        pairs = [(k, mask) for k in range(len(self.arrays)) for mask in range(1, N_DEV)]

        def local():
            return [pltpu.make_async_copy(ins[k].at[me], dst(k, me), local_sems.at[k])
                    for k in range(len(self.arrays))]

        def sends():
            return [pltpu.make_async_remote_copy(
                src_ref=ins[k].at[_index(_peer(mask))], dst_ref=dst(k, me), send_sem=send_sems.at[k, mask - 1],
                recv_sem=recv_sems.at[k, mask - 1], device_id=_peer(mask), device_id_type=MESH)
                for k, mask in pairs]

        def arrivals():
            return [pltpu.make_async_remote_copy(
                src_ref=ins[k].at[me], dst_ref=dst(k, _index(_peer(mask))), send_sem=send_sems.at[k, mask - 1],
                recv_sem=recv_sems.at[k, mask - 1], device_id=_me(), device_id_type=MESH)
                for k, mask in pairs]

        return local, sends, arrivals

    def start(self, *refs):
        local, sends, _ = self._copies(*refs)
        for c in local() + sends():
            c.start()

    def middle(self, *refs):
        pass

    def finish(self, *refs):
        local, sends, arrivals = self._copies(*refs)
        for c in arrivals():
            c.wait_recv()
        for c in sends():
            c.wait_send()
        for c in local():
            c.wait()


def _gridded_call(body, name, nsteps, in_specs, out_specs, out_shape, scratch, args, hosted=None):
    if hosted is None:
        return pl.pallas_call(body, name=name, grid=(nsteps,), in_specs=in_specs, out_specs=out_specs,
                              out_shape=out_shape, scratch_shapes=scratch, compiler_params=_params())(*args)
    n_in, n_out, n_scr, h_in = len(in_specs), len(out_specs), len(scratch), len(hosted.arrays)
    h_out = len(hosted.out_shapes)

    def with_hosted(*refs):
        a = n_in + h_in
        b = a + n_out
        e = b + h_out + n_scr
        comm = (refs[n_in:a], refs[b:b + h_out], refs[e], refs[e + 1], refs[e + 2])
        i = pl.program_id(0)

        @pl.when(i == 0)
        def _():
            hosted.start(*comm)

        body(*refs[:n_in], *refs[a:b], *refs[b + h_out:e])

        @pl.when(i == nsteps // 2)
        def _():
            hosted.middle(*comm)

        @pl.when(i == nsteps - 1)
        def _():
            hosted.finish(*comm)

    return pl.pallas_call(
        with_hosted, name=name, grid=(nsteps,), in_specs=list(in_specs) + [ANY] * h_in,
        out_specs=list(out_specs) + [ANY] * h_out, out_shape=list(out_shape) + hosted.out_shapes,
        scratch_shapes=list(scratch) + hosted.scratch(), compiler_params=_params(),
    )(*args, *hosted.arrays)


CONV_RB = 64
CONV_LC = 128
CONV_WIN = CONV_RB + HALO + 8
CONV_MC = 256


def _shifted(win, r):
    return win if r == 0 else pltpu.roll(win, win.shape[0] - r, 0)


def _conv_fwd(x, wb, lay, w_dw, b_pw1, b_dw, cg, cb, b_pw2, lg, lb, tm, hosted=None):
    t, d = x.shape
    nsteps = t // tm

    def body(x_ref, xh_ref, wb_ref, wdw_ref, b1_ref, bdw_ref, cg_ref, cb_ref, b2_ref, lg_ref, lb_ref,
             xb_ref, ag_ref, xhc_ref, rsc_ref, xh1_ref, rs1_ref, w1_s, w2_s, ubuf, cv_s):
        i = pl.program_id(0)

        @pl.when(i == 0)
        def _():
            _load_weight(wb_ref, lay, "pw1t", w1_s)
            _load_weight(wb_ref, lay, "pw2", w2_s)
            ubuf[pl.ds(HALO + tm, 8), :] = jnp.zeros((8, d), F32)

        xv = x_ref[...]
        xb = xv.astype(BF16)
        xb_ref[...] = xb
        xcat = jnp.concatenate([xh_ref[...].astype(BF16), xb], axis=0)
        for mc in range(d // CONV_MC):
            c0 = mc * CONV_MC
            acols, gcols = slice(c0, c0 + CONV_MC), slice(d + c0, d + c0 + CONV_MC)
            ha = _dot_nt(xcat, _wrows(w1_s, c0, CONV_MC)) + b1_ref[:, acols]
            hg = _dot_nt(xcat, _wrows(w1_s, d + c0, CONV_MC)) + b1_ref[:, gcols]
            ag_ref[:, acols] = ha[HALO:].astype(BF16)
            ag_ref[:, gcols] = hg[HALO:].astype(BF16)
            u = ha * _sigmoid(hg)
            ubuf[0:HALO, acols] = jnp.where(i > 0, u[:HALO], 0.0)
            ubuf[HALO:HALO + tm, acols] = u[HALO:]
            for rb in range(tm // CONV_RB):
                t0 = rb * CONV_RB
                for lc in range(CONV_MC // CONV_LC):
                    lanes = slice(c0 + lc * CONV_LC, c0 + (lc + 1) * CONV_LC)
                    win = ubuf[t0:t0 + CONV_WIN, lanes]
                    acc = jnp.zeros((CONV_RB, CONV_LC), F32)
                    for r in range(8):
                        wr = _shifted(win, r)
                        for k in range(CONV_WIDTH):
                            s = HALO - (CONV_WIDTH - 1) + k
                            if s % 8 == r:
                                q = 8 * (s // 8)
                                acc = acc + wr[q:q + CONV_RB] * wdw_ref[k:k + 1, lanes]
                    cv_s[t0:t0 + CONV_RB, lanes] = acc
        cv = cv_s[...] + bdw_ref[...]
        xhc, rsc = _ln_fwd(cv)
        xhc_ref[...] = xhc
        rsc_ref[...] = rsc
        n = xhc * cg_ref[...] + cb_ref[...]
        s_act = n * _sigmoid(n)
        m = _dot(s_act.astype(BF16), _wfull(w2_s)) + b2_ref[...]
        xh1, rs1 = _ln_fwd(ALPHA * xv + m)
        xh1_ref[...] = xh1
        rs1_ref[...] = rs1

    hb = tm // HALO
    return _gridded_call(
        body, "conv_fwd", nsteps,
        [_tile(tm, d), pl.BlockSpec((HALO, d), lambda i: (jnp.maximum(i * hb - 1, 0), 0)), ANY,
         _fixed(HALO, d), _row(2 * d), _row(d), _row(d), _row(d), _row(d), _row(d), _row(d)],
        [_tile(tm, d), _tile(tm, 2 * d), _tile(tm, d), _tile(tm, 1), _tile(tm, d), _tile(tm, 1)],
        [jax.ShapeDtypeStruct((t, d), BF16), jax.ShapeDtypeStruct((t, 2 * d), BF16),
         jax.ShapeDtypeStruct((t, d), F32), jax.ShapeDtypeStruct((t, 1), F32),
         jax.ShapeDtypeStruct((t, d), F32), jax.ShapeDtypeStruct((t, 1), F32)],
        [_wscratch(lay, "pw1t"), _wscratch(lay, "pw2"),
         pltpu.VMEM((HALO + tm + 8, d), F32), pltpu.VMEM((tm, d), F32)],
        (x, x, wb, w_dw, b_pw1, b_dw, cg, cb, b_pw2, lg, lb), hosted)


def _conv_bwd1(dz1, xhc, rsc, wb, lay, cg, cb, tm):
    t, d = dz1.shape

    def body(dz_ref, xhc_ref, rsc_ref, wb_ref, cg_ref, cb_ref, dzb_ref, s_ref, dcv_ref, st_ref, w2_s):
        i = pl.program_id(0)

        @pl.when(i == 0)
        def _():
            _load_weight(wb_ref, lay, "pw2", w2_s)
            st_ref[...] = jnp.zeros(st_ref.shape, F32)

        dz = dz_ref[...]
        dzb = dz.astype(BF16)
        dzb_ref[...] = dzb
        xhc_v = xhc_ref[...]
        n = xhc_v * cg_ref[...] + cb_ref[...]
        sg = _sigmoid(n)
        s_ref[...] = (n * sg).astype(BF16)
        ds = _dot_nt(dzb, _wfull(w2_s))
        dn = ds * (sg * (1.0 + n * (1.0 - sg)))
        dcv, dg, db = _ln_bwd(dn, xhc_v, rsc_ref[...], cg_ref[...])
        dcv_ref[...] = dcv
        st_ref[0:1, :] += dg
        st_ref[1:2, :] += db
        st_ref[2:3, :] += jnp.sum(dcv, axis=0, keepdims=True)
        st_ref[3:4, :] += jnp.sum(dz, axis=0, keepdims=True)

    return pl.pallas_call(
        body, name="conv_bwd1", grid=(t // tm,),
        in_specs=[_tile(tm, d), _tile(tm, d), _tile(tm, 1), ANY, _row(d), _row(d)],
        out_specs=[_tile(tm, d), _tile(tm, d), _tile(tm, d), _fixed(8, d)],
        out_shape=[jax.ShapeDtypeStruct((t, d), BF16), jax.ShapeDtypeStruct((t, d), BF16),
                   jax.ShapeDtypeStruct((t, d), F32), jax.ShapeDtypeStruct((8, d), F32)],
        scratch_shapes=[_wscratch(lay, "pw2")],
        compiler_params=_params(),
    )(dz1, xhc, rsc, wb, cg, cb)


def _conv_bwd2(dz1, dcv, ag, wb, lay, w_dw, tm, hosted=None):
    t, d = dz1.shape
    nsteps = t // tm

    def body(dz_ref, dcv_ref, dcvn_ref, ag_ref, wb_ref, wdw_ref,
             gx_ref, dh_ref, dw_ref, db1_ref, w1_s, ubuf, dbuf, du_s, dwacc):
        i = pl.program_id(0)

        @pl.when(i == 0)
        def _():
            _load_weight(wb_ref, lay, "pw1t", w1_s)
            dbuf[pl.ds(HALO + tm, 8), :] = jnp.zeros((8, d), F32)
            dwacc[...] = jnp.zeros(dwacc.shape, F32)
            db1_ref[...] = jnp.zeros(db1_ref.shape, F32)

        dbuf[0:tm, :] = dcv_ref[...]
        dbuf[tm:tm + HALO, :] = jnp.where(i < nsteps - 1, dcvn_ref[...], 0.0)
        gx = ALPHA * dz_ref[...]
        for mc in range(d // CONV_MC):
            c0 = mc * CONV_MC
            acols, gcols = slice(c0, c0 + CONV_MC), slice(d + c0, d + c0 + CONV_MC)
            a = ag_ref[:, acols].astype(F32)
            sg = _sigmoid(ag_ref[:, gcols].astype(F32))
            ubuf[:, acols] = a * sg
            for rb in range(tm // CONV_RB):
                t0 = rb * CONV_RB
                for lc in range(CONV_MC // CONV_LC):
                    lanes = slice(c0 + lc * CONV_LC, c0 + (lc + 1) * CONV_LC)
                    dwin = dbuf[t0:t0 + CONV_WIN, lanes]
                    ucur = ubuf[t0:t0 + CONV_RB, lanes]
                    acc = jnp.zeros((CONV_RB, CONV_LC), F32)
                    for r in range(8):
                        dr = _shifted(dwin, r)
                        for k in range(CONV_WIDTH):
                            sd = CONV_WIDTH - 1 - k
                            if sd % 8 == r:
                                q = 8 * (sd // 8)
                                dk = dr[q:q + CONV_RB]
                                acc = acc + dk * wdw_ref[k:k + 1, lanes]
                                prod = ucur * dk
                                part = prod[0:8]
                                for j in range(1, CONV_RB // 8):
                                    part = part + prod[8 * j:8 * j + 8]
                                dwacc[k, :, lanes] += part
                    du_s[t0:t0 + CONV_RB, lanes] = acc
            du = du_s[:, acols]
            da = du * sg
            dg = du * a * sg * (1.0 - sg)
            dab, dgb = da.astype(BF16), dg.astype(BF16)
            dh_ref[:, acols] = dab
            dh_ref[:, gcols] = dgb
            db1_ref[:, acols] += jnp.sum(da, axis=0, keepdims=True)
            db1_ref[:, gcols] += jnp.sum(dg, axis=0, keepdims=True)
            gx = gx + _dot(dab, _wrows(w1_s, c0, CONV_MC)) + _dot(dgb, _wrows(w1_s, d + c0, CONV_MC))
        gx_ref[...] = gx

        @pl.when(i == nsteps - 1)
        def _():
            dw_ref[...] = jnp.sum(dwacc[...], axis=1)

    hb = tm // HALO
    last = t // HALO - 1
    return _gridded_call(
        body, "conv_bwd2", nsteps,
        [_tile(tm, d), _tile(tm, d),
         pl.BlockSpec((HALO, d), lambda i: (jnp.minimum((i + 1) * hb, last), 0)),
         _tile(tm, 2 * d), ANY, _fixed(HALO, d)],
        [_tile(tm, d), _tile(tm, 2 * d), _fixed(HALO, d), _row(2 * d)],
        [jax.ShapeDtypeStruct((t, d), F32), jax.ShapeDtypeStruct((t, 2 * d), BF16),
         jax.ShapeDtypeStruct((HALO, d), F32), jax.ShapeDtypeStruct((1, 2 * d), F32)],
        [_wscratch(lay, "pw1t"), pltpu.VMEM((tm, d), F32),
         pltpu.VMEM((HALO + tm + 8, d), F32), pltpu.VMEM((tm, d), F32),
         pltpu.VMEM((HALO, 8, d), F32)],
        (dz1, dcv, dcv, ag, wb, w_dw), hosted)


FFN_FC = 256
FFN_AHEAD = 1


def _ffn_fwd(xh_in, g_in, b_in, wb, lay, layer, tm, *, kv=None, loss=None):
    t, d = xh_in.shape
    f = lay.f
    names = (f"gt{layer}", f"ut{layer}", f"dn{layer}")

    def body(*refs):
        xh_ref, gi_ref, bi_ref, wb_ref = refs[:4]
        pos = 4
        if kv is not None:
            go_ref, bo_ref, wkv_ref, bkv_ref = refs[pos:pos + 4]
            pos += 4
        if loss is not None:
            go_ref, bo_ref, tgt_ref = refs[pos:pos + 3]
            pos += 3
        xb_ref, hg_ref, hu_ref = refs[pos:pos + 3]
        pos += 3
        if kv is not None:
            xho_ref, rso_ref, xob_ref, kv_ref = refs[pos:pos + 4]
            pos += 4
        if loss is not None:
            dz_ref, st_ref, loss_ref = refs[pos:pos + 3]
            pos += 3
        gt_s, ut_s, dn_s, xin_s, fo_s = refs[pos:pos + 5]
        i = pl.program_id(0)

        @pl.when(i == 0)
        def _():
            for name, dst in zip(names, (gt_s, ut_s, dn_s)):
                _load_weight(wb_ref, lay, name, dst)
            xin_s[...] = jnp.zeros(xin_s.shape, F32)
            fo_s[...] = jnp.zeros(fo_s.shape, F32)
            if loss is not None:
                st_ref[...] = jnp.zeros(st_ref.shape, F32)
                loss_ref[...] = jnp.zeros(loss_ref.shape, F32)

        xin_prev = xin_s[...]
        xho, rso = _ln_fwd(ALPHA * xin_prev + fo_s[...])
        if kv is not None:
            xho_ref[...] = xho
            rso_ref[...] = rso
            xob_ref[...] = (xho * go_ref[...] + bo_ref[...]).astype(BF16)
        if loss is not None:
            real = i > 0
            diff = xho * go_ref[...] + bo_ref[...] - tgt_ref[...]
            loss_ref[...] += jnp.where(real, (0.5 / d) * jnp.sum(diff * diff), 0.0)
            dz, dg, db = _ln_bwd(diff * (1.0 / d), xho, rso, go_ref[...])
            dz_ref[...] = dz
            st_ref[0:1, :] += jnp.where(real, dg, 0.0)
            st_ref[1:2, :] += jnp.where(real, db, 0.0)

        xin = xh_ref[...] * gi_ref[...] + bi_ref[...]
        xb = xin.astype(BF16)
        xb_ref[...] = xb

        def up(c):
            return (_dot_nt(xb, _wrows(gt_s, c * FFN_FC, FFN_FC)), _dot_nt(xb, _wrows(ut_s, c * FFN_FC, FFN_FC)))

        fo = jnp.zeros((tm, d), F32)
        nc = f // FFN_FC
        ahead = [up(c) for c in range(min(FFN_AHEAD, nc))]
        for c in range(nc):
            rows = slice(c * FFN_FC, (c + 1) * FFN_FC)
            hg, hu = ahead.pop(0)
            if c + FFN_AHEAD < nc:
                ahead.append(up(c + FFN_AHEAD))
            hg_ref[:, rows] = hg.astype(BF16)
            hu_ref[:, rows] = hu.astype(BF16)
            act = hg * _sigmoid(hg) * hu
            fo = fo + _dot(act.astype(BF16), _wrows(dn_s, c * FFN_FC, FFN_FC))
        xin_s[...] = xin
        fo_s[...] = fo
        if kv is not None:
            kv_ref[...] = (_dot(xob_ref[...], wkv_ref[...]) + bkv_ref[...]).astype(BF16)

    nsteps = t // tm
    in_specs = [_tile_cur(tm, d, nsteps), _row(d), _row(d), ANY]
    args = [xh_in, g_in, b_in, wb]
    out_specs = [_tile_cur(tm, d, nsteps), _tile_cur(tm, f, nsteps), _tile_cur(tm, f, nsteps)]
    out_shape = [jax.ShapeDtypeStruct((t, d), BF16), jax.ShapeDtypeStruct((t, f), BF16),
                 jax.ShapeDtypeStruct((t, f), BF16)]
    if kv is not None:
        in_specs += [_row(d), _row(d), _fixed(d, 2 * BLOCK), _row(2 * BLOCK)]
        args += list(kv)
        out_specs += [_tile_prev(tm, d), _tile_prev(tm, 1), _tile_prev(tm, d), _tile_prev(tm, 2 * BLOCK)]
        out_shape += [jax.ShapeDtypeStruct((t, d), F32), jax.ShapeDtypeStruct((t, 1), F32),
                      jax.ShapeDtypeStruct((t, d), BF16), jax.ShapeDtypeStruct((t, 2 * BLOCK), BF16)]
    if loss is not None:
        in_specs += [_row(d), _row(d), _tile_prev(tm, d)]
        args += list(loss)
        out_specs += [_tile_prev(tm, d), _fixed(8, d), _fixed(8, 128)]
        out_shape += [jax.ShapeDtypeStruct((t, d), F32), jax.ShapeDtypeStruct((8, d), F32),
                      jax.ShapeDtypeStruct((8, 128), F32)]
    return pl.pallas_call(
        body, name=f"ffn_fwd{layer}", grid=(nsteps + 1,), in_specs=in_specs, out_specs=out_specs,
        out_shape=out_shape,
        scratch_shapes=[_wscratch(lay, n) for n in names] + [pltpu.VMEM((tm, d), F32), pltpu.VMEM((tm, d), F32)],
        compiler_params=_params(),
    )(*args)


def _ffn_bwd(dz, hg, hu, xh_in, rs_in, g_in, wb, lay, layer, tm, hosted=None, qkv=None):
    t, d = dz.shape
    f = lay.f
    nsteps = t // tm
    nbt = tm // BLOCK
    names = (f"gt{layer}", f"ut{layer}", f"dn{layer}")

    def body(*refs):
        dz_ref, hg_ref, hu_ref, xh_ref, rs_ref, gi_ref, wb_ref = refs[:7]
        pos = 7
        if qkv is not None:
            dq_ref, dkc_ref, dkp_ref, dkn_ref, xho_ref, rso_ref, go_ref, wkv_ref = refs[pos:pos + 8]
            pos += 8
        dzb_ref, act_ref, dhg_ref, dhu_ref, dzp_ref, st_ref = refs[pos:pos + 6]
        pos += 6
        if qkv is not None:
            dkv_ref, sto_ref, dbkv_ref = refs[pos:pos + 3]
            pos += 3
        gt_s, ut_s, dn_s = refs[pos:pos + 3]
        i = pl.program_id(0)

        @pl.when(i == 0)
        def _():
            for name, dst in zip(names, (gt_s, ut_s, dn_s)):
                _load_weight(wb_ref, lay, name, dst)
            st_ref[...] = jnp.zeros(st_ref.shape, F32)
            if qkv is not None:
                _load_weight(wb_ref, lay, "wq", refs[pos + 3])
                sto_ref[...] = jnp.zeros(sto_ref.shape, F32)
                dbkv_ref[...] = jnp.zeros(dbkv_ref.shape, F32)

        if qkv is None:
            dzv = dz_ref[...]
        else:
            nxt = jnp.where(i < nsteps - 1, dkn_ref[...], 0.0)
            shifted = jnp.concatenate([dkp_ref[pl.ds(BLOCK, tm - BLOCK), :], nxt], axis=0) if nbt > 1 else nxt
            dkv = dkc_ref[...] + shifted
            dkvb = dkv.astype(BF16)
            dkv_ref[...] = dkvb
            dbkv_ref[...] += jnp.sum(dkv, axis=0, keepdims=True)
            dxo = (ALPHA * dz_ref[...] + _dot_nt(dq_ref[...], refs[pos + 3][...])
                   + _dot_nt(dkvb, wkv_ref[...]))
            dzv, dgo, dbo = _ln_bwd(dxo, xho_ref[...], rso_ref[...], go_ref[...])
            sto_ref[0:1, :] += dgo
            sto_ref[1:2, :] += dbo
        dzb = dzv.astype(BF16)
        dzb_ref[...] = dzb
        dx = ALPHA * dzv
        def back(c):
            return _dot_nt(dzb, _wrows(dn_s, c * FFN_FC, FFN_FC))

        nc = f // FFN_FC
        ahead = [back(c) for c in range(min(FFN_AHEAD, nc))]
        for c in range(nc):
            rows = slice(c * FFN_FC, (c + 1) * FFN_FC)
            dact = ahead.pop(0)
            if c + FFN_AHEAD < nc:
                ahead.append(back(c + FFN_AHEAD))
            hg_v = hg_ref[:, rows].astype(F32)
            hu_v = hu_ref[:, rows].astype(F32)
            sg = _sigmoid(hg_v)
            silu = hg_v * sg
            act_ref[:, rows] = (silu * hu_v).astype(BF16)
            dhu = (dact * silu).astype(BF16)
            dhg = (dact * hu_v * (sg * (1.0 + hg_v * (1.0 - sg)))).astype(BF16)
            dhu_ref[:, rows] = dhu
            dhg_ref[:, rows] = dhg
            dx = (dx + _dot(dhg, _wrows(gt_s, c * FFN_FC, FFN_FC))
                  + _dot(dhu, _wrows(ut_s, c * FFN_FC, FFN_FC)))
        dzp, dg, db = _ln_bwd(dx, xh_ref[...], rs_ref[...], gi_ref[...])
        dzp_ref[...] = dzp
        st_ref[0:1, :] += dg
        st_ref[1:2, :] += db

    in_specs = [_tile(tm, d), _tile(tm, f), _tile(tm, f), _tile(tm, d), _tile(tm, 1), _row(d), ANY]
    args = [dz, hg, hu, xh_in, rs_in, g_in, wb]
    out_specs = [_tile(tm, d), _tile(tm, f), _tile(tm, f), _tile(tm, f), _tile(tm, d), _fixed(8, d)]
    out_shape = [jax.ShapeDtypeStruct((t, d), BF16), jax.ShapeDtypeStruct((t, f), BF16),
                 jax.ShapeDtypeStruct((t, f), BF16), jax.ShapeDtypeStruct((t, f), BF16),
                 jax.ShapeDtypeStruct((t, d), F32), jax.ShapeDtypeStruct((8, d), F32)]
    scratch = [_wscratch(lay, n) for n in names]
    if qkv is not None:
        dq, dkc, dkp, xh_out, rs_out, g_out, wkv = qkv
        last = t // BLOCK - 1
        in_specs += [_tile(tm, d), _tile(tm, 2 * BLOCK), _tile(tm, 2 * BLOCK),
                     pl.BlockSpec((BLOCK, 2 * BLOCK), lambda i: (jnp.minimum((i + 1) * nbt, last), 0)),
                     _tile(tm, d), _tile(tm, 1), _row(d), _fixed(d, 2 * BLOCK)]
        args += [dq, dkc, dkp, dkp, xh_out, rs_out, g_out, wkv]
        out_specs += [_tile(tm, 2 * BLOCK), _fixed(8, d), _row(2 * BLOCK)]
        out_shape += [jax.ShapeDtypeStruct((t, 2 * BLOCK), BF16), jax.ShapeDtypeStruct((8, d), F32),
                      jax.ShapeDtypeStruct((1, 2 * BLOCK), F32)]
        scratch.append(_wscratch(lay, "wq"))
    return _gridded_call(body, f"ffn_bwd{layer}", nsteps, in_specs, out_specs, out_shape, scratch, args, hosted)


def _alibi_slope(h, nq):
    return 2.0 ** (-ALIBI_MAX * (h + 1) / nq)


def _fill_alibi_bias(bias_s, nq):
    qi = lax.broadcasted_iota(jnp.int32, (BLOCK, 2 * BLOCK), 0)
    kj = lax.broadcasted_iota(jnp.int32, (BLOCK, 2 * BLOCK), 1)
    delta = qi + BLOCK - kj
    valid = (delta >= 0) & (delta < BLOCK)
    dist = jnp.where(valid, delta.astype(F32), MASKED_DIST)
    dist_first = jnp.where(kj >= BLOCK, dist, MASKED_DIST)
    for h in range(nq):
        bias_s[0, h] = _alibi_slope(h, nq) * dist
        bias_s[1, h] = _alibi_slope(h, nq) * dist_first


def _padded_kv(kvb, kvh):
    lane = lax.broadcasted_iota(jnp.int32, (2 * BLOCK, BLOCK), 1)
    mine = (lane < HEAD_DIM) if kvh == 0 else (lane >= HEAD_DIM)
    out = []
    for sec in (kvb[:, :BLOCK], kvb[:, BLOCK:]):
        m = jnp.where(mine, sec.astype(F32), 0.0)
        sw = pltpu.roll(m, HEAD_DIM, 1)
        pair = (m, sw) if kvh == 0 else (sw, m)
        out.append(tuple(p.astype(BF16) for p in pair))
    return out


def _attn_fwd(xh_in, g_in, b_in, x_in_b, kvs, wb, lay, bq, sinks, bo, tm):
    t, d = xh_in.shape
    nq = d // HEAD_DIM
    pairs_per_kv = (d // BLOCK) // N_KV_HEADS
    nbt = tm // BLOCK
    scale = HEAD_DIM ** -0.5

    def body(xh_ref, gi_ref, bi_ref, xb_ref, kv_ref, kvp_ref, wb_ref, bq_ref, sk_ref, bo_ref,
             q_ref, o_ref, lse_ref, xho_ref, rso_ref, wq_s, wo_s, kvall, q_s, o_s, bias_s):
        i = pl.program_id(0)

        @pl.when(i == 0)
        def _():
            _load_weight(wb_ref, lay, "wq", wq_s)
            _load_weight(wb_ref, lay, "wo", wo_s)
            _fill_alibi_bias(bias_s, nq)

        qv = ((_dot(xb_ref[...], _wfull(wq_s)) + bq_ref[...]) * scale).astype(BF16)
        q_s[...] = qv
        q_ref[...] = qv
        kvall[pl.ds(0, BLOCK), :] = kvp_ref[...]
        kvall[pl.ds(BLOCK, tm), :] = kv_ref[...]
        lane = lax.broadcasted_iota(jnp.int32, (BLOCK, BLOCK), 1)
        ones = jnp.ones((2 * BLOCK, BLOCK), BF16)

        def score_phase(j):
            rows = slice(j * BLOCK, (j + 1) * BLOCK)
            kvb = kvall[j * BLOCK:(j + 2) * BLOCK, :]
            first = (i * nbt + j == 0).astype(jnp.int32)
            pads = [_padded_kv(kvb, kvh) for kvh in range(N_KV_HEADS)]
            scores = []
            for a in range(d // BLOCK):
                kpad = pads[a // pairs_per_kv][0]
                qp = q_s[rows, a * BLOCK:(a + 1) * BLOCK]
                for e in range(2):
                    scores.append(_dot_nt(qp, kpad[e]) - bias_s[first, 2 * a + e])
            return rows, pads, scores

        def softmax_phase(state):
            rows, pads, scores = state
            probs, inv = [], []
            lse_t = jnp.zeros((BLOCK, BLOCK), F32)
            for h in range(nq):
                sink = sk_ref[:, h:h + 1]
                m = jnp.maximum(jnp.max(scores[h], axis=-1, keepdims=True), sink)
                p = jnp.exp(scores[h] - m).astype(BF16)
                l = _dot(p, ones) + jnp.exp(sink - m)
                lse_t = jnp.where(lane == h, m + jnp.log(l), lse_t)
                probs.append(p)
                inv.append(1.0 / l)
            lse_ref[rows, :] = lse_t
            return rows, pads, probs, inv

        def value_phase(state):
            rows, pads, probs, inv = state
            for a in range(d // BLOCK):
                vpad = pads[a // pairs_per_kv][1]
                opair = (_dot(probs[2 * a], vpad[0]) * inv[2 * a]
                         + _dot(probs[2 * a + 1], vpad[1]) * inv[2 * a + 1])
                o_s[rows, a * BLOCK:(a + 1) * BLOCK] = opair.astype(BF16)

        for state in [softmax_phase(s) for s in [score_phase(j) for j in range(nbt)]]:
            value_phase(state)
        ov = o_s[...]
        o_ref[...] = ov
        xin = xh_ref[...] * gi_ref[...] + bi_ref[...]
        xho, rso = _ln_fwd(ALPHA * xin + _dot(ov, _wfull(wo_s)) + bo_ref[...])
        xho_ref[...] = xho
        rso_ref[...] = rso

    return pl.pallas_call(
        body, name="attn_fwd", grid=(t // tm,),
        in_specs=[_tile(tm, d), _row(d), _row(d), _tile(tm, d), _tile(tm, 2 * BLOCK),
                  pl.BlockSpec((BLOCK, 2 * BLOCK), lambda i: (jnp.maximum(i * nbt - 1, 0), 0)),
                  ANY, _row(d), _row(nq), _row(d)],
        out_specs=[_tile(tm, d), _tile(tm, d), _tile(tm, BLOCK), _tile(tm, d), _tile(tm, 1)],
        out_shape=[jax.ShapeDtypeStruct((t, d), BF16), jax.ShapeDtypeStruct((t, d), BF16),
                   jax.ShapeDtypeStruct((t, BLOCK), F32), jax.ShapeDtypeStruct((t, d), F32),
                   jax.ShapeDtypeStruct((t, 1), F32)],
        scratch_shapes=[_wscratch(lay, "wq"), _wscratch(lay, "wo"),
                        pltpu.VMEM((BLOCK + tm, 2 * BLOCK), BF16), pltpu.VMEM((tm, d), BF16),
                        pltpu.VMEM((tm, d), BF16), pltpu.VMEM((2, nq, BLOCK, 2 * BLOCK), F32)],
        compiler_params=_params(),
    )(xh_in, g_in, b_in, x_in_b, kvs, kvs, wb, bq, sinks, bo)


def _attn_bwd(dz, q, o, lse, kvs, wb, lay, sinks, tm, hosted=None):
    t, d = dz.shape
    nq = d // HEAD_DIM
    pairs_per_kv = (d // BLOCK) // N_KV_HEADS
    nbt = tm // BLOCK
    scale = HEAD_DIM ** -0.5

    def body(dz_ref, q_ref, o_ref, lse_ref, kv_ref, kvp_ref, wb_ref, sk_ref,
             dzb_ref, dq_ref, dkc_ref, dkp_ref, st_ref, dsk_ref, wo_s, kvall, do_s, dq_s, bias_s):
        i = pl.program_id(0)

        @pl.when(i == 0)
        def _():
            _load_weight(wb_ref, lay, "wo", wo_s)
            _fill_alibi_bias(bias_s, nq)
            st_ref[...] = jnp.zeros(st_ref.shape, F32)
            dsk_ref[...] = jnp.zeros(dsk_ref.shape, F32)

        dzv = dz_ref[...]
        dzb = dzv.astype(BF16)
        dzb_ref[...] = dzb
        do_s[...] = _dot_nt(dzb, _wfull(wo_s))
        kvall[pl.ds(0, BLOCK), :] = kvp_ref[...]
        kvall[pl.ds(BLOCK, tm), :] = kv_ref[...]
        lane = lax.broadcasted_iota(jnp.int32, (BLOCK, BLOCK), 1)
        lane1 = lax.broadcasted_iota(jnp.int32, (1, BLOCK), 1)
        lane2 = lax.broadcasted_iota(jnp.int32, (2 * BLOCK, BLOCK), 1)
        halves = (lane < HEAD_DIM, lane >= HEAD_DIM)

        def score_phase(j):
            rows = slice(j * BLOCK, (j + 1) * BLOCK)
            kvb = kvall[j * BLOCK:(j + 2) * BLOCK, :]
            first = (i * nbt + j == 0).astype(jnp.int32)
            pads = [_padded_kv(kvb, kvh) for kvh in range(N_KV_HEADS)]
            scores, dps, dhs, qms, doms = [], [], [], [], []
            for a in range(d // BLOCK):
                kpad, vpad = pads[a // pairs_per_kv]
                cols = slice(a * BLOCK, (a + 1) * BLOCK)
                qp = q_ref[rows, cols]
                dop = do_s[rows, cols]
                dopb = dop.astype(BF16)
                prod = dop * o_ref[rows, cols].astype(F32)
                for e in range(2):
                    scores.append(_dot_nt(qp, kpad[e]) - bias_s[first, 2 * a + e])
                    dps.append(_dot_nt(dopb, vpad[e]))
                    dhs.append(jnp.sum(jnp.where(halves[e], prod, 0.0), axis=-1, keepdims=True))
                    qms.append(jnp.where(halves[e], qp, jnp.zeros_like(qp)))
                    doms.append(jnp.where(halves[e], dopb, jnp.zeros_like(dopb)))
            return rows, pads, scores, dps, dhs, qms, doms

        def softmax_phase(state):
            rows, pads, scores, dps, dhs, qms, doms = state
            dss, pbs = [], []
            dsk_t = jnp.zeros((1, BLOCK), F32)
            for h in range(nq):
                lse_h = lse_ref[rows, h:h + 1]
                p = jnp.exp(scores[h] - lse_h)
                dss.append((p * (dps[h] - dhs[h])).astype(BF16))
                pbs.append(p.astype(BF16))
                dsink = -jnp.sum(jnp.exp(sk_ref[:, h:h + 1] - lse_h) * dhs[h], axis=0, keepdims=True)
                dsk_t = jnp.where(lane1 == h, dsink, dsk_t)
            dsk_ref[...] += dsk_t
            return rows, pads, dss, pbs, qms, doms

        def grad_phase(state):
            rows, pads, dss, pbs, qms, doms = state
            dsecs = []
            for kvh in range(N_KV_HEADS):
                kpad = pads[kvh][0]
                dk_acc = jnp.zeros((2 * BLOCK, BLOCK), F32)
                dv_acc = jnp.zeros((2 * BLOCK, BLOCK), F32)
                for a in range(kvh * pairs_per_kv, (kvh + 1) * pairs_per_kv):
                    dqp = _dot(dss[2 * a], kpad[0]) + _dot(dss[2 * a + 1], kpad[1])
                    dq_s[rows, a * BLOCK:(a + 1) * BLOCK] = dqp * scale
                    for e in range(2):
                        h = 2 * a + e
                        dk_acc = dk_acc + _dot_tn(dss[h], qms[h])
                        dv_acc = dv_acc + _dot_tn(pbs[h], doms[h])
                dsecs.append((dk_acc + pltpu.roll(dk_acc, HEAD_DIM, 1), dv_acc + pltpu.roll(dv_acc, HEAD_DIM, 1)))
            lo = lane2 < HEAD_DIM
            dkv = jnp.concatenate([jnp.where(lo, dsecs[0][0], dsecs[1][0]),
                                   jnp.where(lo, dsecs[0][1], dsecs[1][1])], axis=1)
            dkp_ref[rows, :] = dkv[:BLOCK]
            dkc_ref[rows, :] = dkv[BLOCK:]

        for state in [softmax_phase(s) for s in [score_phase(j) for j in range(nbt)]]:
            grad_phase(state)
        dqv = dq_s[...]
        dq_ref[...] = dqv.astype(BF16)
        st_ref[0:1, :] += jnp.sum(dqv, axis=0, keepdims=True)
        st_ref[1:2, :] += jnp.sum(dzv, axis=0, keepdims=True)

    return _gridded_call(
        body, "attn_bwd", t // tm,
        [_tile(tm, d), _tile(tm, d), _tile(tm, d), _tile(tm, BLOCK), _tile(tm, 2 * BLOCK),
         pl.BlockSpec((BLOCK, 2 * BLOCK), lambda i: (jnp.maximum(i * nbt - 1, 0), 0)),
         ANY, _row(nq)],
        [_tile(tm, d), _tile(tm, d), _tile(tm, 2 * BLOCK), _tile(tm, 2 * BLOCK),
         _fixed(8, d), _row(BLOCK)],
        [jax.ShapeDtypeStruct((t, d), BF16), jax.ShapeDtypeStruct((t, d), BF16),
         jax.ShapeDtypeStruct((t, 2 * BLOCK), F32), jax.ShapeDtypeStruct((t, 2 * BLOCK), F32),
         jax.ShapeDtypeStruct((8, d), F32), jax.ShapeDtypeStruct((1, BLOCK), F32)],
        [_wscratch(lay, "wo"), pltpu.VMEM((BLOCK + tm, 2 * BLOCK), BF16),
         pltpu.VMEM((tm, d), F32), pltpu.VMEM((tm, d), F32),
         pltpu.VMEM((2, nq, BLOCK, 2 * BLOCK), F32)],
        (dz, q, o, lse, kvs, kvs, wb, sinks), hosted)


def _tn_matmul(a, b, name, bm, tk):
    t, m = a.shape
    n = b.shape[1]
    ksteps = t // tk

    nc = max(n // 256, 1)
    cw = n // nc

    def body(a_ref, b_ref, o_ref, acc):
        k = pl.program_id(1)

        @pl.when(k == 0)
        def _():
            acc[...] = jnp.zeros(acc.shape, F32)

        at = a_ref[...].T
        for c in range(nc):
            cols = slice(c * cw, (c + 1) * cw)
            acc[:, cols] += _dot(at, b_ref[:, cols])

        @pl.when(k == ksteps - 1)
        def _():
            o_ref[...] = acc[...].astype(BF16)

    return pl.pallas_call(
        body, name=name, grid=(m // bm, ksteps),
        in_specs=[pl.BlockSpec((tk, bm), lambda j, k: (k, j)), pl.BlockSpec((tk, n), lambda j, k: (k, 0))],
        out_specs=pl.BlockSpec((bm, n), lambda j, k: (j, 0)),
        out_shape=jax.ShapeDtypeStruct((m, n), BF16),
        scratch_shapes=[pltpu.VMEM((bm, n), F32)],
        compiler_params=pltpu.CompilerParams(dimension_semantics=("arbitrary", "arbitrary"),
                                             vmem_limit_bytes=VMEM_LIMIT),
    )(a, b)


def _all_gather(arrays, name):
    n = len(arrays)

    def body(*refs):
        ins, outs = refs[:n], refs[n:2 * n]
        send_sems, recv_sems, local_sems = refs[2 * n:]
        x, y, c = _me()
        me, sibling = (x, y, c), (x, y, 1 - c)
        chips = [(1 - x, y), (x, 1 - y), (1 - x, 1 - y)]

        def slot(ref, dev):
            return ref.at[4 * dev[0] + 2 * dev[1] + dev[2]]

        def copy(a, k, block, to, src=None):
            return pltpu.make_async_remote_copy(
                src_ref=slot(outs[a], block) if src is None else src, dst_ref=slot(outs[a], block),
                send_sem=send_sems.at[a, k], recv_sem=recv_sems.at[a, k], device_id=to, device_id_type=MESH)

        mine = [pltpu.make_async_copy(ins[a], slot(outs[a], me), local_sems.at[a]) for a in range(n)]
        for cp in mine:
            cp.start()
        first = []
        for a in range(n):
            first.append(copy(a, 0, me, sibling, src=ins[a]))
            first += [copy(a, 1 + j, me, (*chip, c), src=ins[a]) for j, chip in enumerate(chips)]
        for cp in first:
            cp.start()
        passed = []
        for a in range(n):
            for j, chip in enumerate(chips):
                copy(a, 1 + j, (*chip, c), me).wait_recv()
                cp = copy(a, 4 + j, (*chip, c), sibling)
                cp.start()
                passed.append(cp)
        for a in range(n):
            copy(a, 0, sibling, me).wait_recv()
            for j, chip in enumerate(chips):
                copy(a, 4 + j, (*chip, 1 - c), me).wait_recv()
        for cp in first + passed:
            cp.wait_send()
        for cp in mine:
            cp.wait()

    return pl.pallas_call(
        body, name=name, in_specs=[ANY] * n, out_specs=[ANY] * n,
        out_shape=[jax.ShapeDtypeStruct((N_DEV,) + a.shape, a.dtype) for a in arrays],
        scratch_shapes=[pltpu.SemaphoreType.DMA((n, 7)), pltpu.SemaphoreType.DMA((n, 7)),
                        pltpu.SemaphoreType.DMA((n,))],
    )(*arrays)


def _exchange(arrays, name):
    n = len(arrays)
    blocked = [a.ndim == 3 for a in arrays]

    def body(*refs):
        ins, outs = refs[:n], refs[n:2 * n]
        send_sems, recv_sems, local_sems = refs[2 * n:]
        me = _index(_me())

        def src(k, dev):
            return ins[k].at[dev] if blocked[k] else ins[k]

        local = [pltpu.make_async_copy(src(k, me), outs[k].at[me], local_sems.at[k]) for k in range(n)]
        sends, arrivals = [], []
        for k in range(n):
            for mask in range(1, N_DEV):
                peer = _peer(mask)
                sends.append(pltpu.make_async_remote_copy(
                    src_ref=src(k, _index(peer)), dst_ref=outs[k].at[me], send_sem=send_sems.at[k, mask - 1],
                    recv_sem=recv_sems.at[k, mask - 1], device_id=peer, device_id_type=MESH))
                arrivals.append(pltpu.make_async_remote_copy(
                    src_ref=src(k, me), dst_ref=outs[k].at[_index(peer)], send_sem=send_sems.at[k, mask - 1],
                    recv_sem=recv_sems.at[k, mask - 1], device_id=_me(), device_id_type=MESH))
        for cp in local + sends:
            cp.start()
        for cp in arrivals:
            cp.wait_recv()
        for cp in sends:
            cp.wait_send()
        for cp in local:
            cp.wait()

    return pl.pallas_call(
        body, name=name, in_specs=[ANY] * n, out_specs=[ANY] * n,
        out_shape=[jax.ShapeDtypeStruct((N_DEV,) + a.shape[-2:], a.dtype) for a in arrays],
        scratch_shapes=[pltpu.SemaphoreType.DMA((n, 7)), pltpu.SemaphoreType.DMA((n, 7)),
                        pltpu.SemaphoreType.DMA((n,))],
    )(*arrays)


def _adamw_update(g, w_ref, m_ref, v_ref, go_ref, d_ref, mo_ref, vo_ref):
    mn = ADAM_B1 * m_ref[...] + (1.0 - ADAM_B1) * g
    vn = ADAM_B2 * v_ref[...] + (1.0 - ADAM_B2) * (g * g)
    m_hat = mn / (1.0 - ADAM_B1 ** ADAM_STEP)
    v_hat = vn / (1.0 - ADAM_B2 ** ADAM_STEP)
    go_ref[...] = g
    d_ref[...] = -ADAM_LR * (m_hat / (jnp.sqrt(v_hat) + ADAM_EPS) + ADAM_WD * w_ref[...])
    mo_ref[...] = mn
    vo_ref[...] = vn


def _sum_sources(g_refs, layer):
    total = None
    for l, g_ref in enumerate(g_refs):
        g = g_ref[0].astype(F32)
        for s in range(1, N_DEV):
            g = g + g_ref[s].astype(F32)
        total = g if total is None else jnp.where(layer == l, g, total)
    return total


def _layer_block(l_mine, nblocks):
    def index(l, j):
        return (0, jnp.where(l == l_mine, j, jnp.where(l < l_mine, 0, nblocks - 1)), 0)
    return index


def _adamw_sum(g8, w, m, v, name, tr):
    r, width = w.shape

    def body(g_ref, *refs):
        _adamw_update(_sum_sources([g_ref], 0), *refs)

    spec = pl.BlockSpec((tr, width), lambda i: (i, 0))
    return pl.pallas_call(
        body, name=name, grid=(r // tr,),
        in_specs=[pl.BlockSpec((N_DEV, tr, width), lambda i: (0, i, 0)), spec, spec, spec],
        out_specs=[spec] * 4, out_shape=[jax.ShapeDtypeStruct((r, width), F32)] * 4,
        compiler_params=_params(),
    )(g8, w, m, v)


def _adamw_rows(g8s, w, m, v, name):
    layers, n, width = w.shape
    tr = max(r for r in range(16, 177, 16) if n % r == 0)
    nb = n // tr

    def body(*refs):
        _adamw_update(_sum_sources(refs[:layers], pl.program_id(0)), *refs[layers:])

    spec = pl.BlockSpec((None, tr, width), lambda l, j: (l, j, 0))
    return pl.pallas_call(
        body, name=name, grid=(layers, nb),
        in_specs=[pl.BlockSpec((N_DEV, tr, width), _layer_block(l, nb)) for l in range(layers)] + [spec] * 3,
        out_specs=[spec] * 4, out_shape=[jax.ShapeDtypeStruct(w.shape, F32)] * 4,
        compiler_params=pltpu.CompilerParams(dimension_semantics=("arbitrary", "arbitrary"),
                                             vmem_limit_bytes=VMEM_LIMIT),
    )(*g8s, w, m, v)


def _adamw_cols(g8s, w, m, v, name):
    layers, k, n = w.shape
    cb = min(BLOCK, n)
    nb = pl.cdiv(n, cb)

    def body(*refs):
        _adamw_update(_sum_sources(refs[:layers], pl.program_id(0)).T, *refs[layers:])

    spec = pl.BlockSpec((None, k, cb), lambda l, j: (l, 0, j))
    return pl.pallas_call(
        body, name=name, grid=(layers, nb),
        in_specs=[pl.BlockSpec((N_DEV, cb, k), _layer_block(l, nb)) for l in range(layers)] + [spec] * 3,
        out_specs=[spec] * 4, out_shape=[jax.ShapeDtypeStruct(w.shape, F32)] * 4,
        compiler_params=pltpu.CompilerParams(dimension_semantics=("arbitrary", "arbitrary"),
                                             vmem_limit_bytes=VMEM_LIMIT),
    )(*g8s, w, m, v)


def _local_step(x, target, wba, shard_b, lay, sm, tm, tk):
    t, d = x.shape
    f = lay.f
    w_dw32 = jnp.concatenate([sm["w_dw"], jnp.zeros((HALO - CONV_WIDTH, d), F32)], axis=0)
    lmg, lmb, lfg, lfb = sm["ln_mix_g"], sm["ln_mix_b"], sm["ln_ffn_g"], sm["ln_ffn_b"]
    bkv = jnp.concatenate([sm["b_k"], sm["b_v"]], axis=1)
    bm_f = f // 2 if (f // 2) % 128 == 0 else f
    tm_light = 2 * tm

    received = {}

    def exchange(grads):
        return _HostedExchange([g.reshape(N_DEV, lay.n[n], d) for n, g in grads.items()])

    def keep(grads, arrived):
        received.update(zip(grads, arrived))

    xb0, ag, xhc, rsc, xh1, rs1, wbb = _conv_fwd(x, wba, lay, w_dw32, sm["b_pw1"], sm["b_dw"], sm["cg"],
                                                 sm["cb"], sm["b_pw2"], lmg[0:1], lmb[0:1], tm,
                                                 hosted=_HostedGather(shard_b))
    wkv = wbb[:, lay.goff["wkv"]:lay.goff["wkv"] + lay.n["wkv"], :].reshape(d, 2 * BLOCK)
    x1b, hg0, hu0, xh2, rs2, x2b, kvs = _ffn_fwd(xh1, lmg[0:1], lmb[0:1], wbb, lay, 0, tm_light,
                                                kv=(lfg[0:1], lfb[0:1], wkv, bkv))
    q, o, lse, xh3, rs3 = _attn_fwd(xh2, lfg[0:1], lfb[0:1], x2b, kvs, wbb, lay, sm["b_q"], sm["sinks"],
                                    sm["b_o"], tm)
    x3b, hg1, hu1, dz4, st4, loss = _ffn_fwd(xh3, lmg[1:2], lmb[1:2], wbb, lay, 1, tm,
                                             loss=(lfg[1:2], lfb[1:2], target))

    dz4b, act1, dhg1, dhu1, dz3, st3 = _ffn_bwd(dz4, hg1, hu1, xh3, rs3, lmg[1:2], wbb, lay, 1, tm)
    g1 = {"gt1": _tn_matmul(dhg1, x3b, "dw_gate1", bm_f, tk), "ut1": _tn_matmul(dhu1, x3b, "dw_up1", bm_f, tk),
          "dn1": _tn_matmul(act1, dz4b, "dw_down1", bm_f, tk)}
    dz3b, dq, dkc, dkp, stq, dsinks, *arrived = _attn_bwd(dz3, q, o, lse, kvs, wbb, lay, sm["sinks"], tm,
                                                          hosted=exchange(g1))
    keep(g1, arrived)
    g2 = {"wq": _tn_matmul(x2b, dq, "dw_q", d, tk), "wo": _tn_matmul(o, dz3b, "dw_o", d, tk)}
    dz2b, act0, dhg0, dhu0, dz1, st1, dkv, st2, dbkv, *arrived = _ffn_bwd(
        dz3, hg0, hu0, xh1, rs1, lmg[0:1], wbb, lay, 0, tm, hosted=exchange(g2),
        qkv=(dq, dkc, dkp, xh2, rs2, lfg[0:1], wkv))
    keep(g2, arrived)
    dz1b, s_act, dcv, stc = _conv_bwd1(dz1, xhc, rsc, wba, lay, sm["cg"], sm["cb"], tm_light)
    g3 = {"gt0": _tn_matmul(dhg0, x1b, "dw_gate0", bm_f, tk), "ut0": _tn_matmul(dhu0, x1b, "dw_up0", bm_f, tk),
          "dn0": _tn_matmul(act0, dz2b, "dw_down0", bm_f, tk), "pw2": _tn_matmul(s_act, dz1b, "dw_pw2", d, tk),
          "wkv": _tn_matmul(x2b, dkv, "dw_kv", d, tk)}
    grad_x, dh1, dwdw, db1, *arrived = _conv_bwd2(dz1, dcv, ag, wba, lay, w_dw32, tm, hosted=exchange(g3))
    keep(g3, arrived)
    g_pw1t = _tn_matmul(dh1, xb0, "dw_pw1", d, tk)
    small = {
        "w_dw": dwdw[:CONV_WIDTH], "b_pw1": db1, "b_dw": stc[2:3], "cg": stc[0:1], "cb": stc[1:2],
        "b_pw2": stc[3:4], "b_k": dbkv[:, :BLOCK], "b_v": dbkv[:, BLOCK:], "b_q": stq[0:1],
        "sinks": dsinks[:, :d // HEAD_DIM],
        "b_o": stq[1:2],
        "ln_mix_g": jnp.concatenate([st1[0:1], st3[0:1]], axis=0),
        "ln_mix_b": jnp.concatenate([st1[1:2], st3[1:2]], axis=0),
        "ln_ffn_g": jnp.concatenate([st2[0:1], st4[0:1]], axis=0),
        "ln_ffn_b": jnp.concatenate([st2[1:2], st4[1:2]], axis=0),
    }
    return loss[0, 0], grad_x, received, g_pw1t, small


SP_ROWS = 40
SP_BDW, SP_CG, SP_CB, SP_BPW2, SP_BPW1 = 32, 33, 34, 35, 36
RP_NAMES = ("ln_mix_g", "ln_mix_b", "ln_ffn_g", "ln_ffn_b", "b_q", "b_o", "b_k", "b_v", "sinks")


def _row_forms(d, pw1, pw2, wq, wo, gate, up, down, wk, wv):
    rf = {"pw1t": pw1[0].T, "pw2": pw2[0], "wq": wq[0], "wo": wo[0],
          "wkv": jnp.concatenate([wk, wv], axis=1).reshape(-1, d)}
    for l in range(DEPTH):
        rf.update({f"gt{l}": gate[l].T, f"ut{l}": up[l].T, f"dn{l}": down[l]})
    return rf


def _pack_rows(rf, names):
    return jnp.concatenate([rf[n] for n in names], axis=0)


def _pack_small(w_dw, b_dw, cg, cb, b_pw2, b_pw1):
    cw = b_dw.shape[1]
    z = jnp.zeros((1, cw), F32)
    return jnp.concatenate([w_dw[0], z, b_dw, cg, cb, b_pw2, b_pw1.reshape(2, cw), z, z], axis=0)


def _unpack_small(p):
    cw = p.shape[1]
    return dict(w_dw=p[None, :CONV_WIDTH], b_dw=p[SP_BDW:SP_BDW + 1], cg=p[SP_CG:SP_CG + 1],
                cb=p[SP_CB:SP_CB + 1], b_pw2=p[SP_BPW2:SP_BPW2 + 1],
                b_pw1=p[SP_BPW1:SP_BPW1 + 2].reshape(1, 2 * cw))


def _small_full(g):
    d = N_DEV * g.shape[2]

    def wide(r0, n=1):
        return jnp.transpose(g[:, r0:r0 + n], (1, 0, 2)).reshape(n, d)

    return dict(w_dw=wide(0, CONV_WIDTH), b_dw=wide(SP_BDW), cg=wide(SP_CG), cb=wide(SP_CB),
                b_pw2=wide(SP_BPW2), b_pw1=g[:, SP_BPW1:SP_BPW1 + 2].reshape(1, 2 * d))


def _small_grad_blocks(sg):
    cw = sg["b_dw"].shape[1] // N_DEV

    def narrow(a):
        return jnp.transpose(a.reshape(a.shape[0], N_DEV, cw), (1, 0, 2))

    z = jnp.zeros((N_DEV, 1, cw), F32)
    return jnp.concatenate([narrow(sg["w_dw"]), z, narrow(sg["b_dw"]), narrow(sg["cg"]), narrow(sg["cb"]),
                            narrow(sg["b_pw2"]), sg["b_pw1"].reshape(N_DEV, 2, cw), z, z], axis=1)


def _pack_rep(vals, rider=0.0):
    parts = []
    for name in RP_NAMES:
        a = vals[name].reshape(-1)
        pad = -a.shape[0] % 128
        parts.append(jnp.concatenate([a, jnp.zeros((pad,), F32)]).reshape(-1, 128))
    parts.append(jnp.full((1, 128), rider, F32))
    rows = sum(p.shape[0] for p in parts)
    parts.append(jnp.zeros((-rows % 8, 128), F32))
    return jnp.concatenate(parts, axis=0)


def _rider_row(shapes):
    return sum(-(-_size(shapes[name]) // 128) for name in RP_NAMES)


def _size(shape):
    n = 1
    for s in shape:
        n *= s
    return n


def _unpack_rep(p, shapes):
    out, r = {}, 0
    for name in RP_NAMES:
        n = _size(shapes[name])
        rows = -(-n // 128)
        out[name] = p[r:r + rows].reshape(-1)[:n].reshape(shapes[name])
        r += rows
    return out


def kernel(x, conv_w_pw1, conv_b_pw1, conv_w_dw, conv_b_dw, conv_ln_g, conv_ln_b, conv_w_pw2, conv_b_pw2, kv_w_k, kv_b_k, kv_w_v, kv_b_v, attn_w_q, attn_b_q, attn_sinks, attn_w_o, attn_b_o, ffn_w_gate, ffn_w_up, ffn_w_down, ln_mix_g, ln_mix_b, ln_ffn_g, ln_ffn_b, loss_target, m_conv_w_pw1, m_conv_b_pw1, m_conv_w_dw, m_conv_b_dw, m_conv_ln_g, m_conv_ln_b, m_conv_w_pw2, m_conv_b_pw2, m_kv_w_k, m_kv_b_k, m_kv_w_v, m_kv_b_v, m_attn_w_q, m_attn_b_q, m_attn_sinks, m_attn_w_o, m_attn_b_o, m_ffn_w_gate, m_ffn_w_up, m_ffn_w_down, m_ln_mix_g, m_ln_mix_b, m_ln_ffn_g, m_ln_ffn_b, v_conv_w_pw1, v_conv_b_pw1, v_conv_w_dw, v_conv_b_dw, v_conv_ln_g, v_conv_ln_b, v_conv_w_pw2, v_conv_b_pw2, v_kv_w_k, v_kv_b_k, v_kv_w_v, v_kv_b_v, v_attn_w_q, v_attn_b_q, v_attn_sinks, v_attn_w_o, v_attn_b_o, v_ffn_w_gate, v_ffn_w_up, v_ffn_w_down, v_ln_mix_g, v_ln_mix_b, v_ln_ffn_g, v_ln_ffn_b):
    t, d = x.shape[1], x.shape[2]
    f = ffn_w_gate.shape[2] * N_DEV
    lay = _Layout(d, f)
    tm, tk = 256, min(2048, t)

    rep_shapes = dict(ln_mix_g=ln_mix_g.shape, ln_mix_b=ln_mix_b.shape, ln_ffn_g=ln_ffn_g.shape,
                      ln_ffn_b=ln_ffn_b.shape, b_q=attn_b_q.shape, b_o=attn_b_o.shape, b_k=kv_b_k.shape,
                      b_v=kv_b_v.shape, sinks=attn_sinks.shape)

    def rep_pack(lmg, lmb, lfg, lfb, bq, bo, bk, bv, sk):
        return _pack_rep(dict(ln_mix_g=lmg, ln_mix_b=lmb, ln_ffn_g=lfg, ln_ffn_b=lfb, b_q=bq, b_o=bo,
                              b_k=bk, b_v=bv, sinks=sk))

    w_rf = _row_forms(d, conv_w_pw1, conv_w_pw2, attn_w_q, attn_w_o, ffn_w_gate, ffn_w_up, ffn_w_down, kv_w_k, kv_w_v)
    w_small = _pack_small(conv_w_dw, conv_b_dw, conv_ln_g, conv_ln_b, conv_b_pw2, conv_b_pw1)
    m_small = _pack_small(m_conv_w_dw, m_conv_b_dw, m_conv_ln_g, m_conv_ln_b, m_conv_b_pw2, m_conv_b_pw1)
    v_small = _pack_small(v_conv_w_dw, v_conv_b_dw, v_conv_ln_g, v_conv_ln_b, v_conv_b_pw2, v_conv_b_pw1)
    w_rep = rep_pack(ln_mix_g, ln_mix_b, ln_ffn_g, ln_ffn_b, attn_b_q, attn_b_o, kv_b_k, kv_b_v, attn_sinks)
    m_rep = rep_pack(m_ln_mix_g, m_ln_mix_b, m_ln_ffn_g, m_ln_ffn_b, m_attn_b_q, m_attn_b_o, m_kv_b_k, m_kv_b_v, m_attn_sinks)
    v_rep = rep_pack(v_ln_mix_g, v_ln_mix_b, v_ln_ffn_g, v_ln_ffn_b, v_attn_b_q, v_attn_b_o, v_kv_b_k, v_kv_b_v, v_attn_sinks)

    wba, smg = _all_gather([_pack_rows(w_rf, lay.GATHER["a"]).astype(BF16), w_small], "gather_conv_weights")
    shard_b = _pack_rows(w_rf, lay.GATHER["b"]).astype(BF16)
    sm = _small_full(smg)
    sm.update(ln_mix_g=ln_mix_g, ln_mix_b=ln_mix_b, ln_ffn_g=ln_ffn_g, ln_ffn_b=ln_ffn_b, b_q=attn_b_q,
              b_o=attn_b_o, sinks=attn_sinks, b_k=kv_b_k.reshape(1, -1), b_v=kv_b_v.reshape(1, -1))

    loss_part, grad_x, received, g_pw1t, gsmall = _local_step(x[0], loss_target[0], wba, shard_b, lay, sm, tm, tk)

    received["pw1t"], g8_small, g8_rep = _exchange(
        [g_pw1t.reshape(N_DEV, lay.n["pw1t"], d), _small_grad_blocks(gsmall), _pack_rep(gsmall, loss_part)],
        "exchange_last_grads")

    def kv_rows(wk, wv):
        return jnp.concatenate([wk, wv], axis=1).reshape(1, -1, d)

    def kv_split(a):
        a = a.reshape(d // N_DEV, 2 * BLOCK)
        return a[:, :BLOCK], a[:, BLOCK:]

    big = dict(
        pw1=_adamw_cols([received["pw1t"]], conv_w_pw1, m_conv_w_pw1, v_conv_w_pw1, "adamw_pw1"),
        gate=_adamw_cols([received["gt0"], received["gt1"]], ffn_w_gate, m_ffn_w_gate, v_ffn_w_gate, "adamw_gate"),
        up=_adamw_cols([received["ut0"], received["ut1"]], ffn_w_up, m_ffn_w_up, v_ffn_w_up, "adamw_up"),
        down=_adamw_rows([received["dn0"], received["dn1"]], ffn_w_down, m_ffn_w_down, v_ffn_w_down, "adamw_down"),
        pw2=_adamw_rows([received["pw2"]], conv_w_pw2, m_conv_w_pw2, v_conv_w_pw2, "adamw_pw2"),
        wq=_adamw_rows([received["wq"]], attn_w_q, m_attn_w_q, v_attn_w_q, "adamw_q"),
        wo=_adamw_rows([received["wo"]], attn_w_o, m_attn_w_o, v_attn_w_o, "adamw_o"),
        wkv=[kv_split(a) for a in _adamw_rows([received["wkv"]], kv_rows(kv_w_k, kv_w_v), kv_rows(m_kv_w_k, m_kv_w_v),
                                              kv_rows(v_kv_w_k, v_kv_w_v), "adamw_kv")])
    big_out = [dict(pw1=big["pw1"][i], pw2=big["pw2"][i], wq=big["wq"][i], wo=big["wo"][i], gate=big["gate"][i],
                    up=big["up"][i], down=big["down"][i], wk=big["wkv"][i][0], wv=big["wkv"][i][1])
               for i in range(4)]
    small_out = [_unpack_small(a) for a in _adamw_sum(g8_small, w_small, m_small, v_small, "adamw_small", SP_ROWS)]
    rep_res = _adamw_sum(g8_rep, w_rep, m_rep, v_rep, "adamw_rep", w_rep.shape[0])
    rep_out = [_unpack_rep(a, rep_shapes) for a in rep_res]
    loss = rep_res[0][_rider_row(rep_shapes), 0]

    outs = [loss, grad_x[None]]
    for b, s, r in zip(big_out, small_out, rep_out):
        outs += [b["pw1"], s["b_pw1"], s["w_dw"], s["b_dw"], s["cg"], s["cb"], b["pw2"], s["b_pw2"],
                 b["wk"], r["b_k"], b["wv"], r["b_v"], b["wq"], r["b_q"], r["sinks"], b["wo"], r["b_o"],
                 b["gate"], b["up"], b["down"], r["ln_mix_g"], r["ln_mix_b"], r["ln_ffn_g"], r["ln_ffn_b"]]
    return tuple(outs)
```

```python
import jax
import jax.numpy as jnp
from jax import lax
from jax.experimental import pallas as pl
from jax.experimental.pallas import tpu as pltpu

F32 = jnp.float32
BF16 = jnp.bfloat16

N_DEV = 8
HEAD_DIM = 64
N_KV_HEADS = 2
BLOCK = 128
CONV_WIDTH = 31
HALO = 32
ALIBI_MAX = 8.0
DEPTH = 2
ALPHA = (2.0 * DEPTH) ** 0.25
LN_EPS = 1e-5
MASKED_DIST = 1e32
ADAM_LR = 0.001
ADAM_B1 = 0.9
ADAM_B2 = 0.999
ADAM_EPS = 1e-08
ADAM_WD = 0.01
ADAM_STEP = 10
VMEM_LIMIT = 56 * 1024 * 1024
BF16_SUBLANES = 16
ADAMW_MAX_ROWS = 176
MESH = pl.DeviceIdType.MESH


def _dot(a, b):
    return jnp.dot(a, b, preferred_element_type=F32)


def _dot_nt(a, b):
    return lax.dot_general(a, b, (((1,), (1,)), ((), ())), preferred_element_type=F32)


def _dot_tn(a, b):
    return lax.dot_general(a, b, (((0,), (0,)), ((), ())), preferred_element_type=F32)


def _sigmoid(v):
    return 1.0 / (1.0 + jnp.exp(-v))


def _ln_fwd(z):
    mu = jnp.mean(z, axis=-1, keepdims=True)
    zc = z - mu
    var = jnp.mean(zc * zc, axis=-1, keepdims=True)
    rstd = lax.rsqrt(var + LN_EPS)
    return zc * rstd, rstd


def _ln_bwd(dout, xh, rstd, g):
    dxh = dout * g
    m1 = jnp.mean(dxh, axis=-1, keepdims=True)
    m2 = jnp.mean(dxh * xh, axis=-1, keepdims=True)
    dz = rstd * (dxh - m1 - xh * m2)
    return dz, jnp.sum(dout * xh, axis=0, keepdims=True), jnp.sum(dout, axis=0, keepdims=True)


def _params(vmem=VMEM_LIMIT):
    return pltpu.CompilerParams(dimension_semantics=("arbitrary",), vmem_limit_bytes=vmem)


def _row(d):
    return pl.BlockSpec((1, d), lambda i: (0, 0))


def _tile(tm, d):
    return pl.BlockSpec((tm, d), lambda i: (i, 0))


def _fixed(r, d):
    return pl.BlockSpec((r, d), lambda i: (0, 0))


def _tile_cur(tm, d, nsteps):
    return pl.BlockSpec((tm, d), lambda i: (jnp.minimum(i, nsteps - 1), 0))


def _tile_prev(tm, d):
    return pl.BlockSpec((tm, d), lambda i: (jnp.maximum(i - 1, 0), 0))


ANY = pl.BlockSpec(memory_space=pl.ANY)


class _Layout:
    GATHER = {"a": ("pw1t", "pw2"),
              "b": ("wq", "wo", "gt0", "ut0", "dn0", "gt1", "ut1", "dn1", "wkv")}

    def __init__(self, d, f):
        self.d, self.f = d, f
        self.n = {"pw1t": 2 * d // N_DEV, "pw2": d // N_DEV, "wq": d // N_DEV, "wo": d // N_DEV,
                  "wkv": (d // N_DEV) * 2 * BLOCK // d}
        for l in range(DEPTH):
            self.n.update({f"gt{l}": f // N_DEV, f"ut{l}": f // N_DEV, f"dn{l}": f // N_DEV})
        self.goff = {}
        for names in self.GATHER.values():
            r = 0
            for name in names:
                self.goff[name] = r
                r += self.n[name]


def _load_weight(wb_ref, lay, name, dst):
    n = lay.n[name]
    for p in range(N_DEV):
        pltpu.sync_copy(wb_ref.at[p, pl.ds(lay.goff[name], n), :], dst.at[pl.ds(p * n, n), :])


def _wscratch(lay, name):
    return pltpu.VMEM((N_DEV * lay.n[name], lay.d), BF16)


def _wfull(ref):
    return ref[...]


def _wrows(ref, r0, nrows):
    return ref[r0:r0 + nrows, :]


def _me():
    return lax.axis_index("x"), lax.axis_index("y"), lax.axis_index("c")


def _peer(mask):
    x, y, c = _me()
    return (1 - x if mask & 4 else x, 1 - y if mask & 2 else y, 1 - c if mask & 1 else c)


def _index(dev):
    return 4 * dev[0] + 2 * dev[1] + dev[2]


class _HostedGather:
    def __init__(self, array):
        self.arrays = [array]
        self.out_shapes = [jax.ShapeDtypeStruct((N_DEV,) + array.shape, array.dtype)]

    def scratch(self):
        return [pltpu.SemaphoreType.DMA((7,)), pltpu.SemaphoreType.DMA((7,)), pltpu.SemaphoreType.DMA(())]

    def _copies(self, ins, outs, send_sems, recv_sems, local_sem):
        out = outs[0]
        x, y, c = _me()
        me, sibling = (x, y, c), (x, y, 1 - c)
        chips = [(1 - x, y), (x, 1 - y), (1 - x, 1 - y)]

        def copy(k, block, to, src=None):
            rows = out.at[_index(block)]
            return pltpu.make_async_remote_copy(
                src_ref=rows if src is None else src, dst_ref=rows, send_sem=send_sems.at[k],
                recv_sem=recv_sems.at[k], device_id=to, device_id_type=MESH)

        return dict(
            mine=lambda: pltpu.make_async_copy(ins[0], out.at[_index(me)], local_sem),
            first=lambda: [copy(0, me, sibling, src=ins[0])] + [copy(1 + j, me, (*chip, c), src=ins[0])
                                                                for j, chip in enumerate(chips)],
            over_ici=lambda: [copy(1 + j, (*chip, c), me) for j, chip in enumerate(chips)],
            passed=lambda: [copy(4 + j, (*chip, c), sibling) for j, chip in enumerate(chips)],
            from_sibling=lambda: [copy(0, sibling, me)] + [copy(4 + j, (*chip, 1 - c), me)
                                                           for j, chip in enumerate(chips)])

    def start(self, *refs):
        cp = self._copies(*refs)
        cp["mine"]().start()
        for c in cp["first"]():
            c.start()

    def middle(self, *refs):
        cp = self._copies(*refs)
        for arrived, onward in zip(cp["over_ici"](), cp["passed"]()):
            arrived.wait_recv()
            onward.start()

    def finish(self, *refs):
        cp = self._copies(*refs)
        for c in cp["from_sibling"]():
            c.wait_recv()
        for c in cp["first"]() + cp["passed"]():
            c.wait_send()
        cp["mine"]().wait()


class _HostedExchange:
    def __init__(self, arrays):
        self.arrays = list(arrays)
        self.out_shapes = [jax.ShapeDtypeStruct(a.shape, a.dtype) for a in self.arrays]

    def scratch(self):
        n = len(self.arrays)
        return [pltpu.SemaphoreType.DMA((n, 7)), pltpu.SemaphoreType.DMA((n, 7)), pltpu.SemaphoreType.DMA((n,))]

    def _copies(self, ins, outs, send_sems, recv_sems, local_sems):
        me = _index(_me())

        def dst(k, src_dev):
            return outs[k].at[src_dev]

        pairs = [(k, mask) for k in range(len(self.arrays)) for mask in range(1, N_DEV)]

        def local():
            return [pltpu.make_async_copy(ins[k].at[me], dst(k, me), local_sems.at[k])
                    for k in range(len(self.arrays))]

        def sends():
            return [pltpu.make_async_remote_copy(
                src_ref=ins[k].at[_index(_peer(mask))], dst_ref=dst(k, me), send_sem=send_sems.at[k, mask - 1],
                recv_sem=recv_sems.at[k, mask - 1], device_id=_peer(mask), device_id_type=MESH)
                for k, mask in pairs]

        def arrivals():
            return [pltpu.make_async_remote_copy(
                src_ref=ins[k].at[me], dst_ref=dst(k, _index(_peer(mask))), send_sem=send_sems.at[k, mask - 1],
                recv_sem=recv_sems.at[k, mask - 1], device_id=_me(), device_id_type=MESH)
                for k, mask in pairs]

        return local, sends, arrivals

    def start(self, *refs):
        local, sends, _ = self._copies(*refs)
        for c in local() + sends():
            c.start()

    def middle(self, *refs):
        pass

    def finish(self, *refs):
        local, sends, arrivals = self._copies(*refs)
        for c in arrivals():
            c.wait_recv()
        for c in sends():
            c.wait_send()
        for c in local():
            c.wait()


def _gridded_call(body, name, nsteps, in_specs, out_specs, out_shape, scratch, args, hosted=None):
    if hosted is None:
        return pl.pallas_call(body, name=name, grid=(nsteps,), in_specs=in_specs, out_specs=out_specs,
                              out_shape=out_shape, scratch_shapes=scratch, compiler_params=_params())(*args)
    n_in, n_out, n_scr, h_in = len(in_specs), len(out_specs), len(scratch), len(hosted.arrays)
    h_out = len(hosted.out_shapes)

    def with_hosted(*refs):
        a = n_in + h_in
        b = a + n_out
        e = b + h_out + n_scr
        comm = (refs[n_in:a], refs[b:b + h_out], refs[e], refs[e + 1], refs[e + 2])
        i = pl.program_id(0)

        @pl.when(i == 0)
        def _():
            hosted.start(*comm)

        body(*refs[:n_in], *refs[a:b], *refs[b + h_out:e])

        @pl.when(i == nsteps // 2)
        def _():
            hosted.middle(*comm)

        @pl.when(i == nsteps - 1)
        def _():
            hosted.finish(*comm)

    return pl.pallas_call(
        with_hosted, name=name, grid=(nsteps,), in_specs=list(in_specs) + [ANY] * h_in,
        out_specs=list(out_specs) + [ANY] * h_out, out_shape=list(out_shape) + hosted.out_shapes,
        scratch_shapes=list(scratch) + hosted.scratch(), compiler_params=_params(),
    )(*args, *hosted.arrays)


CONV_RB = 64
CONV_LC = 128
CONV_WIN = CONV_RB + HALO + 8
CONV_MC = 256


def _shifted(win, r):
    return win if r == 0 else pltpu.roll(win, win.shape[0] - r, 0)


def _conv_fwd(x, wb, lay, w_dw, b_pw1, b_dw, cg, cb, b_pw2, lg, lb, tm, hosted=None):
    t, d = x.shape
    nsteps = t // tm

    def body(x_ref, xh_ref, wb_ref, wdw_ref, b1_ref, bdw_ref, cg_ref, cb_ref, b2_ref, lg_ref, lb_ref,
             xb_ref, ag_ref, xhc_ref, rsc_ref, xh1_ref, rs1_ref, w1_s, w2_s, ubuf, cv_s):
        i = pl.program_id(0)

        @pl.when(i == 0)
        def _():
            _load_weight(wb_ref, lay, "pw1t", w1_s)
            _load_weight(wb_ref, lay, "pw2", w2_s)
            ubuf[pl.ds(HALO + tm, 8), :] = jnp.zeros((8, d), F32)

        xv = x_ref[...]
        xb = xv.astype(BF16)
        xb_ref[...] = xb
        xcat = jnp.concatenate([xh_ref[...].astype(BF16), xb], axis=0)
        for mc in range(d // CONV_MC):
            c0 = mc * CONV_MC
            acols, gcols = slice(c0, c0 + CONV_MC), slice(d + c0, d + c0 + CONV_MC)
            ha = _dot_nt(xcat, _wrows(w1_s, c0, CONV_MC)) + b1_ref[:, acols]
            hg = _dot_nt(xcat, _wrows(w1_s, d + c0, CONV_MC)) + b1_ref[:, gcols]
            ag_ref[:, acols] = ha[HALO:].astype(BF16)
            ag_ref[:, gcols] = hg[HALO:].astype(BF16)
            u = ha * _sigmoid(hg)
            ubuf[0:HALO, acols] = jnp.where(i > 0, u[:HALO], 0.0)
            ubuf[HALO:HALO + tm, acols] = u[HALO:]
            for rb in range(tm // CONV_RB):
                t0 = rb * CONV_RB
                for lc in range(CONV_MC // CONV_LC):
                    lanes = slice(c0 + lc * CONV_LC, c0 + (lc + 1) * CONV_LC)
                    win = ubuf[t0:t0 + CONV_WIN, lanes]
                    acc = jnp.zeros((CONV_RB, CONV_LC), F32)
                    for r in range(8):
                        wr = _shifted(win, r)
                        for k in range(CONV_WIDTH):
                            s = HALO - (CONV_WIDTH - 1) + k
                            if s % 8 == r:
                                q = 8 * (s // 8)
                                acc = acc + wr[q:q + CONV_RB] * wdw_ref[k:k + 1, lanes]
                    cv_s[t0:t0 + CONV_RB, lanes] = acc
        cv = cv_s[...] + bdw_ref[...]
        xhc, rsc = _ln_fwd(cv)
        xhc_ref[...] = xhc
        rsc_ref[...] = rsc
        n = xhc * cg_ref[...] + cb_ref[...]
        s_act = n * _sigmoid(n)
        m = _dot(s_act.astype(BF16), _wfull(w2_s)) + b2_ref[...]
        xh1, rs1 = _ln_fwd(ALPHA * xv + m)
        xh1_ref[...] = xh1
        rs1_ref[...] = rs1

    hb = tm // HALO
    return _gridded_call(
        body, "conv_fwd", nsteps,
        [_tile(tm, d), pl.BlockSpec((HALO, d), lambda i: (jnp.maximum(i * hb - 1, 0), 0)), ANY,
         _fixed(HALO, d), _row(2 * d), _row(d), _row(d), _row(d), _row(d), _row(d), _row(d)],
        [_tile(tm, d), _tile(tm, 2 * d), _tile(tm, d), _tile(tm, 1), _tile(tm, d), _tile(tm, 1)],
        [jax.ShapeDtypeStruct((t, d), BF16), jax.ShapeDtypeStruct((t, 2 * d), BF16),
         jax.ShapeDtypeStruct((t, d), F32), jax.ShapeDtypeStruct((t, 1), F32),
         jax.ShapeDtypeStruct((t, d), F32), jax.ShapeDtypeStruct((t, 1), F32)],
        [_wscratch(lay, "pw1t"), _wscratch(lay, "pw2"),
         pltpu.VMEM((HALO + tm + 8, d), F32), pltpu.VMEM((tm, d), F32)],
        (x, x, wb, w_dw, b_pw1, b_dw, cg, cb, b_pw2, lg, lb), hosted)


def _conv_bwd1(dz1, xhc, rsc, wb, lay, cg, cb, tm):
    t, d = dz1.shape

    def body(dz_ref, xhc_ref, rsc_ref, wb_ref, cg_ref, cb_ref, dzb_ref, s_ref, dcv_ref, st_ref, w2_s):
        i = pl.program_id(0)

        @pl.when(i == 0)
        def _():
            _load_weight(wb_ref, lay, "pw2", w2_s)
            st_ref[...] = jnp.zeros(st_ref.shape, F32)

        dz = dz_ref[...]
        dzb = dz.astype(BF16)
        dzb_ref[...] = dzb
        xhc_v = xhc_ref[...]
        n = xhc_v * cg_ref[...] + cb_ref[...]
        sg = _sigmoid(n)
        s_ref[...] = (n * sg).astype(BF16)
        ds = _dot_nt(dzb, _wfull(w2_s))
        dn = ds * (sg * (1.0 + n * (1.0 - sg)))
        dcv, dg, db = _ln_bwd(dn, xhc_v, rsc_ref[...], cg_ref[...])
        dcv_ref[...] = dcv
        st_ref[0:1, :] += dg
        st_ref[1:2, :] += db
        st_ref[2:3, :] += jnp.sum(dcv, axis=0, keepdims=True)
        st_ref[3:4, :] += jnp.sum(dz, axis=0, keepdims=True)

    return pl.pallas_call(
        body, name="conv_bwd1", grid=(t // tm,),
        in_specs=[_tile(tm, d), _tile(tm, d), _tile(tm, 1), ANY, _row(d), _row(d)],
        out_specs=[_tile(tm, d), _tile(tm, d), _tile(tm, d), _fixed(8, d)],
        out_shape=[jax.ShapeDtypeStruct((t, d), BF16), jax.ShapeDtypeStruct((t, d), BF16),
                   jax.ShapeDtypeStruct((t, d), F32), jax.ShapeDtypeStruct((8, d), F32)],
        scratch_shapes=[_wscratch(lay, "pw2")],
        compiler_params=_params(),
    )(dz1, xhc, rsc, wb, cg, cb)


def _conv_bwd2(dz1, dcv, ag, wb, lay, w_dw, tm, hosted=None):
    t, d = dz1.shape
    nsteps = t // tm

    def body(dz_ref, dcv_ref, dcvn_ref, ag_ref, wb_ref, wdw_ref,
             gx_ref, dh_ref, dw_ref, db1_ref, w1_s, ubuf, dbuf, du_s, dwacc):
        i = pl.program_id(0)

        @pl.when(i == 0)
        def _():
            _load_weight(wb_ref, lay, "pw1t", w1_s)
            dbuf[pl.ds(HALO + tm, 8), :] = jnp.zeros((8, d), F32)
            dwacc[...] = jnp.zeros(dwacc.shape, F32)
            db1_ref[...] = jnp.zeros(db1_ref.shape, F32)

        dbuf[0:tm, :] = dcv_ref[...]
        dbuf[tm:tm + HALO, :] = jnp.where(i < nsteps - 1, dcvn_ref[...], 0.0)
        gx = ALPHA * dz_ref[...]
        for mc in range(d // CONV_MC):
            c0 = mc * CONV_MC
            acols, gcols = slice(c0, c0 + CONV_MC), slice(d + c0, d + c0 + CONV_MC)
            a = ag_ref[:, acols].astype(F32)
            sg = _sigmoid(ag_ref[:, gcols].astype(F32))
            ubuf[:, acols] = a * sg
            for rb in range(tm // CONV_RB):
                t0 = rb * CONV_RB
                for lc in range(CONV_MC // CONV_LC):
                    lanes = slice(c0 + lc * CONV_LC, c0 + (lc + 1) * CONV_LC)
                    dwin = dbuf[t0:t0 + CONV_WIN, lanes]
                    ucur = ubuf[t0:t0 + CONV_RB, lanes]
                    acc = jnp.zeros((CONV_RB, CONV_LC), F32)
                    for r in range(8):
                        dr = _shifted(dwin, r)
                        for k in range(CONV_WIDTH):
                            sd = CONV_WIDTH - 1 - k
                            if sd % 8 == r:
                                q = 8 * (sd // 8)
                                dk = dr[q:q + CONV_RB]
                                acc = acc + dk * wdw_ref[k:k + 1, lanes]
                                prod = ucur * dk
                                part = prod[0:8]
                                for j in range(1, CONV_RB // 8):
                                    part = part + prod[8 * j:8 * j + 8]
                                dwacc[k, :, lanes] += part
                    du_s[t0:t0 + CONV_RB, lanes] = acc
            du = du_s[:, acols]
            da = du * sg
            dg = du * a * sg * (1.0 - sg)
            dab, dgb = da.astype(BF16), dg.astype(BF16)
            dh_ref[:, acols] = dab
            dh_ref[:, gcols] = dgb
            db1_ref[:, acols] += jnp.sum(da, axis=0, keepdims=True)
            db1_ref[:, gcols] += jnp.sum(dg, axis=0, keepdims=True)
            gx = gx + _dot(dab, _wrows(w1_s, c0, CONV_MC)) + _dot(dgb, _wrows(w1_s, d + c0, CONV_MC))
        gx_ref[...] = gx

        @pl.when(i == nsteps - 1)
        def _():
            dw_ref[...] = jnp.sum(dwacc[...], axis=1)

    hb = tm // HALO
    last = t // HALO - 1
    return _gridded_call(
        body, "conv_bwd2", nsteps,
        [_tile(tm, d), _tile(tm, d),
         pl.BlockSpec((HALO, d), lambda i: (jnp.minimum((i + 1) * hb, last), 0)),
         _tile(tm, 2 * d), ANY, _fixed(HALO, d)],
        [_tile(tm, d), _tile(tm, 2 * d), _fixed(HALO, d), _row(2 * d)],
        [jax.ShapeDtypeStruct((t, d), F32), jax.ShapeDtypeStruct((t, 2 * d), BF16),
         jax.ShapeDtypeStruct((HALO, d), F32), jax.ShapeDtypeStruct((1, 2 * d), F32)],
        [_wscratch(lay, "pw1t"), pltpu.VMEM((tm, d), F32),
         pltpu.VMEM((HALO + tm + 8, d), F32), pltpu.VMEM((tm, d), F32),
         pltpu.VMEM((HALO, 8, d), F32)],
        (dz1, dcv, dcv, ag, wb, w_dw), hosted)


FFN_FC = 256
FFN_AHEAD = 1


def _ffn_fwd(xh_in, g_in, b_in, wb, lay, layer, tm, *, kv=None, loss=None):
    t, d = xh_in.shape
    f = lay.f
    names = (f"gt{layer}", f"ut{layer}", f"dn{layer}")

    def body(*refs):
        xh_ref, gi_ref, bi_ref, wb_ref = refs[:4]
        pos = 4
        if kv is not None:
            go_ref, bo_ref, wkv_ref, bkv_ref = refs[pos:pos + 4]
            pos += 4
        if loss is not None:
            go_ref, bo_ref, tgt_ref = refs[pos:pos + 3]
            pos += 3
        xb_ref, hg_ref, hu_ref = refs[pos:pos + 3]
        pos += 3
        if kv is not None:
            xho_ref, rso_ref, xob_ref, kv_ref = refs[pos:pos + 4]
            pos += 4
        if loss is not None:
            dz_ref, st_ref, loss_ref = refs[pos:pos + 3]
            pos += 3
        gt_s, ut_s, dn_s, xin_s, fo_s = refs[pos:pos + 5]
        i = pl.program_id(0)

        @pl.when(i == 0)
        def _():
            for name, dst in zip(names, (gt_s, ut_s, dn_s)):
                _load_weight(wb_ref, lay, name, dst)
            xin_s[...] = jnp.zeros(xin_s.shape, F32)
            fo_s[...] = jnp.zeros(fo_s.shape, F32)
            if loss is not None:
                st_ref[...] = jnp.zeros(st_ref.shape, F32)
                loss_ref[...] = jnp.zeros(loss_ref.shape, F32)

        xin_prev = xin_s[...]
        xho, rso = _ln_fwd(ALPHA * xin_prev + fo_s[...])
        if kv is not None:
            xho_ref[...] = xho
            rso_ref[...] = rso
            xob_ref[...] = (xho * go_ref[...] + bo_ref[...]).astype(BF16)
        if loss is not None:
            real = i > 0
            diff = xho * go_ref[...] + bo_ref[...] - tgt_ref[...]
            loss_ref[...] += jnp.where(real, (0.5 / d) * jnp.sum(diff * diff), 0.0)
            dz, dg, db = _ln_bwd(diff * (1.0 / d), xho, rso, go_ref[...])
            dz_ref[...] = dz
            st_ref[0:1, :] += jnp.where(real, dg, 0.0)
            st_ref[1:2, :] += jnp.where(real, db, 0.0)

        xin = xh_ref[...] * gi_ref[...] + bi_ref[...]
        xb = xin.astype(BF16)
        xb_ref[...] = xb

        def up(c):
            return (_dot_nt(xb, _wrows(gt_s, c * FFN_FC, FFN_FC)), _dot_nt(xb, _wrows(ut_s, c * FFN_FC, FFN_FC)))

        fo = jnp.zeros((tm, d), F32)
        nc = f // FFN_FC
        ahead = [up(c) for c in range(min(FFN_AHEAD, nc))]
        for c in range(nc):
            rows = slice(c * FFN_FC, (c + 1) * FFN_FC)
            hg, hu = ahead.pop(0)
            if c + FFN_AHEAD < nc:
                ahead.append(up(c + FFN_AHEAD))
            hg_ref[:, rows] = hg.astype(BF16)
            hu_ref[:, rows] = hu.astype(BF16)
            act = hg * _sigmoid(hg) * hu
            fo = fo + _dot(act.astype(BF16), _wrows(dn_s, c * FFN_FC, FFN_FC))
        xin_s[...] = xin
        fo_s[...] = fo
        if kv is not None:
            kv_ref[...] = (_dot(xob_ref[...], wkv_ref[...]) + bkv_ref[...]).astype(BF16)

    nsteps = t // tm
    in_specs = [_tile_cur(tm, d, nsteps), _row(d), _row(d), ANY]
    args = [xh_in, g_in, b_in, wb]
    out_specs = [_tile_cur(tm, d, nsteps), _tile_cur(tm, f, nsteps), _tile_cur(tm, f, nsteps)]
    out_shape = [jax.ShapeDtypeStruct((t, d), BF16), jax.ShapeDtypeStruct((t, f), BF16),
                 jax.ShapeDtypeStruct((t, f), BF16)]
    if kv is not None:
        in_specs += [_row(d), _row(d), _fixed(d, 2 * BLOCK), _row(2 * BLOCK)]
        args += list(kv)
        out_specs += [_tile_prev(tm, d), _tile_prev(tm, 1), _tile_prev(tm, d), _tile_prev(tm, 2 * BLOCK)]
        out_shape += [jax.ShapeDtypeStruct((t, d), F32), jax.ShapeDtypeStruct((t, 1), F32),
                      jax.ShapeDtypeStruct((t, d), BF16), jax.ShapeDtypeStruct((t, 2 * BLOCK), BF16)]
    if loss is not None:
        in_specs += [_row(d), _row(d), _tile_prev(tm, d)]
        args += list(loss)
        out_specs += [_tile_prev(tm, d), _fixed(8, d), _fixed(8, 128)]
        out_shape += [jax.ShapeDtypeStruct((t, d), F32), jax.ShapeDtypeStruct((8, d), F32),
                      jax.ShapeDtypeStruct((8, 128), F32)]
    return pl.pallas_call(
        body, name=f"ffn_fwd{layer}", grid=(nsteps + 1,), in_specs=in_specs, out_specs=out_specs,
        out_shape=out_shape,
        scratch_shapes=[_wscratch(lay, n) for n in names] + [pltpu.VMEM((tm, d), F32), pltpu.VMEM((tm, d), F32)],
        compiler_params=_params(),
    )(*args)


def _ffn_bwd(dz, hg, hu, xh_in, rs_in, g_in, wb, lay, layer, tm, hosted=None, qkv=None):
    t, d = dz.shape
    f = lay.f
    nsteps = t // tm
    nbt = tm // BLOCK
    names = (f"gt{layer}", f"ut{layer}", f"dn{layer}")

    def body(*refs):
        dz_ref, hg_ref, hu_ref, xh_ref, rs_ref, gi_ref, wb_ref = refs[:7]
        pos = 7
        if qkv is not None:
            dq_ref, dkc_ref, dkp_ref, dkn_ref, xho_ref, rso_ref, go_ref, wkv_ref = refs[pos:pos + 8]
            pos += 8
        dzb_ref, act_ref, dhg_ref, dhu_ref, dzp_ref, st_ref = refs[pos:pos + 6]
        pos += 6
        if qkv is not None:
            dkv_ref, sto_ref, dbkv_ref = refs[pos:pos + 3]
            pos += 3
        gt_s, ut_s, dn_s = refs[pos:pos + 3]
        i = pl.program_id(0)

        @pl.when(i == 0)
        def _():
            for name, dst in zip(names, (gt_s, ut_s, dn_s)):
                _load_weight(wb_ref, lay, name, dst)
            st_ref[...] = jnp.zeros(st_ref.shape, F32)
            if qkv is not None:
                _load_weight(wb_ref, lay, "wq", refs[pos + 3])
                sto_ref[...] = jnp.zeros(sto_ref.shape, F32)
                dbkv_ref[...] = jnp.zeros(dbkv_ref.shape, F32)

        if qkv is None:
            dzv = dz_ref[...]
        else:
            nxt = jnp.where(i < nsteps - 1, dkn_ref[...], 0.0)
            shifted = jnp.concatenate([dkp_ref[pl.ds(BLOCK, tm - BLOCK), :], nxt], axis=0) if nbt > 1 else nxt
            dkv = dkc_ref[...] + shifted
            dkvb = dkv.astype(BF16)
            dkv_ref[...] = dkvb
            dbkv_ref[...] += jnp.sum(dkv, axis=0, keepdims=True)
            dxo = (ALPHA * dz_ref[...] + _dot_nt(dq_ref[...], refs[pos + 3][...])
                   + _dot_nt(dkvb, wkv_ref[...]))
            dzv, dgo, dbo = _ln_bwd(dxo, xho_ref[...], rso_ref[...], go_ref[...])
            sto_ref[0:1, :] += dgo
            sto_ref[1:2, :] += dbo
        dzb = dzv.astype(BF16)
        dzb_ref[...] = dzb
        dx = ALPHA * dzv
        def back(c):
            return _dot_nt(dzb, _wrows(dn_s, c * FFN_FC, FFN_FC))

        nc = f // FFN_FC
        ahead = [back(c) for c in range(min(FFN_AHEAD, nc))]
        for c in range(nc):
            rows = slice(c * FFN_FC, (c + 1) * FFN_FC)
            dact = ahead.pop(0)
            if c + FFN_AHEAD < nc:
                ahead.append(back(c + FFN_AHEAD))
            hg_v = hg_ref[:, rows].astype(F32)
            hu_v = hu_ref[:, rows].astype(F32)
            sg = _sigmoid(hg_v)
            silu = hg_v * sg
            act_ref[:, rows] = (silu * hu_v).astype(BF16)
            dhu = (dact * silu).astype(BF16)
            dhg = (dact * hu_v * (sg * (1.0 + hg_v * (1.0 - sg)))).astype(BF16)
            dhu_ref[:, rows] = dhu
            dhg_ref[:, rows] = dhg
            dx = (dx + _dot(dhg, _wrows(gt_s, c * FFN_FC, FFN_FC))
                  + _dot(dhu, _wrows(ut_s, c * FFN_FC, FFN_FC)))
        dzp, dg, db = _ln_bwd(dx, xh_ref[...], rs_ref[...], gi_ref[...])
        dzp_ref[...] = dzp
        st_ref[0:1, :] += dg
        st_ref[1:2, :] += db

    in_specs = [_tile(tm, d), _tile(tm, f), _tile(tm, f), _tile(tm, d), _tile(tm, 1), _row(d), ANY]
    args = [dz, hg, hu, xh_in, rs_in, g_in, wb]
    out_specs = [_tile(tm, d), _tile(tm, f), _tile(tm, f), _tile(tm, f), _tile(tm, d), _fixed(8, d)]
    out_shape = [jax.ShapeDtypeStruct((t, d), BF16), jax.ShapeDtypeStruct((t, f), BF16),
                 jax.ShapeDtypeStruct((t, f), BF16), jax.ShapeDtypeStruct((t, f), BF16),
                 jax.ShapeDtypeStruct((t, d), F32), jax.ShapeDtypeStruct((8, d), F32)]
    scratch = [_wscratch(lay, n) for n in names]
    if qkv is not None:
        dq, dkc, dkp, xh_out, rs_out, g_out, wkv = qkv
        last = t // BLOCK - 1
        in_specs += [_tile(tm, d), _tile(tm, 2 * BLOCK), _tile(tm, 2 * BLOCK),
                     pl.BlockSpec((BLOCK, 2 * BLOCK), lambda i: (jnp.minimum((i + 1) * nbt, last), 0)),
                     _tile(tm, d), _tile(tm, 1), _row(d), _fixed(d, 2 * BLOCK)]
        args += [dq, dkc, dkp, dkp, xh_out, rs_out, g_out, wkv]
        out_specs += [_tile(tm, 2 * BLOCK), _fixed(8, d), _row(2 * BLOCK)]
        out_shape += [jax.ShapeDtypeStruct((t, 2 * BLOCK), BF16), jax.ShapeDtypeStruct((8, d), F32),
                      jax.ShapeDtypeStruct((1, 2 * BLOCK), F32)]
        scratch.append(_wscratch(lay, "wq"))
    return _gridded_call(body, f"ffn_bwd{layer}", nsteps, in_specs, out_specs, out_shape, scratch, args, hosted)


def _alibi_slope(h, nq):
    return 2.0 ** (-ALIBI_MAX * (h + 1) / nq)


def _fill_alibi_bias(bias_s, nq, keys_on_rows=False):
    shape = (2 * BLOCK, BLOCK) if keys_on_rows else (BLOCK, 2 * BLOCK)
    qi = lax.broadcasted_iota(jnp.int32, shape, 1 if keys_on_rows else 0)
    kj = lax.broadcasted_iota(jnp.int32, shape, 0 if keys_on_rows else 1)
    delta = qi + BLOCK - kj
    valid = (delta >= 0) & (delta < BLOCK)
    dist = jnp.where(valid, delta.astype(F32), MASKED_DIST)
    dist_first = jnp.where(kj >= BLOCK, dist, MASKED_DIST)
    for h in range(nq):
        bias_s[0, h] = _alibi_slope(h, nq) * dist
        bias_s[1, h] = _alibi_slope(h, nq) * dist_first


def _padded_kv(kvb, kvh, transposed_v=False):
    lane = lax.broadcasted_iota(jnp.int32, (2 * BLOCK, BLOCK), 1)
    mine = (lane < HEAD_DIM) if kvh == 0 else (lane >= HEAD_DIM)
    out = []
    for sec, transposed in ((kvb[:, :BLOCK], False), (kvb[:, BLOCK:], transposed_v)):
        m = jnp.where(mine, sec.astype(F32), 0.0)
        sw = pltpu.roll(m, HEAD_DIM, 1)
        pair = (m, sw) if kvh == 0 else (sw, m)
        out.append(tuple((p.T if transposed else p).astype(BF16) for p in pair))
    return out


def _attn_fwd(xh_in, g_in, b_in, x_in_b, kvs, wb, lay, bq, sinks, bo, tm):
    t, d = xh_in.shape
    nq = d // HEAD_DIM
    pairs_per_kv = (d // BLOCK) // N_KV_HEADS
    nbt = tm // BLOCK
    scale = HEAD_DIM ** -0.5

    def body(xh_ref, gi_ref, bi_ref, xb_ref, kv_ref, kvp_ref, wb_ref, bq_ref, sk_ref, bo_ref,
             q_ref, o_ref, lse_ref, xho_ref, rso_ref, wq_s, wo_s, kvall, q_s, o_s, bias_s):
        i = pl.program_id(0)

        @pl.when(i == 0)
        def _():
            _load_weight(wb_ref, lay, "wq", wq_s)
            _load_weight(wb_ref, lay, "wo", wo_s)
            _fill_alibi_bias(bias_s, nq, keys_on_rows=True)

        qv = ((_dot(xb_ref[...], _wfull(wq_s)) + bq_ref[...]) * scale).astype(BF16)
        q_s[...] = qv
        q_ref[...] = qv
        kvall[pl.ds(0, BLOCK), :] = kvp_ref[...]
        kvall[pl.ds(BLOCK, tm), :] = kv_ref[...]
        head_row = lax.broadcasted_iota(jnp.int32, (BLOCK, BLOCK), 0)

        def score_phase(j):
            rows = slice(j * BLOCK, (j + 1) * BLOCK)
            kvb = kvall[j * BLOCK:(j + 2) * BLOCK, :]
            first = (i * nbt + j == 0).astype(jnp.int32)
            pads = [_padded_kv(kvb, kvh, transposed_v=True) for kvh in range(N_KV_HEADS)]
            scores = []
            for a in range(d // BLOCK):
                kpad = pads[a // pairs_per_kv][0]
                qp = q_s[rows, a * BLOCK:(a + 1) * BLOCK]
                for e in range(2):
                    scores.append(_dot_nt(kpad[e], qp) - bias_s[first, 2 * a + e])
            return rows, pads, scores

        def softmax_phase(state):
            rows, pads, scores = state
            probs, inv = [], []
            lse_t = jnp.zeros((BLOCK, BLOCK), F32)
            for h in range(nq):
                sink = sk_ref[:, h:h + 1]
                m = jnp.maximum(jnp.max(scores[h], axis=0, keepdims=True), sink)
                p = jnp.exp(scores[h] - m)
                l = jnp.sum(p, axis=0, keepdims=True) + jnp.exp(sink - m)
                lse_t = jnp.where(head_row == h, m + jnp.log(l), lse_t)
                probs.append(p.astype(BF16))
                inv.append(1.0 / l)
            lse_ref[rows, :] = lse_t.T
            return rows, pads, probs, inv

        def value_phase(state):
            rows, pads, probs, inv = state
            for a in range(d // BLOCK):
                vpad_t = pads[a // pairs_per_kv][1]
                opair_t = (_dot(vpad_t[0], probs[2 * a]) * inv[2 * a]
                           + _dot(vpad_t[1], probs[2 * a + 1]) * inv[2 * a + 1])
                o_s[rows, a * BLOCK:(a + 1) * BLOCK] = opair_t.T.astype(BF16)

        for state in [softmax_phase(s) for s in [score_phase(j) for j in range(nbt)]]:
            value_phase(state)
        ov = o_s[...]
        o_ref[...] = ov
        xin = xh_ref[...] * gi_ref[...] + bi_ref[...]
        xho, rso = _ln_fwd(ALPHA * xin + _dot(ov, _wfull(wo_s)) + bo_ref[...])
        xho_ref[...] = xho
        rso_ref[...] = rso

    return pl.pallas_call(
        body, name="attn_fwd", grid=(t // tm,),
        in_specs=[_tile(tm, d), _row(d), _row(d), _tile(tm, d), _tile(tm, 2 * BLOCK),
                  pl.BlockSpec((BLOCK, 2 * BLOCK), lambda i: (jnp.maximum(i * nbt - 1, 0), 0)),
                  ANY, _row(d), _row(nq), _row(d)],
        out_specs=[_tile(tm, d), _tile(tm, d), _tile(tm, BLOCK), _tile(tm, d), _tile(tm, 1)],
        out_shape=[jax.ShapeDtypeStruct((t, d), BF16), jax.ShapeDtypeStruct((t, d), BF16),
                   jax.ShapeDtypeStruct((t, BLOCK), F32), jax.ShapeDtypeStruct((t, d), F32),
                   jax.ShapeDtypeStruct((t, 1), F32)],
        scratch_shapes=[_wscratch(lay, "wq"), _wscratch(lay, "wo"),
                        pltpu.VMEM((BLOCK + tm, 2 * BLOCK), BF16), pltpu.VMEM((tm, d), BF16),
                        pltpu.VMEM((tm, d), BF16), pltpu.VMEM((2, nq, 2 * BLOCK, BLOCK), F32)],
        compiler_params=_params(),
    )(xh_in, g_in, b_in, x_in_b, kvs, kvs, wb, bq, sinks, bo)


def _attn_bwd(dz, q, o, lse, kvs, wb, lay, sinks, tm, hosted=None):
    t, d = dz.shape
    nq = d // HEAD_DIM
    pairs_per_kv = (d // BLOCK) // N_KV_HEADS
    nbt = tm // BLOCK
    scale = HEAD_DIM ** -0.5

    def body(dz_ref, q_ref, o_ref, lse_ref, kv_ref, kvp_ref, wb_ref, sk_ref,
             dzb_ref, dq_ref, dkc_ref, dkp_ref, st_ref, dsk_ref, wo_s, kvall, do_s, dq_s, bias_s):
        i = pl.program_id(0)

        @pl.when(i == 0)
        def _():
            _load_weight(wb_ref, lay, "wo", wo_s)
            _fill_alibi_bias(bias_s, nq, keys_on_rows=True)
            st_ref[...] = jnp.zeros(st_ref.shape, F32)
            dsk_ref[...] = jnp.zeros(dsk_ref.shape, F32)

        dzv = dz_ref[...]
        dzb = dzv.astype(BF16)
        dzb_ref[...] = dzb
        do_s[...] = _dot_nt(dzb, _wfull(wo_s))
        kvall[pl.ds(0, BLOCK), :] = kvp_ref[...]
        kvall[pl.ds(BLOCK, tm), :] = kv_ref[...]
        lane = lax.broadcasted_iota(jnp.int32, (BLOCK, BLOCK), 1)
        lane1 = lax.broadcasted_iota(jnp.int32, (1, BLOCK), 1)
        lane2 = lax.broadcasted_iota(jnp.int32, (2 * BLOCK, BLOCK), 1)
        halves = (lane < HEAD_DIM, lane >= HEAD_DIM)
        sel_row = lax.broadcasted_iota(jnp.int32, (8, BLOCK), 0)
        sel_lane = lax.broadcasted_iota(jnp.int32, (8, BLOCK), 1)
        head_sel = jnp.where((sel_row == 0) & (sel_lane < HEAD_DIM) | (sel_row == 1) & (sel_lane >= HEAD_DIM),
                             1.0, 0.0).astype(BF16)

        def score_phase(j):
            rows = slice(j * BLOCK, (j + 1) * BLOCK)
            kvb = kvall[j * BLOCK:(j + 2) * BLOCK, :]
            first = (i * nbt + j == 0).astype(jnp.int32)
            pads = [_padded_kv(kvb, kvh) for kvh in range(N_KV_HEADS)]
            scores, dps, dhs, qms, doms = [], [], [], [], []
            for a in range(d // BLOCK):
                kpad, vpad = pads[a // pairs_per_kv]
                cols = slice(a * BLOCK, (a + 1) * BLOCK)
                qp = q_ref[rows, cols]
                dop = do_s[rows, cols]
                dopb = dop.astype(BF16)
                prod = dop * o_ref[rows, cols].astype(F32)
                hi = prod.astype(BF16)
                lo = (prod - hi.astype(F32)).astype(BF16)
                dh_pair = _dot_nt(head_sel, hi) + _dot_nt(head_sel, lo)
                for e in range(2):
                    scores.append(_dot_nt(kpad[e], qp) - bias_s[first, 2 * a + e])
                    dps.append(_dot_nt(vpad[e], dopb))
                    dhs.append(dh_pair[e:e + 1, :])
                    qms.append(jnp.where(halves[e], qp, jnp.zeros_like(qp)))
                    doms.append(jnp.where(halves[e], dopb, jnp.zeros_like(dopb)))
            return rows, pads, scores, dps, dhs, qms, doms

        def softmax_phase(state):
            rows, pads, scores, dps, dhs, qms, doms = state
            dss, pbs = [], []
            dsk_t = jnp.zeros((1, BLOCK), F32)
            lse_t = lse_ref[rows, :].T
            for h in range(nq):
                lse_h = lse_t[h:h + 1, :]
                p = jnp.exp(scores[h] - lse_h)
                dss.append((p * (dps[h] - dhs[h])).astype(BF16))
                pbs.append(p.astype(BF16))
                dsink = -jnp.sum(jnp.exp(sk_ref[:, h:h + 1] - lse_h) * dhs[h], axis=1, keepdims=True)
                dsk_t = jnp.where(lane1 == h, dsink, dsk_t)
            dsk_ref[...] += dsk_t
            return rows, pads, dss, pbs, qms, doms

        def grad_phase(state):
            rows, pads, dss, pbs, qms, doms = state
            dsecs = []
            for kvh in range(N_KV_HEADS):
                kpad_t = [p.astype(F32).T.astype(BF16) for p in pads[kvh][0]]
                dk_acc = jnp.zeros((2 * BLOCK, BLOCK), F32)
                dv_acc = jnp.zeros((2 * BLOCK, BLOCK), F32)
                for a in range(kvh * pairs_per_kv, (kvh + 1) * pairs_per_kv):
                    dqp_t = _dot(kpad_t[0], dss[2 * a]) + _dot(kpad_t[1], dss[2 * a + 1])
                    dq_s[rows, a * BLOCK:(a + 1) * BLOCK] = dqp_t.T * scale
                    for e in range(2):
                        h = 2 * a + e
                        dk_acc = dk_acc + _dot(dss[h], qms[h])
                        dv_acc = dv_acc + _dot(pbs[h], doms[h])
                dsecs.append((dk_acc + pltpu.roll(dk_acc, HEAD_DIM, 1), dv_acc + pltpu.roll(dv_acc, HEAD_DIM, 1)))
            lo = lane2 < HEAD_DIM
            dkv = jnp.concatenate([jnp.where(lo, dsecs[0][0], dsecs[1][0]),
                                   jnp.where(lo, dsecs[0][1], dsecs[1][1])], axis=1)
            dkp_ref[rows, :] = dkv[:BLOCK]
            dkc_ref[rows, :] = dkv[BLOCK:]

        for state in [softmax_phase(s) for s in [score_phase(j) for j in range(nbt)]]:
            grad_phase(state)
        dqv = dq_s[...]
        dq_ref[...] = dqv.astype(BF16)
        st_ref[0:1, :] += jnp.sum(dqv, axis=0, keepdims=True)
        st_ref[1:2, :] += jnp.sum(dzv, axis=0, keepdims=True)

    return _gridded_call(
        body, "attn_bwd", t // tm,
        [_tile(tm, d), _tile(tm, d), _tile(tm, d), _tile(tm, BLOCK), _tile(tm, 2 * BLOCK),
         pl.BlockSpec((BLOCK, 2 * BLOCK), lambda i: (jnp.maximum(i * nbt - 1, 0), 0)),
         ANY, _row(nq)],
        [_tile(tm, d), _tile(tm, d), _tile(tm, 2 * BLOCK), _tile(tm, 2 * BLOCK),
         _fixed(8, d), _row(BLOCK)],
        [jax.ShapeDtypeStruct((t, d), BF16), jax.ShapeDtypeStruct((t, d), BF16),
         jax.ShapeDtypeStruct((t, 2 * BLOCK), F32), jax.ShapeDtypeStruct((t, 2 * BLOCK), F32),
         jax.ShapeDtypeStruct((8, d), F32), jax.ShapeDtypeStruct((1, BLOCK), F32)],
        [_wscratch(lay, "wo"), pltpu.VMEM((BLOCK + tm, 2 * BLOCK), BF16),
         pltpu.VMEM((tm, d), F32), pltpu.VMEM((tm, d), F32),
         pltpu.VMEM((2, nq, 2 * BLOCK, BLOCK), F32)],
        (dz, q, o, lse, kvs, kvs, wb, sinks), hosted)


def _tn_matmul(a, b, name, bm, tk):
    t, m = a.shape
    n = b.shape[1]
    ksteps = t // tk

    nc = max(n // 256, 1)
    cw = n // nc

    def body(a_ref, b_ref, o_ref, acc):
        k = pl.program_id(1)

        @pl.when(k == 0)
        def _():
            acc[...] = jnp.zeros(acc.shape, F32)

        at = a_ref[...].T
        for c in range(nc):
            cols = slice(c * cw, (c + 1) * cw)
            acc[:, cols] += _dot(at, b_ref[:, cols])

        @pl.when(k == ksteps - 1)
        def _():
            o_ref[...] = acc[...].astype(BF16)

    return pl.pallas_call(
        body, name=name, grid=(m // bm, ksteps),
        in_specs=[pl.BlockSpec((tk, bm), lambda j, k: (k, j)), pl.BlockSpec((tk, n), lambda j, k: (k, 0))],
        out_specs=pl.BlockSpec((bm, n), lambda j, k: (j, 0)),
        out_shape=jax.ShapeDtypeStruct((m, n), BF16),
        scratch_shapes=[pltpu.VMEM((bm, n), F32)],
        compiler_params=pltpu.CompilerParams(dimension_semantics=("arbitrary", "arbitrary"),
                                             vmem_limit_bytes=VMEM_LIMIT),
    )(a, b)


def _all_gather(arrays, name):
    n = len(arrays)

    def body(*refs):
        ins, outs = refs[:n], refs[n:2 * n]
        send_sems, recv_sems, local_sems = refs[2 * n:]
        x, y, c = _me()
        me, sibling = (x, y, c), (x, y, 1 - c)
        chips = [(1 - x, y), (x, 1 - y), (1 - x, 1 - y)]

        def slot(ref, dev):
            return ref.at[4 * dev[0] + 2 * dev[1] + dev[2]]

        def copy(a, k, block, to, src=None):
            return pltpu.make_async_remote_copy(
                src_ref=slot(outs[a], block) if src is None else src, dst_ref=slot(outs[a], block),
                send_sem=send_sems.at[a, k], recv_sem=recv_sems.at[a, k], device_id=to, device_id_type=MESH)

        mine = [pltpu.make_async_copy(ins[a], slot(outs[a], me), local_sems.at[a]) for a in range(n)]
        for cp in mine:
            cp.start()
        first = []
        for a in range(n):
            first.append(copy(a, 0, me, sibling, src=ins[a]))
            first += [copy(a, 1 + j, me, (*chip, c), src=ins[a]) for j, chip in enumerate(chips)]
        for cp in first:
            cp.start()
        passed = []
        for a in range(n):
            for j, chip in enumerate(chips):
                copy(a, 1 + j, (*chip, c), me).wait_recv()
                cp = copy(a, 4 + j, (*chip, c), sibling)
                cp.start()
                passed.append(cp)
        for a in range(n):
            copy(a, 0, sibling, me).wait_recv()
            for j, chip in enumerate(chips):
                copy(a, 4 + j, (*chip, 1 - c), me).wait_recv()
        for cp in first + passed:
            cp.wait_send()
        for cp in mine:
            cp.wait()

    return pl.pallas_call(
        body, name=name, in_specs=[ANY] * n, out_specs=[ANY] * n,
        out_shape=[jax.ShapeDtypeStruct((N_DEV,) + a.shape, a.dtype) for a in arrays],
        scratch_shapes=[pltpu.SemaphoreType.DMA((n, 7)), pltpu.SemaphoreType.DMA((n, 7)),
                        pltpu.SemaphoreType.DMA((n,))],
    )(*arrays)


def _exchange(arrays, name):
    n = len(arrays)
    blocked = [a.ndim == 3 for a in arrays]

    def body(*refs):
        ins, outs = refs[:n], refs[n:2 * n]
        send_sems, recv_sems, local_sems = refs[2 * n:]
        me = _index(_me())

        def src(k, dev):
            return ins[k].at[dev] if blocked[k] else ins[k]

        local = [pltpu.make_async_copy(src(k, me), outs[k].at[me], local_sems.at[k]) for k in range(n)]
        sends, arrivals = [], []
        for k in range(n):
            for mask in range(1, N_DEV):
                peer = _peer(mask)
                sends.append(pltpu.make_async_remote_copy(
                    src_ref=src(k, _index(peer)), dst_ref=outs[k].at[me], send_sem=send_sems.at[k, mask - 1],
                    recv_sem=recv_sems.at[k, mask - 1], device_id=peer, device_id_type=MESH))
                arrivals.append(pltpu.make_async_remote_copy(
                    src_ref=src(k, me), dst_ref=outs[k].at[_index(peer)], send_sem=send_sems.at[k, mask - 1],
                    recv_sem=recv_sems.at[k, mask - 1], device_id=_me(), device_id_type=MESH))
        for cp in local + sends:
            cp.start()
        for cp in arrivals:
            cp.wait_recv()
        for cp in sends:
            cp.wait_send()
        for cp in local:
            cp.wait()

    return pl.pallas_call(
        body, name=name, in_specs=[ANY] * n, out_specs=[ANY] * n,
        out_shape=[jax.ShapeDtypeStruct((N_DEV,) + a.shape[-2:], a.dtype) for a in arrays],
        scratch_shapes=[pltpu.SemaphoreType.DMA((n, 7)), pltpu.SemaphoreType.DMA((n, 7)),
                        pltpu.SemaphoreType.DMA((n,))],
    )(*arrays)


def _adamw_update(g, w_ref, m_ref, v_ref, go_ref, d_ref, mo_ref, vo_ref):
    mn = ADAM_B1 * m_ref[...] + (1.0 - ADAM_B1) * g
    vn = ADAM_B2 * v_ref[...] + (1.0 - ADAM_B2) * (g * g)
    m_hat = mn / (1.0 - ADAM_B1 ** ADAM_STEP)
    v_hat = vn / (1.0 - ADAM_B2 ** ADAM_STEP)
    go_ref[...] = g
    d_ref[...] = -ADAM_LR * (m_hat / (jnp.sqrt(v_hat) + ADAM_EPS) + ADAM_WD * w_ref[...])
    mo_ref[...] = mn
    vo_ref[...] = vn


def _sum_sources(g_refs, layer):
    total = None
    for l, g_ref in enumerate(g_refs):
        g = g_ref[0].astype(F32)
        for s in range(1, N_DEV):
            g = g + g_ref[s].astype(F32)
        total = g if total is None else jnp.where(layer == l, g, total)
    return total


def _layer_block(l_mine, nblocks):
    def index(l, j):
        return (0, jnp.where(l == l_mine, j, jnp.where(l < l_mine, 0, nblocks - 1)), 0)
    return index


def _adamw_sum(g8, w, m, v, name, tr):
    r, width = w.shape

    def body(g_ref, *refs):
        _adamw_update(_sum_sources([g_ref], 0), *refs)

    spec = pl.BlockSpec((tr, width), lambda i: (i, 0))
    return pl.pallas_call(
        body, name=name, grid=(r // tr,),
        in_specs=[pl.BlockSpec((N_DEV, tr, width), lambda i: (0, i, 0)), spec, spec, spec],
        out_specs=[spec] * 4, out_shape=[jax.ShapeDtypeStruct((r, width), F32)] * 4,
        compiler_params=_params(),
    )(g8, w, m, v)


def _adamw_rows(g8s, w, m, v, name):
    layers, n, width = w.shape
    tr = max(r for r in range(BF16_SUBLANES, ADAMW_MAX_ROWS + 1, BF16_SUBLANES) if n % r == 0)
    nb = n // tr

    def body(*refs):
        _adamw_update(_sum_sources(refs[:layers], pl.program_id(0)), *refs[layers:])

    spec = pl.BlockSpec((None, tr, width), lambda l, j: (l, j, 0))
    return pl.pallas_call(
        body, name=name, grid=(layers, nb),
        in_specs=[pl.BlockSpec((N_DEV, tr, width), _layer_block(l, nb)) for l in range(layers)] + [spec] * 3,
        out_specs=[spec] * 4, out_shape=[jax.ShapeDtypeStruct(w.shape, F32)] * 4,
        compiler_params=pltpu.CompilerParams(dimension_semantics=("arbitrary", "arbitrary"),
                                             vmem_limit_bytes=VMEM_LIMIT),
    )(*g8s, w, m, v)


def _adamw_cols(g8s, w, m, v, name):
    layers, k, n = w.shape
    cb = min(BLOCK, n)
    nb = pl.cdiv(n, cb)

    def body(*refs):
        _adamw_update(_sum_sources(refs[:layers], pl.program_id(0)).T, *refs[layers:])

    spec = pl.BlockSpec((None, k, cb), lambda l, j: (l, 0, j))
    return pl.pallas_call(
        body, name=name, grid=(layers, nb),
        in_specs=[pl.BlockSpec((N_DEV, cb, k), _layer_block(l, nb)) for l in range(layers)] + [spec] * 3,
        out_specs=[spec] * 4, out_shape=[jax.ShapeDtypeStruct(w.shape, F32)] * 4,
        compiler_params=pltpu.CompilerParams(dimension_semantics=("arbitrary", "arbitrary"),
                                             vmem_limit_bytes=VMEM_LIMIT),
    )(*g8s, w, m, v)


def _local_step(x, target, wba, shard_b, lay, sm, tm, tk):
    t, d = x.shape
    f = lay.f
    w_dw32 = jnp.concatenate([sm["w_dw"], jnp.zeros((HALO - CONV_WIDTH, d), F32)], axis=0)
    lmg, lmb, lfg, lfb = sm["ln_mix_g"], sm["ln_mix_b"], sm["ln_ffn_g"], sm["ln_ffn_b"]
    bkv = jnp.concatenate([sm["b_k"], sm["b_v"]], axis=1)
    bm_f = f // 2 if (f // 2) % 128 == 0 else f
    tm_light = 2 * tm

    received = {}

    def exchange(grads):
        return _HostedExchange([g.reshape(N_DEV, lay.n[n], d) for n, g in grads.items()])

    def keep(grads, arrived):
        received.update(zip(grads, arrived))

    xb0, ag, xhc, rsc, xh1, rs1, wbb = _conv_fwd(x, wba, lay, w_dw32, sm["b_pw1"], sm["b_dw"], sm["cg"],
                                                 sm["cb"], sm["b_pw2"], lmg[0:1], lmb[0:1], tm,
                                                 hosted=_HostedGather(shard_b))
    wkv = wbb[:, lay.goff["wkv"]:lay.goff["wkv"] + lay.n["wkv"], :].reshape(d, 2 * BLOCK)
    x1b, hg0, hu0, xh2, rs2, x2b, kvs = _ffn_fwd(xh1, lmg[0:1], lmb[0:1], wbb, lay, 0, tm_light,
                                                kv=(lfg[0:1], lfb[0:1], wkv, bkv))
    q, o, lse, xh3, rs3 = _attn_fwd(xh2, lfg[0:1], lfb[0:1], x2b, kvs, wbb, lay, sm["b_q"], sm["sinks"],
                                    sm["b_o"], tm)
    x3b, hg1, hu1, dz4, st4, loss = _ffn_fwd(xh3, lmg[1:2], lmb[1:2], wbb, lay, 1, tm,
                                             loss=(lfg[1:2], lfb[1:2], target))

    dz4b, act1, dhg1, dhu1, dz3, st3 = _ffn_bwd(dz4, hg1, hu1, xh3, rs3, lmg[1:2], wbb, lay, 1, tm)
    g1 = {"gt1": _tn_matmul(dhg1, x3b, "dw_gate1", bm_f, tk), "ut1": _tn_matmul(dhu1, x3b, "dw_up1", bm_f, tk),
          "dn1": _tn_matmul(act1, dz4b, "dw_down1", bm_f, tk)}
    dz3b, dq, dkc, dkp, stq, dsinks, *arrived = _attn_bwd(dz3, q, o, lse, kvs, wbb, lay, sm["sinks"], tm,
                                                          hosted=exchange(g1))
    keep(g1, arrived)
    g2 = {"wq": _tn_matmul(x2b, dq, "dw_q", d, tk), "wo": _tn_matmul(o, dz3b, "dw_o", d, tk)}
    dz2b, act0, dhg0, dhu0, dz1, st1, dkv, st2, dbkv, *arrived = _ffn_bwd(
        dz3, hg0, hu0, xh1, rs1, lmg[0:1], wbb, lay, 0, tm, hosted=exchange(g2),
        qkv=(dq, dkc, dkp, xh2, rs2, lfg[0:1], wkv))
    keep(g2, arrived)
    dz1b, s_act, dcv, stc = _conv_bwd1(dz1, xhc, rsc, wba, lay, sm["cg"], sm["cb"], tm_light)
    g3 = {"gt0": _tn_matmul(dhg0, x1b, "dw_gate0", bm_f, tk), "ut0": _tn_matmul(dhu0, x1b, "dw_up0", bm_f, tk),
          "dn0": _tn_matmul(act0, dz2b, "dw_down0", bm_f, tk), "pw2": _tn_matmul(s_act, dz1b, "dw_pw2", d, tk),
          "wkv": _tn_matmul(x2b, dkv, "dw_kv", d, tk)}
    grad_x, dh1, dwdw, db1, *arrived = _conv_bwd2(dz1, dcv, ag, wba, lay, w_dw32, tm, hosted=exchange(g3))
    keep(g3, arrived)
    g_pw1t = _tn_matmul(dh1, xb0, "dw_pw1", d, tk)
    small = {
        "w_dw": dwdw[:CONV_WIDTH], "b_pw1": db1, "b_dw": stc[2:3], "cg": stc[0:1], "cb": stc[1:2],
        "b_pw2": stc[3:4], "b_k": dbkv[:, :BLOCK], "b_v": dbkv[:, BLOCK:], "b_q": stq[0:1],
        "sinks": dsinks[:, :d // HEAD_DIM],
        "b_o": stq[1:2],
        "ln_mix_g": jnp.concatenate([st1[0:1], st3[0:1]], axis=0),
        "ln_mix_b": jnp.concatenate([st1[1:2], st3[1:2]], axis=0),
        "ln_ffn_g": jnp.concatenate([st2[0:1], st4[0:1]], axis=0),
        "ln_ffn_b": jnp.concatenate([st2[1:2], st4[1:2]], axis=0),
    }
    return loss[0, 0], grad_x, received, g_pw1t, small


SP_ROWS = 40
SP_BDW, SP_CG, SP_CB, SP_BPW2, SP_BPW1 = 32, 33, 34, 35, 36
RP_NAMES = ("ln_mix_g", "ln_mix_b", "ln_ffn_g", "ln_ffn_b", "b_q", "b_o", "b_k", "b_v", "sinks")


def _row_forms(d, pw1, pw2, wq, wo, gate, up, down, wk, wv):
    rf = {"pw1t": pw1[0].T, "pw2": pw2[0], "wq": wq[0], "wo": wo[0],
          "wkv": jnp.concatenate([wk, wv], axis=1).reshape(-1, d)}
    for l in range(DEPTH):
        rf.update({f"gt{l}": gate[l].T, f"ut{l}": up[l].T, f"dn{l}": down[l]})
    return rf


def _pack_rows(rf, names):
    return jnp.concatenate([rf[n] for n in names], axis=0)


def _pack_small(w_dw, b_dw, cg, cb, b_pw2, b_pw1):
    cw = b_dw.shape[1]
    z = jnp.zeros((1, cw), F32)
    return jnp.concatenate([w_dw[0], z, b_dw, cg, cb, b_pw2, b_pw1.reshape(2, cw), z, z], axis=0)


def _unpack_small(p):
    cw = p.shape[1]
    return dict(w_dw=p[None, :CONV_WIDTH], b_dw=p[SP_BDW:SP_BDW + 1], cg=p[SP_CG:SP_CG + 1],
                cb=p[SP_CB:SP_CB + 1], b_pw2=p[SP_BPW2:SP_BPW2 + 1],
                b_pw1=p[SP_BPW1:SP_BPW1 + 2].reshape(1, 2 * cw))


def _small_full(g):
    d = N_DEV * g.shape[2]

    def wide(r0, n=1):
        return jnp.transpose(g[:, r0:r0 + n], (1, 0, 2)).reshape(n, d)

    return dict(w_dw=wide(0, CONV_WIDTH), b_dw=wide(SP_BDW), cg=wide(SP_CG), cb=wide(SP_CB),
                b_pw2=wide(SP_BPW2), b_pw1=g[:, SP_BPW1:SP_BPW1 + 2].reshape(1, 2 * d))


def _small_grad_blocks(sg):
    cw = sg["b_dw"].shape[1] // N_DEV

    def narrow(a):
        return jnp.transpose(a.reshape(a.shape[0], N_DEV, cw), (1, 0, 2))

    z = jnp.zeros((N_DEV, 1, cw), F32)
    return jnp.concatenate([narrow(sg["w_dw"]), z, narrow(sg["b_dw"]), narrow(sg["cg"]), narrow(sg["cb"]),
                            narrow(sg["b_pw2"]), sg["b_pw1"].reshape(N_DEV, 2, cw), z, z], axis=1)


def _pack_rep(vals, rider=0.0):
    parts = []
    for name in RP_NAMES:
        a = vals[name].reshape(-1)
        pad = -a.shape[0] % 128
        parts.append(jnp.concatenate([a, jnp.zeros((pad,), F32)]).reshape(-1, 128))
    parts.append(jnp.full((1, 128), rider, F32))
    rows = sum(p.shape[0] for p in parts)
    parts.append(jnp.zeros((-rows % 8, 128), F32))
    return jnp.concatenate(parts, axis=0)


def _rider_row(shapes):
    return sum(-(-_size(shapes[name]) // 128) for name in RP_NAMES)


def _size(shape):
    n = 1
    for s in shape:
        n *= s
    return n


def _unpack_rep(p, shapes):
    out, r = {}, 0
    for name in RP_NAMES:
        n = _size(shapes[name])
        rows = -(-n // 128)
        out[name] = p[r:r + rows].reshape(-1)[:n].reshape(shapes[name])
        r += rows
    return out


def kernel(x, conv_w_pw1, conv_b_pw1, conv_w_dw, conv_b_dw, conv_ln_g, conv_ln_b, conv_w_pw2, conv_b_pw2, kv_w_k, kv_b_k, kv_w_v, kv_b_v, attn_w_q, attn_b_q, attn_sinks, attn_w_o, attn_b_o, ffn_w_gate, ffn_w_up, ffn_w_down, ln_mix_g, ln_mix_b, ln_ffn_g, ln_ffn_b, loss_target, m_conv_w_pw1, m_conv_b_pw1, m_conv_w_dw, m_conv_b_dw, m_conv_ln_g, m_conv_ln_b, m_conv_w_pw2, m_conv_b_pw2, m_kv_w_k, m_kv_b_k, m_kv_w_v, m_kv_b_v, m_attn_w_q, m_attn_b_q, m_attn_sinks, m_attn_w_o, m_attn_b_o, m_ffn_w_gate, m_ffn_w_up, m_ffn_w_down, m_ln_mix_g, m_ln_mix_b, m_ln_ffn_g, m_ln_ffn_b, v_conv_w_pw1, v_conv_b_pw1, v_conv_w_dw, v_conv_b_dw, v_conv_ln_g, v_conv_ln_b, v_conv_w_pw2, v_conv_b_pw2, v_kv_w_k, v_kv_b_k, v_kv_w_v, v_kv_b_v, v_attn_w_q, v_attn_b_q, v_attn_sinks, v_attn_w_o, v_attn_b_o, v_ffn_w_gate, v_ffn_w_up, v_ffn_w_down, v_ln_mix_g, v_ln_mix_b, v_ln_ffn_g, v_ln_ffn_b):
    t, d = x.shape[1], x.shape[2]
    f = ffn_w_gate.shape[2] * N_DEV
    lay = _Layout(d, f)
    tm, tk = 256, min(2048, t)

    rep_shapes = dict(ln_mix_g=ln_mix_g.shape, ln_mix_b=ln_mix_b.shape, ln_ffn_g=ln_ffn_g.shape,
                      ln_ffn_b=ln_ffn_b.shape, b_q=attn_b_q.shape, b_o=attn_b_o.shape, b_k=kv_b_k.shape,
                      b_v=kv_b_v.shape, sinks=attn_sinks.shape)

    def rep_pack(lmg, lmb, lfg, lfb, bq, bo, bk, bv, sk):
        return _pack_rep(dict(ln_mix_g=lmg, ln_mix_b=lmb, ln_ffn_g=lfg, ln_ffn_b=lfb, b_q=bq, b_o=bo,
                              b_k=bk, b_v=bv, sinks=sk))

    w_rf = _row_forms(d, conv_w_pw1, conv_w_pw2, attn_w_q, attn_w_o, ffn_w_gate, ffn_w_up, ffn_w_down, kv_w_k, kv_w_v)
    w_small = _pack_small(conv_w_dw, conv_b_dw, conv_ln_g, conv_ln_b, conv_b_pw2, conv_b_pw1)
    m_small = _pack_small(m_conv_w_dw, m_conv_b_dw, m_conv_ln_g, m_conv_ln_b, m_conv_b_pw2, m_conv_b_pw1)
    v_small = _pack_small(v_conv_w_dw, v_conv_b_dw, v_conv_ln_g, v_conv_ln_b, v_conv_b_pw2, v_conv_b_pw1)
    w_rep = rep_pack(ln_mix_g, ln_mix_b, ln_ffn_g, ln_ffn_b, attn_b_q, attn_b_o, kv_b_k, kv_b_v, attn_sinks)
    m_rep = rep_pack(m_ln_mix_g, m_ln_mix_b, m_ln_ffn_g, m_ln_ffn_b, m_attn_b_q, m_attn_b_o, m_kv_b_k, m_kv_b_v, m_attn_sinks)
    v_rep = rep_pack(v_ln_mix_g, v_ln_mix_b, v_ln_ffn_g, v_ln_ffn_b, v_attn_b_q, v_attn_b_o, v_kv_b_k, v_kv_b_v, v_attn_sinks)

    wba, smg = _all_gather([_pack_rows(w_rf, lay.GATHER["a"]).astype(BF16), w_small], "gather_conv_weights")
    shard_b = _pack_rows(w_rf, lay.GATHER["b"]).astype(BF16)
    sm = _small_full(smg)
    sm.update(ln_mix_g=ln_mix_g, ln_mix_b=ln_mix_b, ln_ffn_g=ln_ffn_g, ln_ffn_b=ln_ffn_b, b_q=attn_b_q,
              b_o=attn_b_o, sinks=attn_sinks, b_k=kv_b_k.reshape(1, -1), b_v=kv_b_v.reshape(1, -1))

    loss_part, grad_x, received, g_pw1t, gsmall = _local_step(x[0], loss_target[0], wba, shard_b, lay, sm, tm, tk)

    received["pw1t"], g8_small, g8_rep = _exchange(
        [g_pw1t.reshape(N_DEV, lay.n["pw1t"], d), _small_grad_blocks(gsmall), _pack_rep(gsmall, loss_part)],
        "exchange_last_grads")

    def kv_rows(wk, wv):
        return jnp.concatenate([wk, wv], axis=1).reshape(1, -1, d)

    def kv_split(a):
        a = a.reshape(d // N_DEV, 2 * BLOCK)
        return a[:, :BLOCK], a[:, BLOCK:]

    big = dict(
        pw1=_adamw_cols([received["pw1t"]], conv_w_pw1, m_conv_w_pw1, v_conv_w_pw1, "adamw_pw1"),
        gate=_adamw_cols([received["gt0"], received["gt1"]], ffn_w_gate, m_ffn_w_gate, v_ffn_w_gate, "adamw_gate"),
        up=_adamw_cols([received["ut0"], received["ut1"]], ffn_w_up, m_ffn_w_up, v_ffn_w_up, "adamw_up"),
        down=_adamw_rows([received["dn0"], received["dn1"]], ffn_w_down, m_ffn_w_down, v_ffn_w_down, "adamw_down"),
        pw2=_adamw_rows([received["pw2"]], conv_w_pw2, m_conv_w_pw2, v_conv_w_pw2, "adamw_pw2"),
        wq=_adamw_rows([received["wq"]], attn_w_q, m_attn_w_q, v_attn_w_q, "adamw_q"),
        wo=_adamw_rows([received["wo"]], attn_w_o, m_attn_w_o, v_attn_w_o, "adamw_o"),
        wkv=[kv_split(a) for a in _adamw_rows([received["wkv"]], kv_rows(kv_w_k, kv_w_v), kv_rows(m_kv_w_k, m_kv_w_v),
                                              kv_rows(v_kv_w_k, v_kv_w_v), "adamw_kv")])
    big_out = [dict(pw1=big["pw1"][i], pw2=big["pw2"][i], wq=big["wq"][i], wo=big["wo"][i], gate=big["gate"][i],
                    up=big["up"][i], down=big["down"][i], wk=big["wkv"][i][0], wv=big["wkv"][i][1])
               for i in range(4)]
    small_out = [_unpack_small(a) for a in _adamw_sum(g8_small, w_small, m_small, v_small, "adamw_small", SP_ROWS)]
    rep_res = _adamw_sum(g8_rep, w_rep, m_rep, v_rep, "adamw_rep", w_rep.shape[0])
    rep_out = [_unpack_rep(a, rep_shapes) for a in rep_res]
    loss = rep_res[0][_rider_row(rep_shapes), 0]

    outs = [loss, grad_x[None]]
    for b, s, r in zip(big_out, small_out, rep_out):
        outs += [b["pw1"], s["b_pw1"], s["w_dw"], s["b_dw"], s["cg"], s["cb"], b["pw2"], s["b_pw2"],
                 b["wk"], r["b_k"], b["wv"], r["b_v"], b["wq"], r["b_q"], r["sinks"], b["wo"], r["b_o"],
                 b["gate"], b["up"], b["down"], r["ln_mix_g"], r["ln_mix_b"], r["ln_ffn_g"], r["ln_ffn_b"]]
    return tuple(outs)
```

```python
import jax
import jax.numpy as jnp
from jax import lax
from jax.experimental import pallas as pl
from jax.experimental.pallas import tpu as pltpu

F32 = jnp.float32
BF16 = jnp.bfloat16

N_DEV = 8
HEAD_DIM = 64
N_KV_HEADS = 2
BLOCK = 128
CONV_WIDTH = 31
HALO = 32
ALIBI_MAX = 8.0
DEPTH = 2
ALPHA = (2.0 * DEPTH) ** 0.25
LN_EPS = 1e-5
MASKED_DIST = 1e32
ADAM_LR = 0.001
ADAM_B1 = 0.9
ADAM_B2 = 0.999
ADAM_EPS = 1e-08
ADAM_WD = 0.01
ADAM_STEP = 10
VMEM_LIMIT = 56 * 1024 * 1024
BF16_SUBLANES = 16
ADAMW_MAX_ROWS = 176
MESH = pl.DeviceIdType.MESH


def _dot(a, b):
    return jnp.dot(a, b, preferred_element_type=F32)


def _dot_nt(a, b):
    return lax.dot_general(a, b, (((1,), (1,)), ((), ())), preferred_element_type=F32)


def _dot_tn(a, b):
    return lax.dot_general(a, b, (((0,), (0,)), ((), ())), preferred_element_type=F32)


def _sigmoid(v):
    return 1.0 / (1.0 + jnp.exp(-v))


def _ln_fwd(z):
    mu = jnp.mean(z, axis=-1, keepdims=True)
    zc = z - mu
    var = jnp.mean(zc * zc, axis=-1, keepdims=True)
    rstd = lax.rsqrt(var + LN_EPS)
    return zc * rstd, rstd


def _ln_bwd(dout, xh, rstd, g):
    dxh = dout * g
    m1 = jnp.mean(dxh, axis=-1, keepdims=True)
    m2 = jnp.mean(dxh * xh, axis=-1, keepdims=True)
    dz = rstd * (dxh - m1 - xh * m2)
    return dz, jnp.sum(dout * xh, axis=0, keepdims=True), jnp.sum(dout, axis=0, keepdims=True)


def _params(vmem=VMEM_LIMIT):
    return pltpu.CompilerParams(dimension_semantics=("arbitrary",), vmem_limit_bytes=vmem)


def _row(d):
    return pl.BlockSpec((1, d), lambda i: (0, 0))


def _tile(tm, d):
    return pl.BlockSpec((tm, d), lambda i: (i, 0))


def _fixed(r, d):
    return pl.BlockSpec((r, d), lambda i: (0, 0))


def _tile_cur(tm, d, nsteps):
    return pl.BlockSpec((tm, d), lambda i: (jnp.minimum(i, nsteps - 1), 0))


def _tile_prev(tm, d):
    return pl.BlockSpec((tm, d), lambda i: (jnp.maximum(i - 1, 0), 0))


ANY = pl.BlockSpec(memory_space=pl.ANY)


class _Layout:
    GATHER = {"a": ("pw1t", "pw2"),
              "b": ("wq", "wo", "gt0", "ut0", "dn0", "gt1", "ut1", "dn1", "wkv")}

    def __init__(self, d, f):
        self.d, self.f = d, f
        self.n = {"pw1t": 2 * d // N_DEV, "pw2": d // N_DEV, "wq": d // N_DEV, "wo": d // N_DEV,
                  "wkv": (d // N_DEV) * 2 * BLOCK // d}
        for l in range(DEPTH):
            self.n.update({f"gt{l}": f // N_DEV, f"ut{l}": f // N_DEV, f"dn{l}": f // N_DEV})
        self.goff = {}
        for names in self.GATHER.values():
            r = 0
            for name in names:
                self.goff[name] = r
                r += self.n[name]


def _load_weight(wb_ref, lay, name, dst):
    n = lay.n[name]
    for p in range(N_DEV):
        pltpu.sync_copy(wb_ref.at[p, pl.ds(lay.goff[name], n), :], dst.at[pl.ds(p * n, n), :])


def _wscratch(lay, name):
    return pltpu.VMEM((N_DEV * lay.n[name], lay.d), BF16)


def _wfull(ref):
    return ref[...]


def _wrows(ref, r0, nrows):
    return ref[r0:r0 + nrows, :]


def _me():
    return lax.axis_index("x"), lax.axis_index("y"), lax.axis_index("c")


def _peer(mask):
    x, y, c = _me()
    return (1 - x if mask & 4 else x, 1 - y if mask & 2 else y, 1 - c if mask & 1 else c)


def _index(dev):
    return 4 * dev[0] + 2 * dev[1] + dev[2]


class _HostedGather:
    def __init__(self, array):
        self.arrays = [array]
        self.out_shapes = [jax.ShapeDtypeStruct((N_DEV,) + array.shape, array.dtype)]

    def scratch(self):
        return [pltpu.SemaphoreType.DMA((7,)), pltpu.SemaphoreType.DMA((7,)), pltpu.SemaphoreType.DMA(())]

    def _copies(self, ins, outs, send_sems, recv_sems, local_sem):
        out = outs[0]
        x, y, c = _me()
        me, sibling = (x, y, c), (x, y, 1 - c)
        chips = [(1 - x, y), (x, 1 - y), (1 - x, 1 - y)]

        def copy(k, block, to, src=None):
            rows = out.at[_index(block)]
            return pltpu.make_async_remote_copy(
                src_ref=rows if src is None else src, dst_ref=rows, send_sem=send_sems.at[k],
                recv_sem=recv_sems.at[k], device_id=to, device_id_type=MESH)

        return dict(
            mine=lambda: pltpu.make_async_copy(ins[0], out.at[_index(me)], local_sem),
            first=lambda: [copy(0, me, sibling, src=ins[0])] + [copy(1 + j, me, (*chip, c), src=ins[0])
                                                                for j, chip in enumerate(chips)],
            over_ici=lambda: [copy(1 + j, (*chip, c), me) for j, chip in enumerate(chips)],
            passed=lambda: [copy(4 + j, (*chip, c), sibling) for j, chip in enumerate(chips)],
            from_sibling=lambda: [copy(0, sibling, me)] + [copy(4 + j, (*chip, 1 - c), me)
                                                           for j, chip in enumerate(chips)])

    def start(self, *refs):
        cp = self._copies(*refs)
        cp["mine"]().start()
        for c in cp["first"]():
            c.start()

    def middle(self, *refs):
        cp = self._copies(*refs)
        for arrived, onward in zip(cp["over_ici"](), cp["passed"]()):
            arrived.wait_recv()
            onward.start()

    def finish(self, *refs):
        cp = self._copies(*refs)
        for c in cp["from_sibling"]():
            c.wait_recv()
        for c in cp["first"]() + cp["passed"]():
            c.wait_send()
        cp["mine"]().wait()


class _HostedExchange:
    def __init__(self, arrays):
        self.arrays = list(arrays)
        self.out_shapes = [jax.ShapeDtypeStruct(a.shape, a.dtype) for a in self.arrays]

    def scratch(self):
        n = len(self.arrays)
        return [pltpu.SemaphoreType.DMA((n, 7)), pltpu.SemaphoreType.DMA((n, 7)), pltpu.SemaphoreType.DMA((n,))]

    def _copies(self, ins, outs, send_sems, recv_sems, local_sems):
        me = _index(_me())

        def dst(k, src_dev):
            return outs[k].at[src_dev]

        pairs = [(k, mask) for k in range(len(self.arrays)) for mask in range(1, N_DEV)]

        def local():
            return [pltpu.make_async_copy(ins[k].at[me], dst(k, me), local_sems.at[k])
                    for k in range(len(self.arrays))]

        def sends():
            return [pltpu.make_async_remote_copy(
                src_ref=ins[k].at[_index(_peer(mask))], dst_ref=dst(k, me), send_sem=send_sems.at[k, mask - 1],
                recv_sem=recv_sems.at[k, mask - 1], device_id=_peer(mask), device_id_type=MESH)
                for k, mask in pairs]

        def arrivals():
            return [pltpu.make_async_remote_copy(
                src_ref=ins[k].at[me], dst_ref=dst(k, _index(_peer(mask))), send_sem=send_sems.at[k, mask - 1],
                recv_sem=recv_sems.at[k, mask - 1], device_id=_me(), device_id_type=MESH)
                for k, mask in pairs]

        return local, sends, arrivals

    def start(self, *refs):
        local, sends, _ = self._copies(*refs)
        for c in local() + sends():
            c.start()

    def middle(self, *refs):
        pass

    def finish(self, *refs):
        local, sends, arrivals = self._copies(*refs)
        for c in arrivals():
            c.wait_recv()
        for c in sends():
            c.wait_send()
        for c in local():
            c.wait()


def _gridded_call(body, name, nsteps, in_specs, out_specs, out_shape, scratch, args, hosted=None):
    if hosted is None:
        return pl.pallas_call(body, name=name, grid=(nsteps,), in_specs=in_specs, out_specs=out_specs,
                              out_shape=out_shape, scratch_shapes=scratch, compiler_params=_params())(*args)
    n_in, n_out, n_scr, h_in = len(in_specs), len(out_specs), len(scratch), len(hosted.arrays)
    h_out = len(hosted.out_shapes)

    def with_hosted(*refs):
        a = n_in + h_in
        b = a + n_out
        e = b + h_out + n_scr
        comm = (refs[n_in:a], refs[b:b + h_out], refs[e], refs[e + 1], refs[e + 2])
        i = pl.program_id(0)

        @pl.when(i == 0)
        def _():
            hosted.start(*comm)

        body(*refs[:n_in], *refs[a:b], *refs[b + h_out:e])

        @pl.when(i == nsteps // 2)
        def _():
            hosted.middle(*comm)

        @pl.when(i == nsteps - 1)
        def _():
            hosted.finish(*comm)

    return pl.pallas_call(
        with_hosted, name=name, grid=(nsteps,), in_specs=list(in_specs) + [ANY] * h_in,
        out_specs=list(out_specs) + [ANY] * h_out, out_shape=list(out_shape) + hosted.out_shapes,
        scratch_shapes=list(scratch) + hosted.scratch(), compiler_params=_params(),
    )(*args, *hosted.arrays)


CONV_RB = 64
CONV_LC = 128
CONV_MC = 256


def _shifted(win, r):
    return win if r == 0 else pltpu.roll(win, win.shape[0] - r, 0)


def _conv_fwd(x, wb, lay, w_dw, b_pw1, b_dw, cg, cb, b_pw2, lg, lb, tm, hosted=None):
    t, d = x.shape
    nsteps = t // tm

    def body(x_ref, xh_ref, wb_ref, wdw_ref, b1_ref, bdw_ref, cg_ref, cb_ref, b2_ref, lg_ref, lb_ref,
             xb_ref, ag_ref, xhc_ref, rsc_ref, xh1_ref, rs1_ref, w1_s, w2_s, ubuf, cv_s):
        i = pl.program_id(0)

        @pl.when(i == 0)
        def _():
            _load_weight(wb_ref, lay, "pw1t", w1_s)
            _load_weight(wb_ref, lay, "pw2", w2_s)

        xv = x_ref[...]
        xb = xv.astype(BF16)
        xb_ref[...] = xb
        xcat = jnp.concatenate([xh_ref[...].astype(BF16), xb], axis=0)
        for mc in range(d // CONV_MC):
            c0 = mc * CONV_MC
            acols, gcols = slice(c0, c0 + CONV_MC), slice(d + c0, d + c0 + CONV_MC)
            ha = _dot_nt(xcat, _wrows(w1_s, c0, CONV_MC)) + b1_ref[:, acols]
            hg = _dot_nt(xcat, _wrows(w1_s, d + c0, CONV_MC)) + b1_ref[:, gcols]
            ag_ref[:, acols] = ha[HALO:].astype(BF16)
            ag_ref[:, gcols] = hg[HALO:].astype(BF16)
            u = ha * _sigmoid(hg)
            u = jnp.concatenate([jnp.where(i > 0, u[:HALO], 0.0), u[HALO:], jnp.zeros((8, CONV_MC), F32)], axis=0)
            for r in range(8):
                ubuf[r, :, acols] = _shifted(u, r)
            for rb in range(tm // CONV_RB):
                t0 = rb * CONV_RB
                for lc in range(CONV_MC // CONV_LC):
                    lanes = slice(c0 + lc * CONV_LC, c0 + (lc + 1) * CONV_LC)
                    acc = jnp.zeros((CONV_RB, CONV_LC), F32)
                    for k in range(CONV_WIDTH):
                        s = HALO - (CONV_WIDTH - 1) + k
                        q = t0 + 8 * (s // 8)
                        acc = acc + ubuf[s % 8, q:q + CONV_RB, lanes] * wdw_ref[k:k + 1, lanes]
                    cv_s[t0:t0 + CONV_RB, lanes] = acc
        cv = cv_s[...] + bdw_ref[...]
        xhc, rsc = _ln_fwd(cv)
        xhc_ref[...] = xhc
        rsc_ref[...] = rsc
        n = xhc * cg_ref[...] + cb_ref[...]
        s_act = n * _sigmoid(n)
        m = _dot(s_act.astype(BF16), _wfull(w2_s)) + b2_ref[...]
        xh1, rs1 = _ln_fwd(ALPHA * xv + m)
        xh1_ref[...] = xh1
        rs1_ref[...] = rs1

    hb = tm // HALO
    return _gridded_call(
        body, "conv_fwd", nsteps,
        [_tile(tm, d), pl.BlockSpec((HALO, d), lambda i: (jnp.maximum(i * hb - 1, 0), 0)), ANY,
         _fixed(HALO, d), _row(2 * d), _row(d), _row(d), _row(d), _row(d), _row(d), _row(d)],
        [_tile(tm, d), _tile(tm, 2 * d), _tile(tm, d), _tile(tm, 1), _tile(tm, d), _tile(tm, 1)],
        [jax.ShapeDtypeStruct((t, d), BF16), jax.ShapeDtypeStruct((t, 2 * d), BF16),
         jax.ShapeDtypeStruct((t, d), F32), jax.ShapeDtypeStruct((t, 1), F32),
         jax.ShapeDtypeStruct((t, d), F32), jax.ShapeDtypeStruct((t, 1), F32)],
        [_wscratch(lay, "pw1t"), _wscratch(lay, "pw2"),
         pltpu.VMEM((8, HALO + tm + 8, d), F32), pltpu.VMEM((tm, d), F32)],
        (x, x, wb, w_dw, b_pw1, b_dw, cg, cb, b_pw2, lg, lb), hosted)


def _conv_bwd1(dz1, xhc, rsc, wb, lay, cg, cb, tm):
    t, d = dz1.shape

    def body(dz_ref, xhc_ref, rsc_ref, wb_ref, cg_ref, cb_ref, dzb_ref, s_ref, dcv_ref, st_ref, w2_s):
        i = pl.program_id(0)

        @pl.when(i == 0)
        def _():
            _load_weight(wb_ref, lay, "pw2", w2_s)
            st_ref[...] = jnp.zeros(st_ref.shape, F32)

        dz = dz_ref[...]
        dzb = dz.astype(BF16)
        dzb_ref[...] = dzb
        xhc_v = xhc_ref[...]
        n = xhc_v * cg_ref[...] + cb_ref[...]
        sg = _sigmoid(n)
        s_ref[...] = (n * sg).astype(BF16)
        ds = _dot_nt(dzb, _wfull(w2_s))
        dn = ds * (sg * (1.0 + n * (1.0 - sg)))
        dcv, dg, db = _ln_bwd(dn, xhc_v, rsc_ref[...], cg_ref[...])
        dcv_ref[...] = dcv
        st_ref[0:1, :] += dg
        st_ref[1:2, :] += db
        st_ref[2:3, :] += jnp.sum(dcv, axis=0, keepdims=True)
        st_ref[3:4, :] += jnp.sum(dz, axis=0, keepdims=True)

    return pl.pallas_call(
        body, name="conv_bwd1", grid=(t // tm,),
        in_specs=[_tile(tm, d), _tile(tm, d), _tile(tm, 1), ANY, _row(d), _row(d)],
        out_specs=[_tile(tm, d), _tile(tm, d), _tile(tm, d), _fixed(8, d)],
        out_shape=[jax.ShapeDtypeStruct((t, d), BF16), jax.ShapeDtypeStruct((t, d), BF16),
                   jax.ShapeDtypeStruct((t, d), F32), jax.ShapeDtypeStruct((8, d), F32)],
        scratch_shapes=[_wscratch(lay, "pw2")],
        compiler_params=_params(),
    )(dz1, xhc, rsc, wb, cg, cb)


def _conv_bwd2(dz1, dcv, ag, wb, lay, w_dw, tm, hosted=None):
    t, d = dz1.shape
    nsteps = t // tm

    def body(dz_ref, dcv_ref, dcvn_ref, ag_ref, wb_ref, wdw_ref,
             gx_ref, dh_ref, dw_ref, db1_ref, w1_s, ubuf, dbuf, du_s, dwacc):
        i = pl.program_id(0)

        @pl.when(i == 0)
        def _():
            _load_weight(wb_ref, lay, "pw1t", w1_s)
            dwacc[...] = jnp.zeros(dwacc.shape, F32)
            db1_ref[...] = jnp.zeros(db1_ref.shape, F32)

        gx = ALPHA * dz_ref[...]
        for mc in range(d // CONV_MC):
            c0 = mc * CONV_MC
            acols, gcols = slice(c0, c0 + CONV_MC), slice(d + c0, d + c0 + CONV_MC)
            a = ag_ref[:, acols].astype(F32)
            sg = _sigmoid(ag_ref[:, gcols].astype(F32))
            ubuf[:, acols] = a * sg
            dcv_next = jnp.where(i < nsteps - 1, dcvn_ref[:, acols], 0.0)
            dcv_c = jnp.concatenate([dcv_ref[:, acols], dcv_next, jnp.zeros((8, CONV_MC), F32)], axis=0)
            for r in range(8):
                dbuf[r, :, acols] = _shifted(dcv_c, r)
            for rb in range(tm // CONV_RB):
                t0 = rb * CONV_RB
                for lc in range(CONV_MC // CONV_LC):
                    lanes = slice(c0 + lc * CONV_LC, c0 + (lc + 1) * CONV_LC)
                    ucur = ubuf[t0:t0 + CONV_RB, lanes]
                    acc = jnp.zeros((CONV_RB, CONV_LC), F32)
                    for k in range(CONV_WIDTH):
                        sd = CONV_WIDTH - 1 - k
                        q = t0 + 8 * (sd // 8)
                        dk = dbuf[sd % 8, q:q + CONV_RB, lanes]
                        acc = acc + dk * wdw_ref[k:k + 1, lanes]
                        prod = ucur * dk
                        part = prod[0:8]
                        for j in range(1, CONV_RB // 8):
                            part = part + prod[8 * j:8 * j + 8]
                        dwacc[k, :, lanes] += part
                    du_s[t0:t0 + CONV_RB, lanes] = acc
            du = du_s[:, acols]
            da = du * sg
            dg = du * a * sg * (1.0 - sg)
            dab, dgb = da.astype(BF16), dg.astype(BF16)
            dh_ref[:, acols] = dab
            dh_ref[:, gcols] = dgb
            db1_ref[:, acols] += jnp.sum(da, axis=0, keepdims=True)
            db1_ref[:, gcols] += jnp.sum(dg, axis=0, keepdims=True)
            gx = gx + _dot(dab, _wrows(w1_s, c0, CONV_MC)) + _dot(dgb, _wrows(w1_s, d + c0, CONV_MC))
        gx_ref[...] = gx

        @pl.when(i == nsteps - 1)
        def _():
            dw_ref[...] = jnp.sum(dwacc[...], axis=1)

    hb = tm // HALO
    last = t // HALO - 1
    return _gridded_call(
        body, "conv_bwd2", nsteps,
        [_tile(tm, d), _tile(tm, d),
         pl.BlockSpec((HALO, d), lambda i: (jnp.minimum((i + 1) * hb, last), 0)),
         _tile(tm, 2 * d), ANY, _fixed(HALO, d)],
        [_tile(tm, d), _tile(tm, 2 * d), _fixed(HALO, d), _row(2 * d)],
        [jax.ShapeDtypeStruct((t, d), F32), jax.ShapeDtypeStruct((t, 2 * d), BF16),
         jax.ShapeDtypeStruct((HALO, d), F32), jax.ShapeDtypeStruct((1, 2 * d), F32)],
        [_wscratch(lay, "pw1t"), pltpu.VMEM((tm, d), F32),
         pltpu.VMEM((8, HALO + tm + 8, d), F32), pltpu.VMEM((tm, d), F32),
         pltpu.VMEM((HALO, 8, d), F32)],
        (dz1, dcv, dcv, ag, wb, w_dw), hosted)


FFN_FC = 256
FFN_AHEAD = 1


def _ffn_fwd(xh_in, g_in, b_in, wb, lay, layer, tm, *, kv=None, loss=None):
    t, d = xh_in.shape
    f = lay.f
    names = (f"gt{layer}", f"ut{layer}", f"dn{layer}")

    def body(*refs):
        xh_ref, gi_ref, bi_ref, wb_ref = refs[:4]
        pos = 4
        if kv is not None:
            go_ref, bo_ref, wkv_ref, bkv_ref = refs[pos:pos + 4]
            pos += 4
        if loss is not None:
            go_ref, bo_ref, tgt_ref = refs[pos:pos + 3]
            pos += 3
        xb_ref, hg_ref, hu_ref = refs[pos:pos + 3]
        pos += 3
        if kv is not None:
            xho_ref, rso_ref, xob_ref, kv_ref = refs[pos:pos + 4]
            pos += 4
        if loss is not None:
            dz_ref, st_ref, loss_ref = refs[pos:pos + 3]
            pos += 3
        gt_s, ut_s, dn_s, xin_s, fo_s = refs[pos:pos + 5]
        i = pl.program_id(0)

        @pl.when(i == 0)
        def _():
            for name, dst in zip(names, (gt_s, ut_s, dn_s)):
                _load_weight(wb_ref, lay, name, dst)
            xin_s[...] = jnp.zeros(xin_s.shape, F32)
            fo_s[...] = jnp.zeros(fo_s.shape, F32)
            if loss is not None:
                st_ref[...] = jnp.zeros(st_ref.shape, F32)
                loss_ref[...] = jnp.zeros(loss_ref.shape, F32)

        xin_prev = xin_s[...]
        xho, rso = _ln_fwd(ALPHA * xin_prev + fo_s[...])
        if kv is not None:
            xho_ref[...] = xho
            rso_ref[...] = rso
            xob_ref[...] = (xho * go_ref[...] + bo_ref[...]).astype(BF16)
        if loss is not None:
            real = i > 0
            diff = xho * go_ref[...] + bo_ref[...] - tgt_ref[...]
            loss_ref[...] += jnp.where(real, (0.5 / d) * jnp.sum(diff * diff), 0.0)
            dz, dg, db = _ln_bwd(diff * (1.0 / d), xho, rso, go_ref[...])
            dz_ref[...] = dz
            st_ref[0:1, :] += jnp.where(real, dg, 0.0)
            st_ref[1:2, :] += jnp.where(real, db, 0.0)

        xin = xh_ref[...] * gi_ref[...] + bi_ref[...]
        xb = xin.astype(BF16)
        xb_ref[...] = xb

        def up(c):
            return (_dot_nt(xb, _wrows(gt_s, c * FFN_FC, FFN_FC)), _dot_nt(xb, _wrows(ut_s, c * FFN_FC, FFN_FC)))

        fo = jnp.zeros((tm, d), F32)
        nc = f // FFN_FC
        ahead = [up(c) for c in range(min(FFN_AHEAD, nc))]
        for c in range(nc):
            rows = slice(c * FFN_FC, (c + 1) * FFN_FC)
            hg, hu = ahead.pop(0)
            if c + FFN_AHEAD < nc:
                ahead.append(up(c + FFN_AHEAD))
            hg_ref[:, rows] = hg.astype(BF16)
            hu_ref[:, rows] = hu.astype(BF16)
            act = hg * _sigmoid(hg) * hu
            fo = fo + _dot(act.astype(BF16), _wrows(dn_s, c * FFN_FC, FFN_FC))
        xin_s[...] = xin
        fo_s[...] = fo
        if kv is not None:
            kv_ref[...] = (_dot(xob_ref[...], wkv_ref[...]) + bkv_ref[...]).astype(BF16)

    nsteps = t // tm
    in_specs = [_tile_cur(tm, d, nsteps), _row(d), _row(d), ANY]
    args = [xh_in, g_in, b_in, wb]
    out_specs = [_tile_cur(tm, d, nsteps), _tile_cur(tm, f, nsteps), _tile_cur(tm, f, nsteps)]
    out_shape = [jax.ShapeDtypeStruct((t, d), BF16), jax.ShapeDtypeStruct((t, f), BF16),
                 jax.ShapeDtypeStruct((t, f), BF16)]
    if kv is not None:
        in_specs += [_row(d), _row(d), _fixed(d, 2 * BLOCK), _row(2 * BLOCK)]
        args += list(kv)
        out_specs += [_tile_prev(tm, d), _tile_prev(tm, 1), _tile_prev(tm, d), _tile_prev(tm, 2 * BLOCK)]
        out_shape += [jax.ShapeDtypeStruct((t, d), F32), jax.ShapeDtypeStruct((t, 1), F32),
                      jax.ShapeDtypeStruct((t, d), BF16), jax.ShapeDtypeStruct((t, 2 * BLOCK), BF16)]
    if loss is not None:
        in_specs += [_row(d), _row(d), _tile_prev(tm, d)]
        args += list(loss)
        out_specs += [_tile_prev(tm, d), _fixed(8, d), _fixed(8, 128)]
        out_shape += [jax.ShapeDtypeStruct((t, d), F32), jax.ShapeDtypeStruct((8, d), F32),
                      jax.ShapeDtypeStruct((8, 128), F32)]
    return pl.pallas_call(
        body, name=f"ffn_fwd{layer}", grid=(nsteps + 1,), in_specs=in_specs, out_specs=out_specs,
        out_shape=out_shape,
        scratch_shapes=[_wscratch(lay, n) for n in names] + [pltpu.VMEM((tm, d), F32), pltpu.VMEM((tm, d), F32)],
        compiler_params=_params(),
    )(*args)


def _ffn_bwd(dz, hg, hu, xh_in, rs_in, g_in, wb, lay, layer, tm, hosted=None, qkv=None):
    t, d = dz.shape
    f = lay.f
    nsteps = t // tm
    nbt = tm // BLOCK
    names = (f"gt{layer}", f"ut{layer}", f"dn{layer}")

    def body(*refs):
        dz_ref, hg_ref, hu_ref, xh_ref, rs_ref, gi_ref, wb_ref = refs[:7]
        pos = 7
        if qkv is not None:
            dq_ref, dkc_ref, dkp_ref, dkn_ref, xho_ref, rso_ref, go_ref, wkv_ref = refs[pos:pos + 8]
            pos += 8
        dzb_ref, act_ref, dhg_ref, dhu_ref, dzp_ref, st_ref = refs[pos:pos + 6]
        pos += 6
        if qkv is not None:
            dkv_ref, sto_ref, dbkv_ref = refs[pos:pos + 3]
            pos += 3
        gt_s, ut_s, dn_s = refs[pos:pos + 3]
        i = pl.program_id(0)

        @pl.when(i == 0)
        def _():
            for name, dst in zip(names, (gt_s, ut_s, dn_s)):
                _load_weight(wb_ref, lay, name, dst)
            st_ref[...] = jnp.zeros(st_ref.shape, F32)
            if qkv is not None:
                _load_weight(wb_ref, lay, "wq", refs[pos + 3])
                sto_ref[...] = jnp.zeros(sto_ref.shape, F32)
                dbkv_ref[...] = jnp.zeros(dbkv_ref.shape, F32)

        if qkv is None:
            dzv = dz_ref[...]
        else:
            nxt = jnp.where(i < nsteps - 1, dkn_ref[...], 0.0)
            shifted = jnp.concatenate([dkp_ref[pl.ds(BLOCK, tm - BLOCK), :], nxt], axis=0) if nbt > 1 else nxt
            dkv = dkc_ref[...] + shifted
            dkvb = dkv.astype(BF16)
            dkv_ref[...] = dkvb
            dbkv_ref[...] += jnp.sum(dkv, axis=0, keepdims=True)
            dxo = (ALPHA * dz_ref[...] + _dot_nt(dq_ref[...], refs[pos + 3][...])
                   + _dot_nt(dkvb, wkv_ref[...]))
            dzv, dgo, dbo = _ln_bwd(dxo, xho_ref[...], rso_ref[...], go_ref[...])
            sto_ref[0:1, :] += dgo
            sto_ref[1:2, :] += dbo
        dzb = dzv.astype(BF16)
        dzb_ref[...] = dzb
        dx = ALPHA * dzv
        def back(c):
            return _dot_nt(dzb, _wrows(dn_s, c * FFN_FC, FFN_FC))

        nc = f // FFN_FC
        ahead = [back(c) for c in range(min(FFN_AHEAD, nc))]
        for c in range(nc):
            rows = slice(c * FFN_FC, (c + 1) * FFN_FC)
            dact = ahead.pop(0)
            if c + FFN_AHEAD < nc:
                ahead.append(back(c + FFN_AHEAD))
            hg_v = hg_ref[:, rows].astype(F32)
            hu_v = hu_ref[:, rows].astype(F32)
            sg = _sigmoid(hg_v)
            silu = hg_v * sg
            act_ref[:, rows] = (silu * hu_v).astype(BF16)
            dhu = (dact * silu).astype(BF16)
            dhg = (dact * hu_v * (sg * (1.0 + hg_v * (1.0 - sg)))).astype(BF16)
            dhu_ref[:, rows] = dhu
            dhg_ref[:, rows] = dhg
            dx = (dx + _dot(dhg, _wrows(gt_s, c * FFN_FC, FFN_FC))
                  + _dot(dhu, _wrows(ut_s, c * FFN_FC, FFN_FC)))
        dzp, dg, db = _ln_bwd(dx, xh_ref[...], rs_ref[...], gi_ref[...])
        dzp_ref[...] = dzp
        st_ref[0:1, :] += dg
        st_ref[1:2, :] += db

    in_specs = [_tile(tm, d), _tile(tm, f), _tile(tm, f), _tile(tm, d), _tile(tm, 1), _row(d), ANY]
    args = [dz, hg, hu, xh_in, rs_in, g_in, wb]
    out_specs = [_tile(tm, d), _tile(tm, f), _tile(tm, f), _tile(tm, f), _tile(tm, d), _fixed(8, d)]
    out_shape = [jax.ShapeDtypeStruct((t, d), BF16), jax.ShapeDtypeStruct((t, f), BF16),
                 jax.ShapeDtypeStruct((t, f), BF16), jax.ShapeDtypeStruct((t, f), BF16),
                 jax.ShapeDtypeStruct((t, d), F32), jax.ShapeDtypeStruct((8, d), F32)]
    scratch = [_wscratch(lay, n) for n in names]
    if qkv is not None:
        dq, dkc, dkp, xh_out, rs_out, g_out, wkv = qkv
        last = t // BLOCK - 1
        in_specs += [_tile(tm, d), _tile(tm, 2 * BLOCK), _tile(tm, 2 * BLOCK),
                     pl.BlockSpec((BLOCK, 2 * BLOCK), lambda i: (jnp.minimum((i + 1) * nbt, last), 0)),
                     _tile(tm, d), _tile(tm, 1), _row(d), _fixed(d, 2 * BLOCK)]
        args += [dq, dkc, dkp, dkp, xh_out, rs_out, g_out, wkv]
        out_specs += [_tile(tm, 2 * BLOCK), _fixed(8, d), _row(2 * BLOCK)]
        out_shape += [jax.ShapeDtypeStruct((t, 2 * BLOCK), BF16), jax.ShapeDtypeStruct((8, d), F32),
                      jax.ShapeDtypeStruct((1, 2 * BLOCK), F32)]
        scratch.append(_wscratch(lay, "wq"))
    return _gridded_call(body, f"ffn_bwd{layer}", nsteps, in_specs, out_specs, out_shape, scratch, args, hosted)


def _alibi_slope(h, nq):
    return 2.0 ** (-ALIBI_MAX * (h + 1) / nq)


def _fill_alibi_bias(bias_s, nq, keys_on_rows=False):
    shape = (2 * BLOCK, BLOCK) if keys_on_rows else (BLOCK, 2 * BLOCK)
    qi = lax.broadcasted_iota(jnp.int32, shape, 1 if keys_on_rows else 0)
    kj = lax.broadcasted_iota(jnp.int32, shape, 0 if keys_on_rows else 1)
    delta = qi + BLOCK - kj
    valid = (delta >= 0) & (delta < BLOCK)
    dist = jnp.where(valid, delta.astype(F32), MASKED_DIST)
    dist_first = jnp.where(kj >= BLOCK, dist, MASKED_DIST)
    for h in range(nq):
        bias_s[0, h] = _alibi_slope(h, nq) * dist
        bias_s[1, h] = _alibi_slope(h, nq) * dist_first


def _padded_kv(kvb, kvh, transposed_v=False):
    lane = lax.broadcasted_iota(jnp.int32, (2 * BLOCK, BLOCK), 1)
    mine = (lane < HEAD_DIM) if kvh == 0 else (lane >= HEAD_DIM)
    out = []
    for sec, transposed in ((kvb[:, :BLOCK], False), (kvb[:, BLOCK:], transposed_v)):
        m = jnp.where(mine, sec.astype(F32), 0.0)
        sw = pltpu.roll(m, HEAD_DIM, 1)
        pair = (m, sw) if kvh == 0 else (sw, m)
        out.append(tuple((p.T if transposed else p).astype(BF16) for p in pair))
    return out


def _attn_fwd(xh_in, g_in, b_in, x_in_b, kvs, wb, lay, bq, sinks, bo, tm):
    t, d = xh_in.shape
    nq = d // HEAD_DIM
    pairs_per_kv = (d // BLOCK) // N_KV_HEADS
    nbt = tm // BLOCK
    scale = HEAD_DIM ** -0.5

    def body(xh_ref, gi_ref, bi_ref, xb_ref, kv_ref, kvp_ref, wb_ref, bq_ref, sk_ref, bo_ref,
             q_ref, o_ref, lse_ref, xho_ref, rso_ref, wq_s, wo_s, kvall, q_s, o_s, bias_s):
        i = pl.program_id(0)

        @pl.when(i == 0)
        def _():
            _load_weight(wb_ref, lay, "wq", wq_s)
            _load_weight(wb_ref, lay, "wo", wo_s)
            _fill_alibi_bias(bias_s, nq, keys_on_rows=True)

        qv = ((_dot(xb_ref[...], _wfull(wq_s)) + bq_ref[...]) * scale).astype(BF16)
        q_s[...] = qv
        q_ref[...] = qv
        kvall[pl.ds(0, BLOCK), :] = kvp_ref[...]
        kvall[pl.ds(BLOCK, tm), :] = kv_ref[...]
        head_row = lax.broadcasted_iota(jnp.int32, (BLOCK, BLOCK), 0)

        def score_phase(j):
            rows = slice(j * BLOCK, (j + 1) * BLOCK)
            kvb = kvall[j * BLOCK:(j + 2) * BLOCK, :]
            first = (i * nbt + j == 0).astype(jnp.int32)
            pads = [_padded_kv(kvb, kvh, transposed_v=True) for kvh in range(N_KV_HEADS)]
            scores = []
            for a in range(d // BLOCK):
                kpad = pads[a // pairs_per_kv][0]
                qp = q_s[rows, a * BLOCK:(a + 1) * BLOCK]
                for e in range(2):
                    scores.append(_dot_nt(kpad[e], qp) - bias_s[first, 2 * a + e])
            return rows, pads, scores

        def softmax_phase(state):
            rows, pads, scores = state
            probs, inv = [], []
            lse_t = jnp.zeros((BLOCK, BLOCK), F32)
            for h in range(nq):
                sink = sk_ref[:, h:h + 1]
                m = jnp.maximum(jnp.max(scores[h], axis=0, keepdims=True), sink)
                p = jnp.exp(scores[h] - m)
                l = jnp.sum(p, axis=0, keepdims=True) + jnp.exp(sink - m)
                lse_t = jnp.where(head_row == h, m + jnp.log(l), lse_t)
                probs.append(p.astype(BF16))
                inv.append(1.0 / l)
            lse_ref[rows, :] = lse_t.T
            return rows, pads, probs, inv

        def value_phase(state):
            rows, pads, probs, inv = state
            for a in range(d // BLOCK):
                vpad_t = pads[a // pairs_per_kv][1]
                opair_t = (_dot(vpad_t[0], probs[2 * a]) * inv[2 * a]
                           + _dot(vpad_t[1], probs[2 * a + 1]) * inv[2 * a + 1])
                o_s[rows, a * BLOCK:(a + 1) * BLOCK] = opair_t.T.astype(BF16)

        for state in [softmax_phase(s) for s in [score_phase(j) for j in range(nbt)]]:
            value_phase(state)
        ov = o_s[...]
        o_ref[...] = ov
        xin = xh_ref[...] * gi_ref[...] + bi_ref[...]
        xho, rso = _ln_fwd(ALPHA * xin + _dot(ov, _wfull(wo_s)) + bo_ref[...])
        xho_ref[...] = xho
        rso_ref[...] = rso

    return pl.pallas_call(
        body, name="attn_fwd", grid=(t // tm,),
        in_specs=[_tile(tm, d), _row(d), _row(d), _tile(tm, d), _tile(tm, 2 * BLOCK),
                  pl.BlockSpec((BLOCK, 2 * BLOCK), lambda i: (jnp.maximum(i * nbt - 1, 0), 0)),
                  ANY, _row(d), _row(nq), _row(d)],
        out_specs=[_tile(tm, d), _tile(tm, d), _tile(tm, BLOCK), _tile(tm, d), _tile(tm, 1)],
        out_shape=[jax.ShapeDtypeStruct((t, d), BF16), jax.ShapeDtypeStruct((t, d), BF16),
                   jax.ShapeDtypeStruct((t, BLOCK), F32), jax.ShapeDtypeStruct((t, d), F32),
                   jax.ShapeDtypeStruct((t, 1), F32)],
        scratch_shapes=[_wscratch(lay, "wq"), _wscratch(lay, "wo"),
                        pltpu.VMEM((BLOCK + tm, 2 * BLOCK), BF16), pltpu.VMEM((tm, d), BF16),
                        pltpu.VMEM((tm, d), BF16), pltpu.VMEM((2, nq, 2 * BLOCK, BLOCK), F32)],
        compiler_params=_params(),
    )(xh_in, g_in, b_in, x_in_b, kvs, kvs, wb, bq, sinks, bo)


def _attn_bwd(dz, q, o, lse, kvs, wb, lay, sinks, tm, hosted=None):
    t, d = dz.shape
    nq = d // HEAD_DIM
    pairs_per_kv = (d // BLOCK) // N_KV_HEADS
    nbt = tm // BLOCK
    scale = HEAD_DIM ** -0.5

    def body(dz_ref, q_ref, o_ref, lse_ref, kv_ref, kvp_ref, wb_ref, sk_ref,
             dzb_ref, dq_ref, dkc_ref, dkp_ref, st_ref, dsk_ref, wo_s, kvall, do_s, dq_s, bias_s):
        i = pl.program_id(0)

        @pl.when(i == 0)
        def _():
            _load_weight(wb_ref, lay, "wo", wo_s)
            _fill_alibi_bias(bias_s, nq, keys_on_rows=True)
            st_ref[...] = jnp.zeros(st_ref.shape, F32)
            dsk_ref[...] = jnp.zeros(dsk_ref.shape, F32)

        dzv = dz_ref[...]
        dzb = dzv.astype(BF16)
        dzb_ref[...] = dzb
        do_s[...] = _dot_nt(dzb, _wfull(wo_s))
        kvall[pl.ds(0, BLOCK), :] = kvp_ref[...]
        kvall[pl.ds(BLOCK, tm), :] = kv_ref[...]
        lane = lax.broadcasted_iota(jnp.int32, (BLOCK, BLOCK), 1)
        lane1 = lax.broadcasted_iota(jnp.int32, (1, BLOCK), 1)
        lane2 = lax.broadcasted_iota(jnp.int32, (2 * BLOCK, BLOCK), 1)
        halves = (lane < HEAD_DIM, lane >= HEAD_DIM)
        sel_row = lax.broadcasted_iota(jnp.int32, (8, BLOCK), 0)
        sel_lane = lax.broadcasted_iota(jnp.int32, (8, BLOCK), 1)
        head_sel = jnp.where((sel_row == 0) & (sel_lane < HEAD_DIM) | (sel_row == 1) & (sel_lane >= HEAD_DIM),
                             1.0, 0.0).astype(BF16)

        def score_phase(j):
            rows = slice(j * BLOCK, (j + 1) * BLOCK)
            kvb = kvall[j * BLOCK:(j + 2) * BLOCK, :]
            first = (i * nbt + j == 0).astype(jnp.int32)
            pads = [_padded_kv(kvb, kvh) for kvh in range(N_KV_HEADS)]
            scores, dps, dhs, qms, doms = [], [], [], [], []
            for a in range(d // BLOCK):
                kpad, vpad = pads[a // pairs_per_kv]
                cols = slice(a * BLOCK, (a + 1) * BLOCK)
                qp = q_ref[rows, cols]
                dop = do_s[rows, cols]
                dopb = dop.astype(BF16)
                prod = dop * o_ref[rows, cols].astype(F32)
                hi = prod.astype(BF16)
                lo = (prod - hi.astype(F32)).astype(BF16)
                dh_pair = _dot_nt(head_sel, hi) + _dot_nt(head_sel, lo)
                for e in range(2):
                    scores.append(_dot_nt(kpad[e], qp) - bias_s[first, 2 * a + e])
                    dps.append(_dot_nt(vpad[e], dopb))
                    dhs.append(dh_pair[e:e + 1, :])
                    qms.append(jnp.where(halves[e], qp, jnp.zeros_like(qp)))
                    doms.append(jnp.where(halves[e], dopb, jnp.zeros_like(dopb)))
            return rows, pads, scores, dps, dhs, qms, doms

        def softmax_phase(state):
            rows, pads, scores, dps, dhs, qms, doms = state
            dss, pbs = [], []
            dsk_t = jnp.zeros((1, BLOCK), F32)
            lse_t = lse_ref[rows, :].T
            for h in range(nq):
                lse_h = lse_t[h:h + 1, :]
                p = jnp.exp(scores[h] - lse_h)
                dss.append((p * (dps[h] - dhs[h])).astype(BF16))
                pbs.append(p.astype(BF16))
                dsink = -jnp.sum(jnp.exp(sk_ref[:, h:h + 1] - lse_h) * dhs[h], axis=1, keepdims=True)
                dsk_t = jnp.where(lane1 == h, dsink, dsk_t)
            dsk_ref[...] += dsk_t
            return rows, pads, dss, pbs, qms, doms

        def grad_phase(state):
            rows, pads, dss, pbs, qms, doms = state
            dsecs = []
            for kvh in range(N_KV_HEADS):
                kpad_t = [p.astype(F32).T.astype(BF16) for p in pads[kvh][0]]
                dk_acc = jnp.zeros((2 * BLOCK, BLOCK), F32)
                dv_acc = jnp.zeros((2 * BLOCK, BLOCK), F32)
                for a in range(kvh * pairs_per_kv, (kvh + 1) * pairs_per_kv):
                    dqp_t = _dot(kpad_t[0], dss[2 * a]) + _dot(kpad_t[1], dss[2 * a + 1])
                    dq_s[rows, a * BLOCK:(a + 1) * BLOCK] = dqp_t.T * scale
                    for e in range(2):
                        h = 2 * a + e
                        dk_acc = dk_acc + _dot(dss[h], qms[h])
                        dv_acc = dv_acc + _dot(pbs[h], doms[h])
                dsecs.append((dk_acc + pltpu.roll(dk_acc, HEAD_DIM, 1), dv_acc + pltpu.roll(dv_acc, HEAD_DIM, 1)))
            lo = lane2 < HEAD_DIM
            dkv = jnp.concatenate([jnp.where(lo, dsecs[0][0], dsecs[1][0]),
                                   jnp.where(lo, dsecs[0][1], dsecs[1][1])], axis=1)
            dkp_ref[rows, :] = dkv[:BLOCK]
            dkc_ref[rows, :] = dkv[BLOCK:]

        for state in [softmax_phase(s) for s in [score_phase(j) for j in range(nbt)]]:
            grad_phase(state)
        dqv = dq_s[...]
        dq_ref[...] = dqv.astype(BF16)
        st_ref[0:1, :] += jnp.sum(dqv, axis=0, keepdims=True)
        st_ref[1:2, :] += jnp.sum(dzv, axis=0, keepdims=True)

    return _gridded_call(
        body, "attn_bwd", t // tm,
        [_tile(tm, d), _tile(tm, d), _tile(tm, d), _tile(tm, BLOCK), _tile(tm, 2 * BLOCK),
         pl.BlockSpec((BLOCK, 2 * BLOCK), lambda i: (jnp.maximum(i * nbt - 1, 0), 0)),
         ANY, _row(nq)],
        [_tile(tm, d), _tile(tm, d), _tile(tm, 2 * BLOCK), _tile(tm, 2 * BLOCK),
         _fixed(8, d), _row(BLOCK)],
        [jax.ShapeDtypeStruct((t, d), BF16), jax.ShapeDtypeStruct((t, d), BF16),
         jax.ShapeDtypeStruct((t, 2 * BLOCK), F32), jax.ShapeDtypeStruct((t, 2 * BLOCK), F32),
         jax.ShapeDtypeStruct((8, d), F32), jax.ShapeDtypeStruct((1, BLOCK), F32)],
        [_wscratch(lay, "wo"), pltpu.VMEM((BLOCK + tm, 2 * BLOCK), BF16),
         pltpu.VMEM((tm, d), F32), pltpu.VMEM((tm, d), F32),
         pltpu.VMEM((2, nq, 2 * BLOCK, BLOCK), F32)],
        (dz, q, o, lse, kvs, kvs, wb, sinks), hosted)


def _tn_matmul(a, b, name, bm, tk):
    t, m = a.shape
    n = b.shape[1]
    ksteps = t // tk

    nc = max(n // 256, 1)
    cw = n // nc

    def body(a_ref, b_ref, o_ref, acc):
        k = pl.program_id(1)

        @pl.when(k == 0)
        def _():
            acc[...] = jnp.zeros(acc.shape, F32)

        at = a_ref[...].T
        for c in range(nc):
            cols = slice(c * cw, (c + 1) * cw)
            acc[:, cols] += _dot(at, b_ref[:, cols])

        @pl.when(k == ksteps - 1)
        def _():
            o_ref[...] = acc[...].astype(BF16)

    return pl.pallas_call(
        body, name=name, grid=(m // bm, ksteps),
        in_specs=[pl.BlockSpec((tk, bm), lambda j, k: (k, j)), pl.BlockSpec((tk, n), lambda j, k: (k, 0))],
        out_specs=pl.BlockSpec((bm, n), lambda j, k: (j, 0)),
        out_shape=jax.ShapeDtypeStruct((m, n), BF16),
        scratch_shapes=[pltpu.VMEM((bm, n), F32)],
        compiler_params=pltpu.CompilerParams(dimension_semantics=("arbitrary", "arbitrary"),
                                             vmem_limit_bytes=VMEM_LIMIT),
    )(a, b)


def _all_gather(arrays, name):
    n = len(arrays)

    def body(*refs):
        ins, outs = refs[:n], refs[n:2 * n]
        send_sems, recv_sems, local_sems = refs[2 * n:]
        x, y, c = _me()
        me, sibling = (x, y, c), (x, y, 1 - c)
        chips = [(1 - x, y), (x, 1 - y), (1 - x, 1 - y)]

        def slot(ref, dev):
            return ref.at[4 * dev[0] + 2 * dev[1] + dev[2]]

        def copy(a, k, block, to, src=None):
            return pltpu.make_async_remote_copy(
                src_ref=slot(outs[a], block) if src is None else src, dst_ref=slot(outs[a], block),
                send_sem=send_sems.at[a, k], recv_sem=recv_sems.at[a, k], device_id=to, device_id_type=MESH)

        mine = [pltpu.make_async_copy(ins[a], slot(outs[a], me), local_sems.at[a]) for a in range(n)]
        for cp in mine:
            cp.start()
        first = []
        for a in range(n):
            first.append(copy(a, 0, me, sibling, src=ins[a]))
            first += [copy(a, 1 + j, me, (*chip, c), src=ins[a]) for j, chip in enumerate(chips)]
        for cp in first:
            cp.start()
        passed = []
        for a in range(n):
            for j, chip in enumerate(chips):
                copy(a, 1 + j, (*chip, c), me).wait_recv()
                cp = copy(a, 4 + j, (*chip, c), sibling)
                cp.start()
                passed.append(cp)
        for a in range(n):
            copy(a, 0, sibling, me).wait_recv()
            for j, chip in enumerate(chips):
                copy(a, 4 + j, (*chip, 1 - c), me).wait_recv()
        for cp in first + passed:
            cp.wait_send()
        for cp in mine:
            cp.wait()

    return pl.pallas_call(
        body, name=name, in_specs=[ANY] * n, out_specs=[ANY] * n,
        out_shape=[jax.ShapeDtypeStruct((N_DEV,) + a.shape, a.dtype) for a in arrays],
        scratch_shapes=[pltpu.SemaphoreType.DMA((n, 7)), pltpu.SemaphoreType.DMA((n, 7)),
                        pltpu.SemaphoreType.DMA((n,))],
    )(*arrays)


def _exchange(arrays, name):
    n = len(arrays)
    blocked = [a.ndim == 3 for a in arrays]

    def body(*refs):
        ins, outs = refs[:n], refs[n:2 * n]
        send_sems, recv_sems, local_sems = refs[2 * n:]
        me = _index(_me())

        def src(k, dev):
            return ins[k].at[dev] if blocked[k] else ins[k]

        local = [pltpu.make_async_copy(src(k, me), outs[k].at[me], local_sems.at[k]) for k in range(n)]
        sends, arrivals = [], []
        for k in range(n):
            for mask in range(1, N_DEV):
                peer = _peer(mask)
                sends.append(pltpu.make_async_remote_copy(
                    src_ref=src(k, _index(peer)), dst_ref=outs[k].at[me], send_sem=send_sems.at[k, mask - 1],
                    recv_sem=recv_sems.at[k, mask - 1], device_id=peer, device_id_type=MESH))
                arrivals.append(pltpu.make_async_remote_copy(
                    src_ref=src(k, me), dst_ref=outs[k].at[_index(peer)], send_sem=send_sems.at[k, mask - 1],
                    recv_sem=recv_sems.at[k, mask - 1], device_id=_me(), device_id_type=MESH))
        for cp in local + sends:
            cp.start()
        for cp in arrivals:
            cp.wait_recv()
        for cp in sends:
            cp.wait_send()
        for cp in local:
            cp.wait()

    return pl.pallas_call(
        body, name=name, in_specs=[ANY] * n, out_specs=[ANY] * n,
        out_shape=[jax.ShapeDtypeStruct((N_DEV,) + a.shape[-2:], a.dtype) for a in arrays],
        scratch_shapes=[pltpu.SemaphoreType.DMA((n, 7)), pltpu.SemaphoreType.DMA((n, 7)),
                        pltpu.SemaphoreType.DMA((n,))],
    )(*arrays)


def _adamw_update(g, w_ref, m_ref, v_ref, go_ref, d_ref, mo_ref, vo_ref):
    mn = ADAM_B1 * m_ref[...] + (1.0 - ADAM_B1) * g
    vn = ADAM_B2 * v_ref[...] + (1.0 - ADAM_B2) * (g * g)
    m_hat = mn / (1.0 - ADAM_B1 ** ADAM_STEP)
    v_hat = vn / (1.0 - ADAM_B2 ** ADAM_STEP)
    go_ref[...] = g
    d_ref[...] = -ADAM_LR * (m_hat / (jnp.sqrt(v_hat) + ADAM_EPS) + ADAM_WD * w_ref[...])
    mo_ref[...] = mn
    vo_ref[...] = vn


def _sum_sources(g_refs, layer):
    total = None
    for l, g_ref in enumerate(g_refs):
        g = g_ref[0].astype(F32)
        for s in range(1, N_DEV):
            g = g + g_ref[s].astype(F32)
        total = g if total is None else jnp.where(layer == l, g, total)
    return total


def _layer_block(l_mine, nblocks):
    def index(l, j):
        return (0, jnp.where(l == l_mine, j, jnp.where(l < l_mine, 0, nblocks - 1)), 0)
    return index


def _adamw_sum(g8, w, m, v, name, tr):
    r, width = w.shape

    def body(g_ref, *refs):
        _adamw_update(_sum_sources([g_ref], 0), *refs)

    spec = pl.BlockSpec((tr, width), lambda i: (i, 0))
    return pl.pallas_call(
        body, name=name, grid=(r // tr,),
        in_specs=[pl.BlockSpec((N_DEV, tr, width), lambda i: (0, i, 0)), spec, spec, spec],
        out_specs=[spec] * 4, out_shape=[jax.ShapeDtypeStruct((r, width), F32)] * 4,
        compiler_params=_params(),
    )(g8, w, m, v)


def _adamw_rows(g8s, w, m, v, name):
    layers, n, width = w.shape
    tr = max(r for r in range(BF16_SUBLANES, ADAMW_MAX_ROWS + 1, BF16_SUBLANES) if n % r == 0)
    nb = n // tr

    def body(*refs):
        _adamw_update(_sum_sources(refs[:layers], pl.program_id(0)), *refs[layers:])

    spec = pl.BlockSpec((None, tr, width), lambda l, j: (l, j, 0))
    return pl.pallas_call(
        body, name=name, grid=(layers, nb),
        in_specs=[pl.BlockSpec((N_DEV, tr, width), _layer_block(l, nb)) for l in range(layers)] + [spec] * 3,
        out_specs=[spec] * 4, out_shape=[jax.ShapeDtypeStruct(w.shape, F32)] * 4,
        compiler_params=pltpu.CompilerParams(dimension_semantics=("arbitrary", "arbitrary"),
                                             vmem_limit_bytes=VMEM_LIMIT),
    )(*g8s, w, m, v)


def _adamw_cols(g8s, w, m, v, name):
    layers, k, n = w.shape
    cb = min(BLOCK, n)
    nb = pl.cdiv(n, cb)

    def body(*refs):
        _adamw_update(_sum_sources(refs[:layers], pl.program_id(0)).T, *refs[layers:])

    spec = pl.BlockSpec((None, k, cb), lambda l, j: (l, 0, j))
    return pl.pallas_call(
        body, name=name, grid=(layers, nb),
        in_specs=[pl.BlockSpec((N_DEV, cb, k), _layer_block(l, nb)) for l in range(layers)] + [spec] * 3,
        out_specs=[spec] * 4, out_shape=[jax.ShapeDtypeStruct(w.shape, F32)] * 4,
        compiler_params=pltpu.CompilerParams(dimension_semantics=("arbitrary", "arbitrary"),
                                             vmem_limit_bytes=VMEM_LIMIT),
    )(*g8s, w, m, v)


def _local_step(x, target, wba, shard_b, lay, sm, tm, tk):
    t, d = x.shape
    f = lay.f
    w_dw32 = jnp.concatenate([sm["w_dw"], jnp.zeros((HALO - CONV_WIDTH, d), F32)], axis=0)
    lmg, lmb, lfg, lfb = sm["ln_mix_g"], sm["ln_mix_b"], sm["ln_ffn_g"], sm["ln_ffn_b"]
    bkv = jnp.concatenate([sm["b_k"], sm["b_v"]], axis=1)
    bm_f = f // 2 if (f // 2) % 128 == 0 else f
    tm_light = 2 * tm

    received = {}

    def exchange(grads):
        return _HostedExchange([g.reshape(N_DEV, lay.n[n], d) for n, g in grads.items()])

    def keep(grads, arrived):
        received.update(zip(grads, arrived))

    xb0, ag, xhc, rsc, xh1, rs1, wbb = _conv_fwd(x, wba, lay, w_dw32, sm["b_pw1"], sm["b_dw"], sm["cg"],
                                                 sm["cb"], sm["b_pw2"], lmg[0:1], lmb[0:1], tm,
                                                 hosted=_HostedGather(shard_b))
    wkv = wbb[:, lay.goff["wkv"]:lay.goff["wkv"] + lay.n["wkv"], :].reshape(d, 2 * BLOCK)
    x1b, hg0, hu0, xh2, rs2, x2b, kvs = _ffn_fwd(xh1, lmg[0:1], lmb[0:1], wbb, lay, 0, tm_light,
                                                kv=(lfg[0:1], lfb[0:1], wkv, bkv))
    q, o, lse, xh3, rs3 = _attn_fwd(xh2, lfg[0:1], lfb[0:1], x2b, kvs, wbb, lay, sm["b_q"], sm["sinks"],
                                    sm["b_o"], tm)
    x3b, hg1, hu1, dz4, st4, loss = _ffn_fwd(xh3, lmg[1:2], lmb[1:2], wbb, lay, 1, tm,
                                             loss=(lfg[1:2], lfb[1:2], target))

    dz4b, act1, dhg1, dhu1, dz3, st3 = _ffn_bwd(dz4, hg1, hu1, xh3, rs3, lmg[1:2], wbb, lay, 1, tm)
    g1 = {"gt1": _tn_matmul(dhg1, x3b, "dw_gate1", bm_f, tk), "ut1": _tn_matmul(dhu1, x3b, "dw_up1", bm_f, tk),
          "dn1": _tn_matmul(act1, dz4b, "dw_down1", bm_f, tk)}
    dz3b, dq, dkc, dkp, stq, dsinks, *arrived = _attn_bwd(dz3, q, o, lse, kvs, wbb, lay, sm["sinks"], tm,
                                                          hosted=exchange(g1))
    keep(g1, arrived)
    g2 = {"wq": _tn_matmul(x2b, dq, "dw_q", d, tk), "wo": _tn_matmul(o, dz3b, "dw_o", d, tk)}
    dz2b, act0, dhg0, dhu0, dz1, st1, dkv, st2, dbkv, *arrived = _ffn_bwd(
        dz3, hg0, hu0, xh1, rs1, lmg[0:1], wbb, lay, 0, tm, hosted=exchange(g2),
        qkv=(dq, dkc, dkp, xh2, rs2, lfg[0:1], wkv))
    keep(g2, arrived)
    dz1b, s_act, dcv, stc = _conv_bwd1(dz1, xhc, rsc, wba, lay, sm["cg"], sm["cb"], tm_light)
    g3 = {"gt0": _tn_matmul(dhg0, x1b, "dw_gate0", bm_f, tk), "ut0": _tn_matmul(dhu0, x1b, "dw_up0", bm_f, tk),
          "dn0": _tn_matmul(act0, dz2b, "dw_down0", bm_f, tk), "pw2": _tn_matmul(s_act, dz1b, "dw_pw2", d, tk),
          "wkv": _tn_matmul(x2b, dkv, "dw_kv", d, tk)}
    grad_x, dh1, dwdw, db1, *arrived = _conv_bwd2(dz1, dcv, ag, wba, lay, w_dw32, tm, hosted=exchange(g3))
    keep(g3, arrived)
    g_pw1t = _tn_matmul(dh1, xb0, "dw_pw1", d, tk)
    small = {
        "w_dw": dwdw[:CONV_WIDTH], "b_pw1": db1, "b_dw": stc[2:3], "cg": stc[0:1], "cb": stc[1:2],
        "b_pw2": stc[3:4], "b_k": dbkv[:, :BLOCK], "b_v": dbkv[:, BLOCK:], "b_q": stq[0:1],
        "sinks": dsinks[:, :d // HEAD_DIM],
        "b_o": stq[1:2],
        "ln_mix_g": jnp.concatenate([st1[0:1], st3[0:1]], axis=0),
        "ln_mix_b": jnp.concatenate([st1[1:2], st3[1:2]], axis=0),
        "ln_ffn_g": jnp.concatenate([st2[0:1], st4[0:1]], axis=0),
        "ln_ffn_b": jnp.concatenate([st2[1:2], st4[1:2]], axis=0),
    }
    return loss[0, 0], grad_x, received, g_pw1t, small


SP_ROWS = 40
SP_BDW, SP_CG, SP_CB, SP_BPW2, SP_BPW1 = 32, 33, 34, 35, 36
RP_NAMES = ("ln_mix_g", "ln_mix_b", "ln_ffn_g", "ln_ffn_b", "b_q", "b_o", "b_k", "b_v", "sinks")


def _row_forms(d, pw1, pw2, wq, wo, gate, up, down, wk, wv):
    rf = {"pw1t": pw1[0].T, "pw2": pw2[0], "wq": wq[0], "wo": wo[0],
          "wkv": jnp.concatenate([wk, wv], axis=1).reshape(-1, d)}
    for l in range(DEPTH):
        rf.update({f"gt{l}": gate[l].T, f"ut{l}": up[l].T, f"dn{l}": down[l]})
    return rf


def _pack_rows(rf, names):
    return jnp.concatenate([rf[n] for n in names], axis=0)


def _pack_small(w_dw, b_dw, cg, cb, b_pw2, b_pw1):
    cw = b_dw.shape[1]
    z = jnp.zeros((1, cw), F32)
    return jnp.concatenate([w_dw[0], z, b_dw, cg, cb, b_pw2, b_pw1.reshape(2, cw), z, z], axis=0)


def _unpack_small(p):
    cw = p.shape[1]
    return dict(w_dw=p[None, :CONV_WIDTH], b_dw=p[SP_BDW:SP_BDW + 1], cg=p[SP_CG:SP_CG + 1],
                cb=p[SP_CB:SP_CB + 1], b_pw2=p[SP_BPW2:SP_BPW2 + 1],
                b_pw1=p[SP_BPW1:SP_BPW1 + 2].reshape(1, 2 * cw))


def _small_full(g):
    d = N_DEV * g.shape[2]

    def wide(r0, n=1):
        return jnp.transpose(g[:, r0:r0 + n], (1, 0, 2)).reshape(n, d)

    return dict(w_dw=wide(0, CONV_WIDTH), b_dw=wide(SP_BDW), cg=wide(SP_CG), cb=wide(SP_CB),
                b_pw2=wide(SP_BPW2), b_pw1=g[:, SP_BPW1:SP_BPW1 + 2].reshape(1, 2 * d))


def _small_grad_blocks(sg):
    cw = sg["b_dw"].shape[1] // N_DEV

    def narrow(a):
        return jnp.transpose(a.reshape(a.shape[0], N_DEV, cw), (1, 0, 2))

    z = jnp.zeros((N_DEV, 1, cw), F32)
    return jnp.concatenate([narrow(sg["w_dw"]), z, narrow(sg["b_dw"]), narrow(sg["cg"]), narrow(sg["cb"]),
                            narrow(sg["b_pw2"]), sg["b_pw1"].reshape(N_DEV, 2, cw), z, z], axis=1)


def _pack_rep(vals, rider=0.0):
    parts = []
    for name in RP_NAMES:
        a = vals[name].reshape(-1)
        pad = -a.shape[0] % 128
        parts.append(jnp.concatenate([a, jnp.zeros((pad,), F32)]).reshape(-1, 128))
    parts.append(jnp.full((1, 128), rider, F32))
    rows = sum(p.shape[0] for p in parts)
    parts.append(jnp.zeros((-rows % 8, 128), F32))
    return jnp.concatenate(parts, axis=0)


def _rider_row(shapes):
    return sum(-(-_size(shapes[name]) // 128) for name in RP_NAMES)


def _size(shape):
    n = 1
    for s in shape:
        n *= s
    return n


def _unpack_rep(p, shapes):
    out, r = {}, 0
    for name in RP_NAMES:
        n = _size(shapes[name])
        rows = -(-n // 128)
        out[name] = p[r:r + rows].reshape(-1)[:n].reshape(shapes[name])
        r += rows
    return out


def kernel(x, conv_w_pw1, conv_b_pw1, conv_w_dw, conv_b_dw, conv_ln_g, conv_ln_b, conv_w_pw2, conv_b_pw2, kv_w_k, kv_b_k, kv_w_v, kv_b_v, attn_w_q, attn_b_q, attn_sinks, attn_w_o, attn_b_o, ffn_w_gate, ffn_w_up, ffn_w_down, ln_mix_g, ln_mix_b, ln_ffn_g, ln_ffn_b, loss_target, m_conv_w_pw1, m_conv_b_pw1, m_conv_w_dw, m_conv_b_dw, m_conv_ln_g, m_conv_ln_b, m_conv_w_pw2, m_conv_b_pw2, m_kv_w_k, m_kv_b_k, m_kv_w_v, m_kv_b_v, m_attn_w_q, m_attn_b_q, m_attn_sinks, m_attn_w_o, m_attn_b_o, m_ffn_w_gate, m_ffn_w_up, m_ffn_w_down, m_ln_mix_g, m_ln_mix_b, m_ln_ffn_g, m_ln_ffn_b, v_conv_w_pw1, v_conv_b_pw1, v_conv_w_dw, v_conv_b_dw, v_conv_ln_g, v_conv_ln_b, v_conv_w_pw2, v_conv_b_pw2, v_kv_w_k, v_kv_b_k, v_kv_w_v, v_kv_b_v, v_attn_w_q, v_attn_b_q, v_attn_sinks, v_attn_w_o, v_attn_b_o, v_ffn_w_gate, v_ffn_w_up, v_ffn_w_down, v_ln_mix_g, v_ln_mix_b, v_ln_ffn_g, v_ln_ffn_b):
    t, d = x.shape[1], x.shape[2]
    f = ffn_w_gate.shape[2] * N_DEV
    lay = _Layout(d, f)
    tm, tk = 256, min(2048, t)

    rep_shapes = dict(ln_mix_g=ln_mix_g.shape, ln_mix_b=ln_mix_b.shape, ln_ffn_g=ln_ffn_g.shape,
                      ln_ffn_b=ln_ffn_b.shape, b_q=attn_b_q.shape, b_o=attn_b_o.shape, b_k=kv_b_k.shape,
                      b_v=kv_b_v.shape, sinks=attn_sinks.shape)

    def rep_pack(lmg, lmb, lfg, lfb, bq, bo, bk, bv, sk):
        return _pack_rep(dict(ln_mix_g=lmg, ln_mix_b=lmb, ln_ffn_g=lfg, ln_ffn_b=lfb, b_q=bq, b_o=bo,
                              b_k=bk, b_v=bv, sinks=sk))

    w_rf = _row_forms(d, conv_w_pw1, conv_w_pw2, attn_w_q, attn_w_o, ffn_w_gate, ffn_w_up, ffn_w_down, kv_w_k, kv_w_v)
    w_small = _pack_small(conv_w_dw, conv_b_dw, conv_ln_g, conv_ln_b, conv_b_pw2, conv_b_pw1)
    m_small = _pack_small(m_conv_w_dw, m_conv_b_dw, m_conv_ln_g, m_conv_ln_b, m_conv_b_pw2, m_conv_b_pw1)
    v_small = _pack_small(v_conv_w_dw, v_conv_b_dw, v_conv_ln_g, v_conv_ln_b, v_conv_b_pw2, v_conv_b_pw1)
    w_rep = rep_pack(ln_mix_g, ln_mix_b, ln_ffn_g, ln_ffn_b, attn_b_q, attn_b_o, kv_b_k, kv_b_v, attn_sinks)
    m_rep = rep_pack(m_ln_mix_g, m_ln_mix_b, m_ln_ffn_g, m_ln_ffn_b, m_attn_b_q, m_attn_b_o, m_kv_b_k, m_kv_b_v, m_attn_sinks)
    v_rep = rep_pack(v_ln_mix_g, v_ln_mix_b, v_ln_ffn_g, v_ln_ffn_b, v_attn_b_q, v_attn_b_o, v_kv_b_k, v_kv_b_v, v_attn_sinks)

    wba, smg = _all_gather([_pack_rows(w_rf, lay.GATHER["a"]).astype(BF16), w_small], "gather_conv_weights")
    shard_b = _pack_rows(w_rf, lay.GATHER["b"]).astype(BF16)
    sm = _small_full(smg)
    sm.update(ln_mix_g=ln_mix_g, ln_mix_b=ln_mix_b, ln_ffn_g=ln_ffn_g, ln_ffn_b=ln_ffn_b, b_q=attn_b_q,
              b_o=attn_b_o, sinks=attn_sinks, b_k=kv_b_k.reshape(1, -1), b_v=kv_b_v.reshape(1, -1))

    loss_part, grad_x, received, g_pw1t, gsmall = _local_step(x[0], loss_target[0], wba, shard_b, lay, sm, tm, tk)

    received["pw1t"], g8_small, g8_rep = _exchange(
        [g_pw1t.reshape(N_DEV, lay.n["pw1t"], d), _small_grad_blocks(gsmall), _pack_rep(gsmall, loss_part)],
        "exchange_last_grads")

    def kv_rows(wk, wv):
        return jnp.concatenate([wk, wv], axis=1).reshape(1, -1, d)

    def kv_split(a):
        a = a.reshape(d // N_DEV, 2 * BLOCK)
        return a[:, :BLOCK], a[:, BLOCK:]

    big = dict(
        pw1=_adamw_cols([received["pw1t"]], conv_w_pw1, m_conv_w_pw1, v_conv_w_pw1, "adamw_pw1"),
        gate=_adamw_cols([received["gt0"], received["gt1"]], ffn_w_gate, m_ffn_w_gate, v_ffn_w_gate, "adamw_gate"),
        up=_adamw_cols([received["ut0"], received["ut1"]], ffn_w_up, m_ffn_w_up, v_ffn_w_up, "adamw_up"),
        down=_adamw_rows([received["dn0"], received["dn1"]], ffn_w_down, m_ffn_w_down, v_ffn_w_down, "adamw_down"),
        pw2=_adamw_rows([received["pw2"]], conv_w_pw2, m_conv_w_pw2, v_conv_w_pw2, "adamw_pw2"),
        wq=_adamw_rows([received["wq"]], attn_w_q, m_attn_w_q, v_attn_w_q, "adamw_q"),
        wo=_adamw_rows([received["wo"]], attn_w_o, m_attn_w_o, v_attn_w_o, "adamw_o"),
        wkv=[kv_split(a) for a in _adamw_rows([received["wkv"]], kv_rows(kv_w_k, kv_w_v), kv_rows(m_kv_w_k, m_kv_w_v),
                                              kv_rows(v_kv_w_k, v_kv_w_v), "adamw_kv")])
    big_out = [dict(pw1=big["pw1"][i], pw2=big["pw2"][i], wq=big["wq"][i], wo=big["wo"][i], gate=big["gate"][i],
                    up=big["up"][i], down=big["down"][i], wk=big["wkv"][i][0], wv=big["wkv"][i][1])
               for i in range(4)]
    small_out = [_unpack_small(a) for a in _adamw_sum(g8_small, w_small, m_small, v_small, "adamw_small", SP_ROWS)]
    rep_res = _adamw_sum(g8_rep, w_rep, m_rep, v_rep, "adamw_rep", w_rep.shape[0])
    rep_out = [_unpack_rep(a, rep_shapes) for a in rep_res]
    loss = rep_res[0][_rider_row(rep_shapes), 0]

    outs = [loss, grad_x[None]]
    for b, s, r in zip(big_out, small_out, rep_out):
        outs += [b["pw1"], s["b_pw1"], s["w_dw"], s["b_dw"], s["cg"], s["cb"], b["pw2"], s["b_pw2"],
                 b["wk"], r["b_k"], b["wv"], r["b_v"], b["wq"], r["b_q"], r["sinks"], b["wo"], r["b_o"],
                 b["gate"], b["up"], b["down"], r["ln_mix_g"], r["ln_mix_b"], r["ln_ffn_g"], r["ln_ffn_b"]]
    return tuple(outs)
```

```python
import jax
import jax.numpy as jnp
from jax import lax
from jax.experimental import pallas as pl
from jax.experimental.pallas import tpu as pltpu

F32 = jnp.float32
BF16 = jnp.bfloat16

N_DEV = 8
HEAD_DIM = 64
N_KV_HEADS = 2
BLOCK = 128
CONV_WIDTH = 31
HALO = 32
ALIBI_MAX = 8.0
DEPTH = 2
ALPHA = (2.0 * DEPTH) ** 0.25
LN_EPS = 1e-5
MASKED_DIST = 1e32
ADAM_LR = 0.001
ADAM_B1 = 0.9
ADAM_B2 = 0.999
ADAM_EPS = 1e-08
ADAM_WD = 0.01
ADAM_STEP = 10
VMEM_LIMIT = 56 * 1024 * 1024
BF16_SUBLANES = 16
ADAMW_MAX_ROWS = 176
MESH = pl.DeviceIdType.MESH


def _dot(a, b):
    return jnp.dot(a, b, preferred_element_type=F32)


def _dot_nt(a, b):
    return lax.dot_general(a, b, (((1,), (1,)), ((), ())), preferred_element_type=F32)


def _dot_tn(a, b):
    return lax.dot_general(a, b, (((0,), (0,)), ((), ())), preferred_element_type=F32)


def _sigmoid(v):
    return 1.0 / (1.0 + jnp.exp(-v))


def _ln_fwd(z):
    mu = jnp.mean(z, axis=-1, keepdims=True)
    zc = z - mu
    var = jnp.mean(zc * zc, axis=-1, keepdims=True)
    rstd = lax.rsqrt(var + LN_EPS)
    return zc * rstd, rstd


def _ln_bwd(dout, xh, rstd, g):
    dxh = dout * g
    m1 = jnp.mean(dxh, axis=-1, keepdims=True)
    m2 = jnp.mean(dxh * xh, axis=-1, keepdims=True)
    dz = rstd * (dxh - m1 - xh * m2)
    return dz, jnp.sum(dout * xh, axis=0, keepdims=True), jnp.sum(dout, axis=0, keepdims=True)


def _params(vmem=VMEM_LIMIT):
    return pltpu.CompilerParams(dimension_semantics=("arbitrary",), vmem_limit_bytes=vmem)


def _row(d):
    return pl.BlockSpec((1, d), lambda i: (0, 0))


def _tile(tm, d):
    return pl.BlockSpec((tm, d), lambda i: (i, 0))


def _fixed(r, d):
    return pl.BlockSpec((r, d), lambda i: (0, 0))


def _tile_cur(tm, d, nsteps):
    return pl.BlockSpec((tm, d), lambda i: (jnp.minimum(i, nsteps - 1), 0))


def _tile_prev(tm, d):
    return pl.BlockSpec((tm, d), lambda i: (jnp.maximum(i - 1, 0), 0))


ANY = pl.BlockSpec(memory_space=pl.ANY)


class _Layout:
    GATHER = {"a": ("pw1t", "pw2"),
              "b": ("wq", "wo", "gt0", "ut0", "dn0", "gt1", "ut1", "dn1", "wkv")}

    def __init__(self, d, f):
        self.d, self.f = d, f
        self.n = {"pw1t": 2 * d // N_DEV, "pw2": d // N_DEV, "wq": d // N_DEV, "wo": d // N_DEV,
                  "wkv": (d // N_DEV) * 2 * BLOCK // d}
        for l in range(DEPTH):
            self.n.update({f"gt{l}": f // N_DEV, f"ut{l}": f // N_DEV, f"dn{l}": f // N_DEV})
        self.goff = {}
        for names in self.GATHER.values():
            r = 0
            for name in names:
                self.goff[name] = r
                r += self.n[name]


def _load_weight(wb_ref, lay, name, dst):
    n = lay.n[name]
    for p in range(N_DEV):
        pltpu.sync_copy(wb_ref.at[p, pl.ds(lay.goff[name], n), :], dst.at[pl.ds(p * n, n), :])


def _wscratch(lay, name):
    return pltpu.VMEM((N_DEV * lay.n[name], lay.d), BF16)


def _wfull(ref):
    return ref[...]


def _wrows(ref, r0, nrows):
    return ref[r0:r0 + nrows, :]


def _me():
    return lax.axis_index("x"), lax.axis_index("y"), lax.axis_index("c")


def _peer(mask):
    x, y, c = _me()
    return (1 - x if mask & 4 else x, 1 - y if mask & 2 else y, 1 - c if mask & 1 else c)


def _index(dev):
    return 4 * dev[0] + 2 * dev[1] + dev[2]


class _HostedGather:
    def __init__(self, array):
        self.arrays = [array]
        self.out_shapes = [jax.ShapeDtypeStruct((N_DEV,) + array.shape, array.dtype)]

    def scratch(self):
        return [pltpu.SemaphoreType.DMA((7,)), pltpu.SemaphoreType.DMA((7,)), pltpu.SemaphoreType.DMA(())]

    def _copies(self, ins, outs, send_sems, recv_sems, local_sem):
        out = outs[0]
        x, y, c = _me()
        me, sibling = (x, y, c), (x, y, 1 - c)
        chips = [(1 - x, y), (x, 1 - y), (1 - x, 1 - y)]

        def copy(k, block, to, src=None):
            rows = out.at[_index(block)]
            return pltpu.make_async_remote_copy(
                src_ref=rows if src is None else src, dst_ref=rows, send_sem=send_sems.at[k],
                recv_sem=recv_sems.at[k], device_id=to, device_id_type=MESH)

        return dict(
            mine=lambda: pltpu.make_async_copy(ins[0], out.at[_index(me)], local_sem),
            first=lambda: [copy(0, me, sibling, src=ins[0])] + [copy(1 + j, me, (*chip, c), src=ins[0])
                                                                for j, chip in enumerate(chips)],
            over_ici=lambda: [copy(1 + j, (*chip, c), me) for j, chip in enumerate(chips)],
            passed=lambda: [copy(4 + j, (*chip, c), sibling) for j, chip in enumerate(chips)],
            from_sibling=lambda: [copy(0, sibling, me)] + [copy(4 + j, (*chip, 1 - c), me)
                                                           for j, chip in enumerate(chips)])

    def start(self, *refs):
        cp = self._copies(*refs)
        cp["mine"]().start()
        for c in cp["first"]():
            c.start()

    def middle(self, *refs):
        cp = self._copies(*refs)
        for arrived, onward in zip(cp["over_ici"](), cp["passed"]()):
            arrived.wait_recv()
            onward.start()

    def finish(self, *refs):
        cp = self._copies(*refs)
        for c in cp["from_sibling"]():
            c.wait_recv()
        for c in cp["first"]() + cp["passed"]():
            c.wait_send()
        cp["mine"]().wait()


class _HostedExchange:
    def __init__(self, arrays):
        self.arrays = list(arrays)
        self.out_shapes = [jax.ShapeDtypeStruct(a.shape, a.dtype) for a in self.arrays]

    def scratch(self):
        n = len(self.arrays)
        return [pltpu.SemaphoreType.DMA((n, 7)), pltpu.SemaphoreType.DMA((n, 7)), pltpu.SemaphoreType.DMA((n,))]

    def _copies(self, ins, outs, send_sems, recv_sems, local_sems):
        me = _index(_me())

        def dst(k, src_dev):
            return outs[k].at[src_dev]

        pairs = [(k, mask) for k in range(len(self.arrays)) for mask in range(1, N_DEV)]

        def local():
            return [pltpu.make_async_copy(ins[k].at[me], dst(k, me), local_sems.at[k])
                    for k in range(len(self.arrays))]

        def sends():
            return [pltpu.make_async_remote_copy(
                src_ref=ins[k].at[_index(_peer(mask))], dst_ref=dst(k, me), send_sem=send_sems.at[k, mask - 1],
                recv_sem=recv_sems.at[k, mask - 1], device_id=_peer(mask), device_id_type=MESH)
                for k, mask in pairs]

        def arrivals():
            return [pltpu.make_async_remote_copy(
                src_ref=ins[k].at[me], dst_ref=dst(k, _index(_peer(mask))), send_sem=send_sems.at[k, mask - 1],
                recv_sem=recv_sems.at[k, mask - 1], device_id=_me(), device_id_type=MESH)
                for k, mask in pairs]

        return local, sends, arrivals

    def start(self, *refs):
        local, sends, _ = self._copies(*refs)
        for c in local() + sends():
            c.start()

    def middle(self, *refs):
        pass

    def finish(self, *refs):
        local, sends, arrivals = self._copies(*refs)
        for c in arrivals():
            c.wait_recv()
        for c in sends():
            c.wait_send()
        for c in local():
            c.wait()


HAND_ON_AT = 6


def _gridded_call(body, name, nsteps, in_specs, out_specs, out_shape, scratch, args, hosted=None):
    if hosted is None:
        return pl.pallas_call(body, name=name, grid=(nsteps,), in_specs=in_specs, out_specs=out_specs,
                              out_shape=out_shape, scratch_shapes=scratch, compiler_params=_params())(*args)
    n_in, n_out, n_scr, h_in = len(in_specs), len(out_specs), len(scratch), len(hosted.arrays)
    h_out = len(hosted.out_shapes)

    def with_hosted(*refs):
        a = n_in + h_in
        b = a + n_out
        e = b + h_out + n_scr
        comm = (refs[n_in:a], refs[b:b + h_out], refs[e], refs[e + 1], refs[e + 2])
        i = pl.program_id(0)

        @pl.when(i == 0)
        def _():
            hosted.start(*comm)

        body(*refs[:n_in], *refs[a:b], *refs[b + h_out:e])

        @pl.when(i == HAND_ON_AT * nsteps // 8)
        def _():
            hosted.middle(*comm)

        @pl.when(i == nsteps - 1)
        def _():
            hosted.finish(*comm)

    return pl.pallas_call(
        with_hosted, name=name, grid=(nsteps,), in_specs=list(in_specs) + [ANY] * h_in,
        out_specs=list(out_specs) + [ANY] * h_out, out_shape=list(out_shape) + hosted.out_shapes,
        scratch_shapes=list(scratch) + hosted.scratch(), compiler_params=_params(),
    )(*args, *hosted.arrays)


CONV_RB = 64
CONV_LC = 128
CONV_MC = 256


def _shifted(win, r):
    return win if r == 0 else pltpu.roll(win, win.shape[0] - r, 0)


def _conv_fwd(x, wb, lay, w_dw, b_pw1, b_dw, cg, cb, b_pw2, lg, lb, tm, hosted=None):
    t, d = x.shape
    nsteps = t // tm

    def body(x_ref, xh_ref, wb_ref, wdw_ref, b1_ref, bdw_ref, cg_ref, cb_ref, b2_ref, lg_ref, lb_ref,
             xb_ref, ag_ref, xhc_ref, rsc_ref, xh1_ref, rs1_ref, w1_s, w2_s, ubuf, cv_s):
        i = pl.program_id(0)

        @pl.when(i == 0)
        def _():
            _load_weight(wb_ref, lay, "pw1t", w1_s)
            _load_weight(wb_ref, lay, "pw2", w2_s)

        xv = x_ref[...]
        xb = xv.astype(BF16)
        xb_ref[...] = xb
        xcat = jnp.concatenate([xh_ref[...].astype(BF16), xb], axis=0)
        for mc in range(d // CONV_MC):
            c0 = mc * CONV_MC
            acols, gcols = slice(c0, c0 + CONV_MC), slice(d + c0, d + c0 + CONV_MC)
            ha = _dot_nt(xcat, _wrows(w1_s, c0, CONV_MC)) + b1_ref[:, acols]
            hg = _dot_nt(xcat, _wrows(w1_s, d + c0, CONV_MC)) + b1_ref[:, gcols]
            ag_ref[:, acols] = ha[HALO:].astype(BF16)
            ag_ref[:, gcols] = hg[HALO:].astype(BF16)
            u = ha * _sigmoid(hg)
            u = jnp.concatenate([jnp.where(i > 0, u[:HALO], 0.0), u[HALO:], jnp.zeros((8, CONV_MC), F32)], axis=0)
            for r in range(8):
                ubuf[r, :, acols] = _shifted(u, r)
            for rb in range(tm // CONV_RB):
                t0 = rb * CONV_RB
                for lc in range(CONV_MC // CONV_LC):
                    lanes = slice(c0 + lc * CONV_LC, c0 + (lc + 1) * CONV_LC)
                    acc = jnp.zeros((CONV_RB, CONV_LC), F32)
                    for k in range(CONV_WIDTH):
                        s = HALO - (CONV_WIDTH - 1) + k
                        q = t0 + 8 * (s // 8)
                        acc = acc + ubuf[s % 8, q:q + CONV_RB, lanes] * wdw_ref[k:k + 1, lanes]
                    cv_s[t0:t0 + CONV_RB, lanes] = acc
        cv = cv_s[...] + bdw_ref[...]
        xhc, rsc = _ln_fwd(cv)
        xhc_ref[...] = xhc
        rsc_ref[...] = rsc
        n = xhc * cg_ref[...] + cb_ref[...]
        s_act = n * _sigmoid(n)
        m = _dot(s_act.astype(BF16), _wfull(w2_s)) + b2_ref[...]
        xh1, rs1 = _ln_fwd(ALPHA * xv + m)
        xh1_ref[...] = xh1
        rs1_ref[...] = rs1

    hb = tm // HALO
    return _gridded_call(
        body, "conv_fwd", nsteps,
        [_tile(tm, d), pl.BlockSpec((HALO, d), lambda i: (jnp.maximum(i * hb - 1, 0), 0)), ANY,
         _fixed(HALO, d), _row(2 * d), _row(d), _row(d), _row(d), _row(d), _row(d), _row(d)],
        [_tile(tm, d), _tile(tm, 2 * d), _tile(tm, d), _tile(tm, 1), _tile(tm, d), _tile(tm, 1)],
        [jax.ShapeDtypeStruct((t, d), BF16), jax.ShapeDtypeStruct((t, 2 * d), BF16),
         jax.ShapeDtypeStruct((t, d), F32), jax.ShapeDtypeStruct((t, 1), F32),
         jax.ShapeDtypeStruct((t, d), F32), jax.ShapeDtypeStruct((t, 1), F32)],
        [_wscratch(lay, "pw1t"), _wscratch(lay, "pw2"),
         pltpu.VMEM((8, HALO + tm + 8, d), F32), pltpu.VMEM((tm, d), F32)],
        (x, x, wb, w_dw, b_pw1, b_dw, cg, cb, b_pw2, lg, lb), hosted)


def _conv_bwd1(dz1, xhc, rsc, wb, lay, cg, cb, tm):
    t, d = dz1.shape

    def body(dz_ref, xhc_ref, rsc_ref, wb_ref, cg_ref, cb_ref, dzb_ref, s_ref, dcv_ref, st_ref, w2_s):
        i = pl.program_id(0)

        @pl.when(i == 0)
        def _():
            _load_weight(wb_ref, lay, "pw2", w2_s)
            st_ref[...] = jnp.zeros(st_ref.shape, F32)

        dz = dz_ref[...]
        dzb = dz.astype(BF16)
        dzb_ref[...] = dzb
        xhc_v = xhc_ref[...]
        n = xhc_v * cg_ref[...] + cb_ref[...]
        sg = _sigmoid(n)
        s_ref[...] = (n * sg).astype(BF16)
        ds = _dot_nt(dzb, _wfull(w2_s))
        dn = ds * (sg * (1.0 + n * (1.0 - sg)))
        dcv, dg, db = _ln_bwd(dn, xhc_v, rsc_ref[...], cg_ref[...])
        dcv_ref[...] = dcv
        st_ref[0:1, :] += dg
        st_ref[1:2, :] += db
        st_ref[2:3, :] += jnp.sum(dcv, axis=0, keepdims=True)
        st_ref[3:4, :] += jnp.sum(dz, axis=0, keepdims=True)

    return pl.pallas_call(
        body, name="conv_bwd1", grid=(t // tm,),
        in_specs=[_tile(tm, d), _tile(tm, d), _tile(tm, 1), ANY, _row(d), _row(d)],
        out_specs=[_tile(tm, d), _tile(tm, d), _tile(tm, d), _fixed(8, d)],
        out_shape=[jax.ShapeDtypeStruct((t, d), BF16), jax.ShapeDtypeStruct((t, d), BF16),
                   jax.ShapeDtypeStruct((t, d), F32), jax.ShapeDtypeStruct((8, d), F32)],
        scratch_shapes=[_wscratch(lay, "pw2")],
        compiler_params=_params(),
    )(dz1, xhc, rsc, wb, cg, cb)


def _conv_bwd2(dz1, dcv, ag, wb, lay, w_dw, tm, hosted=None):
    t, d = dz1.shape
    nsteps = t // tm

    def body(dz_ref, dcv_ref, dcvn_ref, ag_ref, wb_ref, wdw_ref,
             gx_ref, dh_ref, dw_ref, db1_ref, w1_s, ubuf, dbuf, du_s, dwacc):
        i = pl.program_id(0)

        @pl.when(i == 0)
        def _():
            _load_weight(wb_ref, lay, "pw1t", w1_s)
            dwacc[...] = jnp.zeros(dwacc.shape, F32)
            db1_ref[...] = jnp.zeros(db1_ref.shape, F32)

        gx = ALPHA * dz_ref[...]
        for mc in range(d // CONV_MC):
            c0 = mc * CONV_MC
            acols, gcols = slice(c0, c0 + CONV_MC), slice(d + c0, d + c0 + CONV_MC)
            a = ag_ref[:, acols].astype(F32)
            sg = _sigmoid(ag_ref[:, gcols].astype(F32))
            ubuf[:, acols] = a * sg
            dcv_next = jnp.where(i < nsteps - 1, dcvn_ref[:, acols], 0.0)
            dcv_c = jnp.concatenate([dcv_ref[:, acols], dcv_next, jnp.zeros((8, CONV_MC), F32)], axis=0)
            for r in range(8):
                dbuf[r, :, acols] = _shifted(dcv_c, r)
            for rb in range(tm // CONV_RB):
                t0 = rb * CONV_RB
                for lc in range(CONV_MC // CONV_LC):
                    lanes = slice(c0 + lc * CONV_LC, c0 + (lc + 1) * CONV_LC)
                    ucur = ubuf[t0:t0 + CONV_RB, lanes]
                    acc = jnp.zeros((CONV_RB, CONV_LC), F32)
                    for k in range(CONV_WIDTH):
                        sd = CONV_WIDTH - 1 - k
                        q = t0 + 8 * (sd // 8)
                        dk = dbuf[sd % 8, q:q + CONV_RB, lanes]
                        acc = acc + dk * wdw_ref[k:k + 1, lanes]
                        prod = ucur * dk
                        part = prod[0:8]
                        for j in range(1, CONV_RB // 8):
                            part = part + prod[8 * j:8 * j + 8]
                        dwacc[k, :, lanes] += part
                    du_s[t0:t0 + CONV_RB, lanes] = acc
            du = du_s[:, acols]
            da = du * sg
            dg = du * a * sg * (1.0 - sg)
            dab, dgb = da.astype(BF16), dg.astype(BF16)
            dh_ref[:, acols] = dab
            dh_ref[:, gcols] = dgb
            db1_ref[:, acols] += jnp.sum(da, axis=0, keepdims=True)
            db1_ref[:, gcols] += jnp.sum(dg, axis=0, keepdims=True)
            gx = gx + _dot(dab, _wrows(w1_s, c0, CONV_MC)) + _dot(dgb, _wrows(w1_s, d + c0, CONV_MC))
        gx_ref[...] = gx

        @pl.when(i == nsteps - 1)
        def _():
            dw_ref[...] = jnp.sum(dwacc[...], axis=1)

    hb = tm // HALO
    last = t // HALO - 1
    return _gridded_call(
        body, "conv_bwd2", nsteps,
        [_tile(tm, d), _tile(tm, d),
         pl.BlockSpec((HALO, d), lambda i: (jnp.minimum((i + 1) * hb, last), 0)),
         _tile(tm, 2 * d), ANY, _fixed(HALO, d)],
        [_tile(tm, d), _tile(tm, 2 * d), _fixed(HALO, d), _row(2 * d)],
        [jax.ShapeDtypeStruct((t, d), F32), jax.ShapeDtypeStruct((t, 2 * d), BF16),
         jax.ShapeDtypeStruct((HALO, d), F32), jax.ShapeDtypeStruct((1, 2 * d), F32)],
        [_wscratch(lay, "pw1t"), pltpu.VMEM((tm, d), F32),
         pltpu.VMEM((8, HALO + tm + 8, d), F32), pltpu.VMEM((tm, d), F32),
         pltpu.VMEM((HALO, 8, d), F32)],
        (dz1, dcv, dcv, ag, wb, w_dw), hosted)


FFN_FC = 256
FFN_AHEAD = 1


def _ffn_fwd(xh_in, g_in, b_in, wb, lay, layer, tm, *, kv=None, loss=None):
    t, d = xh_in.shape
    f = lay.f
    names = (f"gt{layer}", f"ut{layer}", f"dn{layer}")

    def body(*refs):
        xh_ref, gi_ref, bi_ref, wb_ref = refs[:4]
        pos = 4
        if kv is not None:
            go_ref, bo_ref, wkv_ref, bkv_ref = refs[pos:pos + 4]
            pos += 4
        if loss is not None:
            go_ref, bo_ref, tgt_ref = refs[pos:pos + 3]
            pos += 3
        xb_ref, hg_ref, hu_ref = refs[pos:pos + 3]
        pos += 3
        if kv is not None:
            xho_ref, rso_ref, xob_ref, kv_ref = refs[pos:pos + 4]
            pos += 4
        if loss is not None:
            dz_ref, st_ref, loss_ref = refs[pos:pos + 3]
            pos += 3
        gt_s, ut_s, dn_s, xin_s, fo_s = refs[pos:pos + 5]
        i = pl.program_id(0)

        @pl.when(i == 0)
        def _():
            for name, dst in zip(names, (gt_s, ut_s, dn_s)):
                _load_weight(wb_ref, lay, name, dst)
            xin_s[...] = jnp.zeros(xin_s.shape, F32)
            fo_s[...] = jnp.zeros(fo_s.shape, F32)
            if loss is not None:
                st_ref[...] = jnp.zeros(st_ref.shape, F32)
                loss_ref[...] = jnp.zeros(loss_ref.shape, F32)

        xin_prev = xin_s[...]
        xho, rso = _ln_fwd(ALPHA * xin_prev + fo_s[...])
        if kv is not None:
            xho_ref[...] = xho
            rso_ref[...] = rso
            xob_ref[...] = (xho * go_ref[...] + bo_ref[...]).astype(BF16)
        if loss is not None:
            real = i > 0
            diff = xho * go_ref[...] + bo_ref[...] - tgt_ref[...]
            loss_ref[...] += jnp.where(real, (0.5 / d) * jnp.sum(diff * diff), 0.0)
            dz, dg, db = _ln_bwd(diff * (1.0 / d), xho, rso, go_ref[...])
            dz_ref[...] = dz
            st_ref[0:1, :] += jnp.where(real, dg, 0.0)
            st_ref[1:2, :] += jnp.where(real, db, 0.0)

        xin = xh_ref[...] * gi_ref[...] + bi_ref[...]
        xb = xin.astype(BF16)
        xb_ref[...] = xb

        def up(c):
            return (_dot_nt(xb, _wrows(gt_s, c * FFN_FC, FFN_FC)), _dot_nt(xb, _wrows(ut_s, c * FFN_FC, FFN_FC)))

        fo = jnp.zeros((tm, d), F32)
        nc = f // FFN_FC
        ahead = [up(c) for c in range(min(FFN_AHEAD, nc))]
        for c in range(nc):
            rows = slice(c * FFN_FC, (c + 1) * FFN_FC)
            hg, hu = ahead.pop(0)
            if c + FFN_AHEAD < nc:
                ahead.append(up(c + FFN_AHEAD))
            hg_ref[:, rows] = hg.astype(BF16)
            hu_ref[:, rows] = hu.astype(BF16)
            act = hg * _sigmoid(hg) * hu
            fo = fo + _dot(act.astype(BF16), _wrows(dn_s, c * FFN_FC, FFN_FC))
        xin_s[...] = xin
        fo_s[...] = fo
        if kv is not None:
            kv_ref[...] = (_dot(xob_ref[...], wkv_ref[...]) + bkv_ref[...]).astype(BF16)

    nsteps = t // tm
    in_specs = [_tile_cur(tm, d, nsteps), _row(d), _row(d), ANY]
    args = [xh_in, g_in, b_in, wb]
    out_specs = [_tile_cur(tm, d, nsteps), _tile_cur(tm, f, nsteps), _tile_cur(tm, f, nsteps)]
    out_shape = [jax.ShapeDtypeStruct((t, d), BF16), jax.ShapeDtypeStruct((t, f), BF16),
                 jax.ShapeDtypeStruct((t, f), BF16)]
    if kv is not None:
        in_specs += [_row(d), _row(d), _fixed(d, 2 * BLOCK), _row(2 * BLOCK)]
        args += list(kv)
        out_specs += [_tile_prev(tm, d), _tile_prev(tm, 1), _tile_prev(tm, d), _tile_prev(tm, 2 * BLOCK)]
        out_shape += [jax.ShapeDtypeStruct((t, d), F32), jax.ShapeDtypeStruct((t, 1), F32),
                      jax.ShapeDtypeStruct((t, d), BF16), jax.ShapeDtypeStruct((t, 2 * BLOCK), BF16)]
    if loss is not None:
        in_specs += [_row(d), _row(d), _tile_prev(tm, d)]
        args += list(loss)
        out_specs += [_tile_prev(tm, d), _fixed(8, d), _fixed(8, 128)]
        out_shape += [jax.ShapeDtypeStruct((t, d), F32), jax.ShapeDtypeStruct((8, d), F32),
                      jax.ShapeDtypeStruct((8, 128), F32)]
    return pl.pallas_call(
        body, name=f"ffn_fwd{layer}", grid=(nsteps + 1,), in_specs=in_specs, out_specs=out_specs,
        out_shape=out_shape,
        scratch_shapes=[_wscratch(lay, n) for n in names] + [pltpu.VMEM((tm, d), F32), pltpu.VMEM((tm, d), F32)],
        compiler_params=_params(),
    )(*args)


def _ffn_bwd(dz, hg, hu, xh_in, rs_in, g_in, wb, lay, layer, tm, hosted=None, qkv=None):
    t, d = dz.shape
    f = lay.f
    nsteps = t // tm
    nbt = tm // BLOCK
    names = (f"gt{layer}", f"ut{layer}", f"dn{layer}")

    def body(*refs):
        dz_ref, hg_ref, hu_ref, xh_ref, rs_ref, gi_ref, wb_ref = refs[:7]
        pos = 7
        if qkv is not None:
            dq_ref, dkc_ref, dkp_ref, dkn_ref, xho_ref, rso_ref, go_ref, wkv_ref = refs[pos:pos + 8]
            pos += 8
        dzb_ref, act_ref, dhg_ref, dhu_ref, dzp_ref, st_ref = refs[pos:pos + 6]
        pos += 6
        if qkv is not None:
            dkv_ref, sto_ref, dbkv_ref = refs[pos:pos + 3]
            pos += 3
        gt_s, ut_s, dn_s = refs[pos:pos + 3]
        i = pl.program_id(0)

        @pl.when(i == 0)
        def _():
            for name, dst in zip(names, (gt_s, ut_s, dn_s)):
                _load_weight(wb_ref, lay, name, dst)
            st_ref[...] = jnp.zeros(st_ref.shape, F32)
            if qkv is not None:
                _load_weight(wb_ref, lay, "wq", refs[pos + 3])
                sto_ref[...] = jnp.zeros(sto_ref.shape, F32)
                dbkv_ref[...] = jnp.zeros(dbkv_ref.shape, F32)

        if qkv is None:
            dzv = dz_ref[...]
        else:
            nxt = jnp.where(i < nsteps - 1, dkn_ref[...], 0.0)
            shifted = jnp.concatenate([dkp_ref[pl.ds(BLOCK, tm - BLOCK), :], nxt], axis=0) if nbt > 1 else nxt
            dkv = dkc_ref[...] + shifted
            dkvb = dkv.astype(BF16)
            dkv_ref[...] = dkvb
            dbkv_ref[...] += jnp.sum(dkv, axis=0, keepdims=True)
            dxo = (ALPHA * dz_ref[...] + _dot_nt(dq_ref[...], refs[pos + 3][...])
                   + _dot_nt(dkvb, wkv_ref[...]))
            dzv, dgo, dbo = _ln_bwd(dxo, xho_ref[...], rso_ref[...], go_ref[...])
            sto_ref[0:1, :] += dgo
            sto_ref[1:2, :] += dbo
        dzb = dzv.astype(BF16)
        dzb_ref[...] = dzb
        dx = ALPHA * dzv
        def back(c):
            return _dot_nt(dzb, _wrows(dn_s, c * FFN_FC, FFN_FC))

        nc = f // FFN_FC
        ahead = [back(c) for c in range(min(FFN_AHEAD, nc))]
        for c in range(nc):
            rows = slice(c * FFN_FC, (c + 1) * FFN_FC)
            dact = ahead.pop(0)
            if c + FFN_AHEAD < nc:
                ahead.append(back(c + FFN_AHEAD))
            hg_v = hg_ref[:, rows].astype(F32)
            hu_v = hu_ref[:, rows].astype(F32)
            sg = _sigmoid(hg_v)
            silu = hg_v * sg
            act_ref[:, rows] = (silu * hu_v).astype(BF16)
            dhu = (dact * silu).astype(BF16)
            dhg = (dact * hu_v * (sg * (1.0 + hg_v * (1.0 - sg)))).astype(BF16)
            dhu_ref[:, rows] = dhu
            dhg_ref[:, rows] = dhg
            dx = (dx + _dot(dhg, _wrows(gt_s, c * FFN_FC, FFN_FC))
                  + _dot(dhu, _wrows(ut_s, c * FFN_FC, FFN_FC)))
        dzp, dg, db = _ln_bwd(dx, xh_ref[...], rs_ref[...], gi_ref[...])
        dzp_ref[...] = dzp
        st_ref[0:1, :] += dg
        st_ref[1:2, :] += db

    in_specs = [_tile(tm, d), _tile(tm, f), _tile(tm, f), _tile(tm, d), _tile(tm, 1), _row(d), ANY]
    args = [dz, hg, hu, xh_in, rs_in, g_in, wb]
    out_specs = [_tile(tm, d), _tile(tm, f), _tile(tm, f), _tile(tm, f), _tile(tm, d), _fixed(8, d)]
    out_shape = [jax.ShapeDtypeStruct((t, d), BF16), jax.ShapeDtypeStruct((t, f), BF16),
                 jax.ShapeDtypeStruct((t, f), BF16), jax.ShapeDtypeStruct((t, f), BF16),
                 jax.ShapeDtypeStruct((t, d), F32), jax.ShapeDtypeStruct((8, d), F32)]
    scratch = [_wscratch(lay, n) for n in names]
    if qkv is not None:
        dq, dkc, dkp, xh_out, rs_out, g_out, wkv = qkv
        last = t // BLOCK - 1
        in_specs += [_tile(tm, d), _tile(tm, 2 * BLOCK), _tile(tm, 2 * BLOCK),
                     pl.BlockSpec((BLOCK, 2 * BLOCK), lambda i: (jnp.minimum((i + 1) * nbt, last), 0)),
                     _tile(tm, d), _tile(tm, 1), _row(d), _fixed(d, 2 * BLOCK)]
        args += [dq, dkc, dkp, dkp, xh_out, rs_out, g_out, wkv]
        out_specs += [_tile(tm, 2 * BLOCK), _fixed(8, d), _row(2 * BLOCK)]
        out_shape += [jax.ShapeDtypeStruct((t, 2 * BLOCK), BF16), jax.ShapeDtypeStruct((8, d), F32),
                      jax.ShapeDtypeStruct((1, 2 * BLOCK), F32)]
        scratch.append(_wscratch(lay, "wq"))
    return _gridded_call(body, f"ffn_bwd{layer}", nsteps, in_specs, out_specs, out_shape, scratch, args, hosted)


def _alibi_slope(h, nq):
    return 2.0 ** (-ALIBI_MAX * (h + 1) / nq)


def _fill_alibi_bias(bias_s, nq, keys_on_rows=False):
    shape = (2 * BLOCK, BLOCK) if keys_on_rows else (BLOCK, 2 * BLOCK)
    qi = lax.broadcasted_iota(jnp.int32, shape, 1 if keys_on_rows else 0)
    kj = lax.broadcasted_iota(jnp.int32, shape, 0 if keys_on_rows else 1)
    delta = qi + BLOCK - kj
    valid = (delta >= 0) & (delta < BLOCK)
    dist = jnp.where(valid, delta.astype(F32), MASKED_DIST)
    dist_first = jnp.where(kj >= BLOCK, dist, MASKED_DIST)
    for h in range(nq):
        bias_s[0, h] = _alibi_slope(h, nq) * dist
        bias_s[1, h] = _alibi_slope(h, nq) * dist_first


def _padded_kv(kvb, kvh, transposed_v=False):
    lane = lax.broadcasted_iota(jnp.int32, (2 * BLOCK, BLOCK), 1)
    mine = (lane < HEAD_DIM) if kvh == 0 else (lane >= HEAD_DIM)
    out = []
    for sec, transposed in ((kvb[:, :BLOCK], False), (kvb[:, BLOCK:], transposed_v)):
        m = jnp.where(mine, sec.astype(F32), 0.0)
        sw = pltpu.roll(m, HEAD_DIM, 1)
        pair = (m, sw) if kvh == 0 else (sw, m)
        out.append(tuple((p.T if transposed else p).astype(BF16) for p in pair))
    return out


def _attn_fwd(xh_in, g_in, b_in, x_in_b, kvs, wb, lay, bq, sinks, bo, tm):
    t, d = xh_in.shape
    nq = d // HEAD_DIM
    pairs_per_kv = (d // BLOCK) // N_KV_HEADS
    nbt = tm // BLOCK
    scale = HEAD_DIM ** -0.5

    def body(xh_ref, gi_ref, bi_ref, xb_ref, kv_ref, kvp_ref, wb_ref, bq_ref, sk_ref, bo_ref,
             q_ref, o_ref, lse_ref, xho_ref, rso_ref, wq_s, wo_s, kvall, q_s, o_s, bias_s):
        i = pl.program_id(0)

        @pl.when(i == 0)
        def _():
            _load_weight(wb_ref, lay, "wq", wq_s)
            _load_weight(wb_ref, lay, "wo", wo_s)
            _fill_alibi_bias(bias_s, nq, keys_on_rows=True)

        qv = ((_dot(xb_ref[...], _wfull(wq_s)) + bq_ref[...]) * scale).astype(BF16)
        q_s[...] = qv
        q_ref[...] = qv
        kvall[pl.ds(0, BLOCK), :] = kvp_ref[...]
        kvall[pl.ds(BLOCK, tm), :] = kv_ref[...]
        head_row = lax.broadcasted_iota(jnp.int32, (BLOCK, BLOCK), 0)

        def score_phase(j):
            rows = slice(j * BLOCK, (j + 1) * BLOCK)
            kvb = kvall[j * BLOCK:(j + 2) * BLOCK, :]
            first = (i * nbt + j == 0).astype(jnp.int32)
            pads = [_padded_kv(kvb, kvh, transposed_v=True) for kvh in range(N_KV_HEADS)]
            scores = []
            for a in range(d // BLOCK):
                kpad = pads[a // pairs_per_kv][0]
                qp = q_s[rows, a * BLOCK:(a + 1) * BLOCK]
                for e in range(2):
                    scores.append(_dot_nt(kpad[e], qp) - bias_s[first, 2 * a + e])
            return rows, pads, scores

        def softmax_phase(state):
            rows, pads, scores = state
            probs, inv = [], []
            lse_t = jnp.zeros((BLOCK, BLOCK), F32)
            for h in range(nq):
                sink = sk_ref[:, h:h + 1]
                m = jnp.maximum(jnp.max(scores[h], axis=0, keepdims=True), sink)
                p = jnp.exp(scores[h] - m)
                l = jnp.sum(p, axis=0, keepdims=True) + jnp.exp(sink - m)
                lse_t = jnp.where(head_row == h, m + jnp.log(l), lse_t)
                probs.append(p.astype(BF16))
                inv.append(1.0 / l)
            lse_ref[rows, :] = lse_t.T
            return rows, pads, probs, inv

        def value_phase(state):
            rows, pads, probs, inv = state
            for a in range(d // BLOCK):
                vpad_t = pads[a // pairs_per_kv][1]
                opair_t = (_dot(vpad_t[0], probs[2 * a]) * inv[2 * a]
                           + _dot(vpad_t[1], probs[2 * a + 1]) * inv[2 * a + 1])
                o_s[rows, a * BLOCK:(a + 1) * BLOCK] = opair_t.T.astype(BF16)

        for state in [softmax_phase(s) for s in [score_phase(j) for j in range(nbt)]]:
            value_phase(state)
        ov = o_s[...]
        o_ref[...] = ov
        xin = xh_ref[...] * gi_ref[...] + bi_ref[...]
        xho, rso = _ln_fwd(ALPHA * xin + _dot(ov, _wfull(wo_s)) + bo_ref[...])
        xho_ref[...] = xho
        rso_ref[...] = rso

    return pl.pallas_call(
        body, name="attn_fwd", grid=(t // tm,),
        in_specs=[_tile(tm, d), _row(d), _row(d), _tile(tm, d), _tile(tm, 2 * BLOCK),
                  pl.BlockSpec((BLOCK, 2 * BLOCK), lambda i: (jnp.maximum(i * nbt - 1, 0), 0)),
                  ANY, _row(d), _row(nq), _row(d)],
        out_specs=[_tile(tm, d), _tile(tm, d), _tile(tm, BLOCK), _tile(tm, d), _tile(tm, 1)],
        out_shape=[jax.ShapeDtypeStruct((t, d), BF16), jax.ShapeDtypeStruct((t, d), BF16),
                   jax.ShapeDtypeStruct((t, BLOCK), F32), jax.ShapeDtypeStruct((t, d), F32),
                   jax.ShapeDtypeStruct((t, 1), F32)],
        scratch_shapes=[_wscratch(lay, "wq"), _wscratch(lay, "wo"),
                        pltpu.VMEM((BLOCK + tm, 2 * BLOCK), BF16), pltpu.VMEM((tm, d), BF16),
                        pltpu.VMEM((tm, d), BF16), pltpu.VMEM((2, nq, 2 * BLOCK, BLOCK), F32)],
        compiler_params=_params(),
    )(xh_in, g_in, b_in, x_in_b, kvs, kvs, wb, bq, sinks, bo)


def _attn_bwd(dz, q, o, lse, kvs, wb, lay, sinks, tm, hosted=None):
    t, d = dz.shape
    nq = d // HEAD_DIM
    pairs_per_kv = (d // BLOCK) // N_KV_HEADS
    nbt = tm // BLOCK
    scale = HEAD_DIM ** -0.5

    def body(dz_ref, q_ref, o_ref, lse_ref, kv_ref, kvp_ref, wb_ref, sk_ref,
             dzb_ref, dq_ref, dkc_ref, dkp_ref, st_ref, dsk_ref, wo_s, kvall, do_s, dq_s, bias_s):
        i = pl.program_id(0)

        @pl.when(i == 0)
        def _():
            _load_weight(wb_ref, lay, "wo", wo_s)
            _fill_alibi_bias(bias_s, nq, keys_on_rows=True)
            st_ref[...] = jnp.zeros(st_ref.shape, F32)
            dsk_ref[...] = jnp.zeros(dsk_ref.shape, F32)

        dzv = dz_ref[...]
        dzb = dzv.astype(BF16)
        dzb_ref[...] = dzb
        do_s[...] = _dot_nt(dzb, _wfull(wo_s))
        kvall[pl.ds(0, BLOCK), :] = kvp_ref[...]
        kvall[pl.ds(BLOCK, tm), :] = kv_ref[...]
        lane = lax.broadcasted_iota(jnp.int32, (BLOCK, BLOCK), 1)
        lane1 = lax.broadcasted_iota(jnp.int32, (1, BLOCK), 1)
        lane2 = lax.broadcasted_iota(jnp.int32, (2 * BLOCK, BLOCK), 1)
        halves = (lane < HEAD_DIM, lane >= HEAD_DIM)
        sel_row = lax.broadcasted_iota(jnp.int32, (8, BLOCK), 0)
        sel_lane = lax.broadcasted_iota(jnp.int32, (8, BLOCK), 1)
        head_sel = jnp.where((sel_row == 0) & (sel_lane < HEAD_DIM) | (sel_row == 1) & (sel_lane >= HEAD_DIM),
                             1.0, 0.0).astype(BF16)

        def score_phase(j):
            rows = slice(j * BLOCK, (j + 1) * BLOCK)
            kvb = kvall[j * BLOCK:(j + 2) * BLOCK, :]
            first = (i * nbt + j == 0).astype(jnp.int32)
            pads = [_padded_kv(kvb, kvh) for kvh in range(N_KV_HEADS)]
            scores, dps, dhs, qms, doms = [], [], [], [], []
            for a in range(d // BLOCK):
                kpad, vpad = pads[a // pairs_per_kv]
                cols = slice(a * BLOCK, (a + 1) * BLOCK)
                qp = q_ref[rows, cols]
                dop = do_s[rows, cols]
                dopb = dop.astype(BF16)
                prod = dop * o_ref[rows, cols].astype(F32)
                hi = prod.astype(BF16)
                lo = (prod - hi.astype(F32)).astype(BF16)
                dh_pair = _dot_nt(head_sel, hi) + _dot_nt(head_sel, lo)
                for e in range(2):
                    scores.append(_dot_nt(kpad[e], qp) - bias_s[first, 2 * a + e])
                    dps.append(_dot_nt(vpad[e], dopb))
                    dhs.append(dh_pair[e:e + 1, :])
                    qms.append(jnp.where(halves[e], qp, jnp.zeros_like(qp)))
                    doms.append(jnp.where(halves[e], dopb, jnp.zeros_like(dopb)))
            return rows, pads, scores, dps, dhs, qms, doms

        def softmax_phase(state):
            rows, pads, scores, dps, dhs, qms, doms = state
            dss, pbs = [], []
            dsk_t = jnp.zeros((1, BLOCK), F32)
            lse_t = lse_ref[rows, :].T
            for h in range(nq):
                lse_h = lse_t[h:h + 1, :]
                p = jnp.exp(scores[h] - lse_h)
                dss.append((p * (dps[h] - dhs[h])).astype(BF16))
                pbs.append(p.astype(BF16))
                dsink = -jnp.sum(jnp.exp(sk_ref[:, h:h + 1] - lse_h) * dhs[h], axis=1, keepdims=True)
                dsk_t = jnp.where(lane1 == h, dsink, dsk_t)
            dsk_ref[...] += dsk_t
            return rows, pads, dss, pbs, qms, doms

        def grad_phase(state):
            rows, pads, dss, pbs, qms, doms = state
            dsecs = []
            for kvh in range(N_KV_HEADS):
                kpad_t = [p.astype(F32).T.astype(BF16) for p in pads[kvh][0]]
                dk_acc = jnp.zeros((2 * BLOCK, BLOCK), F32)
                dv_acc = jnp.zeros((2 * BLOCK, BLOCK), F32)
                for a in range(kvh * pairs_per_kv, (kvh + 1) * pairs_per_kv):
                    dqp_t = _dot(kpad_t[0], dss[2 * a]) + _dot(kpad_t[1], dss[2 * a + 1])
                    dq_s[rows, a * BLOCK:(a + 1) * BLOCK] = dqp_t.T * scale
                    for e in range(2):
                        h = 2 * a + e
                        dk_acc = dk_acc + _dot(dss[h], qms[h])
                        dv_acc = dv_acc + _dot(pbs[h], doms[h])
                dsecs.append((dk_acc + pltpu.roll(dk_acc, HEAD_DIM, 1), dv_acc + pltpu.roll(dv_acc, HEAD_DIM, 1)))
            lo = lane2 < HEAD_DIM
            dkv = jnp.concatenate([jnp.where(lo, dsecs[0][0], dsecs[1][0]),
                                   jnp.where(lo, dsecs[0][1], dsecs[1][1])], axis=1)
            dkp_ref[rows, :] = dkv[:BLOCK]
            dkc_ref[rows, :] = dkv[BLOCK:]

        for state in [softmax_phase(s) for s in [score_phase(j) for j in range(nbt)]]:
            grad_phase(state)
        dqv = dq_s[...]
        dq_ref[...] = dqv.astype(BF16)
        st_ref[0:1, :] += jnp.sum(dqv, axis=0, keepdims=True)
        st_ref[1:2, :] += jnp.sum(dzv, axis=0, keepdims=True)

    return _gridded_call(
        body, "attn_bwd", t // tm,
        [_tile(tm, d), _tile(tm, d), _tile(tm, d), _tile(tm, BLOCK), _tile(tm, 2 * BLOCK),
         pl.BlockSpec((BLOCK, 2 * BLOCK), lambda i: (jnp.maximum(i * nbt - 1, 0), 0)),
         ANY, _row(nq)],
        [_tile(tm, d), _tile(tm, d), _tile(tm, 2 * BLOCK), _tile(tm, 2 * BLOCK),
         _fixed(8, d), _row(BLOCK)],
        [jax.ShapeDtypeStruct((t, d), BF16), jax.ShapeDtypeStruct((t, d), BF16),
         jax.ShapeDtypeStruct((t, 2 * BLOCK), F32), jax.ShapeDtypeStruct((t, 2 * BLOCK), F32),
         jax.ShapeDtypeStruct((8, d), F32), jax.ShapeDtypeStruct((1, BLOCK), F32)],
        [_wscratch(lay, "wo"), pltpu.VMEM((BLOCK + tm, 2 * BLOCK), BF16),
         pltpu.VMEM((tm, d), F32), pltpu.VMEM((tm, d), F32),
         pltpu.VMEM((2, nq, 2 * BLOCK, BLOCK), F32)],
        (dz, q, o, lse, kvs, kvs, wb, sinks), hosted)


def _tn_matmul(a, b, name, bm, tk):
    t, m = a.shape
    n = b.shape[1]
    ksteps = t // tk

    nc = max(n // 256, 1)
    cw = n // nc

    def body(a_ref, b_ref, o_ref, acc):
        k = pl.program_id(1)

        @pl.when(k == 0)
        def _():
            acc[...] = jnp.zeros(acc.shape, F32)

        at = a_ref[...].T
        for c in range(nc):
            cols = slice(c * cw, (c + 1) * cw)
            acc[:, cols] += _dot(at, b_ref[:, cols])

        @pl.when(k == ksteps - 1)
        def _():
            o_ref[...] = acc[...].astype(BF16)

    return pl.pallas_call(
        body, name=name, grid=(m // bm, ksteps),
        in_specs=[pl.BlockSpec((tk, bm), lambda j, k: (k, j)), pl.BlockSpec((tk, n), lambda j, k: (k, 0))],
        out_specs=pl.BlockSpec((bm, n), lambda j, k: (j, 0)),
        out_shape=jax.ShapeDtypeStruct((m, n), BF16),
        scratch_shapes=[pltpu.VMEM((bm, n), F32)],
        compiler_params=pltpu.CompilerParams(dimension_semantics=("arbitrary", "arbitrary"),
                                             vmem_limit_bytes=VMEM_LIMIT),
    )(a, b)


def _all_gather(arrays, name):
    n = len(arrays)

    def body(*refs):
        ins, outs = refs[:n], refs[n:2 * n]
        send_sems, recv_sems, local_sems = refs[2 * n:]
        x, y, c = _me()
        me, sibling = (x, y, c), (x, y, 1 - c)
        chips = [(1 - x, y), (x, 1 - y), (1 - x, 1 - y)]

        def slot(ref, dev):
            return ref.at[4 * dev[0] + 2 * dev[1] + dev[2]]

        def copy(a, k, block, to, src=None):
            return pltpu.make_async_remote_copy(
                src_ref=slot(outs[a], block) if src is None else src, dst_ref=slot(outs[a], block),
                send_sem=send_sems.at[a, k], recv_sem=recv_sems.at[a, k], device_id=to, device_id_type=MESH)

        mine = [pltpu.make_async_copy(ins[a], slot(outs[a], me), local_sems.at[a]) for a in range(n)]
        for cp in mine:
            cp.start()
        first = []
        for a in range(n):
            first.append(copy(a, 0, me, sibling, src=ins[a]))
            first += [copy(a, 1 + j, me, (*chip, c), src=ins[a]) for j, chip in enumerate(chips)]
        for cp in first:
            cp.start()
        passed = []
        for a in range(n):
            for j, chip in enumerate(chips):
                copy(a, 1 + j, (*chip, c), me).wait_recv()
                cp = copy(a, 4 + j, (*chip, c), sibling)
                cp.start()
                passed.append(cp)
        for a in range(n):
            copy(a, 0, sibling, me).wait_recv()
            for j, chip in enumerate(chips):
                copy(a, 4 + j, (*chip, 1 - c), me).wait_recv()
        for cp in first + passed:
            cp.wait_send()
        for cp in mine:
            cp.wait()

    return pl.pallas_call(
        body, name=name, in_specs=[ANY] * n, out_specs=[ANY] * n,
        out_shape=[jax.ShapeDtypeStruct((N_DEV,) + a.shape, a.dtype) for a in arrays],
        scratch_shapes=[pltpu.SemaphoreType.DMA((n, 7)), pltpu.SemaphoreType.DMA((n, 7)),
                        pltpu.SemaphoreType.DMA((n,))],
    )(*arrays)


def _exchange(arrays, name):
    n = len(arrays)
    blocked = [a.ndim == 3 for a in arrays]

    def body(*refs):
        ins, outs = refs[:n], refs[n:2 * n]
        send_sems, recv_sems, local_sems = refs[2 * n:]
        me = _index(_me())

        def src(k, dev):
            return ins[k].at[dev] if blocked[k] else ins[k]

        local = [pltpu.make_async_copy(src(k, me), outs[k].at[me], local_sems.at[k]) for k in range(n)]
        sends, arrivals = [], []
        for k in range(n):
            for mask in range(1, N_DEV):
                peer = _peer(mask)
                sends.append(pltpu.make_async_remote_copy(
                    src_ref=src(k, _index(peer)), dst_ref=outs[k].at[me], send_sem=send_sems.at[k, mask - 1],
                    recv_sem=recv_sems.at[k, mask - 1], device_id=peer, device_id_type=MESH))
                arrivals.append(pltpu.make_async_remote_copy(
                    src_ref=src(k, me), dst_ref=outs[k].at[_index(peer)], send_sem=send_sems.at[k, mask - 1],
                    recv_sem=recv_sems.at[k, mask - 1], device_id=_me(), device_id_type=MESH))
        for cp in local + sends:
            cp.start()
        for cp in arrivals:
            cp.wait_recv()
        for cp in sends:
            cp.wait_send()
        for cp in local:
            cp.wait()

    return pl.pallas_call(
        body, name=name, in_specs=[ANY] * n, out_specs=[ANY] * n,
        out_shape=[jax.ShapeDtypeStruct((N_DEV,) + a.shape[-2:], a.dtype) for a in arrays],
        scratch_shapes=[pltpu.SemaphoreType.DMA((n, 7)), pltpu.SemaphoreType.DMA((n, 7)),
                        pltpu.SemaphoreType.DMA((n,))],
    )(*arrays)


def _adamw_update(g, w_ref, m_ref, v_ref, go_ref, d_ref, mo_ref, vo_ref):
    mn = ADAM_B1 * m_ref[...] + (1.0 - ADAM_B1) * g
    vn = ADAM_B2 * v_ref[...] + (1.0 - ADAM_B2) * (g * g)
    m_hat = mn / (1.0 - ADAM_B1 ** ADAM_STEP)
    v_hat = vn / (1.0 - ADAM_B2 ** ADAM_STEP)
    go_ref[...] = g
    d_ref[...] = -ADAM_LR * (m_hat / (jnp.sqrt(v_hat) + ADAM_EPS) + ADAM_WD * w_ref[...])
    mo_ref[...] = mn
    vo_ref[...] = vn


def _sum_sources(g_refs, layer):
    total = None
    for l, g_ref in enumerate(g_refs):
        g = g_ref[0].astype(F32)
        for s in range(1, N_DEV):
            g = g + g_ref[s].astype(F32)
        total = g if total is None else jnp.where(layer == l, g, total)
    return total


def _layer_block(l_mine, nblocks):
    def index(l, j):
        return (0, jnp.where(l == l_mine, j, jnp.where(l < l_mine, 0, nblocks - 1)), 0)
    return index


def _adamw_sum(g8, w, m, v, name, tr):
    r, width = w.shape

    def body(g_ref, *refs):
        _adamw_update(_sum_sources([g_ref], 0), *refs)

    spec = pl.BlockSpec((tr, width), lambda i: (i, 0))
    return pl.pallas_call(
        body, name=name, grid=(r // tr,),
        in_specs=[pl.BlockSpec((N_DEV, tr, width), lambda i: (0, i, 0)), spec, spec, spec],
        out_specs=[spec] * 4, out_shape=[jax.ShapeDtypeStruct((r, width), F32)] * 4,
        compiler_params=_params(),
    )(g8, w, m, v)


def _adamw_rows(g8s, w, m, v, name):
    layers, n, width = w.shape
    tr = max(r for r in range(BF16_SUBLANES, ADAMW_MAX_ROWS + 1, BF16_SUBLANES) if n % r == 0)
    nb = n // tr

    def body(*refs):
        _adamw_update(_sum_sources(refs[:layers], pl.program_id(0)), *refs[layers:])

    spec = pl.BlockSpec((None, tr, width), lambda l, j: (l, j, 0))
    return pl.pallas_call(
        body, name=name, grid=(layers, nb),
        in_specs=[pl.BlockSpec((N_DEV, tr, width), _layer_block(l, nb)) for l in range(layers)] + [spec] * 3,
        out_specs=[spec] * 4, out_shape=[jax.ShapeDtypeStruct(w.shape, F32)] * 4,
        compiler_params=pltpu.CompilerParams(dimension_semantics=("arbitrary", "arbitrary"),
                                             vmem_limit_bytes=VMEM_LIMIT),
    )(*g8s, w, m, v)


def _adamw_cols(g8s, w, m, v, name):
    layers, k, n = w.shape
    cb = min(BLOCK, n)
    nb = pl.cdiv(n, cb)

    def body(*refs):
        _adamw_update(_sum_sources(refs[:layers], pl.program_id(0)).T, *refs[layers:])

    spec = pl.BlockSpec((None, k, cb), lambda l, j: (l, 0, j))
    return pl.pallas_call(
        body, name=name, grid=(layers, nb),
        in_specs=[pl.BlockSpec((N_DEV, cb, k), _layer_block(l, nb)) for l in range(layers)] + [spec] * 3,
        out_specs=[spec] * 4, out_shape=[jax.ShapeDtypeStruct(w.shape, F32)] * 4,
        compiler_params=pltpu.CompilerParams(dimension_semantics=("arbitrary", "arbitrary"),
                                             vmem_limit_bytes=VMEM_LIMIT),
    )(*g8s, w, m, v)


def _local_step(x, target, wba, shard_b, lay, sm, tm, tk):
    t, d = x.shape
    f = lay.f
    w_dw32 = jnp.concatenate([sm["w_dw"], jnp.zeros((HALO - CONV_WIDTH, d), F32)], axis=0)
    lmg, lmb, lfg, lfb = sm["ln_mix_g"], sm["ln_mix_b"], sm["ln_ffn_g"], sm["ln_ffn_b"]
    bkv = jnp.concatenate([sm["b_k"], sm["b_v"]], axis=1)
    bm_f = f // 2 if (f // 2) % 128 == 0 else f
    tm_light = 2 * tm

    received = {}

    def exchange(grads):
        return _HostedExchange([g.reshape(N_DEV, lay.n[n], d) for n, g in grads.items()])

    def keep(grads, arrived):
        received.update(zip(grads, arrived))

    xb0, ag, xhc, rsc, xh1, rs1, wbb = _conv_fwd(x, wba, lay, w_dw32, sm["b_pw1"], sm["b_dw"], sm["cg"],
                                                 sm["cb"], sm["b_pw2"], lmg[0:1], lmb[0:1], tm,
                                                 hosted=_HostedGather(shard_b))
    wkv = wbb[:, lay.goff["wkv"]:lay.goff["wkv"] + lay.n["wkv"], :].reshape(d, 2 * BLOCK)
    x1b, hg0, hu0, xh2, rs2, x2b, kvs = _ffn_fwd(xh1, lmg[0:1], lmb[0:1], wbb, lay, 0, tm_light,
                                                kv=(lfg[0:1], lfb[0:1], wkv, bkv))
    q, o, lse, xh3, rs3 = _attn_fwd(xh2, lfg[0:1], lfb[0:1], x2b, kvs, wbb, lay, sm["b_q"], sm["sinks"],
                                    sm["b_o"], tm)
    x3b, hg1, hu1, dz4, st4, loss = _ffn_fwd(xh3, lmg[1:2], lmb[1:2], wbb, lay, 1, tm,
                                             loss=(lfg[1:2], lfb[1:2], target))

    dz4b, act1, dhg1, dhu1, dz3, st3 = _ffn_bwd(dz4, hg1, hu1, xh3, rs3, lmg[1:2], wbb, lay, 1, tm)
    g1 = {"gt1": _tn_matmul(dhg1, x3b, "dw_gate1", bm_f, tk), "ut1": _tn_matmul(dhu1, x3b, "dw_up1", bm_f, tk),
          "dn1": _tn_matmul(act1, dz4b, "dw_down1", bm_f, tk)}
    dz3b, dq, dkc, dkp, stq, dsinks, *arrived = _attn_bwd(dz3, q, o, lse, kvs, wbb, lay, sm["sinks"], tm,
                                                          hosted=exchange(g1))
    keep(g1, arrived)
    g2 = {"wq": _tn_matmul(x2b, dq, "dw_q", d, tk), "wo": _tn_matmul(o, dz3b, "dw_o", d, tk)}
    dz2b, act0, dhg0, dhu0, dz1, st1, dkv, st2, dbkv, *arrived = _ffn_bwd(
        dz3, hg0, hu0, xh1, rs1, lmg[0:1], wbb, lay, 0, tm, hosted=exchange(g2),
        qkv=(dq, dkc, dkp, xh2, rs2, lfg[0:1], wkv))
    keep(g2, arrived)
    dz1b, s_act, dcv, stc = _conv_bwd1(dz1, xhc, rsc, wba, lay, sm["cg"], sm["cb"], tm_light)
    g3 = {"gt0": _tn_matmul(dhg0, x1b, "dw_gate0", bm_f, tk), "ut0": _tn_matmul(dhu0, x1b, "dw_up0", bm_f, tk),
          "dn0": _tn_matmul(act0, dz2b, "dw_down0", bm_f, tk), "pw2": _tn_matmul(s_act, dz1b, "dw_pw2", d, tk),
          "wkv": _tn_matmul(x2b, dkv, "dw_kv", d, tk)}
    grad_x, dh1, dwdw, db1, *arrived = _conv_bwd2(dz1, dcv, ag, wba, lay, w_dw32, tm, hosted=exchange(g3))
    keep(g3, arrived)
    g_pw1t = _tn_matmul(dh1, xb0, "dw_pw1", d, tk)
    small = {
        "w_dw": dwdw[:CONV_WIDTH], "b_pw1": db1, "b_dw": stc[2:3], "cg": stc[0:1], "cb": stc[1:2],
        "b_pw2": stc[3:4], "b_k": dbkv[:, :BLOCK], "b_v": dbkv[:, BLOCK:], "b_q": stq[0:1],
        "sinks": dsinks[:, :d // HEAD_DIM],
        "b_o": stq[1:2],
        "ln_mix_g": jnp.concatenate([st1[0:1], st3[0:1]], axis=0),
        "ln_mix_b": jnp.concatenate([st1[1:2], st3[1:2]], axis=0),
        "ln_ffn_g": jnp.concatenate([st2[0:1], st4[0:1]], axis=0),
        "ln_ffn_b": jnp.concatenate([st2[1:2], st4[1:2]], axis=0),
    }
    return loss[0, 0], grad_x, received, g_pw1t, small


SP_ROWS = 40
SP_BDW, SP_CG, SP_CB, SP_BPW2, SP_BPW1 = 32, 33, 34, 35, 36
RP_NAMES = ("ln_mix_g", "ln_mix_b", "ln_ffn_g", "ln_ffn_b", "b_q", "b_o", "b_k", "b_v", "sinks")


def _row_forms(d, pw1, pw2, wq, wo, gate, up, down, wk, wv):
    rf = {"pw1t": pw1[0].T, "pw2": pw2[0], "wq": wq[0], "wo": wo[0],
          "wkv": jnp.concatenate([wk, wv], axis=1).reshape(-1, d)}
    for l in range(DEPTH):
        rf.update({f"gt{l}": gate[l].T, f"ut{l}": up[l].T, f"dn{l}": down[l]})
    return rf


def _pack_rows(rf, names):
    return jnp.concatenate([rf[n] for n in names], axis=0)


def _pack_small(w_dw, b_dw, cg, cb, b_pw2, b_pw1):
    cw = b_dw.shape[1]
    z = jnp.zeros((1, cw), F32)
    return jnp.concatenate([w_dw[0], z, b_dw, cg, cb, b_pw2, b_pw1.reshape(2, cw), z, z], axis=0)


def _unpack_small(p):
    cw = p.shape[1]
    return dict(w_dw=p[None, :CONV_WIDTH], b_dw=p[SP_BDW:SP_BDW + 1], cg=p[SP_CG:SP_CG + 1],
                cb=p[SP_CB:SP_CB + 1], b_pw2=p[SP_BPW2:SP_BPW2 + 1],
                b_pw1=p[SP_BPW1:SP_BPW1 + 2].reshape(1, 2 * cw))


def _small_full(g):
    d = N_DEV * g.shape[2]

    def wide(r0, n=1):
        return jnp.transpose(g[:, r0:r0 + n], (1, 0, 2)).reshape(n, d)

    return dict(w_dw=wide(0, CONV_WIDTH), b_dw=wide(SP_BDW), cg=wide(SP_CG), cb=wide(SP_CB),
                b_pw2=wide(SP_BPW2), b_pw1=g[:, SP_BPW1:SP_BPW1 + 2].reshape(1, 2 * d))


def _small_grad_blocks(sg):
    cw = sg["b_dw"].shape[1] // N_DEV

    def narrow(a):
        return jnp.transpose(a.reshape(a.shape[0], N_DEV, cw), (1, 0, 2))

    z = jnp.zeros((N_DEV, 1, cw), F32)
    return jnp.concatenate([narrow(sg["w_dw"]), z, narrow(sg["b_dw"]), narrow(sg["cg"]), narrow(sg["cb"]),
                            narrow(sg["b_pw2"]), sg["b_pw1"].reshape(N_DEV, 2, cw), z, z], axis=1)


def _pack_rep(vals, rider=0.0):
    parts = []
    for name in RP_NAMES:
        a = vals[name].reshape(-1)
        pad = -a.shape[0] % 128
        parts.append(jnp.concatenate([a, jnp.zeros((pad,), F32)]).reshape(-1, 128))
    parts.append(jnp.full((1, 128), rider, F32))
    rows = sum(p.shape[0] for p in parts)
    parts.append(jnp.zeros((-rows % 8, 128), F32))
    return jnp.concatenate(parts, axis=0)


def _rider_row(shapes):
    return sum(-(-_size(shapes[name]) // 128) for name in RP_NAMES)


def _size(shape):
    n = 1
    for s in shape:
        n *= s
    return n


def _unpack_rep(p, shapes):
    out, r = {}, 0
    for name in RP_NAMES:
        n = _size(shapes[name])
        rows = -(-n // 128)
        out[name] = p[r:r + rows].reshape(-1)[:n].reshape(shapes[name])
        r += rows
    return out


def kernel(x, conv_w_pw1, conv_b_pw1, conv_w_dw, conv_b_dw, conv_ln_g, conv_ln_b, conv_w_pw2, conv_b_pw2, kv_w_k, kv_b_k, kv_w_v, kv_b_v, attn_w_q, attn_b_q, attn_sinks, attn_w_o, attn_b_o, ffn_w_gate, ffn_w_up, ffn_w_down, ln_mix_g, ln_mix_b, ln_ffn_g, ln_ffn_b, loss_target, m_conv_w_pw1, m_conv_b_pw1, m_conv_w_dw, m_conv_b_dw, m_conv_ln_g, m_conv_ln_b, m_conv_w_pw2, m_conv_b_pw2, m_kv_w_k, m_kv_b_k, m_kv_w_v, m_kv_b_v, m_attn_w_q, m_attn_b_q, m_attn_sinks, m_attn_w_o, m_attn_b_o, m_ffn_w_gate, m_ffn_w_up, m_ffn_w_down, m_ln_mix_g, m_ln_mix_b, m_ln_ffn_g, m_ln_ffn_b, v_conv_w_pw1, v_conv_b_pw1, v_conv_w_dw, v_conv_b_dw, v_conv_ln_g, v_conv_ln_b, v_conv_w_pw2, v_conv_b_pw2, v_kv_w_k, v_kv_b_k, v_kv_w_v, v_kv_b_v, v_attn_w_q, v_attn_b_q, v_attn_sinks, v_attn_w_o, v_attn_b_o, v_ffn_w_gate, v_ffn_w_up, v_ffn_w_down, v_ln_mix_g, v_ln_mix_b, v_ln_ffn_g, v_ln_ffn_b):
    t, d = x.shape[1], x.shape[2]
    f = ffn_w_gate.shape[2] * N_DEV
    lay = _Layout(d, f)
    tm, tk = 256, min(2048, t)

    rep_shapes = dict(ln_mix_g=ln_mix_g.shape, ln_mix_b=ln_mix_b.shape, ln_ffn_g=ln_ffn_g.shape,
                      ln_ffn_b=ln_ffn_b.shape, b_q=attn_b_q.shape, b_o=attn_b_o.shape, b_k=kv_b_k.shape,
                      b_v=kv_b_v.shape, sinks=attn_sinks.shape)

    def rep_pack(lmg, lmb, lfg, lfb, bq, bo, bk, bv, sk):
        return _pack_rep(dict(ln_mix_g=lmg, ln_mix_b=lmb, ln_ffn_g=lfg, ln_ffn_b=lfb, b_q=bq, b_o=bo,
                              b_k=bk, b_v=bv, sinks=sk))

    w_rf = _row_forms(d, conv_w_pw1, conv_w_pw2, attn_w_q, attn_w_o, ffn_w_gate, ffn_w_up, ffn_w_down, kv_w_k, kv_w_v)
    w_small = _pack_small(conv_w_dw, conv_b_dw, conv_ln_g, conv_ln_b, conv_b_pw2, conv_b_pw1)
    m_small = _pack_small(m_conv_w_dw, m_conv_b_dw, m_conv_ln_g, m_conv_ln_b, m_conv_b_pw2, m_conv_b_pw1)
    v_small = _pack_small(v_conv_w_dw, v_conv_b_dw, v_conv_ln_g, v_conv_ln_b, v_conv_b_pw2, v_conv_b_pw1)
    w_rep = rep_pack(ln_mix_g, ln_mix_b, ln_ffn_g, ln_ffn_b, attn_b_q, attn_b_o, kv_b_k, kv_b_v, attn_sinks)
    m_rep = rep_pack(m_ln_mix_g, m_ln_mix_b, m_ln_ffn_g, m_ln_ffn_b, m_attn_b_q, m_attn_b_o, m_kv_b_k, m_kv_b_v, m_attn_sinks)
    v_rep = rep_pack(v_ln_mix_g, v_ln_mix_b, v_ln_ffn_g, v_ln_ffn_b, v_attn_b_q, v_attn_b_o, v_kv_b_k, v_kv_b_v, v_attn_sinks)

    wba, smg = _all_gather([_pack_rows(w_rf, lay.GATHER["a"]).astype(BF16), w_small], "gather_conv_weights")
    shard_b = _pack_rows(w_rf, lay.GATHER["b"]).astype(BF16)
    sm = _small_full(smg)
    sm.update(ln_mix_g=ln_mix_g, ln_mix_b=ln_mix_b, ln_ffn_g=ln_ffn_g, ln_ffn_b=ln_ffn_b, b_q=attn_b_q,
              b_o=attn_b_o, sinks=attn_sinks, b_k=kv_b_k.reshape(1, -1), b_v=kv_b_v.reshape(1, -1))

    loss_part, grad_x, received, g_pw1t, gsmall = _local_step(x[0], loss_target[0], wba, shard_b, lay, sm, tm, tk)

    received["pw1t"], g8_small, g8_rep = _exchange(
        [g_pw1t.reshape(N_DEV, lay.n["pw1t"], d), _small_grad_blocks(gsmall), _pack_rep(gsmall, loss_part)],
        "exchange_last_grads")

    def kv_rows(wk, wv):
        return jnp.concatenate([wk, wv], axis=1).reshape(1, -1, d)

    def kv_split(a):
        a = a.reshape(d // N_DEV, 2 * BLOCK)
        return a[:, :BLOCK], a[:, BLOCK:]

    big = dict(
        pw1=_adamw_cols([received["pw1t"]], conv_w_pw1, m_conv_w_pw1, v_conv_w_pw1, "adamw_pw1"),
        gate=_adamw_cols([received["gt0"], received["gt1"]], ffn_w_gate, m_ffn_w_gate, v_ffn_w_gate, "adamw_gate"),
        up=_adamw_cols([received["ut0"], received["ut1"]], ffn_w_up, m_ffn_w_up, v_ffn_w_up, "adamw_up"),
        down=_adamw_rows([received["dn0"], received["dn1"]], ffn_w_down, m_ffn_w_down, v_ffn_w_down, "adamw_down"),
        pw2=_adamw_rows([received["pw2"]], conv_w_pw2, m_conv_w_pw2, v_conv_w_pw2, "adamw_pw2"),
        wq=_adamw_rows([received["wq"]], attn_w_q, m_attn_w_q, v_attn_w_q, "adamw_q"),
        wo=_adamw_rows([received["wo"]], attn_w_o, m_attn_w_o, v_attn_w_o, "adamw_o"),
        wkv=[kv_split(a) for a in _adamw_rows([received["wkv"]], kv_rows(kv_w_k, kv_w_v), kv_rows(m_kv_w_k, m_kv_w_v),
                                              kv_rows(v_kv_w_k, v_kv_w_v), "adamw_kv")])
    big_out = [dict(pw1=big["pw1"][i], pw2=big["pw2"][i], wq=big["wq"][i], wo=big["wo"][i], gate=big["gate"][i],
                    up=big["up"][i], down=big["down"][i], wk=big["wkv"][i][0], wv=big["wkv"][i][1])
               for i in range(4)]
    small_out = [_unpack_small(a) for a in _adamw_sum(g8_small, w_small, m_small, v_small, "adamw_small", SP_ROWS)]
    rep_res = _adamw_sum(g8_rep, w_rep, m_rep, v_rep, "adamw_rep", w_rep.shape[0])
    rep_out = [_unpack_rep(a, rep_shapes) for a in rep_res]
    loss = rep_res[0][_rider_row(rep_shapes), 0]

    outs = [loss, grad_x[None]]
    for b, s, r in zip(big_out, small_out, rep_out):
        outs += [b["pw1"], s["b_pw1"], s["w_dw"], s["b_dw"], s["cg"], s["cb"], b["pw2"], s["b_pw2"],
                 b["wk"], r["b_k"], b["wv"], r["b_v"], b["wq"], r["b_q"], r["sinks"], b["wo"], r["b_o"],
                 b["gate"], b["up"], b["down"], r["ln_mix_g"], r["ln_mix_b"], r["ln_ffn_g"], r["ln_ffn_b"]]
    return tuple(outs)
```

```python
import jax
import jax.numpy as jnp
from jax import lax
from jax.experimental import pallas as pl
from jax.experimental.pallas import tpu as pltpu

F32 = jnp.float32
BF16 = jnp.bfloat16

N_DEV = 8
HEAD_DIM = 64
N_KV_HEADS = 2
BLOCK = 128
CONV_WIDTH = 31
HALO = 32
ALIBI_MAX = 8.0
DEPTH = 2
ALPHA = (2.0 * DEPTH) ** 0.25
LN_EPS = 1e-5
MASKED_DIST = 1e32
ADAM_LR = 0.001
ADAM_B1 = 0.9
ADAM_B2 = 0.999
ADAM_EPS = 1e-08
ADAM_WD = 0.01
ADAM_STEP = 10
VMEM_LIMIT = 56 * 1024 * 1024
BF16_SUBLANES = 16
ADAMW_MAX_ROWS = 176
MESH = pl.DeviceIdType.MESH


def _dot(a, b):
    return jnp.dot(a, b, preferred_element_type=F32)


def _dot_nt(a, b):
    return lax.dot_general(a, b, (((1,), (1,)), ((), ())), preferred_element_type=F32)


def _dot_tn(a, b):
    return lax.dot_general(a, b, (((0,), (0,)), ((), ())), preferred_element_type=F32)


def _sigmoid(v):
    return 1.0 / (1.0 + jnp.exp(-v))


def _ln_fwd(z):
    mu = jnp.mean(z, axis=-1, keepdims=True)
    zc = z - mu
    var = jnp.mean(zc * zc, axis=-1, keepdims=True)
    rstd = lax.rsqrt(var + LN_EPS)
    return zc * rstd, rstd


def _ln_bwd(dout, xh, rstd, g):
    dxh = dout * g
    m1 = jnp.mean(dxh, axis=-1, keepdims=True)
    m2 = jnp.mean(dxh * xh, axis=-1, keepdims=True)
    dz = rstd * (dxh - m1 - xh * m2)
    return dz, jnp.sum(dout * xh, axis=0, keepdims=True), jnp.sum(dout, axis=0, keepdims=True)


def _params(vmem=VMEM_LIMIT):
    return pltpu.CompilerParams(dimension_semantics=("arbitrary",), vmem_limit_bytes=vmem)


def _row(d):
    return pl.BlockSpec((1, d), lambda i: (0, 0))


def _tile(tm, d):
    return pl.BlockSpec((tm, d), lambda i: (i, 0))


def _fixed(r, d):
    return pl.BlockSpec((r, d), lambda i: (0, 0))


def _tile_cur(tm, d, nsteps):
    return pl.BlockSpec((tm, d), lambda i: (jnp.minimum(i, nsteps - 1), 0))


def _tile_prev(tm, d):
    return pl.BlockSpec((tm, d), lambda i: (jnp.maximum(i - 1, 0), 0))


ANY = pl.BlockSpec(memory_space=pl.ANY)


class _Layout:
    GATHER = {"a": ("pw1t", "pw2"),
              "b": ("wq", "wo", "gt0", "ut0", "dn0", "gt1", "ut1", "dn1", "wkv")}

    def __init__(self, d, f):
        self.d, self.f = d, f
        self.n = {"pw1t": 2 * d // N_DEV, "pw2": d // N_DEV, "wq": d // N_DEV, "wo": d // N_DEV,
                  "wkv": (d // N_DEV) * 2 * BLOCK // d}
        for l in range(DEPTH):
            self.n.update({f"gt{l}": f // N_DEV, f"ut{l}": f // N_DEV, f"dn{l}": f // N_DEV})
        self.goff = {}
        for names in self.GATHER.values():
            r = 0
            for name in names:
                self.goff[name] = r
                r += self.n[name]


def _load_weight(wb_ref, lay, name, dst):
    n = lay.n[name]
    for p in range(N_DEV):
        pltpu.sync_copy(wb_ref.at[p, pl.ds(lay.goff[name], n), :], dst.at[pl.ds(p * n, n), :])


def _wscratch(lay, name):
    return pltpu.VMEM((N_DEV * lay.n[name], lay.d), BF16)


def _wfull(ref):
    return ref[...]


def _wrows(ref, r0, nrows):
    return ref[r0:r0 + nrows, :]


def _me():
    return lax.axis_index("x"), lax.axis_index("y"), lax.axis_index("c")


def _peer(mask):
    x, y, c = _me()
    return (1 - x if mask & 4 else x, 1 - y if mask & 2 else y, 1 - c if mask & 1 else c)


def _index(dev):
    return 4 * dev[0] + 2 * dev[1] + dev[2]


class _HostedGather:
    def __init__(self, array):
        self.arrays = [array]
        self.out_shapes = [jax.ShapeDtypeStruct((N_DEV,) + array.shape, array.dtype)]

    def scratch(self):
        return [pltpu.SemaphoreType.DMA((7,)), pltpu.SemaphoreType.DMA((7,)), pltpu.SemaphoreType.DMA(())]

    def _copies(self, ins, outs, send_sems, recv_sems, local_sem):
        out = outs[0]
        x, y, c = _me()
        me, sibling = (x, y, c), (x, y, 1 - c)
        chips = [(1 - x, y), (x, 1 - y), (1 - x, 1 - y)]

        def copy(k, block, to, src=None):
            rows = out.at[_index(block)]
            return pltpu.make_async_remote_copy(
                src_ref=rows if src is None else src, dst_ref=rows, send_sem=send_sems.at[k],
                recv_sem=recv_sems.at[k], device_id=to, device_id_type=MESH)

        return dict(
            mine=lambda: pltpu.make_async_copy(ins[0], out.at[_index(me)], local_sem),
            first=lambda: [copy(0, me, sibling, src=ins[0])] + [copy(1 + j, me, (*chip, c), src=ins[0])
                                                                for j, chip in enumerate(chips)],
            over_ici=lambda: [copy(1 + j, (*chip, c), me) for j, chip in enumerate(chips)],
            passed=lambda: [copy(4 + j, (*chip, c), sibling) for j, chip in enumerate(chips)],
            from_sibling=lambda: [copy(0, sibling, me)] + [copy(4 + j, (*chip, 1 - c), me)
                                                           for j, chip in enumerate(chips)])

    def start(self, *refs):
        cp = self._copies(*refs)
        cp["mine"]().start()
        for c in cp["first"]():
            c.start()

    def middle(self, *refs):
        cp = self._copies(*refs)
        for arrived, onward in zip(cp["over_ici"](), cp["passed"]()):
            arrived.wait_recv()
            onward.start()

    def finish(self, *refs):
        cp = self._copies(*refs)
        for c in cp["from_sibling"]():
            c.wait_recv()
        for c in cp["first"]() + cp["passed"]():
            c.wait_send()
        cp["mine"]().wait()


class _HostedExchange:
    def __init__(self, arrays):
        self.arrays = list(arrays)
        self.out_shapes = [jax.ShapeDtypeStruct(a.shape, a.dtype) for a in self.arrays]

    def scratch(self):
        n = len(self.arrays)
        return [pltpu.SemaphoreType.DMA((n, 7)), pltpu.SemaphoreType.DMA((n, 7)), pltpu.SemaphoreType.DMA((n,))]

    def _copies(self, ins, outs, send_sems, recv_sems, local_sems):
        me = _index(_me())

        def dst(k, src_dev):
            return outs[k].at[src_dev]

        pairs = [(k, mask) for k in range(len(self.arrays)) for mask in range(1, N_DEV)]

        def local():
            return [pltpu.make_async_copy(ins[k].at[me], dst(k, me), local_sems.at[k])
                    for k in range(len(self.arrays))]

        def sends():
            return [pltpu.make_async_remote_copy(
                src_ref=ins[k].at[_index(_peer(mask))], dst_ref=dst(k, me), send_sem=send_sems.at[k, mask - 1],
                recv_sem=recv_sems.at[k, mask - 1], device_id=_peer(mask), device_id_type=MESH)
                for k, mask in pairs]

        def arrivals():
            return [pltpu.make_async_remote_copy(
                src_ref=ins[k].at[me], dst_ref=dst(k, _index(_peer(mask))), send_sem=send_sems.at[k, mask - 1],
                recv_sem=recv_sems.at[k, mask - 1], device_id=_me(), device_id_type=MESH)
                for k, mask in pairs]

        return local, sends, arrivals

    def start(self, *refs):
        local, sends, _ = self._copies(*refs)
        for c in local() + sends():
            c.start()

    def middle(self, *refs):
        pass

    def finish(self, *refs):
        local, sends, arrivals = self._copies(*refs)
        for c in arrivals():
            c.wait_recv()
        for c in sends():
            c.wait_send()
        for c in local():
            c.wait()


HAND_ON_AT = 6


def _gridded_call(body, name, nsteps, in_specs, out_specs, out_shape, scratch, args, hosted=None):
    if hosted is None:
        return pl.pallas_call(body, name=name, grid=(nsteps,), in_specs=in_specs, out_specs=out_specs,
                              out_shape=out_shape, scratch_shapes=scratch, compiler_params=_params())(*args)
    n_in, n_out, n_scr, h_in = len(in_specs), len(out_specs), len(scratch), len(hosted.arrays)
    h_out = len(hosted.out_shapes)

    def with_hosted(*refs):
        a = n_in + h_in
        b = a + n_out
        e = b + h_out + n_scr
        comm = (refs[n_in:a], refs[b:b + h_out], refs[e], refs[e + 1], refs[e + 2])
        i = pl.program_id(0)

        @pl.when(i == 0)
        def _():
            hosted.start(*comm)

        body(*refs[:n_in], *refs[a:b], *refs[b + h_out:e])

        @pl.when(i == HAND_ON_AT * nsteps // 8)
        def _():
            hosted.middle(*comm)

        @pl.when(i == nsteps - 1)
        def _():
            hosted.finish(*comm)

    return pl.pallas_call(
        with_hosted, name=name, grid=(nsteps,), in_specs=list(in_specs) + [ANY] * h_in,
        out_specs=list(out_specs) + [ANY] * h_out, out_shape=list(out_shape) + hosted.out_shapes,
        scratch_shapes=list(scratch) + hosted.scratch(), compiler_params=_params(),
    )(*args, *hosted.arrays)


CONV_RB = 64
CONV_LC = 128
CONV_MC = 256


def _shifted(win, r):
    return win if r == 0 else pltpu.roll(win, win.shape[0] - r, 0)


def _conv_fwd(x, wb, lay, w_dw, b_pw1, b_dw, cg, cb, b_pw2, lg, lb, tm, hosted=None):
    t, d = x.shape
    nsteps = t // tm

    def body(x_ref, xh_ref, wb_ref, wdw_ref, b1_ref, bdw_ref, cg_ref, cb_ref, b2_ref, lg_ref, lb_ref,
             xb_ref, ag_ref, xhc_ref, rsc_ref, xh1_ref, rs1_ref, w1_s, w2_s, ubuf, cv_s):
        i = pl.program_id(0)

        @pl.when(i == 0)
        def _():
            _load_weight(wb_ref, lay, "pw1t", w1_s)
            _load_weight(wb_ref, lay, "pw2", w2_s)

        xv = x_ref[...]
        xb = xv.astype(BF16)
        xb_ref[...] = xb
        xcat = jnp.concatenate([xh_ref[...].astype(BF16), xb], axis=0)
        for mc in range(d // CONV_MC):
            c0 = mc * CONV_MC
            acols, gcols = slice(c0, c0 + CONV_MC), slice(d + c0, d + c0 + CONV_MC)
            ha = _dot_nt(xcat, _wrows(w1_s, c0, CONV_MC)) + b1_ref[:, acols]
            hg = _dot_nt(xcat, _wrows(w1_s, d + c0, CONV_MC)) + b1_ref[:, gcols]
            ag_ref[:, acols] = ha[HALO:].astype(BF16)
            ag_ref[:, gcols] = hg[HALO:].astype(BF16)
            u = ha * _sigmoid(hg)
            u = jnp.concatenate([jnp.where(i > 0, u[:HALO], 0.0), u[HALO:], jnp.zeros((8, CONV_MC), F32)], axis=0)
            for r in range(8):
                ubuf[r, :, acols] = _shifted(u, r)
            for rb in range(tm // CONV_RB):
                t0 = rb * CONV_RB
                for lc in range(CONV_MC // CONV_LC):
                    lanes = slice(c0 + lc * CONV_LC, c0 + (lc + 1) * CONV_LC)
                    acc = jnp.zeros((CONV_RB, CONV_LC), F32)
                    for k in range(CONV_WIDTH):
                        s = HALO - (CONV_WIDTH - 1) + k
                        q = t0 + 8 * (s // 8)
                        acc = acc + ubuf[s % 8, q:q + CONV_RB, lanes] * wdw_ref[k:k + 1, lanes]
                    cv_s[t0:t0 + CONV_RB, lanes] = acc
        cv = cv_s[...] + bdw_ref[...]
        xhc, rsc = _ln_fwd(cv)
        xhc_ref[...] = xhc
        rsc_ref[...] = rsc
        n = xhc * cg_ref[...] + cb_ref[...]
        s_act = n * _sigmoid(n)
        m = _dot(s_act.astype(BF16), _wfull(w2_s)) + b2_ref[...]
        xh1, rs1 = _ln_fwd(ALPHA * xv + m)
        xh1_ref[...] = xh1
        rs1_ref[...] = rs1

    hb = tm // HALO
    return _gridded_call(
        body, "conv_fwd", nsteps,
        [_tile(tm, d), pl.BlockSpec((HALO, d), lambda i: (jnp.maximum(i * hb - 1, 0), 0)), ANY,
         _fixed(HALO, d), _row(2 * d), _row(d), _row(d), _row(d), _row(d), _row(d), _row(d)],
        [_tile(tm, d), _tile(tm, 2 * d), _tile(tm, d), _tile(tm, 1), _tile(tm, d), _tile(tm, 1)],
        [jax.ShapeDtypeStruct((t, d), BF16), jax.ShapeDtypeStruct((t, 2 * d), BF16),
         jax.ShapeDtypeStruct((t, d), F32), jax.ShapeDtypeStruct((t, 1), F32),
         jax.ShapeDtypeStruct((t, d), F32), jax.ShapeDtypeStruct((t, 1), F32)],
        [_wscratch(lay, "pw1t"), _wscratch(lay, "pw2"),
         pltpu.VMEM((8, HALO + tm + 8, d), F32), pltpu.VMEM((tm, d), F32)],
        (x, x, wb, w_dw, b_pw1, b_dw, cg, cb, b_pw2, lg, lb), hosted)


def _conv_bwd1(dz1, xhc, rsc, wb, lay, cg, cb, tm):
    t, d = dz1.shape

    def body(dz_ref, xhc_ref, rsc_ref, wb_ref, cg_ref, cb_ref, dzb_ref, s_ref, dcv_ref, st_ref, w2_s):
        i = pl.program_id(0)

        @pl.when(i == 0)
        def _():
            _load_weight(wb_ref, lay, "pw2", w2_s)
            st_ref[...] = jnp.zeros(st_ref.shape, F32)

        dz = dz_ref[...]
        dzb = dz.astype(BF16)
        dzb_ref[...] = dzb
        xhc_v = xhc_ref[...]
        n = xhc_v * cg_ref[...] + cb_ref[...]
        sg = _sigmoid(n)
        s_ref[...] = (n * sg).astype(BF16)
        ds = _dot_nt(dzb, _wfull(w2_s))
        dn = ds * (sg * (1.0 + n * (1.0 - sg)))
        dcv, dg, db = _ln_bwd(dn, xhc_v, rsc_ref[...], cg_ref[...])
        dcv_ref[...] = dcv
        st_ref[0:1, :] += dg
        st_ref[1:2, :] += db
        st_ref[2:3, :] += jnp.sum(dcv, axis=0, keepdims=True)
        st_ref[3:4, :] += jnp.sum(dz, axis=0, keepdims=True)

    return pl.pallas_call(
        body, name="conv_bwd1", grid=(t // tm,),
        in_specs=[_tile(tm, d), _tile(tm, d), _tile(tm, 1), ANY, _row(d), _row(d)],
        out_specs=[_tile(tm, d), _tile(tm, d), _tile(tm, d), _fixed(8, d)],
        out_shape=[jax.ShapeDtypeStruct((t, d), BF16), jax.ShapeDtypeStruct((t, d), BF16),
                   jax.ShapeDtypeStruct((t, d), F32), jax.ShapeDtypeStruct((8, d), F32)],
        scratch_shapes=[_wscratch(lay, "pw2")],
        compiler_params=_params(),
    )(dz1, xhc, rsc, wb, cg, cb)


def _conv_bwd2(dz1, dcv, ag, wb, lay, w_dw, tm, hosted=None):
    t, d = dz1.shape
    nsteps = t // tm

    def body(dz_ref, dcv_ref, dcvn_ref, ag_ref, wb_ref, wdw_ref,
             gx_ref, dh_ref, dw_ref, db1_ref, w1_s, ubuf, dbuf, du_s, dwacc):
        i = pl.program_id(0)

        @pl.when(i == 0)
        def _():
            _load_weight(wb_ref, lay, "pw1t", w1_s)
            dwacc[...] = jnp.zeros(dwacc.shape, F32)
            db1_ref[...] = jnp.zeros(db1_ref.shape, F32)

        gx = ALPHA * dz_ref[...]
        for mc in range(d // CONV_MC):
            c0 = mc * CONV_MC
            acols, gcols = slice(c0, c0 + CONV_MC), slice(d + c0, d + c0 + CONV_MC)
            a = ag_ref[:, acols].astype(F32)
            sg = _sigmoid(ag_ref[:, gcols].astype(F32))
            ubuf[:, acols] = a * sg
            dcv_next = jnp.where(i < nsteps - 1, dcvn_ref[:, acols], 0.0)
            dcv_c = jnp.concatenate([dcv_ref[:, acols], dcv_next, jnp.zeros((8, CONV_MC), F32)], axis=0)
            for r in range(8):
                dbuf[r, :, acols] = _shifted(dcv_c, r)
            for rb in range(tm // CONV_RB):
                t0 = rb * CONV_RB
                for lc in range(CONV_MC // CONV_LC):
                    lanes = slice(c0 + lc * CONV_LC, c0 + (lc + 1) * CONV_LC)
                    ucur = ubuf[t0:t0 + CONV_RB, lanes]
                    acc = jnp.zeros((CONV_RB, CONV_LC), F32)
                    for k in range(CONV_WIDTH):
                        sd = CONV_WIDTH - 1 - k
                        q = t0 + 8 * (sd // 8)
                        dk = dbuf[sd % 8, q:q + CONV_RB, lanes]
                        acc = acc + dk * wdw_ref[k:k + 1, lanes]
                        prod = ucur * dk
                        part = prod[0:8]
                        for j in range(1, CONV_RB // 8):
                            part = part + prod[8 * j:8 * j + 8]
                        dwacc[k, :, lanes] += part
                    du_s[t0:t0 + CONV_RB, lanes] = acc
            du = du_s[:, acols]
            da = du * sg
            dg = du * a * sg * (1.0 - sg)
            dab, dgb = da.astype(BF16), dg.astype(BF16)
            dh_ref[:, acols] = dab
            dh_ref[:, gcols] = dgb
            db1_ref[:, acols] += jnp.sum(da, axis=0, keepdims=True)
            db1_ref[:, gcols] += jnp.sum(dg, axis=0, keepdims=True)
            gx = gx + _dot(dab, _wrows(w1_s, c0, CONV_MC)) + _dot(dgb, _wrows(w1_s, d + c0, CONV_MC))
        gx_ref[...] = gx

        @pl.when(i == nsteps - 1)
        def _():
            dw_ref[...] = jnp.sum(dwacc[...], axis=1)

    hb = tm // HALO
    last = t // HALO - 1
    return _gridded_call(
        body, "conv_bwd2", nsteps,
        [_tile(tm, d), _tile(tm, d),
         pl.BlockSpec((HALO, d), lambda i: (jnp.minimum((i + 1) * hb, last), 0)),
         _tile(tm, 2 * d), ANY, _fixed(HALO, d)],
        [_tile(tm, d), _tile(tm, 2 * d), _fixed(HALO, d), _row(2 * d)],
        [jax.ShapeDtypeStruct((t, d), F32), jax.ShapeDtypeStruct((t, 2 * d), BF16),
         jax.ShapeDtypeStruct((HALO, d), F32), jax.ShapeDtypeStruct((1, 2 * d), F32)],
        [_wscratch(lay, "pw1t"), pltpu.VMEM((tm, d), F32),
         pltpu.VMEM((8, HALO + tm + 8, d), F32), pltpu.VMEM((tm, d), F32),
         pltpu.VMEM((HALO, 8, d), F32)],
        (dz1, dcv, dcv, ag, wb, w_dw), hosted)


FFN_FC = 256
FFN_AHEAD = 1


def _ffn_fwd(xh_in, g_in, b_in, wb, lay, layer, tm, *, kv=None, loss=None):
    t, d = xh_in.shape
    f = lay.f
    names = (f"gt{layer}", f"ut{layer}", f"dn{layer}")

    def body(*refs):
        xh_ref, gi_ref, bi_ref, wb_ref = refs[:4]
        pos = 4
        if kv is not None:
            go_ref, bo_ref, wkv_ref, bkv_ref = refs[pos:pos + 4]
            pos += 4
        if loss is not None:
            go_ref, bo_ref, tgt_ref = refs[pos:pos + 3]
            pos += 3
        xb_ref, hg_ref, hu_ref = refs[pos:pos + 3]
        pos += 3
        if kv is not None:
            xho_ref, rso_ref, xob_ref, kv_ref = refs[pos:pos + 4]
            pos += 4
        if loss is not None:
            dz_ref, st_ref, loss_ref = refs[pos:pos + 3]
            pos += 3
        gt_s, ut_s, dn_s, xin_s, fo_s = refs[pos:pos + 5]
        i = pl.program_id(0)

        @pl.when(i == 0)
        def _():
            for name, dst in zip(names, (gt_s, ut_s, dn_s)):
                _load_weight(wb_ref, lay, name, dst)
            xin_s[...] = jnp.zeros(xin_s.shape, F32)
            fo_s[...] = jnp.zeros(fo_s.shape, F32)
            if loss is not None:
                st_ref[...] = jnp.zeros(st_ref.shape, F32)
                loss_ref[...] = jnp.zeros(loss_ref.shape, F32)

        xin_prev = xin_s[...]
        xho, rso = _ln_fwd(ALPHA * xin_prev + fo_s[...])
        if kv is not None:
            xho_ref[...] = xho
            rso_ref[...] = rso
            xob_ref[...] = (xho * go_ref[...] + bo_ref[...]).astype(BF16)
        if loss is not None:
            real = i > 0
            diff = xho * go_ref[...] + bo_ref[...] - tgt_ref[...]
            loss_ref[...] += jnp.where(real, (0.5 / d) * jnp.sum(diff * diff), 0.0)
            dz, dg, db = _ln_bwd(diff * (1.0 / d), xho, rso, go_ref[...])
            dz_ref[...] = dz
            st_ref[0:1, :] += jnp.where(real, dg, 0.0)
            st_ref[1:2, :] += jnp.where(real, db, 0.0)

        xin = xh_ref[...] * gi_ref[...] + bi_ref[...]
        xb = xin.astype(BF16)
        xb_ref[...] = xb

        def up(c):
            return (_dot_nt(xb, _wrows(gt_s, c * FFN_FC, FFN_FC)), _dot_nt(xb, _wrows(ut_s, c * FFN_FC, FFN_FC)))

        fo = jnp.zeros((tm, d), F32)
        nc = f // FFN_FC
        ahead = [up(c) for c in range(min(FFN_AHEAD, nc))]
        for c in range(nc):
            rows = slice(c * FFN_FC, (c + 1) * FFN_FC)
            hg, hu = ahead.pop(0)
            if c + FFN_AHEAD < nc:
                ahead.append(up(c + FFN_AHEAD))
            hg_ref[:, rows] = hg.astype(BF16)
            hu_ref[:, rows] = hu.astype(BF16)
            act = hg * _sigmoid(hg) * hu
            fo = fo + _dot(act.astype(BF16), _wrows(dn_s, c * FFN_FC, FFN_FC))
        xin_s[...] = xin
        fo_s[...] = fo
        if kv is not None:
            kv_ref[...] = (_dot(xob_ref[...], wkv_ref[...]) + bkv_ref[...]).astype(BF16)

    nsteps = t // tm
    in_specs = [_tile_cur(tm, d, nsteps), _row(d), _row(d), ANY]
    args = [xh_in, g_in, b_in, wb]
    out_specs = [_tile_cur(tm, d, nsteps), _tile_cur(tm, f, nsteps), _tile_cur(tm, f, nsteps)]
    out_shape = [jax.ShapeDtypeStruct((t, d), BF16), jax.ShapeDtypeStruct((t, f), BF16),
                 jax.ShapeDtypeStruct((t, f), BF16)]
    if kv is not None:
        in_specs += [_row(d), _row(d), _fixed(d, 2 * BLOCK), _row(2 * BLOCK)]
        args += list(kv)
        out_specs += [_tile_prev(tm, d), _tile_prev(tm, 1), _tile_prev(tm, d), _tile_prev(tm, 2 * BLOCK)]
        out_shape += [jax.ShapeDtypeStruct((t, d), F32), jax.ShapeDtypeStruct((t, 1), F32),
                      jax.ShapeDtypeStruct((t, d), BF16), jax.ShapeDtypeStruct((t, 2 * BLOCK), BF16)]
    if loss is not None:
        in_specs += [_row(d), _row(d), _tile_prev(tm, d)]
        args += list(loss)
        out_specs += [_tile_prev(tm, d), _fixed(8, d), _fixed(8, 128)]
        out_shape += [jax.ShapeDtypeStruct((t, d), F32), jax.ShapeDtypeStruct((8, d), F32),
                      jax.ShapeDtypeStruct((8, 128), F32)]
    return pl.pallas_call(
        body, name=f"ffn_fwd{layer}", grid=(nsteps + 1,), in_specs=in_specs, out_specs=out_specs,
        out_shape=out_shape,
        scratch_shapes=[_wscratch(lay, n) for n in names] + [pltpu.VMEM((tm, d), F32), pltpu.VMEM((tm, d), F32)],
        compiler_params=_params(),
    )(*args)


def _ffn_bwd(dz, hg, hu, xh_in, rs_in, g_in, wb, lay, layer, tm, hosted=None, qkv=None):
    t, d = dz.shape
    f = lay.f
    nsteps = t // tm
    nbt = tm // BLOCK
    names = (f"gt{layer}", f"ut{layer}", f"dn{layer}")

    def body(*refs):
        dz_ref, hg_ref, hu_ref, xh_ref, rs_ref, gi_ref, wb_ref = refs[:7]
        pos = 7
        if qkv is not None:
            dq_ref, dkc_ref, dkp_ref, dkn_ref, xho_ref, rso_ref, go_ref, wkv_ref = refs[pos:pos + 8]
            pos += 8
        dzb_ref, act_ref, dhg_ref, dhu_ref, dzp_ref, st_ref = refs[pos:pos + 6]
        pos += 6
        if qkv is not None:
            dkv_ref, sto_ref, dbkv_ref = refs[pos:pos + 3]
            pos += 3
        gt_s, ut_s, dn_s = refs[pos:pos + 3]
        i = pl.program_id(0)

        @pl.when(i == 0)
        def _():
            for name, dst in zip(names, (gt_s, ut_s, dn_s)):
                _load_weight(wb_ref, lay, name, dst)
            st_ref[...] = jnp.zeros(st_ref.shape, F32)
            if qkv is not None:
                _load_weight(wb_ref, lay, "wq", refs[pos + 3])
                sto_ref[...] = jnp.zeros(sto_ref.shape, F32)
                dbkv_ref[...] = jnp.zeros(dbkv_ref.shape, F32)

        if qkv is None:
            dzv = dz_ref[...]
        else:
            nxt = jnp.where(i < nsteps - 1, dkn_ref[...], 0.0)
            shifted = jnp.concatenate([dkp_ref[pl.ds(BLOCK, tm - BLOCK), :], nxt], axis=0) if nbt > 1 else nxt
            dkv = dkc_ref[...] + shifted
            dkvb = dkv.astype(BF16)
            dkv_ref[...] = dkvb
            dbkv_ref[...] += jnp.sum(dkv, axis=0, keepdims=True)
            dxo = (ALPHA * dz_ref[...] + _dot_nt(dq_ref[...], refs[pos + 3][...])
                   + _dot_nt(dkvb, wkv_ref[...]))
            dzv, dgo, dbo = _ln_bwd(dxo, xho_ref[...], rso_ref[...], go_ref[...])
            sto_ref[0:1, :] += dgo
            sto_ref[1:2, :] += dbo
        dzb = dzv.astype(BF16)
        dzb_ref[...] = dzb
        dx = ALPHA * dzv
        def back(c):
            return _dot_nt(dzb, _wrows(dn_s, c * FFN_FC, FFN_FC))

        nc = f // FFN_FC
        ahead = [back(c) for c in range(min(FFN_AHEAD, nc))]
        for c in range(nc):
            rows = slice(c * FFN_FC, (c + 1) * FFN_FC)
            dact = ahead.pop(0)
            if c + FFN_AHEAD < nc:
                ahead.append(back(c + FFN_AHEAD))
            hg_v = hg_ref[:, rows].astype(F32)
            hu_v = hu_ref[:, rows].astype(F32)
            sg = _sigmoid(hg_v)
            silu = hg_v * sg
            act_ref[:, rows] = (silu * hu_v).astype(BF16)
            dhu = (dact * silu).astype(BF16)
            dhg = (dact * hu_v * (sg * (1.0 + hg_v * (1.0 - sg)))).astype(BF16)
            dhu_ref[:, rows] = dhu
            dhg_ref[:, rows] = dhg
            dx = (dx + _dot(dhg, _wrows(gt_s, c * FFN_FC, FFN_FC))
                  + _dot(dhu, _wrows(ut_s, c * FFN_FC, FFN_FC)))
        dzp, dg, db = _ln_bwd(dx, xh_ref[...], rs_ref[...], gi_ref[...])
        dzp_ref[...] = dzp
        st_ref[0:1, :] += dg
        st_ref[1:2, :] += db

    in_specs = [_tile(tm, d), _tile(tm, f), _tile(tm, f), _tile(tm, d), _tile(tm, 1), _row(d), ANY]
    args = [dz, hg, hu, xh_in, rs_in, g_in, wb]
    out_specs = [_tile(tm, d), _tile(tm, f), _tile(tm, f), _tile(tm, f), _tile(tm, d), _fixed(8, d)]
    out_shape = [jax.ShapeDtypeStruct((t, d), BF16), jax.ShapeDtypeStruct((t, f), BF16),
                 jax.ShapeDtypeStruct((t, f), BF16), jax.ShapeDtypeStruct((t, f), BF16),
                 jax.ShapeDtypeStruct((t, d), F32), jax.ShapeDtypeStruct((8, d), F32)]
    scratch = [_wscratch(lay, n) for n in names]
    if qkv is not None:
        dq, dkc, dkp, xh_out, rs_out, g_out, wkv = qkv
        last = t // BLOCK - 1
        in_specs += [_tile(tm, d), _tile(tm, 2 * BLOCK), _tile(tm, 2 * BLOCK),
                     pl.BlockSpec((BLOCK, 2 * BLOCK), lambda i: (jnp.minimum((i + 1) * nbt, last), 0)),
                     _tile(tm, d), _tile(tm, 1), _row(d), _fixed(d, 2 * BLOCK)]
        args += [dq, dkc, dkp, dkp, xh_out, rs_out, g_out, wkv]
        out_specs += [_tile(tm, 2 * BLOCK), _fixed(8, d), _row(2 * BLOCK)]
        out_shape += [jax.ShapeDtypeStruct((t, 2 * BLOCK), BF16), jax.ShapeDtypeStruct((8, d), F32),
                      jax.ShapeDtypeStruct((1, 2 * BLOCK), F32)]
        scratch.append(_wscratch(lay, "wq"))
    return _gridded_call(body, f"ffn_bwd{layer}", nsteps, in_specs, out_specs, out_shape, scratch, args, hosted)


def _alibi_slope(h, nq):
    return 2.0 ** (-ALIBI_MAX * (h + 1) / nq)


def _fill_alibi_bias(bias_s, nq, keys_on_rows=False):
    shape = (2 * BLOCK, BLOCK) if keys_on_rows else (BLOCK, 2 * BLOCK)
    qi = lax.broadcasted_iota(jnp.int32, shape, 1 if keys_on_rows else 0)
    kj = lax.broadcasted_iota(jnp.int32, shape, 0 if keys_on_rows else 1)
    delta = qi + BLOCK - kj
    valid = (delta >= 0) & (delta < BLOCK)
    dist = jnp.where(valid, delta.astype(F32), MASKED_DIST)
    dist_first = jnp.where(kj >= BLOCK, dist, MASKED_DIST)
    for h in range(nq):
        bias_s[0, h] = _alibi_slope(h, nq) * dist
        bias_s[1, h] = _alibi_slope(h, nq) * dist_first


def _padded_kv(kvb, kvh, transposed_v=False):
    lane = lax.broadcasted_iota(jnp.int32, (2 * BLOCK, BLOCK), 1)
    mine = (lane < HEAD_DIM) if kvh == 0 else (lane >= HEAD_DIM)
    out = []
    for sec, transposed in ((kvb[:, :BLOCK], False), (kvb[:, BLOCK:], transposed_v)):
        m = jnp.where(mine, sec.astype(F32), 0.0)
        sw = pltpu.roll(m, HEAD_DIM, 1)
        pair = (m, sw) if kvh == 0 else (sw, m)
        out.append(tuple((p.T if transposed else p).astype(BF16) for p in pair))
    return out


def _attn_fwd(xh_in, g_in, b_in, x_in_b, kvs, wb, lay, bq, sinks, bo, tm):
    t, d = xh_in.shape
    nq = d // HEAD_DIM
    pairs_per_kv = (d // BLOCK) // N_KV_HEADS
    nbt = tm // BLOCK
    scale = HEAD_DIM ** -0.5

    def body(xh_ref, gi_ref, bi_ref, xb_ref, kv_ref, kvp_ref, wb_ref, bq_ref, sk_ref, bo_ref,
             q_ref, o_ref, lse_ref, xho_ref, rso_ref, wq_s, wo_s, kvall, q_s, o_s, bias_s):
        i = pl.program_id(0)

        @pl.when(i == 0)
        def _():
            _load_weight(wb_ref, lay, "wq", wq_s)
            _load_weight(wb_ref, lay, "wo", wo_s)
            _fill_alibi_bias(bias_s, nq, keys_on_rows=True)

        qv = ((_dot(xb_ref[...], _wfull(wq_s)) + bq_ref[...]) * scale).astype(BF16)
        q_s[...] = qv
        q_ref[...] = qv
        kvall[pl.ds(0, BLOCK), :] = kvp_ref[...]
        kvall[pl.ds(BLOCK, tm), :] = kv_ref[...]
        head_row = lax.broadcasted_iota(jnp.int32, (BLOCK, BLOCK), 0)

        def score_phase(j):
            rows = slice(j * BLOCK, (j + 1) * BLOCK)
            kvb = kvall[j * BLOCK:(j + 2) * BLOCK, :]
            first = (i * nbt + j == 0).astype(jnp.int32)
            pads = [_padded_kv(kvb, kvh, transposed_v=True) for kvh in range(N_KV_HEADS)]
            scores = []
            for a in range(d // BLOCK):
                kpad = pads[a // pairs_per_kv][0]
                qp = q_s[rows, a * BLOCK:(a + 1) * BLOCK]
                for e in range(2):
                    scores.append(_dot_nt(kpad[e], qp) - bias_s[first, 2 * a + e])
            return rows, pads, scores

        def softmax_phase(state):
            rows, pads, scores = state
            probs, inv = [], []
            lse_t = jnp.zeros((BLOCK, BLOCK), F32)
            for h in range(nq):
                sink = sk_ref[:, h:h + 1]
                m = jnp.maximum(jnp.max(scores[h], axis=0, keepdims=True), sink)
                p = jnp.exp(scores[h] - m)
                l = jnp.sum(p, axis=0, keepdims=True) + jnp.exp(sink - m)
                lse_t = jnp.where(head_row == h, m + jnp.log(l), lse_t)
                probs.append(p.astype(BF16))
                inv.append(1.0 / l)
            lse_ref[rows, :] = lse_t.T
            return rows, pads, probs, inv

        def value_phase(state):
            rows, pads, probs, inv = state
            for a in range(d // BLOCK):
                vpad_t = pads[a // pairs_per_kv][1]
                opair_t = (_dot(vpad_t[0], probs[2 * a]) * inv[2 * a]
                           + _dot(vpad_t[1], probs[2 * a + 1]) * inv[2 * a + 1])
                o_s[rows, a * BLOCK:(a + 1) * BLOCK] = opair_t.T.astype(BF16)

        for state in [softmax_phase(s) for s in [score_phase(j) for j in range(nbt)]]:
            value_phase(state)
        ov = o_s[...]
        o_ref[...] = ov
        xin = xh_ref[...] * gi_ref[...] + bi_ref[...]
        xho, rso = _ln_fwd(ALPHA * xin + _dot(ov, _wfull(wo_s)) + bo_ref[...])
        xho_ref[...] = xho
        rso_ref[...] = rso

    return pl.pallas_call(
        body, name="attn_fwd", grid=(t // tm,),
        in_specs=[_tile(tm, d), _row(d), _row(d), _tile(tm, d), _tile(tm, 2 * BLOCK),
                  pl.BlockSpec((BLOCK, 2 * BLOCK), lambda i: (jnp.maximum(i * nbt - 1, 0), 0)),
                  ANY, _row(d), _row(nq), _row(d)],
        out_specs=[_tile(tm, d), _tile(tm, d), _tile(tm, BLOCK), _tile(tm, d), _tile(tm, 1)],
        out_shape=[jax.ShapeDtypeStruct((t, d), BF16), jax.ShapeDtypeStruct((t, d), BF16),
                   jax.ShapeDtypeStruct((t, BLOCK), F32), jax.ShapeDtypeStruct((t, d), F32),
                   jax.ShapeDtypeStruct((t, 1), F32)],
        scratch_shapes=[_wscratch(lay, "wq"), _wscratch(lay, "wo"),
                        pltpu.VMEM((BLOCK + tm, 2 * BLOCK), BF16), pltpu.VMEM((tm, d), BF16),
                        pltpu.VMEM((tm, d), BF16), pltpu.VMEM((2, nq, 2 * BLOCK, BLOCK), F32)],
        compiler_params=_params(),
    )(xh_in, g_in, b_in, x_in_b, kvs, kvs, wb, bq, sinks, bo)


def _attn_bwd(dz, q, o, lse, kvs, wb, lay, sinks, tm, hosted=None):
    t, d = dz.shape
    nq = d // HEAD_DIM
    pairs_per_kv = (d // BLOCK) // N_KV_HEADS
    nbt = tm // BLOCK
    scale = HEAD_DIM ** -0.5

    def body(dz_ref, q_ref, o_ref, lse_ref, kv_ref, kvp_ref, wb_ref, sk_ref,
             dzb_ref, dq_ref, dkc_ref, dkp_ref, st_ref, dsk_ref, wo_s, kvall, do_s, dq_s, bias_s):
        i = pl.program_id(0)

        @pl.when(i == 0)
        def _():
            _load_weight(wb_ref, lay, "wo", wo_s)
            _fill_alibi_bias(bias_s, nq, keys_on_rows=True)
            st_ref[...] = jnp.zeros(st_ref.shape, F32)
            dsk_ref[...] = jnp.zeros(dsk_ref.shape, F32)

        dzv = dz_ref[...]
        dzb = dzv.astype(BF16)
        dzb_ref[...] = dzb
        do_s[...] = _dot_nt(dzb, _wfull(wo_s))
        kvall[pl.ds(0, BLOCK), :] = kvp_ref[...]
        kvall[pl.ds(BLOCK, tm), :] = kv_ref[...]
        lane = lax.broadcasted_iota(jnp.int32, (BLOCK, BLOCK), 1)
        lane1 = lax.broadcasted_iota(jnp.int32, (1, BLOCK), 1)
        lane2 = lax.broadcasted_iota(jnp.int32, (2 * BLOCK, BLOCK), 1)
        halves = (lane < HEAD_DIM, lane >= HEAD_DIM)
        sel_row = lax.broadcasted_iota(jnp.int32, (8, BLOCK), 0)
        sel_lane = lax.broadcasted_iota(jnp.int32, (8, BLOCK), 1)
        head_sel = jnp.where((sel_row == 0) & (sel_lane < HEAD_DIM) | (sel_row == 1) & (sel_lane >= HEAD_DIM),
                             1.0, 0.0).astype(BF16)

        def score_phase(j):
            rows = slice(j * BLOCK, (j + 1) * BLOCK)
            kvb = kvall[j * BLOCK:(j + 2) * BLOCK, :]
            first = (i * nbt + j == 0).astype(jnp.int32)
            pads = [_padded_kv(kvb, kvh) for kvh in range(N_KV_HEADS)]
            scores, dps, dhs, qms, doms = [], [], [], [], []
            for a in range(d // BLOCK):
                kpad, vpad = pads[a // pairs_per_kv]
                cols = slice(a * BLOCK, (a + 1) * BLOCK)
                qp = q_ref[rows, cols]
                dop = do_s[rows, cols]
                dopb = dop.astype(BF16)
                prod = dop * o_ref[rows, cols].astype(F32)
                hi = prod.astype(BF16)
                lo = (prod - hi.astype(F32)).astype(BF16)
                dh_pair = _dot_nt(head_sel, hi) + _dot_nt(head_sel, lo)
                for e in range(2):
                    scores.append(_dot_nt(kpad[e], qp) - bias_s[first, 2 * a + e])
                    dps.append(_dot_nt(vpad[e], dopb))
                    dhs.append(dh_pair[e:e + 1, :])
                    qms.append(jnp.where(halves[e], qp, jnp.zeros_like(qp)))
                    doms.append(jnp.where(halves[e], dopb, jnp.zeros_like(dopb)))
            return rows, pads, scores, dps, dhs, qms, doms

        def softmax_phase(state):
            rows, pads, scores, dps, dhs, qms, doms = state
            dss, pbs = [], []
            dsk_t = jnp.zeros((1, BLOCK), F32)
            lse_t = lse_ref[rows, :].T
            for h in range(nq):
                lse_h = lse_t[h:h + 1, :]
                p = jnp.exp(scores[h] - lse_h)
                dss.append((p * (dps[h] - dhs[h])).astype(BF16))
                pbs.append(p.astype(BF16))
                dsink = -jnp.sum(jnp.exp(sk_ref[:, h:h + 1] - lse_h) * dhs[h], axis=1, keepdims=True)
                dsk_t = jnp.where(lane1 == h, dsink, dsk_t)
            dsk_ref[...] += dsk_t
            return rows, pads, dss, pbs, qms, doms

        def grad_phase(state):
            rows, pads, dss, pbs, qms, doms = state
            dsecs = []
            for kvh in range(N_KV_HEADS):
                kpad_t = [p.astype(F32).T.astype(BF16) for p in pads[kvh][0]]
                dk_acc = jnp.zeros((2 * BLOCK, BLOCK), F32)
                dv_acc = jnp.zeros((2 * BLOCK, BLOCK), F32)
                for a in range(kvh * pairs_per_kv, (kvh + 1) * pairs_per_kv):
                    dqp_t = _dot(kpad_t[0], dss[2 * a]) + _dot(kpad_t[1], dss[2 * a + 1])
                    dq_s[rows, a * BLOCK:(a + 1) * BLOCK] = dqp_t.T * scale
                    for e in range(2):
                        h = 2 * a + e
                        dk_acc = dk_acc + _dot(dss[h], qms[h])
                        dv_acc = dv_acc + _dot(pbs[h], doms[h])
                dsecs.append((dk_acc + pltpu.roll(dk_acc, HEAD_DIM, 1), dv_acc + pltpu.roll(dv_acc, HEAD_DIM, 1)))
            lo = lane2 < HEAD_DIM
            dkv = jnp.concatenate([jnp.where(lo, dsecs[0][0], dsecs[1][0]),
                                   jnp.where(lo, dsecs[0][1], dsecs[1][1])], axis=1)
            dkp_ref[rows, :] = dkv[:BLOCK]
            dkc_ref[rows, :] = dkv[BLOCK:]

        for state in [softmax_phase(s) for s in [score_phase(j) for j in range(nbt)]]:
            grad_phase(state)
        dqv = dq_s[...]
        dq_ref[...] = dqv.astype(BF16)
        st_ref[0:1, :] += jnp.sum(dqv, axis=0, keepdims=True)
        st_ref[1:2, :] += jnp.sum(dzv, axis=0, keepdims=True)

    return _gridded_call(
        body, "attn_bwd", t // tm,
        [_tile(tm, d), _tile(tm, d), _tile(tm, d), _tile(tm, BLOCK), _tile(tm, 2 * BLOCK),
         pl.BlockSpec((BLOCK, 2 * BLOCK), lambda i: (jnp.maximum(i * nbt - 1, 0), 0)),
         ANY, _row(nq)],
        [_tile(tm, d), _tile(tm, d), _tile(tm, 2 * BLOCK), _tile(tm, 2 * BLOCK),
         _fixed(8, d), _row(BLOCK)],
        [jax.ShapeDtypeStruct((t, d), BF16), jax.ShapeDtypeStruct((t, d), BF16),
         jax.ShapeDtypeStruct((t, 2 * BLOCK), F32), jax.ShapeDtypeStruct((t, 2 * BLOCK), F32),
         jax.ShapeDtypeStruct((8, d), F32), jax.ShapeDtypeStruct((1, BLOCK), F32)],
        [_wscratch(lay, "wo"), pltpu.VMEM((BLOCK + tm, 2 * BLOCK), BF16),
         pltpu.VMEM((tm, d), F32), pltpu.VMEM((tm, d), F32),
         pltpu.VMEM((2, nq, 2 * BLOCK, BLOCK), F32)],
        (dz, q, o, lse, kvs, kvs, wb, sinks), hosted)


def _tn_matmul(a, b, name, bm, tk):
    t, m = a.shape
    n = b.shape[1]
    ksteps = t // tk

    nc = max(n // 256, 1)
    cw = n // nc

    def body(a_ref, b_ref, o_ref, acc):
        k = pl.program_id(1)

        @pl.when(k == 0)
        def _():
            acc[...] = jnp.zeros(acc.shape, F32)

        at = a_ref[...].T
        for c in range(nc):
            cols = slice(c * cw, (c + 1) * cw)
            acc[:, cols] += _dot(at, b_ref[:, cols])

        @pl.when(k == ksteps - 1)
        def _():
            o_ref[...] = acc[...].astype(BF16)

    return pl.pallas_call(
        body, name=name, grid=(m // bm, ksteps),
        in_specs=[pl.BlockSpec((tk, bm), lambda j, k: (k, j)), pl.BlockSpec((tk, n), lambda j, k: (k, 0))],
        out_specs=pl.BlockSpec((bm, n), lambda j, k: (j, 0)),
        out_shape=jax.ShapeDtypeStruct((m, n), BF16),
        scratch_shapes=[pltpu.VMEM((bm, n), F32)],
        compiler_params=pltpu.CompilerParams(dimension_semantics=("arbitrary", "arbitrary"),
                                             vmem_limit_bytes=VMEM_LIMIT),
    )(a, b)


def _all_gather(arrays, name):
    n = len(arrays)

    def body(*refs):
        ins, outs = refs[:n], refs[n:2 * n]
        send_sems, recv_sems, local_sems = refs[2 * n:]
        x, y, c = _me()
        me, sibling = (x, y, c), (x, y, 1 - c)
        chips = [(1 - x, y), (x, 1 - y), (1 - x, 1 - y)]

        def slot(ref, dev):
            return ref.at[4 * dev[0] + 2 * dev[1] + dev[2]]

        def copy(a, k, block, to, src=None):
            return pltpu.make_async_remote_copy(
                src_ref=slot(outs[a], block) if src is None else src, dst_ref=slot(outs[a], block),
                send_sem=send_sems.at[a, k], recv_sem=recv_sems.at[a, k], device_id=to, device_id_type=MESH)

        mine = [pltpu.make_async_copy(ins[a], slot(outs[a], me), local_sems.at[a]) for a in range(n)]
        for cp in mine:
            cp.start()
        first = []
        for a in range(n):
            first.append(copy(a, 0, me, sibling, src=ins[a]))
            first += [copy(a, 1 + j, me, (*chip, c), src=ins[a]) for j, chip in enumerate(chips)]
        for cp in first:
            cp.start()
        passed = []
        for a in range(n):
            for j, chip in enumerate(chips):
                copy(a, 1 + j, (*chip, c), me).wait_recv()
                cp = copy(a, 4 + j, (*chip, c), sibling)
                cp.start()
                passed.append(cp)
        for a in range(n):
            copy(a, 0, sibling, me).wait_recv()
            for j, chip in enumerate(chips):
                copy(a, 4 + j, (*chip, 1 - c), me).wait_recv()
        for cp in first + passed:
            cp.wait_send()
        for cp in mine:
            cp.wait()

    return pl.pallas_call(
        body, name=name, in_specs=[ANY] * n, out_specs=[ANY] * n,
        out_shape=[jax.ShapeDtypeStruct((N_DEV,) + a.shape, a.dtype) for a in arrays],
        scratch_shapes=[pltpu.SemaphoreType.DMA((n, 7)), pltpu.SemaphoreType.DMA((n, 7)),
                        pltpu.SemaphoreType.DMA((n,))],
    )(*arrays)


def _exchange(arrays, name):
    n = len(arrays)
    blocked = [a.ndim == 3 for a in arrays]

    def body(*refs):
        ins, outs = refs[:n], refs[n:2 * n]
        send_sems, recv_sems, local_sems = refs[2 * n:]
        me = _index(_me())

        def src(k, dev):
            return ins[k].at[dev] if blocked[k] else ins[k]

        local = [pltpu.make_async_copy(src(k, me), outs[k].at[me], local_sems.at[k]) for k in range(n)]
        sends, arrivals = [], []
        for k in range(n):
            for mask in range(1, N_DEV):
                peer = _peer(mask)
                sends.append(pltpu.make_async_remote_copy(
                    src_ref=src(k, _index(peer)), dst_ref=outs[k].at[me], send_sem=send_sems.at[k, mask - 1],
                    recv_sem=recv_sems.at[k, mask - 1], device_id=peer, device_id_type=MESH))
                arrivals.append(pltpu.make_async_remote_copy(
                    src_ref=src(k, me), dst_ref=outs[k].at[_index(peer)], send_sem=send_sems.at[k, mask - 1],
                    recv_sem=recv_sems.at[k, mask - 1], device_id=_me(), device_id_type=MESH))
        for cp in local + sends:
            cp.start()
        for cp in arrivals:
            cp.wait_recv()
        for cp in sends:
            cp.wait_send()
        for cp in local:
            cp.wait()

    return pl.pallas_call(
        body, name=name, in_specs=[ANY] * n, out_specs=[ANY] * n,
        out_shape=[jax.ShapeDtypeStruct((N_DEV,) + a.shape[-2:], a.dtype) for a in arrays],
        scratch_shapes=[pltpu.SemaphoreType.DMA((n, 7)), pltpu.SemaphoreType.DMA((n, 7)),
                        pltpu.SemaphoreType.DMA((n,))],
    )(*arrays)


def _adamw_update(g, w_ref, m_ref, v_ref, go_ref, d_ref, mo_ref, vo_ref):
    mn = ADAM_B1 * m_ref[...] + (1.0 - ADAM_B1) * g
    vn = ADAM_B2 * v_ref[...] + (1.0 - ADAM_B2) * (g * g)
    m_hat = mn / (1.0 - ADAM_B1 ** ADAM_STEP)
    v_hat = vn / (1.0 - ADAM_B2 ** ADAM_STEP)
    go_ref[...] = g
    d_ref[...] = -ADAM_LR * (m_hat / (jnp.sqrt(v_hat) + ADAM_EPS) + ADAM_WD * w_ref[...])
    mo_ref[...] = mn
    vo_ref[...] = vn


def _sum_sources(g_refs, layer):
    total = None
    for l, g_ref in enumerate(g_refs):
        g = g_ref[0].astype(F32)
        for s in range(1, N_DEV):
            g = g + g_ref[s].astype(F32)
        total = g if total is None else jnp.where(layer == l, g, total)
    return total


def _layer_block(l_mine, nblocks):
    def index(l, j):
        return (0, jnp.where(l == l_mine, j, jnp.where(l < l_mine, 0, nblocks - 1)), 0)
    return index


def _adamw_sum(g8, w, m, v, name, tr):
    r, width = w.shape

    def body(g_ref, *refs):
        _adamw_update(_sum_sources([g_ref], 0), *refs)

    spec = pl.BlockSpec((tr, width), lambda i: (i, 0))
    return pl.pallas_call(
        body, name=name, grid=(r // tr,),
        in_specs=[pl.BlockSpec((N_DEV, tr, width), lambda i: (0, i, 0)), spec, spec, spec],
        out_specs=[spec] * 4, out_shape=[jax.ShapeDtypeStruct((r, width), F32)] * 4,
        compiler_params=_params(),
    )(g8, w, m, v)


def _adamw_rows(g8s, w, m, v, name):
    layers, n, width = w.shape
    tr = max(r for r in range(BF16_SUBLANES, ADAMW_MAX_ROWS + 1, BF16_SUBLANES) if n % r == 0)
    nb = n // tr

    def body(*refs):
        _adamw_update(_sum_sources(refs[:layers], pl.program_id(0)), *refs[layers:])

    spec = pl.BlockSpec((None, tr, width), lambda l, j: (l, j, 0))
    return pl.pallas_call(
        body, name=name, grid=(layers, nb),
        in_specs=[pl.BlockSpec((N_DEV, tr, width), _layer_block(l, nb)) for l in range(layers)] + [spec] * 3,
        out_specs=[spec] * 4, out_shape=[jax.ShapeDtypeStruct(w.shape, F32)] * 4,
        compiler_params=pltpu.CompilerParams(dimension_semantics=("arbitrary", "arbitrary"),
                                             vmem_limit_bytes=VMEM_LIMIT),
    )(*g8s, w, m, v)


def _adamw_cols(g8s, w, m, v, name):
    layers, k, n = w.shape
    cb = min(BLOCK, n)
    nb = pl.cdiv(n, cb)

    def body(*refs):
        _adamw_update(_sum_sources(refs[:layers], pl.program_id(0)).T, *refs[layers:])

    spec = pl.BlockSpec((None, k, cb), lambda l, j: (l, 0, j))
    return pl.pallas_call(
        body, name=name, grid=(layers, nb),
        in_specs=[pl.BlockSpec((N_DEV, cb, k), _layer_block(l, nb)) for l in range(layers)] + [spec] * 3,
        out_specs=[spec] * 4, out_shape=[jax.ShapeDtypeStruct(w.shape, F32)] * 4,
        compiler_params=pltpu.CompilerParams(dimension_semantics=("arbitrary", "arbitrary"),
                                             vmem_limit_bytes=VMEM_LIMIT),
    )(*g8s, w, m, v)


def _local_step(x, target, wba, shard_b, lay, sm, tm, tk):
    t, d = x.shape
    f = lay.f
    w_dw32 = jnp.concatenate([sm["w_dw"], jnp.zeros((HALO - CONV_WIDTH, d), F32)], axis=0)
    lmg, lmb, lfg, lfb = sm["ln_mix_g"], sm["ln_mix_b"], sm["ln_ffn_g"], sm["ln_ffn_b"]
    bkv = jnp.concatenate([sm["b_k"], sm["b_v"]], axis=1)
    bm_f = f // 2 if (f // 2) % 128 == 0 else f
    tm_light = 2 * tm

    received = {}

    def exchange(grads):
        return _HostedExchange([g.reshape(N_DEV, lay.n[n], d) for n, g in grads.items()])

    def keep(grads, arrived):
        received.update(zip(grads, arrived))

    xb0, ag, xhc, rsc, xh1, rs1, wbb = _conv_fwd(x, wba, lay, w_dw32, sm["b_pw1"], sm["b_dw"], sm["cg"],
                                                 sm["cb"], sm["b_pw2"], lmg[0:1], lmb[0:1], tm_light,
                                                 hosted=_HostedGather(shard_b))
    wkv = wbb[:, lay.goff["wkv"]:lay.goff["wkv"] + lay.n["wkv"], :].reshape(d, 2 * BLOCK)
    x1b, hg0, hu0, xh2, rs2, x2b, kvs = _ffn_fwd(xh1, lmg[0:1], lmb[0:1], wbb, lay, 0, tm_light,
                                                kv=(lfg[0:1], lfb[0:1], wkv, bkv))
    q, o, lse, xh3, rs3 = _attn_fwd(xh2, lfg[0:1], lfb[0:1], x2b, kvs, wbb, lay, sm["b_q"], sm["sinks"],
                                    sm["b_o"], tm)
    x3b, hg1, hu1, dz4, st4, loss = _ffn_fwd(xh3, lmg[1:2], lmb[1:2], wbb, lay, 1, tm,
                                             loss=(lfg[1:2], lfb[1:2], target))

    dz4b, act1, dhg1, dhu1, dz3, st3 = _ffn_bwd(dz4, hg1, hu1, xh3, rs3, lmg[1:2], wbb, lay, 1, tm)
    g1 = {"gt1": _tn_matmul(dhg1, x3b, "dw_gate1", bm_f, tk), "ut1": _tn_matmul(dhu1, x3b, "dw_up1", bm_f, tk),
          "dn1": _tn_matmul(act1, dz4b, "dw_down1", bm_f, tk)}
    dz3b, dq, dkc, dkp, stq, dsinks, *arrived = _attn_bwd(dz3, q, o, lse, kvs, wbb, lay, sm["sinks"], tm,
                                                          hosted=exchange(g1))
    keep(g1, arrived)
    g2 = {"wq": _tn_matmul(x2b, dq, "dw_q", d, tk), "wo": _tn_matmul(o, dz3b, "dw_o", d, tk)}
    dz2b, act0, dhg0, dhu0, dz1, st1, dkv, st2, dbkv, *arrived = _ffn_bwd(
        dz3, hg0, hu0, xh1, rs1, lmg[0:1], wbb, lay, 0, tm, hosted=exchange(g2),
        qkv=(dq, dkc, dkp, xh2, rs2, lfg[0:1], wkv))
    keep(g2, arrived)
    dz1b, s_act, dcv, stc = _conv_bwd1(dz1, xhc, rsc, wba, lay, sm["cg"], sm["cb"], tm_light)
    g3 = {"gt0": _tn_matmul(dhg0, x1b, "dw_gate0", bm_f, tk), "ut0": _tn_matmul(dhu0, x1b, "dw_up0", bm_f, tk),
          "dn0": _tn_matmul(act0, dz2b, "dw_down0", bm_f, tk), "pw2": _tn_matmul(s_act, dz1b, "dw_pw2", d, tk),
          "wkv": _tn_matmul(x2b, dkv, "dw_kv", d, tk)}
    grad_x, dh1, dwdw, db1, *arrived = _conv_bwd2(dz1, dcv, ag, wba, lay, w_dw32, tm, hosted=exchange(g3))
    keep(g3, arrived)
    g_pw1t = _tn_matmul(dh1, xb0, "dw_pw1", d, tk)
    small = {
        "w_dw": dwdw[:CONV_WIDTH], "b_pw1": db1, "b_dw": stc[2:3], "cg": stc[0:1], "cb": stc[1:2],
        "b_pw2": stc[3:4], "b_k": dbkv[:, :BLOCK], "b_v": dbkv[:, BLOCK:], "b_q": stq[0:1],
        "sinks": dsinks[:, :d // HEAD_DIM],
        "b_o": stq[1:2],
        "ln_mix_g": jnp.concatenate([st1[0:1], st3[0:1]], axis=0),
        "ln_mix_b": jnp.concatenate([st1[1:2], st3[1:2]], axis=0),
        "ln_ffn_g": jnp.concatenate([st2[0:1], st4[0:1]], axis=0),
        "ln_ffn_b": jnp.concatenate([st2[1:2], st4[1:2]], axis=0),
    }
    return loss[0, 0], grad_x, received, g_pw1t, small


SP_ROWS = 40
SP_BDW, SP_CG, SP_CB, SP_BPW2, SP_BPW1 = 32, 33, 34, 35, 36
RP_NAMES = ("ln_mix_g", "ln_mix_b", "ln_ffn_g", "ln_ffn_b", "b_q", "b_o", "b_k", "b_v", "sinks")


def _row_forms(d, pw1, pw2, wq, wo, gate, up, down, wk, wv):
    rf = {"pw1t": pw1[0].T, "pw2": pw2[0], "wq": wq[0], "wo": wo[0],
          "wkv": jnp.concatenate([wk, wv], axis=1).reshape(-1, d)}
    for l in range(DEPTH):
        rf.update({f"gt{l}": gate[l].T, f"ut{l}": up[l].T, f"dn{l}": down[l]})
    return rf


def _pack_rows(rf, names):
    return jnp.concatenate([rf[n] for n in names], axis=0)


def _pack_small(w_dw, b_dw, cg, cb, b_pw2, b_pw1):
    cw = b_dw.shape[1]
    z = jnp.zeros((1, cw), F32)
    return jnp.concatenate([w_dw[0], z, b_dw, cg, cb, b_pw2, b_pw1.reshape(2, cw), z, z], axis=0)


def _unpack_small(p):
    cw = p.shape[1]
    return dict(w_dw=p[None, :CONV_WIDTH], b_dw=p[SP_BDW:SP_BDW + 1], cg=p[SP_CG:SP_CG + 1],
                cb=p[SP_CB:SP_CB + 1], b_pw2=p[SP_BPW2:SP_BPW2 + 1],
                b_pw1=p[SP_BPW1:SP_BPW1 + 2].reshape(1, 2 * cw))


def _small_full(g):
    d = N_DEV * g.shape[2]

    def wide(r0, n=1):
        return jnp.transpose(g[:, r0:r0 + n], (1, 0, 2)).reshape(n, d)

    return dict(w_dw=wide(0, CONV_WIDTH), b_dw=wide(SP_BDW), cg=wide(SP_CG), cb=wide(SP_CB),
                b_pw2=wide(SP_BPW2), b_pw1=g[:, SP_BPW1:SP_BPW1 + 2].reshape(1, 2 * d))


def _small_grad_blocks(sg):
    cw = sg["b_dw"].shape[1] // N_DEV

    def narrow(a):
        return jnp.transpose(a.reshape(a.shape[0], N_DEV, cw), (1, 0, 2))

    z = jnp.zeros((N_DEV, 1, cw), F32)
    return jnp.concatenate([narrow(sg["w_dw"]), z, narrow(sg["b_dw"]), narrow(sg["cg"]), narrow(sg["cb"]),
                            narrow(sg["b_pw2"]), sg["b_pw1"].reshape(N_DEV, 2, cw), z, z], axis=1)


def _pack_rep(vals, rider=0.0):
    parts = []
    for name in RP_NAMES:
        a = vals[name].reshape(-1)
        pad = -a.shape[0] % 128
        parts.append(jnp.concatenate([a, jnp.zeros((pad,), F32)]).reshape(-1, 128))
    parts.append(jnp.full((1, 128), rider, F32))
    rows = sum(p.shape[0] for p in parts)
    parts.append(jnp.zeros((-rows % 8, 128), F32))
    return jnp.concatenate(parts, axis=0)


def _rider_row(shapes):
    return sum(-(-_size(shapes[name]) // 128) for name in RP_NAMES)


def _size(shape):
    n = 1
    for s in shape:
        n *= s
    return n


def _unpack_rep(p, shapes):
    out, r = {}, 0
    for name in RP_NAMES:
        n = _size(shapes[name])
        rows = -(-n // 128)
        out[name] = p[r:r + rows].reshape(-1)[:n].reshape(shapes[name])
        r += rows
    return out


def kernel(x, conv_w_pw1, conv_b_pw1, conv_w_dw, conv_b_dw, conv_ln_g, conv_ln_b, conv_w_pw2, conv_b_pw2, kv_w_k, kv_b_k, kv_w_v, kv_b_v, attn_w_q, attn_b_q, attn_sinks, attn_w_o, attn_b_o, ffn_w_gate, ffn_w_up, ffn_w_down, ln_mix_g, ln_mix_b, ln_ffn_g, ln_ffn_b, loss_target, m_conv_w_pw1, m_conv_b_pw1, m_conv_w_dw, m_conv_b_dw, m_conv_ln_g, m_conv_ln_b, m_conv_w_pw2, m_conv_b_pw2, m_kv_w_k, m_kv_b_k, m_kv_w_v, m_kv_b_v, m_attn_w_q, m_attn_b_q, m_attn_sinks, m_attn_w_o, m_attn_b_o, m_ffn_w_gate, m_ffn_w_up, m_ffn_w_down, m_ln_mix_g, m_ln_mix_b, m_ln_ffn_g, m_ln_ffn_b, v_conv_w_pw1, v_conv_b_pw1, v_conv_w_dw, v_conv_b_dw, v_conv_ln_g, v_conv_ln_b, v_conv_w_pw2, v_conv_b_pw2, v_kv_w_k, v_kv_b_k, v_kv_w_v, v_kv_b_v, v_attn_w_q, v_attn_b_q, v_attn_sinks, v_attn_w_o, v_attn_b_o, v_ffn_w_gate, v_ffn_w_up, v_ffn_w_down, v_ln_mix_g, v_ln_mix_b, v_ln_ffn_g, v_ln_ffn_b):
    t, d = x.shape[1], x.shape[2]
    f = ffn_w_gate.shape[2] * N_DEV
    lay = _Layout(d, f)
    tm, tk = 256, min(2048, t)

    rep_shapes = dict(ln_mix_g=ln_mix_g.shape, ln_mix_b=ln_mix_b.shape, ln_ffn_g=ln_ffn_g.shape,
                      ln_ffn_b=ln_ffn_b.shape, b_q=attn_b_q.shape, b_o=attn_b_o.shape, b_k=kv_b_k.shape,
                      b_v=kv_b_v.shape, sinks=attn_sinks.shape)

    def rep_pack(lmg, lmb, lfg, lfb, bq, bo, bk, bv, sk):
        return _pack_rep(dict(ln_mix_g=lmg, ln_mix_b=lmb, ln_ffn_g=lfg, ln_ffn_b=lfb, b_q=bq, b_o=bo,
                              b_k=bk, b_v=bv, sinks=sk))

    w_rf = _row_forms(d, conv_w_pw1, conv_w_pw2, attn_w_q, attn_w_o, ffn_w_gate, ffn_w_up, ffn_w_down, kv_w_k, kv_w_v)
    w_small = _pack_small(conv_w_dw, conv_b_dw, conv_ln_g, conv_ln_b, conv_b_pw2, conv_b_pw1)
    m_small = _pack_small(m_conv_w_dw, m_conv_b_dw, m_conv_ln_g, m_conv_ln_b, m_conv_b_pw2, m_conv_b_pw1)
    v_small = _pack_small(v_conv_w_dw, v_conv_b_dw, v_conv_ln_g, v_conv_ln_b, v_conv_b_pw2, v_conv_b_pw1)
    w_rep = rep_pack(ln_mix_g, ln_mix_b, ln_ffn_g, ln_ffn_b, attn_b_q, attn_b_o, kv_b_k, kv_b_v, attn_sinks)
    m_rep = rep_pack(m_ln_mix_g, m_ln_mix_b, m_ln_ffn_g, m_ln_ffn_b, m_attn_b_q, m_attn_b_o, m_kv_b_k, m_kv_b_v, m_attn_sinks)
    v_rep = rep_pack(v_ln_mix_g, v_ln_mix_b, v_ln_ffn_g, v_ln_ffn_b, v_attn_b_q, v_attn_b_o, v_kv_b_k, v_kv_b_v, v_attn_sinks)

    wba, smg = _all_gather([_pack_rows(w_rf, lay.GATHER["a"]).astype(BF16), w_small], "gather_conv_weights")
    shard_b = _pack_rows(w_rf, lay.GATHER["b"]).astype(BF16)
    sm = _small_full(smg)
    sm.update(ln_mix_g=ln_mix_g, ln_mix_b=ln_mix_b, ln_ffn_g=ln_ffn_g, ln_ffn_b=ln_ffn_b, b_q=attn_b_q,
              b_o=attn_b_o, sinks=attn_sinks, b_k=kv_b_k.reshape(1, -1), b_v=kv_b_v.reshape(1, -1))

    loss_part, grad_x, received, g_pw1t, gsmall = _local_step(x[0], loss_target[0], wba, shard_b, lay, sm, tm, tk)

    received["pw1t"], g8_small, g8_rep = _exchange(
        [g_pw1t.reshape(N_DEV, lay.n["pw1t"], d), _small_grad_blocks(gsmall), _pack_rep(gsmall, loss_part)],
        "exchange_last_grads")

    def kv_rows(wk, wv):
        return jnp.concatenate([wk, wv], axis=1).reshape(1, -1, d)

    def kv_split(a):
        a = a.reshape(d // N_DEV, 2 * BLOCK)
        return a[:, :BLOCK], a[:, BLOCK:]

    big = dict(
        pw1=_adamw_cols([received["pw1t"]], conv_w_pw1, m_conv_w_pw1, v_conv_w_pw1, "adamw_pw1"),
        gate=_adamw_cols([received["gt0"], received["gt1"]], ffn_w_gate, m_ffn_w_gate, v_ffn_w_gate, "adamw_gate"),
        up=_adamw_cols([received["ut0"], received["ut1"]], ffn_w_up, m_ffn_w_up, v_ffn_w_up, "adamw_up"),
        down=_adamw_rows([received["dn0"], received["dn1"]], ffn_w_down, m_ffn_w_down, v_ffn_w_down, "adamw_down"),
        pw2=_adamw_rows([received["pw2"]], conv_w_pw2, m_conv_w_pw2, v_conv_w_pw2, "adamw_pw2"),
        wq=_adamw_rows([received["wq"]], attn_w_q, m_attn_w_q, v_attn_w_q, "adamw_q"),
        wo=_adamw_rows([received["wo"]], attn_w_o, m_attn_w_o, v_attn_w_o, "adamw_o"),
        wkv=[kv_split(a) for a in _adamw_rows([received["wkv"]], kv_rows(kv_w_k, kv_w_v), kv_rows(m_kv_w_k, m_kv_w_v),
                                              kv_rows(v_kv_w_k, v_kv_w_v), "adamw_kv")])
    big_out = [dict(pw1=big["pw1"][i], pw2=big["pw2"][i], wq=big["wq"][i], wo=big["wo"][i], gate=big["gate"][i],
                    up=big["up"][i], down=big["down"][i], wk=big["wkv"][i][0], wv=big["wkv"][i][1])
               for i in range(4)]
    small_out = [_unpack_small(a) for a in _adamw_sum(g8_small, w_small, m_small, v_small, "adamw_small", SP_ROWS)]
    rep_res = _adamw_sum(g8_rep, w_rep, m_rep, v_rep, "adamw_rep", w_rep.shape[0])
    rep_out = [_unpack_rep(a, rep_shapes) for a in rep_res]
    loss = rep_res[0][_rider_row(rep_shapes), 0]

    outs = [loss, grad_x[None]]
    for b, s, r in zip(big_out, small_out, rep_out):
        outs += [b["pw1"], s["b_pw1"], s["w_dw"], s["b_dw"], s["cg"], s["cb"], b["pw2"], s["b_pw2"],
                 b["wk"], r["b_k"], b["wv"], r["b_v"], b["wq"], r["b_q"], r["sinks"], b["wo"], r["b_o"],
                 b["gate"], b["up"], b["down"], r["ln_mix_g"], r["ln_mix_b"], r["ln_ffn_g"], r["ln_ffn_b"]]
    return tuple(outs)
```

```python
import jax
import jax.numpy as jnp
from jax import lax
from jax.experimental import pallas as pl
from jax.experimental.pallas import tpu as pltpu

F32 = jnp.float32
BF16 = jnp.bfloat16

N_DEV = 8
HEAD_DIM = 64
N_KV_HEADS = 2
BLOCK = 128
CONV_WIDTH = 31
HALO = 32
ALIBI_MAX = 8.0
DEPTH = 2
ALPHA = (2.0 * DEPTH) ** 0.25
LN_EPS = 1e-5
MASKED_DIST = 1e32
ADAM_LR = 0.001
ADAM_B1 = 0.9
ADAM_B2 = 0.999
ADAM_EPS = 1e-08
ADAM_WD = 0.01
ADAM_STEP = 10
VMEM_LIMIT = 56 * 1024 * 1024
BF16_SUBLANES = 16
ADAMW_MAX_ROWS = 176
MESH = pl.DeviceIdType.MESH


def _dot(a, b):
    return jnp.dot(a, b, preferred_element_type=F32)


def _dot_nt(a, b):
    return lax.dot_general(a, b, (((1,), (1,)), ((), ())), preferred_element_type=F32)


def _dot_tn(a, b):
    return lax.dot_general(a, b, (((0,), (0,)), ((), ())), preferred_element_type=F32)


def _sigmoid(v):
    return 1.0 / (1.0 + jnp.exp(-v))


def _ln_fwd(z):
    mu = jnp.mean(z, axis=-1, keepdims=True)
    zc = z - mu
    var = jnp.mean(zc * zc, axis=-1, keepdims=True)
    rstd = lax.rsqrt(var + LN_EPS)
    return zc * rstd, rstd


def _ln_bwd(dout, xh, rstd, g):
    dxh = dout * g
    m1 = jnp.mean(dxh, axis=-1, keepdims=True)
    m2 = jnp.mean(dxh * xh, axis=-1, keepdims=True)
    dz = rstd * (dxh - m1 - xh * m2)
    return dz, jnp.sum(dout * xh, axis=0, keepdims=True), jnp.sum(dout, axis=0, keepdims=True)


def _params(vmem=VMEM_LIMIT):
    return pltpu.CompilerParams(dimension_semantics=("arbitrary",), vmem_limit_bytes=vmem)


def _row(d):
    return pl.BlockSpec((1, d), lambda i: (0, 0))


def _tile(tm, d):
    return pl.BlockSpec((tm, d), lambda i: (i, 0))


def _fixed(r, d):
    return pl.BlockSpec((r, d), lambda i: (0, 0))


def _tile_cur(tm, d, nsteps):
    return pl.BlockSpec((tm, d), lambda i: (jnp.minimum(i, nsteps - 1), 0))


def _tile_prev(tm, d):
    return pl.BlockSpec((tm, d), lambda i: (jnp.maximum(i - 1, 0), 0))


ANY = pl.BlockSpec(memory_space=pl.ANY)


class _Layout:
    GATHER = {"a": ("pw1t", "pw2"),
              "b": ("wq", "wo", "gt0", "ut0", "dn0", "gt1", "ut1", "dn1", "wkv")}

    def __init__(self, d, f):
        self.d, self.f = d, f
        self.n = {"pw1t": 2 * d // N_DEV, "pw2": d // N_DEV, "wq": d // N_DEV, "wo": d // N_DEV,
                  "wkv": (d // N_DEV) * 2 * BLOCK // d}
        for l in range(DEPTH):
            self.n.update({f"gt{l}": f // N_DEV, f"ut{l}": f // N_DEV, f"dn{l}": f // N_DEV})
        self.goff = {}
        for names in self.GATHER.values():
            r = 0
            for name in names:
                self.goff[name] = r
                r += self.n[name]


def _load_weight(wb_ref, lay, name, dst):
    n = lay.n[name]
    for p in range(N_DEV):
        pltpu.sync_copy(wb_ref.at[p, pl.ds(lay.goff[name], n), :], dst.at[pl.ds(p * n, n), :])


def _wscratch(lay, name):
    return pltpu.VMEM((N_DEV * lay.n[name], lay.d), BF16)


def _wfull(ref):
    return ref[...]


def _wrows(ref, r0, nrows):
    return ref[r0:r0 + nrows, :]


def _me():
    return lax.axis_index("x"), lax.axis_index("y"), lax.axis_index("c")


def _peer(mask):
    x, y, c = _me()
    return (1 - x if mask & 4 else x, 1 - y if mask & 2 else y, 1 - c if mask & 1 else c)


def _index(dev):
    return 4 * dev[0] + 2 * dev[1] + dev[2]


class _HostedGather:
    def __init__(self, array):
        self.arrays = [array]
        self.out_shapes = [jax.ShapeDtypeStruct((N_DEV,) + array.shape, array.dtype)]

    def scratch(self):
        return [pltpu.SemaphoreType.DMA((7,)), pltpu.SemaphoreType.DMA((7,)), pltpu.SemaphoreType.DMA(())]

    def _copies(self, ins, outs, send_sems, recv_sems, local_sem):
        out = outs[0]
        x, y, c = _me()
        me, sibling = (x, y, c), (x, y, 1 - c)
        chips = [(1 - x, y), (x, 1 - y), (1 - x, 1 - y)]

        def copy(k, block, to, src=None):
            rows = out.at[_index(block)]
            return pltpu.make_async_remote_copy(
                src_ref=rows if src is None else src, dst_ref=rows, send_sem=send_sems.at[k],
                recv_sem=recv_sems.at[k], device_id=to, device_id_type=MESH)

        return dict(
            mine=lambda: pltpu.make_async_copy(ins[0], out.at[_index(me)], local_sem),
            first=lambda: [copy(0, me, sibling, src=ins[0])] + [copy(1 + j, me, (*chip, c), src=ins[0])
                                                                for j, chip in enumerate(chips)],
            over_ici=lambda: [copy(1 + j, (*chip, c), me) for j, chip in enumerate(chips)],
            passed=lambda: [copy(4 + j, (*chip, c), sibling) for j, chip in enumerate(chips)],
            from_sibling=lambda: [copy(0, sibling, me)] + [copy(4 + j, (*chip, 1 - c), me)
                                                           for j, chip in enumerate(chips)])

    def start(self, *refs):
        cp = self._copies(*refs)
        cp["mine"]().start()
        for c in cp["first"]():
            c.start()

    def middle(self, *refs):
        cp = self._copies(*refs)
        for arrived, onward in zip(cp["over_ici"](), cp["passed"]()):
            arrived.wait_recv()
            onward.start()

    def finish(self, *refs):
        cp = self._copies(*refs)
        for c in cp["from_sibling"]():
            c.wait_recv()
        for c in cp["first"]() + cp["passed"]():
            c.wait_send()
        cp["mine"]().wait()


class _HostedExchange:
    def __init__(self, arrays):
        self.arrays = list(arrays)
        self.out_shapes = [jax.ShapeDtypeStruct(a.shape, a.dtype) for a in self.arrays]

    def scratch(self):
        n = len(self.arrays)
        return [pltpu.SemaphoreType.DMA((n, 7)), pltpu.SemaphoreType.DMA((n, 7)), pltpu.SemaphoreType.DMA((n,))]

    def _copies(self, ins, outs, send_sems, recv_sems, local_sems):
        me = _index(_me())

        def dst(k, src_dev):
            return outs[k].at[src_dev]

        pairs = [(k, mask) for k in range(len(self.arrays)) for mask in range(1, N_DEV)]

        def local():
            return [pltpu.make_async_copy(ins[k].at[me], dst(k, me), local_sems.at[k])
                    for k in range(len(self.arrays))]

        def sends():
            return [pltpu.make_async_remote_copy(
                src_ref=ins[k].at[_index(_peer(mask))], dst_ref=dst(k, me), send_sem=send_sems.at[k, mask - 1],
                recv_sem=recv_sems.at[k, mask - 1], device_id=_peer(mask), device_id_type=MESH)
                for k, mask in pairs]

        def arrivals():
            return [pltpu.make_async_remote_copy(
                src_ref=ins[k].at[me], dst_ref=dst(k, _index(_peer(mask))), send_sem=send_sems.at[k, mask - 1],
                recv_sem=recv_sems.at[k, mask - 1], device_id=_me(), device_id_type=MESH)
                for k, mask in pairs]

        return local, sends, arrivals

    def start(self, *refs):
        local, sends, _ = self._copies(*refs)
        for c in local() + sends():
            c.start()

    def middle(self, *refs):
        pass

    def finish(self, *refs):
        local, sends, arrivals = self._copies(*refs)
        for c in arrivals():
            c.wait_recv()
        for c in sends():
            c.wait_send()
        for c in local():
            c.wait()


HAND_ON_AT = 6


def _gridded_call(body, name, nsteps, in_specs, out_specs, out_shape, scratch, args, hosted=None):
    if hosted is None:
        return pl.pallas_call(body, name=name, grid=(nsteps,), in_specs=in_specs, out_specs=out_specs,
                              out_shape=out_shape, scratch_shapes=scratch, compiler_params=_params())(*args)
    n_in, n_out, n_scr, h_in = len(in_specs), len(out_specs), len(scratch), len(hosted.arrays)
    h_out = len(hosted.out_shapes)

    def with_hosted(*refs):
        a = n_in + h_in
        b = a + n_out
        e = b + h_out + n_scr
        comm = (refs[n_in:a], refs[b:b + h_out], refs[e], refs[e + 1], refs[e + 2])
        i = pl.program_id(0)

        @pl.when(i == 0)
        def _():
            hosted.start(*comm)

        body(*refs[:n_in], *refs[a:b], *refs[b + h_out:e])

        @pl.when(i == HAND_ON_AT * nsteps // 8)
        def _():
            hosted.middle(*comm)

        @pl.when(i == nsteps - 1)
        def _():
            hosted.finish(*comm)

    return pl.pallas_call(
        with_hosted, name=name, grid=(nsteps,), in_specs=list(in_specs) + [ANY] * h_in,
        out_specs=list(out_specs) + [ANY] * h_out, out_shape=list(out_shape) + hosted.out_shapes,
        scratch_shapes=list(scratch) + hosted.scratch(), compiler_params=_params(),
    )(*args, *hosted.arrays)


CONV_RB = 64
CONV_LC = 128
CONV_MC = 256


def _shifted(win, r):
    return win if r == 0 else pltpu.roll(win, win.shape[0] - r, 0)


def _conv_fwd(x, wb, lay, w_dw, b_pw1, b_dw, cg, cb, b_pw2, lg, lb, tm, hosted=None):
    t, d = x.shape
    nsteps = t // tm

    def body(x_ref, xh_ref, wb_ref, wdw_ref, b1_ref, bdw_ref, cg_ref, cb_ref, b2_ref, lg_ref, lb_ref,
             xb_ref, ag_ref, xhc_ref, rsc_ref, xh1_ref, rs1_ref, w1_s, w2_s, ubuf, cv_s):
        i = pl.program_id(0)

        @pl.when(i == 0)
        def _():
            _load_weight(wb_ref, lay, "pw1t", w1_s)
            _load_weight(wb_ref, lay, "pw2", w2_s)

        xv = x_ref[...]
        xb = xv.astype(BF16)
        xb_ref[...] = xb
        xcat = jnp.concatenate([xh_ref[...].astype(BF16), xb], axis=0)
        for mc in range(d // CONV_MC):
            c0 = mc * CONV_MC
            acols, gcols = slice(c0, c0 + CONV_MC), slice(d + c0, d + c0 + CONV_MC)
            ha = _dot_nt(xcat, _wrows(w1_s, c0, CONV_MC)) + b1_ref[:, acols]
            hg = _dot_nt(xcat, _wrows(w1_s, d + c0, CONV_MC)) + b1_ref[:, gcols]
            ag_ref[:, acols] = ha[HALO:].astype(BF16)
            ag_ref[:, gcols] = hg[HALO:].astype(BF16)
            u = ha * _sigmoid(hg)
            u = jnp.concatenate([jnp.where(i > 0, u[:HALO], 0.0), u[HALO:], jnp.zeros((8, CONV_MC), F32)], axis=0)
            for r in range(8):
                ubuf[r, :, acols] = _shifted(u, r)
            for rb in range(tm // CONV_RB):
                t0 = rb * CONV_RB
                for lc in range(CONV_MC // CONV_LC):
                    lanes = slice(c0 + lc * CONV_LC, c0 + (lc + 1) * CONV_LC)
                    acc = jnp.zeros((CONV_RB, CONV_LC), F32)
                    for k in range(CONV_WIDTH):
                        s = HALO - (CONV_WIDTH - 1) + k
                        q = t0 + 8 * (s // 8)
                        acc = acc + ubuf[s % 8, q:q + CONV_RB, lanes] * wdw_ref[k:k + 1, lanes]
                    cv_s[t0:t0 + CONV_RB, lanes] = acc
        cv = cv_s[...] + bdw_ref[...]
        xhc, rsc = _ln_fwd(cv)
        xhc_ref[...] = xhc
        rsc_ref[...] = rsc
        n = xhc * cg_ref[...] + cb_ref[...]
        s_act = n * _sigmoid(n)
        m = _dot(s_act.astype(BF16), _wfull(w2_s)) + b2_ref[...]
        xh1, rs1 = _ln_fwd(ALPHA * xv + m)
        xh1_ref[...] = xh1
        rs1_ref[...] = rs1

    hb = tm // HALO
    return _gridded_call(
        body, "conv_fwd", nsteps,
        [_tile(tm, d), pl.BlockSpec((HALO, d), lambda i: (jnp.maximum(i * hb - 1, 0), 0)), ANY,
         _fixed(HALO, d), _row(2 * d), _row(d), _row(d), _row(d), _row(d), _row(d), _row(d)],
        [_tile(tm, d), _tile(tm, 2 * d), _tile(tm, d), _tile(tm, 1), _tile(tm, d), _tile(tm, 1)],
        [jax.ShapeDtypeStruct((t, d), BF16), jax.ShapeDtypeStruct((t, 2 * d), BF16),
         jax.ShapeDtypeStruct((t, d), F32), jax.ShapeDtypeStruct((t, 1), F32),
         jax.ShapeDtypeStruct((t, d), F32), jax.ShapeDtypeStruct((t, 1), F32)],
        [_wscratch(lay, "pw1t"), _wscratch(lay, "pw2"),
         pltpu.VMEM((8, HALO + tm + 8, d), F32), pltpu.VMEM((tm, d), F32)],
        (x, x, wb, w_dw, b_pw1, b_dw, cg, cb, b_pw2, lg, lb), hosted)


def _conv_bwd1(dz1, xhc, rsc, wb, lay, cg, cb, tm):
    t, d = dz1.shape

    def body(dz_ref, xhc_ref, rsc_ref, wb_ref, cg_ref, cb_ref, dzb_ref, s_ref, dcv_ref, st_ref, w2_s):
        i = pl.program_id(0)

        @pl.when(i == 0)
        def _():
            _load_weight(wb_ref, lay, "pw2", w2_s)
            st_ref[...] = jnp.zeros(st_ref.shape, F32)

        dz = dz_ref[...]
        dzb = dz.astype(BF16)
        dzb_ref[...] = dzb
        xhc_v = xhc_ref[...]
        n = xhc_v * cg_ref[...] + cb_ref[...]
        sg = _sigmoid(n)
        s_ref[...] = (n * sg).astype(BF16)
        ds = _dot_nt(dzb, _wfull(w2_s))
        dn = ds * (sg * (1.0 + n * (1.0 - sg)))
        dcv, dg, db = _ln_bwd(dn, xhc_v, rsc_ref[...], cg_ref[...])
        dcv_ref[...] = dcv
        st_ref[0:1, :] += dg
        st_ref[1:2, :] += db
        st_ref[2:3, :] += jnp.sum(dcv, axis=0, keepdims=True)
        st_ref[3:4, :] += jnp.sum(dz, axis=0, keepdims=True)

    return pl.pallas_call(
        body, name="conv_bwd1", grid=(t // tm,),
        in_specs=[_tile(tm, d), _tile(tm, d), _tile(tm, 1), ANY, _row(d), _row(d)],
        out_specs=[_tile(tm, d), _tile(tm, d), _tile(tm, d), _fixed(8, d)],
        out_shape=[jax.ShapeDtypeStruct((t, d), BF16), jax.ShapeDtypeStruct((t, d), BF16),
                   jax.ShapeDtypeStruct((t, d), F32), jax.ShapeDtypeStruct((8, d), F32)],
        scratch_shapes=[_wscratch(lay, "pw2")],
        compiler_params=_params(),
    )(dz1, xhc, rsc, wb, cg, cb)


def _conv_bwd2(dz1, dcv, ag, wb, lay, w_dw, tm, hosted=None):
    t, d = dz1.shape
    nsteps = t // tm

    def body(dz_ref, dcv_ref, dcvn_ref, ag_ref, wb_ref, wdw_ref,
             gx_ref, dh_ref, dw_ref, db1_ref, w1_s, ubuf, dbuf, du_s, dwacc):
        i = pl.program_id(0)

        @pl.when(i == 0)
        def _():
            _load_weight(wb_ref, lay, "pw1t", w1_s)
            dwacc[...] = jnp.zeros(dwacc.shape, F32)
            db1_ref[...] = jnp.zeros(db1_ref.shape, F32)

        gx = ALPHA * dz_ref[...]
        for mc in range(d // CONV_MC):
            c0 = mc * CONV_MC
            acols, gcols = slice(c0, c0 + CONV_MC), slice(d + c0, d + c0 + CONV_MC)
            a = ag_ref[:, acols].astype(F32)
            sg = _sigmoid(ag_ref[:, gcols].astype(F32))
            ubuf[:, acols] = a * sg
            dcv_next = jnp.where(i < nsteps - 1, dcvn_ref[:, acols], 0.0)
            dcv_c = jnp.concatenate([dcv_ref[:, acols], dcv_next, jnp.zeros((8, CONV_MC), F32)], axis=0)
            for r in range(8):
                dbuf[r, :, acols] = _shifted(dcv_c, r)
            for rb in range(tm // CONV_RB):
                t0 = rb * CONV_RB
                for lc in range(CONV_MC // CONV_LC):
                    lanes = slice(c0 + lc * CONV_LC, c0 + (lc + 1) * CONV_LC)
                    ucur = ubuf[t0:t0 + CONV_RB, lanes]
                    acc = jnp.zeros((CONV_RB, CONV_LC), F32)
                    for k in range(CONV_WIDTH):
                        sd = CONV_WIDTH - 1 - k
                        q = t0 + 8 * (sd // 8)
                        dk = dbuf[sd % 8, q:q + CONV_RB, lanes]
                        acc = acc + dk * wdw_ref[k:k + 1, lanes]
                        prod = ucur * dk
                        part = prod[0:8]
                        for j in range(1, CONV_RB // 8):
                            part = part + prod[8 * j:8 * j + 8]
                        dwacc[k, :, lanes] += part
                    du_s[t0:t0 + CONV_RB, lanes] = acc
            du = du_s[:, acols]
            da = du * sg
            dg = du * a * sg * (1.0 - sg)
            dab, dgb = da.astype(BF16), dg.astype(BF16)
            dh_ref[:, acols] = dab
            dh_ref[:, gcols] = dgb
            db1_ref[:, acols] += jnp.sum(da, axis=0, keepdims=True)
            db1_ref[:, gcols] += jnp.sum(dg, axis=0, keepdims=True)
            gx = gx + _dot(dab, _wrows(w1_s, c0, CONV_MC)) + _dot(dgb, _wrows(w1_s, d + c0, CONV_MC))
        gx_ref[...] = gx

        @pl.when(i == nsteps - 1)
        def _():
            dw_ref[...] = jnp.sum(dwacc[...], axis=1)

    hb = tm // HALO
    last = t // HALO - 1
    return _gridded_call(
        body, "conv_bwd2", nsteps,
        [_tile(tm, d), _tile(tm, d),
         pl.BlockSpec((HALO, d), lambda i: (jnp.minimum((i + 1) * hb, last), 0)),
         _tile(tm, 2 * d), ANY, _fixed(HALO, d)],
        [_tile(tm, d), _tile(tm, 2 * d), _fixed(HALO, d), _row(2 * d)],
        [jax.ShapeDtypeStruct((t, d), F32), jax.ShapeDtypeStruct((t, 2 * d), BF16),
         jax.ShapeDtypeStruct((HALO, d), F32), jax.ShapeDtypeStruct((1, 2 * d), F32)],
        [_wscratch(lay, "pw1t"), pltpu.VMEM((tm, d), F32),
         pltpu.VMEM((8, HALO + tm + 8, d), F32), pltpu.VMEM((tm, d), F32),
         pltpu.VMEM((HALO, 8, d), F32)],
        (dz1, dcv, dcv, ag, wb, w_dw), hosted)


FFN_FC = 256
FFN_AHEAD = 1


def _ffn_fwd(xh_in, g_in, b_in, wb, lay, layer, tm, *, kv=None, loss=None):
    t, d = xh_in.shape
    f = lay.f
    names = (f"gt{layer}", f"ut{layer}", f"dn{layer}")

    def body(*refs):
        xh_ref, gi_ref, bi_ref, wb_ref = refs[:4]
        pos = 4
        if kv is not None:
            go_ref, bo_ref, wkv_ref, bkv_ref = refs[pos:pos + 4]
            pos += 4
        if loss is not None:
            go_ref, bo_ref, tgt_ref = refs[pos:pos + 3]
            pos += 3
        xb_ref, hg_ref, hu_ref = refs[pos:pos + 3]
        pos += 3
        if kv is not None:
            xho_ref, rso_ref, xob_ref, kv_ref = refs[pos:pos + 4]
            pos += 4
        if loss is not None:
            dz_ref, st_ref, loss_ref = refs[pos:pos + 3]
            pos += 3
        gt_s, ut_s, dn_s, xin_s, fo_s = refs[pos:pos + 5]
        i = pl.program_id(0)

        @pl.when(i == 0)
        def _():
            for name, dst in zip(names, (gt_s, ut_s, dn_s)):
                _load_weight(wb_ref, lay, name, dst)
            xin_s[...] = jnp.zeros(xin_s.shape, F32)
            fo_s[...] = jnp.zeros(fo_s.shape, F32)
            if loss is not None:
                st_ref[...] = jnp.zeros(st_ref.shape, F32)
                loss_ref[...] = jnp.zeros(loss_ref.shape, F32)

        xin_prev = xin_s[...]
        xho, rso = _ln_fwd(ALPHA * xin_prev + fo_s[...])
        if kv is not None:
            xho_ref[...] = xho
            rso_ref[...] = rso
            xob_ref[...] = (xho * go_ref[...] + bo_ref[...]).astype(BF16)
        if loss is not None:
            real = i > 0
            diff = xho * go_ref[...] + bo_ref[...] - tgt_ref[...]
            loss_ref[...] += jnp.where(real, (0.5 / d) * jnp.sum(diff * diff), 0.0)
            dz, dg, db = _ln_bwd(diff * (1.0 / d), xho, rso, go_ref[...])
            dz_ref[...] = dz
            st_ref[0:1, :] += jnp.where(real, dg, 0.0)
            st_ref[1:2, :] += jnp.where(real, db, 0.0)

        xin = xh_ref[...] * gi_ref[...] + bi_ref[...]
        xb = xin.astype(BF16)
        xb_ref[...] = xb

        def up(c):
            return (_dot_nt(xb, _wrows(gt_s, c * FFN_FC, FFN_FC)), _dot_nt(xb, _wrows(ut_s, c * FFN_FC, FFN_FC)))

        fo = jnp.zeros((tm, d), F32)
        nc = f // FFN_FC
        ahead = [up(c) for c in range(min(FFN_AHEAD, nc))]
        for c in range(nc):
            rows = slice(c * FFN_FC, (c + 1) * FFN_FC)
            hg, hu = ahead.pop(0)
            if c + FFN_AHEAD < nc:
                ahead.append(up(c + FFN_AHEAD))
            hg_ref[:, rows] = hg.astype(BF16)
            hu_ref[:, rows] = hu.astype(BF16)
            act = hg * _sigmoid(hg) * hu
            fo = fo + _dot(act.astype(BF16), _wrows(dn_s, c * FFN_FC, FFN_FC))
        xin_s[...] = xin
        fo_s[...] = fo
        if kv is not None:
            kv_ref[...] = (_dot(xob_ref[...], wkv_ref[...]) + bkv_ref[...]).astype(BF16)

    nsteps = t // tm
    in_specs = [_tile_cur(tm, d, nsteps), _row(d), _row(d), ANY]
    args = [xh_in, g_in, b_in, wb]
    out_specs = [_tile_cur(tm, d, nsteps), _tile_cur(tm, f, nsteps), _tile_cur(tm, f, nsteps)]
    out_shape = [jax.ShapeDtypeStruct((t, d), BF16), jax.ShapeDtypeStruct((t, f), BF16),
                 jax.ShapeDtypeStruct((t, f), BF16)]
    if kv is not None:
        in_specs += [_row(d), _row(d), _fixed(d, 2 * BLOCK), _row(2 * BLOCK)]
        args += list(kv)
        out_specs += [_tile_prev(tm, d), _tile_prev(tm, 1), _tile_prev(tm, d), _tile_prev(tm, 2 * BLOCK)]
        out_shape += [jax.ShapeDtypeStruct((t, d), F32), jax.ShapeDtypeStruct((t, 1), F32),
                      jax.ShapeDtypeStruct((t, d), BF16), jax.ShapeDtypeStruct((t, 2 * BLOCK), BF16)]
    if loss is not None:
        in_specs += [_row(d), _row(d), _tile_prev(tm, d)]
        args += list(loss)
        out_specs += [_tile_prev(tm, d), _fixed(8, d), _fixed(8, 128)]
        out_shape += [jax.ShapeDtypeStruct((t, d), F32), jax.ShapeDtypeStruct((8, d), F32),
                      jax.ShapeDtypeStruct((8, 128), F32)]
    return pl.pallas_call(
        body, name=f"ffn_fwd{layer}", grid=(nsteps + 1,), in_specs=in_specs, out_specs=out_specs,
        out_shape=out_shape,
        scratch_shapes=[_wscratch(lay, n) for n in names] + [pltpu.VMEM((tm, d), F32), pltpu.VMEM((tm, d), F32)],
        compiler_params=_params(),
    )(*args)


def _ffn_bwd(dz, hg, hu, xh_in, rs_in, g_in, wb, lay, layer, tm, hosted=None, qkv=None):
    t, d = dz.shape
    f = lay.f
    nsteps = t // tm
    nbt = tm // BLOCK
    names = (f"gt{layer}", f"ut{layer}", f"dn{layer}")

    def body(*refs):
        dz_ref, hg_ref, hu_ref, xh_ref, rs_ref, gi_ref, wb_ref = refs[:7]
        pos = 7
        if qkv is not None:
            dq_ref, dkc_ref, dkp_ref, dkn_ref, xho_ref, rso_ref, go_ref, wkv_ref = refs[pos:pos + 8]
            pos += 8
        dzb_ref, act_ref, dhg_ref, dhu_ref, dzp_ref, st_ref = refs[pos:pos + 6]
        pos += 6
        if qkv is not None:
            dkv_ref, sto_ref, dbkv_ref = refs[pos:pos + 3]
            pos += 3
        gt_s, ut_s, dn_s = refs[pos:pos + 3]
        i = pl.program_id(0)

        @pl.when(i == 0)
        def _():
            for name, dst in zip(names, (gt_s, ut_s, dn_s)):
                _load_weight(wb_ref, lay, name, dst)
            st_ref[...] = jnp.zeros(st_ref.shape, F32)
            if qkv is not None:
                _load_weight(wb_ref, lay, "wq", refs[pos + 3])
                sto_ref[...] = jnp.zeros(sto_ref.shape, F32)
                dbkv_ref[...] = jnp.zeros(dbkv_ref.shape, F32)

        if qkv is None:
            dzv = dz_ref[...]
        else:
            nxt = jnp.where(i < nsteps - 1, dkn_ref[...], 0.0)
            shifted = jnp.concatenate([dkp_ref[pl.ds(BLOCK, tm - BLOCK), :], nxt], axis=0) if nbt > 1 else nxt
            dkv = dkc_ref[...] + shifted
            dkvb = dkv.astype(BF16)
            dkv_ref[...] = dkvb
            dbkv_ref[...] += jnp.sum(dkv, axis=0, keepdims=True)
            dxo = (ALPHA * dz_ref[...] + _dot_nt(dq_ref[...], refs[pos + 3][...])
                   + _dot_nt(dkvb, wkv_ref[...]))
            dzv, dgo, dbo = _ln_bwd(dxo, xho_ref[...], rso_ref[...], go_ref[...])
            sto_ref[0:1, :] += dgo
            sto_ref[1:2, :] += dbo
        dzb = dzv.astype(BF16)
        dzb_ref[...] = dzb
        dx = ALPHA * dzv
        def back(c):
            return _dot_nt(dzb, _wrows(dn_s, c * FFN_FC, FFN_FC))

        nc = f // FFN_FC
        ahead = [back(c) for c in range(min(FFN_AHEAD, nc))]
        for c in range(nc):
            rows = slice(c * FFN_FC, (c + 1) * FFN_FC)
            dact = ahead.pop(0)
            if c + FFN_AHEAD < nc:
                ahead.append(back(c + FFN_AHEAD))
            hg_v = hg_ref[:, rows].astype(F32)
            hu_v = hu_ref[:, rows].astype(F32)
            sg = _sigmoid(hg_v)
            silu = hg_v * sg
            act_ref[:, rows] = (silu * hu_v).astype(BF16)
            dhu = (dact * silu).astype(BF16)
            dhg = (dact * hu_v * (sg * (1.0 + hg_v * (1.0 - sg)))).astype(BF16)
            dhu_ref[:, rows] = dhu
            dhg_ref[:, rows] = dhg
            dx = (dx + _dot(dhg, _wrows(gt_s, c * FFN_FC, FFN_FC))
                  + _dot(dhu, _wrows(ut_s, c * FFN_FC, FFN_FC)))
        dzp, dg, db = _ln_bwd(dx, xh_ref[...], rs_ref[...], gi_ref[...])
        dzp_ref[...] = dzp
        st_ref[0:1, :] += dg
        st_ref[1:2, :] += db

    in_specs = [_tile(tm, d), _tile(tm, f), _tile(tm, f), _tile(tm, d), _tile(tm, 1), _row(d), ANY]
    args = [dz, hg, hu, xh_in, rs_in, g_in, wb]
    out_specs = [_tile(tm, d), _tile(tm, f), _tile(tm, f), _tile(tm, f), _tile(tm, d), _fixed(8, d)]
    out_shape = [jax.ShapeDtypeStruct((t, d), BF16), jax.ShapeDtypeStruct((t, f), BF16),
                 jax.ShapeDtypeStruct((t, f), BF16), jax.ShapeDtypeStruct((t, f), BF16),
                 jax.ShapeDtypeStruct((t, d), F32), jax.ShapeDtypeStruct((8, d), F32)]
    scratch = [_wscratch(lay, n) for n in names]
    if qkv is not None:
        dq, dkc, dkp, xh_out, rs_out, g_out, wkv = qkv
        last = t // BLOCK - 1
        in_specs += [_tile(tm, d), _tile(tm, 2 * BLOCK), _tile(tm, 2 * BLOCK),
                     pl.BlockSpec((BLOCK, 2 * BLOCK), lambda i: (jnp.minimum((i + 1) * nbt, last), 0)),
                     _tile(tm, d), _tile(tm, 1), _row(d), _fixed(d, 2 * BLOCK)]
        args += [dq, dkc, dkp, dkp, xh_out, rs_out, g_out, wkv]
        out_specs += [_tile(tm, 2 * BLOCK), _fixed(8, d), _row(2 * BLOCK)]
        out_shape += [jax.ShapeDtypeStruct((t, 2 * BLOCK), BF16), jax.ShapeDtypeStruct((8, d), F32),
                      jax.ShapeDtypeStruct((1, 2 * BLOCK), F32)]
        scratch.append(_wscratch(lay, "wq"))
    return _gridded_call(body, f"ffn_bwd{layer}", nsteps, in_specs, out_specs, out_shape, scratch, args, hosted)


def _alibi_slope(h, nq):
    return 2.0 ** (-ALIBI_MAX * (h + 1) / nq)


def _fill_alibi_bias(bias_s, nq, keys_on_rows=False):
    shape = (2 * BLOCK, BLOCK) if keys_on_rows else (BLOCK, 2 * BLOCK)
    qi = lax.broadcasted_iota(jnp.int32, shape, 1 if keys_on_rows else 0)
    kj = lax.broadcasted_iota(jnp.int32, shape, 0 if keys_on_rows else 1)
    delta = qi + BLOCK - kj
    valid = (delta >= 0) & (delta < BLOCK)
    dist = jnp.where(valid, delta.astype(F32), MASKED_DIST)
    dist_first = jnp.where(kj >= BLOCK, dist, MASKED_DIST)
    for h in range(nq):
        bias_s[0, h] = _alibi_slope(h, nq) * dist
        bias_s[1, h] = _alibi_slope(h, nq) * dist_first


def _padded_kv(kvb, kvh, transposed_v=False):
    lane = lax.broadcasted_iota(jnp.int32, (2 * BLOCK, BLOCK), 1)
    mine = (lane < HEAD_DIM) if kvh == 0 else (lane >= HEAD_DIM)
    out = []
    for sec, transposed in ((kvb[:, :BLOCK], False), (kvb[:, BLOCK:], transposed_v)):
        m = jnp.where(mine, sec.astype(F32), 0.0)
        sw = pltpu.roll(m, HEAD_DIM, 1)
        pair = (m, sw) if kvh == 0 else (sw, m)
        out.append(tuple((p.T if transposed else p).astype(BF16) for p in pair))
    return out


def _attn_fwd(xh_in, g_in, b_in, x_in_b, kvs, wb, lay, bq, sinks, bo, tm):
    t, d = xh_in.shape
    nq = d // HEAD_DIM
    pairs_per_kv = (d // BLOCK) // N_KV_HEADS
    nbt = tm // BLOCK
    scale = HEAD_DIM ** -0.5

    def body(xh_ref, gi_ref, bi_ref, xb_ref, kv_ref, kvp_ref, wb_ref, bq_ref, sk_ref, bo_ref,
             q_ref, o_ref, lse_ref, xho_ref, rso_ref, wq_s, wo_s, kvall, q_s, o_s, bias_s):
        i = pl.program_id(0)

        @pl.when(i == 0)
        def _():
            _load_weight(wb_ref, lay, "wq", wq_s)
            _load_weight(wb_ref, lay, "wo", wo_s)
            _fill_alibi_bias(bias_s, nq, keys_on_rows=True)

        qv = ((_dot(xb_ref[...], _wfull(wq_s)) + bq_ref[...]) * scale).astype(BF16)
        q_s[...] = qv
        q_ref[...] = qv
        kvall[pl.ds(0, BLOCK), :] = kvp_ref[...]
        kvall[pl.ds(BLOCK, tm), :] = kv_ref[...]
        head_row = lax.broadcasted_iota(jnp.int32, (BLOCK, BLOCK), 0)

        def score_phase(j):
            rows = slice(j * BLOCK, (j + 1) * BLOCK)
            kvb = kvall[j * BLOCK:(j + 2) * BLOCK, :]
            first = (i * nbt + j == 0).astype(jnp.int32)
            pads = [_padded_kv(kvb, kvh, transposed_v=True) for kvh in range(N_KV_HEADS)]
            scores = []
            for a in range(d // BLOCK):
                kpad = pads[a // pairs_per_kv][0]
                qp = q_s[rows, a * BLOCK:(a + 1) * BLOCK]
                for e in range(2):
                    scores.append(_dot_nt(kpad[e], qp) - bias_s[first, 2 * a + e])
            return rows, pads, scores

        def softmax_phase(state):
            rows, pads, scores = state
            probs, inv = [], []
            lse_t = jnp.zeros((BLOCK, BLOCK), F32)
            for h in range(nq):
                sink = sk_ref[:, h:h + 1]
                m = jnp.maximum(jnp.max(scores[h], axis=0, keepdims=True), sink)
                p = jnp.exp(scores[h] - m)
                l = jnp.sum(p, axis=0, keepdims=True) + jnp.exp(sink - m)
                lse_t = jnp.where(head_row == h, m + jnp.log(l), lse_t)
                probs.append(p.astype(BF16))
                inv.append(1.0 / l)
            lse_ref[rows, :] = lse_t.T
            return rows, pads, probs, inv

        def value_phase(state):
            rows, pads, probs, inv = state
            for a in range(d // BLOCK):
                vpad_t = pads[a // pairs_per_kv][1]
                opair_t = (_dot(vpad_t[0], probs[2 * a]) * inv[2 * a]
                           + _dot(vpad_t[1], probs[2 * a + 1]) * inv[2 * a + 1])
                o_s[rows, a * BLOCK:(a + 1) * BLOCK] = opair_t.T.astype(BF16)

        for state in [softmax_phase(s) for s in [score_phase(j) for j in range(nbt)]]:
            value_phase(state)
        ov = o_s[...]
        o_ref[...] = ov
        xin = xh_ref[...] * gi_ref[...] + bi_ref[...]
        xho, rso = _ln_fwd(ALPHA * xin + _dot(ov, _wfull(wo_s)) + bo_ref[...])
        xho_ref[...] = xho
        rso_ref[...] = rso

    return pl.pallas_call(
        body, name="attn_fwd", grid=(t // tm,),
        in_specs=[_tile(tm, d), _row(d), _row(d), _tile(tm, d), _tile(tm, 2 * BLOCK),
                  pl.BlockSpec((BLOCK, 2 * BLOCK), lambda i: (jnp.maximum(i * nbt - 1, 0), 0)),
                  ANY, _row(d), _row(nq), _row(d)],
        out_specs=[_tile(tm, d), _tile(tm, d), _tile(tm, BLOCK), _tile(tm, d), _tile(tm, 1)],
        out_shape=[jax.ShapeDtypeStruct((t, d), BF16), jax.ShapeDtypeStruct((t, d), BF16),
                   jax.ShapeDtypeStruct((t, BLOCK), F32), jax.ShapeDtypeStruct((t, d), F32),
                   jax.ShapeDtypeStruct((t, 1), F32)],
        scratch_shapes=[_wscratch(lay, "wq"), _wscratch(lay, "wo"),
                        pltpu.VMEM((BLOCK + tm, 2 * BLOCK), BF16), pltpu.VMEM((tm, d), BF16),
                        pltpu.VMEM((tm, d), BF16), pltpu.VMEM((2, nq, 2 * BLOCK, BLOCK), F32)],
        compiler_params=_params(),
    )(xh_in, g_in, b_in, x_in_b, kvs, kvs, wb, bq, sinks, bo)


def _attn_bwd(dz, q, o, lse, kvs, wb, lay, sinks, tm, hosted=None):
    t, d = dz.shape
    nq = d // HEAD_DIM
    pairs_per_kv = (d // BLOCK) // N_KV_HEADS
    nbt = tm // BLOCK
    scale = HEAD_DIM ** -0.5

    def body(dz_ref, q_ref, o_ref, lse_ref, kv_ref, kvp_ref, wb_ref, sk_ref,
             dzb_ref, dq_ref, dkc_ref, dkp_ref, st_ref, dsk_ref, wo_s, kvall, do_s, dq_s, bias_s):
        i = pl.program_id(0)

        @pl.when(i == 0)
        def _():
            _load_weight(wb_ref, lay, "wo", wo_s)
            _fill_alibi_bias(bias_s, nq, keys_on_rows=True)
            st_ref[...] = jnp.zeros(st_ref.shape, F32)
            dsk_ref[...] = jnp.zeros(dsk_ref.shape, F32)

        dzv = dz_ref[...]
        dzb = dzv.astype(BF16)
        dzb_ref[...] = dzb
        do_s[...] = _dot_nt(dzb, _wfull(wo_s))
        kvall[pl.ds(0, BLOCK), :] = kvp_ref[...]
        kvall[pl.ds(BLOCK, tm), :] = kv_ref[...]
        lane = lax.broadcasted_iota(jnp.int32, (BLOCK, BLOCK), 1)
        lane1 = lax.broadcasted_iota(jnp.int32, (1, BLOCK), 1)
        lane2 = lax.broadcasted_iota(jnp.int32, (2 * BLOCK, BLOCK), 1)
        halves = (lane < HEAD_DIM, lane >= HEAD_DIM)
        sel_row = lax.broadcasted_iota(jnp.int32, (8, BLOCK), 0)
        sel_lane = lax.broadcasted_iota(jnp.int32, (8, BLOCK), 1)
        head_sel = jnp.where((sel_row == 0) & (sel_lane < HEAD_DIM) | (sel_row == 1) & (sel_lane >= HEAD_DIM),
                             1.0, 0.0).astype(BF16)

        def score_phase(j):
            rows = slice(j * BLOCK, (j + 1) * BLOCK)
            kvb = kvall[j * BLOCK:(j + 2) * BLOCK, :]
            first = (i * nbt + j == 0).astype(jnp.int32)
            pads = [_padded_kv(kvb, kvh) for kvh in range(N_KV_HEADS)]
            scores, dps, dhs, qms, doms = [], [], [], [], []
            for a in range(d // BLOCK):
                kpad, vpad = pads[a // pairs_per_kv]
                cols = slice(a * BLOCK, (a + 1) * BLOCK)
                qp = q_ref[rows, cols]
                dop = do_s[rows, cols]
                dopb = dop.astype(BF16)
                prod = dop * o_ref[rows, cols].astype(F32)
                hi = prod.astype(BF16)
                lo = (prod - hi.astype(F32)).astype(BF16)
                dh_pair = _dot_nt(head_sel, hi) + _dot_nt(head_sel, lo)
                for e in range(2):
                    scores.append(_dot_nt(kpad[e], qp) - bias_s[first, 2 * a + e])
                    dps.append(_dot_nt(vpad[e], dopb))
                    dhs.append(dh_pair[e:e + 1, :])
                    qms.append(jnp.where(halves[e], qp, jnp.zeros_like(qp)))
                    doms.append(jnp.where(halves[e], dopb, jnp.zeros_like(dopb)))
            return rows, pads, scores, dps, dhs, qms, doms

        def softmax_phase(state):
            rows, pads, scores, dps, dhs, qms, doms = state
            dss, pbs = [], []
            dsk_t = jnp.zeros((1, BLOCK), F32)
            lse_t = lse_ref[rows, :].T
            for h in range(nq):
                lse_h = lse_t[h:h + 1, :]
                p = jnp.exp(scores[h] - lse_h)
                dss.append((p * (dps[h] - dhs[h])).astype(BF16))
                pbs.append(p.astype(BF16))
                dsink = -jnp.sum(jnp.exp(sk_ref[:, h:h + 1] - lse_h) * dhs[h], axis=1, keepdims=True)
                dsk_t = jnp.where(lane1 == h, dsink, dsk_t)
            dsk_ref[...] += dsk_t
            return rows, pads, dss, pbs, qms, doms

        def grad_phase(state):
            rows, pads, dss, pbs, qms, doms = state
            dsecs = []
            for kvh in range(N_KV_HEADS):
                kpad_t = [p.astype(F32).T.astype(BF16) for p in pads[kvh][0]]
                dk_acc = jnp.zeros((2 * BLOCK, BLOCK), F32)
                dv_acc = jnp.zeros((2 * BLOCK, BLOCK), F32)
                for a in range(kvh * pairs_per_kv, (kvh + 1) * pairs_per_kv):
                    dqp_t = _dot(kpad_t[0], dss[2 * a]) + _dot(kpad_t[1], dss[2 * a + 1])
                    dq_s[rows, a * BLOCK:(a + 1) * BLOCK] = dqp_t.T * scale
                    for e in range(2):
                        h = 2 * a + e
                        dk_acc = dk_acc + _dot(dss[h], qms[h])
                        dv_acc = dv_acc + _dot(pbs[h], doms[h])
                dsecs.append((dk_acc + pltpu.roll(dk_acc, HEAD_DIM, 1), dv_acc + pltpu.roll(dv_acc, HEAD_DIM, 1)))
            lo = lane2 < HEAD_DIM
            dkv = jnp.concatenate([jnp.where(lo, dsecs[0][0], dsecs[1][0]),
                                   jnp.where(lo, dsecs[0][1], dsecs[1][1])], axis=1)
            dkp_ref[rows, :] = dkv[:BLOCK]
            dkc_ref[rows, :] = dkv[BLOCK:]

        for state in [softmax_phase(s) for s in [score_phase(j) for j in range(nbt)]]:
            grad_phase(state)
        dqv = dq_s[...]
        dq_ref[...] = dqv.astype(BF16)
        st_ref[0:1, :] += jnp.sum(dqv, axis=0, keepdims=True)
        st_ref[1:2, :] += jnp.sum(dzv, axis=0, keepdims=True)

    return _gridded_call(
        body, "attn_bwd", t // tm,
        [_tile(tm, d), _tile(tm, d), _tile(tm, d), _tile(tm, BLOCK), _tile(tm, 2 * BLOCK),
         pl.BlockSpec((BLOCK, 2 * BLOCK), lambda i: (jnp.maximum(i * nbt - 1, 0), 0)),
         ANY, _row(nq)],
        [_tile(tm, d), _tile(tm, d), _tile(tm, 2 * BLOCK), _tile(tm, 2 * BLOCK),
         _fixed(8, d), _row(BLOCK)],
        [jax.ShapeDtypeStruct((t, d), BF16), jax.ShapeDtypeStruct((t, d), BF16),
         jax.ShapeDtypeStruct((t, 2 * BLOCK), F32), jax.ShapeDtypeStruct((t, 2 * BLOCK), F32),
         jax.ShapeDtypeStruct((8, d), F32), jax.ShapeDtypeStruct((1, BLOCK), F32)],
        [_wscratch(lay, "wo"), pltpu.VMEM((BLOCK + tm, 2 * BLOCK), BF16),
         pltpu.VMEM((tm, d), F32), pltpu.VMEM((tm, d), F32),
         pltpu.VMEM((2, nq, 2 * BLOCK, BLOCK), F32)],
        (dz, q, o, lse, kvs, kvs, wb, sinks), hosted)


def _tn_matmul(a, b, name, bm, tk):
    t, m = a.shape
    n = b.shape[1]
    ksteps = t // tk

    nc = max(n // 256, 1)
    cw = n // nc

    def body(a_ref, b_ref, o_ref, acc):
        k = pl.program_id(1)

        @pl.when(k == 0)
        def _():
            acc[...] = jnp.zeros(acc.shape, F32)

        at = a_ref[...].T
        for c in range(nc):
            cols = slice(c * cw, (c + 1) * cw)
            acc[:, cols] += _dot(at, b_ref[:, cols])

        @pl.when(k == ksteps - 1)
        def _():
            o_ref[...] = acc[...].astype(BF16)

    return pl.pallas_call(
        body, name=name, grid=(m // bm, ksteps),
        in_specs=[pl.BlockSpec((tk, bm), lambda j, k: (k, j)), pl.BlockSpec((tk, n), lambda j, k: (k, 0))],
        out_specs=pl.BlockSpec((bm, n), lambda j, k: (j, 0)),
        out_shape=jax.ShapeDtypeStruct((m, n), BF16),
        scratch_shapes=[pltpu.VMEM((bm, n), F32)],
        compiler_params=pltpu.CompilerParams(dimension_semantics=("arbitrary", "arbitrary"),
                                             vmem_limit_bytes=VMEM_LIMIT),
    )(a, b)


def _all_gather(arrays, name):
    n = len(arrays)

    def body(*refs):
        ins, outs = refs[:n], refs[n:2 * n]
        send_sems, recv_sems, local_sems = refs[2 * n:]
        x, y, c = _me()
        me, sibling = (x, y, c), (x, y, 1 - c)
        chips = [(1 - x, y), (x, 1 - y), (1 - x, 1 - y)]

        def slot(ref, dev):
            return ref.at[4 * dev[0] + 2 * dev[1] + dev[2]]

        def copy(a, k, block, to, src=None):
            return pltpu.make_async_remote_copy(
                src_ref=slot(outs[a], block) if src is None else src, dst_ref=slot(outs[a], block),
                send_sem=send_sems.at[a, k], recv_sem=recv_sems.at[a, k], device_id=to, device_id_type=MESH)

        mine = [pltpu.make_async_copy(ins[a], slot(outs[a], me), local_sems.at[a]) for a in range(n)]
        for cp in mine:
            cp.start()
        first = []
        for a in range(n):
            first.append(copy(a, 0, me, sibling, src=ins[a]))
            first += [copy(a, 1 + j, me, (*chip, c), src=ins[a]) for j, chip in enumerate(chips)]
        for cp in first:
            cp.start()
        passed = []
        for a in range(n):
            for j, chip in enumerate(chips):
                copy(a, 1 + j, (*chip, c), me).wait_recv()
                cp = copy(a, 4 + j, (*chip, c), sibling)
                cp.start()
                passed.append(cp)
        for a in range(n):
            copy(a, 0, sibling, me).wait_recv()
            for j, chip in enumerate(chips):
                copy(a, 4 + j, (*chip, 1 - c), me).wait_recv()
        for cp in first + passed:
            cp.wait_send()
        for cp in mine:
            cp.wait()

    return pl.pallas_call(
        body, name=name, in_specs=[ANY] * n, out_specs=[ANY] * n,
        out_shape=[jax.ShapeDtypeStruct((N_DEV,) + a.shape, a.dtype) for a in arrays],
        scratch_shapes=[pltpu.SemaphoreType.DMA((n, 7)), pltpu.SemaphoreType.DMA((n, 7)),
                        pltpu.SemaphoreType.DMA((n,))],
    )(*arrays)


def _exchange(arrays, name):
    n = len(arrays)
    blocked = [a.ndim == 3 for a in arrays]

    def body(*refs):
        ins, outs = refs[:n], refs[n:2 * n]
        send_sems, recv_sems, local_sems = refs[2 * n:]
        me = _index(_me())

        def src(k, dev):
            return ins[k].at[dev] if blocked[k] else ins[k]

        local = [pltpu.make_async_copy(src(k, me), outs[k].at[me], local_sems.at[k]) for k in range(n)]
        sends, arrivals = [], []
        for k in range(n):
            for mask in range(1, N_DEV):
                peer = _peer(mask)
                sends.append(pltpu.make_async_remote_copy(
                    src_ref=src(k, _index(peer)), dst_ref=outs[k].at[me], send_sem=send_sems.at[k, mask - 1],
                    recv_sem=recv_sems.at[k, mask - 1], device_id=peer, device_id_type=MESH))
                arrivals.append(pltpu.make_async_remote_copy(
                    src_ref=src(k, me), dst_ref=outs[k].at[_index(peer)], send_sem=send_sems.at[k, mask - 1],
                    recv_sem=recv_sems.at[k, mask - 1], device_id=_me(), device_id_type=MESH))
        for cp in local + sends:
            cp.start()
        for cp in arrivals:
            cp.wait_recv()
        for cp in sends:
            cp.wait_send()
        for cp in local:
            cp.wait()

    return pl.pallas_call(
        body, name=name, in_specs=[ANY] * n, out_specs=[ANY] * n,
        out_shape=[jax.ShapeDtypeStruct((N_DEV,) + a.shape[-2:], a.dtype) for a in arrays],
        scratch_shapes=[pltpu.SemaphoreType.DMA((n, 7)), pltpu.SemaphoreType.DMA((n, 7)),
                        pltpu.SemaphoreType.DMA((n,))],
    )(*arrays)


def _adamw_update(g, w_ref, m_ref, v_ref, go_ref, d_ref, mo_ref, vo_ref):
    mn = ADAM_B1 * m_ref[...] + (1.0 - ADAM_B1) * g
    vn = ADAM_B2 * v_ref[...] + (1.0 - ADAM_B2) * (g * g)
    m_hat = mn / (1.0 - ADAM_B1 ** ADAM_STEP)
    v_hat = vn / (1.0 - ADAM_B2 ** ADAM_STEP)
    go_ref[...] = g
    d_ref[...] = -ADAM_LR * (m_hat / (jnp.sqrt(v_hat) + ADAM_EPS) + ADAM_WD * w_ref[...])
    mo_ref[...] = mn
    vo_ref[...] = vn


def _sum_sources(g_refs, layer):
    total = None
    for l, g_ref in enumerate(g_refs):
        g = g_ref[0].astype(F32)
        for s in range(1, N_DEV):
            g = g + g_ref[s].astype(F32)
        total = g if total is None else jnp.where(layer == l, g, total)
    return total


def _layer_block(l_mine, nblocks):
    def index(l, j):
        return (0, jnp.where(l == l_mine, j, jnp.where(l < l_mine, 0, nblocks - 1)), 0)
    return index


def _adamw_sum(g8, w, m, v, name, tr):
    r, width = w.shape

    def body(g_ref, *refs):
        _adamw_update(_sum_sources([g_ref], 0), *refs)

    spec = pl.BlockSpec((tr, width), lambda i: (i, 0))
    return pl.pallas_call(
        body, name=name, grid=(r // tr,),
        in_specs=[pl.BlockSpec((N_DEV, tr, width), lambda i: (0, i, 0)), spec, spec, spec],
        out_specs=[spec] * 4, out_shape=[jax.ShapeDtypeStruct((r, width), F32)] * 4,
        compiler_params=_params(),
    )(g8, w, m, v)


def _adamw_rows(g8s, w, m, v, name):
    layers, n, width = w.shape
    tr = max(r for r in range(BF16_SUBLANES, ADAMW_MAX_ROWS + 1, BF16_SUBLANES) if n % r == 0)
    nb = n // tr

    def body(*refs):
        _adamw_update(_sum_sources(refs[:layers], pl.program_id(0)), *refs[layers:])

    spec = pl.BlockSpec((None, tr, width), lambda l, j: (l, j, 0))
    return pl.pallas_call(
        body, name=name, grid=(layers, nb),
        in_specs=[pl.BlockSpec((N_DEV, tr, width), _layer_block(l, nb)) for l in range(layers)] + [spec] * 3,
        out_specs=[spec] * 4, out_shape=[jax.ShapeDtypeStruct(w.shape, F32)] * 4,
        compiler_params=pltpu.CompilerParams(dimension_semantics=("arbitrary", "arbitrary"),
                                             vmem_limit_bytes=VMEM_LIMIT),
    )(*g8s, w, m, v)


def _adamw_cols(g8s, w, m, v, name):
    layers, k, n = w.shape
    cb = min(BLOCK, n)
    nb = pl.cdiv(n, cb)

    def body(*refs):
        _adamw_update(_sum_sources(refs[:layers], pl.program_id(0)).T, *refs[layers:])

    spec = pl.BlockSpec((None, k, cb), lambda l, j: (l, 0, j))
    return pl.pallas_call(
        body, name=name, grid=(layers, nb),
        in_specs=[pl.BlockSpec((N_DEV, cb, k), _layer_block(l, nb)) for l in range(layers)] + [spec] * 3,
        out_specs=[spec] * 4, out_shape=[jax.ShapeDtypeStruct(w.shape, F32)] * 4,
        compiler_params=pltpu.CompilerParams(dimension_semantics=("arbitrary", "arbitrary"),
                                             vmem_limit_bytes=VMEM_LIMIT),
    )(*g8s, w, m, v)


def _local_step(x, target, wba, shard_b, lay, sm, tm, tk):
    t, d = x.shape
    f = lay.f
    w_dw32 = jnp.concatenate([sm["w_dw"], jnp.zeros((HALO - CONV_WIDTH, d), F32)], axis=0)
    lmg, lmb, lfg, lfb = sm["ln_mix_g"], sm["ln_mix_b"], sm["ln_ffn_g"], sm["ln_ffn_b"]
    bkv = jnp.concatenate([sm["b_k"], sm["b_v"]], axis=1)
    bm_f = f // 2 if (f // 2) % 128 == 0 else f
    tm_light = 2 * tm

    received = {}

    def exchange(grads):
        return _HostedExchange([g.reshape(N_DEV, lay.n[n], d) for n, g in grads.items()])

    def keep(grads, arrived):
        received.update(zip(grads, arrived))

    xb0, ag, xhc, rsc, xh1, rs1, wbb = _conv_fwd(x, wba, lay, w_dw32, sm["b_pw1"], sm["b_dw"], sm["cg"],
                                                 sm["cb"], sm["b_pw2"], lmg[0:1], lmb[0:1], tm_light,
                                                 hosted=_HostedGather(shard_b))
    wkv = wbb[:, lay.goff["wkv"]:lay.goff["wkv"] + lay.n["wkv"], :].reshape(d, 2 * BLOCK)
    x1b, hg0, hu0, xh2, rs2, x2b, kvs = _ffn_fwd(xh1, lmg[0:1], lmb[0:1], wbb, lay, 0, tm_light,
                                                kv=(lfg[0:1], lfb[0:1], wkv, bkv))
    q, o, lse, xh3, rs3 = _attn_fwd(xh2, lfg[0:1], lfb[0:1], x2b, kvs, wbb, lay, sm["b_q"], sm["sinks"],
                                    sm["b_o"], tm)
    x3b, hg1, hu1, dz4, st4, loss = _ffn_fwd(xh3, lmg[1:2], lmb[1:2], wbb, lay, 1, tm,
                                             loss=(lfg[1:2], lfb[1:2], target))

    dz4b, act1, dhg1, dhu1, dz3, st3 = _ffn_bwd(dz4, hg1, hu1, xh3, rs3, lmg[1:2], wbb, lay, 1, tm)
    g1 = {"gt1": _tn_matmul(dhg1, x3b, "dw_gate1", bm_f, tk), "ut1": _tn_matmul(dhu1, x3b, "dw_up1", bm_f, tk),
          "dn1": _tn_matmul(act1, dz4b, "dw_down1", bm_f, tk)}
    dz3b, dq, dkc, dkp, stq, dsinks, *arrived = _attn_bwd(dz3, q, o, lse, kvs, wbb, lay, sm["sinks"], tm,
                                                          hosted=exchange(g1))
    keep(g1, arrived)
    g2 = {"wq": _tn_matmul(x2b, dq, "dw_q", d, tk), "wo": _tn_matmul(o, dz3b, "dw_o", d, tk)}
    dz2b, act0, dhg0, dhu0, dz1, st1, dkv, st2, dbkv, *arrived = _ffn_bwd(
        dz3, hg0, hu0, xh1, rs1, lmg[0:1], wbb, lay, 0, tm, hosted=exchange(g2),
        qkv=(dq, dkc, dkp, xh2, rs2, lfg[0:1], wkv))
    keep(g2, arrived)
    dz1b, s_act, dcv, stc = _conv_bwd1(dz1, xhc, rsc, wba, lay, sm["cg"], sm["cb"], tm_light)
    g3 = {"gt0": _tn_matmul(dhg0, x1b, "dw_gate0", bm_f, tk), "ut0": _tn_matmul(dhu0, x1b, "dw_up0", bm_f, tk),
          "dn0": _tn_matmul(act0, dz2b, "dw_down0", bm_f, tk), "pw2": _tn_matmul(s_act, dz1b, "dw_pw2", d, tk),
          "wkv": _tn_matmul(x2b, dkv, "dw_kv", d, tk)}
    grad_x, dh1, dwdw, db1, *arrived = _conv_bwd2(dz1, dcv, ag, wba, lay, w_dw32, tm_light, hosted=exchange(g3))
    keep(g3, arrived)
    g_pw1t = _tn_matmul(dh1, xb0, "dw_pw1", d, tk)
    small = {
        "w_dw": dwdw[:CONV_WIDTH], "b_pw1": db1, "b_dw": stc[2:3], "cg": stc[0:1], "cb": stc[1:2],
        "b_pw2": stc[3:4], "b_k": dbkv[:, :BLOCK], "b_v": dbkv[:, BLOCK:], "b_q": stq[0:1],
        "sinks": dsinks[:, :d // HEAD_DIM],
        "b_o": stq[1:2],
        "ln_mix_g": jnp.concatenate([st1[0:1], st3[0:1]], axis=0),
        "ln_mix_b": jnp.concatenate([st1[1:2], st3[1:2]], axis=0),
        "ln_ffn_g": jnp.concatenate([st2[0:1], st4[0:1]], axis=0),
        "ln_ffn_b": jnp.concatenate([st2[1:2], st4[1:2]], axis=0),
    }
    return loss[0, 0], grad_x, received, g_pw1t, small


SP_ROWS = 40
SP_BDW, SP_CG, SP_CB, SP_BPW2, SP_BPW1 = 32, 33, 34, 35, 36
RP_NAMES = ("ln_mix_g", "ln_mix_b", "ln_ffn_g", "ln_ffn_b", "b_q", "b_o", "b_k", "b_v", "sinks")


def _row_forms(d, pw1, pw2, wq, wo, gate, up, down, wk, wv):
    rf = {"pw1t": pw1[0].T, "pw2": pw2[0], "wq": wq[0], "wo": wo[0],
          "wkv": jnp.concatenate([wk, wv], axis=1).reshape(-1, d)}
    for l in range(DEPTH):
        rf.update({f"gt{l}": gate[l].T, f"ut{l}": up[l].T, f"dn{l}": down[l]})
    return rf


def _pack_rows(rf, names):
    return jnp.concatenate([rf[n] for n in names], axis=0)


def _pack_small(w_dw, b_dw, cg, cb, b_pw2, b_pw1):
    cw = b_dw.shape[1]
    z = jnp.zeros((1, cw), F32)
    return jnp.concatenate([w_dw[0], z, b_dw, cg, cb, b_pw2, b_pw1.reshape(2, cw), z, z], axis=0)


def _unpack_small(p):
    cw = p.shape[1]
    return dict(w_dw=p[None, :CONV_WIDTH], b_dw=p[SP_BDW:SP_BDW + 1], cg=p[SP_CG:SP_CG + 1],
                cb=p[SP_CB:SP_CB + 1], b_pw2=p[SP_BPW2:SP_BPW2 + 1],
                b_pw1=p[SP_BPW1:SP_BPW1 + 2].reshape(1, 2 * cw))


def _small_full(g):
    d = N_DEV * g.shape[2]

    def wide(r0, n=1):
        return jnp.transpose(g[:, r0:r0 + n], (1, 0, 2)).reshape(n, d)

    return dict(w_dw=wide(0, CONV_WIDTH), b_dw=wide(SP_BDW), cg=wide(SP_CG), cb=wide(SP_CB),
                b_pw2=wide(SP_BPW2), b_pw1=g[:, SP_BPW1:SP_BPW1 + 2].reshape(1, 2 * d))


def _small_grad_blocks(sg):
    cw = sg["b_dw"].shape[1] // N_DEV

    def narrow(a):
        return jnp.transpose(a.reshape(a.shape[0], N_DEV, cw), (1, 0, 2))

    z = jnp.zeros((N_DEV, 1, cw), F32)
    return jnp.concatenate([narrow(sg["w_dw"]), z, narrow(sg["b_dw"]), narrow(sg["cg"]), narrow(sg["cb"]),
                            narrow(sg["b_pw2"]), sg["b_pw1"].reshape(N_DEV, 2, cw), z, z], axis=1)


def _pack_rep(vals, rider=0.0):
    parts = []
    for name in RP_NAMES:
        a = vals[name].reshape(-1)
        pad = -a.shape[0] % 128
        parts.append(jnp.concatenate([a, jnp.zeros((pad,), F32)]).reshape(-1, 128))
    parts.append(jnp.full((1, 128), rider, F32))
    rows = sum(p.shape[0] for p in parts)
    parts.append(jnp.zeros((-rows % 8, 128), F32))
    return jnp.concatenate(parts, axis=0)


def _rider_row(shapes):
    return sum(-(-_size(shapes[name]) // 128) for name in RP_NAMES)


def _size(shape):
    n = 1
    for s in shape:
        n *= s
    return n


def _unpack_rep(p, shapes):
    out, r = {}, 0
    for name in RP_NAMES:
        n = _size(shapes[name])
        rows = -(-n // 128)
        out[name] = p[r:r + rows].reshape(-1)[:n].reshape(shapes[name])
        r += rows
    return out


def kernel(x, conv_w_pw1, conv_b_pw1, conv_w_dw, conv_b_dw, conv_ln_g, conv_ln_b, conv_w_pw2, conv_b_pw2, kv_w_k, kv_b_k, kv_w_v, kv_b_v, attn_w_q, attn_b_q, attn_sinks, attn_w_o, attn_b_o, ffn_w_gate, ffn_w_up, ffn_w_down, ln_mix_g, ln_mix_b, ln_ffn_g, ln_ffn_b, loss_target, m_conv_w_pw1, m_conv_b_pw1, m_conv_w_dw, m_conv_b_dw, m_conv_ln_g, m_conv_ln_b, m_conv_w_pw2, m_conv_b_pw2, m_kv_w_k, m_kv_b_k, m_kv_w_v, m_kv_b_v, m_attn_w_q, m_attn_b_q, m_attn_sinks, m_attn_w_o, m_attn_b_o, m_ffn_w_gate, m_ffn_w_up, m_ffn_w_down, m_ln_mix_g, m_ln_mix_b, m_ln_ffn_g, m_ln_ffn_b, v_conv_w_pw1, v_conv_b_pw1, v_conv_w_dw, v_conv_b_dw, v_conv_ln_g, v_conv_ln_b, v_conv_w_pw2, v_conv_b_pw2, v_kv_w_k, v_kv_b_k, v_kv_w_v, v_kv_b_v, v_attn_w_q, v_attn_b_q, v_attn_sinks, v_attn_w_o, v_attn_b_o, v_ffn_w_gate, v_ffn_w_up, v_ffn_w_down, v_ln_mix_g, v_ln_mix_b, v_ln_ffn_g, v_ln_ffn_b):
    t, d = x.shape[1], x.shape[2]
    f = ffn_w_gate.shape[2] * N_DEV
    lay = _Layout(d, f)
    tm, tk = 256, min(2048, t)

    rep_shapes = dict(ln_mix_g=ln_mix_g.shape, ln_mix_b=ln_mix_b.shape, ln_ffn_g=ln_ffn_g.shape,
                      ln_ffn_b=ln_ffn_b.shape, b_q=attn_b_q.shape, b_o=attn_b_o.shape, b_k=kv_b_k.shape,
                      b_v=kv_b_v.shape, sinks=attn_sinks.shape)

    def rep_pack(lmg, lmb, lfg, lfb, bq, bo, bk, bv, sk):
        return _pack_rep(dict(ln_mix_g=lmg, ln_mix_b=lmb, ln_ffn_g=lfg, ln_ffn_b=lfb, b_q=bq, b_o=bo,
                              b_k=bk, b_v=bv, sinks=sk))

    w_rf = _row_forms(d, conv_w_pw1, conv_w_pw2, attn_w_q, attn_w_o, ffn_w_gate, ffn_w_up, ffn_w_down, kv_w_k, kv_w_v)
    w_small = _pack_small(conv_w_dw, conv_b_dw, conv_ln_g, conv_ln_b, conv_b_pw2, conv_b_pw1)
    m_small = _pack_small(m_conv_w_dw, m_conv_b_dw, m_conv_ln_g, m_conv_ln_b, m_conv_b_pw2, m_conv_b_pw1)
    v_small = _pack_small(v_conv_w_dw, v_conv_b_dw, v_conv_ln_g, v_conv_ln_b, v_conv_b_pw2, v_conv_b_pw1)
    w_rep = rep_pack(ln_mix_g, ln_mix_b, ln_ffn_g, ln_ffn_b, attn_b_q, attn_b_o, kv_b_k, kv_b_v, attn_sinks)
    m_rep = rep_pack(m_ln_mix_g, m_ln_mix_b, m_ln_ffn_g, m_ln_ffn_b, m_attn_b_q, m_attn_b_o, m_kv_b_k, m_kv_b_v, m_attn_sinks)
    v_rep = rep_pack(v_ln_mix_g, v_ln_mix_b, v_ln_ffn_g, v_ln_ffn_b, v_attn_b_q, v_attn_b_o, v_kv_b_k, v_kv_b_v, v_attn_sinks)

    wba, smg = _all_gather([_pack_rows(w_rf, lay.GATHER["a"]).astype(BF16), w_small], "gather_conv_weights")
    shard_b = _pack_rows(w_rf, lay.GATHER["b"]).astype(BF16)
    sm = _small_full(smg)
    sm.update(ln_mix_g=ln_mix_g, ln_mix_b=ln_mix_b, ln_ffn_g=ln_ffn_g, ln_ffn_b=ln_ffn_b, b_q=attn_b_q,
              b_o=attn_b_o, sinks=attn_sinks, b_k=kv_b_k.reshape(1, -1), b_v=kv_b_v.reshape(1, -1))

    loss_part, grad_x, received, g_pw1t, gsmall = _local_step(x[0], loss_target[0], wba, shard_b, lay, sm, tm, tk)

    received["pw1t"], g8_small, g8_rep = _exchange(
        [g_pw1t.reshape(N_DEV, lay.n["pw1t"], d), _small_grad_blocks(gsmall), _pack_rep(gsmall, loss_part)],
        "exchange_last_grads")

    def kv_rows(wk, wv):
        return jnp.concatenate([wk, wv], axis=1).reshape(1, -1, d)

    def kv_split(a):
        a = a.reshape(d // N_DEV, 2 * BLOCK)
        return a[:, :BLOCK], a[:, BLOCK:]

    big = dict(
        pw1=_adamw_cols([received["pw1t"]], conv_w_pw1, m_conv_w_pw1, v_conv_w_pw1, "adamw_pw1"),
        gate=_adamw_cols([received["gt0"], received["gt1"]], ffn_w_gate, m_ffn_w_gate, v_ffn_w_gate, "adamw_gate"),
        up=_adamw_cols([received["ut0"], received["ut1"]], ffn_w_up, m_ffn_w_up, v_ffn_w_up, "adamw_up"),
        down=_adamw_rows([received["dn0"], received["dn1"]], ffn_w_down, m_ffn_w_down, v_ffn_w_down, "adamw_down"),
        pw2=_adamw_rows([received["pw2"]], conv_w_pw2, m_conv_w_pw2, v_conv_w_pw2, "adamw_pw2"),
        wq=_adamw_rows([received["wq"]], attn_w_q, m_attn_w_q, v_attn_w_q, "adamw_q"),
        wo=_adamw_rows([received["wo"]], attn_w_o, m_attn_w_o, v_attn_w_o, "adamw_o"),
        wkv=[kv_split(a) for a in _adamw_rows([received["wkv"]], kv_rows(kv_w_k, kv_w_v), kv_rows(m_kv_w_k, m_kv_w_v),
                                              kv_rows(v_kv_w_k, v_kv_w_v), "adamw_kv")])
    big_out = [dict(pw1=big["pw1"][i], pw2=big["pw2"][i], wq=big["wq"][i], wo=big["wo"][i], gate=big["gate"][i],
                    up=big["up"][i], down=big["down"][i], wk=big["wkv"][i][0], wv=big["wkv"][i][1])
               for i in range(4)]
    small_out = [_unpack_small(a) for a in _adamw_sum(g8_small, w_small, m_small, v_small, "adamw_small", SP_ROWS)]
    rep_res = _adamw_sum(g8_rep, w_rep, m_rep, v_rep, "adamw_rep", w_rep.shape[0])
    rep_out = [_unpack_rep(a, rep_shapes) for a in rep_res]
    loss = rep_res[0][_rider_row(rep_shapes), 0]

    outs = [loss, grad_x[None]]
    for b, s, r in zip(big_out, small_out, rep_out):
        outs += [b["pw1"], s["b_pw1"], s["w_dw"], s["b_dw"], s["cg"], s["cb"], b["pw2"], s["b_pw2"],
                 b["wk"], r["b_k"], b["wv"], r["b_v"], b["wq"], r["b_q"], r["sinks"], b["wo"], r["b_o"],
                 b["gate"], b["up"], b["down"], r["ln_mix_g"], r["ln_mix_b"], r["ln_ffn_g"], r["ln_ffn_b"]]
    return tuple(outs)
```

```python
import jax
import jax.numpy as jnp
from jax import lax
from jax.experimental import pallas as pl
from jax.experimental.pallas import tpu as pltpu

F32 = jnp.float32
BF16 = jnp.bfloat16

N_DEV = 8
HEAD_DIM = 64
N_KV_HEADS = 2
BLOCK = 128
CONV_WIDTH = 31
HALO = 32
ALIBI_MAX = 8.0
DEPTH = 2
ALPHA = (2.0 * DEPTH) ** 0.25
LN_EPS = 1e-5
MASKED_DIST = 1e32
ADAM_LR = 0.001
ADAM_B1 = 0.9
ADAM_B2 = 0.999
ADAM_EPS = 1e-08
ADAM_WD = 0.01
ADAM_STEP = 10
VMEM_LIMIT = 56 * 1024 * 1024
BF16_SUBLANES = 16
ADAMW_MAX_ROWS = 176
MESH = pl.DeviceIdType.MESH


def _dot(a, b):
    return jnp.dot(a, b, preferred_element_type=F32)


def _dot_nt(a, b):
    return lax.dot_general(a, b, (((1,), (1,)), ((), ())), preferred_element_type=F32)


def _dot_tn(a, b):
    return lax.dot_general(a, b, (((0,), (0,)), ((), ())), preferred_element_type=F32)


def _sigmoid(v):
    return 1.0 / (1.0 + jnp.exp(-v))


def _ln_fwd(z):
    mu = jnp.mean(z, axis=-1, keepdims=True)
    zc = z - mu
    var = jnp.mean(zc * zc, axis=-1, keepdims=True)
    rstd = lax.rsqrt(var + LN_EPS)
    return zc * rstd, rstd


def _ln_bwd(dout, xh, rstd, g):
    dxh = dout * g
    m1 = jnp.mean(dxh, axis=-1, keepdims=True)
    m2 = jnp.mean(dxh * xh, axis=-1, keepdims=True)
    dz = rstd * (dxh - m1 - xh * m2)
    return dz, jnp.sum(dout * xh, axis=0, keepdims=True), jnp.sum(dout, axis=0, keepdims=True)


def _params(vmem=VMEM_LIMIT):
    return pltpu.CompilerParams(dimension_semantics=("arbitrary",), vmem_limit_bytes=vmem)


def _row(d):
    return pl.BlockSpec((1, d), lambda i: (0, 0))


def _tile(tm, d):
    return pl.BlockSpec((tm, d), lambda i: (i, 0))


def _fixed(r, d):
    return pl.BlockSpec((r, d), lambda i: (0, 0))


def _tile_cur(tm, d, nsteps):
    return pl.BlockSpec((tm, d), lambda i: (jnp.minimum(i, nsteps - 1), 0))


def _tile_prev(tm, d):
    return pl.BlockSpec((tm, d), lambda i: (jnp.maximum(i - 1, 0), 0))


ANY = pl.BlockSpec(memory_space=pl.ANY)


class _Layout:
    GATHER = {"a": ("pw1t", "pw2"),
              "b": ("wq", "wo", "gt0", "ut0", "dn0", "gt1", "ut1", "dn1", "wkv")}

    def __init__(self, d, f):
        self.d, self.f = d, f
        self.n = {"pw1t": 2 * d // N_DEV, "pw2": d // N_DEV, "wq": d // N_DEV, "wo": d // N_DEV,
                  "wkv": (d // N_DEV) * 2 * BLOCK // d}
        for l in range(DEPTH):
            self.n.update({f"gt{l}": f // N_DEV, f"ut{l}": f // N_DEV, f"dn{l}": f // N_DEV})
        self.goff = {}
        for names in self.GATHER.values():
            r = 0
            for name in names:
                self.goff[name] = r
                r += self.n[name]


def _load_weight(wb_ref, lay, name, dst):
    n = lay.n[name]
    for p in range(N_DEV):
        pltpu.sync_copy(wb_ref.at[p, pl.ds(lay.goff[name], n), :], dst.at[pl.ds(p * n, n), :])


def _wscratch(lay, name):
    return pltpu.VMEM((N_DEV * lay.n[name], lay.d), BF16)


def _wfull(ref):
    return ref[...]


def _wrows(ref, r0, nrows):
    return ref[r0:r0 + nrows, :]


def _me():
    return lax.axis_index("x"), lax.axis_index("y"), lax.axis_index("c")


def _peer(mask):
    x, y, c = _me()
    return (1 - x if mask & 4 else x, 1 - y if mask & 2 else y, 1 - c if mask & 1 else c)


def _index(dev):
    return 4 * dev[0] + 2 * dev[1] + dev[2]


class _HostedGather:
    def __init__(self, array):
        self.arrays = [array]
        self.out_shapes = [jax.ShapeDtypeStruct((N_DEV,) + array.shape, array.dtype)]

    def scratch(self):
        return [pltpu.SemaphoreType.DMA((7,)), pltpu.SemaphoreType.DMA((7,)), pltpu.SemaphoreType.DMA(())]

    def _copies(self, ins, outs, send_sems, recv_sems, local_sem):
        out = outs[0]
        x, y, c = _me()
        me, sibling = (x, y, c), (x, y, 1 - c)
        chips = [(1 - x, y), (x, 1 - y), (1 - x, 1 - y)]

        def copy(k, block, to, src=None):
            rows = out.at[_index(block)]
            return pltpu.make_async_remote_copy(
                src_ref=rows if src is None else src, dst_ref=rows, send_sem=send_sems.at[k],
                recv_sem=recv_sems.at[k], device_id=to, device_id_type=MESH)

        return dict(
            mine=lambda: pltpu.make_async_copy(ins[0], out.at[_index(me)], local_sem),
            first=lambda: [copy(0, me, sibling, src=ins[0])] + [copy(1 + j, me, (*chip, c), src=ins[0])
                                                                for j, chip in enumerate(chips)],
            over_ici=lambda: [copy(1 + j, (*chip, c), me) for j, chip in enumerate(chips)],
            passed=lambda: [copy(4 + j, (*chip, c), sibling) for j, chip in enumerate(chips)],
            from_sibling=lambda: [copy(0, sibling, me)] + [copy(4 + j, (*chip, 1 - c), me)
                                                           for j, chip in enumerate(chips)])

    def start(self, *refs):
        cp = self._copies(*refs)
        cp["mine"]().start()
        for c in cp["first"]():
            c.start()

    def middle(self, *refs):
        cp = self._copies(*refs)
        for arrived, onward in zip(cp["over_ici"](), cp["passed"]()):
            arrived.wait_recv()
            onward.start()

    def finish(self, *refs):
        cp = self._copies(*refs)
        for c in cp["from_sibling"]():
            c.wait_recv()
        for c in cp["first"]() + cp["passed"]():
            c.wait_send()
        cp["mine"]().wait()


class _HostedExchange:
    def __init__(self, arrays):
        self.arrays = list(arrays)
        self.out_shapes = [jax.ShapeDtypeStruct(a.shape, a.dtype) for a in self.arrays]

    def scratch(self):
        n = len(self.arrays)
        return [pltpu.SemaphoreType.DMA((n, 7)), pltpu.SemaphoreType.DMA((n, 7)), pltpu.SemaphoreType.DMA((n,))]

    def _copies(self, ins, outs, send_sems, recv_sems, local_sems):
        me = _index(_me())

        def dst(k, src_dev):
            return outs[k].at[src_dev]

        pairs = [(k, mask) for k in range(len(self.arrays)) for mask in range(1, N_DEV)]

        def local():
            return [pltpu.make_async_copy(ins[k].at[me], dst(k, me), local_sems.at[k])
                    for k in range(len(self.arrays))]

        def sends():
            return [pltpu.make_async_remote_copy(
                src_ref=ins[k].at[_index(_peer(mask))], dst_ref=dst(k, me), send_sem=send_sems.at[k, mask - 1],
                recv_sem=recv_sems.at[k, mask - 1], device_id=_peer(mask), device_id_type=MESH)
                for k, mask in pairs]

        def arrivals():
            return [pltpu.make_async_remote_copy(
                src_ref=ins[k].at[me], dst_ref=dst(k, _index(_peer(mask))), send_sem=send_sems.at[k, mask - 1],
                recv_sem=recv_sems.at[k, mask - 1], device_id=_me(), device_id_type=MESH)
                for k, mask in pairs]

        return local, sends, arrivals

    def start(self, *refs):
        local, sends, _ = self._copies(*refs)
        for c in local() + sends():
            c.start()

    def middle(self, *refs):
        pass

    def finish(self, *refs):
        local, sends, arrivals = self._copies(*refs)
        for c in arrivals():
            c.wait_recv()
        for c in sends():
            c.wait_send()
        for c in local():
            c.wait()


HAND_ON_AT = 6


def _gridded_call(body, name, nsteps, in_specs, out_specs, out_shape, scratch, args, hosted=None):
    if hosted is None:
        return pl.pallas_call(body, name=name, grid=(nsteps,), in_specs=in_specs, out_specs=out_specs,
                              out_shape=out_shape, scratch_shapes=scratch, compiler_params=_params())(*args)
    n_in, n_out, n_scr, h_in = len(in_specs), len(out_specs), len(scratch), len(hosted.arrays)
    h_out = len(hosted.out_shapes)

    def with_hosted(*refs):
        a = n_in + h_in
        b = a + n_out
        e = b + h_out + n_scr
        comm = (refs[n_in:a], refs[b:b + h_out], refs[e], refs[e + 1], refs[e + 2])
        i = pl.program_id(0)

        @pl.when(i == 0)
        def _():
            hosted.start(*comm)

        body(*refs[:n_in], *refs[a:b], *refs[b + h_out:e])

        @pl.when(i == HAND_ON_AT * nsteps // 8)
        def _():
            hosted.middle(*comm)

        @pl.when(i == nsteps - 1)
        def _():
            hosted.finish(*comm)

    return pl.pallas_call(
        with_hosted, name=name, grid=(nsteps,), in_specs=list(in_specs) + [ANY] * h_in,
        out_specs=list(out_specs) + [ANY] * h_out, out_shape=list(out_shape) + hosted.out_shapes,
        scratch_shapes=list(scratch) + hosted.scratch(), compiler_params=_params(),
    )(*args, *hosted.arrays)


CONV_RB = 64
CONV_LC = 128
CONV_MC = 256


def _shifted(win, r):
    return win if r == 0 else pltpu.roll(win, win.shape[0] - r, 0)


def _conv_fwd(x, wb, lay, w_dw, b_pw1, b_dw, cg, cb, b_pw2, lg, lb, tm, hosted=None):
    t, d = x.shape
    nsteps = t // tm

    def body(x_ref, xh_ref, wb_ref, wdw_ref, b1_ref, bdw_ref, cg_ref, cb_ref, b2_ref, lg_ref, lb_ref,
             xb_ref, ag_ref, xhc_ref, rsc_ref, xh1_ref, rs1_ref, w1_s, w2_s, ubuf, cv_s):
        i = pl.program_id(0)

        @pl.when(i == 0)
        def _():
            _load_weight(wb_ref, lay, "pw1t", w1_s)
            _load_weight(wb_ref, lay, "pw2", w2_s)

        xv = x_ref[...]
        xb = xv.astype(BF16)
        xb_ref[...] = xb
        xcat = jnp.concatenate([xh_ref[...].astype(BF16), xb], axis=0)
        for mc in range(d // CONV_MC):
            c0 = mc * CONV_MC
            acols, gcols = slice(c0, c0 + CONV_MC), slice(d + c0, d + c0 + CONV_MC)
            ha = _dot_nt(xcat, _wrows(w1_s, c0, CONV_MC)) + b1_ref[:, acols]
            hg = _dot_nt(xcat, _wrows(w1_s, d + c0, CONV_MC)) + b1_ref[:, gcols]
            ag_ref[:, acols] = ha[HALO:].astype(BF16)
            ag_ref[:, gcols] = hg[HALO:].astype(BF16)
            u = ha * _sigmoid(hg)
            u = jnp.concatenate([jnp.where(i > 0, u[:HALO], 0.0), u[HALO:], jnp.zeros((8, CONV_MC), F32)], axis=0)
            for r in range(8):
                ubuf[r, :, acols] = _shifted(u, r)
            for rb in range(tm // CONV_RB):
                t0 = rb * CONV_RB
                for lc in range(CONV_MC // CONV_LC):
                    lanes = slice(c0 + lc * CONV_LC, c0 + (lc + 1) * CONV_LC)
                    acc = jnp.zeros((CONV_RB, CONV_LC), F32)
                    for k in range(CONV_WIDTH):
                        s = HALO - (CONV_WIDTH - 1) + k
                        q = t0 + 8 * (s // 8)
                        acc = acc + ubuf[s % 8, q:q + CONV_RB, lanes] * wdw_ref[k:k + 1, lanes]
                    cv_s[t0:t0 + CONV_RB, lanes] = acc
        cv = cv_s[...] + bdw_ref[...]
        xhc, rsc = _ln_fwd(cv)
        xhc_ref[...] = xhc
        rsc_ref[...] = rsc
        n = xhc * cg_ref[...] + cb_ref[...]
        s_act = n * _sigmoid(n)
        m = _dot(s_act.astype(BF16), _wfull(w2_s)) + b2_ref[...]
        xh1, rs1 = _ln_fwd(ALPHA * xv + m)
        xh1_ref[...] = xh1
        rs1_ref[...] = rs1

    hb = tm // HALO
    return _gridded_call(
        body, "conv_fwd", nsteps,
        [_tile(tm, d), pl.BlockSpec((HALO, d), lambda i: (jnp.maximum(i * hb - 1, 0), 0)), ANY,
         _fixed(HALO, d), _row(2 * d), _row(d), _row(d), _row(d), _row(d), _row(d), _row(d)],
        [_tile(tm, d), _tile(tm, 2 * d), _tile(tm, d), _tile(tm, 1), _tile(tm, d), _tile(tm, 1)],
        [jax.ShapeDtypeStruct((t, d), BF16), jax.ShapeDtypeStruct((t, 2 * d), BF16),
         jax.ShapeDtypeStruct((t, d), F32), jax.ShapeDtypeStruct((t, 1), F32),
         jax.ShapeDtypeStruct((t, d), F32), jax.ShapeDtypeStruct((t, 1), F32)],
        [_wscratch(lay, "pw1t"), _wscratch(lay, "pw2"),
         pltpu.VMEM((8, HALO + tm + 8, d), F32), pltpu.VMEM((tm, d), F32)],
        (x, x, wb, w_dw, b_pw1, b_dw, cg, cb, b_pw2, lg, lb), hosted)


def _conv_bwd1(dz1, xhc, rsc, wb, lay, cg, cb, tm):
    t, d = dz1.shape

    def body(dz_ref, xhc_ref, rsc_ref, wb_ref, cg_ref, cb_ref, dzb_ref, s_ref, dcv_ref, st_ref, w2_s):
        i = pl.program_id(0)

        @pl.when(i == 0)
        def _():
            _load_weight(wb_ref, lay, "pw2", w2_s)
            st_ref[...] = jnp.zeros(st_ref.shape, F32)

        dz = dz_ref[...]
        dzb = dz.astype(BF16)
        dzb_ref[...] = dzb
        xhc_v = xhc_ref[...]
        n = xhc_v * cg_ref[...] + cb_ref[...]
        sg = _sigmoid(n)
        s_ref[...] = (n * sg).astype(BF16)
        ds = _dot_nt(dzb, _wfull(w2_s))
        dn = ds * (sg * (1.0 + n * (1.0 - sg)))
        dcv, dg, db = _ln_bwd(dn, xhc_v, rsc_ref[...], cg_ref[...])
        dcv_ref[...] = dcv
        st_ref[0:1, :] += dg
        st_ref[1:2, :] += db
        st_ref[2:3, :] += jnp.sum(dcv, axis=0, keepdims=True)
        st_ref[3:4, :] += jnp.sum(dz, axis=0, keepdims=True)

    return pl.pallas_call(
        body, name="conv_bwd1", grid=(t // tm,),
        in_specs=[_tile(tm, d), _tile(tm, d), _tile(tm, 1), ANY, _row(d), _row(d)],
        out_specs=[_tile(tm, d), _tile(tm, d), _tile(tm, d), _fixed(8, d)],
        out_shape=[jax.ShapeDtypeStruct((t, d), BF16), jax.ShapeDtypeStruct((t, d), BF16),
                   jax.ShapeDtypeStruct((t, d), F32), jax.ShapeDtypeStruct((8, d), F32)],
        scratch_shapes=[_wscratch(lay, "pw2")],
        compiler_params=_params(),
    )(dz1, xhc, rsc, wb, cg, cb)


def _conv_bwd2(dz1, dcv, ag, wb, lay, w_dw, tm, hosted=None):
    t, d = dz1.shape
    nsteps = t // tm

    def body(dz_ref, dcv_ref, dcvn_ref, ag_ref, wb_ref, wdw_ref,
             gx_ref, dh_ref, dw_ref, db1_ref, w1_s, ubuf, dbuf, du_s, dwacc):
        i = pl.program_id(0)

        @pl.when(i == 0)
        def _():
            _load_weight(wb_ref, lay, "pw1t", w1_s)
            dwacc[...] = jnp.zeros(dwacc.shape, F32)
            db1_ref[...] = jnp.zeros(db1_ref.shape, F32)

        gx = ALPHA * dz_ref[...]
        for mc in range(d // CONV_MC):
            c0 = mc * CONV_MC
            acols, gcols = slice(c0, c0 + CONV_MC), slice(d + c0, d + c0 + CONV_MC)
            a = ag_ref[:, acols].astype(F32)
            sg = _sigmoid(ag_ref[:, gcols].astype(F32))
            ubuf[:, acols] = a * sg
            dcv_next = jnp.where(i < nsteps - 1, dcvn_ref[:, acols], 0.0)
            dcv_c = jnp.concatenate([dcv_ref[:, acols], dcv_next, jnp.zeros((8, CONV_MC), F32)], axis=0)
            for r in range(8):
                dbuf[r, :, acols] = _shifted(dcv_c, r)
            for rb in range(tm // CONV_RB):
                t0 = rb * CONV_RB
                for lc in range(CONV_MC // CONV_LC):
                    lanes = slice(c0 + lc * CONV_LC, c0 + (lc + 1) * CONV_LC)
                    ucur = ubuf[t0:t0 + CONV_RB, lanes]
                    acc = jnp.zeros((CONV_RB, CONV_LC), F32)
                    for k in range(CONV_WIDTH):
                        sd = CONV_WIDTH - 1 - k
                        q = t0 + 8 * (sd // 8)
                        dk = dbuf[sd % 8, q:q + CONV_RB, lanes]
                        acc = acc + dk * wdw_ref[k:k + 1, lanes]
                        prod = ucur * dk
                        part = prod[0:8]
                        for j in range(1, CONV_RB // 8):
                            part = part + prod[8 * j:8 * j + 8]
                        dwacc[k, :, lanes] += part
                    du_s[t0:t0 + CONV_RB, lanes] = acc
            du = du_s[:, acols]
            da = du * sg
            dg = du * a * sg * (1.0 - sg)
            dab, dgb = da.astype(BF16), dg.astype(BF16)
            dh_ref[:, acols] = dab
            dh_ref[:, gcols] = dgb
            db1_ref[:, acols] += jnp.sum(da, axis=0, keepdims=True)
            db1_ref[:, gcols] += jnp.sum(dg, axis=0, keepdims=True)
            gx = gx + _dot(dab, _wrows(w1_s, c0, CONV_MC)) + _dot(dgb, _wrows(w1_s, d + c0, CONV_MC))
        gx_ref[...] = gx

        @pl.when(i == nsteps - 1)
        def _():
            dw_ref[...] = jnp.sum(dwacc[...], axis=1)

    hb = tm // HALO
    last = t // HALO - 1
    return _gridded_call(
        body, "conv_bwd2", nsteps,
        [_tile(tm, d), _tile(tm, d),
         pl.BlockSpec((HALO, d), lambda i: (jnp.minimum((i + 1) * hb, last), 0)),
         _tile(tm, 2 * d), ANY, _fixed(HALO, d)],
        [_tile(tm, d), _tile(tm, 2 * d), _fixed(HALO, d), _row(2 * d)],
        [jax.ShapeDtypeStruct((t, d), F32), jax.ShapeDtypeStruct((t, 2 * d), BF16),
         jax.ShapeDtypeStruct((HALO, d), F32), jax.ShapeDtypeStruct((1, 2 * d), F32)],
        [_wscratch(lay, "pw1t"), pltpu.VMEM((tm, d), F32),
         pltpu.VMEM((8, HALO + tm + 8, d), F32), pltpu.VMEM((tm, d), F32),
         pltpu.VMEM((HALO, 8, d), F32)],
        (dz1, dcv, dcv, ag, wb, w_dw), hosted)


FFN_FC = 256
FFN_AHEAD = 1


def _ffn_fwd(xh_in, g_in, b_in, wb, lay, layer, tm, *, kv=None, loss=None):
    t, d = xh_in.shape
    f = lay.f
    names = (f"gt{layer}", f"ut{layer}", f"dn{layer}")

    def body(*refs):
        xh_ref, gi_ref, bi_ref, wb_ref = refs[:4]
        pos = 4
        if kv is not None:
            go_ref, bo_ref, wkv_ref, bkv_ref = refs[pos:pos + 4]
            pos += 4
        if loss is not None:
            go_ref, bo_ref, tgt_ref = refs[pos:pos + 3]
            pos += 3
        xb_ref, hg_ref, hu_ref = refs[pos:pos + 3]
        pos += 3
        if kv is not None:
            xho_ref, rso_ref, xob_ref, kv_ref = refs[pos:pos + 4]
            pos += 4
        if loss is not None:
            dz_ref, st_ref, loss_ref = refs[pos:pos + 3]
            pos += 3
        gt_s, ut_s, dn_s, xin_s, fo_s = refs[pos:pos + 5]
        i = pl.program_id(0)

        @pl.when(i == 0)
        def _():
            for name, dst in zip(names, (gt_s, ut_s, dn_s)):
                _load_weight(wb_ref, lay, name, dst)
            xin_s[...] = jnp.zeros(xin_s.shape, F32)
            fo_s[...] = jnp.zeros(fo_s.shape, F32)
            if loss is not None:
                st_ref[...] = jnp.zeros(st_ref.shape, F32)
                loss_ref[...] = jnp.zeros(loss_ref.shape, F32)

        xin_prev = xin_s[...]
        xho, rso = _ln_fwd(ALPHA * xin_prev + fo_s[...])
        if kv is not None:
            xho_ref[...] = xho
            rso_ref[...] = rso
            xob_ref[...] = (xho * go_ref[...] + bo_ref[...]).astype(BF16)
        if loss is not None:
            real = i > 0
            diff = xho * go_ref[...] + bo_ref[...] - tgt_ref[...]
            loss_ref[...] += jnp.where(real, (0.5 / d) * jnp.sum(diff * diff), 0.0)
            dz, dg, db = _ln_bwd(diff * (1.0 / d), xho, rso, go_ref[...])
            dz_ref[...] = dz
            st_ref[0:1, :] += jnp.where(real, dg, 0.0)
            st_ref[1:2, :] += jnp.where(real, db, 0.0)

        xin = xh_ref[...] * gi_ref[...] + bi_ref[...]
        xb = xin.astype(BF16)
        xb_ref[...] = xb

        def up(c):
            return (_dot_nt(xb, _wrows(gt_s, c * FFN_FC, FFN_FC)), _dot_nt(xb, _wrows(ut_s, c * FFN_FC, FFN_FC)))

        fo = jnp.zeros((tm, d), F32)
        nc = f // FFN_FC
        ahead = [up(c) for c in range(min(FFN_AHEAD, nc))]
        for c in range(nc):
            rows = slice(c * FFN_FC, (c + 1) * FFN_FC)
            hg, hu = ahead.pop(0)
            if c + FFN_AHEAD < nc:
                ahead.append(up(c + FFN_AHEAD))
            hg_ref[:, rows] = hg.astype(BF16)
            hu_ref[:, rows] = hu.astype(BF16)
            act = hg * _sigmoid(hg) * hu
            fo = fo + _dot(act.astype(BF16), _wrows(dn_s, c * FFN_FC, FFN_FC))
        xin_s[...] = xin
        fo_s[...] = fo
        if kv is not None:
            kv_ref[...] = (_dot(xob_ref[...], wkv_ref[...]) + bkv_ref[...]).astype(BF16)

    nsteps = t // tm
    in_specs = [_tile_cur(tm, d, nsteps), _row(d), _row(d), ANY]
    args = [xh_in, g_in, b_in, wb]
    out_specs = [_tile_cur(tm, d, nsteps), _tile_cur(tm, f, nsteps), _tile_cur(tm, f, nsteps)]
    out_shape = [jax.ShapeDtypeStruct((t, d), BF16), jax.ShapeDtypeStruct((t, f), BF16),
                 jax.ShapeDtypeStruct((t, f), BF16)]
    if kv is not None:
        in_specs += [_row(d), _row(d), _fixed(d, 2 * BLOCK), _row(2 * BLOCK)]
        args += list(kv)
        out_specs += [_tile_prev(tm, d), _tile_prev(tm, 1), _tile_prev(tm, d), _tile_prev(tm, 2 * BLOCK)]
        out_shape += [jax.ShapeDtypeStruct((t, d), F32), jax.ShapeDtypeStruct((t, 1), F32),
                      jax.ShapeDtypeStruct((t, d), BF16), jax.ShapeDtypeStruct((t, 2 * BLOCK), BF16)]
    if loss is not None:
        in_specs += [_row(d), _row(d), _tile_prev(tm, d)]
        args += list(loss)
        out_specs += [_tile_prev(tm, d), _fixed(8, d), _fixed(8, 128)]
        out_shape += [jax.ShapeDtypeStruct((t, d), F32), jax.ShapeDtypeStruct((8, d), F32),
                      jax.ShapeDtypeStruct((8, 128), F32)]
    return pl.pallas_call(
        body, name=f"ffn_fwd{layer}", grid=(nsteps + 1,), in_specs=in_specs, out_specs=out_specs,
        out_shape=out_shape,
        scratch_shapes=[_wscratch(lay, n) for n in names] + [pltpu.VMEM((tm, d), F32), pltpu.VMEM((tm, d), F32)],
        compiler_params=_params(),
    )(*args)


def _ffn_bwd(dz, hg, hu, xh_in, rs_in, g_in, wb, lay, layer, tm, hosted=None, qkv=None):
    t, d = dz.shape
    f = lay.f
    nsteps = t // tm
    nbt = tm // BLOCK
    names = (f"gt{layer}", f"ut{layer}", f"dn{layer}")

    def body(*refs):
        dz_ref, hg_ref, hu_ref, xh_ref, rs_ref, gi_ref, wb_ref = refs[:7]
        pos = 7
        if qkv is not None:
            dq_ref, dkc_ref, dkp_ref, dkn_ref, xho_ref, rso_ref, go_ref, wkv_ref = refs[pos:pos + 8]
            pos += 8
        dzb_ref, act_ref, dhg_ref, dhu_ref, dzp_ref, st_ref = refs[pos:pos + 6]
        pos += 6
        if qkv is not None:
            dkv_ref, sto_ref, dbkv_ref = refs[pos:pos + 3]
            pos += 3
        gt_s, ut_s, dn_s = refs[pos:pos + 3]
        i = pl.program_id(0)

        @pl.when(i == 0)
        def _():
            for name, dst in zip(names, (gt_s, ut_s, dn_s)):
                _load_weight(wb_ref, lay, name, dst)
            st_ref[...] = jnp.zeros(st_ref.shape, F32)
            if qkv is not None:
                _load_weight(wb_ref, lay, "wq", refs[pos + 3])
                sto_ref[...] = jnp.zeros(sto_ref.shape, F32)
                dbkv_ref[...] = jnp.zeros(dbkv_ref.shape, F32)

        if qkv is None:
            dzv = dz_ref[...]
        else:
            nxt = jnp.where(i < nsteps - 1, dkn_ref[...], 0.0)
            shifted = jnp.concatenate([dkp_ref[pl.ds(BLOCK, tm - BLOCK), :], nxt], axis=0) if nbt > 1 else nxt
            dkv = dkc_ref[...] + shifted
            dkvb = dkv.astype(BF16)
            dkv_ref[...] = dkvb
            dbkv_ref[...] += jnp.sum(dkv, axis=0, keepdims=True)
            dxo = (ALPHA * dz_ref[...] + _dot_nt(dq_ref[...], refs[pos + 3][...])
                   + _dot_nt(dkvb, wkv_ref[...]))
            dzv, dgo, dbo = _ln_bwd(dxo, xho_ref[...], rso_ref[...], go_ref[...])
            sto_ref[0:1, :] += dgo
            sto_ref[1:2, :] += dbo
        dzb = dzv.astype(BF16)
        dzb_ref[...] = dzb
        dx = ALPHA * dzv
        def back(c):
            return _dot_nt(dzb, _wrows(dn_s, c * FFN_FC, FFN_FC))

        nc = f // FFN_FC
        ahead = [back(c) for c in range(min(FFN_AHEAD, nc))]
        for c in range(nc):
            rows = slice(c * FFN_FC, (c + 1) * FFN_FC)
            dact = ahead.pop(0)
            if c + FFN_AHEAD < nc:
                ahead.append(back(c + FFN_AHEAD))
            hg_v = hg_ref[:, rows].astype(F32)
            hu_v = hu_ref[:, rows].astype(F32)
            sg = _sigmoid(hg_v)
            silu = hg_v * sg
            act_ref[:, rows] = (silu * hu_v).astype(BF16)
            dhu = (dact * silu).astype(BF16)
            dhg = (dact * hu_v * (sg * (1.0 + hg_v * (1.0 - sg)))).astype(BF16)
            dhu_ref[:, rows] = dhu
            dhg_ref[:, rows] = dhg
            dx = (dx + _dot(dhg, _wrows(gt_s, c * FFN_FC, FFN_FC))
                  + _dot(dhu, _wrows(ut_s, c * FFN_FC, FFN_FC)))
        dzp, dg, db = _ln_bwd(dx, xh_ref[...], rs_ref[...], gi_ref[...])
        dzp_ref[...] = dzp
        st_ref[0:1, :] += dg
        st_ref[1:2, :] += db

    in_specs = [_tile(tm, d), _tile(tm, f), _tile(tm, f), _tile(tm, d), _tile(tm, 1), _row(d), ANY]
    args = [dz, hg, hu, xh_in, rs_in, g_in, wb]
    out_specs = [_tile(tm, d), _tile(tm, f), _tile(tm, f), _tile(tm, f), _tile(tm, d), _fixed(8, d)]
    out_shape = [jax.ShapeDtypeStruct((t, d), BF16), jax.ShapeDtypeStruct((t, f), BF16),
                 jax.ShapeDtypeStruct((t, f), BF16), jax.ShapeDtypeStruct((t, f), BF16),
                 jax.ShapeDtypeStruct((t, d), F32), jax.ShapeDtypeStruct((8, d), F32)]
    scratch = [_wscratch(lay, n) for n in names]
    if qkv is not None:
        dq, dkc, dkp, xh_out, rs_out, g_out, wkv = qkv
        last = t // BLOCK - 1
        in_specs += [_tile(tm, d), _tile(tm, 2 * BLOCK), _tile(tm, 2 * BLOCK),
                     pl.BlockSpec((BLOCK, 2 * BLOCK), lambda i: (jnp.minimum((i + 1) * nbt, last), 0)),
                     _tile(tm, d), _tile(tm, 1), _row(d), _fixed(d, 2 * BLOCK)]
        args += [dq, dkc, dkp, dkp, xh_out, rs_out, g_out, wkv]
        out_specs += [_tile(tm, 2 * BLOCK), _fixed(8, d), _row(2 * BLOCK)]
        out_shape += [jax.ShapeDtypeStruct((t, 2 * BLOCK), BF16), jax.ShapeDtypeStruct((8, d), F32),
                      jax.ShapeDtypeStruct((1, 2 * BLOCK), F32)]
        scratch.append(_wscratch(lay, "wq"))
    return _gridded_call(body, f"ffn_bwd{layer}", nsteps, in_specs, out_specs, out_shape, scratch, args, hosted)


ATTN_GROUP = 2


def _alibi_slope(h, nq):
    return 2.0 ** (-ALIBI_MAX * (h + 1) / nq)


def _fill_alibi_bias(bias_s, nq, keys_on_rows=False):
    shape = (2 * BLOCK, BLOCK) if keys_on_rows else (BLOCK, 2 * BLOCK)
    qi = lax.broadcasted_iota(jnp.int32, shape, 1 if keys_on_rows else 0)
    kj = lax.broadcasted_iota(jnp.int32, shape, 0 if keys_on_rows else 1)
    delta = qi + BLOCK - kj
    valid = (delta >= 0) & (delta < BLOCK)
    dist = jnp.where(valid, delta.astype(F32), MASKED_DIST)
    dist_first = jnp.where(kj >= BLOCK, dist, MASKED_DIST)
    for h in range(nq):
        bias_s[0, h] = _alibi_slope(h, nq) * dist
        bias_s[1, h] = _alibi_slope(h, nq) * dist_first


def _padded_kv(kvb, kvh, transposed_v=False):
    lane = lax.broadcasted_iota(jnp.int32, (2 * BLOCK, BLOCK), 1)
    mine = (lane < HEAD_DIM) if kvh == 0 else (lane >= HEAD_DIM)
    out = []
    for sec, transposed in ((kvb[:, :BLOCK], False), (kvb[:, BLOCK:], transposed_v)):
        m = jnp.where(mine, sec.astype(F32), 0.0)
        sw = pltpu.roll(m, HEAD_DIM, 1)
        pair = (m, sw) if kvh == 0 else (sw, m)
        out.append(tuple((p.T if transposed else p).astype(BF16) for p in pair))
    return out


def _attn_fwd(xh_in, g_in, b_in, x_in_b, kvs, wb, lay, bq, sinks, bo, tm):
    t, d = xh_in.shape
    nq = d // HEAD_DIM
    pairs_per_kv = (d // BLOCK) // N_KV_HEADS
    nbt = tm // BLOCK
    scale = HEAD_DIM ** -0.5

    def body(xh_ref, gi_ref, bi_ref, xb_ref, kv_ref, kvp_ref, wb_ref, bq_ref, sk_ref, bo_ref,
             q_ref, o_ref, lse_ref, xho_ref, rso_ref, wq_s, wo_s, kvall, q_s, o_s, bias_s):
        i = pl.program_id(0)

        @pl.when(i == 0)
        def _():
            _load_weight(wb_ref, lay, "wq", wq_s)
            _load_weight(wb_ref, lay, "wo", wo_s)
            _fill_alibi_bias(bias_s, nq, keys_on_rows=True)

        qv = ((_dot(xb_ref[...], _wfull(wq_s)) + bq_ref[...]) * scale).astype(BF16)
        q_s[...] = qv
        q_ref[...] = qv
        kvall[pl.ds(0, BLOCK), :] = kvp_ref[...]
        kvall[pl.ds(BLOCK, tm), :] = kv_ref[...]
        head_row = lax.broadcasted_iota(jnp.int32, (BLOCK, BLOCK), 0)

        def score_phase(j):
            rows = slice(j * BLOCK, (j + 1) * BLOCK)
            kvb = kvall[j * BLOCK:(j + 2) * BLOCK, :]
            first = (i * nbt + j == 0).astype(jnp.int32)
            pads = [_padded_kv(kvb, kvh, transposed_v=True) for kvh in range(N_KV_HEADS)]
            scores = []
            for a in range(d // BLOCK):
                kpad = pads[a // pairs_per_kv][0]
                qp = q_s[rows, a * BLOCK:(a + 1) * BLOCK]
                for e in range(2):
                    scores.append(_dot_nt(kpad[e], qp) - bias_s[first, 2 * a + e])
            return rows, pads, scores

        def softmax_phase(state):
            rows, pads, scores = state
            probs, inv = [], []
            lse_t = jnp.zeros((BLOCK, BLOCK), F32)
            for h in range(nq):
                sink = sk_ref[:, h:h + 1]
                m = jnp.maximum(jnp.max(scores[h], axis=0, keepdims=True), sink)
                p = jnp.exp(scores[h] - m)
                l = jnp.sum(p, axis=0, keepdims=True) + jnp.exp(sink - m)
                lse_t = jnp.where(head_row == h, m + jnp.log(l), lse_t)
                probs.append(p.astype(BF16))
                inv.append(1.0 / l)
            lse_ref[rows, :] = lse_t.T
            return rows, pads, probs, inv

        def value_phase(state):
            rows, pads, probs, inv = state
            for a in range(d // BLOCK):
                vpad_t = pads[a // pairs_per_kv][1]
                opair_t = (_dot(vpad_t[0], probs[2 * a]) * inv[2 * a]
                           + _dot(vpad_t[1], probs[2 * a + 1]) * inv[2 * a + 1])
                o_s[rows, a * BLOCK:(a + 1) * BLOCK] = opair_t.T.astype(BF16)

        for j0 in range(0, nbt, ATTN_GROUP):
            group = range(j0, min(j0 + ATTN_GROUP, nbt))
            for state in [softmax_phase(s) for s in [score_phase(j) for j in group]]:
                value_phase(state)
        ov = o_s[...]
        o_ref[...] = ov
        xin = xh_ref[...] * gi_ref[...] + bi_ref[...]
        xho, rso = _ln_fwd(ALPHA * xin + _dot(ov, _wfull(wo_s)) + bo_ref[...])
        xho_ref[...] = xho
        rso_ref[...] = rso

    return pl.pallas_call(
        body, name="attn_fwd", grid=(t // tm,),
        in_specs=[_tile(tm, d), _row(d), _row(d), _tile(tm, d), _tile(tm, 2 * BLOCK),
                  pl.BlockSpec((BLOCK, 2 * BLOCK), lambda i: (jnp.maximum(i * nbt - 1, 0), 0)),
                  ANY, _row(d), _row(nq), _row(d)],
        out_specs=[_tile(tm, d), _tile(tm, d), _tile(tm, BLOCK), _tile(tm, d), _tile(tm, 1)],
        out_shape=[jax.ShapeDtypeStruct((t, d), BF16), jax.ShapeDtypeStruct((t, d), BF16),
                   jax.ShapeDtypeStruct((t, BLOCK), F32), jax.ShapeDtypeStruct((t, d), F32),
                   jax.ShapeDtypeStruct((t, 1), F32)],
        scratch_shapes=[_wscratch(lay, "wq"), _wscratch(lay, "wo"),
                        pltpu.VMEM((BLOCK + tm, 2 * BLOCK), BF16), pltpu.VMEM((tm, d), BF16),
                        pltpu.VMEM((tm, d), BF16), pltpu.VMEM((2, nq, 2 * BLOCK, BLOCK), F32)],
        compiler_params=_params(),
    )(xh_in, g_in, b_in, x_in_b, kvs, kvs, wb, bq, sinks, bo)


def _attn_bwd(dz, q, o, lse, kvs, wb, lay, sinks, tm, hosted=None):
    t, d = dz.shape
    nq = d // HEAD_DIM
    pairs_per_kv = (d // BLOCK) // N_KV_HEADS
    nbt = tm // BLOCK
    scale = HEAD_DIM ** -0.5

    def body(dz_ref, q_ref, o_ref, lse_ref, kv_ref, kvp_ref, wb_ref, sk_ref,
             dzb_ref, dq_ref, dkc_ref, dkp_ref, st_ref, dsk_ref, wo_s, kvall, do_s, dq_s, bias_s):
        i = pl.program_id(0)

        @pl.when(i == 0)
        def _():
            _load_weight(wb_ref, lay, "wo", wo_s)
            _fill_alibi_bias(bias_s, nq, keys_on_rows=True)
            st_ref[...] = jnp.zeros(st_ref.shape, F32)
            dsk_ref[...] = jnp.zeros(dsk_ref.shape, F32)

        dzv = dz_ref[...]
        dzb = dzv.astype(BF16)
        dzb_ref[...] = dzb
        do_s[...] = _dot_nt(dzb, _wfull(wo_s))
        kvall[pl.ds(0, BLOCK), :] = kvp_ref[...]
        kvall[pl.ds(BLOCK, tm), :] = kv_ref[...]
        lane = lax.broadcasted_iota(jnp.int32, (BLOCK, BLOCK), 1)
        lane1 = lax.broadcasted_iota(jnp.int32, (1, BLOCK), 1)
        lane2 = lax.broadcasted_iota(jnp.int32, (2 * BLOCK, BLOCK), 1)
        halves = (lane < HEAD_DIM, lane >= HEAD_DIM)
        sel_row = lax.broadcasted_iota(jnp.int32, (8, BLOCK), 0)
        sel_lane = lax.broadcasted_iota(jnp.int32, (8, BLOCK), 1)
        head_sel = jnp.where((sel_row == 0) & (sel_lane < HEAD_DIM) | (sel_row == 1) & (sel_lane >= HEAD_DIM),
                             1.0, 0.0).astype(BF16)

        def score_phase(j):
            rows = slice(j * BLOCK, (j + 1) * BLOCK)
            kvb = kvall[j * BLOCK:(j + 2) * BLOCK, :]
            first = (i * nbt + j == 0).astype(jnp.int32)
            pads = [_padded_kv(kvb, kvh) for kvh in range(N_KV_HEADS)]
            scores, dps, dhs, qms, doms = [], [], [], [], []
            for a in range(d // BLOCK):
                kpad, vpad = pads[a // pairs_per_kv]
                cols = slice(a * BLOCK, (a + 1) * BLOCK)
                qp = q_ref[rows, cols]
                dop = do_s[rows, cols]
                dopb = dop.astype(BF16)
                prod = dop * o_ref[rows, cols].astype(F32)
                hi = prod.astype(BF16)
                lo = (prod - hi.astype(F32)).astype(BF16)
                dh_pair = _dot_nt(head_sel, hi) + _dot_nt(head_sel, lo)
                for e in range(2):
                    scores.append(_dot_nt(kpad[e], qp) - bias_s[first, 2 * a + e])
                    dps.append(_dot_nt(vpad[e], dopb))
                    dhs.append(dh_pair[e:e + 1, :])
                    qms.append(jnp.where(halves[e], qp, jnp.zeros_like(qp)))
                    doms.append(jnp.where(halves[e], dopb, jnp.zeros_like(dopb)))
            return rows, pads, scores, dps, dhs, qms, doms

        def softmax_phase(state):
            rows, pads, scores, dps, dhs, qms, doms = state
            dss, pbs = [], []
            dsk_t = jnp.zeros((1, BLOCK), F32)
            lse_t = lse_ref[rows, :].T
            for h in range(nq):
                lse_h = lse_t[h:h + 1, :]
                p = jnp.exp(scores[h] - lse_h)
                dss.append((p * (dps[h] - dhs[h])).astype(BF16))
                pbs.append(p.astype(BF16))
                dsink = -jnp.sum(jnp.exp(sk_ref[:, h:h + 1] - lse_h) * dhs[h], axis=1, keepdims=True)
                dsk_t = jnp.where(lane1 == h, dsink, dsk_t)
            dsk_ref[...] += dsk_t
            return rows, pads, dss, pbs, qms, doms

        def grad_phase(state):
            rows, pads, dss, pbs, qms, doms = state
            dsecs = []
            for kvh in range(N_KV_HEADS):
                kpad_t = [p.astype(F32).T.astype(BF16) for p in pads[kvh][0]]
                dk_acc = jnp.zeros((2 * BLOCK, BLOCK), F32)
                dv_acc = jnp.zeros((2 * BLOCK, BLOCK), F32)
                for a in range(kvh * pairs_per_kv, (kvh + 1) * pairs_per_kv):
                    dqp_t = _dot(kpad_t[0], dss[2 * a]) + _dot(kpad_t[1], dss[2 * a + 1])
                    dq_s[rows, a * BLOCK:(a + 1) * BLOCK] = dqp_t.T * scale
                    for e in range(2):
                        h = 2 * a + e
                        dk_acc = dk_acc + _dot(dss[h], qms[h])
                        dv_acc = dv_acc + _dot(pbs[h], doms[h])
                dsecs.append((dk_acc + pltpu.roll(dk_acc, HEAD_DIM, 1), dv_acc + pltpu.roll(dv_acc, HEAD_DIM, 1)))
            lo = lane2 < HEAD_DIM
            dkv = jnp.concatenate([jnp.where(lo, dsecs[0][0], dsecs[1][0]),
                                   jnp.where(lo, dsecs[0][1], dsecs[1][1])], axis=1)
            dkp_ref[rows, :] = dkv[:BLOCK]
            dkc_ref[rows, :] = dkv[BLOCK:]

        for j0 in range(0, nbt, ATTN_GROUP):
            group = range(j0, min(j0 + ATTN_GROUP, nbt))
            for state in [softmax_phase(s) for s in [score_phase(j) for j in group]]:
                grad_phase(state)
        dqv = dq_s[...]
        dq_ref[...] = dqv.astype(BF16)
        st_ref[0:1, :] += jnp.sum(dqv, axis=0, keepdims=True)
        st_ref[1:2, :] += jnp.sum(dzv, axis=0, keepdims=True)

    return _gridded_call(
        body, "attn_bwd", t // tm,
        [_tile(tm, d), _tile(tm, d), _tile(tm, d), _tile(tm, BLOCK), _tile(tm, 2 * BLOCK),
         pl.BlockSpec((BLOCK, 2 * BLOCK), lambda i: (jnp.maximum(i * nbt - 1, 0), 0)),
         ANY, _row(nq)],
        [_tile(tm, d), _tile(tm, d), _tile(tm, 2 * BLOCK), _tile(tm, 2 * BLOCK),
         _fixed(8, d), _row(BLOCK)],
        [jax.ShapeDtypeStruct((t, d), BF16), jax.ShapeDtypeStruct((t, d), BF16),
         jax.ShapeDtypeStruct((t, 2 * BLOCK), F32), jax.ShapeDtypeStruct((t, 2 * BLOCK), F32),
         jax.ShapeDtypeStruct((8, d), F32), jax.ShapeDtypeStruct((1, BLOCK), F32)],
        [_wscratch(lay, "wo"), pltpu.VMEM((BLOCK + tm, 2 * BLOCK), BF16),
         pltpu.VMEM((tm, d), F32), pltpu.VMEM((tm, d), F32),
         pltpu.VMEM((2, nq, 2 * BLOCK, BLOCK), F32)],
        (dz, q, o, lse, kvs, kvs, wb, sinks), hosted)


def _tn_matmul(a, b, name, bm, tk):
    t, m = a.shape
    n = b.shape[1]
    ksteps = t // tk

    nc = max(n // 256, 1)
    cw = n // nc

    def body(a_ref, b_ref, o_ref, acc):
        k = pl.program_id(1)

        @pl.when(k == 0)
        def _():
            acc[...] = jnp.zeros(acc.shape, F32)

        at = a_ref[...].T
        for c in range(nc):
            cols = slice(c * cw, (c + 1) * cw)
            acc[:, cols] += _dot(at, b_ref[:, cols])

        @pl.when(k == ksteps - 1)
        def _():
            o_ref[...] = acc[...].astype(BF16)

    return pl.pallas_call(
        body, name=name, grid=(m // bm, ksteps),
        in_specs=[pl.BlockSpec((tk, bm), lambda j, k: (k, j)), pl.BlockSpec((tk, n), lambda j, k: (k, 0))],
        out_specs=pl.BlockSpec((bm, n), lambda j, k: (j, 0)),
        out_shape=jax.ShapeDtypeStruct((m, n), BF16),
        scratch_shapes=[pltpu.VMEM((bm, n), F32)],
        compiler_params=pltpu.CompilerParams(dimension_semantics=("arbitrary", "arbitrary"),
                                             vmem_limit_bytes=VMEM_LIMIT),
    )(a, b)


def _all_gather(arrays, name):
    n = len(arrays)

    def body(*refs):
        ins, outs = refs[:n], refs[n:2 * n]
        send_sems, recv_sems, local_sems = refs[2 * n:]
        x, y, c = _me()
        me, sibling = (x, y, c), (x, y, 1 - c)
        chips = [(1 - x, y), (x, 1 - y), (1 - x, 1 - y)]

        def slot(ref, dev):
            return ref.at[4 * dev[0] + 2 * dev[1] + dev[2]]

        def copy(a, k, block, to, src=None):
            return pltpu.make_async_remote_copy(
                src_ref=slot(outs[a], block) if src is None else src, dst_ref=slot(outs[a], block),
                send_sem=send_sems.at[a, k], recv_sem=recv_sems.at[a, k], device_id=to, device_id_type=MESH)

        mine = [pltpu.make_async_copy(ins[a], slot(outs[a], me), local_sems.at[a]) for a in range(n)]
        for cp in mine:
            cp.start()
        first = []
        for a in range(n):
            first.append(copy(a, 0, me, sibling, src=ins[a]))
            first += [copy(a, 1 + j, me, (*chip, c), src=ins[a]) for j, chip in enumerate(chips)]
        for cp in first:
            cp.start()
        passed = []
        for a in range(n):
            for j, chip in enumerate(chips):
                copy(a, 1 + j, (*chip, c), me).wait_recv()
                cp = copy(a, 4 + j, (*chip, c), sibling)
                cp.start()
                passed.append(cp)
        for a in range(n):
            copy(a, 0, sibling, me).wait_recv()
            for j, chip in enumerate(chips):
                copy(a, 4 + j, (*chip, 1 - c), me).wait_recv()
        for cp in first + passed:
            cp.wait_send()
        for cp in mine:
            cp.wait()

    return pl.pallas_call(
        body, name=name, in_specs=[ANY] * n, out_specs=[ANY] * n,
        out_shape=[jax.ShapeDtypeStruct((N_DEV,) + a.shape, a.dtype) for a in arrays],
        scratch_shapes=[pltpu.SemaphoreType.DMA((n, 7)), pltpu.SemaphoreType.DMA((n, 7)),
                        pltpu.SemaphoreType.DMA((n,))],
    )(*arrays)


def _exchange(arrays, name):
    n = len(arrays)
    blocked = [a.ndim == 3 for a in arrays]

    def body(*refs):
        ins, outs = refs[:n], refs[n:2 * n]
        send_sems, recv_sems, local_sems = refs[2 * n:]
        me = _index(_me())

        def src(k, dev):
            return ins[k].at[dev] if blocked[k] else ins[k]

        local = [pltpu.make_async_copy(src(k, me), outs[k].at[me], local_sems.at[k]) for k in range(n)]
        sends, arrivals = [], []
        for k in range(n):
            for mask in range(1, N_DEV):
                peer = _peer(mask)
                sends.append(pltpu.make_async_remote_copy(
                    src_ref=src(k, _index(peer)), dst_ref=outs[k].at[me], send_sem=send_sems.at[k, mask - 1],
                    recv_sem=recv_sems.at[k, mask - 1], device_id=peer, device_id_type=MESH))
                arrivals.append(pltpu.make_async_remote_copy(
                    src_ref=src(k, me), dst_ref=outs[k].at[_index(peer)], send_sem=send_sems.at[k, mask - 1],
                    recv_sem=recv_sems.at[k, mask - 1], device_id=_me(), device_id_type=MESH))
        for cp in local + sends:
            cp.start()
        for cp in arrivals:
            cp.wait_recv()
        for cp in sends:
            cp.wait_send()
        for cp in local:
            cp.wait()

    return pl.pallas_call(
        body, name=name, in_specs=[ANY] * n, out_specs=[ANY] * n,
        out_shape=[jax.ShapeDtypeStruct((N_DEV,) + a.shape[-2:], a.dtype) for a in arrays],
        scratch_shapes=[pltpu.SemaphoreType.DMA((n, 7)), pltpu.SemaphoreType.DMA((n, 7)),
                        pltpu.SemaphoreType.DMA((n,))],
    )(*arrays)


def _adamw_update(g, w_ref, m_ref, v_ref, go_ref, d_ref, mo_ref, vo_ref):
    mn = ADAM_B1 * m_ref[...] + (1.0 - ADAM_B1) * g
    vn = ADAM_B2 * v_ref[...] + (1.0 - ADAM_B2) * (g * g)
    m_hat = mn / (1.0 - ADAM_B1 ** ADAM_STEP)
    v_hat = vn / (1.0 - ADAM_B2 ** ADAM_STEP)
    go_ref[...] = g
    d_ref[...] = -ADAM_LR * (m_hat / (jnp.sqrt(v_hat) + ADAM_EPS) + ADAM_WD * w_ref[...])
    mo_ref[...] = mn
    vo_ref[...] = vn


def _sum_sources(g_refs, layer):
    total = None
    for l, g_ref in enumerate(g_refs):
        g = g_ref[0].astype(F32)
        for s in range(1, N_DEV):
            g = g + g_ref[s].astype(F32)
        total = g if total is None else jnp.where(layer == l, g, total)
    return total


def _layer_block(l_mine, nblocks):
    def index(l, j):
        return (0, jnp.where(l == l_mine, j, jnp.where(l < l_mine, 0, nblocks - 1)), 0)
    return index


def _adamw_sum(g8, w, m, v, name, tr):
    r, width = w.shape

    def body(g_ref, *refs):
        _adamw_update(_sum_sources([g_ref], 0), *refs)

    spec = pl.BlockSpec((tr, width), lambda i: (i, 0))
    return pl.pallas_call(
        body, name=name, grid=(r // tr,),
        in_specs=[pl.BlockSpec((N_DEV, tr, width), lambda i: (0, i, 0)), spec, spec, spec],
        out_specs=[spec] * 4, out_shape=[jax.ShapeDtypeStruct((r, width), F32)] * 4,
        compiler_params=_params(),
    )(g8, w, m, v)


def _adamw_rows(g8s, w, m, v, name):
    layers, n, width = w.shape
    tr = max(r for r in range(BF16_SUBLANES, ADAMW_MAX_ROWS + 1, BF16_SUBLANES) if n % r == 0)
    nb = n // tr

    def body(*refs):
        _adamw_update(_sum_sources(refs[:layers], pl.program_id(0)), *refs[layers:])

    spec = pl.BlockSpec((None, tr, width), lambda l, j: (l, j, 0))
    return pl.pallas_call(
        body, name=name, grid=(layers, nb),
        in_specs=[pl.BlockSpec((N_DEV, tr, width), _layer_block(l, nb)) for l in range(layers)] + [spec] * 3,
        out_specs=[spec] * 4, out_shape=[jax.ShapeDtypeStruct(w.shape, F32)] * 4,
        compiler_params=pltpu.CompilerParams(dimension_semantics=("arbitrary", "arbitrary"),
                                             vmem_limit_bytes=VMEM_LIMIT),
    )(*g8s, w, m, v)


def _adamw_cols(g8s, w, m, v, name):
    layers, k, n = w.shape
    cb = min(BLOCK, n)
    nb = pl.cdiv(n, cb)

    def body(*refs):
        _adamw_update(_sum_sources(refs[:layers], pl.program_id(0)).T, *refs[layers:])

    spec = pl.BlockSpec((None, k, cb), lambda l, j: (l, 0, j))
    return pl.pallas_call(
        body, name=name, grid=(layers, nb),
        in_specs=[pl.BlockSpec((N_DEV, cb, k), _layer_block(l, nb)) for l in range(layers)] + [spec] * 3,
        out_specs=[spec] * 4, out_shape=[jax.ShapeDtypeStruct(w.shape, F32)] * 4,
        compiler_params=pltpu.CompilerParams(dimension_semantics=("arbitrary", "arbitrary"),
                                             vmem_limit_bytes=VMEM_LIMIT),
    )(*g8s, w, m, v)


def _local_step(x, target, wba, shard_b, lay, sm, tm, tk):
    t, d = x.shape
    f = lay.f
    w_dw32 = jnp.concatenate([sm["w_dw"], jnp.zeros((HALO - CONV_WIDTH, d), F32)], axis=0)
    lmg, lmb, lfg, lfb = sm["ln_mix_g"], sm["ln_mix_b"], sm["ln_ffn_g"], sm["ln_ffn_b"]
    bkv = jnp.concatenate([sm["b_k"], sm["b_v"]], axis=1)
    bm_f = f // 2 if (f // 2) % 128 == 0 else f
    tm_light = 2 * tm

    received = {}

    def exchange(grads):
        return _HostedExchange([g.reshape(N_DEV, lay.n[n], d) for n, g in grads.items()])

    def keep(grads, arrived):
        received.update(zip(grads, arrived))

    xb0, ag, xhc, rsc, xh1, rs1, wbb = _conv_fwd(x, wba, lay, w_dw32, sm["b_pw1"], sm["b_dw"], sm["cg"],
                                                 sm["cb"], sm["b_pw2"], lmg[0:1], lmb[0:1], tm_light,
                                                 hosted=_HostedGather(shard_b))
    wkv = wbb[:, lay.goff["wkv"]:lay.goff["wkv"] + lay.n["wkv"], :].reshape(d, 2 * BLOCK)
    x1b, hg0, hu0, xh2, rs2, x2b, kvs = _ffn_fwd(xh1, lmg[0:1], lmb[0:1], wbb, lay, 0, tm_light,
                                                kv=(lfg[0:1], lfb[0:1], wkv, bkv))
    q, o, lse, xh3, rs3 = _attn_fwd(xh2, lfg[0:1], lfb[0:1], x2b, kvs, wbb, lay, sm["b_q"], sm["sinks"],
                                    sm["b_o"], tm_light)
    x3b, hg1, hu1, dz4, st4, loss = _ffn_fwd(xh3, lmg[1:2], lmb[1:2], wbb, lay, 1, tm,
                                             loss=(lfg[1:2], lfb[1:2], target))

    dz4b, act1, dhg1, dhu1, dz3, st3 = _ffn_bwd(dz4, hg1, hu1, xh3, rs3, lmg[1:2], wbb, lay, 1, tm)
    g1 = {"gt1": _tn_matmul(dhg1, x3b, "dw_gate1", bm_f, tk), "ut1": _tn_matmul(dhu1, x3b, "dw_up1", bm_f, tk),
          "dn1": _tn_matmul(act1, dz4b, "dw_down1", bm_f, tk)}
    dz3b, dq, dkc, dkp, stq, dsinks, *arrived = _attn_bwd(dz3, q, o, lse, kvs, wbb, lay, sm["sinks"], tm,
                                                          hosted=exchange(g1))
    keep(g1, arrived)
    g2 = {"wq": _tn_matmul(x2b, dq, "dw_q", d, tk), "wo": _tn_matmul(o, dz3b, "dw_o", d, tk)}
    dz2b, act0, dhg0, dhu0, dz1, st1, dkv, st2, dbkv, *arrived = _ffn_bwd(
        dz3, hg0, hu0, xh1, rs1, lmg[0:1], wbb, lay, 0, tm, hosted=exchange(g2),
        qkv=(dq, dkc, dkp, xh2, rs2, lfg[0:1], wkv))
    keep(g2, arrived)
    dz1b, s_act, dcv, stc = _conv_bwd1(dz1, xhc, rsc, wba, lay, sm["cg"], sm["cb"], tm_light)
    g3 = {"gt0": _tn_matmul(dhg0, x1b, "dw_gate0", bm_f, tk), "ut0": _tn_matmul(dhu0, x1b, "dw_up0", bm_f, tk),
          "dn0": _tn_matmul(act0, dz2b, "dw_down0", bm_f, tk), "pw2": _tn_matmul(s_act, dz1b, "dw_pw2", d, tk),
          "wkv": _tn_matmul(x2b, dkv, "dw_kv", d, tk)}
    grad_x, dh1, dwdw, db1, *arrived = _conv_bwd2(dz1, dcv, ag, wba, lay, w_dw32, tm_light, hosted=exchange(g3))
    keep(g3, arrived)
    g_pw1t = _tn_matmul(dh1, xb0, "dw_pw1", d, tk)
    small = {
        "w_dw": dwdw[:CONV_WIDTH], "b_pw1": db1, "b_dw": stc[2:3], "cg": stc[0:1], "cb": stc[1:2],
        "b_pw2": stc[3:4], "b_k": dbkv[:, :BLOCK], "b_v": dbkv[:, BLOCK:], "b_q": stq[0:1],
        "sinks": dsinks[:, :d // HEAD_DIM],
        "b_o": stq[1:2],
        "ln_mix_g": jnp.concatenate([st1[0:1], st3[0:1]], axis=0),
        "ln_mix_b": jnp.concatenate([st1[1:2], st3[1:2]], axis=0),
        "ln_ffn_g": jnp.concatenate([st2[0:1], st4[0:1]], axis=0),
        "ln_ffn_b": jnp.concatenate([st2[1:2], st4[1:2]], axis=0),
    }
    return loss[0, 0], grad_x, received, g_pw1t, small


SP_ROWS = 40
SP_BDW, SP_CG, SP_CB, SP_BPW2, SP_BPW1 = 32, 33, 34, 35, 36
RP_NAMES = ("ln_mix_g", "ln_mix_b", "ln_ffn_g", "ln_ffn_b", "b_q", "b_o", "b_k", "b_v", "sinks")


def _row_forms(d, pw1, pw2, wq, wo, gate, up, down, wk, wv):
    rf = {"pw1t": pw1[0].T, "pw2": pw2[0], "wq": wq[0], "wo": wo[0],
          "wkv": jnp.concatenate([wk, wv], axis=1).reshape(-1, d)}
    for l in range(DEPTH):
        rf.update({f"gt{l}": gate[l].T, f"ut{l}": up[l].T, f"dn{l}": down[l]})
    return rf


def _pack_rows(rf, names):
    return jnp.concatenate([rf[n] for n in names], axis=0)


def _pack_small(w_dw, b_dw, cg, cb, b_pw2, b_pw1):
    cw = b_dw.shape[1]
    z = jnp.zeros((1, cw), F32)
    return jnp.concatenate([w_dw[0], z, b_dw, cg, cb, b_pw2, b_pw1.reshape(2, cw), z, z], axis=0)


def _unpack_small(p):
    cw = p.shape[1]
    return dict(w_dw=p[None, :CONV_WIDTH], b_dw=p[SP_BDW:SP_BDW + 1], cg=p[SP_CG:SP_CG + 1],
                cb=p[SP_CB:SP_CB + 1], b_pw2=p[SP_BPW2:SP_BPW2 + 1],
                b_pw1=p[SP_BPW1:SP_BPW1 + 2].reshape(1, 2 * cw))


def _small_full(g):
    d = N_DEV * g.shape[2]

    def wide(r0, n=1):
        return jnp.transpose(g[:, r0:r0 + n], (1, 0, 2)).reshape(n, d)

    return dict(w_dw=wide(0, CONV_WIDTH), b_dw=wide(SP_BDW), cg=wide(SP_CG), cb=wide(SP_CB),
                b_pw2=wide(SP_BPW2), b_pw1=g[:, SP_BPW1:SP_BPW1 + 2].reshape(1, 2 * d))


def _small_grad_blocks(sg):
    cw = sg["b_dw"].shape[1] // N_DEV

    def narrow(a):
        return jnp.transpose(a.reshape(a.shape[0], N_DEV, cw), (1, 0, 2))

    z = jnp.zeros((N_DEV, 1, cw), F32)
    return jnp.concatenate([narrow(sg["w_dw"]), z, narrow(sg["b_dw"]), narrow(sg["cg"]), narrow(sg["cb"]),
                            narrow(sg["b_pw2"]), sg["b_pw1"].reshape(N_DEV, 2, cw), z, z], axis=1)


def _pack_rep(vals, rider=0.0):
    parts = []
    for name in RP_NAMES:
        a = vals[name].reshape(-1)
        pad = -a.shape[0] % 128
        parts.append(jnp.concatenate([a, jnp.zeros((pad,), F32)]).reshape(-1, 128))
    parts.append(jnp.full((1, 128), rider, F32))
    rows = sum(p.shape[0] for p in parts)
    parts.append(jnp.zeros((-rows % 8, 128), F32))
    return jnp.concatenate(parts, axis=0)


def _rider_row(shapes):
    return sum(-(-_size(shapes[name]) // 128) for name in RP_NAMES)


def _size(shape):
    n = 1
    for s in shape:
        n *= s
    return n


def _unpack_rep(p, shapes):
    out, r = {}, 0
    for name in RP_NAMES:
        n = _size(shapes[name])
        rows = -(-n // 128)
        out[name] = p[r:r + rows].reshape(-1)[:n].reshape(shapes[name])
        r += rows
    return out


def kernel(x, conv_w_pw1, conv_b_pw1, conv_w_dw, conv_b_dw, conv_ln_g, conv_ln_b, conv_w_pw2, conv_b_pw2, kv_w_k, kv_b_k, kv_w_v, kv_b_v, attn_w_q, attn_b_q, attn_sinks, attn_w_o, attn_b_o, ffn_w_gate, ffn_w_up, ffn_w_down, ln_mix_g, ln_mix_b, ln_ffn_g, ln_ffn_b, loss_target, m_conv_w_pw1, m_conv_b_pw1, m_conv_w_dw, m_conv_b_dw, m_conv_ln_g, m_conv_ln_b, m_conv_w_pw2, m_conv_b_pw2, m_kv_w_k, m_kv_b_k, m_kv_w_v, m_kv_b_v, m_attn_w_q, m_attn_b_q, m_attn_sinks, m_attn_w_o, m_attn_b_o, m_ffn_w_gate, m_ffn_w_up, m_ffn_w_down, m_ln_mix_g, m_ln_mix_b, m_ln_ffn_g, m_ln_ffn_b, v_conv_w_pw1, v_conv_b_pw1, v_conv_w_dw, v_conv_b_dw, v_conv_ln_g, v_conv_ln_b, v_conv_w_pw2, v_conv_b_pw2, v_kv_w_k, v_kv_b_k, v_kv_w_v, v_kv_b_v, v_attn_w_q, v_attn_b_q, v_attn_sinks, v_attn_w_o, v_attn_b_o, v_ffn_w_gate, v_ffn_w_up, v_ffn_w_down, v_ln_mix_g, v_ln_mix_b, v_ln_ffn_g, v_ln_ffn_b):
    t, d = x.shape[1], x.shape[2]
    f = ffn_w_gate.shape[2] * N_DEV
    lay = _Layout(d, f)
    tm, tk = 256, min(2048, t)

    rep_shapes = dict(ln_mix_g=ln_mix_g.shape, ln_mix_b=ln_mix_b.shape, ln_ffn_g=ln_ffn_g.shape,
                      ln_ffn_b=ln_ffn_b.shape, b_q=attn_b_q.shape, b_o=attn_b_o.shape, b_k=kv_b_k.shape,
                      b_v=kv_b_v.shape, sinks=attn_sinks.shape)

    def rep_pack(lmg, lmb, lfg, lfb, bq, bo, bk, bv, sk):
        return _pack_rep(dict(ln_mix_g=lmg, ln_mix_b=lmb, ln_ffn_g=lfg, ln_ffn_b=lfb, b_q=bq, b_o=bo,
                              b_k=bk, b_v=bv, sinks=sk))

    w_rf = _row_forms(d, conv_w_pw1, conv_w_pw2, attn_w_q, attn_w_o, ffn_w_gate, ffn_w_up, ffn_w_down, kv_w_k, kv_w_v)
    w_small = _pack_small(conv_w_dw, conv_b_dw, conv_ln_g, conv_ln_b, conv_b_pw2, conv_b_pw1)
    m_small = _pack_small(m_conv_w_dw, m_conv_b_dw, m_conv_ln_g, m_conv_ln_b, m_conv_b_pw2, m_conv_b_pw1)
    v_small = _pack_small(v_conv_w_dw, v_conv_b_dw, v_conv_ln_g, v_conv_ln_b, v_conv_b_pw2, v_conv_b_pw1)
    w_rep = rep_pack(ln_mix_g, ln_mix_b, ln_ffn_g, ln_ffn_b, attn_b_q, attn_b_o, kv_b_k, kv_b_v, attn_sinks)
    m_rep = rep_pack(m_ln_mix_g, m_ln_mix_b, m_ln_ffn_g, m_ln_ffn_b, m_attn_b_q, m_attn_b_o, m_kv_b_k, m_kv_b_v, m_attn_sinks)
    v_rep = rep_pack(v_ln_mix_g, v_ln_mix_b, v_ln_ffn_g, v_ln_ffn_b, v_attn_b_q, v_attn_b_o, v_kv_b_k, v_kv_b_v, v_attn_sinks)

    wba, smg = _all_gather([_pack_rows(w_rf, lay.GATHER["a"]).astype(BF16), w_small], "gather_conv_weights")
    shard_b = _pack_rows(w_rf, lay.GATHER["b"]).astype(BF16)
    sm = _small_full(smg)
    sm.update(ln_mix_g=ln_mix_g, ln_mix_b=ln_mix_b, ln_ffn_g=ln_ffn_g, ln_ffn_b=ln_ffn_b, b_q=attn_b_q,
              b_o=attn_b_o, sinks=attn_sinks, b_k=kv_b_k.reshape(1, -1), b_v=kv_b_v.reshape(1, -1))

    loss_part, grad_x, received, g_pw1t, gsmall = _local_step(x[0], loss_target[0], wba, shard_b, lay, sm, tm, tk)

    received["pw1t"], g8_small, g8_rep = _exchange(
        [g_pw1t.reshape(N_DEV, lay.n["pw1t"], d), _small_grad_blocks(gsmall), _pack_rep(gsmall, loss_part)],
        "exchange_last_grads")

    def kv_rows(wk, wv):
        return jnp.concatenate([wk, wv], axis=1).reshape(1, -1, d)

    def kv_split(a):
        a = a.reshape(d // N_DEV, 2 * BLOCK)
        return a[:, :BLOCK], a[:, BLOCK:]

    big = dict(
        pw1=_adamw_cols([received["pw1t"]], conv_w_pw1, m_conv_w_pw1, v_conv_w_pw1, "adamw_pw1"),
        gate=_adamw_cols([received["gt0"], received["gt1"]], ffn_w_gate, m_ffn_w_gate, v_ffn_w_gate, "adamw_gate"),
        up=_adamw_cols([received["ut0"], received["ut1"]], ffn_w_up, m_ffn_w_up, v_ffn_w_up, "adamw_up"),
        down=_adamw_rows([received["dn0"], received["dn1"]], ffn_w_down, m_ffn_w_down, v_ffn_w_down, "adamw_down"),
        pw2=_adamw_rows([received["pw2"]], conv_w_pw2, m_conv_w_pw2, v_conv_w_pw2, "adamw_pw2"),
        wq=_adamw_rows([received["wq"]], attn_w_q, m_attn_w_q, v_attn_w_q, "adamw_q"),
        wo=_adamw_rows([received["wo"]], attn_w_o, m_attn_w_o, v_attn_w_o, "adamw_o"),
        wkv=[kv_split(a) for a in _adamw_rows([received["wkv"]], kv_rows(kv_w_k, kv_w_v), kv_rows(m_kv_w_k, m_kv_w_v),
                                              kv_rows(v_kv_w_k, v_kv_w_v), "adamw_kv")])
    big_out = [dict(pw1=big["pw1"][i], pw2=big["pw2"][i], wq=big["wq"][i], wo=big["wo"][i], gate=big["gate"][i],
                    up=big["up"][i], down=big["down"][i], wk=big["wkv"][i][0], wv=big["wkv"][i][1])
               for i in range(4)]
    small_out = [_unpack_small(a) for a in _adamw_sum(g8_small, w_small, m_small, v_small, "adamw_small", SP_ROWS)]
    rep_res = _adamw_sum(g8_rep, w_rep, m_rep, v_rep, "adamw_rep", w_rep.shape[0])
    rep_out = [_unpack_rep(a, rep_shapes) for a in rep_res]
    loss = rep_res[0][_rider_row(rep_shapes), 0]

    outs = [loss, grad_x[None]]
    for b, s, r in zip(big_out, small_out, rep_out):
        outs += [b["pw1"], s["b_pw1"], s["w_dw"], s["b_dw"], s["cg"], s["cb"], b["pw2"], s["b_pw2"],
                 b["wk"], r["b_k"], b["wv"], r["b_v"], b["wq"], r["b_q"], r["sinks"], b["wo"], r["b_o"],
                 b["gate"], b["up"], b["down"], r["ln_mix_g"], r["ln_mix_b"], r["ln_ffn_g"], r["ln_ffn_b"]]
    return tuple(outs)
```

```python
import jax
import jax.numpy as jnp
from jax import lax
from jax.experimental import pallas as pl
from jax.experimental.pallas import tpu as pltpu

F32 = jnp.float32
BF16 = jnp.bfloat16

N_DEV = 8
HEAD_DIM = 64
N_KV_HEADS = 2
BLOCK = 128
CONV_WIDTH = 31
HALO = 32
ALIBI_MAX = 8.0
DEPTH = 2
ALPHA = (2.0 * DEPTH) ** 0.25
LN_EPS = 1e-5
MASKED_DIST = 1e32
ADAM_LR = 0.001
ADAM_B1 = 0.9
ADAM_B2 = 0.999
ADAM_EPS = 1e-08
ADAM_WD = 0.01
ADAM_STEP = 10
VMEM_LIMIT = 56 * 1024 * 1024
BF16_SUBLANES = 16
ADAMW_MAX_ROWS = 176
MESH = pl.DeviceIdType.MESH


def _dot(a, b):
    return jnp.dot(a, b, preferred_element_type=F32)


def _dot_nt(a, b):
    return lax.dot_general(a, b, (((1,), (1,)), ((), ())), preferred_element_type=F32)


def _dot_tn(a, b):
    return lax.dot_general(a, b, (((0,), (0,)), ((), ())), preferred_element_type=F32)


def _sigmoid(v):
    return 1.0 / (1.0 + jnp.exp(-v))


def _ln_fwd(z):
    mu = jnp.mean(z, axis=-1, keepdims=True)
    zc = z - mu
    var = jnp.mean(zc * zc, axis=-1, keepdims=True)
    rstd = lax.rsqrt(var + LN_EPS)
    return zc * rstd, rstd


def _ln_bwd(dout, xh, rstd, g):
    dxh = dout * g
    m1 = jnp.mean(dxh, axis=-1, keepdims=True)
    m2 = jnp.mean(dxh * xh, axis=-1, keepdims=True)
    dz = rstd * (dxh - m1 - xh * m2)
    return dz, jnp.sum(dout * xh, axis=0, keepdims=True), jnp.sum(dout, axis=0, keepdims=True)


def _params(vmem=VMEM_LIMIT):
    return pltpu.CompilerParams(dimension_semantics=("arbitrary",), vmem_limit_bytes=vmem)


def _row(d):
    return pl.BlockSpec((1, d), lambda i: (0, 0))


def _tile(tm, d):
    return pl.BlockSpec((tm, d), lambda i: (i, 0))


def _fixed(r, d):
    return pl.BlockSpec((r, d), lambda i: (0, 0))


def _tile_cur(tm, d, nsteps):
    return pl.BlockSpec((tm, d), lambda i: (jnp.minimum(i, nsteps - 1), 0))


def _tile_prev(tm, d):
    return pl.BlockSpec((tm, d), lambda i: (jnp.maximum(i - 1, 0), 0))


ANY = pl.BlockSpec(memory_space=pl.ANY)


class _Layout:
    GATHER = {"a": ("pw1t", "pw2"),
              "b": ("wq", "wo", "gt0", "ut0", "dn0", "gt1", "ut1", "dn1", "wkv")}

    def __init__(self, d, f):
        self.d, self.f = d, f
        self.n = {"pw1t": 2 * d // N_DEV, "pw2": d // N_DEV, "wq": d // N_DEV, "wo": d // N_DEV,
                  "wkv": (d // N_DEV) * 2 * BLOCK // d}
        for l in range(DEPTH):
            self.n.update({f"gt{l}": f // N_DEV, f"ut{l}": f // N_DEV, f"dn{l}": f // N_DEV})
        self.goff = {}
        for names in self.GATHER.values():
            r = 0
            for name in names:
                self.goff[name] = r
                r += self.n[name]


def _load_weight(wb_ref, lay, name, dst):
    n = lay.n[name]
    for p in range(N_DEV):
        pltpu.sync_copy(wb_ref.at[p, pl.ds(lay.goff[name], n), :], dst.at[pl.ds(p * n, n), :])


def _wscratch(lay, name):
    return pltpu.VMEM((N_DEV * lay.n[name], lay.d), BF16)


def _wfull(ref):
    return ref[...]


def _wrows(ref, r0, nrows):
    return ref[r0:r0 + nrows, :]


def _me():
    return lax.axis_index("x"), lax.axis_index("y"), lax.axis_index("c")


def _peer(mask):
    x, y, c = _me()
    return (1 - x if mask & 4 else x, 1 - y if mask & 2 else y, 1 - c if mask & 1 else c)


def _index(dev):
    return 4 * dev[0] + 2 * dev[1] + dev[2]


class _HostedGather:
    def __init__(self, array):
        self.arrays = [array]
        self.out_shapes = [jax.ShapeDtypeStruct((N_DEV,) + array.shape, array.dtype)]

    def scratch(self):
        return [pltpu.SemaphoreType.DMA((7,)), pltpu.SemaphoreType.DMA((7,)), pltpu.SemaphoreType.DMA(())]

    def _copies(self, ins, outs, send_sems, recv_sems, local_sem):
        out = outs[0]
        x, y, c = _me()
        me, sibling = (x, y, c), (x, y, 1 - c)
        chips = [(1 - x, y), (x, 1 - y), (1 - x, 1 - y)]

        def copy(k, block, to, src=None):
            rows = out.at[_index(block)]
            return pltpu.make_async_remote_copy(
                src_ref=rows if src is None else src, dst_ref=rows, send_sem=send_sems.at[k],
                recv_sem=recv_sems.at[k], device_id=to, device_id_type=MESH)

        return dict(
            mine=lambda: pltpu.make_async_copy(ins[0], out.at[_index(me)], local_sem),
            first=lambda: [copy(0, me, sibling, src=ins[0])] + [copy(1 + j, me, (*chip, c), src=ins[0])
                                                                for j, chip in enumerate(chips)],
            over_ici=lambda: [copy(1 + j, (*chip, c), me) for j, chip in enumerate(chips)],
            passed=lambda: [copy(4 + j, (*chip, c), sibling) for j, chip in enumerate(chips)],
            from_sibling=lambda: [copy(0, sibling, me)] + [copy(4 + j, (*chip, 1 - c), me)
                                                           for j, chip in enumerate(chips)])

    def start(self, *refs):
        cp = self._copies(*refs)
        cp["mine"]().start()
        for c in cp["first"]():
            c.start()

    def middle(self, *refs):
        cp = self._copies(*refs)
        for arrived, onward in zip(cp["over_ici"](), cp["passed"]()):
            arrived.wait_recv()
            onward.start()

    def finish(self, *refs):
        cp = self._copies(*refs)
        for c in cp["from_sibling"]():
            c.wait_recv()
        for c in cp["first"]() + cp["passed"]():
            c.wait_send()
        cp["mine"]().wait()


class _HostedExchange:
    def __init__(self, arrays):
        self.arrays = list(arrays)
        self.out_shapes = [jax.ShapeDtypeStruct(a.shape, a.dtype) for a in self.arrays]

    def scratch(self):
        n = len(self.arrays)
        return [pltpu.SemaphoreType.DMA((n, 7)), pltpu.SemaphoreType.DMA((n, 7)), pltpu.SemaphoreType.DMA((n,))]

    def _copies(self, ins, outs, send_sems, recv_sems, local_sems):
        me = _index(_me())

        def dst(k, src_dev):
            return outs[k].at[src_dev]

        pairs = [(k, mask) for k in range(len(self.arrays)) for mask in range(1, N_DEV)]

        def local():
            return [pltpu.make_async_copy(ins[k].at[me], dst(k, me), local_sems.at[k])
                    for k in range(len(self.arrays))]

        def sends():
            return [pltpu.make_async_remote_copy(
                src_ref=ins[k].at[_index(_peer(mask))], dst_ref=dst(k, me), send_sem=send_sems.at[k, mask - 1],
                recv_sem=recv_sems.at[k, mask - 1], device_id=_peer(mask), device_id_type=MESH)
                for k, mask in pairs]

        def arrivals():
            return [pltpu.make_async_remote_copy(
                src_ref=ins[k].at[me], dst_ref=dst(k, _index(_peer(mask))), send_sem=send_sems.at[k, mask - 1],
                recv_sem=recv_sems.at[k, mask - 1], device_id=_me(), device_id_type=MESH)
                for k, mask in pairs]

        return local, sends, arrivals

    def start(self, *refs):
        local, sends, _ = self._copies(*refs)
        for c in local() + sends():
            c.start()

    def middle(self, *refs):
        pass

    def finish(self, *refs):
        local, sends, arrivals = self._copies(*refs)
        for c in arrivals():
            c.wait_recv()
        for c in sends():
            c.wait_send()
        for c in local():
            c.wait()


HAND_ON_AT = 6


def _gridded_call(body, name, nsteps, in_specs, out_specs, out_shape, scratch, args, hosted=None):
    if hosted is None:
        return pl.pallas_call(body, name=name, grid=(nsteps,), in_specs=in_specs, out_specs=out_specs,
                              out_shape=out_shape, scratch_shapes=scratch, compiler_params=_params())(*args)
    n_in, n_out, n_scr, h_in = len(in_specs), len(out_specs), len(scratch), len(hosted.arrays)
    h_out = len(hosted.out_shapes)

    def with_hosted(*refs):
        a = n_in + h_in
        b = a + n_out
        e = b + h_out + n_scr
        comm = (refs[n_in:a], refs[b:b + h_out], refs[e], refs[e + 1], refs[e + 2])
        i = pl.program_id(0)

        @pl.when(i == 0)
        def _():
            hosted.start(*comm)

        body(*refs[:n_in], *refs[a:b], *refs[b + h_out:e])

        @pl.when(i == HAND_ON_AT * nsteps // 8)
        def _():
            hosted.middle(*comm)

        @pl.when(i == nsteps - 1)
        def _():
            hosted.finish(*comm)

    return pl.pallas_call(
        with_hosted, name=name, grid=(nsteps,), in_specs=list(in_specs) + [ANY] * h_in,
        out_specs=list(out_specs) + [ANY] * h_out, out_shape=list(out_shape) + hosted.out_shapes,
        scratch_shapes=list(scratch) + hosted.scratch(), compiler_params=_params(),
    )(*args, *hosted.arrays)


CONV_RB = 64
CONV_LC = 128
CONV_MC = 256


def _shifted(win, r):
    return win if r == 0 else pltpu.roll(win, win.shape[0] - r, 0)


def _conv_fwd(x, wb, lay, w_dw, b_pw1, b_dw, cg, cb, b_pw2, lg, lb, tm, hosted=None):
    t, d = x.shape
    nsteps = t // tm

    def body(x_ref, xh_ref, wb_ref, wdw_ref, b1_ref, bdw_ref, cg_ref, cb_ref, b2_ref, lg_ref, lb_ref,
             xb_ref, ag_ref, xhc_ref, rsc_ref, xh1_ref, rs1_ref, w1_s, w2_s, ubuf, cv_s):
        i = pl.program_id(0)

        @pl.when(i == 0)
        def _():
            _load_weight(wb_ref, lay, "pw1t", w1_s)
            _load_weight(wb_ref, lay, "pw2", w2_s)

        xv = x_ref[...]
        xb = xv.astype(BF16)
        xb_ref[...] = xb
        xcat = jnp.concatenate([xh_ref[...].astype(BF16), xb], axis=0)
        for mc in range(d // CONV_MC):
            c0 = mc * CONV_MC
            acols, gcols = slice(c0, c0 + CONV_MC), slice(d + c0, d + c0 + CONV_MC)
            ha = _dot_nt(xcat, _wrows(w1_s, c0, CONV_MC)) + b1_ref[:, acols]
            hg = _dot_nt(xcat, _wrows(w1_s, d + c0, CONV_MC)) + b1_ref[:, gcols]
            ag_ref[:, acols] = ha[HALO:].astype(BF16)
            ag_ref[:, gcols] = hg[HALO:].astype(BF16)
            u = ha * _sigmoid(hg)
            u = jnp.concatenate([jnp.where(i > 0, u[:HALO], 0.0), u[HALO:], jnp.zeros((8, CONV_MC), F32)], axis=0)
            for r in range(8):
                ubuf[r, :, acols] = _shifted(u, r)
            for rb in range(tm // CONV_RB):
                t0 = rb * CONV_RB
                for lc in range(CONV_MC // CONV_LC):
                    lanes = slice(c0 + lc * CONV_LC, c0 + (lc + 1) * CONV_LC)
                    acc = jnp.zeros((CONV_RB, CONV_LC), F32)
                    for k in range(CONV_WIDTH):
                        s = HALO - (CONV_WIDTH - 1) + k
                        q = t0 + 8 * (s // 8)
                        acc = acc + ubuf[s % 8, q:q + CONV_RB, lanes] * wdw_ref[k:k + 1, lanes]
                    cv_s[t0:t0 + CONV_RB, lanes] = acc
        cv = cv_s[...] + bdw_ref[...]
        xhc, rsc = _ln_fwd(cv)
        xhc_ref[...] = xhc
        rsc_ref[...] = rsc
        n = xhc * cg_ref[...] + cb_ref[...]
        s_act = n * _sigmoid(n)
        m = _dot(s_act.astype(BF16), _wfull(w2_s)) + b2_ref[...]
        xh1, rs1 = _ln_fwd(ALPHA * xv + m)
        xh1_ref[...] = xh1
        rs1_ref[...] = rs1

    hb = tm // HALO
    return _gridded_call(
        body, "conv_fwd", nsteps,
        [_tile(tm, d), pl.BlockSpec((HALO, d), lambda i: (jnp.maximum(i * hb - 1, 0), 0)), ANY,
         _fixed(HALO, d), _row(2 * d), _row(d), _row(d), _row(d), _row(d), _row(d), _row(d)],
        [_tile(tm, d), _tile(tm, 2 * d), _tile(tm, d), _tile(tm, 1), _tile(tm, d), _tile(tm, 1)],
        [jax.ShapeDtypeStruct((t, d), BF16), jax.ShapeDtypeStruct((t, 2 * d), BF16),
         jax.ShapeDtypeStruct((t, d), F32), jax.ShapeDtypeStruct((t, 1), F32),
         jax.ShapeDtypeStruct((t, d), F32), jax.ShapeDtypeStruct((t, 1), F32)],
        [_wscratch(lay, "pw1t"), _wscratch(lay, "pw2"),
         pltpu.VMEM((8, HALO + tm + 8, d), F32), pltpu.VMEM((tm, d), F32)],
        (x, x, wb, w_dw, b_pw1, b_dw, cg, cb, b_pw2, lg, lb), hosted)


def _conv_bwd1(dz1, xhc, rsc, wb, lay, cg, cb, tm):
    t, d = dz1.shape

    def body(dz_ref, xhc_ref, rsc_ref, wb_ref, cg_ref, cb_ref, dzb_ref, s_ref, dcv_ref, st_ref, w2_s):
        i = pl.program_id(0)

        @pl.when(i == 0)
        def _():
            _load_weight(wb_ref, lay, "pw2", w2_s)
            st_ref[...] = jnp.zeros(st_ref.shape, F32)

        dz = dz_ref[...]
        dzb = dz.astype(BF16)
        dzb_ref[...] = dzb
        xhc_v = xhc_ref[...]
        n = xhc_v * cg_ref[...] + cb_ref[...]
        sg = _sigmoid(n)
        s_ref[...] = (n * sg).astype(BF16)
        ds = _dot_nt(dzb, _wfull(w2_s))
        dn = ds * (sg * (1.0 + n * (1.0 - sg)))
        dcv, dg, db = _ln_bwd(dn, xhc_v, rsc_ref[...], cg_ref[...])
        dcv_ref[...] = dcv
        st_ref[0:1, :] += dg
        st_ref[1:2, :] += db
        st_ref[2:3, :] += jnp.sum(dcv, axis=0, keepdims=True)
        st_ref[3:4, :] += jnp.sum(dz, axis=0, keepdims=True)

    return pl.pallas_call(
        body, name="conv_bwd1", grid=(t // tm,),
        in_specs=[_tile(tm, d), _tile(tm, d), _tile(tm, 1), ANY, _row(d), _row(d)],
        out_specs=[_tile(tm, d), _tile(tm, d), _tile(tm, d), _fixed(8, d)],
        out_shape=[jax.ShapeDtypeStruct((t, d), BF16), jax.ShapeDtypeStruct((t, d), BF16),
                   jax.ShapeDtypeStruct((t, d), F32), jax.ShapeDtypeStruct((8, d), F32)],
        scratch_shapes=[_wscratch(lay, "pw2")],
        compiler_params=_params(),
    )(dz1, xhc, rsc, wb, cg, cb)


def _conv_bwd2(dz1, dcv, ag, wb, lay, w_dw, tm, hosted=None):
    t, d = dz1.shape
    nsteps = t // tm

    def body(dz_ref, dcv_ref, dcvn_ref, ag_ref, wb_ref, wdw_ref,
             gx_ref, dh_ref, dw_ref, db1_ref, w1_s, ubuf, dbuf, du_s, dwacc):
        i = pl.program_id(0)

        @pl.when(i == 0)
        def _():
            _load_weight(wb_ref, lay, "pw1t", w1_s)
            dwacc[...] = jnp.zeros(dwacc.shape, F32)
            db1_ref[...] = jnp.zeros(db1_ref.shape, F32)

        gx = ALPHA * dz_ref[...]
        for mc in range(d // CONV_MC):
            c0 = mc * CONV_MC
            acols, gcols = slice(c0, c0 + CONV_MC), slice(d + c0, d + c0 + CONV_MC)
            a = ag_ref[:, acols].astype(F32)
            sg = _sigmoid(ag_ref[:, gcols].astype(F32))
            ubuf[:, acols] = a * sg
            dcv_next = jnp.where(i < nsteps - 1, dcvn_ref[:, acols], 0.0)
            dcv_c = jnp.concatenate([dcv_ref[:, acols], dcv_next, jnp.zeros((8, CONV_MC), F32)], axis=0)
            for r in range(8):
                dbuf[r, :, acols] = _shifted(dcv_c, r)
            for rb in range(tm // CONV_RB):
                t0 = rb * CONV_RB
                for lc in range(CONV_MC // CONV_LC):
                    lanes = slice(c0 + lc * CONV_LC, c0 + (lc + 1) * CONV_LC)
                    ucur = ubuf[t0:t0 + CONV_RB, lanes]
                    acc = jnp.zeros((CONV_RB, CONV_LC), F32)
                    for k in range(CONV_WIDTH):
                        sd = CONV_WIDTH - 1 - k
                        q = t0 + 8 * (sd // 8)
                        dk = dbuf[sd % 8, q:q + CONV_RB, lanes]
                        acc = acc + dk * wdw_ref[k:k + 1, lanes]
                        prod = ucur * dk
                        part = prod[0:8]
                        for j in range(1, CONV_RB // 8):
                            part = part + prod[8 * j:8 * j + 8]
                        dwacc[k, :, lanes] += part
                    du_s[t0:t0 + CONV_RB, lanes] = acc
            du = du_s[:, acols]
            da = du * sg
            dg = du * a * sg * (1.0 - sg)
            dab, dgb = da.astype(BF16), dg.astype(BF16)
            dh_ref[:, acols] = dab
            dh_ref[:, gcols] = dgb
            db1_ref[:, acols] += jnp.sum(da, axis=0, keepdims=True)
            db1_ref[:, gcols] += jnp.sum(dg, axis=0, keepdims=True)
            gx = gx + _dot(dab, _wrows(w1_s, c0, CONV_MC)) + _dot(dgb, _wrows(w1_s, d + c0, CONV_MC))
        gx_ref[...] = gx

        @pl.when(i == nsteps - 1)
        def _():
            dw_ref[...] = jnp.sum(dwacc[...], axis=1)

    hb = tm // HALO
    last = t // HALO - 1
    return _gridded_call(
        body, "conv_bwd2", nsteps,
        [_tile(tm, d), _tile(tm, d),
         pl.BlockSpec((HALO, d), lambda i: (jnp.minimum((i + 1) * hb, last), 0)),
         _tile(tm, 2 * d), ANY, _fixed(HALO, d)],
        [_tile(tm, d), _tile(tm, 2 * d), _fixed(HALO, d), _row(2 * d)],
        [jax.ShapeDtypeStruct((t, d), F32), jax.ShapeDtypeStruct((t, 2 * d), BF16),
         jax.ShapeDtypeStruct((HALO, d), F32), jax.ShapeDtypeStruct((1, 2 * d), F32)],
        [_wscratch(lay, "pw1t"), pltpu.VMEM((tm, d), F32),
         pltpu.VMEM((8, HALO + tm + 8, d), F32), pltpu.VMEM((tm, d), F32),
         pltpu.VMEM((HALO, 8, d), F32)],
        (dz1, dcv, dcv, ag, wb, w_dw), hosted)


FFN_FC = 256
FFN_AHEAD = 1


def _ffn_fwd(xh_in, g_in, b_in, wb, lay, layer, tm, *, kv=None, loss=None):
    t, d = xh_in.shape
    f = lay.f
    names = (f"gt{layer}", f"ut{layer}", f"dn{layer}")

    def body(*refs):
        xh_ref, gi_ref, bi_ref, wb_ref = refs[:4]
        pos = 4
        if kv is not None:
            go_ref, bo_ref, wkv_ref, bkv_ref = refs[pos:pos + 4]
            pos += 4
        if loss is not None:
            go_ref, bo_ref, tgt_ref = refs[pos:pos + 3]
            pos += 3
        xb_ref, hg_ref, hu_ref = refs[pos:pos + 3]
        pos += 3
        if kv is not None:
            xho_ref, rso_ref, xob_ref, kv_ref = refs[pos:pos + 4]
            pos += 4
        if loss is not None:
            dz_ref, st_ref, loss_ref = refs[pos:pos + 3]
            pos += 3
        gt_s, ut_s, dn_s, xin_s, fo_s = refs[pos:pos + 5]
        i = pl.program_id(0)

        @pl.when(i == 0)
        def _():
            for name, dst in zip(names, (gt_s, ut_s, dn_s)):
                _load_weight(wb_ref, lay, name, dst)
            xin_s[...] = jnp.zeros(xin_s.shape, F32)
            fo_s[...] = jnp.zeros(fo_s.shape, F32)
            if loss is not None:
                st_ref[...] = jnp.zeros(st_ref.shape, F32)
                loss_ref[...] = jnp.zeros(loss_ref.shape, F32)

        xin_prev = xin_s[...]
        xho, rso = _ln_fwd(ALPHA * xin_prev + fo_s[...])
        if kv is not None:
            xho_ref[...] = xho
            rso_ref[...] = rso
            xob_ref[...] = (xho * go_ref[...] + bo_ref[...]).astype(BF16)
        if loss is not None:
            real = i > 0
            diff = xho * go_ref[...] + bo_ref[...] - tgt_ref[...]
            loss_ref[...] += jnp.where(real, (0.5 / d) * jnp.sum(diff * diff), 0.0)
            dz, dg, db = _ln_bwd(diff * (1.0 / d), xho, rso, go_ref[...])
            dz_ref[...] = dz
            st_ref[0:1, :] += jnp.where(real, dg, 0.0)
            st_ref[1:2, :] += jnp.where(real, db, 0.0)

        xin = xh_ref[...] * gi_ref[...] + bi_ref[...]
        xb = xin.astype(BF16)
        xb_ref[...] = xb

        def up(c):
            return (_dot_nt(xb, _wrows(gt_s, c * FFN_FC, FFN_FC)), _dot_nt(xb, _wrows(ut_s, c * FFN_FC, FFN_FC)))

        fo = jnp.zeros((tm, d), F32)
        nc = f // FFN_FC
        ahead = [up(c) for c in range(min(FFN_AHEAD, nc))]
        for c in range(nc):
            rows = slice(c * FFN_FC, (c + 1) * FFN_FC)
            hg, hu = ahead.pop(0)
            if c + FFN_AHEAD < nc:
                ahead.append(up(c + FFN_AHEAD))
            hg_ref[:, rows] = hg.astype(BF16)
            hu_ref[:, rows] = hu.astype(BF16)
            act = hg * _sigmoid(hg) * hu
            fo = fo + _dot(act.astype(BF16), _wrows(dn_s, c * FFN_FC, FFN_FC))
        xin_s[...] = xin
        fo_s[...] = fo
        if kv is not None:
            kv_ref[...] = (_dot(xob_ref[...], wkv_ref[...]) + bkv_ref[...]).astype(BF16)

    nsteps = t // tm
    in_specs = [_tile_cur(tm, d, nsteps), _row(d), _row(d), ANY]
    args = [xh_in, g_in, b_in, wb]
    out_specs = [_tile_cur(tm, d, nsteps), _tile_cur(tm, f, nsteps), _tile_cur(tm, f, nsteps)]
    out_shape = [jax.ShapeDtypeStruct((t, d), BF16), jax.ShapeDtypeStruct((t, f), BF16),
                 jax.ShapeDtypeStruct((t, f), BF16)]
    if kv is not None:
        in_specs += [_row(d), _row(d), _fixed(d, 2 * BLOCK), _row(2 * BLOCK)]
        args += list(kv)
        out_specs += [_tile_prev(tm, d), _tile_prev(tm, 1), _tile_prev(tm, d), _tile_prev(tm, 2 * BLOCK)]
        out_shape += [jax.ShapeDtypeStruct((t, d), F32), jax.ShapeDtypeStruct((t, 1), F32),
                      jax.ShapeDtypeStruct((t, d), BF16), jax.ShapeDtypeStruct((t, 2 * BLOCK), BF16)]
    if loss is not None:
        in_specs += [_row(d), _row(d), _tile_prev(tm, d)]
        args += list(loss)
        out_specs += [_tile_prev(tm, d), _fixed(8, d), _fixed(8, 128)]
        out_shape += [jax.ShapeDtypeStruct((t, d), F32), jax.ShapeDtypeStruct((8, d), F32),
                      jax.ShapeDtypeStruct((8, 128), F32)]
    return pl.pallas_call(
        body, name=f"ffn_fwd{layer}", grid=(nsteps + 1,), in_specs=in_specs, out_specs=out_specs,
        out_shape=out_shape,
        scratch_shapes=[_wscratch(lay, n) for n in names] + [pltpu.VMEM((tm, d), F32), pltpu.VMEM((tm, d), F32)],
        compiler_params=_params(),
    )(*args)


def _ffn_bwd(dz, hg, hu, xh_in, rs_in, g_in, wb, lay, layer, tm, hosted=None, qkv=None):
    t, d = dz.shape
    f = lay.f
    nsteps = t // tm
    nbt = tm // BLOCK
    names = (f"gt{layer}", f"ut{layer}", f"dn{layer}")

    def body(*refs):
        dz_ref, hg_ref, hu_ref, xh_ref, rs_ref, gi_ref, wb_ref = refs[:7]
        pos = 7
        if qkv is not None:
            dq_ref, dkc_ref, dkp_ref, dkn_ref, xho_ref, rso_ref, go_ref, wkv_ref = refs[pos:pos + 8]
            pos += 8
        dzb_ref, act_ref, dhg_ref, dhu_ref, dzp_ref, st_ref = refs[pos:pos + 6]
        pos += 6
        if qkv is not None:
            dkv_ref, sto_ref, dbkv_ref = refs[pos:pos + 3]
            pos += 3
        gt_s, ut_s, dn_s = refs[pos:pos + 3]
        i = pl.program_id(0)

        @pl.when(i == 0)
        def _():
            for name, dst in zip(names, (gt_s, ut_s, dn_s)):
                _load_weight(wb_ref, lay, name, dst)
            st_ref[...] = jnp.zeros(st_ref.shape, F32)
            if qkv is not None:
                _load_weight(wb_ref, lay, "wq", refs[pos + 3])
                sto_ref[...] = jnp.zeros(sto_ref.shape, F32)
                dbkv_ref[...] = jnp.zeros(dbkv_ref.shape, F32)

        if qkv is None:
            dzv = dz_ref[...]
        else:
            nxt = jnp.where(i < nsteps - 1, dkn_ref[...], 0.0)
            shifted = jnp.concatenate([dkp_ref[pl.ds(BLOCK, tm - BLOCK), :], nxt], axis=0) if nbt > 1 else nxt
            dkv = dkc_ref[...] + shifted
            dkvb = dkv.astype(BF16)
            dkv_ref[...] = dkvb
            dbkv_ref[...] += jnp.sum(dkv, axis=0, keepdims=True)
            dxo = (ALPHA * dz_ref[...] + _dot_nt(dq_ref[...], refs[pos + 3][...])
                   + _dot_nt(dkvb, wkv_ref[...]))
            dzv, dgo, dbo = _ln_bwd(dxo, xho_ref[...], rso_ref[...], go_ref[...])
            sto_ref[0:1, :] += dgo
            sto_ref[1:2, :] += dbo
        dzb = dzv.astype(BF16)
        dzb_ref[...] = dzb
        dx = ALPHA * dzv
        def back(c):
            return _dot_nt(dzb, _wrows(dn_s, c * FFN_FC, FFN_FC))

        nc = f // FFN_FC
        ahead = [back(c) for c in range(min(FFN_AHEAD, nc))]
        for c in range(nc):
            rows = slice(c * FFN_FC, (c + 1) * FFN_FC)
            dact = ahead.pop(0)
            if c + FFN_AHEAD < nc:
                ahead.append(back(c + FFN_AHEAD))
            hg_v = hg_ref[:, rows].astype(F32)
            hu_v = hu_ref[:, rows].astype(F32)
            sg = _sigmoid(hg_v)
            silu = hg_v * sg
            act_ref[:, rows] = (silu * hu_v).astype(BF16)
            dhu = (dact * silu).astype(BF16)
            dhg = (dact * hu_v * (sg * (1.0 + hg_v * (1.0 - sg)))).astype(BF16)
            dhu_ref[:, rows] = dhu
            dhg_ref[:, rows] = dhg
            dx = (dx + _dot(dhg, _wrows(gt_s, c * FFN_FC, FFN_FC))
                  + _dot(dhu, _wrows(ut_s, c * FFN_FC, FFN_FC)))
        dzp, dg, db = _ln_bwd(dx, xh_ref[...], rs_ref[...], gi_ref[...])
        dzp_ref[...] = dzp
        st_ref[0:1, :] += dg
        st_ref[1:2, :] += db

    in_specs = [_tile(tm, d), _tile(tm, f), _tile(tm, f), _tile(tm, d), _tile(tm, 1), _row(d), ANY]
    args = [dz, hg, hu, xh_in, rs_in, g_in, wb]
    out_specs = [_tile(tm, d), _tile(tm, f), _tile(tm, f), _tile(tm, f), _tile(tm, d), _fixed(8, d)]
    out_shape = [jax.ShapeDtypeStruct((t, d), BF16), jax.ShapeDtypeStruct((t, f), BF16),
                 jax.ShapeDtypeStruct((t, f), BF16), jax.ShapeDtypeStruct((t, f), BF16),
                 jax.ShapeDtypeStruct((t, d), F32), jax.ShapeDtypeStruct((8, d), F32)]
    scratch = [_wscratch(lay, n) for n in names]
    if qkv is not None:
        dq, dkc, dkp, xh_out, rs_out, g_out, wkv = qkv
        last = t // BLOCK - 1
        in_specs += [_tile(tm, d), _tile(tm, 2 * BLOCK), _tile(tm, 2 * BLOCK),
                     pl.BlockSpec((BLOCK, 2 * BLOCK), lambda i: (jnp.minimum((i + 1) * nbt, last), 0)),
                     _tile(tm, d), _tile(tm, 1), _row(d), _fixed(d, 2 * BLOCK)]
        args += [dq, dkc, dkp, dkp, xh_out, rs_out, g_out, wkv]
        out_specs += [_tile(tm, 2 * BLOCK), _fixed(8, d), _row(2 * BLOCK)]
        out_shape += [jax.ShapeDtypeStruct((t, 2 * BLOCK), BF16), jax.ShapeDtypeStruct((8, d), F32),
                      jax.ShapeDtypeStruct((1, 2 * BLOCK), F32)]
        scratch.append(_wscratch(lay, "wq"))
    return _gridded_call(body, f"ffn_bwd{layer}", nsteps, in_specs, out_specs, out_shape, scratch, args, hosted)


ATTN_GROUP = 2


def _alibi_slope(h, nq):
    return 2.0 ** (-ALIBI_MAX * (h + 1) / nq)


def _fill_alibi_bias(bias_s, nq, keys_on_rows=False):
    shape = (2 * BLOCK, BLOCK) if keys_on_rows else (BLOCK, 2 * BLOCK)
    qi = lax.broadcasted_iota(jnp.int32, shape, 1 if keys_on_rows else 0)
    kj = lax.broadcasted_iota(jnp.int32, shape, 0 if keys_on_rows else 1)
    delta = qi + BLOCK - kj
    valid = (delta >= 0) & (delta < BLOCK)
    dist = jnp.where(valid, delta.astype(F32), MASKED_DIST)
    dist_first = jnp.where(kj >= BLOCK, dist, MASKED_DIST)
    for h in range(nq):
        bias_s[0, h] = _alibi_slope(h, nq) * dist
        bias_s[1, h] = _alibi_slope(h, nq) * dist_first


def _padded_kv(kvb, kvh, transposed_v=False):
    lane = lax.broadcasted_iota(jnp.int32, (2 * BLOCK, BLOCK), 1)
    mine = (lane < HEAD_DIM) if kvh == 0 else (lane >= HEAD_DIM)
    out = []
    for sec, transposed in ((kvb[:, :BLOCK], False), (kvb[:, BLOCK:], transposed_v)):
        m = jnp.where(mine, sec.astype(F32), 0.0)
        sw = pltpu.roll(m, HEAD_DIM, 1)
        pair = (m, sw) if kvh == 0 else (sw, m)
        out.append(tuple((p.T if transposed else p).astype(BF16) for p in pair))
    return out


def _attn_fwd(xh_in, g_in, b_in, x_in_b, kvs, wb, lay, bq, sinks, bo, tm):
    t, d = xh_in.shape
    nq = d // HEAD_DIM
    pairs_per_kv = (d // BLOCK) // N_KV_HEADS
    nbt = tm // BLOCK
    scale = HEAD_DIM ** -0.5

    def body(xh_ref, gi_ref, bi_ref, xb_ref, kv_ref, kvp_ref, wb_ref, bq_ref, sk_ref, bo_ref,
             q_ref, o_ref, lse_ref, xho_ref, rso_ref, wq_s, wo_s, kvall, q_s, o_s, bias_s):
        i = pl.program_id(0)

        @pl.when(i == 0)
        def _():
            _load_weight(wb_ref, lay, "wq", wq_s)
            _load_weight(wb_ref, lay, "wo", wo_s)
            _fill_alibi_bias(bias_s, nq, keys_on_rows=True)

        qv = ((_dot(xb_ref[...], _wfull(wq_s)) + bq_ref[...]) * scale).astype(BF16)
        q_s[...] = qv
        q_ref[...] = qv
        kvall[pl.ds(0, BLOCK), :] = kvp_ref[...]
        kvall[pl.ds(BLOCK, tm), :] = kv_ref[...]
        head_row = lax.broadcasted_iota(jnp.int32, (BLOCK, BLOCK), 0)

        def score_phase(j):
            rows = slice(j * BLOCK, (j + 1) * BLOCK)
            kvb = kvall[j * BLOCK:(j + 2) * BLOCK, :]
            first = (i * nbt + j == 0).astype(jnp.int32)
            pads = [_padded_kv(kvb, kvh, transposed_v=True) for kvh in range(N_KV_HEADS)]
            scores = []
            for a in range(d // BLOCK):
                kpad = pads[a // pairs_per_kv][0]
                qp = q_s[rows, a * BLOCK:(a + 1) * BLOCK]
                for e in range(2):
                    scores.append(_dot_nt(kpad[e], qp) - bias_s[first, 2 * a + e])
            return rows, pads, scores

        def softmax_phase(state):
            rows, pads, scores = state
            probs, inv = [], []
            lse_t = jnp.zeros((BLOCK, BLOCK), F32)
            for h in range(nq):
                sink = sk_ref[:, h:h + 1]
                m = jnp.maximum(jnp.max(scores[h], axis=0, keepdims=True), sink)
                p = jnp.exp(scores[h] - m)
                l = jnp.sum(p, axis=0, keepdims=True) + jnp.exp(sink - m)
                lse_t = jnp.where(head_row == h, m + jnp.log(l), lse_t)
                probs.append(p.astype(BF16))
                inv.append(1.0 / l)
            lse_ref[rows, :] = lse_t.T
            return rows, pads, probs, inv

        def value_phase(state):
            rows, pads, probs, inv = state
            for a in range(d // BLOCK):
                vpad_t = pads[a // pairs_per_kv][1]
                opair_t = (_dot(vpad_t[0], probs[2 * a]) * inv[2 * a]
                           + _dot(vpad_t[1], probs[2 * a + 1]) * inv[2 * a + 1])
                o_s[rows, a * BLOCK:(a + 1) * BLOCK] = opair_t.T.astype(BF16)

        for j0 in range(0, nbt, ATTN_GROUP):
            group = range(j0, min(j0 + ATTN_GROUP, nbt))
            for state in [softmax_phase(s) for s in [score_phase(j) for j in group]]:
                value_phase(state)
        ov = o_s[...]
        o_ref[...] = ov
        xin = xh_ref[...] * gi_ref[...] + bi_ref[...]
        xho, rso = _ln_fwd(ALPHA * xin + _dot(ov, _wfull(wo_s)) + bo_ref[...])
        xho_ref[...] = xho
        rso_ref[...] = rso

    return pl.pallas_call(
        body, name="attn_fwd", grid=(t // tm,),
        in_specs=[_tile(tm, d), _row(d), _row(d), _tile(tm, d), _tile(tm, 2 * BLOCK),
                  pl.BlockSpec((BLOCK, 2 * BLOCK), lambda i: (jnp.maximum(i * nbt - 1, 0), 0)),
                  ANY, _row(d), _row(nq), _row(d)],
        out_specs=[_tile(tm, d), _tile(tm, d), _tile(tm, BLOCK), _tile(tm, d), _tile(tm, 1)],
        out_shape=[jax.ShapeDtypeStruct((t, d), BF16), jax.ShapeDtypeStruct((t, d), BF16),
                   jax.ShapeDtypeStruct((t, BLOCK), F32), jax.ShapeDtypeStruct((t, d), F32),
                   jax.ShapeDtypeStruct((t, 1), F32)],
        scratch_shapes=[_wscratch(lay, "wq"), _wscratch(lay, "wo"),
                        pltpu.VMEM((BLOCK + tm, 2 * BLOCK), BF16), pltpu.VMEM((tm, d), BF16),
                        pltpu.VMEM((tm, d), BF16), pltpu.VMEM((2, nq, 2 * BLOCK, BLOCK), F32)],
        compiler_params=_params(),
    )(xh_in, g_in, b_in, x_in_b, kvs, kvs, wb, bq, sinks, bo)


def _attn_bwd(dz, q, o, lse, kvs, wb, lay, sinks, tm, hosted=None):
    t, d = dz.shape
    nq = d // HEAD_DIM
    pairs_per_kv = (d // BLOCK) // N_KV_HEADS
    nbt = tm // BLOCK
    scale = HEAD_DIM ** -0.5

    def body(dz_ref, q_ref, o_ref, lse_ref, kv_ref, kvp_ref, wb_ref, sk_ref,
             dzb_ref, dq_ref, dkc_ref, dkp_ref, st_ref, dsk_ref, wo_s, kvall, do_s, dq_s, bias_s):
        i = pl.program_id(0)

        @pl.when(i == 0)
        def _():
            _load_weight(wb_ref, lay, "wo", wo_s)
            _fill_alibi_bias(bias_s, nq, keys_on_rows=True)
            st_ref[...] = jnp.zeros(st_ref.shape, F32)
            dsk_ref[...] = jnp.zeros(dsk_ref.shape, F32)

        dzv = dz_ref[...]
        dzb = dzv.astype(BF16)
        dzb_ref[...] = dzb
        do_s[...] = _dot_nt(dzb, _wfull(wo_s))
        kvall[pl.ds(0, BLOCK), :] = kvp_ref[...]
        kvall[pl.ds(BLOCK, tm), :] = kv_ref[...]
        lane = lax.broadcasted_iota(jnp.int32, (BLOCK, BLOCK), 1)
        lane1 = lax.broadcasted_iota(jnp.int32, (1, BLOCK), 1)
        lane2 = lax.broadcasted_iota(jnp.int32, (2 * BLOCK, BLOCK), 1)
        halves = (lane < HEAD_DIM, lane >= HEAD_DIM)
        sel_row = lax.broadcasted_iota(jnp.int32, (8, BLOCK), 0)
        sel_lane = lax.broadcasted_iota(jnp.int32, (8, BLOCK), 1)
        head_sel = jnp.where((sel_row == 0) & (sel_lane < HEAD_DIM) | (sel_row == 1) & (sel_lane >= HEAD_DIM),
                             1.0, 0.0).astype(BF16)

        def score_phase(j):
            rows = slice(j * BLOCK, (j + 1) * BLOCK)
            kvb = kvall[j * BLOCK:(j + 2) * BLOCK, :]
            first = (i * nbt + j == 0).astype(jnp.int32)
            pads = [_padded_kv(kvb, kvh) for kvh in range(N_KV_HEADS)]
            scores, dps, dhs, qms, doms = [], [], [], [], []
            for a in range(d // BLOCK):
                kpad, vpad = pads[a // pairs_per_kv]
                cols = slice(a * BLOCK, (a + 1) * BLOCK)
                qp = q_ref[rows, cols]
                dop = do_s[rows, cols]
                dopb = dop.astype(BF16)
                prod = dop * o_ref[rows, cols].astype(F32)
                hi = prod.astype(BF16)
                lo = (prod - hi.astype(F32)).astype(BF16)
                dh_pair = _dot_nt(head_sel, hi) + _dot_nt(head_sel, lo)
                for e in range(2):
                    scores.append(_dot_nt(kpad[e], qp) - bias_s[first, 2 * a + e])
                    dps.append(_dot_nt(vpad[e], dopb))
                    dhs.append(dh_pair[e:e + 1, :])
                    qms.append(jnp.where(halves[e], qp, jnp.zeros_like(qp)))
                    doms.append(jnp.where(halves[e], dopb, jnp.zeros_like(dopb)))
            return rows, pads, scores, dps, dhs, qms, doms

        def softmax_phase(state):
            rows, pads, scores, dps, dhs, qms, doms = state
            dss, pbs = [], []
            dsk_t = jnp.zeros((1, BLOCK), F32)
            lse_t = lse_ref[rows, :].T
            for h in range(nq):
                lse_h = lse_t[h:h + 1, :]
                p = jnp.exp(scores[h] - lse_h)
                dss.append((p * (dps[h] - dhs[h])).astype(BF16))
                pbs.append(p.astype(BF16))
                dsink = -jnp.sum(jnp.exp(sk_ref[:, h:h + 1] - lse_h) * dhs[h], axis=1, keepdims=True)
                dsk_t = jnp.where(lane1 == h, dsink, dsk_t)
            dsk_ref[...] += dsk_t
            return rows, pads, dss, pbs, qms, doms

        def grad_phase(state):
            rows, pads, dss, pbs, qms, doms = state
            dsecs = []
            for kvh in range(N_KV_HEADS):
                kpad_t = [p.astype(F32).T.astype(BF16) for p in pads[kvh][0]]
                dk_acc = jnp.zeros((2 * BLOCK, BLOCK), F32)
                dv_acc = jnp.zeros((2 * BLOCK, BLOCK), F32)
                for a in range(kvh * pairs_per_kv, (kvh + 1) * pairs_per_kv):
                    dqp_t = _dot(kpad_t[0], dss[2 * a]) + _dot(kpad_t[1], dss[2 * a + 1])
                    dq_s[rows, a * BLOCK:(a + 1) * BLOCK] = dqp_t.T * scale
                    for e in range(2):
                        h = 2 * a + e
                        dk_acc = dk_acc + _dot(dss[h], qms[h])
                        dv_acc = dv_acc + _dot(pbs[h], doms[h])
                dsecs.append((dk_acc + pltpu.roll(dk_acc, HEAD_DIM, 1), dv_acc + pltpu.roll(dv_acc, HEAD_DIM, 1)))
            lo = lane2 < HEAD_DIM
            dkv = jnp.concatenate([jnp.where(lo, dsecs[0][0], dsecs[1][0]),
                                   jnp.where(lo, dsecs[0][1], dsecs[1][1])], axis=1)
            dkp_ref[rows, :] = dkv[:BLOCK]
            dkc_ref[rows, :] = dkv[BLOCK:]

        for j0 in range(0, nbt, ATTN_GROUP):
            group = range(j0, min(j0 + ATTN_GROUP, nbt))
            for state in [softmax_phase(s) for s in [score_phase(j) for j in group]]:
                grad_phase(state)
        dqv = dq_s[...]
        dq_ref[...] = dqv.astype(BF16)
        st_ref[0:1, :] += jnp.sum(dqv, axis=0, keepdims=True)
        st_ref[1:2, :] += jnp.sum(dzv, axis=0, keepdims=True)

    return _gridded_call(
        body, "attn_bwd", t // tm,
        [_tile(tm, d), _tile(tm, d), _tile(tm, d), _tile(tm, BLOCK), _tile(tm, 2 * BLOCK),
         pl.BlockSpec((BLOCK, 2 * BLOCK), lambda i: (jnp.maximum(i * nbt - 1, 0), 0)),
         ANY, _row(nq)],
        [_tile(tm, d), _tile(tm, d), _tile(tm, 2 * BLOCK), _tile(tm, 2 * BLOCK),
         _fixed(8, d), _row(BLOCK)],
        [jax.ShapeDtypeStruct((t, d), BF16), jax.ShapeDtypeStruct((t, d), BF16),
         jax.ShapeDtypeStruct((t, 2 * BLOCK), F32), jax.ShapeDtypeStruct((t, 2 * BLOCK), F32),
         jax.ShapeDtypeStruct((8, d), F32), jax.ShapeDtypeStruct((1, BLOCK), F32)],
        [_wscratch(lay, "wo"), pltpu.VMEM((BLOCK + tm, 2 * BLOCK), BF16),
         pltpu.VMEM((tm, d), F32), pltpu.VMEM((tm, d), F32),
         pltpu.VMEM((2, nq, 2 * BLOCK, BLOCK), F32)],
        (dz, q, o, lse, kvs, kvs, wb, sinks), hosted)


def _tn_matmul(a, b, name, bm, tk):
    t, m = a.shape
    n = b.shape[1]
    ksteps = t // tk

    nc = max(n // 256, 1)
    cw = n // nc

    def body(a_ref, b_ref, o_ref, acc):
        k = pl.program_id(1)

        @pl.when(k == 0)
        def _():
            acc[...] = jnp.zeros(acc.shape, F32)

        at = a_ref[...].T
        for c in range(nc):
            cols = slice(c * cw, (c + 1) * cw)
            acc[:, cols] += _dot(at, b_ref[:, cols])

        @pl.when(k == ksteps - 1)
        def _():
            o_ref[...] = acc[...].astype(BF16)

    return pl.pallas_call(
        body, name=name, grid=(m // bm, ksteps),
        in_specs=[pl.BlockSpec((tk, bm), lambda j, k: (k, j)), pl.BlockSpec((tk, n), lambda j, k: (k, 0))],
        out_specs=pl.BlockSpec((bm, n), lambda j, k: (j, 0)),
        out_shape=jax.ShapeDtypeStruct((m, n), BF16),
        scratch_shapes=[pltpu.VMEM((bm, n), F32)],
        compiler_params=pltpu.CompilerParams(dimension_semantics=("arbitrary", "arbitrary"),
                                             vmem_limit_bytes=VMEM_LIMIT),
    )(a, b)


def _all_gather(arrays, name):
    n = len(arrays)

    def body(*refs):
        ins, outs = refs[:n], refs[n:2 * n]
        send_sems, recv_sems, local_sems = refs[2 * n:]
        x, y, c = _me()
        me, sibling = (x, y, c), (x, y, 1 - c)
        chips = [(1 - x, y), (x, 1 - y), (1 - x, 1 - y)]

        def slot(ref, dev):
            return ref.at[4 * dev[0] + 2 * dev[1] + dev[2]]

        def copy(a, k, block, to, src=None):
            return pltpu.make_async_remote_copy(
                src_ref=slot(outs[a], block) if src is None else src, dst_ref=slot(outs[a], block),
                send_sem=send_sems.at[a, k], recv_sem=recv_sems.at[a, k], device_id=to, device_id_type=MESH)

        mine = [pltpu.make_async_copy(ins[a], slot(outs[a], me), local_sems.at[a]) for a in range(n)]
        for cp in mine:
            cp.start()
        first = []
        for a in range(n):
            first.append(copy(a, 0, me, sibling, src=ins[a]))
            first += [copy(a, 1 + j, me, (*chip, c), src=ins[a]) for j, chip in enumerate(chips)]
        for cp in first:
            cp.start()
        passed = []
        for a in range(n):
            for j, chip in enumerate(chips):
                copy(a, 1 + j, (*chip, c), me).wait_recv()
                cp = copy(a, 4 + j, (*chip, c), sibling)
                cp.start()
                passed.append(cp)
        for a in range(n):
            copy(a, 0, sibling, me).wait_recv()
            for j, chip in enumerate(chips):
                copy(a, 4 + j, (*chip, 1 - c), me).wait_recv()
        for cp in first + passed:
            cp.wait_send()
        for cp in mine:
            cp.wait()

    return pl.pallas_call(
        body, name=name, in_specs=[ANY] * n, out_specs=[ANY] * n,
        out_shape=[jax.ShapeDtypeStruct((N_DEV,) + a.shape, a.dtype) for a in arrays],
        scratch_shapes=[pltpu.SemaphoreType.DMA((n, 7)), pltpu.SemaphoreType.DMA((n, 7)),
                        pltpu.SemaphoreType.DMA((n,))],
    )(*arrays)


def _exchange(arrays, name):
    n = len(arrays)
    blocked = [a.ndim == 3 for a in arrays]

    def body(*refs):
        ins, outs = refs[:n], refs[n:2 * n]
        send_sems, recv_sems, local_sems = refs[2 * n:]
        me = _index(_me())

        def src(k, dev):
            return ins[k].at[dev] if blocked[k] else ins[k]

        local = [pltpu.make_async_copy(src(k, me), outs[k].at[me], local_sems.at[k]) for k in range(n)]
        sends, arrivals = [], []
        for k in range(n):
            for mask in range(1, N_DEV):
                peer = _peer(mask)
                sends.append(pltpu.make_async_remote_copy(
                    src_ref=src(k, _index(peer)), dst_ref=outs[k].at[me], send_sem=send_sems.at[k, mask - 1],
                    recv_sem=recv_sems.at[k, mask - 1], device_id=peer, device_id_type=MESH))
                arrivals.append(pltpu.make_async_remote_copy(
                    src_ref=src(k, me), dst_ref=outs[k].at[_index(peer)], send_sem=send_sems.at[k, mask - 1],
                    recv_sem=recv_sems.at[k, mask - 1], device_id=_me(), device_id_type=MESH))
        for cp in local + sends:
            cp.start()
        for cp in arrivals:
            cp.wait_recv()
        for cp in sends:
            cp.wait_send()
        for cp in local:
            cp.wait()

    return pl.pallas_call(
        body, name=name, in_specs=[ANY] * n, out_specs=[ANY] * n,
        out_shape=[jax.ShapeDtypeStruct((N_DEV,) + a.shape[-2:], a.dtype) for a in arrays],
        scratch_shapes=[pltpu.SemaphoreType.DMA((n, 7)), pltpu.SemaphoreType.DMA((n, 7)),
                        pltpu.SemaphoreType.DMA((n,))],
    )(*arrays)


def _adamw_update(g, w_ref, m_ref, v_ref, go_ref, d_ref, mo_ref, vo_ref):
    mn = ADAM_B1 * m_ref[...] + (1.0 - ADAM_B1) * g
    vn = ADAM_B2 * v_ref[...] + (1.0 - ADAM_B2) * (g * g)
    m_hat = mn / (1.0 - ADAM_B1 ** ADAM_STEP)
    v_hat = vn / (1.0 - ADAM_B2 ** ADAM_STEP)
    go_ref[...] = g
    d_ref[...] = -ADAM_LR * (m_hat / (jnp.sqrt(v_hat) + ADAM_EPS) + ADAM_WD * w_ref[...])
    mo_ref[...] = mn
    vo_ref[...] = vn


def _sum_sources(g_refs, layer):
    total = None
    for l, g_ref in enumerate(g_refs):
        g = g_ref[0].astype(F32)
        for s in range(1, N_DEV):
            g = g + g_ref[s].astype(F32)
        total = g if total is None else jnp.where(layer == l, g, total)
    return total


def _layer_block(l_mine, nblocks):
    def index(l, j):
        return (0, jnp.where(l == l_mine, j, jnp.where(l < l_mine, 0, nblocks - 1)), 0)
    return index


def _adamw_sum(g8, w, m, v, name, tr):
    r, width = w.shape

    def body(g_ref, *refs):
        _adamw_update(_sum_sources([g_ref], 0), *refs)

    spec = pl.BlockSpec((tr, width), lambda i: (i, 0))
    return pl.pallas_call(
        body, name=name, grid=(r // tr,),
        in_specs=[pl.BlockSpec((N_DEV, tr, width), lambda i: (0, i, 0)), spec, spec, spec],
        out_specs=[spec] * 4, out_shape=[jax.ShapeDtypeStruct((r, width), F32)] * 4,
        compiler_params=_params(),
    )(g8, w, m, v)


def _adamw_rows(g8s, w, m, v, name):
    layers, n, width = w.shape
    tr = max(r for r in range(BF16_SUBLANES, ADAMW_MAX_ROWS + 1, BF16_SUBLANES) if n % r == 0)
    nb = n // tr

    def body(*refs):
        _adamw_update(_sum_sources(refs[:layers], pl.program_id(0)), *refs[layers:])

    spec = pl.BlockSpec((None, tr, width), lambda l, j: (l, j, 0))
    return pl.pallas_call(
        body, name=name, grid=(layers, nb),
        in_specs=[pl.BlockSpec((N_DEV, tr, width), _layer_block(l, nb)) for l in range(layers)] + [spec] * 3,
        out_specs=[spec] * 4, out_shape=[jax.ShapeDtypeStruct(w.shape, F32)] * 4,
        compiler_params=pltpu.CompilerParams(dimension_semantics=("arbitrary", "arbitrary"),
                                             vmem_limit_bytes=VMEM_LIMIT),
    )(*g8s, w, m, v)


def _adamw_cols(g8s, w, m, v, name):
    layers, k, n = w.shape
    cb = min(BLOCK, n)
    nb = pl.cdiv(n, cb)

    def body(*refs):
        _adamw_update(_sum_sources(refs[:layers], pl.program_id(0)).T, *refs[layers:])

    spec = pl.BlockSpec((None, k, cb), lambda l, j: (l, 0, j))
    return pl.pallas_call(
        body, name=name, grid=(layers, nb),
        in_specs=[pl.BlockSpec((N_DEV, cb, k), _layer_block(l, nb)) for l in range(layers)] + [spec] * 3,
        out_specs=[spec] * 4, out_shape=[jax.ShapeDtypeStruct(w.shape, F32)] * 4,
        compiler_params=pltpu.CompilerParams(dimension_semantics=("arbitrary", "arbitrary"),
                                             vmem_limit_bytes=VMEM_LIMIT),
    )(*g8s, w, m, v)


def _local_step(x, target, wba, shard_b, lay, sm, tm, tk):
    t, d = x.shape
    f = lay.f
    w_dw32 = jnp.concatenate([sm["w_dw"], jnp.zeros((HALO - CONV_WIDTH, d), F32)], axis=0)
    lmg, lmb, lfg, lfb = sm["ln_mix_g"], sm["ln_mix_b"], sm["ln_ffn_g"], sm["ln_ffn_b"]
    bkv = jnp.concatenate([sm["b_k"], sm["b_v"]], axis=1)
    bm_f = f // 2 if (f // 2) % 128 == 0 else f
    tm_light = 2 * tm

    received = {}

    def exchange(grads):
        return _HostedExchange([g.reshape(N_DEV, lay.n[n], d) for n, g in grads.items()])

    def keep(grads, arrived):
        received.update(zip(grads, arrived))

    xb0, ag, xhc, rsc, xh1, rs1, wbb = _conv_fwd(x, wba, lay, w_dw32, sm["b_pw1"], sm["b_dw"], sm["cg"],
                                                 sm["cb"], sm["b_pw2"], lmg[0:1], lmb[0:1], tm_light,
                                                 hosted=_HostedGather(shard_b))
    wkv = wbb[:, lay.goff["wkv"]:lay.goff["wkv"] + lay.n["wkv"], :].reshape(d, 2 * BLOCK)
    x1b, hg0, hu0, xh2, rs2, x2b, kvs = _ffn_fwd(xh1, lmg[0:1], lmb[0:1], wbb, lay, 0, tm_light,
                                                kv=(lfg[0:1], lfb[0:1], wkv, bkv))
    q, o, lse, xh3, rs3 = _attn_fwd(xh2, lfg[0:1], lfb[0:1], x2b, kvs, wbb, lay, sm["b_q"], sm["sinks"],
                                    sm["b_o"], tm_light)
    x3b, hg1, hu1, dz4, st4, loss = _ffn_fwd(xh3, lmg[1:2], lmb[1:2], wbb, lay, 1, tm,
                                             loss=(lfg[1:2], lfb[1:2], target))

    dz4b, act1, dhg1, dhu1, dz3, st3 = _ffn_bwd(dz4, hg1, hu1, xh3, rs3, lmg[1:2], wbb, lay, 1, tm)
    g1 = {"gt1": _tn_matmul(dhg1, x3b, "dw_gate1", bm_f, tk), "ut1": _tn_matmul(dhu1, x3b, "dw_up1", bm_f, tk),
          "dn1": _tn_matmul(act1, dz4b, "dw_down1", bm_f, tk)}
    dz3b, dq, dkc, dkp, stq, dsinks, *arrived = _attn_bwd(dz3, q, o, lse, kvs, wbb, lay, sm["sinks"], tm_light,
                                                          hosted=exchange(g1))
    keep(g1, arrived)
    g2 = {"wq": _tn_matmul(x2b, dq, "dw_q", d, tk), "wo": _tn_matmul(o, dz3b, "dw_o", d, tk)}
    dz2b, act0, dhg0, dhu0, dz1, st1, dkv, st2, dbkv, *arrived = _ffn_bwd(
        dz3, hg0, hu0, xh1, rs1, lmg[0:1], wbb, lay, 0, tm, hosted=exchange(g2),
        qkv=(dq, dkc, dkp, xh2, rs2, lfg[0:1], wkv))
    keep(g2, arrived)
    dz1b, s_act, dcv, stc = _conv_bwd1(dz1, xhc, rsc, wba, lay, sm["cg"], sm["cb"], tm_light)
    g3 = {"gt0": _tn_matmul(dhg0, x1b, "dw_gate0", bm_f, tk), "ut0": _tn_matmul(dhu0, x1b, "dw_up0", bm_f, tk),
          "dn0": _tn_matmul(act0, dz2b, "dw_down0", bm_f, tk), "pw2": _tn_matmul(s_act, dz1b, "dw_pw2", d, tk),
          "wkv": _tn_matmul(x2b, dkv, "dw_kv", d, tk)}
    grad_x, dh1, dwdw, db1, *arrived = _conv_bwd2(dz1, dcv, ag, wba, lay, w_dw32, tm_light, hosted=exchange(g3))
    keep(g3, arrived)
    g_pw1t = _tn_matmul(dh1, xb0, "dw_pw1", d, tk)
    small = {
        "w_dw": dwdw[:CONV_WIDTH], "b_pw1": db1, "b_dw": stc[2:3], "cg": stc[0:1], "cb": stc[1:2],
        "b_pw2": stc[3:4], "b_k": dbkv[:, :BLOCK], "b_v": dbkv[:, BLOCK:], "b_q": stq[0:1],
        "sinks": dsinks[:, :d // HEAD_DIM],
        "b_o": stq[1:2],
        "ln_mix_g": jnp.concatenate([st1[0:1], st3[0:1]], axis=0),
        "ln_mix_b": jnp.concatenate([st1[1:2], st3[1:2]], axis=0),
        "ln_ffn_g": jnp.concatenate([st2[0:1], st4[0:1]], axis=0),
        "ln_ffn_b": jnp.concatenate([st2[1:2], st4[1:2]], axis=0),
    }
    return loss[0, 0], grad_x, received, g_pw1t, small


SP_ROWS = 40
SP_BDW, SP_CG, SP_CB, SP_BPW2, SP_BPW1 = 32, 33, 34, 35, 36
RP_NAMES = ("ln_mix_g", "ln_mix_b", "ln_ffn_g", "ln_ffn_b", "b_q", "b_o", "b_k", "b_v", "sinks")


def _row_forms(d, pw1, pw2, wq, wo, gate, up, down, wk, wv):
    rf = {"pw1t": pw1[0].T, "pw2": pw2[0], "wq": wq[0], "wo": wo[0],
          "wkv": jnp.concatenate([wk, wv], axis=1).reshape(-1, d)}
    for l in range(DEPTH):
        rf.update({f"gt{l}": gate[l].T, f"ut{l}": up[l].T, f"dn{l}": down[l]})
    return rf


def _pack_rows(rf, names):
    return jnp.concatenate([rf[n] for n in names], axis=0)


def _pack_small(w_dw, b_dw, cg, cb, b_pw2, b_pw1):
    cw = b_dw.shape[1]
    z = jnp.zeros((1, cw), F32)
    return jnp.concatenate([w_dw[0], z, b_dw, cg, cb, b_pw2, b_pw1.reshape(2, cw), z, z], axis=0)


def _unpack_small(p):
    cw = p.shape[1]
    return dict(w_dw=p[None, :CONV_WIDTH], b_dw=p[SP_BDW:SP_BDW + 1], cg=p[SP_CG:SP_CG + 1],
                cb=p[SP_CB:SP_CB + 1], b_pw2=p[SP_BPW2:SP_BPW2 + 1],
                b_pw1=p[SP_BPW1:SP_BPW1 + 2].reshape(1, 2 * cw))


def _small_full(g):
    d = N_DEV * g.shape[2]

    def wide(r0, n=1):
        return jnp.transpose(g[:, r0:r0 + n], (1, 0, 2)).reshape(n, d)

    return dict(w_dw=wide(0, CONV_WIDTH), b_dw=wide(SP_BDW), cg=wide(SP_CG), cb=wide(SP_CB),
                b_pw2=wide(SP_BPW2), b_pw1=g[:, SP_BPW1:SP_BPW1 + 2].reshape(1, 2 * d))


def _small_grad_blocks(sg):
    cw = sg["b_dw"].shape[1] // N_DEV

    def narrow(a):
        return jnp.transpose(a.reshape(a.shape[0], N_DEV, cw), (1, 0, 2))

    z = jnp.zeros((N_DEV, 1, cw), F32)
    return jnp.concatenate([narrow(sg["w_dw"]), z, narrow(sg["b_dw"]), narrow(sg["cg"]), narrow(sg["cb"]),
                            narrow(sg["b_pw2"]), sg["b_pw1"].reshape(N_DEV, 2, cw), z, z], axis=1)


def _pack_rep(vals, rider=0.0):
    parts = []
    for name in RP_NAMES:
        a = vals[name].reshape(-1)
        pad = -a.shape[0] % 128
        parts.append(jnp.concatenate([a, jnp.zeros((pad,), F32)]).reshape(-1, 128))
    parts.append(jnp.full((1, 128), rider, F32))
    rows = sum(p.shape[0] for p in parts)
    parts.append(jnp.zeros((-rows % 8, 128), F32))
    return jnp.concatenate(parts, axis=0)


def _rider_row(shapes):
    return sum(-(-_size(shapes[name]) // 128) for name in RP_NAMES)


def _size(shape):
    n = 1
    for s in shape:
        n *= s
    return n


def _unpack_rep(p, shapes):
    out, r = {}, 0
    for name in RP_NAMES:
        n = _size(shapes[name])
        rows = -(-n // 128)
        out[name] = p[r:r + rows].reshape(-1)[:n].reshape(shapes[name])
        r += rows
    return out


def kernel(x, conv_w_pw1, conv_b_pw1, conv_w_dw, conv_b_dw, conv_ln_g, conv_ln_b, conv_w_pw2, conv_b_pw2, kv_w_k, kv_b_k, kv_w_v, kv_b_v, attn_w_q, attn_b_q, attn_sinks, attn_w_o, attn_b_o, ffn_w_gate, ffn_w_up, ffn_w_down, ln_mix_g, ln_mix_b, ln_ffn_g, ln_ffn_b, loss_target, m_conv_w_pw1, m_conv_b_pw1, m_conv_w_dw, m_conv_b_dw, m_conv_ln_g, m_conv_ln_b, m_conv_w_pw2, m_conv_b_pw2, m_kv_w_k, m_kv_b_k, m_kv_w_v, m_kv_b_v, m_attn_w_q, m_attn_b_q, m_attn_sinks, m_attn_w_o, m_attn_b_o, m_ffn_w_gate, m_ffn_w_up, m_ffn_w_down, m_ln_mix_g, m_ln_mix_b, m_ln_ffn_g, m_ln_ffn_b, v_conv_w_pw1, v_conv_b_pw1, v_conv_w_dw, v_conv_b_dw, v_conv_ln_g, v_conv_ln_b, v_conv_w_pw2, v_conv_b_pw2, v_kv_w_k, v_kv_b_k, v_kv_w_v, v_kv_b_v, v_attn_w_q, v_attn_b_q, v_attn_sinks, v_attn_w_o, v_attn_b_o, v_ffn_w_gate, v_ffn_w_up, v_ffn_w_down, v_ln_mix_g, v_ln_mix_b, v_ln_ffn_g, v_ln_ffn_b):
    t, d = x.shape[1], x.shape[2]
    f = ffn_w_gate.shape[2] * N_DEV
    lay = _Layout(d, f)
    tm, tk = 256, min(2048, t)

    rep_shapes = dict(ln_mix_g=ln_mix_g.shape, ln_mix_b=ln_mix_b.shape, ln_ffn_g=ln_ffn_g.shape,
                      ln_ffn_b=ln_ffn_b.shape, b_q=attn_b_q.shape, b_o=attn_b_o.shape, b_k=kv_b_k.shape,
                      b_v=kv_b_v.shape, sinks=attn_sinks.shape)

    def rep_pack(lmg, lmb, lfg, lfb, bq, bo, bk, bv, sk):
        return _pack_rep(dict(ln_mix_g=lmg, ln_mix_b=lmb, ln_ffn_g=lfg, ln_ffn_b=lfb, b_q=bq, b_o=bo,
                              b_k=bk, b_v=bv, sinks=sk))

    w_rf = _row_forms(d, conv_w_pw1, conv_w_pw2, attn_w_q, attn_w_o, ffn_w_gate, ffn_w_up, ffn_w_down, kv_w_k, kv_w_v)
    w_small = _pack_small(conv_w_dw, conv_b_dw, conv_ln_g, conv_ln_b, conv_b_pw2, conv_b_pw1)
    m_small = _pack_small(m_conv_w_dw, m_conv_b_dw, m_conv_ln_g, m_conv_ln_b, m_conv_b_pw2, m_conv_b_pw1)
    v_small = _pack_small(v_conv_w_dw, v_conv_b_dw, v_conv_ln_g, v_conv_ln_b, v_conv_b_pw2, v_conv_b_pw1)
    w_rep = rep_pack(ln_mix_g, ln_mix_b, ln_ffn_g, ln_ffn_b, attn_b_q, attn_b_o, kv_b_k, kv_b_v, attn_sinks)
    m_rep = rep_pack(m_ln_mix_g, m_ln_mix_b, m_ln_ffn_g, m_ln_ffn_b, m_attn_b_q, m_attn_b_o, m_kv_b_k, m_kv_b_v, m_attn_sinks)
    v_rep = rep_pack(v_ln_mix_g, v_ln_mix_b, v_ln_ffn_g, v_ln_ffn_b, v_attn_b_q, v_attn_b_o, v_kv_b_k, v_kv_b_v, v_attn_sinks)

    wba, smg = _all_gather([_pack_rows(w_rf, lay.GATHER["a"]).astype(BF16), w_small], "gather_conv_weights")
    shard_b = _pack_rows(w_rf, lay.GATHER["b"]).astype(BF16)
    sm = _small_full(smg)
    sm.update(ln_mix_g=ln_mix_g, ln_mix_b=ln_mix_b, ln_ffn_g=ln_ffn_g, ln_ffn_b=ln_ffn_b, b_q=attn_b_q,
              b_o=attn_b_o, sinks=attn_sinks, b_k=kv_b_k.reshape(1, -1), b_v=kv_b_v.reshape(1, -1))

    loss_part, grad_x, received, g_pw1t, gsmall = _local_step(x[0], loss_target[0], wba, shard_b, lay, sm, tm, tk)

    received["pw1t"], g8_small, g8_rep = _exchange(
        [g_pw1t.reshape(N_DEV, lay.n["pw1t"], d), _small_grad_blocks(gsmall), _pack_rep(gsmall, loss_part)],
        "exchange_last_grads")

    def kv_rows(wk, wv):
        return jnp.concatenate([wk, wv], axis=1).reshape(1, -1, d)

    def kv_split(a):
        a = a.reshape(d // N_DEV, 2 * BLOCK)
        return a[:, :BLOCK], a[:, BLOCK:]

    big = dict(
        pw1=_adamw_cols([received["pw1t"]], conv_w_pw1, m_conv_w_pw1, v_conv_w_pw1, "adamw_pw1"),
        gate=_adamw_cols([received["gt0"], received["gt1"]], ffn_w_gate, m_ffn_w_gate, v_ffn_w_gate, "adamw_gate"),
        up=_adamw_cols([received["ut0"], received["ut1"]], ffn_w_up, m_ffn_w_up, v_ffn_w_up, "adamw_up"),
        down=_adamw_rows([received["dn0"], received["dn1"]], ffn_w_down, m_ffn_w_down, v_ffn_w_down, "adamw_down"),
        pw2=_adamw_rows([received["pw2"]], conv_w_pw2, m_conv_w_pw2, v_conv_w_pw2, "adamw_pw2"),
        wq=_adamw_rows([received["wq"]], attn_w_q, m_attn_w_q, v_attn_w_q, "adamw_q"),
        wo=_adamw_rows([received["wo"]], attn_w_o, m_attn_w_o, v_attn_w_o, "adamw_o"),
        wkv=[kv_split(a) for a in _adamw_rows([received["wkv"]], kv_rows(kv_w_k, kv_w_v), kv_rows(m_kv_w_k, m_kv_w_v),
                                              kv_rows(v_kv_w_k, v_kv_w_v), "adamw_kv")])
    big_out = [dict(pw1=big["pw1"][i], pw2=big["pw2"][i], wq=big["wq"][i], wo=big["wo"][i], gate=big["gate"][i],
                    up=big["up"][i], down=big["down"][i], wk=big["wkv"][i][0], wv=big["wkv"][i][1])
               for i in range(4)]
    small_out = [_unpack_small(a) for a in _adamw_sum(g8_small, w_small, m_small, v_small, "adamw_small", SP_ROWS)]
    rep_res = _adamw_sum(g8_rep, w_rep, m_rep, v_rep, "adamw_rep", w_rep.shape[0])
    rep_out = [_unpack_rep(a, rep_shapes) for a in rep_res]
    loss = rep_res[0][_rider_row(rep_shapes), 0]

    outs = [loss, grad_x[None]]
    for b, s, r in zip(big_out, small_out, rep_out):
        outs += [b["pw1"], s["b_pw1"], s["w_dw"], s["b_dw"], s["cg"], s["cb"], b["pw2"], s["b_pw2"],
                 b["wk"], r["b_k"], b["wv"], r["b_v"], b["wq"], r["b_q"], r["sinks"], b["wo"], r["b_o"],
                 b["gate"], b["up"], b["down"], r["ln_mix_g"], r["ln_mix_b"], r["ln_ffn_g"], r["ln_ffn_b"]]
    return tuple(outs)
```
